```python
import jax, jax.numpy as jnp
from jax import lax
import numpy as np

D_MODEL = 1024
BATCH = 8
SEQ = 4096
DEPTH = 4

CHUNK = 64
Q_BLOCK = 128
PLE_DIM = 256
D_FF = 2816
CONV_DIM = 512
CONV_GROUPS = 8
CONV_K = 3
N_HEADS = 8
NOPE_DIM = 128
ROPE_DIM = 64
V_DIM = 128
Q_LORA = 384
KV_LORA = 256
ROPE_THETA = 10000.0
EPS = 1e-6
QK_DIM = NOPE_DIM + ROPE_DIM
ATTN_SCALE = QK_DIM ** -0.5
IN_SPLITS = (CONV_DIM, CONV_DIM, CONV_DIM, Q_LORA, KV_LORA, ROPE_DIM, D_MODEL, D_MODEL)
IN_COLS = sum(IN_SPLITS)

kernel_name = "hybrid_conv_mla_macaron_ple_trunk"


def rmsnorm(x, g):
    xf = x.astype(jnp.float32)
    y = xf * lax.rsqrt(jnp.mean(xf * xf, axis=-1, keepdims=True) + EPS)
    return (y * g.astype(jnp.float32)).astype(x.dtype)


def swiglu(x, w_gu, w_down):
    g, u = jnp.split(x @ w_gu, 2, axis=-1)
    return (jax.nn.silu(g) * u) @ w_down


def rope_tables(positions):
    inv_freq = ROPE_THETA ** (-jnp.arange(0, ROPE_DIM, 2, dtype=jnp.float32) / ROPE_DIM)
    ang = positions.astype(jnp.float32)[..., None] * inv_freq
    return jnp.cos(ang), jnp.sin(ang)


def apply_rope(x, cos, sin):
    half = ROPE_DIM // 2
    xf = x.astype(jnp.float32)
    x1, x2 = xf[..., :half], xf[..., half:]
    return jnp.concatenate([x1 * cos - x2 * sin, x2 * cos + x1 * sin], axis=-1).astype(x.dtype)


def short_conv_branch(b_gate, c_gate, v, conv_w, w_conv_out):
    seq = v.shape[1]
    z = c_gate * v
    zp = jnp.pad(z, ((0, 0), (CONV_K - 1, 0), (0, 0)))
    y = conv_w[0] * zp[:, 0:seq]
    for j in range(1, CONV_K):
        y = y + conv_w[j] * zp[:, j:j + seq]
    return (b_gate * y) @ w_conv_out


def block_causal_attention(q_nope, q_rope, k_nope, k_rope, v):
    seq = q_nope.shape[1]
    outs = []
    for i in range(seq // Q_BLOCK):
        q0, q1 = i * Q_BLOCK, (i + 1) * Q_BLOCK
        kn, kr, vb = k_nope[:, :q1], k_rope[:, :q1], v[:, :q1]
        s = (jnp.einsum('bqhd,bkhd->bhqk', q_nope[:, q0:q1], kn)
             + jnp.einsum('bqhd,bkd->bhqk', q_rope[:, q0:q1], kr)).astype(jnp.float32) * ATTN_SCALE
        q_chunk = (q0 + jnp.arange(Q_BLOCK)) // CHUNK
        k_chunk = jnp.arange(q1) // CHUNK
        mask = k_chunk[None, :] <= q_chunk[:, None]
        s = jnp.where(mask, s, -1e30)
        pr = jax.nn.softmax(s, axis=-1).astype(v.dtype)
        outs.append(jnp.einsum('bhqk,bkhd->bqhd', pr, vb))
    return jnp.concatenate(outs, axis=1)


def mla_branch(q_c, kv_c, k_r, q_norm_g, kv_norm_g, w_uq, w_ukv, w_mla_out, cos, sin):
    b, s, _ = q_c.shape
    q = (rmsnorm(q_c, q_norm_g) @ w_uq).reshape(b, s, N_HEADS, QK_DIM)
    q_nope = q[..., :NOPE_DIM]
    q_rope = apply_rope(q[..., NOPE_DIM:], cos[:, :, None, :], sin[:, :, None, :])
    kv = (rmsnorm(kv_c, kv_norm_g) @ w_ukv).reshape(b, s, N_HEADS, NOPE_DIM + V_DIM)
    k_nope, v = kv[..., :NOPE_DIM], kv[..., NOPE_DIM:]
    k_rope = apply_rope(k_r, cos, sin)
    o = block_causal_attention(q_nope, q_rope, k_nope, k_rope, v)
    return o.reshape(b, s, N_HEADS * V_DIM) @ w_mla_out


def _fwd_setup_inputs(seed: int = 0) -> dict:
    key = jax.random.key(seed)
    ks = jax.random.split(key, 24)
    f32 = jnp.float32

    def w(k, shape, fan_in):
        return jax.random.normal(k, shape, f32) * (fan_in ** -0.5)

    def gain(k, shape):
        return 1.0 + 0.05 * jax.random.normal(k, shape, f32)

    x = jax.random.normal(ks[0], (BATCH, SEQ, D_MODEL), f32)
    p = jax.random.normal(ks[1], (DEPTH, BATCH, SEQ, PLE_DIM), f32)
    offset = jax.random.randint(ks[2], (BATCH, 1), 0, 4096, dtype=jnp.int32)
    positions = offset + jnp.arange(SEQ, dtype=jnp.int32)[None, :]
    return {
        "x": x,
        "p": p,
        "positions": positions,
        "ffn1_norm": gain(ks[3], (DEPTH, D_MODEL)),
        "ffn1_w_gu": w(ks[4], (DEPTH, D_MODEL, 2 * D_FF), D_MODEL),
        "ffn1_w_down": w(ks[5], (DEPTH, D_FF, D_MODEL), D_FF),
        "mix_norm": gain(ks[6], (DEPTH, D_MODEL)),
        "w_in": w(ks[7], (DEPTH, D_MODEL, IN_COLS), D_MODEL),
        "conv_w": w(ks[8], (DEPTH, CONV_K, CONV_DIM), CONV_K),
        "w_conv_out": w(ks[9], (DEPTH, CONV_DIM, D_MODEL), CONV_DIM),
        "q_norm": gain(ks[10], (DEPTH, Q_LORA)),
        "kv_norm": gain(ks[11], (DEPTH, KV_LORA)),
        "w_uq": w(ks[12], (DEPTH, Q_LORA, N_HEADS * QK_DIM), Q_LORA),
        "w_ukv": w(ks[13], (DEPTH, KV_LORA, N_HEADS * (NOPE_DIM + V_DIM)), KV_LORA),
        "w_mla_out": w(ks[14], (DEPTH, N_HEADS * V_DIM, D_MODEL), N_HEADS * V_DIM),
        "w_o": w(ks[15], (DEPTH, D_MODEL, D_MODEL), D_MODEL),
        "ffn2_norm": gain(ks[16], (DEPTH, D_MODEL)),
        "ffn2_w_gu": w(ks[17], (DEPTH, D_MODEL, 2 * D_FF), D_MODEL),
        "ffn2_w_down": w(ks[18], (DEPTH, D_FF, D_MODEL), D_FF),
        "ple_norm": gain(ks[19], (DEPTH, D_MODEL)),
        "w_ple_gate": w(ks[20], (DEPTH, D_MODEL, D_MODEL), D_MODEL),
        "w_ple_proj": w(ks[21], (DEPTH, PLE_DIM, D_MODEL), PLE_DIM),
        "final_norm": gain(ks[22], (D_MODEL,)),
    }


def _fwd_reference(x, p, positions, ffn1_norm, ffn1_w_gu, ffn1_w_down, mix_norm, w_in, conv_w,
              w_conv_out, q_norm, kv_norm, w_uq, w_ukv, w_mla_out, w_o, ffn2_norm, ffn2_w_gu,
              ffn2_w_down, ple_norm, w_ple_gate, w_ple_proj, final_norm):
    cos, sin = rope_tables(positions)
    split_pts = list(np.cumsum(IN_SPLITS)[:-1])
    h = x
    for i in range(DEPTH):
        h = h + 0.5 * swiglu(rmsnorm(h, ffn1_norm[i]), ffn1_w_gu[i], ffn1_w_down[i])

        u = rmsnorm(h, mix_norm[i])
        b_g, c_g, v_c, q_c, kv_c, k_r, g_conv, g_mla = jnp.split(u @ w_in[i], split_pts, axis=-1)
        y_conv = short_conv_branch(b_g, c_g, v_c, conv_w[i], w_conv_out[i])
        y_mla = mla_branch(q_c, kv_c, k_r, q_norm[i], kv_norm[i], w_uq[i], w_ukv[i],
                           w_mla_out[i], cos, sin)
        merged = jax.nn.sigmoid(g_conv) * y_conv + jax.nn.sigmoid(g_mla) * y_mla
        h = h + merged @ w_o[i]

        h = h + 0.5 * swiglu(rmsnorm(h, ffn2_norm[i]), ffn2_w_gu[i], ffn2_w_down[i])

        gate = jax.nn.sigmoid(rmsnorm(h, ple_norm[i]) @ w_ple_gate[i])
        h = h + gate * (p[i] @ w_ple_proj[i])
    return rmsnorm(h, final_norm)


import jax as _jax
import jax.numpy as _jnp

TWIN_FORMAT = 'train_step'
FWD_PARAMS = ['x', 'p', 'positions', 'ffn1_norm', 'ffn1_w_gu', 'ffn1_w_down', 'mix_norm', 'w_in', 'conv_w', 'w_conv_out', 'q_norm', 'kv_norm', 'w_uq', 'w_ukv', 'w_mla_out', 'w_o', 'ffn2_norm', 'ffn2_w_gu', 'ffn2_w_down', 'ple_norm', 'w_ple_gate', 'w_ple_proj', 'final_norm']
TWIN_WEIGHTS = ['ffn1_norm', 'ffn1_w_gu', 'ffn1_w_down', 'mix_norm', 'w_in', 'conv_w', 'w_conv_out', 'q_norm', 'kv_norm', 'w_uq', 'w_ukv', 'w_mla_out', 'w_o', 'ffn2_norm', 'ffn2_w_gu', 'ffn2_w_down', 'ple_norm', 'w_ple_gate', 'w_ple_proj', 'final_norm']
TWIN_DIFF_INPUT = 'x'
TWIN_INPUTS = ['x', 'p', 'positions', 'ffn1_norm', 'ffn1_w_gu', 'ffn1_w_down', 'mix_norm', 'w_in', 'conv_w', 'w_conv_out', 'q_norm', 'kv_norm', 'w_uq', 'w_ukv', 'w_mla_out', 'w_o', 'ffn2_norm', 'ffn2_w_gu', 'ffn2_w_down', 'ple_norm', 'w_ple_gate', 'w_ple_proj', 'final_norm', 'loss_target', 'm_ffn1_norm', 'm_ffn1_w_gu', 'm_ffn1_w_down', 'm_mix_norm', 'm_w_in', 'm_conv_w', 'm_w_conv_out', 'm_q_norm', 'm_kv_norm', 'm_w_uq', 'm_w_ukv', 'm_w_mla_out', 'm_w_o', 'm_ffn2_norm', 'm_ffn2_w_gu', 'm_ffn2_w_down', 'm_ple_norm', 'm_w_ple_gate', 'm_w_ple_proj', 'm_final_norm', 'v_ffn1_norm', 'v_ffn1_w_gu', 'v_ffn1_w_down', 'v_mix_norm', 'v_w_in', 'v_conv_w', 'v_w_conv_out', 'v_q_norm', 'v_kv_norm', 'v_w_uq', 'v_w_ukv', 'v_w_mla_out', 'v_w_o', 'v_ffn2_norm', 'v_ffn2_w_gu', 'v_ffn2_w_down', 'v_ple_norm', 'v_w_ple_gate', 'v_w_ple_proj', 'v_final_norm']
TWIN_OUTPUTS = ['loss', 'grad_x', 'grad_ffn1_norm', 'grad_ffn1_w_gu', 'grad_ffn1_w_down', 'grad_mix_norm', 'grad_w_in', 'grad_conv_w', 'grad_w_conv_out', 'grad_q_norm', 'grad_kv_norm', 'grad_w_uq', 'grad_w_ukv', 'grad_w_mla_out', 'grad_w_o', 'grad_ffn2_norm', 'grad_ffn2_w_gu', 'grad_ffn2_w_down', 'grad_ple_norm', 'grad_w_ple_gate', 'grad_w_ple_proj', 'grad_final_norm', 'delta_ffn1_norm', 'delta_ffn1_w_gu', 'delta_ffn1_w_down', 'delta_mix_norm', 'delta_w_in', 'delta_conv_w', 'delta_w_conv_out', 'delta_q_norm', 'delta_kv_norm', 'delta_w_uq', 'delta_w_ukv', 'delta_w_mla_out', 'delta_w_o', 'delta_ffn2_norm', 'delta_ffn2_w_gu', 'delta_ffn2_w_down', 'delta_ple_norm', 'delta_w_ple_gate', 'delta_w_ple_proj', 'delta_final_norm', 'new_m_ffn1_norm', 'new_m_ffn1_w_gu', 'new_m_ffn1_w_down', 'new_m_mix_norm', 'new_m_w_in', 'new_m_conv_w', 'new_m_w_conv_out', 'new_m_q_norm', 'new_m_kv_norm', 'new_m_w_uq', 'new_m_w_ukv', 'new_m_w_mla_out', 'new_m_w_o', 'new_m_ffn2_norm', 'new_m_ffn2_w_gu', 'new_m_ffn2_w_down', 'new_m_ple_norm', 'new_m_w_ple_gate', 'new_m_w_ple_proj', 'new_m_final_norm', 'new_v_ffn1_norm', 'new_v_ffn1_w_gu', 'new_v_ffn1_w_down', 'new_v_mix_norm', 'new_v_w_in', 'new_v_conv_w', 'new_v_w_conv_out', 'new_v_q_norm', 'new_v_kv_norm', 'new_v_w_uq', 'new_v_w_ukv', 'new_v_w_mla_out', 'new_v_w_o', 'new_v_ffn2_norm', 'new_v_ffn2_w_gu', 'new_v_ffn2_w_down', 'new_v_ple_norm', 'new_v_w_ple_gate', 'new_v_w_ple_proj', 'new_v_final_norm']
TWIN_LEAF_KINDS = {'loss': 'loss', 'grad_x': 'grad_x', 'grad_ffn1_norm': 'grad_w', 'grad_ffn1_w_gu': 'grad_w', 'grad_ffn1_w_down': 'grad_w', 'grad_mix_norm': 'grad_w', 'grad_w_in': 'grad_w', 'grad_conv_w': 'grad_w', 'grad_w_conv_out': 'grad_w', 'grad_q_norm': 'grad_w', 'grad_kv_norm': 'grad_w', 'grad_w_uq': 'grad_w', 'grad_w_ukv': 'grad_w', 'grad_w_mla_out': 'grad_w', 'grad_w_o': 'grad_w', 'grad_ffn2_norm': 'grad_w', 'grad_ffn2_w_gu': 'grad_w', 'grad_ffn2_w_down': 'grad_w', 'grad_ple_norm': 'grad_w', 'grad_w_ple_gate': 'grad_w', 'grad_w_ple_proj': 'grad_w', 'grad_final_norm': 'grad_w', 'delta_ffn1_norm': 'delta_w', 'delta_ffn1_w_gu': 'delta_w', 'delta_ffn1_w_down': 'delta_w', 'delta_mix_norm': 'delta_w', 'delta_w_in': 'delta_w', 'delta_conv_w': 'delta_w', 'delta_w_conv_out': 'delta_w', 'delta_q_norm': 'delta_w', 'delta_kv_norm': 'delta_w', 'delta_w_uq': 'delta_w', 'delta_w_ukv': 'delta_w', 'delta_w_mla_out': 'delta_w', 'delta_w_o': 'delta_w', 'delta_ffn2_norm': 'delta_w', 'delta_ffn2_w_gu': 'delta_w', 'delta_ffn2_w_down': 'delta_w', 'delta_ple_norm': 'delta_w', 'delta_w_ple_gate': 'delta_w', 'delta_w_ple_proj': 'delta_w', 'delta_final_norm': 'delta_w', 'new_m_ffn1_norm': 'new_m', 'new_m_ffn1_w_gu': 'new_m', 'new_m_ffn1_w_down': 'new_m', 'new_m_mix_norm': 'new_m', 'new_m_w_in': 'new_m', 'new_m_conv_w': 'new_m', 'new_m_w_conv_out': 'new_m', 'new_m_q_norm': 'new_m', 'new_m_kv_norm': 'new_m', 'new_m_w_uq': 'new_m', 'new_m_w_ukv': 'new_m', 'new_m_w_mla_out': 'new_m', 'new_m_w_o': 'new_m', 'new_m_ffn2_norm': 'new_m', 'new_m_ffn2_w_gu': 'new_m', 'new_m_ffn2_w_down': 'new_m', 'new_m_ple_norm': 'new_m', 'new_m_w_ple_gate': 'new_m', 'new_m_w_ple_proj': 'new_m', 'new_m_final_norm': 'new_m', 'new_v_ffn1_norm': 'new_v', 'new_v_ffn1_w_gu': 'new_v', 'new_v_ffn1_w_down': 'new_v', 'new_v_mix_norm': 'new_v', 'new_v_w_in': 'new_v', 'new_v_conv_w': 'new_v', 'new_v_w_conv_out': 'new_v', 'new_v_q_norm': 'new_v', 'new_v_kv_norm': 'new_v', 'new_v_w_uq': 'new_v', 'new_v_w_ukv': 'new_v', 'new_v_w_mla_out': 'new_v', 'new_v_w_o': 'new_v', 'new_v_ffn2_norm': 'new_v', 'new_v_ffn2_w_gu': 'new_v', 'new_v_ffn2_w_down': 'new_v', 'new_v_ple_norm': 'new_v', 'new_v_w_ple_gate': 'new_v', 'new_v_w_ple_proj': 'new_v', 'new_v_final_norm': 'new_v'}


def _forward(args):
    return _fwd_reference(*[args[k] for k in FWD_PARAMS])


def _output_shape():
    out = _jax.eval_shape(lambda: _forward(_fwd_setup_inputs(0)))
    return out.shape, out.dtype

N_MICROBATCH = 1
ADAM_LR = 0.001
ADAM_B1 = 0.9
ADAM_B2 = 0.999
ADAM_EPS = 1e-08
ADAM_WD = 0.01
ADAM_STEP = 10
PER_EXAMPLE_BATCH_AXIS = {'x': 0, 'p': 1, 'positions': 0, 'loss_target': 0}
SHARED_INPUTS = []
_WEIGHT_DTYPES = {'ffn1_norm': _jnp.float32, 'ffn1_w_gu': _jnp.float32, 'ffn1_w_down': _jnp.float32, 'mix_norm': _jnp.float32, 'w_in': _jnp.float32, 'conv_w': _jnp.float32, 'w_conv_out': _jnp.float32, 'q_norm': _jnp.float32, 'kv_norm': _jnp.float32, 'w_uq': _jnp.float32, 'w_ukv': _jnp.float32, 'w_mla_out': _jnp.float32, 'w_o': _jnp.float32, 'ffn2_norm': _jnp.float32, 'ffn2_w_gu': _jnp.float32, 'ffn2_w_down': _jnp.float32, 'ple_norm': _jnp.float32, 'w_ple_gate': _jnp.float32, 'w_ple_proj': _jnp.float32, 'final_norm': _jnp.float32}
MOMENT_SCALE = {'ffn1_norm': 7.608709e-02, 'ffn1_w_gu': 3.245791e-02, 'ffn1_w_down': 5.297854e-02, 'mix_norm': 1.364348e-01, 'w_in': 6.559536e-02, 'conv_w': 1.100283e-01, 'w_conv_out': 7.459760e-02, 'q_norm': 1.788479e-02, 'kv_norm': 3.480480e-02, 'w_uq': 9.184868e-03, 'w_ukv': 1.103111e-02, 'w_mla_out': 1.241447e-02, 'w_o': 7.545500e-02, 'ffn2_norm': 5.775428e-02, 'ffn2_w_gu': 2.438159e-02, 'ffn2_w_down': 3.981393e-02, 'ple_norm': 2.838181e-02, 'w_ple_gate': 2.781470e-02, 'w_ple_proj': 7.127286e-02, 'final_norm': 3.199368e+01}


def _to_microbatches(a, axis):
    t = _jnp.moveaxis(a, axis, 0)
    t = t.reshape((N_MICROBATCH, t.shape[0] // N_MICROBATCH) + t.shape[1:])
    return _jnp.moveaxis(t, 1, axis + 1)


def setup_inputs(seed: int = 0) -> dict:
    inp = _fwd_setup_inputs(seed)
    key = _jax.random.fold_in(_jax.random.key(seed), 7919)
    shape, _ = _output_shape()
    out = dict(inp)
    out["loss_target"] = _jax.random.normal(_jax.random.fold_in(key, 0), shape, _jnp.float32)
    for i, name in enumerate(TWIN_WEIGHTS):
        w = inp[name].astype(_jnp.float32)
        if MOMENT_SCALE is None:
            s = _jnp.sqrt(_jnp.mean(_jnp.square(w)) + 1e-30)
        else:
            s = MOMENT_SCALE[name]
        km, kv = _jax.random.split(_jax.random.fold_in(key, i + 1))
        out[name] = w
        out["m_" + name] = s * _jax.random.normal(km, w.shape, _jnp.float32)
        out["v_" + name] = (s * s) * _jax.random.uniform(kv, w.shape, _jnp.float32, 0.5, 1.5)
    if N_MICROBATCH > 1:
        for name, axis in PER_EXAMPLE_BATCH_AXIS.items():
            out[name] = _to_microbatches(out[name], axis)
    return {'x': out['x'], 'p': out['p'], 'positions': out['positions'], 'ffn1_norm': out['ffn1_norm'], 'ffn1_w_gu': out['ffn1_w_gu'], 'ffn1_w_down': out['ffn1_w_down'], 'mix_norm': out['mix_norm'], 'w_in': out['w_in'], 'conv_w': out['conv_w'], 'w_conv_out': out['w_conv_out'], 'q_norm': out['q_norm'], 'kv_norm': out['kv_norm'], 'w_uq': out['w_uq'], 'w_ukv': out['w_ukv'], 'w_mla_out': out['w_mla_out'], 'w_o': out['w_o'], 'ffn2_norm': out['ffn2_norm'], 'ffn2_w_gu': out['ffn2_w_gu'], 'ffn2_w_down': out['ffn2_w_down'], 'ple_norm': out['ple_norm'], 'w_ple_gate': out['w_ple_gate'], 'w_ple_proj': out['w_ple_proj'], 'final_norm': out['final_norm'], 'loss_target': out['loss_target'], 'm_ffn1_norm': out['m_ffn1_norm'], 'm_ffn1_w_gu': out['m_ffn1_w_gu'], 'm_ffn1_w_down': out['m_ffn1_w_down'], 'm_mix_norm': out['m_mix_norm'], 'm_w_in': out['m_w_in'], 'm_conv_w': out['m_conv_w'], 'm_w_conv_out': out['m_w_conv_out'], 'm_q_norm': out['m_q_norm'], 'm_kv_norm': out['m_kv_norm'], 'm_w_uq': out['m_w_uq'], 'm_w_ukv': out['m_w_ukv'], 'm_w_mla_out': out['m_w_mla_out'], 'm_w_o': out['m_w_o'], 'm_ffn2_norm': out['m_ffn2_norm'], 'm_ffn2_w_gu': out['m_ffn2_w_gu'], 'm_ffn2_w_down': out['m_ffn2_w_down'], 'm_ple_norm': out['m_ple_norm'], 'm_w_ple_gate': out['m_w_ple_gate'], 'm_w_ple_proj': out['m_w_ple_proj'], 'm_final_norm': out['m_final_norm'], 'v_ffn1_norm': out['v_ffn1_norm'], 'v_ffn1_w_gu': out['v_ffn1_w_gu'], 'v_ffn1_w_down': out['v_ffn1_w_down'], 'v_mix_norm': out['v_mix_norm'], 'v_w_in': out['v_w_in'], 'v_conv_w': out['v_conv_w'], 'v_w_conv_out': out['v_w_conv_out'], 'v_q_norm': out['v_q_norm'], 'v_kv_norm': out['v_kv_norm'], 'v_w_uq': out['v_w_uq'], 'v_w_ukv': out['v_w_ukv'], 'v_w_mla_out': out['v_w_mla_out'], 'v_w_o': out['v_w_o'], 'v_ffn2_norm': out['v_ffn2_norm'], 'v_ffn2_w_gu': out['v_ffn2_w_gu'], 'v_ffn2_w_down': out['v_ffn2_w_down'], 'v_ple_norm': out['v_ple_norm'], 'v_w_ple_gate': out['v_w_ple_gate'], 'v_w_ple_proj': out['v_w_ple_proj'], 'v_final_norm': out['v_final_norm']}


def _loss(weights, diff, rest, loss_target):
    with _jax.named_scope("forward"):
        args = {**rest, TWIN_DIFF_INPUT: diff, **{k: w.astype(_WEIGHT_DTYPES[k]) for k, w in weights.items()}}
        y = _forward(args)
    with _jax.named_scope("loss_head"):
        err = _jnp.square(y.astype(_jnp.float32) - loss_target)
        return 0.5 * _jnp.sum(_jnp.mean(err, axis=-1)) if err.ndim else 0.5 * err


def _adamw(w, g, m, v):
    m = ADAM_B1 * m + (1.0 - ADAM_B1) * g
    v = ADAM_B2 * v + (1.0 - ADAM_B2) * _jnp.square(g)
    m_hat = m / (1.0 - ADAM_B1 ** ADAM_STEP)
    v_hat = v / (1.0 - ADAM_B2 ** ADAM_STEP)
    delta = -ADAM_LR * (m_hat / (_jnp.sqrt(v_hat) + ADAM_EPS) + ADAM_WD * w)
    return delta, m, v


def reference(x, p, positions, ffn1_norm, ffn1_w_gu, ffn1_w_down, mix_norm, w_in, conv_w, w_conv_out, q_norm, kv_norm, w_uq, w_ukv, w_mla_out, w_o, ffn2_norm, ffn2_w_gu, ffn2_w_down, ple_norm, w_ple_gate, w_ple_proj, final_norm, loss_target, m_ffn1_norm, m_ffn1_w_gu, m_ffn1_w_down, m_mix_norm, m_w_in, m_conv_w, m_w_conv_out, m_q_norm, m_kv_norm, m_w_uq, m_w_ukv, m_w_mla_out, m_w_o, m_ffn2_norm, m_ffn2_w_gu, m_ffn2_w_down, m_ple_norm, m_w_ple_gate, m_w_ple_proj, m_final_norm, v_ffn1_norm, v_ffn1_w_gu, v_ffn1_w_down, v_mix_norm, v_w_in, v_conv_w, v_w_conv_out, v_q_norm, v_kv_norm, v_w_uq, v_w_ukv, v_w_mla_out, v_w_o, v_ffn2_norm, v_ffn2_w_gu, v_ffn2_w_down, v_ple_norm, v_w_ple_gate, v_w_ple_proj, v_final_norm):
    given = dict(x=x, p=p, positions=positions, ffn1_norm=ffn1_norm, ffn1_w_gu=ffn1_w_gu, ffn1_w_down=ffn1_w_down, mix_norm=mix_norm, w_in=w_in, conv_w=conv_w, w_conv_out=w_conv_out, q_norm=q_norm, kv_norm=kv_norm, w_uq=w_uq, w_ukv=w_ukv, w_mla_out=w_mla_out, w_o=w_o, ffn2_norm=ffn2_norm, ffn2_w_gu=ffn2_w_gu, ffn2_w_down=ffn2_w_down, ple_norm=ple_norm, w_ple_gate=w_ple_gate, w_ple_proj=w_ple_proj, final_norm=final_norm, loss_target=loss_target, m_ffn1_norm=m_ffn1_norm, m_ffn1_w_gu=m_ffn1_w_gu, m_ffn1_w_down=m_ffn1_w_down, m_mix_norm=m_mix_norm, m_w_in=m_w_in, m_conv_w=m_conv_w, m_w_conv_out=m_w_conv_out, m_q_norm=m_q_norm, m_kv_norm=m_kv_norm, m_w_uq=m_w_uq, m_w_ukv=m_w_ukv, m_w_mla_out=m_w_mla_out, m_w_o=m_w_o, m_ffn2_norm=m_ffn2_norm, m_ffn2_w_gu=m_ffn2_w_gu, m_ffn2_w_down=m_ffn2_w_down, m_ple_norm=m_ple_norm, m_w_ple_gate=m_w_ple_gate, m_w_ple_proj=m_w_ple_proj, m_final_norm=m_final_norm, v_ffn1_norm=v_ffn1_norm, v_ffn1_w_gu=v_ffn1_w_gu, v_ffn1_w_down=v_ffn1_w_down, v_mix_norm=v_mix_norm, v_w_in=v_w_in, v_conv_w=v_conv_w, v_w_conv_out=v_w_conv_out, v_q_norm=v_q_norm, v_kv_norm=v_kv_norm, v_w_uq=v_w_uq, v_w_ukv=v_w_ukv, v_w_mla_out=v_w_mla_out, v_w_o=v_w_o, v_ffn2_norm=v_ffn2_norm, v_ffn2_w_gu=v_ffn2_w_gu, v_ffn2_w_down=v_ffn2_w_down, v_ple_norm=v_ple_norm, v_w_ple_gate=v_w_ple_gate, v_w_ple_proj=v_w_ple_proj, v_final_norm=v_final_norm)
    weights = {n: given[n] for n in TWIN_WEIGHTS}
    shared = {n: given[n] for n in SHARED_INPUTS}
    per_example = {n: given[n] for n in ['x', 'p', 'positions']}
    grad_fn = _jax.value_and_grad(_loss, argnums=(0, 1))

    def one_microbatch(ex, loss_target):
        ex = dict(ex)
        diff = ex.pop(TWIN_DIFF_INPUT)
        return grad_fn(weights, diff, {**shared, **ex}, loss_target)

    if N_MICROBATCH == 1:
        loss, (grad_w, grad_x) = one_microbatch(per_example, given["loss_target"])
    else:
        def body(carry, xs):
            loss_sum, grad_sum = carry
            l_k, (gw_k, gx_k) = one_microbatch(xs[0], xs[1])
            with _jax.named_scope("update"):
                return (loss_sum + l_k, _jax.tree.map(_jnp.add, grad_sum, gw_k)), gx_k

        init = (_jnp.zeros((), _jnp.float32), _jax.tree.map(_jnp.zeros_like, weights))
        (loss, grad_w), grad_x = _jax.lax.scan(body, init, (per_example, given["loss_target"]))
    with _jax.named_scope("update"):
        delta_w, new_m, new_v = {}, {}, {}
        for n in TWIN_WEIGHTS:
            delta_w[n], new_m[n], new_v[n] = _adamw(weights[n], grad_w[n], given["m_" + n], given["v_" + n])
    return (loss, grad_x, *[grad_w[n] for n in TWIN_WEIGHTS], *[delta_w[n] for n in TWIN_WEIGHTS],
            *[new_m[n] for n in TWIN_WEIGHTS], *[new_v[n] for n in TWIN_WEIGHTS])
```

```python
import functools

import jax
import jax.numpy as jnp
from jax import lax
from jax.experimental import pallas as pl
from jax.experimental.pallas import tpu as pltpu

BF16 = jnp.bfloat16
F32 = jnp.float32
SDS = jax.ShapeDtypeStruct
MESH = pl.DeviceIdType.MESH

N_HEADS = 8
NOPE_DIM = 128
ROPE_DIM = 64
V_DIM = 128
QK_PAD = 256
CHUNK = 64
ROPE_THETA = 10000.0
EPS = 1e-6
ATTN_SCALE = (NOPE_DIM + ROPE_DIM) ** -0.5
NEG_BIG = -1e30

ADAM_LR = 0.001
ADAM_B1 = 0.9
ADAM_B2 = 0.999
ADAM_EPS = 1e-08
ADAM_WD = 0.01
ADAM_STEP = 10

LANES = 128
VMEM_LIMIT_BYTES = 56 * 1024 * 1024
ACC_BYTES = 6 * 1024 * 1024
PACK_W = 512
PACK_ROWS = 256

SHARDED = (("ffn1_w_gu", 1), ("ffn1_w_down", 0), ("w_in", 1), ("conv_w", 1), ("w_conv_out", 1),
           ("w_uq", 1), ("w_ukv", 1), ("w_mla_out", 0), ("w_o", 0), ("ffn2_w_gu", 1),
           ("ffn2_w_down", 0), ("w_ple_gate", 0), ("w_ple_proj", 1))
REPLICATED = ("ffn1_norm", "mix_norm", "q_norm", "kv_norm", "ffn2_norm", "ple_norm")
WEIGHTS = ("ffn1_norm", "ffn1_w_gu", "ffn1_w_down", "mix_norm", "w_in", "conv_w", "w_conv_out", "q_norm",
           "kv_norm", "w_uq", "w_ukv", "w_mla_out", "w_o", "ffn2_norm", "ffn2_w_gu", "ffn2_w_down",
           "ple_norm", "w_ple_gate", "w_ple_proj", "final_norm")
ARG_NAMES = ("x", "p", "positions") + WEIGHTS + ("loss_target",) + tuple("m_" + n for n in WEIGHTS) + tuple(
    "v_" + n for n in WEIGHTS)


def _cparams(semantics=None):
    return pltpu.CompilerParams(dimension_semantics=semantics, vmem_limit_bytes=VMEM_LIMIT_BYTES)


def _tile(n, cap, mult=LANES):
    best = None
    for t in range(mult, min(n, cap) + 1, mult):
        if n % t == 0:
            best = t
    return n if best is None else best


def _sigmoid(x):
    return 1.0 / (1.0 + jnp.exp(-x))


def _rowspec(tm, width, col_block=0):
    return pl.BlockSpec((tm, width), lambda i: (i, col_block))


def _colspec(tm, width, offset):
    assert offset % width == 0, (width, offset)
    return _rowspec(tm, width, offset // width)


def _mm(a, b, mode, out_dtype, name, scale=None, res=None):
    if mode == "nn":
        (m, k), (k2, n) = a.shape, b.shape
    elif mode == "nt":
        (m, k), (n, k2) = a.shape, b.shape
    else:
        (k, m), (k2, n) = a.shape, b.shape
    assert k == k2, (a.shape, b.shape, mode)
    tn = _tile(n, 1536)
    tm = _tile(m, min(1408, ACC_BYTES // (4 * tn)))
    tk = _tile(k, 1536)
    nk = k // tk
    dims = {"nn": (((1,), (0,)), ((), ())), "nt": (((1,), (1,)), ((), ())), "tn": (((0,), (0,)), ((), ()))}[mode]

    def body(*refs):
        a_ref, b_ref = refs[0], refs[1]
        res_ref = refs[2] if res is not None else None
        o_ref = refs[3] if res is not None else refs[2]
        acc_ref = refs[-1] if nk > 1 else None

        def finish(acc):
            if scale is not None:
                acc = acc * scale
            if res_ref is not None:
                acc = res_ref[...] + acc
            o_ref[...] = acc.astype(out_dtype)

        part = lax.dot_general(a_ref[...].astype(BF16), b_ref[...].astype(BF16), dims,
                               preferred_element_type=F32)
        if nk == 1:
            finish(part)
        else:
            kk = pl.program_id(2)

            @pl.when(kk == 0)
            def _():
                acc_ref[...] = part

            @pl.when(kk > 0)
            def _():
                acc_ref[...] += part

            @pl.when(kk == nk - 1)
            def _():
                finish(acc_ref[...])

    if mode == "nn":
        a_spec = pl.BlockSpec((tm, tk), lambda i, j, kk: (i, kk))
        b_spec = pl.BlockSpec((tk, tn), lambda i, j, kk: (kk, j))
    elif mode == "nt":
        a_spec = pl.BlockSpec((tm, tk), lambda i, j, kk: (i, kk))
        b_spec = pl.BlockSpec((tn, tk), lambda i, j, kk: (j, kk))
    else:
        a_spec = pl.BlockSpec((tk, tm), lambda i, j, kk: (kk, i))
        b_spec = pl.BlockSpec((tk, tn), lambda i, j, kk: (kk, j))
    o_spec = pl.BlockSpec((tm, tn), lambda i, j, kk: (i, j))
    in_specs = [a_spec, b_spec] + ([o_spec] if res is not None else [])
    operands = (a, b) + ((res,) if res is not None else ())
    return pl.pallas_call(
        body, out_shape=SDS((m, n), out_dtype), grid=(m // tm, n // tn, nk), in_specs=in_specs, out_specs=o_spec,
        scratch_shapes=[pltpu.VMEM((tm, tn), F32)] if nk > 1 else [],
        compiler_params=_cparams(("parallel", "parallel", "arbitrary")), name=name)(*operands)


def _rn_fwd_math(x, g):
    r = lax.rsqrt(jnp.mean(x * x, axis=-1, keepdims=True) + EPS)
    return (x * r) * g


def _rn_bwd_math(x, g, dn):
    r = lax.rsqrt(jnp.mean(x * x, axis=-1, keepdims=True) + EPS)
    xh = x * r
    gy = dn * g
    dx = r * (gy - xh * jnp.mean(gy * xh, axis=-1, keepdims=True))
    dg = jnp.sum(dn * xh, axis=0, keepdims=True)
    return dx, dg


def _rmsnorm_fwd(h, gain, name):
    t, d = h.shape
    tm = _tile(t, 512, 8)

    def body(h_ref, g_ref, o_ref):
        o_ref[...] = _rn_fwd_math(h_ref[...], g_ref[...]).astype(BF16)

    return pl.pallas_call(
        body, out_shape=SDS((t, d), BF16), grid=(t // tm,),
        in_specs=[_rowspec(tm, d), pl.BlockSpec((1, d), lambda i: (0, 0))], out_specs=_rowspec(tm, d),
        compiler_params=_cparams(("parallel",)), name=name)(h, gain.reshape(1, d))


def _rmsnorm_bwd(h, gain, dn, dh_in, name):
    t, d = h.shape
    tm = _tile(t, 512, 8)

    def body(h_ref, g_ref, dn_ref, dhin_ref, dh_ref, dhb_ref, dg_ref):
        dx, dg = _rn_bwd_math(h_ref[...], g_ref[...], dn_ref[...].astype(F32))
        dh = dhin_ref[...] + dx
        dh_ref[...] = dh
        dhb_ref[...] = dh.astype(BF16)

        @pl.when(pl.program_id(0) == 0)
        def _():
            dg_ref[...] = dg

        @pl.when(pl.program_id(0) > 0)
        def _():
            dg_ref[...] += dg

    vec = pl.BlockSpec((1, d), lambda i: (0, 0))
    return pl.pallas_call(
        body, out_shape=(SDS((t, d), F32), SDS((t, d), BF16), SDS((1, d), F32)), grid=(t // tm,),
        in_specs=[_rowspec(tm, d), vec, _rowspec(tm, d), _rowspec(tm, d)],
        out_specs=(_rowspec(tm, d), _rowspec(tm, d), vec),
        compiler_params=_cparams(("arbitrary",)), name=name)(h, gain.reshape(1, d), dn, dh_in)


def _loss_head(h, gain, target):
    t, d = h.shape
    tm = _tile(t, 512, 8)

    def body(h_ref, g_ref, t_ref, loss_ref, dh_ref, dhb_ref, dg_ref):
        x, g = h_ref[...], g_ref[...]
        err = _rn_fwd_math(x, g) - t_ref[...]
        part = 0.5 * jnp.sum(jnp.sum(err * err, axis=1, keepdims=True), axis=0, keepdims=True) * (1.0 / d)
        dx, dg = _rn_bwd_math(x, g, err * (1.0 / d))
        dh_ref[...] = dx
        dhb_ref[...] = dx.astype(BF16)

        @pl.when(pl.program_id(0) == 0)
        def _():
            dg_ref[...] = dg
            loss_ref[...] = jnp.broadcast_to(part, (1, LANES))

        @pl.when(pl.program_id(0) > 0)
        def _():
            dg_ref[...] += dg
            loss_ref[...] += jnp.broadcast_to(part, (1, LANES))

    vec = pl.BlockSpec((1, d), lambda i: (0, 0))
    return pl.pallas_call(
        body, out_shape=(SDS((1, LANES), F32), SDS((t, d), F32), SDS((t, d), BF16), SDS((1, d), F32)),
        grid=(t // tm,), in_specs=[_rowspec(tm, d), vec, _rowspec(tm, d)],
        out_specs=(pl.BlockSpec((1, LANES), lambda i: (0, 0)), _rowspec(tm, d), _rowspec(tm, d), vec),
        compiler_params=_cparams(("arbitrary",)), name="loss_head")(h, gain.reshape(1, d), target)


def _swiglu_fwd(gu):
    t, f2 = gu.shape
    f = f2 // 2
    tm = _tile(t, 256, 8)

    def body(gu_ref, a_ref):
        g = gu_ref[:, :f].astype(F32)
        u = gu_ref[:, f:].astype(F32)
        a_ref[...] = (g * _sigmoid(g) * u).astype(BF16)

    return pl.pallas_call(
        body, out_shape=SDS((t, f), BF16), grid=(t // tm,), in_specs=[_rowspec(tm, f2)],
        out_specs=_rowspec(tm, f), compiler_params=_cparams(("parallel",)), name="swiglu_fwd")(gu)


def _swiglu_bwd(gu, da):
    t, f2 = gu.shape
    f = f2 // 2
    tm = _tile(t, 256, 8)

    def body(gu_ref, da_ref, dgu_ref):
        g = gu_ref[:, :f].astype(F32)
        u = gu_ref[:, f:].astype(F32)
        da_ = da_ref[...].astype(F32)
        s = _sigmoid(g)
        dgu_ref[:, :f] = (da_ * u * (s * (1.0 + g * (1.0 - s)))).astype(BF16)
        dgu_ref[:, f:] = (da_ * (g * s)).astype(BF16)

    return pl.pallas_call(
        body, out_shape=SDS((t, f2), BF16), grid=(t // tm,), in_specs=[_rowspec(tm, f2), _rowspec(tm, f)],
        out_specs=_rowspec(tm, f2), compiler_params=_cparams(("parallel",)), name="swiglu_bwd")(gu, da)


def _shift_down(z, k, row):
    return jnp.where(row >= k, pltpu.roll(z, k, 0), 0.0)


def _shift_up(z, k, row, t):
    return jnp.where(row < t - k, pltpu.roll(z, t - k, 0), 0.0)


def _conv_specs(t, conv):
    nb = conv // LANES
    return [pl.BlockSpec((t, LANES), lambda j: (0, j)), pl.BlockSpec((t, LANES), lambda j: (0, nb + j)),
            pl.BlockSpec((t, LANES), lambda j: (0, 2 * nb + j))]


def _conv_fwd(proj, conv_w):
    t = proj.shape[0]
    conv = conv_w.shape[1]

    def body(b_ref, c_ref, v_ref, w_ref, o_ref):
        z = c_ref[...].astype(F32) * v_ref[...].astype(F32)
        row = lax.broadcasted_iota(jnp.int32, z.shape, 0)
        y = w_ref[0:1, :] * _shift_down(z, 2, row) + w_ref[1:2, :] * _shift_down(z, 1, row) + w_ref[2:3, :] * z
        o_ref[...] = (b_ref[...].astype(F32) * y).astype(BF16)

    cspec = pl.BlockSpec((t, LANES), lambda j: (0, j))
    return pl.pallas_call(
        body, out_shape=SDS((t, conv), BF16), grid=(conv // LANES,),
        in_specs=_conv_specs(t, conv) + [pl.BlockSpec((3, LANES), lambda j: (0, j))], out_specs=cspec,
        compiler_params=_cparams(("parallel",)), name="conv_fwd")(proj, proj, proj, conv_w)


def _conv_bwd(proj, conv_w, dcb):
    t = proj.shape[0]
    conv = conv_w.shape[1]

    def body(b_ref, c_ref, v_ref, w_ref, d_ref, db_ref, dc_ref, dv_ref, dw_ref):
        b, c, v = b_ref[...].astype(F32), c_ref[...].astype(F32), v_ref[...].astype(F32)
        d = d_ref[...].astype(F32)
        z = c * v
        row = lax.broadcasted_iota(jnp.int32, z.shape, 0)
        z1, z2 = _shift_down(z, 1, row), _shift_down(z, 2, row)
        w0, w1, w2 = w_ref[0:1, :], w_ref[1:2, :], w_ref[2:3, :]
        y = w0 * z2 + w1 * z1 + w2 * z
        dy = d * b
        db_ref[...] = (d * y).astype(BF16)
        dz = w2 * dy + w1 * _shift_up(dy, 1, row, t) + w0 * _shift_up(dy, 2, row, t)
        dc_ref[...] = (dz * v).astype(BF16)
        dv_ref[...] = (dz * c).astype(BF16)
        dw_ref[0:1, :] = jnp.sum(dy * z2, axis=0, keepdims=True)
        dw_ref[1:2, :] = jnp.sum(dy * z1, axis=0, keepdims=True)
        dw_ref[2:3, :] = jnp.sum(dy * z, axis=0, keepdims=True)

    cspec = pl.BlockSpec((t, LANES), lambda j: (0, j))
    wspec = pl.BlockSpec((3, LANES), lambda j: (0, j))
    return pl.pallas_call(
        body, out_shape=(SDS((t, conv), BF16),) * 3 + (SDS((3, conv), F32),), grid=(conv // LANES,),
        in_specs=_conv_specs(t, conv) + [wspec, cspec], out_specs=(cspec, cspec, cspec, wspec),
        compiler_params=_cparams(("parallel",)), name="conv_bwd")(proj, proj, proj, conv_w, dcb)


def _qkvnorm_fwd(proj, lay, q_gain, kv_gain):
    t = proj.shape[0]
    ql, kvl = lay["ql"], lay["kvl"]
    tm = _tile(t, 512, 8)

    def body(q_ref, kv_ref, gq_ref, gkv_ref, qn_ref, kvn_ref):
        qn_ref[...] = _rn_fwd_math(q_ref[...].astype(F32), gq_ref[...]).astype(BF16)
        kvn_ref[...] = _rn_fwd_math(kv_ref[...].astype(F32), gkv_ref[...]).astype(BF16)

    return pl.pallas_call(
        body, out_shape=(SDS((t, ql), BF16), SDS((t, kvl), BF16)), grid=(t // tm,),
        in_specs=[_colspec(tm, ql, lay["q"]), _colspec(tm, kvl, lay["kv"]),
                  pl.BlockSpec((1, ql), lambda i: (0, 0)), pl.BlockSpec((1, kvl), lambda i: (0, 0))],
        out_specs=(_rowspec(tm, ql), _rowspec(tm, kvl)), compiler_params=_cparams(("parallel",)),
        name="qkvnorm_fwd")(proj, proj, q_gain.reshape(1, ql), kv_gain.reshape(1, kvl))


def _qkvnorm_bwd(proj, lay, q_gain, kv_gain, dqn, dkvn):
    t = proj.shape[0]
    ql, kvl = lay["ql"], lay["kvl"]
    tm = _tile(t, 512, 8)

    def body(q_ref, kv_ref, gq_ref, gkv_ref, dqn_ref, dkvn_ref, dq_ref, dkv_ref, dgq_ref, dgkv_ref):
        dq, dgq = _rn_bwd_math(q_ref[...].astype(F32), gq_ref[...], dqn_ref[...].astype(F32))
        dkv, dgkv = _rn_bwd_math(kv_ref[...].astype(F32), gkv_ref[...], dkvn_ref[...].astype(F32))
        dq_ref[...] = dq.astype(BF16)
        dkv_ref[...] = dkv.astype(BF16)

        @pl.when(pl.program_id(0) == 0)
        def _():
            dgq_ref[...] = dgq
            dgkv_ref[...] = dgkv

        @pl.when(pl.program_id(0) > 0)
        def _():
            dgq_ref[...] += dgq
            dgkv_ref[...] += dgkv

    vq = pl.BlockSpec((1, ql), lambda i: (0, 0))
    vkv = pl.BlockSpec((1, kvl), lambda i: (0, 0))
    return pl.pallas_call(
        body, out_shape=(SDS((t, ql), BF16), SDS((t, kvl), BF16), SDS((1, ql), F32), SDS((1, kvl), F32)),
        grid=(t // tm,),
        in_specs=[_colspec(tm, ql, lay["q"]), _colspec(tm, kvl, lay["kv"]), vq, vkv, _rowspec(tm, ql),
                  _rowspec(tm, kvl)],
        out_specs=(_rowspec(tm, ql), _rowspec(tm, kvl), vq, vkv), compiler_params=_cparams(("arbitrary",)),
        name="qkvnorm_bwd")(proj, proj, q_gain.reshape(1, ql), kv_gain.reshape(1, kvl), dqn, dkvn)


def _rope(x, cos_t, sin_a, sin_b):
    return x * cos_t + pltpu.roll(x, LANES - ROPE_DIM // 2, 1) * sin_a + pltpu.roll(x, ROPE_DIM // 2, 1) * sin_b


def _rope_fwd(qf, kv, proj, lay, tables):
    t = qf.shape[0]
    tm = _tile(t, 256, 8)
    hq = N_HEADS * QK_PAD

    def body(q_ref, kn_ref, kr_ref, cos_ref, sa_ref, sb_ref, qr_ref, kf_ref):
        cos_t, sin_a, sin_b = cos_ref[...], sa_ref[...], sb_ref[...]
        kr = _rope(kr_ref[...].astype(F32), cos_t, sin_a, sin_b).astype(BF16)
        for h in range(N_HEADS):
            lo = h * QK_PAD
            qr_ref[:, lo:lo + NOPE_DIM] = q_ref[:, lo:lo + NOPE_DIM]
            qr_ref[:, lo + NOPE_DIM:lo + QK_PAD] = _rope(
                q_ref[:, lo + NOPE_DIM:lo + QK_PAD].astype(F32), cos_t, sin_a, sin_b).astype(BF16)
            kf_ref[:, lo:lo + NOPE_DIM] = kn_ref[:, h * NOPE_DIM:(h + 1) * NOPE_DIM]
            kf_ref[:, lo + NOPE_DIM:lo + QK_PAD] = kr

    tab = _rowspec(tm, LANES)
    return pl.pallas_call(
        body, out_shape=(SDS((t, hq), BF16), SDS((t, hq), BF16)), grid=(t // tm,),
        in_specs=[_rowspec(tm, hq), _rowspec(tm, N_HEADS * NOPE_DIM), _colspec(tm, LANES, lay["kr"]), tab, tab, tab],
        out_specs=(_rowspec(tm, hq), _rowspec(tm, hq)), compiler_params=_cparams(("parallel",)),
        name="rope_fwd")(qf, kv, proj, *tables)


def _rope_bwd(dqr, dkf, dv, tables):
    t = dqr.shape[0]
    tm = _tile(t, 256, 8)
    hq = N_HEADS * QK_PAD
    hn = N_HEADS * NOPE_DIM

    def body(dq_ref, dk_ref, dv_ref, cos_ref, sa_ref, sb_ref, dqf_ref, dkv_ref, dkr_ref):
        cos_t, sin_a, sin_b = cos_ref[...], -sa_ref[...], -sb_ref[...]
        dkr = jnp.zeros((tm, LANES), F32)
        for h in range(N_HEADS):
            lo = h * QK_PAD
            dqf_ref[:, lo:lo + NOPE_DIM] = dq_ref[:, lo:lo + NOPE_DIM].astype(BF16)
            dqf_ref[:, lo + NOPE_DIM:lo + QK_PAD] = _rope(
                dq_ref[:, lo + NOPE_DIM:lo + QK_PAD].astype(F32), cos_t, sin_a, sin_b).astype(BF16)
            dkv_ref[:, h * NOPE_DIM:(h + 1) * NOPE_DIM] = dk_ref[:, lo:lo + NOPE_DIM]
            dkr = dkr + dk_ref[:, lo + NOPE_DIM:lo + QK_PAD].astype(F32)
        dkv_ref[:, hn:] = dv_ref[...]
        dkr_ref[...] = _rope(dkr, cos_t, sin_a, sin_b).astype(BF16)

    tab = _rowspec(tm, LANES)
    return pl.pallas_call(
        body, out_shape=(SDS((t, hq), BF16), SDS((t, 2 * hn), BF16), SDS((t, LANES), BF16)), grid=(t // tm,),
        in_specs=[_rowspec(tm, hq), _rowspec(tm, hq), _rowspec(tm, hn), tab, tab, tab],
        out_specs=(_rowspec(tm, hq), _rowspec(tm, 2 * hn), tab), compiler_params=_cparams(("parallel",)),
        name="rope_bwd")(dqr, dkf, dv, *tables)


def _chunk_mask(bq):
    rows = lax.broadcasted_iota(jnp.int32, (bq, bq), 0) // CHUNK
    cols = lax.broadcasted_iota(jnp.int32, (bq, bq), 1) // CHUNK
    return cols <= rows


_NT = (((1,), (1,)), ((), ()))
_TN = (((0,), (0,)), ((), ()))


def _attn_block(t):
    return 512 if t >= 2048 else 128


def _attn_fwd(qr, kf, kv):
    t = qr.shape[0]
    bq = _attn_block(t)
    nq = t // bq

    def body(q_ref, k_ref, v_ref, o_ref, lse_ref):
        i = pl.program_id(1)
        q = q_ref[...]

        def block(j, carry, masked):
            m, l, acc = carry
            off = pl.multiple_of(j * bq, bq)
            k = k_ref[pl.ds(off, bq), :]
            v = v_ref[pl.ds(off, bq), :]
            s = lax.dot_general(q, k, _NT, preferred_element_type=F32) * ATTN_SCALE
            if masked:
                s = jnp.where(_chunk_mask(bq), s, NEG_BIG)
            m_new = jnp.maximum(m, jnp.max(s, axis=1, keepdims=True))
            alpha = jnp.exp(m - m_new)
            pr = jnp.exp(s - m_new)
            l = alpha * l + jnp.sum(pr, axis=1, keepdims=True)
            acc = alpha * acc + jnp.dot(pr.astype(BF16), v, preferred_element_type=F32)
            return m_new, l, acc

        init = (jnp.full((bq, 1), NEG_BIG, F32), jnp.zeros((bq, 1), F32), jnp.zeros((bq, V_DIM), F32))
        carry = lax.fori_loop(0, i, lambda j, c: block(j, c, False), init)
        m, l, acc = block(i, carry, True)
        o_ref[...] = (acc / l).astype(BF16)
        lse_ref[0] = jnp.broadcast_to(m + jnp.log(l), (bq, LANES))

    return pl.pallas_call(
        body, out_shape=(SDS((t, N_HEADS * V_DIM), BF16), SDS((N_HEADS, t, LANES), F32)), grid=(N_HEADS, nq),
        in_specs=[pl.BlockSpec((bq, QK_PAD), lambda h, i: (i, h)), pl.BlockSpec((t, QK_PAD), lambda h, i: (0, h)),
                  pl.BlockSpec((t, V_DIM), lambda h, i: (0, N_HEADS + h))],
        out_specs=(pl.BlockSpec((bq, V_DIM), lambda h, i: (i, h)), pl.BlockSpec((1, bq, LANES), lambda h, i: (h, i, 0))),
        compiler_params=_cparams(("parallel", "parallel")), name="attn_fwd")(qr, kf, kv)


def _attn_delta(do, o):
    t = do.shape[0]
    tm = _tile(t, 512, 8)

    def body(do_ref, o_ref, dl_ref):
        prod = do_ref[...].astype(F32) * o_ref[...].astype(F32)
        for h in range(N_HEADS):
            s = jnp.sum(prod[:, h * V_DIM:(h + 1) * V_DIM], axis=1, keepdims=True)
            dl_ref[h] = jnp.broadcast_to(s, (tm, LANES))

    return pl.pallas_call(
        body, out_shape=SDS((N_HEADS, t, LANES), F32), grid=(t // tm,),
        in_specs=[_rowspec(tm, N_HEADS * V_DIM), _rowspec(tm, N_HEADS * V_DIM)],
        out_specs=pl.BlockSpec((N_HEADS, tm, LANES), lambda i: (0, i, 0)), compiler_params=_cparams(("parallel",)),
        name="attn_delta")(do, o)


def _attn_bwd(qr, kf, kv, do, lse, delta):
    t = qr.shape[0]
    bq = _attn_block(t)
    nq = t // bq

    def body(k_ref, v_ref, q_ref, do_ref, lse_ref, dl_ref, dk_ref, dv_ref, dq_ref):
        j = pl.program_id(1)

        @pl.when(j == 0)
        def _():
            dq_ref[...] = jnp.zeros_like(dq_ref)

        k = k_ref[...]
        v = v_ref[...]

        def block(i, carry, masked):
            dk, dv = carry
            off = pl.multiple_of(i * bq, bq)
            q = q_ref[pl.ds(off, bq), :]
            do_ = do_ref[pl.ds(off, bq), :]
            lse_i = lse_ref[0, pl.ds(off, bq), :][:, :1]
            dl_i = dl_ref[0, pl.ds(off, bq), :][:, :1]
            s = lax.dot_general(q, k, _NT, preferred_element_type=F32) * ATTN_SCALE
            if masked:
                s = jnp.where(_chunk_mask(bq), s, NEG_BIG)
            pr = jnp.exp(s - lse_i)
            dv = dv + lax.dot_general(pr.astype(BF16), do_, _TN, preferred_element_type=F32)
            dp = lax.dot_general(do_, v, _NT, preferred_element_type=F32)
            ds = (pr * (dp - dl_i) * ATTN_SCALE).astype(BF16)
            dk = dk + lax.dot_general(ds, q, _TN, preferred_element_type=F32)
            dq_ref[pl.ds(off, bq), :] += jnp.dot(ds, k, preferred_element_type=F32)
            return dk, dv

        carry = block(j, (jnp.zeros((bq, QK_PAD), F32), jnp.zeros((bq, V_DIM), F32)), True)
        dk, dv = lax.fori_loop(j + 1, nq, lambda i, c: block(i, c, False), carry)
        dk_ref[...] = dk.astype(BF16)
        dv_ref[...] = dv.astype(BF16)

    stat = pl.BlockSpec((1, t, LANES), lambda h, j: (h, 0, 0))
    return pl.pallas_call(
        body,
        out_shape=(SDS((t, N_HEADS * QK_PAD), BF16), SDS((t, N_HEADS * V_DIM), BF16), SDS((t, N_HEADS * QK_PAD), F32)),
        grid=(N_HEADS, nq),
        in_specs=[pl.BlockSpec((bq, QK_PAD), lambda h, j: (j, h)), pl.BlockSpec((bq, V_DIM), lambda h, j: (j, N_HEADS + h)),
                  pl.BlockSpec((t, QK_PAD), lambda h, j: (0, h)), pl.BlockSpec((t, V_DIM), lambda h, j: (0, h)), stat, stat],
        out_specs=(pl.BlockSpec((bq, QK_PAD), lambda h, j: (j, h)), pl.BlockSpec((bq, V_DIM), lambda h, j: (j, h)),
                   pl.BlockSpec((t, QK_PAD), lambda h, j: (0, h))),
        compiler_params=_cparams(("parallel", "arbitrary")), name="attn_bwd")(kf, kv, qr, do, lse, delta)


def _merge_fwd(proj, lay, ya, yb):
    t, d = ya.shape
    tm = _tile(t, 512, 8)

    def body(gc_ref, gm_ref, ya_ref, yb_ref, o_ref):
        o_ref[...] = (_sigmoid(gc_ref[...].astype(F32)) * ya_ref[...].astype(F32)
                      + _sigmoid(gm_ref[...].astype(F32)) * yb_ref[...].astype(F32)).astype(BF16)

    return pl.pallas_call(
        body, out_shape=SDS((t, d), BF16), grid=(t // tm,),
        in_specs=[_colspec(tm, d, lay["gc"]), _colspec(tm, d, lay["gm"]), _rowspec(tm, d), _rowspec(tm, d)],
        out_specs=_rowspec(tm, d), compiler_params=_cparams(("parallel",)), name="merge_fwd")(proj, proj, ya, yb)


def _merge_bwd(proj, lay, ya, yb, dmg):
    t, d = ya.shape
    tm = _tile(t, 512, 8)

    def body(gc_ref, gm_ref, ya_ref, yb_ref, d_ref, dya_ref, dyb_ref, dgc_ref, dgm_ref):
        dm = d_ref[...].astype(F32)
        sc = _sigmoid(gc_ref[...].astype(F32))
        sm = _sigmoid(gm_ref[...].astype(F32))
        dya_ref[...] = (dm * sc).astype(BF16)
        dyb_ref[...] = (dm * sm).astype(BF16)
        dgc_ref[...] = (dm * ya_ref[...].astype(F32) * (sc * (1.0 - sc))).astype(BF16)
        dgm_ref[...] = (dm * yb_ref[...].astype(F32) * (sm * (1.0 - sm))).astype(BF16)

    r = _rowspec(tm, d)
    return pl.pallas_call(
        body, out_shape=(SDS((t, d), BF16),) * 4, grid=(t // tm,),
        in_specs=[_colspec(tm, d, lay["gc"]), _colspec(tm, d, lay["gm"]), r, r, r], out_specs=(r, r, r, r),
        compiler_params=_cparams(("parallel",)), name="merge_bwd")(proj, proj, ya, yb, dmg)


def _ple_fwd(h, gp, pp):
    t, d = h.shape
    tm = _tile(t, 512, 8)

    def body(h_ref, gp_ref, pp_ref, o_ref):
        o_ref[...] = h_ref[...] + _sigmoid(gp_ref[...].astype(F32)) * pp_ref[...].astype(F32)

    r = _rowspec(tm, d)
    return pl.pallas_call(body, out_shape=SDS((t, d), F32), grid=(t // tm,), in_specs=[r, r, r], out_specs=r,
                          compiler_params=_cparams(("parallel",)), name="ple_fwd")(h, gp, pp)


def _ple_bwd(dh, gp, pp):
    t, d = dh.shape
    tm = _tile(t, 512, 8)

    def body(dh_ref, gp_ref, pp_ref, dpp_ref, dgp_ref):
        g = dh_ref[...]
        s = _sigmoid(gp_ref[...].astype(F32))
        dpp_ref[...] = (g * s).astype(BF16)
        dgp_ref[...] = (g * pp_ref[...].astype(F32) * (s * (1.0 - s))).astype(BF16)

    r = _rowspec(tm, d)
    return pl.pallas_call(body, out_shape=(SDS((t, d), BF16),) * 2, grid=(t // tm,), in_specs=[r, r, r],
                          out_specs=(r, r), compiler_params=_cparams(("parallel",)), name="ple_bwd")(dh, gp, pp)


def _adamw(w, g, m, v):
    shape = w.shape
    cols = shape[-1]
    rows = w.size // cols
    tr = _tile(rows, max(8, (1 << 19) // cols // 8 * 8), 8)

    def body(w_ref, g_ref, m_ref, v_ref, d_ref, nm_ref, nv_ref):
        g_ = g_ref[...]
        nm = ADAM_B1 * m_ref[...] + (1.0 - ADAM_B1) * g_
        nv = ADAM_B2 * v_ref[...] + (1.0 - ADAM_B2) * (g_ * g_)
        m_hat = nm / (1.0 - ADAM_B1 ** ADAM_STEP)
        v_hat = nv / (1.0 - ADAM_B2 ** ADAM_STEP)
        d_ref[...] = -ADAM_LR * (m_hat / (jnp.sqrt(v_hat) + ADAM_EPS) + ADAM_WD * w_ref[...])
        nm_ref[...] = nm
        nv_ref[...] = nv

    r = _rowspec(tr, cols)
    outs = pl.pallas_call(
        body, out_shape=(SDS((rows, cols), F32),) * 3, grid=(rows // tr,), in_specs=[r, r, r, r], out_specs=(r, r, r),
        compiler_params=_cparams(("parallel",)), name="adamw")(*(a.reshape(rows, cols) for a in (w, g, m, v)))
    return tuple(o.reshape(shape) for o in outs)


ANY = pl.BlockSpec(memory_space=pl.ANY)


def _place():
    x, y, c = lax.axis_index("x"), lax.axis_index("y"), lax.axis_index("c")
    return x, y, c, [(1 - x, y), (x, 1 - y), (1 - x, 1 - y)]


def _all_gather_weights(wpack):
    _, r, w = wpack.shape

    def body(w_ref, out_ref, send_sems, recv_sems, local_sem):
        x, y, c, chips = _place()
        me, sibling = (x, y, c), (x, y, 1 - c)

        def slot(half, chip):
            return out_ref.at[half, 2 * chip[0] + chip[1]]

        def copy(k, half, chip, to, src=None):
            return pltpu.make_async_remote_copy(
                src_ref=slot(half, chip) if src is None else src, dst_ref=slot(half, chip),
                send_sem=send_sems.at[k], recv_sem=recv_sems.at[k], device_id=to, device_id_type=MESH)

        mine = pltpu.make_async_copy(w_ref.at[c], slot(c, (x, y)), local_sem)
        mine.start()
        first = [copy(0, c, (x, y), sibling, src=w_ref.at[c])]
        first += [copy(1 + n, c, (x, y), (*chip, c), src=w_ref.at[c]) for n, chip in enumerate(chips)]
        for cp in first:
            cp.start()
        passed = [copy(4 + n, c, chip, sibling) for n, chip in enumerate(chips)]
        for n, chip in enumerate(chips):
            copy(1 + n, c, chip, me).wait_recv()
            passed[n].start()
        copy(0, 1 - c, (x, y), me).wait_recv()
        for n, chip in enumerate(chips):
            copy(4 + n, 1 - c, chip, me).wait_recv()
        for cp in first + passed:
            cp.wait_send()
        mine.wait()

    return pl.pallas_call(
        body, out_shape=SDS((2, 4, r, w), wpack.dtype), in_specs=[ANY], out_specs=ANY,
        scratch_shapes=[pltpu.SemaphoreType.DMA((7,)), pltpu.SemaphoreType.DMA((7,)), pltpu.SemaphoreType.DMA],
        name="all_gather_weights")(wpack)


def _pair_swap_halves(g):
    _, _, r, w = g.shape

    def body(g_ref, out_ref, send_sems, recv_sems):
        x, y, c, _ = _place()
        copies = [pltpu.make_async_remote_copy(
            src_ref=g_ref.at[k, 1 - c], dst_ref=out_ref.at[k], send_sem=send_sems.at[k], recv_sem=recv_sems.at[k],
            device_id=(x, y, 1 - c), device_id_type=MESH) for k in range(4)]
        for cp in copies:
            cp.start()
        for cp in copies:
            cp.wait()

    return pl.pallas_call(
        body, out_shape=SDS((4, r, w), g.dtype), in_specs=[ANY], out_specs=ANY,
        scratch_shapes=[pltpu.SemaphoreType.DMA((4,)), pltpu.SemaphoreType.DMA((4,))], name="pair_swap_halves")(g)


def _pair_add(g, other, c_idx):
    _, _, r, w = g.shape
    tr = _tile(r, PACK_ROWS, 8)

    def body(c_ref, g_ref, o_ref, out_ref):
        out_ref[...] = (g_ref[0] + o_ref[...]).astype(BF16)

    return pl.pallas_call(
        body, out_shape=SDS((4, r, w), BF16),
        grid_spec=pltpu.PrefetchScalarGridSpec(
            num_scalar_prefetch=1, grid=(4, r // tr),
            in_specs=[pl.BlockSpec((1, 1, tr, w), lambda k, i, c: (k, c[0], i, 0)),
                      pl.BlockSpec((1, tr, w), lambda k, i, c: (k, i, 0))],
            out_specs=pl.BlockSpec((1, tr, w), lambda k, i, c: (k, i, 0))),
        compiler_params=_cparams(("parallel", "parallel")), name="pair_add")(c_idx, g, other)


def _chip_all_to_all(pb):
    _, r, w = pb.shape

    def body(p_ref, out_ref, send_sems, recv_sems, local_sem):
        x, y, c, chips = _place()
        mine_k = 2 * x + y
        mine = pltpu.make_async_copy(p_ref.at[mine_k], out_ref.at[mine_k], local_sem)
        mine.start()
        copies = [pltpu.make_async_remote_copy(
            src_ref=p_ref.at[2 * chip[0] + chip[1]], dst_ref=out_ref.at[mine_k], send_sem=send_sems.at[n],
            recv_sem=recv_sems.at[n], device_id=(*chip, c), device_id_type=MESH) for n, chip in enumerate(chips)]
        for cp in copies:
            cp.start()
        for n, chip in enumerate(chips):
            pltpu.make_async_remote_copy(
                src_ref=p_ref.at[mine_k], dst_ref=out_ref.at[2 * chip[0] + chip[1]], send_sem=send_sems.at[n],
                recv_sem=recv_sems.at[n], device_id=(*chip, c), device_id_type=MESH).wait_recv()
        for cp in copies:
            cp.wait_send()
        mine.wait()

    return pl.pallas_call(
        body, out_shape=SDS((4, r, w), pb.dtype), in_specs=[ANY], out_specs=ANY,
        scratch_shapes=[pltpu.SemaphoreType.DMA((3,)), pltpu.SemaphoreType.DMA((3,)), pltpu.SemaphoreType.DMA],
        name="chip_all_to_all")(pb)


def _sum_chips(rin):
    _, r, w = rin.shape
    tr = _tile(r, PACK_ROWS, 8)

    def body(r_ref, out_ref):
        out_ref[...] = ((r_ref[0].astype(F32) + r_ref[1].astype(F32)) + r_ref[2].astype(F32)) + r_ref[3].astype(F32)

    return pl.pallas_call(
        body, out_shape=SDS((r, w), F32), grid=(r // tr,), in_specs=[pl.BlockSpec((4, tr, w), lambda i: (0, i, 0))],
        out_specs=pl.BlockSpec((tr, w), lambda i: (i, 0)), compiler_params=_cparams(("parallel",)), name="sum_chips")(rin)


def _pair_gather(half):
    r, w = half.shape

    def body(h_ref, out_ref, send_sem, recv_sem, local_sem):
        x, y, c, _ = _place()
        mine = pltpu.make_async_copy(h_ref, out_ref.at[c], local_sem)
        mine.start()
        cp = pltpu.make_async_remote_copy(src_ref=h_ref, dst_ref=out_ref.at[c], send_sem=send_sem, recv_sem=recv_sem,
                                          device_id=(x, y, 1 - c), device_id_type=MESH)
        cp.start()
        pltpu.make_async_remote_copy(src_ref=h_ref, dst_ref=out_ref.at[1 - c], send_sem=send_sem, recv_sem=recv_sem,
                                     device_id=(x, y, 1 - c), device_id_type=MESH).wait_recv()
        cp.wait_send()
        mine.wait()

    return pl.pallas_call(
        body, out_shape=SDS((2, r, w), half.dtype), in_specs=[ANY], out_specs=ANY,
        scratch_shapes=[pltpu.SemaphoreType.DMA, pltpu.SemaphoreType.DMA, pltpu.SemaphoreType.DMA],
        name="pair_gather")(half)


def _all_sum_small(vec):
    rows, w = vec.shape

    def body(v_ref, sum_ref, all_ref, send_sems, recv_sems, local_sem):
        x, y, c, chips = _place()
        me, sibling = (x, y, c), (x, y, 1 - c)

        def slot(px, py, pc):
            return all_ref.at[4 * px + 2 * py + pc]

        def copy(k, block, to, src=None):
            return pltpu.make_async_remote_copy(
                src_ref=slot(*block) if src is None else src, dst_ref=slot(*block), send_sem=send_sems.at[k],
                recv_sem=recv_sems.at[k], device_id=to, device_id_type=MESH)

        mine = pltpu.make_async_copy(v_ref, slot(*me), local_sem)
        mine.start()
        first = [copy(0, me, sibling, src=v_ref)]
        first += [copy(1 + n, me, (*chip, c), src=v_ref) for n, chip in enumerate(chips)]
        for cp in first:
            cp.start()
        passed = [copy(4 + n, (*chip, c), sibling) for n, chip in enumerate(chips)]
        for n, chip in enumerate(chips):
            copy(1 + n, (*chip, c), me).wait_recv()
            passed[n].start()
        copy(0, sibling, me).wait_recv()
        for n, chip in enumerate(chips):
            copy(4 + n, (*chip, 1 - c), me).wait_recv()
        for cp in first + passed:
            cp.wait_send()
        mine.wait()
        total = all_ref[0]
        for dev in range(1, 8):
            total = total + all_ref[dev]
        sum_ref[...] = total

    vm = pl.BlockSpec(memory_space=pltpu.VMEM)
    return pl.pallas_call(
        body, out_shape=(SDS((rows, w), F32), SDS((8, rows, w), F32)), in_specs=[vm], out_specs=(vm, vm),
        scratch_shapes=[pltpu.SemaphoreType.DMA((7,)), pltpu.SemaphoreType.DMA((7,)), pltpu.SemaphoreType.DMA],
        name="all_sum_small")(vec)[0]


def _in_layout(conv, ql, kvl, d):
    lay = {"conv": conv, "ql": ql, "kvl": kvl, "d": d}
    lay["q"] = 3 * conv
    lay["kr"] = lay["q"] + ql
    lay["gc"] = lay["kr"] + LANES
    lay["gm"] = lay["gc"] + d
    lay["kv"] = lay["gm"] + d
    used = lay["kv"] + kvl
    lay["width"] = -(-used // 512) * 512
    return lay


def _w_in_to_layout(w, lay):
    conv, ql, kvl, d = lay["conv"], lay["ql"], lay["kvl"], lay["d"]
    o_kv = 3 * conv + ql
    o_kr = o_kv + kvl
    o_g = o_kr + ROPE_DIM
    rows = w.shape[0]
    parts = [w[:, :o_kv], w[:, o_kr:o_g], jnp.zeros((rows, LANES - ROPE_DIM), w.dtype), w[:, o_g:o_g + 2 * d],
             w[:, o_kv:o_kr], jnp.zeros((rows, lay["width"] - lay["kv"] - kvl), w.dtype)]
    return jnp.concatenate(parts, axis=1)


def _w_in_from_layout(g, lay):
    ql, kvl, d = lay["ql"], lay["kvl"], lay["d"]
    return jnp.concatenate([g[:, :lay["q"] + ql], g[:, lay["kv"]:lay["kv"] + kvl], g[:, lay["kr"]:lay["kr"] + ROPE_DIM],
                            g[:, lay["gc"]:lay["gc"] + 2 * d]], axis=1)


def _w_uq_to_layout(w):
    r = w.shape[0]
    w3 = w.reshape(r, N_HEADS, NOPE_DIM + ROPE_DIM)
    return jnp.pad(w3, ((0, 0), (0, 0), (0, QK_PAD - NOPE_DIM - ROPE_DIM))).reshape(r, N_HEADS * QK_PAD)


def _w_uq_from_layout(g):
    r = g.shape[0]
    return g.reshape(r, N_HEADS, QK_PAD)[:, :, :NOPE_DIM + ROPE_DIM].reshape(r, N_HEADS * (NOPE_DIM + ROPE_DIM))


def _w_ukv_to_layout(w):
    r = w.shape[0]
    return w.reshape(r, N_HEADS, 2, NOPE_DIM).transpose(0, 2, 1, 3).reshape(r, 2 * N_HEADS * NOPE_DIM)


def _w_ukv_from_layout(g):
    r = g.shape[0]
    return g.reshape(r, 2, N_HEADS, NOPE_DIM).transpose(0, 2, 1, 3).reshape(r, 2 * N_HEADS * NOPE_DIM)


def _pack_sizes(shard_shapes):
    n = sum(s[0] * s[1] for s in shard_shapes.values())
    quantum = 2 * PACK_W * PACK_ROWS
    n_pad = -(-n // quantum) * quantum
    return n, n_pad


def _gather_layer_weights(shards, shard_shapes):
    n, n_pad = _pack_sizes(shard_shapes)
    flat = jnp.concatenate([shards[name].reshape(-1) for name, _ in SHARDED]).astype(BF16)
    flat = jnp.pad(flat, (0, n_pad - n)).reshape(2, n_pad // (2 * PACK_W), PACK_W)
    got = _all_gather_weights(flat).transpose(1, 0, 2, 3).reshape(4, n_pad)
    full, off = {}, 0
    for name, axis in SHARDED:
        shape = shard_shapes[name]
        size = shape[0] * shape[1]
        full[name] = jnp.concatenate([got[k, off:off + size].reshape(shape) for k in range(4)], axis=axis)
        off += size
    return full


def _reduce_scatter_layer_grads(grads, shard_shapes, c_idx):
    n, n_pad = _pack_sizes(shard_shapes)
    rows = n_pad // (2 * PACK_W)
    per_chip = []
    for k in range(4):
        pieces = []
        for name, axis in SHARDED:
            size = shard_shapes[name][axis]
            pieces.append(lax.slice_in_dim(grads[name], k * size, (k + 1) * size, axis=axis).reshape(-1))
        per_chip.append(jnp.pad(jnp.concatenate(pieces), (0, n_pad - n)))
    g = jnp.stack(per_chip).reshape(4, 2, rows, PACK_W)
    pair = _pair_add(g, _pair_swap_halves(g), c_idx)
    mine = _sum_chips(_chip_all_to_all(pair))
    flat = _pair_gather(mine).reshape(n_pad)
    out, off = {}, 0
    for name, _ in SHARDED:
        shape = shard_shapes[name]
        size = shape[0] * shape[1]
        out[name] = flat[off:off + size].reshape(shape)
        off += size
    return out


def _ffn_fwd(h, gain, w_gu, w_down, tag):
    n = _rmsnorm_fwd(h, gain, tag + "_norm_fwd")
    gu = _mm(n, w_gu, "nn", BF16, tag + "_gu_fwd")
    a = _swiglu_fwd(gu)
    out = _mm(a, w_down, "nn", F32, tag + "_down_fwd", scale=0.5, res=h)
    return out, (h, n, gu, a)


def _ffn_bwd(dh, dhb, saved, gain, w_gu, w_down, tag):
    h, n, gu, a = saved
    d_wdown = _mm(a, dhb, "tn", F32, tag + "_down_dw", scale=0.5)
    da = _mm(dhb, w_down, "nt", BF16, tag + "_down_dx", scale=0.5)
    dgu = _swiglu_bwd(gu, da)
    d_wgu = _mm(n, dgu, "tn", F32, tag + "_gu_dw")
    dn = _mm(dgu, w_gu, "nt", BF16, tag + "_gu_dx")
    dh, dhb, dgain = _rmsnorm_bwd(h, gain, dn, dh, tag + "_norm_bwd")
    return dh, dhb, d_wgu, d_wdown, dgain


def _layer_fwd(h0, p_i, w, norms, lay, tables):
    h1, s_ffn1 = _ffn_fwd(h0, norms["ffn1_norm"], w["ffn1_w_gu"], w["ffn1_w_down"], "ffn1")
    n2 = _rmsnorm_fwd(h1, norms["mix_norm"], "mix_norm_fwd")
    proj = _mm(n2, w["w_in"], "nn", BF16, "in_fwd")
    cb = _conv_fwd(proj, w["conv_w"])
    ya = _mm(cb, w["w_conv_out"], "nn", BF16, "conv_out_fwd")
    qn, kvn = _qkvnorm_fwd(proj, lay, norms["q_norm"], norms["kv_norm"])
    qf = _mm(qn, w["w_uq"], "nn", BF16, "uq_fwd")
    kv = _mm(kvn, w["w_ukv"], "nn", BF16, "ukv_fwd")
    qr, kf = _rope_fwd(qf, kv, proj, lay, tables)
    o, lse = _attn_fwd(qr, kf, kv)
    yb = _mm(o, w["w_mla_out"], "nn", BF16, "mla_out_fwd")
    mg = _merge_fwd(proj, lay, ya, yb)
    h2 = _mm(mg, w["w_o"], "nn", F32, "o_fwd", res=h1)
    h3, s_ffn2 = _ffn_fwd(h2, norms["ffn2_norm"], w["ffn2_w_gu"], w["ffn2_w_down"], "ffn2")
    n4 = _rmsnorm_fwd(h3, norms["ple_norm"], "ple_norm_fwd")
    gp = _mm(n4, w["w_ple_gate"], "nn", BF16, "ple_gate_fwd")
    pp = _mm(p_i, w["w_ple_proj"], "nn", BF16, "ple_proj_fwd")
    h4 = _ple_fwd(h3, gp, pp)
    saved = dict(s_ffn1=s_ffn1, h1=h1, n2=n2, proj=proj, cb=cb, ya=ya, qn=qn, kvn=kvn, qr=qr, kf=kf, kv=kv, o=o,
                 lse=lse, yb=yb, mg=mg, h2=h2, s_ffn2=s_ffn2, h3=h3, n4=n4, gp=gp, pp=pp, p=p_i)
    return h4, saved


def _layer_bwd(dh, s, w, norms, lay, tables):
    gw, gn = {}, {}
    dpp, dgp = _ple_bwd(dh, s["gp"], s["pp"])
    gw["w_ple_proj"] = _mm(s["p"], dpp, "tn", F32, "ple_proj_dw")
    gw["w_ple_gate"] = _mm(s["n4"], dgp, "tn", F32, "ple_gate_dw")
    dn4 = _mm(dgp, w["w_ple_gate"], "nt", BF16, "ple_gate_dx")
    dh, dhb, gn["ple_norm"] = _rmsnorm_bwd(s["h3"], norms["ple_norm"], dn4, dh, "ple_norm_bwd")
    dh, dhb, gw["ffn2_w_gu"], gw["ffn2_w_down"], gn["ffn2_norm"] = _ffn_bwd(
        dh, dhb, s["s_ffn2"], norms["ffn2_norm"], w["ffn2_w_gu"], w["ffn2_w_down"], "ffn2")
    gw["w_o"] = _mm(s["mg"], dhb, "tn", F32, "o_dw")
    dmg = _mm(dhb, w["w_o"], "nt", BF16, "o_dx")
    dya, dyb, dgc, dgm = _merge_bwd(s["proj"], lay, s["ya"], s["yb"], dmg)
    gw["w_conv_out"] = _mm(s["cb"], dya, "tn", F32, "conv_out_dw")
    dcb = _mm(dya, w["w_conv_out"], "nt", BF16, "conv_out_dx")
    db, dc, dv_conv, gw["conv_w"] = _conv_bwd(s["proj"], w["conv_w"], dcb)
    gw["w_mla_out"] = _mm(s["o"], dyb, "tn", F32, "mla_out_dw")
    do = _mm(dyb, w["w_mla_out"], "nt", BF16, "mla_out_dx")
    delta = _attn_delta(do, s["o"])
    dkf, dv, dqr = _attn_bwd(s["qr"], s["kf"], s["kv"], do, s["lse"], delta)
    dqf, dkv, dkr = _rope_bwd(dqr, dkf, dv, tables)
    g_uq = _mm(s["qn"], dqf, "tn", F32, "uq_dw")
    dqn = _mm(dqf, w["w_uq"], "nt", BF16, "uq_dx")
    g_ukv = _mm(s["kvn"], dkv, "tn", F32, "ukv_dw")
    dkvn = _mm(dkv, w["w_ukv"], "nt", BF16, "ukv_dx")
    dqc, dkvc, gn["q_norm"], gn["kv_norm"] = _qkvnorm_bwd(s["proj"], lay, norms["q_norm"], norms["kv_norm"], dqn, dkvn)
    t = dh.shape[0]
    dproj = jnp.concatenate([db, dc, dv_conv, dqc, dkr, dgc, dgm, dkvc,
                             jnp.zeros((t, lay["width"] - lay["kv"] - lay["kvl"]), BF16)], axis=1)
    g_in = _mm(s["n2"], dproj, "tn", F32, "in_dw")
    dn2 = _mm(dproj, w["w_in"], "nt", BF16, "in_dx")
    dh, dhb, gn["mix_norm"] = _rmsnorm_bwd(s["h1"], norms["mix_norm"], dn2, dh, "mix_norm_bwd")
    dh, dhb, gw["ffn1_w_gu"], gw["ffn1_w_down"], gn["ffn1_norm"] = _ffn_bwd(
        dh, dhb, s["s_ffn1"], norms["ffn1_norm"], w["ffn1_w_gu"], w["ffn1_w_down"], "ffn1")
    gw["w_in"] = _w_in_from_layout(g_in, lay)
    gw["w_uq"] = _w_uq_from_layout(g_uq)
    gw["w_ukv"] = _w_ukv_from_layout(g_ukv)
    return dh, gw, gn


def _rope_tables(positions):
    half = ROPE_DIM // 2
    inv_freq = ROPE_THETA ** (-jnp.arange(0, ROPE_DIM, 2, dtype=F32) / ROPE_DIM)
    ang = positions.astype(F32)[:, None] * inv_freq
    cos, sin = jnp.cos(ang), jnp.sin(ang)
    zeros = jnp.zeros_like(cos)
    cos_t = jnp.concatenate([cos, cos, zeros, zeros], axis=1)
    sin_a = jnp.concatenate([-sin, zeros, zeros, zeros], axis=1)
    sin_b = jnp.concatenate([zeros, sin, zeros, zeros], axis=1)
    assert cos_t.shape[1] == LANES and half * 4 == LANES
    return cos_t, sin_a, sin_b


def kernel(x, p, positions, ffn1_norm, ffn1_w_gu, ffn1_w_down, mix_norm, w_in, conv_w, w_conv_out, q_norm, kv_norm, w_uq, w_ukv, w_mla_out, w_o, ffn2_norm, ffn2_w_gu, ffn2_w_down, ple_norm, w_ple_gate, w_ple_proj, final_norm, loss_target, m_ffn1_norm, m_ffn1_w_gu, m_ffn1_w_down, m_mix_norm, m_w_in, m_conv_w, m_w_conv_out, m_q_norm, m_kv_norm, m_w_uq, m_w_ukv, m_w_mla_out, m_w_o, m_ffn2_norm, m_ffn2_w_gu, m_ffn2_w_down, m_ple_norm, m_w_ple_gate, m_w_ple_proj, m_final_norm, v_ffn1_norm, v_ffn1_w_gu, v_ffn1_w_down, v_mix_norm, v_w_in, v_conv_w, v_w_conv_out, v_q_norm, v_kv_norm, v_w_uq, v_w_ukv, v_w_mla_out, v_w_o, v_ffn2_norm, v_ffn2_w_gu, v_ffn2_w_down, v_ple_norm, v_w_ple_gate, v_w_ple_proj, v_final_norm):
    args = dict(zip(ARG_NAMES, (x, p, positions, ffn1_norm, ffn1_w_gu, ffn1_w_down, mix_norm, w_in, conv_w, w_conv_out, q_norm, kv_norm, w_uq, w_ukv, w_mla_out, w_o, ffn2_norm, ffn2_w_gu, ffn2_w_down, ple_norm, w_ple_gate, w_ple_proj, final_norm, loss_target, m_ffn1_norm, m_ffn1_w_gu, m_ffn1_w_down, m_mix_norm, m_w_in, m_conv_w, m_w_conv_out, m_q_norm, m_kv_norm, m_w_uq, m_w_ukv, m_w_mla_out, m_w_o, m_ffn2_norm, m_ffn2_w_gu, m_ffn2_w_down, m_ple_norm, m_w_ple_gate, m_w_ple_proj, m_final_norm, v_ffn1_norm, v_ffn1_w_gu, v_ffn1_w_down, v_mix_norm, v_w_in, v_conv_w, v_w_conv_out, v_q_norm, v_kv_norm, v_w_uq, v_w_ukv, v_w_mla_out, v_w_o, v_ffn2_norm, v_ffn2_w_gu, v_ffn2_w_down, v_ple_norm, v_w_ple_gate, v_w_ple_proj, v_final_norm)))
    depth = ffn1_norm.shape[0]
    d = x.shape[-1]
    lay = _in_layout(conv_w.shape[-1] * 4, q_norm.shape[-1], kv_norm.shape[-1], d)
    shard_shapes = {name: tuple(args[name].shape[1:]) for name, _ in SHARDED}
    c_idx = lax.axis_index("c").astype(jnp.int32).reshape(1)
    tables = _rope_tables(positions[0])

    weights = []
    for i in range(depth):
        full = _gather_layer_weights({name: args[name][i] for name, _ in SHARDED}, shard_shapes)
        full["w_in"] = _w_in_to_layout(full["w_in"], lay)
        full["w_uq"] = _w_uq_to_layout(full["w_uq"])
        full["w_ukv"] = _w_ukv_to_layout(full["w_ukv"])
        full["conv_w"] = full["conv_w"].astype(F32)
        weights.append(full)
    norms = [{name: args[name][i] for name in REPLICATED} for i in range(depth)]

    h = x[0]
    saved = []
    for i in range(depth):
        h, s = _layer_fwd(h, p[i, 0], weights[i], norms[i], lay, tables)
        saved.append(s)
    loss_part, dh, _, g_final = _loss_head(h, final_norm, loss_target[0])
    loss = lax.psum(loss_part[0, 0], ("x", "y", "c"))

    shard_grads, norm_grads = [None] * depth, [None] * depth
    for i in reversed(range(depth)):
        dh, gw, gn = _layer_bwd(dh, saved[i], weights[i], norms[i], lay, tables)
        shard_grads[i] = _reduce_scatter_layer_grads(gw, shard_shapes, c_idx)
        norm_grads[i] = gn
    grad_x = dh[None]

    pieces = [norm_grads[i][name].reshape(-1) for i in range(depth) for name in REPLICATED] + [g_final.reshape(-1)]
    vec = jnp.concatenate(pieces)
    n_vec = vec.shape[0]
    rows = -(-n_vec // (8 * LANES)) * 8
    vec = _all_sum_small(jnp.pad(vec, (0, rows * LANES - n_vec)).reshape(rows, LANES)).reshape(-1)
    grads, off = {}, 0
    for name in REPLICATED:
        grads[name] = []
    for i in range(depth):
        for name in REPLICATED:
            size = args[name].shape[1]
            grads[name].append(vec[off:off + size])
            off += size
    for name in REPLICATED:
        grads[name] = jnp.stack(grads[name])
    grads["final_norm"] = vec[off:off + d]
    for name, _ in SHARDED:
        grads[name] = jnp.stack([shard_grads[i][name] for i in range(depth)])

    delta, new_m, new_v = {}, {}, {}
    for name in WEIGHTS:
        w_, g_, m_, v_ = args[name], grads[name], args["m_" + name], args["v_" + name]
        if w_.ndim == 1:
            outs = _adamw(w_[None], g_[None], m_[None], v_[None])
            delta[name], new_m[name], new_v[name] = (o[0] for o in outs)
        else:
            delta[name], new_m[name], new_v[name] = _adamw(w_, g_, m_, v_)
    return (loss, grad_x, *[grads[n] for n in WEIGHTS], *[delta[n] for n in WEIGHTS],
            *[new_m[n] for n in WEIGHTS], *[new_v[n] for n in WEIGHTS])
```

```python
import functools

import jax
import jax.numpy as jnp
from jax import lax
from jax.experimental import pallas as pl
from jax.experimental.pallas import tpu as pltpu

BF16 = jnp.bfloat16
F32 = jnp.float32
SDS = jax.ShapeDtypeStruct
MESH = pl.DeviceIdType.MESH

N_HEADS = 8
NOPE_DIM = 128
ROPE_DIM = 64
V_DIM = 128
QK_PAD = 256
CHUNK = 64
ROPE_THETA = 10000.0
EPS = 1e-6
ATTN_SCALE = (NOPE_DIM + ROPE_DIM) ** -0.5
NEG_BIG = -1e30

ADAM_LR = 0.001
ADAM_B1 = 0.9
ADAM_B2 = 0.999
ADAM_EPS = 1e-08
ADAM_WD = 0.01
ADAM_STEP = 10

LANES = 128
N_CHIPS = 4
VMEM_LIMIT_BYTES = 56 * 1024 * 1024
ACC_BYTES = 6 * 1024 * 1024
BLOCK_ELEMS = 1 << 19

SHARDED = (("ffn1_w_gu", 1), ("ffn1_w_down", 0), ("w_in", 1), ("w_conv_out", 1), ("w_uq", 1), ("w_ukv", 1),
           ("w_mla_out", 0), ("w_o", 0), ("ffn2_w_gu", 1), ("ffn2_w_down", 0), ("w_ple_gate", 0), ("w_ple_proj", 1))
BIG = tuple(name for name, _ in SHARDED)
REPLICATED = ("ffn1_norm", "mix_norm", "q_norm", "kv_norm", "ffn2_norm", "ple_norm")
WEIGHTS = ("ffn1_norm", "ffn1_w_gu", "ffn1_w_down", "mix_norm", "w_in", "conv_w", "w_conv_out", "q_norm",
           "kv_norm", "w_uq", "w_ukv", "w_mla_out", "w_o", "ffn2_norm", "ffn2_w_gu", "ffn2_w_down",
           "ple_norm", "w_ple_gate", "w_ple_proj", "final_norm")
ARG_NAMES = ("x", "p", "positions") + WEIGHTS + ("loss_target",) + tuple("m_" + n for n in WEIGHTS) + tuple(
    "v_" + n for n in WEIGHTS)


def _cparams(semantics=None):
    return pltpu.CompilerParams(dimension_semantics=semantics, vmem_limit_bytes=VMEM_LIMIT_BYTES)


def _tile(n, cap, mult=LANES):
    best = None
    for t in range(mult, min(n, cap) + 1, mult):
        if n % t == 0:
            best = t
    return n if best is None else best


def _sigmoid(x):
    return 1.0 / (1.0 + jnp.exp(-x))


def _rowspec(tm, width, col_block=0):
    return pl.BlockSpec((tm, width), lambda i: (i, col_block))


def _colspec(tm, width, offset):
    assert offset % width == 0, (width, offset)
    return _rowspec(tm, width, offset // width)


def _mm(a, b, mode, out_dtype, name, scale=None, res=None, layer=None, b_chip=False, out_chip=False):
    bshape = b.shape if layer is None else b.shape[1:]
    if b_chip:
        bshape = (bshape[1], N_CHIPS * bshape[2])
    if mode == "nn":
        (m, k), (k2, n) = a.shape, bshape
    elif mode == "nt":
        (m, k), (n, k2) = a.shape, bshape
    else:
        (k, m), (k2, n) = a.shape, bshape
    assert k == k2, (a.shape, b.shape, mode)
    n_unit = n // N_CHIPS if (out_chip or (b_chip and mode == "nn")) else n
    k_unit = k // N_CHIPS if (b_chip and mode == "nt") else k
    tn = _tile(n_unit, 1536)
    tm = _tile(m, min(1408, ACC_BYTES // (4 * tn)))
    tk = _tile(k_unit, 1536)
    nk = k // tk
    n_per, k_per = n_unit // tn, k_unit // tk
    dims = {"nn": (((1,), (0,)), ((), ())), "nt": (((1,), (1,)), ((), ())), "tn": (((0,), (0,)), ((), ()))}[mode]

    def body(*refs):
        a_ref, b_ref = refs[0], refs[1]
        res_ref = refs[2] if res is not None else None
        o_ref = refs[3] if res is not None else refs[2]
        acc_ref = refs[-1] if nk > 1 else None

        def finish(acc):
            if scale is not None:
                acc = acc * scale
            if res_ref is not None:
                acc = res_ref[...] + acc
            o_ref[...] = acc.astype(out_dtype)

        part = lax.dot_general(a_ref[...].astype(BF16), b_ref[...].astype(BF16), dims,
                               preferred_element_type=F32)
        if nk == 1:
            finish(part)
        else:
            kk = pl.program_id(2)

            @pl.when(kk == 0)
            def _():
                acc_ref[...] = part

            @pl.when(kk > 0)
            def _():
                acc_ref[...] += part

            @pl.when(kk == nk - 1)
            def _():
                finish(acc_ref[...])

    lead = () if layer is None else (layer,)
    lead_block = () if layer is None else (None,)
    if mode == "nn":
        a_spec = pl.BlockSpec((tm, tk), lambda i, j, kk: (i, kk))
        if b_chip:
            b_spec = pl.BlockSpec(lead_block + (None, tk, tn), lambda i, j, kk: lead + (j // n_per, kk, j % n_per))
        else:
            b_spec = pl.BlockSpec(lead_block + (tk, tn), lambda i, j, kk: lead + (kk, j))
    elif mode == "nt":
        a_spec = pl.BlockSpec((tm, tk), lambda i, j, kk: (i, kk))
        if b_chip:
            b_spec = pl.BlockSpec(lead_block + (None, tn, tk), lambda i, j, kk: lead + (kk // k_per, j, kk % k_per))
        else:
            b_spec = pl.BlockSpec(lead_block + (tn, tk), lambda i, j, kk: lead + (j, kk))
    else:
        assert layer is None and not b_chip
        a_spec = pl.BlockSpec((tk, tm), lambda i, j, kk: (kk, i))
        b_spec = pl.BlockSpec((tk, tn), lambda i, j, kk: (kk, j))
    if out_chip:
        o_spec = pl.BlockSpec((None, tm, tn), lambda i, j, kk: (j // n_per, i, j % n_per))
        out_shape = SDS((N_CHIPS, m, n_unit), out_dtype)
    else:
        o_spec = pl.BlockSpec((tm, tn), lambda i, j, kk: (i, j))
        out_shape = SDS((m, n), out_dtype)
    in_specs = [a_spec, b_spec] + ([o_spec] if res is not None else [])
    operands = (a, b) + ((res,) if res is not None else ())
    return pl.pallas_call(
        body, out_shape=out_shape, grid=(m // tm, n // tn, nk), in_specs=in_specs, out_specs=o_spec,
        scratch_shapes=[pltpu.VMEM((tm, tn), F32)] if nk > 1 else [],
        compiler_params=_cparams(("parallel", "parallel", "arbitrary")), name=name)(*operands)


def _rn_fwd_math(x, g):
    r = lax.rsqrt(jnp.mean(x * x, axis=-1, keepdims=True) + EPS)
    return (x * r) * g


def _rn_bwd_math(x, g, dn):
    r = lax.rsqrt(jnp.mean(x * x, axis=-1, keepdims=True) + EPS)
    xh = x * r
    gy = dn * g
    dx = r * (gy - xh * jnp.mean(gy * xh, axis=-1, keepdims=True))
    dg = jnp.sum(dn * xh, axis=0, keepdims=True)
    return dx, dg


def _rmsnorm_fwd(h, gain, name):
    t, d = h.shape
    tm = _tile(t, 512, 8)

    def body(h_ref, g_ref, o_ref):
        o_ref[...] = _rn_fwd_math(h_ref[...], g_ref[...]).astype(BF16)

    return pl.pallas_call(
        body, out_shape=SDS((t, d), BF16), grid=(t // tm,),
        in_specs=[_rowspec(tm, d), pl.BlockSpec((1, d), lambda i: (0, 0))], out_specs=_rowspec(tm, d),
        compiler_params=_cparams(("parallel",)), name=name)(h, gain.reshape(1, d))


def _rmsnorm_bwd(h, gain, dn, dh_in, name):
    t, d = h.shape
    tm = _tile(t, 512, 8)

    def body(h_ref, g_ref, dn_ref, dhin_ref, dh_ref, dhb_ref, dg_ref):
        dx, dg = _rn_bwd_math(h_ref[...], g_ref[...], dn_ref[...].astype(F32))
        dh = dhin_ref[...] + dx
        dh_ref[...] = dh
        dhb_ref[...] = dh.astype(BF16)

        @pl.when(pl.program_id(0) == 0)
        def _():
            dg_ref[...] = dg

        @pl.when(pl.program_id(0) > 0)
        def _():
            dg_ref[...] += dg

    vec = pl.BlockSpec((1, d), lambda i: (0, 0))
    return pl.pallas_call(
        body, out_shape=(SDS((t, d), F32), SDS((t, d), BF16), SDS((1, d), F32)), grid=(t // tm,),
        in_specs=[_rowspec(tm, d), vec, _rowspec(tm, d), _rowspec(tm, d)],
        out_specs=(_rowspec(tm, d), _rowspec(tm, d), vec),
        compiler_params=_cparams(("arbitrary",)), name=name)(h, gain.reshape(1, d), dn, dh_in)


def _loss_head(h, gain, target):
    t, d = h.shape
    tm = _tile(t, 512, 8)

    def body(h_ref, g_ref, t_ref, loss_ref, dh_ref, dhb_ref, dg_ref):
        x, g = h_ref[...], g_ref[...]
        err = _rn_fwd_math(x, g) - t_ref[...]
        part = 0.5 * jnp.sum(jnp.sum(err * err, axis=1, keepdims=True), axis=0, keepdims=True) * (1.0 / d)
        dx, dg = _rn_bwd_math(x, g, err * (1.0 / d))
        dh_ref[...] = dx
        dhb_ref[...] = dx.astype(BF16)

        @pl.when(pl.program_id(0) == 0)
        def _():
            dg_ref[...] = dg
            loss_ref[...] = jnp.broadcast_to(part, (1, LANES))

        @pl.when(pl.program_id(0) > 0)
        def _():
            dg_ref[...] += dg
            loss_ref[...] += jnp.broadcast_to(part, (1, LANES))

    vec = pl.BlockSpec((1, d), lambda i: (0, 0))
    return pl.pallas_call(
        body, out_shape=(SDS((1, LANES), F32), SDS((t, d), F32), SDS((t, d), BF16), SDS((1, d), F32)),
        grid=(t // tm,), in_specs=[_rowspec(tm, d), vec, _rowspec(tm, d)],
        out_specs=(pl.BlockSpec((1, LANES), lambda i: (0, 0)), _rowspec(tm, d), _rowspec(tm, d), vec),
        compiler_params=_cparams(("arbitrary",)), name="loss_head")(h, gain.reshape(1, d), target)


def _swiglu_fwd(gu):
    t, f2 = gu.shape
    f = f2 // 2
    tm = _tile(t, 256, 8)

    def body(gu_ref, a_ref):
        g = gu_ref[:, :f].astype(F32)
        u = gu_ref[:, f:].astype(F32)
        a_ref[...] = (g * _sigmoid(g) * u).astype(BF16)

    return pl.pallas_call(
        body, out_shape=SDS((t, f), BF16), grid=(t // tm,), in_specs=[_rowspec(tm, f2)],
        out_specs=_rowspec(tm, f), compiler_params=_cparams(("parallel",)), name="swiglu_fwd")(gu)


def _swiglu_bwd(gu, da):
    t, f2 = gu.shape
    f = f2 // 2
    tm = _tile(t, 256, 8)

    def body(gu_ref, da_ref, dgu_ref):
        g = gu_ref[:, :f].astype(F32)
        u = gu_ref[:, f:].astype(F32)
        da_ = da_ref[...].astype(F32)
        s = _sigmoid(g)
        dgu_ref[:, :f] = (da_ * u * (s * (1.0 + g * (1.0 - s)))).astype(BF16)
        dgu_ref[:, f:] = (da_ * (g * s)).astype(BF16)

    return pl.pallas_call(
        body, out_shape=SDS((t, f2), BF16), grid=(t // tm,), in_specs=[_rowspec(tm, f2), _rowspec(tm, f)],
        out_specs=_rowspec(tm, f2), compiler_params=_cparams(("parallel",)), name="swiglu_bwd")(gu, da)


def _shift_down(z, k, row):
    return jnp.where(row >= k, pltpu.roll(z, k, 0), 0.0)


def _shift_up(z, k, row, t):
    return jnp.where(row < t - k, pltpu.roll(z, t - k, 0), 0.0)


def _conv_specs(t, conv):
    nb = conv // LANES
    return [pl.BlockSpec((t, LANES), lambda j: (0, j)), pl.BlockSpec((t, LANES), lambda j: (0, nb + j)),
            pl.BlockSpec((t, LANES), lambda j: (0, 2 * nb + j))]


def _conv_fwd(proj, conv_w):
    t = proj.shape[0]
    conv = conv_w.shape[1]

    def body(b_ref, c_ref, v_ref, w_ref, o_ref):
        z = c_ref[...].astype(F32) * v_ref[...].astype(F32)
        row = lax.broadcasted_iota(jnp.int32, z.shape, 0)
        y = w_ref[0:1, :] * _shift_down(z, 2, row) + w_ref[1:2, :] * _shift_down(z, 1, row) + w_ref[2:3, :] * z
        o_ref[...] = (b_ref[...].astype(F32) * y).astype(BF16)

    cspec = pl.BlockSpec((t, LANES), lambda j: (0, j))
    return pl.pallas_call(
        body, out_shape=SDS((t, conv), BF16), grid=(conv // LANES,),
        in_specs=_conv_specs(t, conv) + [pl.BlockSpec((3, LANES), lambda j: (0, j))], out_specs=cspec,
        compiler_params=_cparams(("parallel",)), name="conv_fwd")(proj, proj, proj, conv_w)


def _conv_bwd(proj, conv_w, dcb):
    t = proj.shape[0]
    conv = conv_w.shape[1]

    def body(b_ref, c_ref, v_ref, w_ref, d_ref, db_ref, dc_ref, dv_ref, dw_ref):
        b, c, v = b_ref[...].astype(F32), c_ref[...].astype(F32), v_ref[...].astype(F32)
        d = d_ref[...].astype(F32)
        z = c * v
        row = lax.broadcasted_iota(jnp.int32, z.shape, 0)
        z1, z2 = _shift_down(z, 1, row), _shift_down(z, 2, row)
        w0, w1, w2 = w_ref[0:1, :], w_ref[1:2, :], w_ref[2:3, :]
        y = w0 * z2 + w1 * z1 + w2 * z
        dy = d * b
        db_ref[...] = (d * y).astype(BF16)
        dz = w2 * dy + w1 * _shift_up(dy, 1, row, t) + w0 * _shift_up(dy, 2, row, t)
        dc_ref[...] = (dz * v).astype(BF16)
        dv_ref[...] = (dz * c).astype(BF16)
        dw_ref[0:1, :] = jnp.sum(dy * z2, axis=0, keepdims=True)
        dw_ref[1:2, :] = jnp.sum(dy * z1, axis=0, keepdims=True)
        dw_ref[2:3, :] = jnp.sum(dy * z, axis=0, keepdims=True)

    cspec = pl.BlockSpec((t, LANES), lambda j: (0, j))
    wspec = pl.BlockSpec((3, LANES), lambda j: (0, j))
    return pl.pallas_call(
        body, out_shape=(SDS((t, conv), BF16),) * 3 + (SDS((3, conv), F32),), grid=(conv // LANES,),
        in_specs=_conv_specs(t, conv) + [wspec, cspec], out_specs=(cspec, cspec, cspec, wspec),
        compiler_params=_cparams(("parallel",)), name="conv_bwd")(proj, proj, proj, conv_w, dcb)


def _qkvnorm_fwd(proj, lay, q_gain, kv_gain):
    t = proj.shape[0]
    ql, kvl = lay["ql"], lay["kvl"]
    tm = _tile(t, 512, 8)

    def body(q_ref, kv_ref, gq_ref, gkv_ref, qn_ref, kvn_ref):
        qn_ref[...] = _rn_fwd_math(q_ref[...].astype(F32), gq_ref[...]).astype(BF16)
        kvn_ref[...] = _rn_fwd_math(kv_ref[...].astype(F32), gkv_ref[...]).astype(BF16)

    return pl.pallas_call(
        body, out_shape=(SDS((t, ql), BF16), SDS((t, kvl), BF16)), grid=(t // tm,),
        in_specs=[_colspec(tm, ql, lay["q"]), _colspec(tm, kvl, lay["kv"]),
                  pl.BlockSpec((1, ql), lambda i: (0, 0)), pl.BlockSpec((1, kvl), lambda i: (0, 0))],
        out_specs=(_rowspec(tm, ql), _rowspec(tm, kvl)), compiler_params=_cparams(("parallel",)),
        name="qkvnorm_fwd")(proj, proj, q_gain.reshape(1, ql), kv_gain.reshape(1, kvl))


def _qkvnorm_bwd(proj, lay, q_gain, kv_gain, dqn, dkvn):
    t = proj.shape[0]
    ql, kvl = lay["ql"], lay["kvl"]
    tm = _tile(t, 512, 8)

    def body(q_ref, kv_ref, gq_ref, gkv_ref, dqn_ref, dkvn_ref, dq_ref, dkv_ref, dgq_ref, dgkv_ref):
        dq, dgq = _rn_bwd_math(q_ref[...].astype(F32), gq_ref[...], dqn_ref[...].astype(F32))
        dkv, dgkv = _rn_bwd_math(kv_ref[...].astype(F32), gkv_ref[...], dkvn_ref[...].astype(F32))
        dq_ref[...] = dq.astype(BF16)
        dkv_ref[...] = dkv.astype(BF16)

        @pl.when(pl.program_id(0) == 0)
        def _():
            dgq_ref[...] = dgq
            dgkv_ref[...] = dgkv

        @pl.when(pl.program_id(0) > 0)
        def _():
            dgq_ref[...] += dgq
            dgkv_ref[...] += dgkv

    vq = pl.BlockSpec((1, ql), lambda i: (0, 0))
    vkv = pl.BlockSpec((1, kvl), lambda i: (0, 0))
    return pl.pallas_call(
        body, out_shape=(SDS((t, ql), BF16), SDS((t, kvl), BF16), SDS((1, ql), F32), SDS((1, kvl), F32)),
        grid=(t // tm,),
        in_specs=[_colspec(tm, ql, lay["q"]), _colspec(tm, kvl, lay["kv"]), vq, vkv, _rowspec(tm, ql),
                  _rowspec(tm, kvl)],
        out_specs=(_rowspec(tm, ql), _rowspec(tm, kvl), vq, vkv), compiler_params=_cparams(("arbitrary",)),
        name="qkvnorm_bwd")(proj, proj, q_gain.reshape(1, ql), kv_gain.reshape(1, kvl), dqn, dkvn)


def _rope(x, cos_t, sin_a, sin_b):
    return x * cos_t + pltpu.roll(x, LANES - ROPE_DIM // 2, 1) * sin_a + pltpu.roll(x, ROPE_DIM // 2, 1) * sin_b


def _rope_fwd(qf, kv, proj, lay, tables):
    t = qf.shape[0]
    tm = _tile(t, 256, 8)
    hq = N_HEADS * QK_PAD

    def body(q_ref, kn_ref, kr_ref, cos_ref, sa_ref, sb_ref, qr_ref, kf_ref):
        cos_t, sin_a, sin_b = cos_ref[...], sa_ref[...], sb_ref[...]
        kr = _rope(kr_ref[...].astype(F32), cos_t, sin_a, sin_b).astype(BF16)
        for h in range(N_HEADS):
            lo = h * QK_PAD
            qr_ref[:, lo:lo + NOPE_DIM] = q_ref[:, lo:lo + NOPE_DIM]
            qr_ref[:, lo + NOPE_DIM:lo + QK_PAD] = _rope(
                q_ref[:, lo + NOPE_DIM:lo + QK_PAD].astype(F32), cos_t, sin_a, sin_b).astype(BF16)
            kf_ref[:, lo:lo + NOPE_DIM] = kn_ref[:, h * NOPE_DIM:(h + 1) * NOPE_DIM]
            kf_ref[:, lo + NOPE_DIM:lo + QK_PAD] = kr

    tab = _rowspec(tm, LANES)
    return pl.pallas_call(
        body, out_shape=(SDS((t, hq), BF16), SDS((t, hq), BF16)), grid=(t // tm,),
        in_specs=[_rowspec(tm, hq), _rowspec(tm, N_HEADS * NOPE_DIM), _colspec(tm, LANES, lay["kr"]), tab, tab, tab],
        out_specs=(_rowspec(tm, hq), _rowspec(tm, hq)), compiler_params=_cparams(("parallel",)),
        name="rope_fwd")(qf, kv, proj, *tables)


def _rope_bwd(dqr, dkf, dv, tables):
    t = dqr.shape[0]
    tm = _tile(t, 256, 8)
    hq = N_HEADS * QK_PAD
    hn = N_HEADS * NOPE_DIM

    def body(dq_ref, dk_ref, dv_ref, cos_ref, sa_ref, sb_ref, dqf_ref, dkv_ref, dkr_ref):
        cos_t, sin_a, sin_b = cos_ref[...], -sa_ref[...], -sb_ref[...]
        dkr = jnp.zeros((tm, LANES), F32)
        for h in range(N_HEADS):
            lo = h * QK_PAD
            dqf_ref[:, lo:lo + NOPE_DIM] = dq_ref[:, lo:lo + NOPE_DIM].astype(BF16)
            dqf_ref[:, lo + NOPE_DIM:lo + QK_PAD] = _rope(
                dq_ref[:, lo + NOPE_DIM:lo + QK_PAD].astype(F32), cos_t, sin_a, sin_b).astype(BF16)
            dkv_ref[:, h * NOPE_DIM:(h + 1) * NOPE_DIM] = dk_ref[:, lo:lo + NOPE_DIM]
            dkr = dkr + dk_ref[:, lo + NOPE_DIM:lo + QK_PAD].astype(F32)
        dkv_ref[:, hn:] = dv_ref[...]
        dkr_ref[...] = _rope(dkr, cos_t, sin_a, sin_b).astype(BF16)

    tab = _rowspec(tm, LANES)
    return pl.pallas_call(
        body, out_shape=(SDS((t, hq), BF16), SDS((t, 2 * hn), BF16), SDS((t, LANES), BF16)), grid=(t // tm,),
        in_specs=[_rowspec(tm, hq), _rowspec(tm, hq), _rowspec(tm, hn), tab, tab, tab],
        out_specs=(_rowspec(tm, hq), _rowspec(tm, 2 * hn), tab), compiler_params=_cparams(("parallel",)),
        name="rope_bwd")(dqr, dkf, dv, *tables)


def _chunk_mask(bq):
    rows = lax.broadcasted_iota(jnp.int32, (bq, bq), 0) // CHUNK
    cols = lax.broadcasted_iota(jnp.int32, (bq, bq), 1) // CHUNK
    return cols <= rows


_NT = (((1,), (1,)), ((), ()))
_TN = (((0,), (0,)), ((), ()))


def _attn_block(t):
    return 512 if t >= 2048 else 128


def _attn_fwd(qr, kf, kv):
    t = qr.shape[0]
    bq = _attn_block(t)
    nq = t // bq

    def body(q_ref, k_ref, v_ref, o_ref, lse_ref):
        i = pl.program_id(1)
        q = q_ref[...]

        def block(j, carry, masked):
            m, l, acc = carry
            off = pl.multiple_of(j * bq, bq)
            k = k_ref[pl.ds(off, bq), :]
            v = v_ref[pl.ds(off, bq), :]
            s = lax.dot_general(q, k, _NT, preferred_element_type=F32) * ATTN_SCALE
            if masked:
                s = jnp.where(_chunk_mask(bq), s, NEG_BIG)
            m_new = jnp.maximum(m, jnp.max(s, axis=1, keepdims=True))
            alpha = jnp.exp(m - m_new)
            pr = jnp.exp(s - m_new)
            l = alpha * l + jnp.sum(pr, axis=1, keepdims=True)
            acc = alpha * acc + jnp.dot(pr.astype(BF16), v, preferred_element_type=F32)
            return m_new, l, acc

        init = (jnp.full((bq, 1), NEG_BIG, F32), jnp.zeros((bq, 1), F32), jnp.zeros((bq, V_DIM), F32))
        carry = lax.fori_loop(0, i, lambda j, c: block(j, c, False), init)
        m, l, acc = block(i, carry, True)
        o_ref[...] = (acc / l).astype(BF16)
        lse_ref[0] = jnp.broadcast_to(m + jnp.log(l), (bq, LANES))

    return pl.pallas_call(
        body, out_shape=(SDS((t, N_HEADS * V_DIM), BF16), SDS((N_HEADS, t, LANES), F32)), grid=(N_HEADS, nq),
        in_specs=[pl.BlockSpec((bq, QK_PAD), lambda h, i: (i, h)), pl.BlockSpec((t, QK_PAD), lambda h, i: (0, h)),
                  pl.BlockSpec((t, V_DIM), lambda h, i: (0, N_HEADS + h))],
        out_specs=(pl.BlockSpec((bq, V_DIM), lambda h, i: (i, h)), pl.BlockSpec((1, bq, LANES), lambda h, i: (h, i, 0))),
        compiler_params=_cparams(("parallel", "parallel")), name="attn_fwd")(qr, kf, kv)


def _attn_delta(do, o):
    t = do.shape[0]
    tm = _tile(t, 512, 8)

    def body(do_ref, o_ref, dl_ref):
        prod = do_ref[...].astype(F32) * o_ref[...].astype(F32)
        for h in range(N_HEADS):
            s = jnp.sum(prod[:, h * V_DIM:(h + 1) * V_DIM], axis=1, keepdims=True)
            dl_ref[h] = jnp.broadcast_to(s, (tm, LANES))

    return pl.pallas_call(
        body, out_shape=SDS((N_HEADS, t, LANES), F32), grid=(t // tm,),
        in_specs=[_rowspec(tm, N_HEADS * V_DIM), _rowspec(tm, N_HEADS * V_DIM)],
        out_specs=pl.BlockSpec((N_HEADS, tm, LANES), lambda i: (0, i, 0)), compiler_params=_cparams(("parallel",)),
        name="attn_delta")(do, o)


def _attn_bwd(qr, kf, kv, do, lse, delta):
    t = qr.shape[0]
    bq = _attn_block(t)
    nq = t // bq

    def body(k_ref, v_ref, q_ref, do_ref, lse_ref, dl_ref, dk_ref, dv_ref, dq_ref):
        j = pl.program_id(1)

        @pl.when(j == 0)
        def _():
            dq_ref[...] = jnp.zeros_like(dq_ref)

        k = k_ref[...]
        v = v_ref[...]

        def block(i, carry, masked):
            dk, dv = carry
            off = pl.multiple_of(i * bq, bq)
            q = q_ref[pl.ds(off, bq), :]
            do_ = do_ref[pl.ds(off, bq), :]
            lse_i = lse_ref[0, pl.ds(off, bq), :][:, :1]
            dl_i = dl_ref[0, pl.ds(off, bq), :][:, :1]
            s = lax.dot_general(q, k, _NT, preferred_element_type=F32) * ATTN_SCALE
            if masked:
                s = jnp.where(_chunk_mask(bq), s, NEG_BIG)
            pr = jnp.exp(s - lse_i)
            dv = dv + lax.dot_general(pr.astype(BF16), do_, _TN, preferred_element_type=F32)
            dp = lax.dot_general(do_, v, _NT, preferred_element_type=F32)
            ds = (pr * (dp - dl_i) * ATTN_SCALE).astype(BF16)
            dk = dk + lax.dot_general(ds, q, _TN, preferred_element_type=F32)
            dq_ref[pl.ds(off, bq), :] += jnp.dot(ds, k, preferred_element_type=F32)
            return dk, dv

        carry = block(j, (jnp.zeros((bq, QK_PAD), F32), jnp.zeros((bq, V_DIM), F32)), True)
        dk, dv = lax.fori_loop(j + 1, nq, lambda i, c: block(i, c, False), carry)
        dk_ref[...] = dk.astype(BF16)
        dv_ref[...] = dv.astype(BF16)

    stat = pl.BlockSpec((1, t, LANES), lambda h, j: (h, 0, 0))
    return pl.pallas_call(
        body,
        out_shape=(SDS((t, N_HEADS * QK_PAD), BF16), SDS((t, N_HEADS * V_DIM), BF16), SDS((t, N_HEADS * QK_PAD), F32)),
        grid=(N_HEADS, nq),
        in_specs=[pl.BlockSpec((bq, QK_PAD), lambda h, j: (j, h)), pl.BlockSpec((bq, V_DIM), lambda h, j: (j, N_HEADS + h)),
                  pl.BlockSpec((t, QK_PAD), lambda h, j: (0, h)), pl.BlockSpec((t, V_DIM), lambda h, j: (0, h)), stat, stat],
        out_specs=(pl.BlockSpec((bq, QK_PAD), lambda h, j: (j, h)), pl.BlockSpec((bq, V_DIM), lambda h, j: (j, h)),
                   pl.BlockSpec((t, QK_PAD), lambda h, j: (0, h))),
        compiler_params=_cparams(("parallel", "arbitrary")), name="attn_bwd")(kf, kv, qr, do, lse, delta)


def _merge_fwd(proj, lay, ya, yb):
    t, d = ya.shape
    tm = _tile(t, 512, 8)

    def body(gc_ref, gm_ref, ya_ref, yb_ref, o_ref):
        o_ref[...] = (_sigmoid(gc_ref[...].astype(F32)) * ya_ref[...].astype(F32)
                      + _sigmoid(gm_ref[...].astype(F32)) * yb_ref[...].astype(F32)).astype(BF16)

    return pl.pallas_call(
        body, out_shape=SDS((t, d), BF16), grid=(t // tm,),
        in_specs=[_colspec(tm, d, lay["gc"]), _colspec(tm, d, lay["gm"]), _rowspec(tm, d), _rowspec(tm, d)],
        out_specs=_rowspec(tm, d), compiler_params=_cparams(("parallel",)), name="merge_fwd")(proj, proj, ya, yb)


def _merge_bwd(proj, lay, ya, yb, dmg):
    t, d = ya.shape
    tm = _tile(t, 512, 8)

    def body(gc_ref, gm_ref, ya_ref, yb_ref, d_ref, dya_ref, dyb_ref, dgc_ref, dgm_ref):
        dm = d_ref[...].astype(F32)
        sc = _sigmoid(gc_ref[...].astype(F32))
        sm = _sigmoid(gm_ref[...].astype(F32))
        dya_ref[...] = (dm * sc).astype(BF16)
        dyb_ref[...] = (dm * sm).astype(BF16)
        dgc_ref[...] = (dm * ya_ref[...].astype(F32) * (sc * (1.0 - sc))).astype(BF16)
        dgm_ref[...] = (dm * yb_ref[...].astype(F32) * (sm * (1.0 - sm))).astype(BF16)

    r = _rowspec(tm, d)
    return pl.pallas_call(
        body, out_shape=(SDS((t, d), BF16),) * 4, grid=(t // tm,),
        in_specs=[_colspec(tm, d, lay["gc"]), _colspec(tm, d, lay["gm"]), r, r, r], out_specs=(r, r, r, r),
        compiler_params=_cparams(("parallel",)), name="merge_bwd")(proj, proj, ya, yb, dmg)


def _ple_fwd(h, gp, pp):
    t, d = h.shape
    tm = _tile(t, 512, 8)

    def body(h_ref, gp_ref, pp_ref, o_ref):
        o_ref[...] = h_ref[...] + _sigmoid(gp_ref[...].astype(F32)) * pp_ref[...].astype(F32)

    r = _rowspec(tm, d)
    return pl.pallas_call(body, out_shape=SDS((t, d), F32), grid=(t // tm,), in_specs=[r, r, r], out_specs=r,
                          compiler_params=_cparams(("parallel",)), name="ple_fwd")(h, gp, pp)


def _ple_bwd(dh, gp, pp):
    t, d = dh.shape
    tm = _tile(t, 512, 8)

    def body(dh_ref, gp_ref, pp_ref, dpp_ref, dgp_ref):
        g = dh_ref[...]
        s = _sigmoid(gp_ref[...].astype(F32))
        dpp_ref[...] = (g * s).astype(BF16)
        dgp_ref[...] = (g * pp_ref[...].astype(F32) * (s * (1.0 - s))).astype(BF16)

    r = _rowspec(tm, d)
    return pl.pallas_call(body, out_shape=(SDS((t, d), BF16),) * 2, grid=(t // tm,), in_specs=[r, r, r],
                          out_specs=(r, r), compiler_params=_cparams(("parallel",)), name="ple_bwd")(dh, gp, pp)


def _adamw(w, g, m, v):
    shape = w.shape
    cols = shape[-1]
    rows = w.size // cols
    tr = _tile(rows, max(8, BLOCK_ELEMS // cols // 8 * 8), 8)

    def body(w_ref, g_ref, m_ref, v_ref, d_ref, nm_ref, nv_ref):
        g_ = g_ref[...]
        nm = ADAM_B1 * m_ref[...] + (1.0 - ADAM_B1) * g_
        nv = ADAM_B2 * v_ref[...] + (1.0 - ADAM_B2) * (g_ * g_)
        m_hat = nm / (1.0 - ADAM_B1 ** ADAM_STEP)
        v_hat = nv / (1.0 - ADAM_B2 ** ADAM_STEP)
        d_ref[...] = -ADAM_LR * (m_hat / (jnp.sqrt(v_hat) + ADAM_EPS) + ADAM_WD * w_ref[...])
        nm_ref[...] = nm
        nv_ref[...] = nv

    r = _rowspec(tr, cols)
    outs = pl.pallas_call(
        body, out_shape=(SDS((rows, cols), F32),) * 3, grid=(rows // tr,), in_specs=[r, r, r, r], out_specs=(r, r, r),
        compiler_params=_cparams(("parallel",)), name="adamw")(*(a.reshape(rows, cols) for a in (w, g, m, v)))
    return tuple(o.reshape(shape) for o in outs)


ANY = pl.BlockSpec(memory_space=pl.ANY)


def _place():
    x, y, c = lax.axis_index("x"), lax.axis_index("y"), lax.axis_index("c")
    return x, y, c, [(1 - x, y), (x, 1 - y), (1 - x, 1 - y)]


def _half_rows(rows, cols):
    half = rows // 2
    return half, _tile(half, max(16, BLOCK_ELEMS // cols // 16 * 16), 16)


def _my_chip():
    return 2 * lax.axis_index("x") + lax.axis_index("y")


def _cast_into_slot(w):
    nl, r, c = w.shape
    tr = _tile(r, max(16, BLOCK_ELEMS // c // 16 * 16), 16)

    def body(w_ref, o_ref):
        o_ref[...] = w_ref[...].astype(BF16)

    return pl.pallas_call(
        body, out_shape=SDS((nl, N_CHIPS, r, c), BF16), grid=(nl, r // tr),
        in_specs=[pl.BlockSpec((None, tr, c), lambda l, i: (l, i, 0))],
        out_specs=pl.BlockSpec((None, None, tr, c), lambda l, i: (l, _my_chip(), i, 0)),
        compiler_params=_cparams(("parallel", "parallel")), name="cast_into_slot")(w)


def _all_gather_weights(bufs):
    nw = len(bufs)

    def body(*refs):
        outs = refs[nw:2 * nw]
        send_sems, recv_sems = refs[2 * nw], refs[2 * nw + 1]
        x, y, c, chips = _place()
        me, sibling, mine = (x, y, c), (x, y, 1 - c), 2 * x + y

        def rows(ref, chip, half):
            r2 = ref.shape[2] // 2
            return ref.at[:, chip, pl.ds(half * r2, r2)]

        def copy(ref, sem, chip, half, to):
            return pltpu.make_async_remote_copy(
                src_ref=rows(ref, chip, half), dst_ref=rows(ref, chip, half), send_sem=send_sems.at[sem],
                recv_sem=recv_sems.at[sem], device_id=to, device_id_type=MESH)

        started = []
        for wi, ref in enumerate(outs):
            for n, chip in enumerate(chips):
                started.append(copy(ref, 6 * wi + n, mine, c, (*chip, c)))
                started[-1].start()
        for wi, ref in enumerate(outs):
            for n, chip in enumerate(chips):
                k = 2 * chip[0] + chip[1]
                copy(ref, 6 * wi + n, k, c, me).wait_recv()
                started.append(copy(ref, 6 * wi + 3 + n, k, c, sibling))
                started[-1].start()
        for wi, ref in enumerate(outs):
            for n, chip in enumerate(chips):
                copy(ref, 6 * wi + 3 + n, 2 * chip[0] + chip[1], 1 - c, me).wait_recv()
        for cp in started:
            cp.wait_send()

    return pl.pallas_call(
        body, out_shape=tuple(SDS(b.shape, b.dtype) for b in bufs), in_specs=[ANY] * nw, out_specs=(ANY,) * nw,
        input_output_aliases={i: i for i in range(nw)},
        scratch_shapes=[pltpu.SemaphoreType.DMA((6 * nw,)), pltpu.SemaphoreType.DMA((6 * nw,))],
        name="all_gather_weights")(*bufs)


def _pair_swap(grads):
    nw = len(grads)

    def body(*refs):
        ins, outs = refs[:nw], refs[nw:2 * nw]
        send_sems, recv_sems = refs[2 * nw], refs[2 * nw + 1]
        x, y, c, _ = _place()
        copies = []
        for wi, (g_ref, o_ref) in enumerate(zip(ins, outs)):
            r2 = g_ref.shape[1] // 2
            copies.append(pltpu.make_async_remote_copy(
                src_ref=g_ref.at[:, pl.ds((1 - c) * r2, r2)], dst_ref=o_ref, send_sem=send_sems.at[wi],
                recv_sem=recv_sems.at[wi], device_id=(x, y, 1 - c), device_id_type=MESH))
            copies[-1].start()
        for cp in copies:
            cp.wait()

    return pl.pallas_call(
        body, out_shape=tuple(SDS((N_CHIPS, g.shape[1] // 2, g.shape[2]), g.dtype) for g in grads),
        in_specs=[ANY] * nw, out_specs=(ANY,) * nw,
        scratch_shapes=[pltpu.SemaphoreType.DMA((nw,)), pltpu.SemaphoreType.DMA((nw,))], name="pair_swap")(*grads)


def _pair_add(g, other):
    _, r, c = g.shape
    r2, tr = _half_rows(r, c)
    nb = r2 // tr

    def body(g_ref, o_ref, pb_ref, land_ref):
        total = (g_ref[...].astype(F32) + o_ref[...].astype(F32)).astype(BF16)
        pb_ref[...] = total

        @pl.when(pl.program_id(1) == _my_chip())
        def _():
            land_ref[...] = total

    blk = pl.BlockSpec((None, tr, c), lambda j, k: (k, j, 0))
    return pl.pallas_call(
        body, out_shape=(SDS((N_CHIPS, r2, c), BF16),) * 2, grid=(nb, N_CHIPS),
        in_specs=[pl.BlockSpec((None, tr, c), lambda j, k: (k, lax.axis_index("c") * nb + j, 0)), blk],
        out_specs=(blk, pl.BlockSpec((None, tr, c), lambda j, k: (_my_chip(), j, 0))),
        compiler_params=_cparams(("parallel", "arbitrary")), name="pair_add")(g, other)


def _chip_all_to_all(pbs, lands):
    nw = len(pbs)

    def body(*refs):
        ins, outs = refs[:nw], refs[2 * nw:3 * nw]
        send_sems, recv_sems = refs[3 * nw], refs[3 * nw + 1]
        x, y, c, chips = _place()
        mine = 2 * x + y
        copies = []
        for wi, (p_ref, l_ref) in enumerate(zip(ins, outs)):
            for n, chip in enumerate(chips):
                copies.append(pltpu.make_async_remote_copy(
                    src_ref=p_ref.at[2 * chip[0] + chip[1]], dst_ref=l_ref.at[mine], send_sem=send_sems.at[3 * wi + n],
                    recv_sem=recv_sems.at[3 * wi + n], device_id=(*chip, c), device_id_type=MESH))
                copies[-1].start()
        for wi, (p_ref, l_ref) in enumerate(zip(ins, outs)):
            for n, chip in enumerate(chips):
                pltpu.make_async_remote_copy(
                    src_ref=p_ref.at[mine], dst_ref=l_ref.at[2 * chip[0] + chip[1]], send_sem=send_sems.at[3 * wi + n],
                    recv_sem=recv_sems.at[3 * wi + n], device_id=(*chip, c), device_id_type=MESH).wait_recv()
        for cp in copies:
            cp.wait_send()

    return pl.pallas_call(
        body, out_shape=tuple(SDS(l.shape, l.dtype) for l in lands), in_specs=[ANY] * (2 * nw), out_specs=(ANY,) * nw,
        input_output_aliases={nw + i: i for i in range(nw)},
        scratch_shapes=[pltpu.SemaphoreType.DMA((3 * nw,)), pltpu.SemaphoreType.DMA((3 * nw,))],
        name="chip_all_to_all")(*pbs, *lands)


def _sum_chips(land, gstack, layer):
    _, r, c = gstack.shape
    r2, tr = _half_rows(r, c)
    nb = r2 // tr

    def body(l_ref, g_ref, out_ref):
        out_ref[...] = ((l_ref[0].astype(F32) + l_ref[1].astype(F32)) + l_ref[2].astype(F32)) + l_ref[3].astype(F32)

    return pl.pallas_call(
        body, out_shape=SDS(gstack.shape, F32), grid=(nb,),
        in_specs=[pl.BlockSpec((N_CHIPS, tr, c), lambda j: (0, j, 0)), ANY],
        out_specs=pl.BlockSpec((None, tr, c), lambda j: (layer, lax.axis_index("c") * nb + j, 0)),
        input_output_aliases={1: 0}, compiler_params=_cparams(("parallel",)), name="sum_chips")(land, gstack)


def _pair_gather(gstacks, layer):
    nw = len(gstacks)

    def body(*refs):
        outs = refs[nw:2 * nw]
        send_sems, recv_sems = refs[2 * nw], refs[2 * nw + 1]
        x, y, c, _ = _place()

        def copy(ref, wi, half):
            r2 = ref.shape[1] // 2
            blk = ref.at[layer, pl.ds(half * r2, r2)]
            return pltpu.make_async_remote_copy(src_ref=blk, dst_ref=blk, send_sem=send_sems.at[wi],
                                                recv_sem=recv_sems.at[wi], device_id=(x, y, 1 - c), device_id_type=MESH)

        sent = [copy(ref, wi, c) for wi, ref in enumerate(outs)]
        for cp in sent:
            cp.start()
        for wi, ref in enumerate(outs):
            copy(ref, wi, 1 - c).wait_recv()
        for cp in sent:
            cp.wait_send()

    return pl.pallas_call(
        body, out_shape=tuple(SDS(g.shape, g.dtype) for g in gstacks), in_specs=[ANY] * nw, out_specs=(ANY,) * nw,
        input_output_aliases={i: i for i in range(nw)},
        scratch_shapes=[pltpu.SemaphoreType.DMA((nw,)), pltpu.SemaphoreType.DMA((nw,))], name="pair_gather")(*gstacks)


def _all_gather_small(vec, name):
    rows, w = vec.shape

    def body(v_ref, sum_ref, all_ref, send_sems, recv_sems):
        x, y, c, chips = _place()
        me, sibling = (x, y, c), (x, y, 1 - c)

        def slot(px, py, pc):
            return all_ref.at[4 * px + 2 * py + pc]

        def copy(k, block, to, src=None):
            return pltpu.make_async_remote_copy(
                src_ref=slot(*block) if src is None else src, dst_ref=slot(*block), send_sem=send_sems.at[k],
                recv_sem=recv_sems.at[k], device_id=to, device_id_type=MESH)

        first = [copy(0, me, sibling, src=v_ref)]
        first += [copy(1 + n, me, (*chip, c), src=v_ref) for n, chip in enumerate(chips)]
        for cp in first:
            cp.start()
        slot(*me)[...] = v_ref[...]
        passed = [copy(4 + n, (*chip, c), sibling) for n, chip in enumerate(chips)]
        for n, chip in enumerate(chips):
            copy(1 + n, (*chip, c), me).wait_recv()
            passed[n].start()
        copy(0, sibling, me).wait_recv()
        for n, chip in enumerate(chips):
            copy(4 + n, (*chip, 1 - c), me).wait_recv()
        for cp in first + passed:
            cp.wait_send()
        total = all_ref[0]
        for dev in range(1, 8):
            total = total + all_ref[dev]
        sum_ref[...] = total

    vm = pl.BlockSpec(memory_space=pltpu.VMEM)
    return pl.pallas_call(
        body, out_shape=(SDS((rows, w), F32), SDS((8, rows, w), F32)), in_specs=[vm], out_specs=(vm, vm),
        scratch_shapes=[pltpu.SemaphoreType.DMA((7,)), pltpu.SemaphoreType.DMA((7,))], name=name)(vec)


def _to_rows128(flat):
    n = flat.shape[0]
    rows = -(-n // (8 * LANES)) * 8
    return jnp.pad(flat, (0, rows * LANES - n)).reshape(rows, LANES)


def _in_layout(conv, ql, kvl, d):
    lay = {"conv": conv, "ql": ql, "kvl": kvl, "d": d}
    lay["q"] = 3 * conv
    lay["kr"] = lay["q"] + ql
    lay["gc"] = lay["kr"] + LANES
    lay["gm"] = lay["gc"] + d
    lay["kv"] = lay["gm"] + d
    used = lay["kv"] + kvl
    lay["width"] = -(-used // 512) * 512
    return lay


def _w_in_to_layout(w, lay):
    conv, ql, kvl, d = lay["conv"], lay["ql"], lay["kvl"], lay["d"]
    o_kv = 3 * conv + ql
    o_kr = o_kv + kvl
    o_g = o_kr + ROPE_DIM
    lead = w.shape[:-1]
    parts = [w[..., :o_kv], w[..., o_kr:o_g], jnp.zeros(lead + (LANES - ROPE_DIM,), w.dtype), w[..., o_g:o_g + 2 * d],
             w[..., o_kv:o_kr], jnp.zeros(lead + (lay["width"] - lay["kv"] - kvl,), w.dtype)]
    return jnp.concatenate(parts, axis=-1)


def _w_in_from_layout(g, lay):
    ql, kvl, d = lay["ql"], lay["kvl"], lay["d"]
    return jnp.concatenate([g[:, :lay["q"] + ql], g[:, lay["kv"]:lay["kv"] + kvl], g[:, lay["kr"]:lay["kr"] + ROPE_DIM],
                            g[:, lay["gc"]:lay["gc"] + 2 * d]], axis=1)


def _w_uq_to_layout(w):
    nl, r, _ = w.shape
    w4 = w.reshape(nl, r, N_HEADS, NOPE_DIM + ROPE_DIM)
    return jnp.pad(w4, ((0, 0), (0, 0), (0, 0), (0, QK_PAD - NOPE_DIM - ROPE_DIM))).reshape(nl, r, N_HEADS * QK_PAD)


def _w_uq_from_layout(g):
    r = g.shape[0]
    return g.reshape(r, N_HEADS, QK_PAD)[:, :, :NOPE_DIM + ROPE_DIM].reshape(r, N_HEADS * (NOPE_DIM + ROPE_DIM))


def _w_ukv_to_layout(w):
    nl, r, _ = w.shape
    return w.reshape(nl, r, N_HEADS, 2, NOPE_DIM).transpose(0, 1, 3, 2, 4).reshape(nl, r, 2 * N_HEADS * NOPE_DIM)


def _w_ukv_from_layout(g):
    r = g.shape[0]
    return g.reshape(r, 2, N_HEADS, NOPE_DIM).transpose(0, 2, 1, 3).reshape(r, 2 * N_HEADS * NOPE_DIM)


def _chips_to_cols(buf):
    nl, _, r, c = buf.shape
    return buf.transpose(0, 2, 1, 3).reshape(nl, r, N_CHIPS * c)


def _cols_to_chips(g):
    r, c4 = g.shape
    return g.reshape(r, N_CHIPS, c4 // N_CHIPS).transpose(1, 0, 2)


def _ffn_fwd(h, gain, w_gu, w_down, layer, tag):
    n = _rmsnorm_fwd(h, gain, tag + "_norm_fwd")
    gu = _mm(n, w_gu, "nn", BF16, tag + "_gu_fwd", layer=layer, b_chip=True)
    a = _swiglu_fwd(gu)
    out = _mm(a, w_down, "nn", F32, tag + "_down_fwd", scale=0.5, res=h, layer=layer)
    return out, (h, n, gu, a)


def _ffn_bwd(dh, dhb, saved, gain, w_gu, w_down, layer, tag):
    h, n, gu, a = saved
    d_wdown = _mm(a, dhb, "tn", BF16, tag + "_down_dw", scale=0.5)
    da = _mm(dhb, w_down, "nt", BF16, tag + "_down_dx", scale=0.5, layer=layer)
    dgu = _swiglu_bwd(gu, da)
    d_wgu = _mm(n, dgu, "tn", BF16, tag + "_gu_dw", out_chip=True)
    dn = _mm(dgu, w_gu, "nt", BF16, tag + "_gu_dx", layer=layer, b_chip=True)
    dh, dhb, dgain = _rmsnorm_bwd(h, gain, dn, dh, tag + "_norm_bwd")
    return dh, dhb, d_wgu, d_wdown, dgain


def _layer_fwd(h0, p_i, w, conv_w, norms, layer, lay, tables):
    h1, s_ffn1 = _ffn_fwd(h0, norms["ffn1_norm"], w["ffn1_w_gu"], w["ffn1_w_down"], layer, "ffn1")
    n2 = _rmsnorm_fwd(h1, norms["mix_norm"], "mix_norm_fwd")
    proj = _mm(n2, w["w_in"], "nn", BF16, "in_fwd", layer=layer)
    cb = _conv_fwd(proj, conv_w)
    ya = _mm(cb, w["w_conv_out"], "nn", BF16, "conv_out_fwd", layer=layer)
    qn, kvn = _qkvnorm_fwd(proj, lay, norms["q_norm"], norms["kv_norm"])
    qf = _mm(qn, w["w_uq"], "nn", BF16, "uq_fwd", layer=layer)
    kv = _mm(kvn, w["w_ukv"], "nn", BF16, "ukv_fwd", layer=layer)
    qr, kf = _rope_fwd(qf, kv, proj, lay, tables)
    o, lse = _attn_fwd(qr, kf, kv)
    yb = _mm(o, w["w_mla_out"], "nn", BF16, "mla_out_fwd", layer=layer)
    mg = _merge_fwd(proj, lay, ya, yb)
    h2 = _mm(mg, w["w_o"], "nn", F32, "o_fwd", res=h1, layer=layer)
    h3, s_ffn2 = _ffn_fwd(h2, norms["ffn2_norm"], w["ffn2_w_gu"], w["ffn2_w_down"], layer, "ffn2")
    n4 = _rmsnorm_fwd(h3, norms["ple_norm"], "ple_norm_fwd")
    gp = _mm(n4, w["w_ple_gate"], "nn", BF16, "ple_gate_fwd", layer=layer)
    pp = _mm(p_i, w["w_ple_proj"], "nn", BF16, "ple_proj_fwd", layer=layer)
    h4 = _ple_fwd(h3, gp, pp)
    saved = dict(s_ffn1=s_ffn1, h1=h1, n2=n2, proj=proj, cb=cb, ya=ya, qn=qn, kvn=kvn, qr=qr, kf=kf, kv=kv, o=o,
                 lse=lse, yb=yb, mg=mg, h2=h2, s_ffn2=s_ffn2, h3=h3, n4=n4, gp=gp, pp=pp, p=p_i)
    return h4, saved


def _layer_bwd(dh, s, w, conv_w, norms, layer, lay, tables):
    gw, gn = {}, {}

    def by_rows(g):
        return g.reshape(N_CHIPS, g.shape[0] // N_CHIPS, g.shape[1])

    dpp, dgp = _ple_bwd(dh, s["gp"], s["pp"])
    gw["w_ple_proj"] = _cols_to_chips(_mm(s["p"], dpp, "tn", BF16, "ple_proj_dw"))
    gw["w_ple_gate"] = by_rows(_mm(s["n4"], dgp, "tn", BF16, "ple_gate_dw"))
    dn4 = _mm(dgp, w["w_ple_gate"], "nt", BF16, "ple_gate_dx", layer=layer)
    dh, dhb, gn["ple_norm"] = _rmsnorm_bwd(s["h3"], norms["ple_norm"], dn4, dh, "ple_norm_bwd")
    dh, dhb, gw["ffn2_w_gu"], g_down, gn["ffn2_norm"] = _ffn_bwd(
        dh, dhb, s["s_ffn2"], norms["ffn2_norm"], w["ffn2_w_gu"], w["ffn2_w_down"], layer, "ffn2")
    gw["ffn2_w_down"] = by_rows(g_down)
    gw["w_o"] = by_rows(_mm(s["mg"], dhb, "tn", BF16, "o_dw"))
    dmg = _mm(dhb, w["w_o"], "nt", BF16, "o_dx", layer=layer)
    dya, dyb, dgc, dgm = _merge_bwd(s["proj"], lay, s["ya"], s["yb"], dmg)
    gw["w_conv_out"] = _cols_to_chips(_mm(s["cb"], dya, "tn", BF16, "conv_out_dw"))
    dcb = _mm(dya, w["w_conv_out"], "nt", BF16, "conv_out_dx", layer=layer)
    db, dc, dv_conv, g_conv = _conv_bwd(s["proj"], conv_w, dcb)
    gw["w_mla_out"] = by_rows(_mm(s["o"], dyb, "tn", BF16, "mla_out_dw"))
    do = _mm(dyb, w["w_mla_out"], "nt", BF16, "mla_out_dx", layer=layer)
    delta = _attn_delta(do, s["o"])
    dkf, dv, dqr = _attn_bwd(s["qr"], s["kf"], s["kv"], do, s["lse"], delta)
    dqf, dkv, dkr = _rope_bwd(dqr, dkf, dv, tables)
    gw["w_uq"] = _cols_to_chips(_w_uq_from_layout(_mm(s["qn"], dqf, "tn", BF16, "uq_dw")))
    dqn = _mm(dqf, w["w_uq"], "nt", BF16, "uq_dx", layer=layer)
    gw["w_ukv"] = _cols_to_chips(_w_ukv_from_layout(_mm(s["kvn"], dkv, "tn", BF16, "ukv_dw")))
    dkvn = _mm(dkv, w["w_ukv"], "nt", BF16, "ukv_dx", layer=layer)
    dqc, dkvc, gn["q_norm"], gn["kv_norm"] = _qkvnorm_bwd(s["proj"], lay, norms["q_norm"], norms["kv_norm"], dqn, dkvn)
    t = dh.shape[0]
    dproj = jnp.concatenate([db, dc, dv_conv, dqc, dkr, dgc, dgm, dkvc,
                             jnp.zeros((t, lay["width"] - lay["kv"] - lay["kvl"]), BF16)], axis=1)
    gw["w_in"] = _cols_to_chips(_w_in_from_layout(_mm(s["n2"], dproj, "tn", BF16, "in_dw"), lay))
    dn2 = _mm(dproj, w["w_in"], "nt", BF16, "in_dx", layer=layer)
    dh, dhb, gn["mix_norm"] = _rmsnorm_bwd(s["h1"], norms["mix_norm"], dn2, dh, "mix_norm_bwd")
    dh, dhb, gw["ffn1_w_gu"], g_down, gn["ffn1_norm"] = _ffn_bwd(
        dh, dhb, s["s_ffn1"], norms["ffn1_norm"], w["ffn1_w_gu"], w["ffn1_w_down"], layer, "ffn1")
    gw["ffn1_w_down"] = by_rows(g_down)
    return dh, gw, g_conv, gn


def _rope_tables(positions):
    half = ROPE_DIM // 2
    inv_freq = ROPE_THETA ** (-jnp.arange(0, ROPE_DIM, 2, dtype=F32) / ROPE_DIM)
    ang = positions.astype(F32)[:, None] * inv_freq
    cos, sin = jnp.cos(ang), jnp.sin(ang)
    zeros = jnp.zeros_like(cos)
    cos_t = jnp.concatenate([cos, cos, zeros, zeros], axis=1)
    sin_a = jnp.concatenate([-sin, zeros, zeros, zeros], axis=1)
    sin_b = jnp.concatenate([zeros, sin, zeros, zeros], axis=1)
    assert cos_t.shape[1] == LANES and half * 4 == LANES
    return cos_t, sin_a, sin_b


def kernel(x, p, positions, ffn1_norm, ffn1_w_gu, ffn1_w_down, mix_norm, w_in, conv_w, w_conv_out, q_norm, kv_norm, w_uq, w_ukv, w_mla_out, w_o, ffn2_norm, ffn2_w_gu, ffn2_w_down, ple_norm, w_ple_gate, w_ple_proj, final_norm, loss_target, m_ffn1_norm, m_ffn1_w_gu, m_ffn1_w_down, m_mix_norm, m_w_in, m_conv_w, m_w_conv_out, m_q_norm, m_kv_norm, m_w_uq, m_w_ukv, m_w_mla_out, m_w_o, m_ffn2_norm, m_ffn2_w_gu, m_ffn2_w_down, m_ple_norm, m_w_ple_gate, m_w_ple_proj, m_final_norm, v_ffn1_norm, v_ffn1_w_gu, v_ffn1_w_down, v_mix_norm, v_w_in, v_conv_w, v_w_conv_out, v_q_norm, v_kv_norm, v_w_uq, v_w_ukv, v_w_mla_out, v_w_o, v_ffn2_norm, v_ffn2_w_gu, v_ffn2_w_down, v_ple_norm, v_w_ple_gate, v_w_ple_proj, v_final_norm):
    args = dict(zip(ARG_NAMES, (x, p, positions, ffn1_norm, ffn1_w_gu, ffn1_w_down, mix_norm, w_in, conv_w, w_conv_out, q_norm, kv_norm, w_uq, w_ukv, w_mla_out, w_o, ffn2_norm, ffn2_w_gu, ffn2_w_down, ple_norm, w_ple_gate, w_ple_proj, final_norm, loss_target, m_ffn1_norm, m_ffn1_w_gu, m_ffn1_w_down, m_mix_norm, m_w_in, m_conv_w, m_w_conv_out, m_q_norm, m_kv_norm, m_w_uq, m_w_ukv, m_w_mla_out, m_w_o, m_ffn2_norm, m_ffn2_w_gu, m_ffn2_w_down, m_ple_norm, m_w_ple_gate, m_w_ple_proj, m_final_norm, v_ffn1_norm, v_ffn1_w_gu, v_ffn1_w_down, v_mix_norm, v_w_in, v_conv_w, v_w_conv_out, v_q_norm, v_kv_norm, v_w_uq, v_w_ukv, v_w_mla_out, v_w_o, v_ffn2_norm, v_ffn2_w_gu, v_ffn2_w_down, v_ple_norm, v_w_ple_gate, v_w_ple_proj, v_final_norm)))
    depth = ffn1_norm.shape[0]
    t, d = x.shape[1], x.shape[2]
    conv = conv_w.shape[-1] * N_CHIPS
    lay = _in_layout(conv, q_norm.shape[-1], kv_norm.shape[-1], d)
    chip = 2 * lax.axis_index("x") + lax.axis_index("y")
    tables = _rope_tables(positions[0])

    bufs = _all_gather_weights([_cast_into_slot(args[name]) for name in BIG])
    w = {}
    for (name, axis), buf in zip(SHARDED, bufs):
        if axis == 0:
            w[name] = buf.reshape(depth, N_CHIPS * buf.shape[2], buf.shape[3])
        elif name in ("ffn1_w_gu", "ffn2_w_gu"):
            w[name] = buf
        else:
            w[name] = _chips_to_cols(buf)
    w["w_in"] = _w_in_to_layout(w["w_in"], lay)
    w["w_uq"] = _w_uq_to_layout(w["w_uq"])
    w["w_ukv"] = _w_ukv_to_layout(w["w_ukv"])
    conv_rows = depth * conv_w.shape[1]
    conv_all = _all_gather_small(_to_rows128(conv_w.reshape(-1)), "all_gather_conv_w")[1]
    conv_full = conv_all[0::2, :conv_rows].reshape(N_CHIPS, depth, conv_w.shape[1], LANES)
    conv_full = conv_full.transpose(1, 2, 0, 3).reshape(depth, conv_w.shape[1], conv)
    norms = [{name: args[name][i] for name in REPLICATED} for i in range(depth)]
    p3 = p.reshape(depth, t, p.shape[-1])

    h = x[0]
    saved = []
    for i in range(depth):
        h, s = _layer_fwd(h, p3[i], w, conv_full[i], norms[i], i, lay, tables)
        saved.append(s)
    loss_part, dh, _, g_final = _loss_head(h, final_norm, loss_target[0])
    loss = lax.psum(loss_part[0, 0], ("x", "y", "c"))

    gstacks = [lax.empty(args[name].shape, F32) for name in BIG]
    norm_grads, conv_grads = [None] * depth, [None] * depth
    for i in reversed(range(depth)):
        dh, gw, conv_grads[i], norm_grads[i] = _layer_bwd(dh, saved[i], w, conv_full[i], norms[i], i, lay, tables)
        grads_i = [gw[name] for name in BIG]
        others = _pair_swap(grads_i)
        pairs = [_pair_add(g, o) for g, o in zip(grads_i, others)]
        lands = _chip_all_to_all([pb for pb, _ in pairs], [land for _, land in pairs])
        gstacks = [_sum_chips(land, gs, i) for land, gs in zip(lands, gstacks)]
        gstacks = list(_pair_gather(gstacks, i))
    grad_x = dh[None]
    grads = dict(zip(BIG, gstacks))

    pieces = [norm_grads[i][name].reshape(-1) for i in range(depth) for name in REPLICATED]
    pieces += [g_final.reshape(-1)] + [conv_grads[i].reshape(-1) for i in range(depth)]
    vec = _all_gather_small(_to_rows128(jnp.concatenate(pieces)), "all_sum_small")[0].reshape(-1)
    off = 0
    per_name = {name: [] for name in REPLICATED}
    for i in range(depth):
        for name in REPLICATED:
            size = args[name].shape[1]
            per_name[name].append(vec[off:off + size])
            off += size
    for name in REPLICATED:
        grads[name] = jnp.stack(per_name[name])
    grads["final_norm"] = vec[off:off + d]
    off += d
    conv_g = vec[off:off + depth * 3 * conv].reshape(depth, 3, conv)
    grads["conv_w"] = lax.dynamic_slice_in_dim(conv_g, chip * conv_w.shape[-1], conv_w.shape[-1], axis=2)

    delta, new_m, new_v = {}, {}, {}
    for name in WEIGHTS:
        w_, g_, m_, v_ = args[name], grads[name], args["m_" + name], args["v_" + name]
        if w_.ndim == 1:
            outs = _adamw(w_[None], g_[None], m_[None], v_[None])
            delta[name], new_m[name], new_v[name] = (o[0] for o in outs)
        else:
            delta[name], new_m[name], new_v[name] = _adamw(w_, g_, m_, v_)
    return (loss, grad_x, *[grads[n] for n in WEIGHTS], *[delta[n] for n in WEIGHTS],
            *[new_m[n] for n in WEIGHTS], *[new_v[n] for n in WEIGHTS])
```

```python
import functools

import jax
import jax.numpy as jnp
from jax import lax
from jax.experimental import pallas as pl
from jax.experimental.pallas import tpu as pltpu

BF16 = jnp.bfloat16
F32 = jnp.float32
SDS = jax.ShapeDtypeStruct
MESH = pl.DeviceIdType.MESH

N_HEADS = 8
NOPE_DIM = 128
ROPE_DIM = 64
V_DIM = 128
QK_PAD = 256
CHUNK = 64
ROPE_THETA = 10000.0
EPS = 1e-6
ATTN_SCALE = (NOPE_DIM + ROPE_DIM) ** -0.5
NEG_BIG = -1e30

ADAM_LR = 0.001
ADAM_B1 = 0.9
ADAM_B2 = 0.999
ADAM_EPS = 1e-08
ADAM_WD = 0.01
ADAM_STEP = 10

LANES = 128
N_CHIPS = 4
VMEM_LIMIT_BYTES = 56 * 1024 * 1024
ACC_BYTES = 6 * 1024 * 1024
BLOCK_ELEMS = 1 << 19

SHARDED = (("ffn1_w_gu", 1), ("ffn1_w_down", 0), ("w_in", 1), ("w_conv_out", 1), ("w_uq", 1), ("w_ukv", 1),
           ("w_mla_out", 0), ("w_o", 0), ("ffn2_w_gu", 1), ("ffn2_w_down", 0), ("w_ple_gate", 0), ("w_ple_proj", 1))
BIG = tuple(name for name, _ in SHARDED)
REPLICATED = ("ffn1_norm", "mix_norm", "q_norm", "kv_norm", "ffn2_norm", "ple_norm")
WEIGHTS = ("ffn1_norm", "ffn1_w_gu", "ffn1_w_down", "mix_norm", "w_in", "conv_w", "w_conv_out", "q_norm",
           "kv_norm", "w_uq", "w_ukv", "w_mla_out", "w_o", "ffn2_norm", "ffn2_w_gu", "ffn2_w_down",
           "ple_norm", "w_ple_gate", "w_ple_proj", "final_norm")
ARG_NAMES = ("x", "p", "positions") + WEIGHTS + ("loss_target",) + tuple("m_" + n for n in WEIGHTS) + tuple(
    "v_" + n for n in WEIGHTS)


def _cparams(semantics=None):
    return pltpu.CompilerParams(dimension_semantics=semantics, vmem_limit_bytes=VMEM_LIMIT_BYTES)


def _tile(n, cap, mult=LANES):
    best = None
    for t in range(mult, min(n, cap) + 1, mult):
        if n % t == 0:
            best = t
    return n if best is None else best


def _sigmoid(x):
    return 1.0 / (1.0 + jnp.exp(-x))


def _rowspec(tm, width, col_block=0):
    return pl.BlockSpec((tm, width), lambda i: (i, col_block))


def _colspec(tm, width, offset):
    assert offset % width == 0, (width, offset)
    return _rowspec(tm, width, offset // width)


def _mm(a, b, mode, out_dtype, name, scale=None, res=None, layer=None, b_chip=False, out_chip=False):
    bshape = b.shape if layer is None else b.shape[1:]
    if b_chip:
        bshape = (bshape[1], N_CHIPS * bshape[2])
    if mode == "nn":
        (m, k), (k2, n) = a.shape, bshape
    elif mode == "nt":
        (m, k), (n, k2) = a.shape, bshape
    else:
        (k, m), (k2, n) = a.shape, bshape
    assert k == k2, (a.shape, b.shape, mode)
    n_unit = n // N_CHIPS if (out_chip or (b_chip and mode == "nn")) else n
    k_unit = k // N_CHIPS if (b_chip and mode == "nt") else k
    tn = _tile(n_unit, 1536)
    tm = _tile(m, min(1408, ACC_BYTES // (4 * tn)))
    tk = _tile(k_unit, 1536)
    nk = k // tk
    n_per, k_per = n_unit // tn, k_unit // tk
    dims = {"nn": (((1,), (0,)), ((), ())), "nt": (((1,), (1,)), ((), ())), "tn": (((0,), (0,)), ((), ()))}[mode]

    def body(*refs):
        a_ref, b_ref = refs[0], refs[1]
        res_ref = refs[2] if res is not None else None
        o_ref = refs[3] if res is not None else refs[2]
        acc_ref = refs[-1] if nk > 1 else None

        def finish(acc):
            if scale is not None:
                acc = acc * scale
            if res_ref is not None:
                acc = res_ref[...] + acc
            o_ref[...] = acc.astype(out_dtype)

        part = lax.dot_general(a_ref[...].astype(BF16), b_ref[...].astype(BF16), dims,
                               preferred_element_type=F32)
        if nk == 1:
            finish(part)
        else:
            kk = pl.program_id(2)

            @pl.when(kk == 0)
            def _():
                acc_ref[...] = part

            @pl.when(kk > 0)
            def _():
                acc_ref[...] += part

            @pl.when(kk == nk - 1)
            def _():
                finish(acc_ref[...])

    lead = () if layer is None else (layer,)
    lead_block = () if layer is None else (None,)
    if mode == "nn":
        a_spec = pl.BlockSpec((tm, tk), lambda i, j, kk: (i, kk))
        if b_chip:
            b_spec = pl.BlockSpec(lead_block + (None, tk, tn), lambda i, j, kk: lead + (j // n_per, kk, j % n_per))
        else:
            b_spec = pl.BlockSpec(lead_block + (tk, tn), lambda i, j, kk: lead + (kk, j))
    elif mode == "nt":
        a_spec = pl.BlockSpec((tm, tk), lambda i, j, kk: (i, kk))
        if b_chip:
            b_spec = pl.BlockSpec(lead_block + (None, tn, tk), lambda i, j, kk: lead + (kk // k_per, j, kk % k_per))
        else:
            b_spec = pl.BlockSpec(lead_block + (tn, tk), lambda i, j, kk: lead + (j, kk))
    else:
        assert layer is None and not b_chip
        a_spec = pl.BlockSpec((tk, tm), lambda i, j, kk: (kk, i))
        b_spec = pl.BlockSpec((tk, tn), lambda i, j, kk: (kk, j))
    if out_chip:
        o_spec = pl.BlockSpec((None, tm, tn), lambda i, j, kk: (j // n_per, i, j % n_per))
        out_shape = SDS((N_CHIPS, m, n_unit), out_dtype)
    else:
        o_spec = pl.BlockSpec((tm, tn), lambda i, j, kk: (i, j))
        out_shape = SDS((m, n), out_dtype)
    in_specs = [a_spec, b_spec] + ([o_spec] if res is not None else [])
    operands = (a, b) + ((res,) if res is not None else ())
    return pl.pallas_call(
        body, out_shape=out_shape, grid=(m // tm, n // tn, nk), in_specs=in_specs, out_specs=o_spec,
        scratch_shapes=[pltpu.VMEM((tm, tn), F32)] if nk > 1 else [],
        compiler_params=_cparams(("parallel", "parallel", "arbitrary")), name=name)(*operands)


def _rn_fwd_math(x, g):
    r = lax.rsqrt(jnp.mean(x * x, axis=-1, keepdims=True) + EPS)
    return (x * r) * g


def _rn_bwd_math(x, g, dn):
    r = lax.rsqrt(jnp.mean(x * x, axis=-1, keepdims=True) + EPS)
    xh = x * r
    gy = dn * g
    dx = r * (gy - xh * jnp.mean(gy * xh, axis=-1, keepdims=True))
    dg = jnp.sum(dn * xh, axis=0, keepdims=True)
    return dx, dg


def _rmsnorm_fwd(h, gain, name):
    t, d = h.shape
    tm = _tile(t, 512, 8)

    def body(h_ref, g_ref, o_ref):
        o_ref[...] = _rn_fwd_math(h_ref[...], g_ref[...]).astype(BF16)

    return pl.pallas_call(
        body, out_shape=SDS((t, d), BF16), grid=(t // tm,),
        in_specs=[_rowspec(tm, d), pl.BlockSpec((1, d), lambda i: (0, 0))], out_specs=_rowspec(tm, d),
        compiler_params=_cparams(("parallel",)), name=name)(h, gain.reshape(1, d))


def _rmsnorm_bwd(h, gain, dn, dh_in, name):
    t, d = h.shape
    tm = _tile(t, 512, 8)

    def body(h_ref, g_ref, dn_ref, dhin_ref, dh_ref, dhb_ref, dg_ref):
        dx, dg = _rn_bwd_math(h_ref[...], g_ref[...], dn_ref[...].astype(F32))
        dh = dhin_ref[...] + dx
        dh_ref[...] = dh
        dhb_ref[...] = dh.astype(BF16)

        @pl.when(pl.program_id(0) == 0)
        def _():
            dg_ref[...] = dg

        @pl.when(pl.program_id(0) > 0)
        def _():
            dg_ref[...] += dg

    vec = pl.BlockSpec((1, d), lambda i: (0, 0))
    return pl.pallas_call(
        body, out_shape=(SDS((t, d), F32), SDS((t, d), BF16), SDS((1, d), F32)), grid=(t // tm,),
        in_specs=[_rowspec(tm, d), vec, _rowspec(tm, d), _rowspec(tm, d)],
        out_specs=(_rowspec(tm, d), _rowspec(tm, d), vec),
        compiler_params=_cparams(("arbitrary",)), name=name)(h, gain.reshape(1, d), dn, dh_in)


def _loss_head(h, gain, target):
    t, d = h.shape
    tm = _tile(t, 512, 8)

    def body(h_ref, g_ref, t_ref, loss_ref, dh_ref, dhb_ref, dg_ref):
        x, g = h_ref[...], g_ref[...]
        err = _rn_fwd_math(x, g) - t_ref[...]
        part = 0.5 * jnp.sum(jnp.sum(err * err, axis=1, keepdims=True), axis=0, keepdims=True) * (1.0 / d)
        dx, dg = _rn_bwd_math(x, g, err * (1.0 / d))
        dh_ref[...] = dx
        dhb_ref[...] = dx.astype(BF16)

        @pl.when(pl.program_id(0) == 0)
        def _():
            dg_ref[...] = dg
            loss_ref[...] = jnp.broadcast_to(part, (1, LANES))

        @pl.when(pl.program_id(0) > 0)
        def _():
            dg_ref[...] += dg
            loss_ref[...] += jnp.broadcast_to(part, (1, LANES))

    vec = pl.BlockSpec((1, d), lambda i: (0, 0))
    return pl.pallas_call(
        body, out_shape=(SDS((1, LANES), F32), SDS((t, d), F32), SDS((t, d), BF16), SDS((1, d), F32)),
        grid=(t // tm,), in_specs=[_rowspec(tm, d), vec, _rowspec(tm, d)],
        out_specs=(pl.BlockSpec((1, LANES), lambda i: (0, 0)), _rowspec(tm, d), _rowspec(tm, d), vec),
        compiler_params=_cparams(("arbitrary",)), name="loss_head")(h, gain.reshape(1, d), target)


def _swiglu_fwd(gu):
    t, f2 = gu.shape
    f = f2 // 2
    tm = _tile(t, 256, 8)

    def body(gu_ref, a_ref):
        g = gu_ref[:, :f].astype(F32)
        u = gu_ref[:, f:].astype(F32)
        a_ref[...] = (g * _sigmoid(g) * u).astype(BF16)

    return pl.pallas_call(
        body, out_shape=SDS((t, f), BF16), grid=(t // tm,), in_specs=[_rowspec(tm, f2)],
        out_specs=_rowspec(tm, f), compiler_params=_cparams(("parallel",)), name="swiglu_fwd")(gu)


def _swiglu_bwd(gu, da):
    t, f2 = gu.shape
    f = f2 // 2
    tm = _tile(t, 256, 8)

    def body(gu_ref, da_ref, dgu_ref):
        g = gu_ref[:, :f].astype(F32)
        u = gu_ref[:, f:].astype(F32)
        da_ = da_ref[...].astype(F32)
        s = _sigmoid(g)
        dgu_ref[:, :f] = (da_ * u * (s * (1.0 + g * (1.0 - s)))).astype(BF16)
        dgu_ref[:, f:] = (da_ * (g * s)).astype(BF16)

    return pl.pallas_call(
        body, out_shape=SDS((t, f2), BF16), grid=(t // tm,), in_specs=[_rowspec(tm, f2), _rowspec(tm, f)],
        out_specs=_rowspec(tm, f2), compiler_params=_cparams(("parallel",)), name="swiglu_bwd")(gu, da)


def _shift_down(z, k, row):
    return jnp.where(row >= k, pltpu.roll(z, k, 0), 0.0)


def _shift_up(z, k, row, t):
    return jnp.where(row < t - k, pltpu.roll(z, t - k, 0), 0.0)


def _conv_specs(t, conv):
    nb = conv // LANES
    return [pl.BlockSpec((t, LANES), lambda j: (0, j)), pl.BlockSpec((t, LANES), lambda j: (0, nb + j)),
            pl.BlockSpec((t, LANES), lambda j: (0, 2 * nb + j))]


def _conv_fwd(proj, conv_w):
    t = proj.shape[0]
    conv = conv_w.shape[1]

    def body(b_ref, c_ref, v_ref, w_ref, o_ref):
        z = c_ref[...].astype(F32) * v_ref[...].astype(F32)
        row = lax.broadcasted_iota(jnp.int32, z.shape, 0)
        y = w_ref[0:1, :] * _shift_down(z, 2, row) + w_ref[1:2, :] * _shift_down(z, 1, row) + w_ref[2:3, :] * z
        o_ref[...] = (b_ref[...].astype(F32) * y).astype(BF16)

    cspec = pl.BlockSpec((t, LANES), lambda j: (0, j))
    return pl.pallas_call(
        body, out_shape=SDS((t, conv), BF16), grid=(conv // LANES,),
        in_specs=_conv_specs(t, conv) + [pl.BlockSpec((3, LANES), lambda j: (0, j))], out_specs=cspec,
        compiler_params=_cparams(("parallel",)), name="conv_fwd")(proj, proj, proj, conv_w)


def _conv_bwd(proj, conv_w, dcb):
    t = proj.shape[0]
    conv = conv_w.shape[1]

    def body(b_ref, c_ref, v_ref, w_ref, d_ref, db_ref, dc_ref, dv_ref, dw_ref):
        b, c, v = b_ref[...].astype(F32), c_ref[...].astype(F32), v_ref[...].astype(F32)
        d = d_ref[...].astype(F32)
        z = c * v
        row = lax.broadcasted_iota(jnp.int32, z.shape, 0)
        z1, z2 = _shift_down(z, 1, row), _shift_down(z, 2, row)
        w0, w1, w2 = w_ref[0:1, :], w_ref[1:2, :], w_ref[2:3, :]
        y = w0 * z2 + w1 * z1 + w2 * z
        dy = d * b
        db_ref[...] = (d * y).astype(BF16)
        dz = w2 * dy + w1 * _shift_up(dy, 1, row, t) + w0 * _shift_up(dy, 2, row, t)
        dc_ref[...] = (dz * v).astype(BF16)
        dv_ref[...] = (dz * c).astype(BF16)
        dw_ref[0:1, :] = jnp.sum(dy * z2, axis=0, keepdims=True)
        dw_ref[1:2, :] = jnp.sum(dy * z1, axis=0, keepdims=True)
        dw_ref[2:3, :] = jnp.sum(dy * z, axis=0, keepdims=True)

    cspec = pl.BlockSpec((t, LANES), lambda j: (0, j))
    wspec = pl.BlockSpec((3, LANES), lambda j: (0, j))
    return pl.pallas_call(
        body, out_shape=(SDS((t, conv), BF16),) * 3 + (SDS((3, conv), F32),), grid=(conv // LANES,),
        in_specs=_conv_specs(t, conv) + [wspec, cspec], out_specs=(cspec, cspec, cspec, wspec),
        compiler_params=_cparams(("parallel",)), name="conv_bwd")(proj, proj, proj, conv_w, dcb)


def _qkvnorm_fwd(proj, lay, q_gain, kv_gain):
    t = proj.shape[0]
    ql, kvl = lay["ql"], lay["kvl"]
    tm = _tile(t, 512, 8)

    def body(q_ref, kv_ref, gq_ref, gkv_ref, qn_ref, kvn_ref):
        qn_ref[...] = _rn_fwd_math(q_ref[...].astype(F32), gq_ref[...]).astype(BF16)
        kvn_ref[...] = _rn_fwd_math(kv_ref[...].astype(F32), gkv_ref[...]).astype(BF16)

    return pl.pallas_call(
        body, out_shape=(SDS((t, ql), BF16), SDS((t, kvl), BF16)), grid=(t // tm,),
        in_specs=[_colspec(tm, ql, lay["q"]), _colspec(tm, kvl, lay["kv"]),
                  pl.BlockSpec((1, ql), lambda i: (0, 0)), pl.BlockSpec((1, kvl), lambda i: (0, 0))],
        out_specs=(_rowspec(tm, ql), _rowspec(tm, kvl)), compiler_params=_cparams(("parallel",)),
        name="qkvnorm_fwd")(proj, proj, q_gain.reshape(1, ql), kv_gain.reshape(1, kvl))


def _qkvnorm_bwd(proj, lay, q_gain, kv_gain, dqn, dkvn):
    t = proj.shape[0]
    ql, kvl = lay["ql"], lay["kvl"]
    tm = _tile(t, 512, 8)

    def body(q_ref, kv_ref, gq_ref, gkv_ref, dqn_ref, dkvn_ref, dq_ref, dkv_ref, dgq_ref, dgkv_ref):
        dq, dgq = _rn_bwd_math(q_ref[...].astype(F32), gq_ref[...], dqn_ref[...].astype(F32))
        dkv, dgkv = _rn_bwd_math(kv_ref[...].astype(F32), gkv_ref[...], dkvn_ref[...].astype(F32))
        dq_ref[...] = dq.astype(BF16)
        dkv_ref[...] = dkv.astype(BF16)

        @pl.when(pl.program_id(0) == 0)
        def _():
            dgq_ref[...] = dgq
            dgkv_ref[...] = dgkv

        @pl.when(pl.program_id(0) > 0)
        def _():
            dgq_ref[...] += dgq
            dgkv_ref[...] += dgkv

    vq = pl.BlockSpec((1, ql), lambda i: (0, 0))
    vkv = pl.BlockSpec((1, kvl), lambda i: (0, 0))
    return pl.pallas_call(
        body, out_shape=(SDS((t, ql), BF16), SDS((t, kvl), BF16), SDS((1, ql), F32), SDS((1, kvl), F32)),
        grid=(t // tm,),
        in_specs=[_colspec(tm, ql, lay["q"]), _colspec(tm, kvl, lay["kv"]), vq, vkv, _rowspec(tm, ql),
                  _rowspec(tm, kvl)],
        out_specs=(_rowspec(tm, ql), _rowspec(tm, kvl), vq, vkv), compiler_params=_cparams(("arbitrary",)),
        name="qkvnorm_bwd")(proj, proj, q_gain.reshape(1, ql), kv_gain.reshape(1, kvl), dqn, dkvn)


def _rope(x, cos_t, sin_a, sin_b):
    return x * cos_t + pltpu.roll(x, LANES - ROPE_DIM // 2, 1) * sin_a + pltpu.roll(x, ROPE_DIM // 2, 1) * sin_b


def _rope_fwd(qf, kv, proj, lay, tables):
    t = qf.shape[0]
    tm = _tile(t, 256, 8)
    hq = N_HEADS * QK_PAD

    def body(q_ref, kn_ref, kr_ref, cos_ref, sa_ref, sb_ref, qr_ref, kf_ref):
        cos_t, sin_a, sin_b = cos_ref[...], sa_ref[...], sb_ref[...]
        kr = _rope(kr_ref[...].astype(F32), cos_t, sin_a, sin_b).astype(BF16)
        for h in range(N_HEADS):
            lo = h * QK_PAD
            qr_ref[:, lo:lo + NOPE_DIM] = q_ref[:, lo:lo + NOPE_DIM]
            qr_ref[:, lo + NOPE_DIM:lo + QK_PAD] = _rope(
                q_ref[:, lo + NOPE_DIM:lo + QK_PAD].astype(F32), cos_t, sin_a, sin_b).astype(BF16)
            kf_ref[:, lo:lo + NOPE_DIM] = kn_ref[:, h * NOPE_DIM:(h + 1) * NOPE_DIM]
            kf_ref[:, lo + NOPE_DIM:lo + QK_PAD] = kr

    tab = _rowspec(tm, LANES)
    return pl.pallas_call(
        body, out_shape=(SDS((t, hq), BF16), SDS((t, hq), BF16)), grid=(t // tm,),
        in_specs=[_rowspec(tm, hq), _rowspec(tm, N_HEADS * NOPE_DIM), _colspec(tm, LANES, lay["kr"]), tab, tab, tab],
        out_specs=(_rowspec(tm, hq), _rowspec(tm, hq)), compiler_params=_cparams(("parallel",)),
        name="rope_fwd")(qf, kv, proj, *tables)


def _rope_bwd(dqr, dkf, dv, tables):
    t = dqr.shape[0]
    tm = _tile(t, 256, 8)
    hq = N_HEADS * QK_PAD
    hn = N_HEADS * NOPE_DIM

    def body(dq_ref, dk_ref, dv_ref, cos_ref, sa_ref, sb_ref, dqf_ref, dkv_ref, dkr_ref):
        cos_t, sin_a, sin_b = cos_ref[...], -sa_ref[...], -sb_ref[...]
        dkr = jnp.zeros((tm, LANES), F32)
        for h in range(N_HEADS):
            lo = h * QK_PAD
            dqf_ref[:, lo:lo + NOPE_DIM] = dq_ref[:, lo:lo + NOPE_DIM].astype(BF16)
            dqf_ref[:, lo + NOPE_DIM:lo + QK_PAD] = _rope(
                dq_ref[:, lo + NOPE_DIM:lo + QK_PAD].astype(F32), cos_t, sin_a, sin_b).astype(BF16)
            dkv_ref[:, h * NOPE_DIM:(h + 1) * NOPE_DIM] = dk_ref[:, lo:lo + NOPE_DIM]
            dkr = dkr + dk_ref[:, lo + NOPE_DIM:lo + QK_PAD].astype(F32)
        dkv_ref[:, hn:] = dv_ref[...]
        dkr_ref[...] = _rope(dkr, cos_t, sin_a, sin_b).astype(BF16)

    tab = _rowspec(tm, LANES)
    return pl.pallas_call(
        body, out_shape=(SDS((t, hq), BF16), SDS((t, 2 * hn), BF16), SDS((t, LANES), BF16)), grid=(t // tm,),
        in_specs=[_rowspec(tm, hq), _rowspec(tm, hq), _rowspec(tm, hn), tab, tab, tab],
        out_specs=(_rowspec(tm, hq), _rowspec(tm, 2 * hn), tab), compiler_params=_cparams(("parallel",)),
        name="rope_bwd")(dqr, dkf, dv, *tables)


def _chunk_mask(row0, rows, cols):
    qc = (row0 + lax.broadcasted_iota(jnp.int32, (rows, cols), 0)) // CHUNK
    kc = lax.broadcasted_iota(jnp.int32, (rows, cols), 1) // CHUNK
    return kc <= qc


_NT = (((1,), (1,)), ((), ()))
_TN = (((0,), (0,)), ((), ()))
LOG2E = 1.4426950408889634
EXP2_SCALE = ATTN_SCALE * LOG2E
ROW_SPLIT = 2


def _attn_block(t):
    return 512 if t >= 2048 else 128


def _attn_fwd(qr, kf, kv, gather=None):
    t = qr.shape[0]
    bq = _attn_block(t)
    nq = t // bq
    rs = bq // ROW_SPLIT
    bufs, layer = gather if gather is not None else ((), None)
    nw = len(bufs)

    def body(*refs):
        q_ref, k_ref, v_ref = refs[:3]
        o_ref, lse_ref = refs[3 + nw:5 + nw]
        buf_refs = refs[5 + nw:5 + 2 * nw]
        vaug_ref = refs[5 + 2 * nw]
        sems = refs[6 + 2 * nw:]
        h, i = pl.program_id(0), pl.program_id(1)

        if nw:
            @pl.when((h == 0) & (i == 0))
            def _():
                _gather_start(buf_refs, layer, *sems)

        @pl.when(i == 0)
        def _():
            vaug_ref[:, :V_DIM] = v_ref[...]
            vaug_ref[:, V_DIM:] = jnp.ones((t, V_DIM), BF16)

        def block(j, carry, masked):
            off = pl.multiple_of(j * bq, bq)
            k = k_ref[pl.ds(off, bq), :]
            va = vaug_ref[pl.ds(off, bq), :]
            new = []
            for u in range(ROW_SPLIT):
                m, acc = carry[u]
                s = lax.dot_general(q_ref[u * rs:(u + 1) * rs, :], k, _NT, preferred_element_type=F32)
                if masked:
                    s = jnp.where(_chunk_mask(u * rs, rs, bq), s, NEG_BIG)
                m_new = jnp.maximum(m, jnp.max(s, axis=1, keepdims=True))
                alpha = jnp.exp2((m - m_new) * EXP2_SCALE)
                pr = jnp.exp2((s - m_new) * EXP2_SCALE)
                acc = alpha * acc + jnp.dot(pr.astype(BF16), va, preferred_element_type=F32)
                new.append((m_new, acc))
            return tuple(new)

        init = tuple((jnp.full((rs, 1), NEG_BIG, F32), jnp.zeros((rs, 2 * V_DIM), F32)) for _ in range(ROW_SPLIT))
        carry = lax.fori_loop(0, i, lambda j, c: block(j, c, False), init)
        carry = block(i, carry, True)
        for u in range(ROW_SPLIT):
            m, acc = carry[u]
            l = acc[:, V_DIM:V_DIM + 1]
            o_ref[u * rs:(u + 1) * rs, :] = (acc[:, :V_DIM] / l).astype(BF16)
            lse_ref[0, u * rs:(u + 1) * rs, :] = jnp.broadcast_to(m * ATTN_SCALE + jnp.log(l), (rs, LANES))

        if nw:
            @pl.when((h == N_HEADS - 1) & (i == nq - 1))
            def _():
                _gather_finish(buf_refs, layer, *sems)

    out_shape = (SDS((t, N_HEADS * V_DIM), BF16), SDS((N_HEADS, t, LANES), F32)) + tuple(SDS(b.shape, b.dtype) for b in bufs)
    sem_shapes = [pltpu.SemaphoreType.DMA((6 * nw,)), pltpu.SemaphoreType.DMA((6 * nw,))] if nw else []
    outs = pl.pallas_call(
        body, out_shape=out_shape, grid=(N_HEADS, nq),
        in_specs=[pl.BlockSpec((bq, QK_PAD), lambda h, i: (i, h)), pl.BlockSpec((t, QK_PAD), lambda h, i: (0, h)),
                  pl.BlockSpec((t, V_DIM), lambda h, i: (0, N_HEADS + h))] + [ANY] * nw,
        out_specs=(pl.BlockSpec((bq, V_DIM), lambda h, i: (i, h)),
                   pl.BlockSpec((1, bq, LANES), lambda h, i: (h, i, 0))) + (ANY,) * nw,
        input_output_aliases={3 + n: 2 + n for n in range(nw)},
        scratch_shapes=[pltpu.VMEM((t, 2 * V_DIM), BF16)] + sem_shapes,
        compiler_params=_cparams(("arbitrary", "arbitrary")), name="attn_fwd_gather" if nw else "attn_fwd")(qr, kf, kv, *bufs)
    return outs[0], outs[1], list(outs[2:])


def _attn_delta(do, o):
    t = do.shape[0]
    tm = _tile(t, 512, 8)

    def body(do_ref, o_ref, dl_ref):
        prod = do_ref[...].astype(F32) * o_ref[...].astype(F32)
        for h in range(N_HEADS):
            s = jnp.sum(prod[:, h * V_DIM:(h + 1) * V_DIM], axis=1, keepdims=True)
            dl_ref[h] = jnp.broadcast_to(s, (tm, LANES))

    return pl.pallas_call(
        body, out_shape=SDS((N_HEADS, t, LANES), F32), grid=(t // tm,),
        in_specs=[_rowspec(tm, N_HEADS * V_DIM), _rowspec(tm, N_HEADS * V_DIM)],
        out_specs=pl.BlockSpec((N_HEADS, tm, LANES), lambda i: (0, i, 0)), compiler_params=_cparams(("parallel",)),
        name="attn_delta")(do, o)


def _attn_bwd(qr, kf, kv, do, lse, delta, exchange=None):
    t = qr.shape[0]
    bq = _attn_block(t)
    nq = t // bq
    rs = bq // ROW_SPLIT
    pbs, lands = exchange if exchange is not None else ((), ())
    nw = len(pbs)

    def body(*refs):
        k_ref, v_ref, q_ref, do_ref, lse_ref, dl_ref = refs[:6]
        pb_refs = refs[6:6 + nw]
        dk_ref, dv_ref, dq_ref = refs[6 + 2 * nw:9 + 2 * nw]
        land_refs = refs[9 + 2 * nw:9 + 3 * nw]
        sems = refs[9 + 3 * nw:]
        h, j = pl.program_id(0), pl.program_id(1)

        if nw:
            @pl.when((h == 0) & (j == 0))
            def _():
                _exchange_start(pb_refs, land_refs, *sems)

        @pl.when(j == 0)
        def _():
            dq_ref[...] = jnp.zeros_like(dq_ref)

        k = k_ref[...]
        v = v_ref[...]

        def block(i, carry, masked):
            dk, dv = carry
            for u in range(ROW_SPLIT):
                off = pl.multiple_of(i * bq + u * rs, rs)
                q = q_ref[pl.ds(off, rs), :]
                do_ = do_ref[pl.ds(off, rs), :]
                lse2 = lse_ref[0, pl.ds(off, rs), :][:, :1] * LOG2E
                dl_i = dl_ref[0, pl.ds(off, rs), :][:, :1]
                s = lax.dot_general(q, k, _NT, preferred_element_type=F32)
                if masked:
                    s = jnp.where(_chunk_mask(u * rs, rs, bq), s, NEG_BIG)
                pr = jnp.exp2(s * EXP2_SCALE - lse2)
                dv = dv + lax.dot_general(pr.astype(BF16), do_, _TN, preferred_element_type=F32)
                dp = lax.dot_general(do_, v, _NT, preferred_element_type=F32)
                ds = (pr * (dp - dl_i)).astype(BF16)
                dk = dk + lax.dot_general(ds, q, _TN, preferred_element_type=F32)
                dq_ref[pl.ds(off, rs), :] += jnp.dot(ds, k, preferred_element_type=F32) * ATTN_SCALE
            return dk, dv

        carry = block(j, (jnp.zeros((bq, QK_PAD), F32), jnp.zeros((bq, V_DIM), F32)), True)
        dk, dv = lax.fori_loop(j + 1, nq, lambda i, c: block(i, c, False), carry)
        dk_ref[...] = (dk * ATTN_SCALE).astype(BF16)
        dv_ref[...] = dv.astype(BF16)

        if nw:
            @pl.when((h == N_HEADS - 1) & (j == nq - 1))
            def _():
                _exchange_finish(pb_refs, land_refs, *sems)

    stat = pl.BlockSpec((1, t, LANES), lambda h, j: (h, 0, 0))
    out_shape = (SDS((t, N_HEADS * QK_PAD), BF16), SDS((t, N_HEADS * V_DIM), BF16), SDS((t, N_HEADS * QK_PAD), F32))
    sem_shapes = [pltpu.SemaphoreType.DMA((3 * nw,)), pltpu.SemaphoreType.DMA((3 * nw,))] if nw else []
    outs = pl.pallas_call(
        body, out_shape=out_shape + tuple(SDS(l.shape, l.dtype) for l in lands), grid=(N_HEADS, nq),
        in_specs=[pl.BlockSpec((bq, QK_PAD), lambda h, j: (j, h)), pl.BlockSpec((bq, V_DIM), lambda h, j: (j, N_HEADS + h)),
                  pl.BlockSpec((t, QK_PAD), lambda h, j: (0, h)), pl.BlockSpec((t, V_DIM), lambda h, j: (0, h)), stat, stat]
        + [ANY] * (2 * nw),
        out_specs=(pl.BlockSpec((bq, QK_PAD), lambda h, j: (j, h)), pl.BlockSpec((bq, V_DIM), lambda h, j: (j, h)),
                   pl.BlockSpec((t, QK_PAD), lambda h, j: (0, h))) + (ANY,) * nw,
        input_output_aliases={6 + nw + n: 3 + n for n in range(nw)}, scratch_shapes=sem_shapes,
        compiler_params=_cparams(("arbitrary", "arbitrary")),
        name="attn_bwd_exchange" if nw else "attn_bwd")(kf, kv, qr, do, lse, delta, *pbs, *lands)
    return outs[0], outs[1], outs[2], list(outs[3:])


def _merge_fwd(proj, lay, ya, yb):
    t, d = ya.shape
    tm = _tile(t, 512, 8)

    def body(gc_ref, gm_ref, ya_ref, yb_ref, o_ref):
        o_ref[...] = (_sigmoid(gc_ref[...].astype(F32)) * ya_ref[...].astype(F32)
                      + _sigmoid(gm_ref[...].astype(F32)) * yb_ref[...].astype(F32)).astype(BF16)

    return pl.pallas_call(
        body, out_shape=SDS((t, d), BF16), grid=(t // tm,),
        in_specs=[_colspec(tm, d, lay["gc"]), _colspec(tm, d, lay["gm"]), _rowspec(tm, d), _rowspec(tm, d)],
        out_specs=_rowspec(tm, d), compiler_params=_cparams(("parallel",)), name="merge_fwd")(proj, proj, ya, yb)


def _merge_bwd(proj, lay, ya, yb, dmg):
    t, d = ya.shape
    tm = _tile(t, 512, 8)

    def body(gc_ref, gm_ref, ya_ref, yb_ref, d_ref, dya_ref, dyb_ref, dgc_ref, dgm_ref):
        dm = d_ref[...].astype(F32)
        sc = _sigmoid(gc_ref[...].astype(F32))
        sm = _sigmoid(gm_ref[...].astype(F32))
        dya_ref[...] = (dm * sc).astype(BF16)
        dyb_ref[...] = (dm * sm).astype(BF16)
        dgc_ref[...] = (dm * ya_ref[...].astype(F32) * (sc * (1.0 - sc))).astype(BF16)
        dgm_ref[...] = (dm * yb_ref[...].astype(F32) * (sm * (1.0 - sm))).astype(BF16)

    r = _rowspec(tm, d)
    return pl.pallas_call(
        body, out_shape=(SDS((t, d), BF16),) * 4, grid=(t // tm,),
        in_specs=[_colspec(tm, d, lay["gc"]), _colspec(tm, d, lay["gm"]), r, r, r], out_specs=(r, r, r, r),
        compiler_params=_cparams(("parallel",)), name="merge_bwd")(proj, proj, ya, yb, dmg)


def _ple_fwd(h, gp, pp):
    t, d = h.shape
    tm = _tile(t, 512, 8)

    def body(h_ref, gp_ref, pp_ref, o_ref):
        o_ref[...] = h_ref[...] + _sigmoid(gp_ref[...].astype(F32)) * pp_ref[...].astype(F32)

    r = _rowspec(tm, d)
    return pl.pallas_call(body, out_shape=SDS((t, d), F32), grid=(t // tm,), in_specs=[r, r, r], out_specs=r,
                          compiler_params=_cparams(("parallel",)), name="ple_fwd")(h, gp, pp)


def _ple_bwd(dh, gp, pp):
    t, d = dh.shape
    tm = _tile(t, 512, 8)

    def body(dh_ref, gp_ref, pp_ref, dpp_ref, dgp_ref):
        g = dh_ref[...]
        s = _sigmoid(gp_ref[...].astype(F32))
        dpp_ref[...] = (g * s).astype(BF16)
        dgp_ref[...] = (g * pp_ref[...].astype(F32) * (s * (1.0 - s))).astype(BF16)

    r = _rowspec(tm, d)
    return pl.pallas_call(body, out_shape=(SDS((t, d), BF16),) * 2, grid=(t // tm,), in_specs=[r, r, r],
                          out_specs=(r, r), compiler_params=_cparams(("parallel",)), name="ple_bwd")(dh, gp, pp)


def _adamw(w, g, m, v):
    shape = w.shape
    cols = shape[-1]
    rows = w.size // cols
    tr = _tile(rows, max(8, BLOCK_ELEMS // cols // 8 * 8), 8)

    def body(w_ref, g_ref, m_ref, v_ref, d_ref, nm_ref, nv_ref):
        g_ = g_ref[...]
        nm = ADAM_B1 * m_ref[...] + (1.0 - ADAM_B1) * g_
        nv = ADAM_B2 * v_ref[...] + (1.0 - ADAM_B2) * (g_ * g_)
        m_hat = nm / (1.0 - ADAM_B1 ** ADAM_STEP)
        v_hat = nv / (1.0 - ADAM_B2 ** ADAM_STEP)
        d_ref[...] = -ADAM_LR * (m_hat / (jnp.sqrt(v_hat) + ADAM_EPS) + ADAM_WD * w_ref[...])
        nm_ref[...] = nm
        nv_ref[...] = nv

    r = _rowspec(tr, cols)
    outs = pl.pallas_call(
        body, out_shape=(SDS((rows, cols), F32),) * 3, grid=(rows // tr,), in_specs=[r, r, r, r], out_specs=(r, r, r),
        compiler_params=_cparams(("parallel",)), name="adamw")(*(a.reshape(rows, cols) for a in (w, g, m, v)))
    return tuple(o.reshape(shape) for o in outs)


ANY = pl.BlockSpec(memory_space=pl.ANY)


def _place():
    x, y, c = lax.axis_index("x"), lax.axis_index("y"), lax.axis_index("c")
    return x, y, c, [(1 - x, y), (x, 1 - y), (1 - x, 1 - y)]


def _half_rows(rows, cols):
    half = rows // 2
    return half, _tile(half, max(16, BLOCK_ELEMS // cols // 16 * 16), 16)


def _my_chip():
    return 2 * lax.axis_index("x") + lax.axis_index("y")


def _cast_into_slot(w):
    nl, r, c = w.shape
    tr = _tile(r, max(16, BLOCK_ELEMS // c // 16 * 16), 16)

    def body(w_ref, o_ref):
        o_ref[...] = w_ref[...].astype(BF16)

    return pl.pallas_call(
        body, out_shape=SDS((nl, N_CHIPS, r, c), BF16), grid=(nl, r // tr),
        in_specs=[pl.BlockSpec((None, tr, c), lambda l, i: (l, i, 0))],
        out_specs=pl.BlockSpec((None, None, tr, c), lambda l, i: (l, _my_chip(), i, 0)),
        compiler_params=_cparams(("parallel", "parallel")), name="cast_into_slot")(w)


def _gather_copy(ref, layer, send_sems, recv_sems, sem, chip, half, to):
    r2 = ref.shape[2] // 2
    rows = ref.at[layer, chip, pl.ds(half * r2, r2)]
    return pltpu.make_async_remote_copy(src_ref=rows, dst_ref=rows, send_sem=send_sems.at[sem], recv_sem=recv_sems.at[sem],
                                        device_id=to, device_id_type=MESH)


def _gather_start(refs, layer, send_sems, recv_sems):
    x, y, c, chips = _place()
    for wi, ref in enumerate(refs):
        for n, chip in enumerate(chips):
            _gather_copy(ref, layer, send_sems, recv_sems, 6 * wi + n, 2 * x + y, c, (*chip, c)).start()


def _gather_finish(refs, layer, send_sems, recv_sems):
    x, y, c, chips = _place()
    me, sibling = (x, y, c), (x, y, 1 - c)
    for wi, ref in enumerate(refs):
        for n, chip in enumerate(chips):
            k = 2 * chip[0] + chip[1]
            _gather_copy(ref, layer, send_sems, recv_sems, 6 * wi + n, k, c, me).wait_recv()
            _gather_copy(ref, layer, send_sems, recv_sems, 6 * wi + 3 + n, k, c, sibling).start()
    for wi, ref in enumerate(refs):
        for n, chip in enumerate(chips):
            _gather_copy(ref, layer, send_sems, recv_sems, 6 * wi + 3 + n, 2 * chip[0] + chip[1], 1 - c, me).wait_recv()
    for wi, ref in enumerate(refs):
        for n, chip in enumerate(chips):
            _gather_copy(ref, layer, send_sems, recv_sems, 6 * wi + n, 2 * x + y, c, (*chip, c)).wait_send()
            _gather_copy(ref, layer, send_sems, recv_sems, 6 * wi + 3 + n, 2 * chip[0] + chip[1], c, sibling).wait_send()


def _all_gather_weights(bufs, layer):
    nw = len(bufs)

    def body(*refs):
        _gather_start(refs[nw:2 * nw], layer, refs[2 * nw], refs[2 * nw + 1])
        _gather_finish(refs[nw:2 * nw], layer, refs[2 * nw], refs[2 * nw + 1])

    return list(pl.pallas_call(
        body, out_shape=tuple(SDS(b.shape, b.dtype) for b in bufs), in_specs=[ANY] * nw, out_specs=(ANY,) * nw,
        input_output_aliases={i: i for i in range(nw)},
        scratch_shapes=[pltpu.SemaphoreType.DMA((6 * nw,)), pltpu.SemaphoreType.DMA((6 * nw,))],
        name="all_gather_weights")(*bufs))


def _pair_swap(grads):
    nw = len(grads)

    def body(*refs):
        ins, outs = refs[:nw], refs[nw:2 * nw]
        send_sems, recv_sems = refs[2 * nw], refs[2 * nw + 1]
        x, y, c, _ = _place()
        copies = []
        for wi, (g_ref, o_ref) in enumerate(zip(ins, outs)):
            r2 = g_ref.shape[1] // 2
            copies.append(pltpu.make_async_remote_copy(
                src_ref=g_ref.at[:, pl.ds((1 - c) * r2, r2)], dst_ref=o_ref, send_sem=send_sems.at[wi],
                recv_sem=recv_sems.at[wi], device_id=(x, y, 1 - c), device_id_type=MESH))
            copies[-1].start()
        for cp in copies:
            cp.wait()

    return pl.pallas_call(
        body, out_shape=tuple(SDS((N_CHIPS, g.shape[1] // 2, g.shape[2]), g.dtype) for g in grads),
        in_specs=[ANY] * nw, out_specs=(ANY,) * nw,
        scratch_shapes=[pltpu.SemaphoreType.DMA((nw,)), pltpu.SemaphoreType.DMA((nw,))], name="pair_swap")(*grads)


def _pair_add(g, other):
    _, r, c = g.shape
    r2, tr = _half_rows(r, c)
    nb = r2 // tr

    def body(g_ref, o_ref, pb_ref, land_ref):
        total = (g_ref[...].astype(F32) + o_ref[...].astype(F32)).astype(BF16)
        pb_ref[...] = total

        @pl.when(pl.program_id(1) == _my_chip())
        def _():
            land_ref[...] = total

    blk = pl.BlockSpec((None, tr, c), lambda j, k: (k, j, 0))
    return pl.pallas_call(
        body, out_shape=(SDS((N_CHIPS, r2, c), BF16),) * 2, grid=(nb, N_CHIPS),
        in_specs=[pl.BlockSpec((None, tr, c), lambda j, k: (k, lax.axis_index("c") * nb + j, 0)), blk],
        out_specs=(blk, pl.BlockSpec((None, tr, c), lambda j, k: (_my_chip(), j, 0))),
        compiler_params=_cparams(("parallel", "arbitrary")), name="pair_add")(g, other)


def _exchange_copy(p_ref, l_ref, send_sems, recv_sems, sem, src_slot, dst_slot, to):
    return pltpu.make_async_remote_copy(src_ref=p_ref.at[src_slot], dst_ref=l_ref.at[dst_slot], send_sem=send_sems.at[sem],
                                        recv_sem=recv_sems.at[sem], device_id=to, device_id_type=MESH)


def _exchange_start(p_refs, l_refs, send_sems, recv_sems):
    x, y, c, chips = _place()
    for wi, (p_ref, l_ref) in enumerate(zip(p_refs, l_refs)):
        for n, chip in enumerate(chips):
            _exchange_copy(p_ref, l_ref, send_sems, recv_sems, 3 * wi + n, 2 * chip[0] + chip[1], 2 * x + y, (*chip, c)).start()


def _exchange_finish(p_refs, l_refs, send_sems, recv_sems):
    x, y, c, chips = _place()
    for wi, (p_ref, l_ref) in enumerate(zip(p_refs, l_refs)):
        for n, chip in enumerate(chips):
            _exchange_copy(p_ref, l_ref, send_sems, recv_sems, 3 * wi + n, 2 * x + y, 2 * chip[0] + chip[1], (x, y, c)).wait_recv()
    for wi, (p_ref, l_ref) in enumerate(zip(p_refs, l_refs)):
        for n, chip in enumerate(chips):
            _exchange_copy(p_ref, l_ref, send_sems, recv_sems, 3 * wi + n, 2 * chip[0] + chip[1], 2 * x + y, (*chip, c)).wait_send()


def _chip_all_to_all(pbs, lands):
    nw = len(pbs)

    def body(*refs):
        _exchange_start(refs[:nw], refs[2 * nw:3 * nw], refs[3 * nw], refs[3 * nw + 1])
        _exchange_finish(refs[:nw], refs[2 * nw:3 * nw], refs[3 * nw], refs[3 * nw + 1])

    return list(pl.pallas_call(
        body, out_shape=tuple(SDS(l.shape, l.dtype) for l in lands), in_specs=[ANY] * (2 * nw), out_specs=(ANY,) * nw,
        input_output_aliases={nw + i: i for i in range(nw)},
        scratch_shapes=[pltpu.SemaphoreType.DMA((3 * nw,)), pltpu.SemaphoreType.DMA((3 * nw,))],
        name="chip_all_to_all")(*pbs, *lands))


def _sum_chips(land, gstack, layer):
    _, r, c = gstack.shape
    r2, tr = _half_rows(r, c)
    nb = r2 // tr

    def body(l_ref, g_ref, out_ref):
        out_ref[...] = ((l_ref[0].astype(F32) + l_ref[1].astype(F32)) + l_ref[2].astype(F32)) + l_ref[3].astype(F32)

    return pl.pallas_call(
        body, out_shape=SDS(gstack.shape, F32), grid=(nb,),
        in_specs=[pl.BlockSpec((N_CHIPS, tr, c), lambda j: (0, j, 0)), ANY],
        out_specs=pl.BlockSpec((None, tr, c), lambda j: (layer, lax.axis_index("c") * nb + j, 0)),
        input_output_aliases={1: 0}, compiler_params=_cparams(("parallel",)), name="sum_chips")(land, gstack)


def _pair_gather(gstacks, layer):
    nw = len(gstacks)

    def body(*refs):
        outs = refs[nw:2 * nw]
        send_sems, recv_sems = refs[2 * nw], refs[2 * nw + 1]
        x, y, c, _ = _place()

        def copy(ref, wi, half):
            r2 = ref.shape[1] // 2
            blk = ref.at[layer, pl.ds(half * r2, r2)]
            return pltpu.make_async_remote_copy(src_ref=blk, dst_ref=blk, send_sem=send_sems.at[wi],
                                                recv_sem=recv_sems.at[wi], device_id=(x, y, 1 - c), device_id_type=MESH)

        sent = [copy(ref, wi, c) for wi, ref in enumerate(outs)]
        for cp in sent:
            cp.start()
        for wi, ref in enumerate(outs):
            copy(ref, wi, 1 - c).wait_recv()
        for cp in sent:
            cp.wait_send()

    return pl.pallas_call(
        body, out_shape=tuple(SDS(g.shape, g.dtype) for g in gstacks), in_specs=[ANY] * nw, out_specs=(ANY,) * nw,
        input_output_aliases={i: i for i in range(nw)},
        scratch_shapes=[pltpu.SemaphoreType.DMA((nw,)), pltpu.SemaphoreType.DMA((nw,))], name="pair_gather")(*gstacks)


def _all_gather_small(vec, name):
    rows, w = vec.shape

    def body(v_ref, sum_ref, all_ref, send_sems, recv_sems):
        x, y, c, chips = _place()
        me, sibling = (x, y, c), (x, y, 1 - c)

        def slot(px, py, pc):
            return all_ref.at[4 * px + 2 * py + pc]

        def copy(k, block, to, src=None):
            return pltpu.make_async_remote_copy(
                src_ref=slot(*block) if src is None else src, dst_ref=slot(*block), send_sem=send_sems.at[k],
                recv_sem=recv_sems.at[k], device_id=to, device_id_type=MESH)

        first = [copy(0, me, sibling, src=v_ref)]
        first += [copy(1 + n, me, (*chip, c), src=v_ref) for n, chip in enumerate(chips)]
        for cp in first:
            cp.start()
        slot(*me)[...] = v_ref[...]
        passed = [copy(4 + n, (*chip, c), sibling) for n, chip in enumerate(chips)]
        for n, chip in enumerate(chips):
            copy(1 + n, (*chip, c), me).wait_recv()
            passed[n].start()
        copy(0, sibling, me).wait_recv()
        for n, chip in enumerate(chips):
            copy(4 + n, (*chip, 1 - c), me).wait_recv()
        for cp in first + passed:
            cp.wait_send()
        total = all_ref[0]
        for dev in range(1, 8):
            total = total + all_ref[dev]
        sum_ref[...] = total

    vm = pl.BlockSpec(memory_space=pltpu.VMEM)
    return pl.pallas_call(
        body, out_shape=(SDS((rows, w), F32), SDS((8, rows, w), F32)), in_specs=[vm], out_specs=(vm, vm),
        scratch_shapes=[pltpu.SemaphoreType.DMA((7,)), pltpu.SemaphoreType.DMA((7,))], name=name)(vec)


def _to_rows128(flat):
    n = flat.shape[0]
    rows = -(-n // (8 * LANES)) * 8
    return jnp.pad(flat, (0, rows * LANES - n)).reshape(rows, LANES)


def _in_layout(conv, ql, kvl, d):
    lay = {"conv": conv, "ql": ql, "kvl": kvl, "d": d}
    lay["q"] = 3 * conv
    lay["kr"] = lay["q"] + ql
    lay["gc"] = lay["kr"] + LANES
    lay["gm"] = lay["gc"] + d
    lay["kv"] = lay["gm"] + d
    used = lay["kv"] + kvl
    lay["width"] = -(-used // 512) * 512
    return lay


def _w_in_to_layout(w, lay):
    conv, ql, kvl, d = lay["conv"], lay["ql"], lay["kvl"], lay["d"]
    o_kv = 3 * conv + ql
    o_kr = o_kv + kvl
    o_g = o_kr + ROPE_DIM
    lead = w.shape[:-1]
    parts = [w[..., :o_kv], w[..., o_kr:o_g], jnp.zeros(lead + (LANES - ROPE_DIM,), w.dtype), w[..., o_g:o_g + 2 * d],
             w[..., o_kv:o_kr], jnp.zeros(lead + (lay["width"] - lay["kv"] - kvl,), w.dtype)]
    return jnp.concatenate(parts, axis=-1)


def _w_in_from_layout(g, lay):
    ql, kvl, d = lay["ql"], lay["kvl"], lay["d"]
    return jnp.concatenate([g[:, :lay["q"] + ql], g[:, lay["kv"]:lay["kv"] + kvl], g[:, lay["kr"]:lay["kr"] + ROPE_DIM],
                            g[:, lay["gc"]:lay["gc"] + 2 * d]], axis=1)


def _w_uq_to_layout(w):
    r = w.shape[0]
    w3 = w.reshape(r, N_HEADS, NOPE_DIM + ROPE_DIM)
    return jnp.pad(w3, ((0, 0), (0, 0), (0, QK_PAD - NOPE_DIM - ROPE_DIM))).reshape(r, N_HEADS * QK_PAD)


def _w_uq_from_layout(g):
    r = g.shape[0]
    return g.reshape(r, N_HEADS, QK_PAD)[:, :, :NOPE_DIM + ROPE_DIM].reshape(r, N_HEADS * (NOPE_DIM + ROPE_DIM))


def _w_ukv_to_layout(w):
    r = w.shape[0]
    return w.reshape(r, N_HEADS, 2, NOPE_DIM).transpose(0, 2, 1, 3).reshape(r, 2 * N_HEADS * NOPE_DIM)


def _w_ukv_from_layout(g):
    r = g.shape[0]
    return g.reshape(r, 2, N_HEADS, NOPE_DIM).transpose(0, 2, 1, 3).reshape(r, 2 * N_HEADS * NOPE_DIM)


def _chips_to_cols(buf):
    _, r, c = buf.shape
    return buf.transpose(1, 0, 2).reshape(r, N_CHIPS * c)


def _cols_to_chips(g):
    r, c4 = g.shape
    return g.reshape(r, N_CHIPS, c4 // N_CHIPS).transpose(1, 0, 2)


GATE_UP = ("ffn1_w_gu", "ffn2_w_gu")
SMALL = ("w_in", "w_conv_out", "w_uq", "w_ukv", "w_ple_proj")


def _big_views(bufs):
    views = {}
    for (name, axis), buf in zip(SHARDED, bufs):
        if axis == 0:
            views[name] = buf.reshape(buf.shape[0], N_CHIPS * buf.shape[2], buf.shape[3])
        elif name in GATE_UP:
            views[name] = buf
    return views


def _small_weights(bufs, layer, lay):
    by_name = dict(zip(BIG, bufs))
    cols = {name: _chips_to_cols(by_name[name][layer]) for name in SMALL}
    cols["w_in"] = _w_in_to_layout(cols["w_in"], lay)
    cols["w_uq"] = _w_uq_to_layout(cols["w_uq"])
    cols["w_ukv"] = _w_ukv_to_layout(cols["w_ukv"])
    return cols


def _ffn_fwd(h, gain, w_gu, w_down, layer, tag):
    n = _rmsnorm_fwd(h, gain, tag + "_norm_fwd")
    gu = _mm(n, w_gu, "nn", BF16, tag + "_gu_fwd", layer=layer, b_chip=True)
    a = _swiglu_fwd(gu)
    out = _mm(a, w_down, "nn", F32, tag + "_down_fwd", scale=0.5, res=h, layer=layer)
    return out, (h, n, gu, a)


def _ffn_bwd(dh, dhb, saved, gain, w_gu, w_down, layer, tag):
    h, n, gu, a = saved
    d_wdown = _mm(a, dhb, "tn", BF16, tag + "_down_dw", scale=0.5)
    da = _mm(dhb, w_down, "nt", BF16, tag + "_down_dx", scale=0.5, layer=layer)
    dgu = _swiglu_bwd(gu, da)
    d_wgu = _mm(n, dgu, "tn", BF16, tag + "_gu_dw", out_chip=True)
    dn = _mm(dgu, w_gu, "nt", BF16, tag + "_gu_dx", layer=layer, b_chip=True)
    dh, dhb, dgain = _rmsnorm_bwd(h, gain, dn, dh, tag + "_norm_bwd")
    return dh, dhb, d_wgu, d_wdown, dgain


def _layer_fwd(h0, p_i, bufs, small, conv_w, norms, layer, lay, tables, gather_next):
    big = _big_views(bufs)
    h1, s_ffn1 = _ffn_fwd(h0, norms["ffn1_norm"], big["ffn1_w_gu"], big["ffn1_w_down"], layer, "ffn1")
    n2 = _rmsnorm_fwd(h1, norms["mix_norm"], "mix_norm_fwd")
    proj = _mm(n2, small["w_in"], "nn", BF16, "in_fwd")
    cb = _conv_fwd(proj, conv_w)
    ya = _mm(cb, small["w_conv_out"], "nn", BF16, "conv_out_fwd")
    qn, kvn = _qkvnorm_fwd(proj, lay, norms["q_norm"], norms["kv_norm"])
    qf = _mm(qn, small["w_uq"], "nn", BF16, "uq_fwd")
    kv = _mm(kvn, small["w_ukv"], "nn", BF16, "ukv_fwd")
    qr, kf = _rope_fwd(qf, kv, proj, lay, tables)
    o, lse, new_bufs = _attn_fwd(qr, kf, kv, gather=(bufs, layer + 1) if gather_next else None)
    if gather_next:
        bufs = new_bufs
        big = _big_views(bufs)
    yb = _mm(o, big["w_mla_out"], "nn", BF16, "mla_out_fwd", layer=layer)
    mg = _merge_fwd(proj, lay, ya, yb)
    h2 = _mm(mg, big["w_o"], "nn", F32, "o_fwd", res=h1, layer=layer)
    h3, s_ffn2 = _ffn_fwd(h2, norms["ffn2_norm"], big["ffn2_w_gu"], big["ffn2_w_down"], layer, "ffn2")
    n4 = _rmsnorm_fwd(h3, norms["ple_norm"], "ple_norm_fwd")
    gp = _mm(n4, big["w_ple_gate"], "nn", BF16, "ple_gate_fwd", layer=layer)
    pp = _mm(p_i, small["w_ple_proj"], "nn", BF16, "ple_proj_fwd")
    h4 = _ple_fwd(h3, gp, pp)
    saved = dict(s_ffn1=s_ffn1, h1=h1, n2=n2, proj=proj, cb=cb, ya=ya, qn=qn, kvn=kvn, qr=qr, kf=kf, kv=kv, o=o,
                 lse=lse, yb=yb, mg=mg, h2=h2, s_ffn2=s_ffn2, h3=h3, n4=n4, gp=gp, pp=pp, p=p_i)
    return h4, saved, bufs


def _layer_bwd(dh, s, big, small, conv_w, norms, layer, lay, tables, exchange):
    gw, gn = {}, {}

    def by_rows(g):
        return g.reshape(N_CHIPS, g.shape[0] // N_CHIPS, g.shape[1])

    dpp, dgp = _ple_bwd(dh, s["gp"], s["pp"])
    gw["w_ple_proj"] = _cols_to_chips(_mm(s["p"], dpp, "tn", BF16, "ple_proj_dw"))
    gw["w_ple_gate"] = by_rows(_mm(s["n4"], dgp, "tn", BF16, "ple_gate_dw"))
    dn4 = _mm(dgp, big["w_ple_gate"], "nt", BF16, "ple_gate_dx", layer=layer)
    dh, dhb, gn["ple_norm"] = _rmsnorm_bwd(s["h3"], norms["ple_norm"], dn4, dh, "ple_norm_bwd")
    dh, dhb, gw["ffn2_w_gu"], g_down, gn["ffn2_norm"] = _ffn_bwd(
        dh, dhb, s["s_ffn2"], norms["ffn2_norm"], big["ffn2_w_gu"], big["ffn2_w_down"], layer, "ffn2")
    gw["ffn2_w_down"] = by_rows(g_down)
    gw["w_o"] = by_rows(_mm(s["mg"], dhb, "tn", BF16, "o_dw"))
    dmg = _mm(dhb, big["w_o"], "nt", BF16, "o_dx", layer=layer)
    dya, dyb, dgc, dgm = _merge_bwd(s["proj"], lay, s["ya"], s["yb"], dmg)
    gw["w_conv_out"] = _cols_to_chips(_mm(s["cb"], dya, "tn", BF16, "conv_out_dw"))
    dcb = _mm(dya, small["w_conv_out"], "nt", BF16, "conv_out_dx")
    db, dc, dv_conv, g_conv = _conv_bwd(s["proj"], conv_w, dcb)
    gw["w_mla_out"] = by_rows(_mm(s["o"], dyb, "tn", BF16, "mla_out_dw"))
    do = _mm(dyb, big["w_mla_out"], "nt", BF16, "mla_out_dx", layer=layer)
    delta = _attn_delta(do, s["o"])
    dkf, dv, dqr, landed = _attn_bwd(s["qr"], s["kf"], s["kv"], do, s["lse"], delta, exchange=exchange)
    dqf, dkv, dkr = _rope_bwd(dqr, dkf, dv, tables)
    gw["w_uq"] = _cols_to_chips(_w_uq_from_layout(_mm(s["qn"], dqf, "tn", BF16, "uq_dw")))
    dqn = _mm(dqf, small["w_uq"], "nt", BF16, "uq_dx")
    gw["w_ukv"] = _cols_to_chips(_w_ukv_from_layout(_mm(s["kvn"], dkv, "tn", BF16, "ukv_dw")))
    dkvn = _mm(dkv, small["w_ukv"], "nt", BF16, "ukv_dx")
    dqc, dkvc, gn["q_norm"], gn["kv_norm"] = _qkvnorm_bwd(s["proj"], lay, norms["q_norm"], norms["kv_norm"], dqn, dkvn)
    t = dh.shape[0]
    dproj = jnp.concatenate([db, dc, dv_conv, dqc, dkr, dgc, dgm, dkvc,
                             jnp.zeros((t, lay["width"] - lay["kv"] - lay["kvl"]), BF16)], axis=1)
    gw["w_in"] = _cols_to_chips(_w_in_from_layout(_mm(s["n2"], dproj, "tn", BF16, "in_dw"), lay))
    dn2 = _mm(dproj, small["w_in"], "nt", BF16, "in_dx")
    dh, dhb, gn["mix_norm"] = _rmsnorm_bwd(s["h1"], norms["mix_norm"], dn2, dh, "mix_norm_bwd")
    dh, dhb, gw["ffn1_w_gu"], g_down, gn["ffn1_norm"] = _ffn_bwd(
        dh, dhb, s["s_ffn1"], norms["ffn1_norm"], big["ffn1_w_gu"], big["ffn1_w_down"], layer, "ffn1")
    gw["ffn1_w_down"] = by_rows(g_down)
    return dh, gw, g_conv, gn, landed


def _rope_tables(positions):
    half = ROPE_DIM // 2
    inv_freq = ROPE_THETA ** (-jnp.arange(0, ROPE_DIM, 2, dtype=F32) / ROPE_DIM)
    ang = positions.astype(F32)[:, None] * inv_freq
    cos, sin = jnp.cos(ang), jnp.sin(ang)
    zeros = jnp.zeros_like(cos)
    cos_t = jnp.concatenate([cos, cos, zeros, zeros], axis=1)
    sin_a = jnp.concatenate([-sin, zeros, zeros, zeros], axis=1)
    sin_b = jnp.concatenate([zeros, sin, zeros, zeros], axis=1)
    assert cos_t.shape[1] == LANES and half * 4 == LANES
    return cos_t, sin_a, sin_b


def kernel(x, p, positions, ffn1_norm, ffn1_w_gu, ffn1_w_down, mix_norm, w_in, conv_w, w_conv_out, q_norm, kv_norm, w_uq, w_ukv, w_mla_out, w_o, ffn2_norm, ffn2_w_gu, ffn2_w_down, ple_norm, w_ple_gate, w_ple_proj, final_norm, loss_target, m_ffn1_norm, m_ffn1_w_gu, m_ffn1_w_down, m_mix_norm, m_w_in, m_conv_w, m_w_conv_out, m_q_norm, m_kv_norm, m_w_uq, m_w_ukv, m_w_mla_out, m_w_o, m_ffn2_norm, m_ffn2_w_gu, m_ffn2_w_down, m_ple_norm, m_w_ple_gate, m_w_ple_proj, m_final_norm, v_ffn1_norm, v_ffn1_w_gu, v_ffn1_w_down, v_mix_norm, v_w_in, v_conv_w, v_w_conv_out, v_q_norm, v_kv_norm, v_w_uq, v_w_ukv, v_w_mla_out, v_w_o, v_ffn2_norm, v_ffn2_w_gu, v_ffn2_w_down, v_ple_norm, v_w_ple_gate, v_w_ple_proj, v_final_norm):
    args = dict(zip(ARG_NAMES, (x, p, positions, ffn1_norm, ffn1_w_gu, ffn1_w_down, mix_norm, w_in, conv_w, w_conv_out, q_norm, kv_norm, w_uq, w_ukv, w_mla_out, w_o, ffn2_norm, ffn2_w_gu, ffn2_w_down, ple_norm, w_ple_gate, w_ple_proj, final_norm, loss_target, m_ffn1_norm, m_ffn1_w_gu, m_ffn1_w_down, m_mix_norm, m_w_in, m_conv_w, m_w_conv_out, m_q_norm, m_kv_norm, m_w_uq, m_w_ukv, m_w_mla_out, m_w_o, m_ffn2_norm, m_ffn2_w_gu, m_ffn2_w_down, m_ple_norm, m_w_ple_gate, m_w_ple_proj, m_final_norm, v_ffn1_norm, v_ffn1_w_gu, v_ffn1_w_down, v_mix_norm, v_w_in, v_conv_w, v_w_conv_out, v_q_norm, v_kv_norm, v_w_uq, v_w_ukv, v_w_mla_out, v_w_o, v_ffn2_norm, v_ffn2_w_gu, v_ffn2_w_down, v_ple_norm, v_w_ple_gate, v_w_ple_proj, v_final_norm)))
    depth = ffn1_norm.shape[0]
    t, d = x.shape[1], x.shape[2]
    conv = conv_w.shape[-1] * N_CHIPS
    lay = _in_layout(conv, q_norm.shape[-1], kv_norm.shape[-1], d)
    chip = 2 * lax.axis_index("x") + lax.axis_index("y")
    tables = _rope_tables(positions[0])

    bufs = _all_gather_weights([_cast_into_slot(args[name]) for name in BIG], 0)
    small = [None] * depth
    small[0] = _small_weights(bufs, 0, lay)
    conv_rows = depth * conv_w.shape[1]
    conv_all = _all_gather_small(_to_rows128(conv_w.reshape(-1)), "all_gather_conv_w")[1]
    conv_full = conv_all[0::2, :conv_rows].reshape(N_CHIPS, depth, conv_w.shape[1], LANES)
    conv_full = conv_full.transpose(1, 2, 0, 3).reshape(depth, conv_w.shape[1], conv)
    norms = [{name: args[name][i] for name in REPLICATED} for i in range(depth)]
    p3 = p.reshape(depth, t, p.shape[-1])

    h = x[0]
    saved = []
    for i in range(depth):
        h, s, bufs = _layer_fwd(h, p3[i], bufs, small[i], conv_full[i], norms[i], i, lay, tables, i + 1 < depth)
        if i + 1 < depth:
            small[i + 1] = _small_weights(bufs, i + 1, lay)
        saved.append(s)
    loss_part, dh, _, g_final = _loss_head(h, final_norm, loss_target[0])
    loss = lax.psum(loss_part[0, 0], ("x", "y", "c"))

    big = _big_views(bufs)
    gstacks = [lax.empty(args[name].shape, F32) for name in BIG]
    norm_grads, conv_grads = [None] * depth, [None] * depth

    def finish(layer, lands, stacks):
        stacks = [_sum_chips(land, gs, layer) for land, gs in zip(lands, stacks)]
        return _pair_gather(stacks, layer)

    pending = None
    for i in reversed(range(depth)):
        dh, gw, conv_grads[i], norm_grads[i], landed = _layer_bwd(
            dh, saved[i], big, small[i], conv_full[i], norms[i], i, lay, tables,
            exchange=None if pending is None else pending[1:])
        if pending is not None:
            gstacks = finish(pending[0], landed, gstacks)
        grads_i = [gw[name] for name in BIG]
        pairs = [_pair_add(g, o) for g, o in zip(grads_i, _pair_swap(grads_i))]
        pending = (i, [pb for pb, _ in pairs], [land for _, land in pairs])
    gstacks = finish(pending[0], _chip_all_to_all(pending[1], pending[2]), gstacks)
    grad_x = dh[None]
    grads = dict(zip(BIG, gstacks))

    pieces = [norm_grads[i][name].reshape(-1) for i in range(depth) for name in REPLICATED]
    pieces += [g_final.reshape(-1)] + [conv_grads[i].reshape(-1) for i in range(depth)]
    vec = _all_gather_small(_to_rows128(jnp.concatenate(pieces)), "all_sum_small")[0].reshape(-1)
    off = 0
    per_name = {name: [] for name in REPLICATED}
    for i in range(depth):
        for name in REPLICATED:
            size = args[name].shape[1]
            per_name[name].append(vec[off:off + size])
            off += size
    for name in REPLICATED:
        grads[name] = jnp.stack(per_name[name])
    grads["final_norm"] = vec[off:off + d]
    off += d
    conv_g = vec[off:off + depth * 3 * conv].reshape(depth, 3, conv)
    grads["conv_w"] = lax.dynamic_slice_in_dim(conv_g, chip * conv_w.shape[-1], conv_w.shape[-1], axis=2)

    delta, new_m, new_v = {}, {}, {}
    for name in WEIGHTS:
        w_, g_, m_, v_ = args[name], grads[name], args["m_" + name], args["v_" + name]
        if w_.ndim == 1:
            outs = _adamw(w_[None], g_[None], m_[None], v_[None])
            delta[name], new_m[name], new_v[name] = (o[0] for o in outs)
        else:
            delta[name], new_m[name], new_v[name] = _adamw(w_, g_, m_, v_)
    return (loss, grad_x, *[grads[n] for n in WEIGHTS], *[delta[n] for n in WEIGHTS],
            *[new_m[n] for n in WEIGHTS], *[new_v[n] for n in WEIGHTS])
```

```python
import functools

import jax
import jax.numpy as jnp
from jax import lax
from jax.experimental import pallas as pl
from jax.experimental.pallas import tpu as pltpu

BF16 = jnp.bfloat16
F32 = jnp.float32
SDS = jax.ShapeDtypeStruct
MESH = pl.DeviceIdType.MESH

N_HEADS = 8
NOPE_DIM = 128
ROPE_DIM = 64
V_DIM = 128
QK_PAD = 256
CHUNK = 64
ROPE_THETA = 10000.0
EPS = 1e-6
ATTN_SCALE = (NOPE_DIM + ROPE_DIM) ** -0.5
NEG_BIG = -1e30

ADAM_LR = 0.001
ADAM_B1 = 0.9
ADAM_B2 = 0.999
ADAM_EPS = 1e-08
ADAM_WD = 0.01
ADAM_STEP = 10

LANES = 128
N_CHIPS = 4
VMEM_LIMIT_BYTES = 56 * 1024 * 1024
ACC_BYTES = 6 * 1024 * 1024
BLOCK_ELEMS = 1 << 19

SHARDED = (("ffn1_w_gu", 1), ("ffn1_w_down", 0), ("w_in", 1), ("w_conv_out", 1), ("w_uq", 1), ("w_ukv", 1),
           ("w_mla_out", 0), ("w_o", 0), ("ffn2_w_gu", 1), ("ffn2_w_down", 0), ("w_ple_gate", 0), ("w_ple_proj", 1))
BIG = tuple(name for name, _ in SHARDED)
REPLICATED = ("ffn1_norm", "mix_norm", "q_norm", "kv_norm", "ffn2_norm", "ple_norm")
WEIGHTS = ("ffn1_norm", "ffn1_w_gu", "ffn1_w_down", "mix_norm", "w_in", "conv_w", "w_conv_out", "q_norm",
           "kv_norm", "w_uq", "w_ukv", "w_mla_out", "w_o", "ffn2_norm", "ffn2_w_gu", "ffn2_w_down",
           "ple_norm", "w_ple_gate", "w_ple_proj", "final_norm")
ARG_NAMES = ("x", "p", "positions") + WEIGHTS + ("loss_target",) + tuple("m_" + n for n in WEIGHTS) + tuple(
    "v_" + n for n in WEIGHTS)


def _cparams(semantics=None):
    return pltpu.CompilerParams(dimension_semantics=semantics, vmem_limit_bytes=VMEM_LIMIT_BYTES)


def _tile(n, cap, mult=LANES):
    best = None
    for t in range(mult, min(n, cap) + 1, mult):
        if n % t == 0:
            best = t
    return n if best is None else best


def _sigmoid(x):
    return 1.0 / (1.0 + jnp.exp(-x))


def _rowspec(tm, width, col_block=0):
    return pl.BlockSpec((tm, width), lambda i: (i, col_block))


def _colspec(tm, width, offset):
    assert offset % width == 0, (width, offset)
    return _rowspec(tm, width, offset // width)


def _mm(a, b, mode, out_dtype, name, scale=None, res=None, layer=None, b_chip=False, out_chip=False, norm_gain=None):
    bshape = b.shape if layer is None else b.shape[1:]
    if b_chip:
        bshape = (bshape[1], N_CHIPS * bshape[2])
    if mode == "nn":
        (m, k), (k2, n) = a.shape, bshape
    elif mode == "nt":
        (m, k), (n, k2) = a.shape, bshape
    else:
        (k, m), (k2, n) = a.shape, bshape
    assert k == k2, (a.shape, b.shape, mode)
    n_unit = n // N_CHIPS if (out_chip or (b_chip and mode == "nn")) else n
    k_unit = k // N_CHIPS if (b_chip and mode == "nt") else k
    tn = _tile(n_unit, 1536)
    tm = _tile(m, min(1408, ACC_BYTES // (4 * tn)))
    tk = _tile(k_unit, 1536)
    nk = k // tk
    n_per, k_per = n_unit // tn, k_unit // tk
    dims = {"nn": (((1,), (0,)), ((), ())), "nt": (((1,), (1,)), ((), ())), "tn": (((0,), (0,)), ((), ()))}[mode]

    fuse_norm = norm_gain is not None
    assert not fuse_norm or (tn == n and not out_chip), (name, tn, n)

    def body(*refs):
        refs = list(refs)
        a_ref, b_ref = refs.pop(0), refs.pop(0)
        res_ref = refs.pop(0) if res is not None else None
        gain_ref = refs.pop(0) if fuse_norm else None
        o_ref = refs.pop(0)
        on_ref = refs.pop(0) if fuse_norm else None
        acc_ref = refs.pop(0) if nk > 1 else None

        def finish(acc):
            if scale is not None:
                acc = acc * scale
            if res_ref is not None:
                acc = res_ref[...] + acc
            o_ref[...] = acc.astype(out_dtype)
            if fuse_norm:
                on_ref[...] = _rn_fwd_math(acc, gain_ref[...]).astype(BF16)

        part = lax.dot_general(a_ref[...].astype(BF16), b_ref[...].astype(BF16), dims,
                               preferred_element_type=F32)
        if nk == 1:
            finish(part)
        else:
            kk = pl.program_id(2)

            @pl.when(kk == 0)
            def _():
                acc_ref[...] = part

            @pl.when(kk > 0)
            def _():
                acc_ref[...] += part

            @pl.when(kk == nk - 1)
            def _():
                finish(acc_ref[...])

    lead = () if layer is None else (layer,)
    lead_block = () if layer is None else (None,)
    if mode == "nn":
        a_spec = pl.BlockSpec((tm, tk), lambda i, j, kk: (i, kk))
        if b_chip:
            b_spec = pl.BlockSpec(lead_block + (None, tk, tn), lambda i, j, kk: lead + (j // n_per, kk, j % n_per))
        else:
            b_spec = pl.BlockSpec(lead_block + (tk, tn), lambda i, j, kk: lead + (kk, j))
    elif mode == "nt":
        a_spec = pl.BlockSpec((tm, tk), lambda i, j, kk: (i, kk))
        if b_chip:
            b_spec = pl.BlockSpec(lead_block + (None, tn, tk), lambda i, j, kk: lead + (kk // k_per, j, kk % k_per))
        else:
            b_spec = pl.BlockSpec(lead_block + (tn, tk), lambda i, j, kk: lead + (j, kk))
    else:
        assert layer is None and not b_chip
        a_spec = pl.BlockSpec((tk, tm), lambda i, j, kk: (kk, i))
        b_spec = pl.BlockSpec((tk, tn), lambda i, j, kk: (kk, j))
    if out_chip:
        o_spec = pl.BlockSpec((None, tm, tn), lambda i, j, kk: (j // n_per, i, j % n_per))
        out_shape = SDS((N_CHIPS, m, n_unit), out_dtype)
    else:
        o_spec = pl.BlockSpec((tm, tn), lambda i, j, kk: (i, j))
        out_shape = SDS((m, n), out_dtype)
    in_specs = [a_spec, b_spec] + ([o_spec] if res is not None else [])
    operands = (a, b) + ((res,) if res is not None else ())
    out_specs = o_spec
    if fuse_norm:
        in_specs.append(pl.BlockSpec((1, tn), lambda i, j, kk: (0, j)))
        operands += (norm_gain.reshape(1, n),)
        out_shape, out_specs = (out_shape, SDS((m, n), BF16)), (o_spec, o_spec)
    return pl.pallas_call(
        body, out_shape=out_shape, grid=(m // tm, n // tn, nk), in_specs=in_specs, out_specs=out_specs,
        scratch_shapes=[pltpu.VMEM((tm, tn), F32)] if nk > 1 else [],
        compiler_params=_cparams(("parallel", "parallel", "arbitrary")), name=name)(*operands)


def _rn_fwd_math(x, g):
    r = lax.rsqrt(jnp.mean(x * x, axis=-1, keepdims=True) + EPS)
    return (x * r) * g


def _rn_bwd_math(x, g, dn):
    r = lax.rsqrt(jnp.mean(x * x, axis=-1, keepdims=True) + EPS)
    xh = x * r
    gy = dn * g
    dx = r * (gy - xh * jnp.mean(gy * xh, axis=-1, keepdims=True))
    dg = jnp.sum(dn * xh, axis=0, keepdims=True)
    return dx, dg


def _rmsnorm_fwd(h, gain, name):
    t, d = h.shape
    tm = _tile(t, 512, 8)

    def body(h_ref, g_ref, o_ref):
        o_ref[...] = _rn_fwd_math(h_ref[...], g_ref[...]).astype(BF16)

    return pl.pallas_call(
        body, out_shape=SDS((t, d), BF16), grid=(t // tm,),
        in_specs=[_rowspec(tm, d), pl.BlockSpec((1, d), lambda i: (0, 0))], out_specs=_rowspec(tm, d),
        compiler_params=_cparams(("parallel",)), name=name)(h, gain.reshape(1, d))


def _rmsnorm_bwd(h, gain, dn, dh_in, name):
    t, d = h.shape
    tm = _tile(t, 512, 8)

    def body(h_ref, g_ref, dn_ref, dhin_ref, dh_ref, dhb_ref, dg_ref):
        dx, dg = _rn_bwd_math(h_ref[...], g_ref[...], dn_ref[...].astype(F32))
        dh = dhin_ref[...] + dx
        dh_ref[...] = dh
        dhb_ref[...] = dh.astype(BF16)

        @pl.when(pl.program_id(0) == 0)
        def _():
            dg_ref[...] = dg

        @pl.when(pl.program_id(0) > 0)
        def _():
            dg_ref[...] += dg

    vec = pl.BlockSpec((1, d), lambda i: (0, 0))
    return pl.pallas_call(
        body, out_shape=(SDS((t, d), F32), SDS((t, d), BF16), SDS((1, d), F32)), grid=(t // tm,),
        in_specs=[_rowspec(tm, d), vec, _rowspec(tm, d), _rowspec(tm, d)],
        out_specs=(_rowspec(tm, d), _rowspec(tm, d), vec),
        compiler_params=_cparams(("arbitrary",)), name=name)(h, gain.reshape(1, d), dn, dh_in)


def _loss_head(h, gain, target):
    t, d = h.shape
    tm = _tile(t, 512, 8)

    def body(h_ref, g_ref, t_ref, loss_ref, dh_ref, dhb_ref, dg_ref):
        x, g = h_ref[...], g_ref[...]
        err = _rn_fwd_math(x, g) - t_ref[...]
        part = 0.5 * jnp.sum(jnp.sum(err * err, axis=1, keepdims=True), axis=0, keepdims=True) * (1.0 / d)
        dx, dg = _rn_bwd_math(x, g, err * (1.0 / d))
        dh_ref[...] = dx
        dhb_ref[...] = dx.astype(BF16)

        @pl.when(pl.program_id(0) == 0)
        def _():
            dg_ref[...] = dg
            loss_ref[...] = jnp.broadcast_to(part, (1, LANES))

        @pl.when(pl.program_id(0) > 0)
        def _():
            dg_ref[...] += dg
            loss_ref[...] += jnp.broadcast_to(part, (1, LANES))

    vec = pl.BlockSpec((1, d), lambda i: (0, 0))
    return pl.pallas_call(
        body, out_shape=(SDS((1, LANES), F32), SDS((t, d), F32), SDS((t, d), BF16), SDS((1, d), F32)),
        grid=(t // tm,), in_specs=[_rowspec(tm, d), vec, _rowspec(tm, d)],
        out_specs=(pl.BlockSpec((1, LANES), lambda i: (0, 0)), _rowspec(tm, d), _rowspec(tm, d), vec),
        compiler_params=_cparams(("arbitrary",)), name="loss_head")(h, gain.reshape(1, d), target)


def _swiglu_fwd(gu):
    t, f2 = gu.shape
    f = f2 // 2
    tm = _tile(t, 256, 8)

    def body(gu_ref, a_ref):
        g = gu_ref[:, :f].astype(F32)
        u = gu_ref[:, f:].astype(F32)
        a_ref[...] = (g * _sigmoid(g) * u).astype(BF16)

    return pl.pallas_call(
        body, out_shape=SDS((t, f), BF16), grid=(t // tm,), in_specs=[_rowspec(tm, f2)],
        out_specs=_rowspec(tm, f), compiler_params=_cparams(("parallel",)), name="swiglu_fwd")(gu)


def _swiglu_bwd(gu, da):
    t, f2 = gu.shape
    f = f2 // 2
    tm = _tile(t, 256, 8)

    def body(gu_ref, da_ref, dgu_ref):
        g = gu_ref[:, :f].astype(F32)
        u = gu_ref[:, f:].astype(F32)
        da_ = da_ref[...].astype(F32)
        s = _sigmoid(g)
        dgu_ref[:, :f] = (da_ * u * (s * (1.0 + g * (1.0 - s)))).astype(BF16)
        dgu_ref[:, f:] = (da_ * (g * s)).astype(BF16)

    return pl.pallas_call(
        body, out_shape=SDS((t, f2), BF16), grid=(t // tm,), in_specs=[_rowspec(tm, f2), _rowspec(tm, f)],
        out_specs=_rowspec(tm, f2), compiler_params=_cparams(("parallel",)), name="swiglu_bwd")(gu, da)


def _shift_down(z, k, row):
    return jnp.where(row >= k, pltpu.roll(z, k, 0), 0.0)


def _shift_up(z, k, row, t):
    return jnp.where(row < t - k, pltpu.roll(z, t - k, 0), 0.0)


def _conv_specs(t, conv):
    nb = conv // LANES
    return [pl.BlockSpec((t, LANES), lambda j: (0, j)), pl.BlockSpec((t, LANES), lambda j: (0, nb + j)),
            pl.BlockSpec((t, LANES), lambda j: (0, 2 * nb + j))]


def _conv_fwd(proj, conv_w):
    t = proj.shape[0]
    conv = conv_w.shape[1]

    def body(b_ref, c_ref, v_ref, w_ref, o_ref):
        z = c_ref[...].astype(F32) * v_ref[...].astype(F32)
        row = lax.broadcasted_iota(jnp.int32, z.shape, 0)
        y = w_ref[0:1, :] * _shift_down(z, 2, row) + w_ref[1:2, :] * _shift_down(z, 1, row) + w_ref[2:3, :] * z
        o_ref[...] = (b_ref[...].astype(F32) * y).astype(BF16)

    cspec = pl.BlockSpec((t, LANES), lambda j: (0, j))
    return pl.pallas_call(
        body, out_shape=SDS((t, conv), BF16), grid=(conv // LANES,),
        in_specs=_conv_specs(t, conv) + [pl.BlockSpec((3, LANES), lambda j: (0, j))], out_specs=cspec,
        compiler_params=_cparams(("parallel",)), name="conv_fwd")(proj, proj, proj, conv_w)


def _conv_bwd(proj, conv_w, dcb):
    t = proj.shape[0]
    conv = conv_w.shape[1]

    def body(b_ref, c_ref, v_ref, w_ref, d_ref, db_ref, dc_ref, dv_ref, dw_ref):
        b, c, v = b_ref[...].astype(F32), c_ref[...].astype(F32), v_ref[...].astype(F32)
        d = d_ref[...].astype(F32)
        z = c * v
        row = lax.broadcasted_iota(jnp.int32, z.shape, 0)
        z1, z2 = _shift_down(z, 1, row), _shift_down(z, 2, row)
        w0, w1, w2 = w_ref[0:1, :], w_ref[1:2, :], w_ref[2:3, :]
        y = w0 * z2 + w1 * z1 + w2 * z
        dy = d * b
        db_ref[...] = (d * y).astype(BF16)
        dz = w2 * dy + w1 * _shift_up(dy, 1, row, t) + w0 * _shift_up(dy, 2, row, t)
        dc_ref[...] = (dz * v).astype(BF16)
        dv_ref[...] = (dz * c).astype(BF16)
        dw_ref[0:1, :] = jnp.sum(dy * z2, axis=0, keepdims=True)
        dw_ref[1:2, :] = jnp.sum(dy * z1, axis=0, keepdims=True)
        dw_ref[2:3, :] = jnp.sum(dy * z, axis=0, keepdims=True)

    cspec = pl.BlockSpec((t, LANES), lambda j: (0, j))
    wspec = pl.BlockSpec((3, LANES), lambda j: (0, j))
    return pl.pallas_call(
        body, out_shape=(SDS((t, conv), BF16),) * 3 + (SDS((3, conv), F32),), grid=(conv // LANES,),
        in_specs=_conv_specs(t, conv) + [wspec, cspec], out_specs=(cspec, cspec, cspec, wspec),
        compiler_params=_cparams(("parallel",)), name="conv_bwd")(proj, proj, proj, conv_w, dcb)


def _qkvnorm_fwd(proj, lay, q_gain, kv_gain):
    t = proj.shape[0]
    ql, kvl = lay["ql"], lay["kvl"]
    tm = _tile(t, 512, 8)

    def body(q_ref, kv_ref, gq_ref, gkv_ref, qn_ref, kvn_ref):
        qn_ref[...] = _rn_fwd_math(q_ref[...].astype(F32), gq_ref[...]).astype(BF16)
        kvn_ref[...] = _rn_fwd_math(kv_ref[...].astype(F32), gkv_ref[...]).astype(BF16)

    return pl.pallas_call(
        body, out_shape=(SDS((t, ql), BF16), SDS((t, kvl), BF16)), grid=(t // tm,),
        in_specs=[_colspec(tm, ql, lay["q"]), _colspec(tm, kvl, lay["kv"]),
                  pl.BlockSpec((1, ql), lambda i: (0, 0)), pl.BlockSpec((1, kvl), lambda i: (0, 0))],
        out_specs=(_rowspec(tm, ql), _rowspec(tm, kvl)), compiler_params=_cparams(("parallel",)),
        name="qkvnorm_fwd")(proj, proj, q_gain.reshape(1, ql), kv_gain.reshape(1, kvl))


def _qkvnorm_bwd(proj, lay, q_gain, kv_gain, dqn, dkvn):
    t = proj.shape[0]
    ql, kvl = lay["ql"], lay["kvl"]
    tm = _tile(t, 512, 8)

    def body(q_ref, kv_ref, gq_ref, gkv_ref, dqn_ref, dkvn_ref, dq_ref, dkv_ref, dgq_ref, dgkv_ref):
        dq, dgq = _rn_bwd_math(q_ref[...].astype(F32), gq_ref[...], dqn_ref[...].astype(F32))
        dkv, dgkv = _rn_bwd_math(kv_ref[...].astype(F32), gkv_ref[...], dkvn_ref[...].astype(F32))
        dq_ref[...] = dq.astype(BF16)
        dkv_ref[...] = dkv.astype(BF16)

        @pl.when(pl.program_id(0) == 0)
        def _():
            dgq_ref[...] = dgq
            dgkv_ref[...] = dgkv

        @pl.when(pl.program_id(0) > 0)
        def _():
            dgq_ref[...] += dgq
            dgkv_ref[...] += dgkv

    vq = pl.BlockSpec((1, ql), lambda i: (0, 0))
    vkv = pl.BlockSpec((1, kvl), lambda i: (0, 0))
    return pl.pallas_call(
        body, out_shape=(SDS((t, ql), BF16), SDS((t, kvl), BF16), SDS((1, ql), F32), SDS((1, kvl), F32)),
        grid=(t // tm,),
        in_specs=[_colspec(tm, ql, lay["q"]), _colspec(tm, kvl, lay["kv"]), vq, vkv, _rowspec(tm, ql),
                  _rowspec(tm, kvl)],
        out_specs=(_rowspec(tm, ql), _rowspec(tm, kvl), vq, vkv), compiler_params=_cparams(("arbitrary",)),
        name="qkvnorm_bwd")(proj, proj, q_gain.reshape(1, ql), kv_gain.reshape(1, kvl), dqn, dkvn)


def _rope(x, cos_t, sin_a, sin_b):
    return x * cos_t + pltpu.roll(x, LANES - ROPE_DIM // 2, 1) * sin_a + pltpu.roll(x, ROPE_DIM // 2, 1) * sin_b


def _rope_fwd(qf, kv, proj, lay, tables):
    t = qf.shape[0]
    tm = _tile(t, 256, 8)
    hq = N_HEADS * QK_PAD

    def body(q_ref, kn_ref, kr_ref, cos_ref, sa_ref, sb_ref, qr_ref, kf_ref):
        cos_t, sin_a, sin_b = cos_ref[...], sa_ref[...], sb_ref[...]
        kr = _rope(kr_ref[...].astype(F32), cos_t, sin_a, sin_b).astype(BF16)
        for h in range(N_HEADS):
            lo = h * QK_PAD
            qr_ref[:, lo:lo + NOPE_DIM] = q_ref[:, lo:lo + NOPE_DIM]
            qr_ref[:, lo + NOPE_DIM:lo + QK_PAD] = _rope(
                q_ref[:, lo + NOPE_DIM:lo + QK_PAD].astype(F32), cos_t, sin_a, sin_b).astype(BF16)
            kf_ref[:, lo:lo + NOPE_DIM] = kn_ref[:, h * NOPE_DIM:(h + 1) * NOPE_DIM]
            kf_ref[:, lo + NOPE_DIM:lo + QK_PAD] = kr

    tab = _rowspec(tm, LANES)
    return pl.pallas_call(
        body, out_shape=(SDS((t, hq), BF16), SDS((t, hq), BF16)), grid=(t // tm,),
        in_specs=[_rowspec(tm, hq), _rowspec(tm, N_HEADS * NOPE_DIM), _colspec(tm, LANES, lay["kr"]), tab, tab, tab],
        out_specs=(_rowspec(tm, hq), _rowspec(tm, hq)), compiler_params=_cparams(("parallel",)),
        name="rope_fwd")(qf, kv, proj, *tables)


def _rope_bwd(dqr, dkf, dv, tables):
    t = dqr.shape[0]
    tm = _tile(t, 256, 8)
    hq = N_HEADS * QK_PAD
    hn = N_HEADS * NOPE_DIM

    def body(dq_ref, dk_ref, dv_ref, cos_ref, sa_ref, sb_ref, dqf_ref, dkv_ref, dkr_ref):
        cos_t, sin_a, sin_b = cos_ref[...], -sa_ref[...], -sb_ref[...]
        dkr = jnp.zeros((tm, LANES), F32)
        for h in range(N_HEADS):
            lo = h * QK_PAD
            dqf_ref[:, lo:lo + NOPE_DIM] = dq_ref[:, lo:lo + NOPE_DIM].astype(BF16)
            dqf_ref[:, lo + NOPE_DIM:lo + QK_PAD] = _rope(
                dq_ref[:, lo + NOPE_DIM:lo + QK_PAD].astype(F32), cos_t, sin_a, sin_b).astype(BF16)
            dkv_ref[:, h * NOPE_DIM:(h + 1) * NOPE_DIM] = dk_ref[:, lo:lo + NOPE_DIM]
            dkr = dkr + dk_ref[:, lo + NOPE_DIM:lo + QK_PAD].astype(F32)
        dkv_ref[:, hn:] = dv_ref[...]
        dkr_ref[...] = _rope(dkr, cos_t, sin_a, sin_b).astype(BF16)

    tab = _rowspec(tm, LANES)
    return pl.pallas_call(
        body, out_shape=(SDS((t, hq), BF16), SDS((t, 2 * hn), BF16), SDS((t, LANES), BF16)), grid=(t // tm,),
        in_specs=[_rowspec(tm, hq), _rowspec(tm, hq), _rowspec(tm, hn), tab, tab, tab],
        out_specs=(_rowspec(tm, hq), _rowspec(tm, 2 * hn), tab), compiler_params=_cparams(("parallel",)),
        name="rope_bwd")(dqr, dkf, dv, *tables)


def _chunk_mask(bq):
    qc = lax.broadcasted_iota(jnp.int32, (bq, bq), 0) // CHUNK
    kc = lax.broadcasted_iota(jnp.int32, (bq, bq), 1) // CHUNK
    return kc <= qc


_NT = (((1,), (1,)), ((), ()))
_TN = (((0,), (0,)), ((), ()))
LOG2E = 1.4426950408889634
EXP2_SCALE = ATTN_SCALE * LOG2E


def _attn_block(t):
    return 512 if t >= 2048 else 128


def _two_slot_pipeline(unmasked, issue, consume, carry):
    issue(0, 0)

    def pair(n, c):
        issue(2 * n + 1, 1)
        c = consume(2 * n, 0, c, False)
        issue(2 * n + 2, 0)
        return consume(2 * n + 1, 1, c, False)

    carry = lax.fori_loop(0, unmasked // 2, pair, carry)

    def even(c):
        return consume(unmasked, 0, c, True)

    def odd(c):
        issue(unmasked, 1)
        c = consume(unmasked - 1, 0, c, False)
        return consume(unmasked, 1, c, True)

    return lax.cond(unmasked % 2 == 0, even, odd, carry)


def _attn_fwd(qr, kf, kv, gather=None):
    t = qr.shape[0]
    bq = _attn_block(t)
    nq = t // bq
    bufs, layer = gather if gather is not None else ((), None)
    nw = len(bufs)

    def body(*refs):
        q_ref, k_ref, v_ref = refs[:3]
        o_ref, lse_ref = refs[3 + nw:5 + nw]
        buf_refs = refs[5 + nw:5 + 2 * nw]
        vaug_ref, s_ref = refs[5 + 2 * nw], refs[6 + 2 * nw]
        sems = refs[7 + 2 * nw:]
        h, i = pl.program_id(0), pl.program_id(1)

        if nw:
            @pl.when((h == 0) & (i == 0))
            def _():
                _gather_start(buf_refs, layer, *sems)

        @pl.when(i == 0)
        def _():
            vaug_ref[:, :V_DIM] = v_ref[...]
            vaug_ref[:, V_DIM:] = jnp.ones((t, V_DIM), BF16)

        def issue(j, slot):
            off = pl.multiple_of(j * bq, bq)
            s_ref[slot] = lax.dot_general(q_ref[...], k_ref[pl.ds(off, bq), :], _NT, preferred_element_type=F32)

        def consume(j, slot, carry, masked):
            m, acc = carry
            off = pl.multiple_of(j * bq, bq)
            s = s_ref[slot]
            if masked:
                s = jnp.where(_chunk_mask(bq), s, NEG_BIG)
            m_new = jnp.maximum(m, jnp.max(s, axis=1, keepdims=True))
            alpha = jnp.exp2((m - m_new) * EXP2_SCALE)
            pr = jnp.exp2((s - m_new) * EXP2_SCALE)
            acc = alpha * acc + jnp.dot(pr.astype(BF16), vaug_ref[pl.ds(off, bq), :], preferred_element_type=F32)
            return m_new, acc

        init = (jnp.full((bq, 1), NEG_BIG, F32), jnp.zeros((bq, 2 * V_DIM), F32))
        m, acc = _two_slot_pipeline(i, issue, consume, init)
        l = acc[:, V_DIM:V_DIM + 1]
        o_ref[...] = (acc[:, :V_DIM] / l).astype(BF16)
        lse_ref[0] = jnp.broadcast_to(m * ATTN_SCALE + jnp.log(l), (bq, LANES))

        if nw:
            @pl.when((h == N_HEADS - 1) & (i == nq - 1))
            def _():
                _gather_finish(buf_refs, layer, *sems)

    out_shape = (SDS((t, N_HEADS * V_DIM), BF16), SDS((N_HEADS, t, LANES), F32)) + tuple(SDS(b.shape, b.dtype) for b in bufs)
    sem_shapes = [pltpu.SemaphoreType.DMA((6 * nw,)), pltpu.SemaphoreType.DMA((6 * nw,))] if nw else []
    outs = pl.pallas_call(
        body, out_shape=out_shape, grid=(N_HEADS, nq),
        in_specs=[pl.BlockSpec((bq, QK_PAD), lambda h, i: (i, h)), pl.BlockSpec((t, QK_PAD), lambda h, i: (0, h)),
                  pl.BlockSpec((t, V_DIM), lambda h, i: (0, N_HEADS + h))] + [ANY] * nw,
        out_specs=(pl.BlockSpec((bq, V_DIM), lambda h, i: (i, h)),
                   pl.BlockSpec((1, bq, LANES), lambda h, i: (h, i, 0))) + (ANY,) * nw,
        input_output_aliases={3 + n: 2 + n for n in range(nw)},
        scratch_shapes=[pltpu.VMEM((t, 2 * V_DIM), BF16), pltpu.VMEM((2, bq, bq), F32)] + sem_shapes,
        compiler_params=_cparams(("arbitrary", "arbitrary")), name="attn_fwd_gather" if nw else "attn_fwd")(qr, kf, kv, *bufs)
    return outs[0], outs[1], list(outs[2:])


def _attn_delta(do, o):
    t = do.shape[0]
    tm = _tile(t, 512, 8)

    def body(do_ref, o_ref, dl_ref):
        prod = do_ref[...].astype(F32) * o_ref[...].astype(F32)
        for h in range(N_HEADS):
            s = jnp.sum(prod[:, h * V_DIM:(h + 1) * V_DIM], axis=1, keepdims=True)
            dl_ref[h] = jnp.broadcast_to(s, (tm, LANES))

    return pl.pallas_call(
        body, out_shape=SDS((N_HEADS, t, LANES), F32), grid=(t // tm,),
        in_specs=[_rowspec(tm, N_HEADS * V_DIM), _rowspec(tm, N_HEADS * V_DIM)],
        out_specs=pl.BlockSpec((N_HEADS, tm, LANES), lambda i: (0, i, 0)), compiler_params=_cparams(("parallel",)),
        name="attn_delta")(do, o)


def _attn_bwd(qr, kf, kv, do, lse, delta, exchange=None):
    t = qr.shape[0]
    bq = _attn_block(t)
    nq = t // bq
    pbs, lands = exchange if exchange is not None else ((), ())
    nw = len(pbs)

    def body(*refs):
        k_ref, v_ref, q_ref, do_ref, lse_ref, dl_ref = refs[:6]
        pb_refs = refs[6:6 + nw]
        dk_ref, dv_ref, dq_ref = refs[6 + 2 * nw:9 + 2 * nw]
        land_refs = refs[9 + 2 * nw:9 + 3 * nw]
        s_ref, dp_ref = refs[9 + 3 * nw], refs[10 + 3 * nw]
        sems = refs[11 + 3 * nw:]
        h, j = pl.program_id(0), pl.program_id(1)

        if nw:
            @pl.when((h == 0) & (j == 0))
            def _():
                _exchange_start(pb_refs, land_refs, *sems)

        @pl.when(j == 0)
        def _():
            dq_ref[...] = jnp.zeros_like(dq_ref)

        k = k_ref[...]
        v = v_ref[...]

        def issue(b, slot):
            off = pl.multiple_of((nq - 1 - b) * bq, bq)
            s_ref[slot] = lax.dot_general(q_ref[pl.ds(off, bq), :], k, _NT, preferred_element_type=F32)
            dp_ref[slot] = lax.dot_general(do_ref[pl.ds(off, bq), :], v, _NT, preferred_element_type=F32)

        def consume(b, slot, carry, masked):
            dk, dv = carry
            off = pl.multiple_of((nq - 1 - b) * bq, bq)
            q = q_ref[pl.ds(off, bq), :]
            do_ = do_ref[pl.ds(off, bq), :]
            lse2 = lse_ref[0, pl.ds(off, bq), :][:, :1] * LOG2E
            dl_i = dl_ref[0, pl.ds(off, bq), :][:, :1]
            s = s_ref[slot]
            if masked:
                s = jnp.where(_chunk_mask(bq), s, NEG_BIG)
            pr = jnp.exp2(s * EXP2_SCALE - lse2)
            dv = dv + lax.dot_general(pr.astype(BF16), do_, _TN, preferred_element_type=F32)
            ds = (pr * (dp_ref[slot] - dl_i)).astype(BF16)
            dk = dk + lax.dot_general(ds, q, _TN, preferred_element_type=F32)
            dq_ref[pl.ds(off, bq), :] += jnp.dot(ds, k, preferred_element_type=F32) * ATTN_SCALE
            return dk, dv

        init = (jnp.zeros((bq, QK_PAD), F32), jnp.zeros((bq, V_DIM), F32))
        dk, dv = _two_slot_pipeline(nq - 1 - j, issue, consume, init)
        dk_ref[...] = (dk * ATTN_SCALE).astype(BF16)
        dv_ref[...] = dv.astype(BF16)

        if nw:
            @pl.when((h == N_HEADS - 1) & (j == nq - 1))
            def _():
                _exchange_finish(pb_refs, land_refs, *sems)

    stat = pl.BlockSpec((1, t, LANES), lambda h, j: (h, 0, 0))
    out_shape = (SDS((t, N_HEADS * QK_PAD), BF16), SDS((t, N_HEADS * V_DIM), BF16), SDS((t, N_HEADS * QK_PAD), F32))
    sem_shapes = [pltpu.SemaphoreType.DMA((3 * nw,)), pltpu.SemaphoreType.DMA((3 * nw,))] if nw else []
    outs = pl.pallas_call(
        body, out_shape=out_shape + tuple(SDS(l.shape, l.dtype) for l in lands), grid=(N_HEADS, nq),
        in_specs=[pl.BlockSpec((bq, QK_PAD), lambda h, j: (j, h)), pl.BlockSpec((bq, V_DIM), lambda h, j: (j, N_HEADS + h)),
                  pl.BlockSpec((t, QK_PAD), lambda h, j: (0, h)), pl.BlockSpec((t, V_DIM), lambda h, j: (0, h)), stat, stat]
        + [ANY] * (2 * nw),
        out_specs=(pl.BlockSpec((bq, QK_PAD), lambda h, j: (j, h)), pl.BlockSpec((bq, V_DIM), lambda h, j: (j, h)),
                   pl.BlockSpec((t, QK_PAD), lambda h, j: (0, h))) + (ANY,) * nw,
        input_output_aliases={6 + nw + n: 3 + n for n in range(nw)},
        scratch_shapes=[pltpu.VMEM((2, bq, bq), F32), pltpu.VMEM((2, bq, bq), F32)] + sem_shapes,
        compiler_params=_cparams(("arbitrary", "arbitrary")),
        name="attn_bwd_exchange" if nw else "attn_bwd")(kf, kv, qr, do, lse, delta, *pbs, *lands)
    return outs[0], outs[1], outs[2], list(outs[3:])


def _merge_fwd(proj, lay, ya, yb):
    t, d = ya.shape
    tm = _tile(t, 512, 8)

    def body(gc_ref, gm_ref, ya_ref, yb_ref, o_ref):
        o_ref[...] = (_sigmoid(gc_ref[...].astype(F32)) * ya_ref[...].astype(F32)
                      + _sigmoid(gm_ref[...].astype(F32)) * yb_ref[...].astype(F32)).astype(BF16)

    return pl.pallas_call(
        body, out_shape=SDS((t, d), BF16), grid=(t // tm,),
        in_specs=[_colspec(tm, d, lay["gc"]), _colspec(tm, d, lay["gm"]), _rowspec(tm, d), _rowspec(tm, d)],
        out_specs=_rowspec(tm, d), compiler_params=_cparams(("parallel",)), name="merge_fwd")(proj, proj, ya, yb)


def _merge_bwd(proj, lay, ya, yb, dmg):
    t, d = ya.shape
    tm = _tile(t, 512, 8)

    def body(gc_ref, gm_ref, ya_ref, yb_ref, d_ref, dya_ref, dyb_ref, dgc_ref, dgm_ref):
        dm = d_ref[...].astype(F32)
        sc = _sigmoid(gc_ref[...].astype(F32))
        sm = _sigmoid(gm_ref[...].astype(F32))
        dya_ref[...] = (dm * sc).astype(BF16)
        dyb_ref[...] = (dm * sm).astype(BF16)
        dgc_ref[...] = (dm * ya_ref[...].astype(F32) * (sc * (1.0 - sc))).astype(BF16)
        dgm_ref[...] = (dm * yb_ref[...].astype(F32) * (sm * (1.0 - sm))).astype(BF16)

    r = _rowspec(tm, d)
    return pl.pallas_call(
        body, out_shape=(SDS((t, d), BF16),) * 4, grid=(t // tm,),
        in_specs=[_colspec(tm, d, lay["gc"]), _colspec(tm, d, lay["gm"]), r, r, r], out_specs=(r, r, r, r),
        compiler_params=_cparams(("parallel",)), name="merge_bwd")(proj, proj, ya, yb, dmg)


def _ple_fwd(h, gp, pp, norm_gain=None):
    t, d = h.shape
    tm = _tile(t, 512, 8)
    fuse_norm = norm_gain is not None

    def body(h_ref, gp_ref, pp_ref, *rest):
        out = h_ref[...] + _sigmoid(gp_ref[...].astype(F32)) * pp_ref[...].astype(F32)
        rest[-2 if fuse_norm else -1][...] = out
        if fuse_norm:
            rest[-1][...] = _rn_fwd_math(out, rest[0][...]).astype(BF16)

    r = _rowspec(tm, d)
    if not fuse_norm:
        return pl.pallas_call(body, out_shape=SDS((t, d), F32), grid=(t // tm,), in_specs=[r, r, r], out_specs=r,
                              compiler_params=_cparams(("parallel",)), name="ple_fwd")(h, gp, pp), None
    return pl.pallas_call(
        body, out_shape=(SDS((t, d), F32), SDS((t, d), BF16)), grid=(t // tm,),
        in_specs=[r, r, r, pl.BlockSpec((1, d), lambda i: (0, 0))], out_specs=(r, r),
        compiler_params=_cparams(("parallel",)), name="ple_fwd_norm")(h, gp, pp, norm_gain.reshape(1, d))


def _ple_bwd(dh, gp, pp):
    t, d = dh.shape
    tm = _tile(t, 512, 8)

    def body(dh_ref, gp_ref, pp_ref, dpp_ref, dgp_ref):
        g = dh_ref[...]
        s = _sigmoid(gp_ref[...].astype(F32))
        dpp_ref[...] = (g * s).astype(BF16)
        dgp_ref[...] = (g * pp_ref[...].astype(F32) * (s * (1.0 - s))).astype(BF16)

    r = _rowspec(tm, d)
    return pl.pallas_call(body, out_shape=(SDS((t, d), BF16),) * 2, grid=(t // tm,), in_specs=[r, r, r],
                          out_specs=(r, r), compiler_params=_cparams(("parallel",)), name="ple_bwd")(dh, gp, pp)


def _adamw(w, g, m, v):
    shape = w.shape
    cols = shape[-1]
    rows = w.size // cols
    tr = _tile(rows, max(8, BLOCK_ELEMS // cols // 8 * 8), 8)

    def body(w_ref, g_ref, m_ref, v_ref, d_ref, nm_ref, nv_ref):
        g_ = g_ref[...]
        nm = ADAM_B1 * m_ref[...] + (1.0 - ADAM_B1) * g_
        nv = ADAM_B2 * v_ref[...] + (1.0 - ADAM_B2) * (g_ * g_)
        m_hat = nm / (1.0 - ADAM_B1 ** ADAM_STEP)
        v_hat = nv / (1.0 - ADAM_B2 ** ADAM_STEP)
        d_ref[...] = -ADAM_LR * (m_hat / (jnp.sqrt(v_hat) + ADAM_EPS) + ADAM_WD * w_ref[...])
        nm_ref[...] = nm
        nv_ref[...] = nv

    r = _rowspec(tr, cols)
    outs = pl.pallas_call(
        body, out_shape=(SDS((rows, cols), F32),) * 3, grid=(rows // tr,), in_specs=[r, r, r, r], out_specs=(r, r, r),
        compiler_params=_cparams(("parallel",)), name="adamw")(*(a.reshape(rows, cols) for a in (w, g, m, v)))
    return tuple(o.reshape(shape) for o in outs)


ANY = pl.BlockSpec(memory_space=pl.ANY)


def _place():
    x, y, c = lax.axis_index("x"), lax.axis_index("y"), lax.axis_index("c")
    return x, y, c, [(1 - x, y), (x, 1 - y), (1 - x, 1 - y)]


def _half_rows(rows, cols):
    half = rows // 2
    return half, _tile(half, max(16, BLOCK_ELEMS // cols // 16 * 16), 16)


def _my_chip():
    return 2 * lax.axis_index("x") + lax.axis_index("y")


def _cast_into_slot(w):
    nl, r, c = w.shape
    tr = _tile(r, max(16, BLOCK_ELEMS // c // 16 * 16), 16)

    def body(w_ref, o_ref):
        o_ref[...] = w_ref[...].astype(BF16)

    return pl.pallas_call(
        body, out_shape=SDS((nl, N_CHIPS, r, c), BF16), grid=(nl, r // tr),
        in_specs=[pl.BlockSpec((None, tr, c), lambda l, i: (l, i, 0))],
        out_specs=pl.BlockSpec((None, None, tr, c), lambda l, i: (l, _my_chip(), i, 0)),
        compiler_params=_cparams(("parallel", "parallel")), name="cast_into_slot")(w)


def _gather_copy(ref, layer, send_sems, recv_sems, sem, chip, half, to):
    r2 = ref.shape[2] // 2
    rows = ref.at[layer, chip, pl.ds(half * r2, r2)]
    return pltpu.make_async_remote_copy(src_ref=rows, dst_ref=rows, send_sem=send_sems.at[sem], recv_sem=recv_sems.at[sem],
                                        device_id=to, device_id_type=MESH)


def _gather_start(refs, layer, send_sems, recv_sems):
    x, y, c, chips = _place()
    for wi, ref in enumerate(refs):
        for n, chip in enumerate(chips):
            _gather_copy(ref, layer, send_sems, recv_sems, 6 * wi + n, 2 * x + y, c, (*chip, c)).start()


def _gather_finish(refs, layer, send_sems, recv_sems):
    x, y, c, chips = _place()
    me, sibling = (x, y, c), (x, y, 1 - c)
    for wi, ref in enumerate(refs):
        for n, chip in enumerate(chips):
            k = 2 * chip[0] + chip[1]
            _gather_copy(ref, layer, send_sems, recv_sems, 6 * wi + n, k, c, me).wait_recv()
            _gather_copy(ref, layer, send_sems, recv_sems, 6 * wi + 3 + n, k, c, sibling).start()
    for wi, ref in enumerate(refs):
        for n, chip in enumerate(chips):
            _gather_copy(ref, layer, send_sems, recv_sems, 6 * wi + 3 + n, 2 * chip[0] + chip[1], 1 - c, me).wait_recv()
    for wi, ref in enumerate(refs):
        for n, chip in enumerate(chips):
            _gather_copy(ref, layer, send_sems, recv_sems, 6 * wi + n, 2 * x + y, c, (*chip, c)).wait_send()
            _gather_copy(ref, layer, send_sems, recv_sems, 6 * wi + 3 + n, 2 * chip[0] + chip[1], c, sibling).wait_send()


def _all_gather_weights(bufs, layer):
    nw = len(bufs)

    def body(*refs):
        _gather_start(refs[nw:2 * nw], layer, refs[2 * nw], refs[2 * nw + 1])
        _gather_finish(refs[nw:2 * nw], layer, refs[2 * nw], refs[2 * nw + 1])

    return list(pl.pallas_call(
        body, out_shape=tuple(SDS(b.shape, b.dtype) for b in bufs), in_specs=[ANY] * nw, out_specs=(ANY,) * nw,
        input_output_aliases={i: i for i in range(nw)},
        scratch_shapes=[pltpu.SemaphoreType.DMA((6 * nw,)), pltpu.SemaphoreType.DMA((6 * nw,))],
        name="all_gather_weights")(*bufs))


def _pair_swap(grads):
    nw = len(grads)

    def body(*refs):
        ins, outs = refs[:nw], refs[nw:2 * nw]
        send_sems, recv_sems = refs[2 * nw], refs[2 * nw + 1]
        x, y, c, _ = _place()
        copies = []
        for wi, (g_ref, o_ref) in enumerate(zip(ins, outs)):
            r2 = g_ref.shape[1] // 2
            copies.append(pltpu.make_async_remote_copy(
                src_ref=g_ref.at[:, pl.ds((1 - c) * r2, r2)], dst_ref=o_ref, send_sem=send_sems.at[wi],
                recv_sem=recv_sems.at[wi], device_id=(x, y, 1 - c), device_id_type=MESH))
            copies[-1].start()
        for cp in copies:
            cp.wait()

    return pl.pallas_call(
        body, out_shape=tuple(SDS((N_CHIPS, g.shape[1] // 2, g.shape[2]), g.dtype) for g in grads),
        in_specs=[ANY] * nw, out_specs=(ANY,) * nw,
        scratch_shapes=[pltpu.SemaphoreType.DMA((nw,)), pltpu.SemaphoreType.DMA((nw,))], name="pair_swap")(*grads)


def _pair_add(g, other):
    _, r, c = g.shape
    r2, tr = _half_rows(r, c)
    nb = r2 // tr

    def body(g_ref, o_ref, pb_ref, land_ref):
        total = (g_ref[...].astype(F32) + o_ref[...].astype(F32)).astype(BF16)
        pb_ref[...] = total

        @pl.when(pl.program_id(1) == _my_chip())
        def _():
            land_ref[...] = total

    blk = pl.BlockSpec((None, tr, c), lambda j, k: (k, j, 0))
    return pl.pallas_call(
        body, out_shape=(SDS((N_CHIPS, r2, c), BF16),) * 2, grid=(nb, N_CHIPS),
        in_specs=[pl.BlockSpec((None, tr, c), lambda j, k: (k, lax.axis_index("c") * nb + j, 0)), blk],
        out_specs=(blk, pl.BlockSpec((None, tr, c), lambda j, k: (_my_chip(), j, 0))),
        compiler_params=_cparams(("parallel", "arbitrary")), name="pair_add")(g, other)


def _exchange_copy(p_ref, l_ref, send_sems, recv_sems, sem, src_slot, dst_slot, to):
    return pltpu.make_async_remote_copy(src_ref=p_ref.at[src_slot], dst_ref=l_ref.at[dst_slot], send_sem=send_sems.at[sem],
                                        recv_sem=recv_sems.at[sem], device_id=to, device_id_type=MESH)


def _exchange_start(p_refs, l_refs, send_sems, recv_sems):
    x, y, c, chips = _place()
    for wi, (p_ref, l_ref) in enumerate(zip(p_refs, l_refs)):
        for n, chip in enumerate(chips):
            _exchange_copy(p_ref, l_ref, send_sems, recv_sems, 3 * wi + n, 2 * chip[0] + chip[1], 2 * x + y, (*chip, c)).start()


def _exchange_finish(p_refs, l_refs, send_sems, recv_sems):
    x, y, c, chips = _place()
    for wi, (p_ref, l_ref) in enumerate(zip(p_refs, l_refs)):
        for n, chip in enumerate(chips):
            _exchange_copy(p_ref, l_ref, send_sems, recv_sems, 3 * wi + n, 2 * x + y, 2 * chip[0] + chip[1], (x, y, c)).wait_recv()
    for wi, (p_ref, l_ref) in enumerate(zip(p_refs, l_refs)):
        for n, chip in enumerate(chips):
            _exchange_copy(p_ref, l_ref, send_sems, recv_sems, 3 * wi + n, 2 * chip[0] + chip[1], 2 * x + y, (*chip, c)).wait_send()


def _chip_all_to_all(pbs, lands):
    nw = len(pbs)

    def body(*refs):
        _exchange_start(refs[:nw], refs[2 * nw:3 * nw], refs[3 * nw], refs[3 * nw + 1])
        _exchange_finish(refs[:nw], refs[2 * nw:3 * nw], refs[3 * nw], refs[3 * nw + 1])

    return list(pl.pallas_call(
        body, out_shape=tuple(SDS(l.shape, l.dtype) for l in lands), in_specs=[ANY] * (2 * nw), out_specs=(ANY,) * nw,
        input_output_aliases={nw + i: i for i in range(nw)},
        scratch_shapes=[pltpu.SemaphoreType.DMA((3 * nw,)), pltpu.SemaphoreType.DMA((3 * nw,))],
        name="chip_all_to_all")(*pbs, *lands))


def _sum_chips(land, gstack, layer):
    _, r, c = gstack.shape
    r2, tr = _half_rows(r, c)
    nb = r2 // tr

    def body(l_ref, g_ref, out_ref):
        out_ref[...] = ((l_ref[0].astype(F32) + l_ref[1].astype(F32)) + l_ref[2].astype(F32)) + l_ref[3].astype(F32)

    return pl.pallas_call(
        body, out_shape=SDS(gstack.shape, F32), grid=(nb,),
        in_specs=[pl.BlockSpec((N_CHIPS, tr, c), lambda j: (0, j, 0)), ANY],
        out_specs=pl.BlockSpec((None, tr, c), lambda j: (layer, lax.axis_index("c") * nb + j, 0)),
        input_output_aliases={1: 0}, compiler_params=_cparams(("parallel",)), name="sum_chips")(land, gstack)


def _pair_gather(gstacks, layer):
    nw = len(gstacks)

    def body(*refs):
        outs = refs[nw:2 * nw]
        send_sems, recv_sems = refs[2 * nw], refs[2 * nw + 1]
        x, y, c, _ = _place()

        def copy(ref, wi, half):
            r2 = ref.shape[1] // 2
            blk = ref.at[layer, pl.ds(half * r2, r2)]
            return pltpu.make_async_remote_copy(src_ref=blk, dst_ref=blk, send_sem=send_sems.at[wi],
                                                recv_sem=recv_sems.at[wi], device_id=(x, y, 1 - c), device_id_type=MESH)

        sent = [copy(ref, wi, c) for wi, ref in enumerate(outs)]
        for cp in sent:
            cp.start()
        for wi, ref in enumerate(outs):
            copy(ref, wi, 1 - c).wait_recv()
        for cp in sent:
            cp.wait_send()

    return pl.pallas_call(
        body, out_shape=tuple(SDS(g.shape, g.dtype) for g in gstacks), in_specs=[ANY] * nw, out_specs=(ANY,) * nw,
        input_output_aliases={i: i for i in range(nw)},
        scratch_shapes=[pltpu.SemaphoreType.DMA((nw,)), pltpu.SemaphoreType.DMA((nw,))], name="pair_gather")(*gstacks)


def _all_gather_small(vec, name):
    rows, w = vec.shape

    def body(v_ref, sum_ref, all_ref, send_sems, recv_sems):
        x, y, c, chips = _place()
        me, sibling = (x, y, c), (x, y, 1 - c)

        def slot(px, py, pc):
            return all_ref.at[4 * px + 2 * py + pc]

        def copy(k, block, to, src=None):
            return pltpu.make_async_remote_copy(
                src_ref=slot(*block) if src is None else src, dst_ref=slot(*block), send_sem=send_sems.at[k],
                recv_sem=recv_sems.at[k], device_id=to, device_id_type=MESH)

        first = [copy(0, me, sibling, src=v_ref)]
        first += [copy(1 + n, me, (*chip, c), src=v_ref) for n, chip in enumerate(chips)]
        for cp in first:
            cp.start()
        slot(*me)[...] = v_ref[...]
        passed = [copy(4 + n, (*chip, c), sibling) for n, chip in enumerate(chips)]
        for n, chip in enumerate(chips):
            copy(1 + n, (*chip, c), me).wait_recv()
            passed[n].start()
        copy(0, sibling, me).wait_recv()
        for n, chip in enumerate(chips):
            copy(4 + n, (*chip, 1 - c), me).wait_recv()
        for cp in first + passed:
            cp.wait_send()
        total = all_ref[0]
        for dev in range(1, 8):
            total = total + all_ref[dev]
        sum_ref[...] = total

    vm = pl.BlockSpec(memory_space=pltpu.VMEM)
    return pl.pallas_call(
        body, out_shape=(SDS((rows, w), F32), SDS((8, rows, w), F32)), in_specs=[vm], out_specs=(vm, vm),
        scratch_shapes=[pltpu.SemaphoreType.DMA((7,)), pltpu.SemaphoreType.DMA((7,))], name=name)(vec)


def _to_rows128(flat):
    n = flat.shape[0]
    rows = -(-n // (8 * LANES)) * 8
    return jnp.pad(flat, (0, rows * LANES - n)).reshape(rows, LANES)


def _in_layout(conv, ql, kvl, d):
    lay = {"conv": conv, "ql": ql, "kvl": kvl, "d": d}
    lay["q"] = 3 * conv
    lay["kr"] = lay["q"] + ql
    lay["gc"] = lay["kr"] + LANES
    lay["gm"] = lay["gc"] + d
    lay["kv"] = lay["gm"] + d
    used = lay["kv"] + kvl
    lay["width"] = -(-used // 512) * 512
    return lay


def _w_in_to_layout(w, lay):
    conv, ql, kvl, d = lay["conv"], lay["ql"], lay["kvl"], lay["d"]
    o_kv = 3 * conv + ql
    o_kr = o_kv + kvl
    o_g = o_kr + ROPE_DIM
    lead = w.shape[:-1]
    parts = [w[..., :o_kv], w[..., o_kr:o_g], jnp.zeros(lead + (LANES - ROPE_DIM,), w.dtype), w[..., o_g:o_g + 2 * d],
             w[..., o_kv:o_kr], jnp.zeros(lead + (lay["width"] - lay["kv"] - kvl,), w.dtype)]
    return jnp.concatenate(parts, axis=-1)


def _w_in_from_layout(g, lay):
    ql, kvl, d = lay["ql"], lay["kvl"], lay["d"]
    return jnp.concatenate([g[:, :lay["q"] + ql], g[:, lay["kv"]:lay["kv"] + kvl], g[:, lay["kr"]:lay["kr"] + ROPE_DIM],
                            g[:, lay["gc"]:lay["gc"] + 2 * d]], axis=1)


def _w_uq_to_layout(w):
    r = w.shape[0]
    w3 = w.reshape(r, N_HEADS, NOPE_DIM + ROPE_DIM)
    return jnp.pad(w3, ((0, 0), (0, 0), (0, QK_PAD - NOPE_DIM - ROPE_DIM))).reshape(r, N_HEADS * QK_PAD)


def _w_uq_from_layout(g):
    r = g.shape[0]
    return g.reshape(r, N_HEADS, QK_PAD)[:, :, :NOPE_DIM + ROPE_DIM].reshape(r, N_HEADS * (NOPE_DIM + ROPE_DIM))


def _w_ukv_to_layout(w):
    r = w.shape[0]
    return w.reshape(r, N_HEADS, 2, NOPE_DIM).transpose(0, 2, 1, 3).reshape(r, 2 * N_HEADS * NOPE_DIM)


def _w_ukv_from_layout(g):
    r = g.shape[0]
    return g.reshape(r, 2, N_HEADS, NOPE_DIM).transpose(0, 2, 1, 3).reshape(r, 2 * N_HEADS * NOPE_DIM)


def _chips_to_cols(buf):
    _, r, c = buf.shape
    return buf.transpose(1, 0, 2).reshape(r, N_CHIPS * c)


def _cols_to_chips(g):
    r, c4 = g.shape
    return g.reshape(r, N_CHIPS, c4 // N_CHIPS).transpose(1, 0, 2)


GATE_UP = ("ffn1_w_gu", "ffn2_w_gu")
SMALL = ("w_in", "w_conv_out", "w_uq", "w_ukv", "w_ple_proj")


def _big_views(bufs):
    views = {}
    for (name, axis), buf in zip(SHARDED, bufs):
        if axis == 0:
            views[name] = buf.reshape(buf.shape[0], N_CHIPS * buf.shape[2], buf.shape[3])
        elif name in GATE_UP:
            views[name] = buf
    return views


def _small_weights(bufs, layer, lay):
    by_name = dict(zip(BIG, bufs))
    cols = {name: _chips_to_cols(by_name[name][layer]) for name in SMALL}
    cols["w_in"] = _w_in_to_layout(cols["w_in"], lay)
    cols["w_uq"] = _w_uq_to_layout(cols["w_uq"])
    cols["w_ukv"] = _w_ukv_to_layout(cols["w_ukv"])
    return cols


def _ffn_fwd(h, n, w_gu, w_down, layer, tag, next_gain):
    gu = _mm(n, w_gu, "nn", BF16, tag + "_gu_fwd", layer=layer, b_chip=True)
    a = _swiglu_fwd(gu)
    out, n_out = _mm(a, w_down, "nn", F32, tag + "_down_fwd", scale=0.5, res=h, layer=layer, norm_gain=next_gain)
    return out, n_out, (h, n, gu, a)


def _ffn_bwd(dh, dhb, saved, gain, w_gu, w_down, layer, tag):
    h, n, gu, a = saved
    d_wdown = _mm(a, dhb, "tn", BF16, tag + "_down_dw", scale=0.5)
    da = _mm(dhb, w_down, "nt", BF16, tag + "_down_dx", scale=0.5, layer=layer)
    dgu = _swiglu_bwd(gu, da)
    d_wgu = _mm(n, dgu, "tn", BF16, tag + "_gu_dw", out_chip=True)
    dn = _mm(dgu, w_gu, "nt", BF16, tag + "_gu_dx", layer=layer, b_chip=True)
    dh, dhb, dgain = _rmsnorm_bwd(h, gain, dn, dh, tag + "_norm_bwd")
    return dh, dhb, d_wgu, d_wdown, dgain


def _layer_fwd(h0, n0, p_i, bufs, small, conv_w, norms, layer, lay, tables, gather_next, next_gain):
    big = _big_views(bufs)
    h1, n2, s_ffn1 = _ffn_fwd(h0, n0, big["ffn1_w_gu"], big["ffn1_w_down"], layer, "ffn1", norms["mix_norm"])
    proj = _mm(n2, small["w_in"], "nn", BF16, "in_fwd")
    cb = _conv_fwd(proj, conv_w)
    ya = _mm(cb, small["w_conv_out"], "nn", BF16, "conv_out_fwd")
    qn, kvn = _qkvnorm_fwd(proj, lay, norms["q_norm"], norms["kv_norm"])
    qf = _mm(qn, small["w_uq"], "nn", BF16, "uq_fwd")
    kv = _mm(kvn, small["w_ukv"], "nn", BF16, "ukv_fwd")
    qr, kf = _rope_fwd(qf, kv, proj, lay, tables)
    o, lse, new_bufs = _attn_fwd(qr, kf, kv, gather=(bufs, layer + 1) if gather_next else None)
    if gather_next:
        bufs = new_bufs
        big = _big_views(bufs)
    yb = _mm(o, big["w_mla_out"], "nn", BF16, "mla_out_fwd", layer=layer)
    mg = _merge_fwd(proj, lay, ya, yb)
    h2, n3 = _mm(mg, big["w_o"], "nn", F32, "o_fwd", res=h1, layer=layer, norm_gain=norms["ffn2_norm"])
    h3, n4, s_ffn2 = _ffn_fwd(h2, n3, big["ffn2_w_gu"], big["ffn2_w_down"], layer, "ffn2", norms["ple_norm"])
    gp = _mm(n4, big["w_ple_gate"], "nn", BF16, "ple_gate_fwd", layer=layer)
    pp = _mm(p_i, small["w_ple_proj"], "nn", BF16, "ple_proj_fwd")
    h4, n_out = _ple_fwd(h3, gp, pp, next_gain)
    saved = dict(s_ffn1=s_ffn1, h1=h1, n2=n2, proj=proj, cb=cb, ya=ya, qn=qn, kvn=kvn, qr=qr, kf=kf, kv=kv, o=o,
                 lse=lse, yb=yb, mg=mg, h2=h2, s_ffn2=s_ffn2, h3=h3, n4=n4, gp=gp, pp=pp, p=p_i)
    return h4, n_out, saved, bufs


def _layer_bwd(dh, s, big, small, conv_w, norms, layer, lay, tables, exchange):
    gw, gn = {}, {}

    def by_rows(g):
        return g.reshape(N_CHIPS, g.shape[0] // N_CHIPS, g.shape[1])

    dpp, dgp = _ple_bwd(dh, s["gp"], s["pp"])
    gw["w_ple_proj"] = _cols_to_chips(_mm(s["p"], dpp, "tn", BF16, "ple_proj_dw"))
    gw["w_ple_gate"] = by_rows(_mm(s["n4"], dgp, "tn", BF16, "ple_gate_dw"))
    dn4 = _mm(dgp, big["w_ple_gate"], "nt", BF16, "ple_gate_dx", layer=layer)
    dh, dhb, gn["ple_norm"] = _rmsnorm_bwd(s["h3"], norms["ple_norm"], dn4, dh, "ple_norm_bwd")
    dh, dhb, gw["ffn2_w_gu"], g_down, gn["ffn2_norm"] = _ffn_bwd(
        dh, dhb, s["s_ffn2"], norms["ffn2_norm"], big["ffn2_w_gu"], big["ffn2_w_down"], layer, "ffn2")
    gw["ffn2_w_down"] = by_rows(g_down)
    gw["w_o"] = by_rows(_mm(s["mg"], dhb, "tn", BF16, "o_dw"))
    dmg = _mm(dhb, big["w_o"], "nt", BF16, "o_dx", layer=layer)
    dya, dyb, dgc, dgm = _merge_bwd(s["proj"], lay, s["ya"], s["yb"], dmg)
    gw["w_conv_out"] = _cols_to_chips(_mm(s["cb"], dya, "tn", BF16, "conv_out_dw"))
    dcb = _mm(dya, small["w_conv_out"], "nt", BF16, "conv_out_dx")
    db, dc, dv_conv, g_conv = _conv_bwd(s["proj"], conv_w, dcb)
    gw["w_mla_out"] = by_rows(_mm(s["o"], dyb, "tn", BF16, "mla_out_dw"))
    do = _mm(dyb, big["w_mla_out"], "nt", BF16, "mla_out_dx", layer=layer)
    delta = _attn_delta(do, s["o"])
    dkf, dv, dqr, landed = _attn_bwd(s["qr"], s["kf"], s["kv"], do, s["lse"], delta, exchange=exchange)
    dqf, dkv, dkr = _rope_bwd(dqr, dkf, dv, tables)
    gw["w_uq"] = _cols_to_chips(_w_uq_from_layout(_mm(s["qn"], dqf, "tn", BF16, "uq_dw")))
    dqn = _mm(dqf, small["w_uq"], "nt", BF16, "uq_dx")
    gw["w_ukv"] = _cols_to_chips(_w_ukv_from_layout(_mm(s["kvn"], dkv, "tn", BF16, "ukv_dw")))
    dkvn = _mm(dkv, small["w_ukv"], "nt", BF16, "ukv_dx")
    dqc, dkvc, gn["q_norm"], gn["kv_norm"] = _qkvnorm_bwd(s["proj"], lay, norms["q_norm"], norms["kv_norm"], dqn, dkvn)
    t = dh.shape[0]
    dproj = jnp.concatenate([db, dc, dv_conv, dqc, dkr, dgc, dgm, dkvc,
                             jnp.zeros((t, lay["width"] - lay["kv"] - lay["kvl"]), BF16)], axis=1)
    gw["w_in"] = _cols_to_chips(_w_in_from_layout(_mm(s["n2"], dproj, "tn", BF16, "in_dw"), lay))
    dn2 = _mm(dproj, small["w_in"], "nt", BF16, "in_dx")
    dh, dhb, gn["mix_norm"] = _rmsnorm_bwd(s["h1"], norms["mix_norm"], dn2, dh, "mix_norm_bwd")
    dh, dhb, gw["ffn1_w_gu"], g_down, gn["ffn1_norm"] = _ffn_bwd(
        dh, dhb, s["s_ffn1"], norms["ffn1_norm"], big["ffn1_w_gu"], big["ffn1_w_down"], layer, "ffn1")
    gw["ffn1_w_down"] = by_rows(g_down)
    return dh, gw, g_conv, gn, landed


def _rope_tables(positions):
    half = ROPE_DIM // 2
    inv_freq = ROPE_THETA ** (-jnp.arange(0, ROPE_DIM, 2, dtype=F32) / ROPE_DIM)
    ang = positions.astype(F32)[:, None] * inv_freq
    cos, sin = jnp.cos(ang), jnp.sin(ang)
    zeros = jnp.zeros_like(cos)
    cos_t = jnp.concatenate([cos, cos, zeros, zeros], axis=1)
    sin_a = jnp.concatenate([-sin, zeros, zeros, zeros], axis=1)
    sin_b = jnp.concatenate([zeros, sin, zeros, zeros], axis=1)
    assert cos_t.shape[1] == LANES and half * 4 == LANES
    return cos_t, sin_a, sin_b


def kernel(x, p, positions, ffn1_norm, ffn1_w_gu, ffn1_w_down, mix_norm, w_in, conv_w, w_conv_out, q_norm, kv_norm, w_uq, w_ukv, w_mla_out, w_o, ffn2_norm, ffn2_w_gu, ffn2_w_down, ple_norm, w_ple_gate, w_ple_proj, final_norm, loss_target, m_ffn1_norm, m_ffn1_w_gu, m_ffn1_w_down, m_mix_norm, m_w_in, m_conv_w, m_w_conv_out, m_q_norm, m_kv_norm, m_w_uq, m_w_ukv, m_w_mla_out, m_w_o, m_ffn2_norm, m_ffn2_w_gu, m_ffn2_w_down, m_ple_norm, m_w_ple_gate, m_w_ple_proj, m_final_norm, v_ffn1_norm, v_ffn1_w_gu, v_ffn1_w_down, v_mix_norm, v_w_in, v_conv_w, v_w_conv_out, v_q_norm, v_kv_norm, v_w_uq, v_w_ukv, v_w_mla_out, v_w_o, v_ffn2_norm, v_ffn2_w_gu, v_ffn2_w_down, v_ple_norm, v_w_ple_gate, v_w_ple_proj, v_final_norm):
    args = dict(zip(ARG_NAMES, (x, p, positions, ffn1_norm, ffn1_w_gu, ffn1_w_down, mix_norm, w_in, conv_w, w_conv_out, q_norm, kv_norm, w_uq, w_ukv, w_mla_out, w_o, ffn2_norm, ffn2_w_gu, ffn2_w_down, ple_norm, w_ple_gate, w_ple_proj, final_norm, loss_target, m_ffn1_norm, m_ffn1_w_gu, m_ffn1_w_down, m_mix_norm, m_w_in, m_conv_w, m_w_conv_out, m_q_norm, m_kv_norm, m_w_uq, m_w_ukv, m_w_mla_out, m_w_o, m_ffn2_norm, m_ffn2_w_gu, m_ffn2_w_down, m_ple_norm, m_w_ple_gate, m_w_ple_proj, m_final_norm, v_ffn1_norm, v_ffn1_w_gu, v_ffn1_w_down, v_mix_norm, v_w_in, v_conv_w, v_w_conv_out, v_q_norm, v_kv_norm, v_w_uq, v_w_ukv, v_w_mla_out, v_w_o, v_ffn2_norm, v_ffn2_w_gu, v_ffn2_w_down, v_ple_norm, v_w_ple_gate, v_w_ple_proj, v_final_norm)))
    depth = ffn1_norm.shape[0]
    t, d = x.shape[1], x.shape[2]
    conv = conv_w.shape[-1] * N_CHIPS
    lay = _in_layout(conv, q_norm.shape[-1], kv_norm.shape[-1], d)
    chip = 2 * lax.axis_index("x") + lax.axis_index("y")
    tables = _rope_tables(positions[0])

    bufs = _all_gather_weights([_cast_into_slot(args[name]) for name in BIG], 0)
    small = [None] * depth
    small[0] = _small_weights(bufs, 0, lay)
    conv_rows = depth * conv_w.shape[1]
    conv_all = _all_gather_small(_to_rows128(conv_w.reshape(-1)), "all_gather_conv_w")[1]
    conv_full = conv_all[0::2, :conv_rows].reshape(N_CHIPS, depth, conv_w.shape[1], LANES)
    conv_full = conv_full.transpose(1, 2, 0, 3).reshape(depth, conv_w.shape[1], conv)
    norms = [{name: args[name][i] for name in REPLICATED} for i in range(depth)]
    p3 = p.reshape(depth, t, p.shape[-1])

    h = x[0]
    n = _rmsnorm_fwd(h, norms[0]["ffn1_norm"], "first_norm_fwd")
    saved = []
    for i in range(depth):
        h, n, s, bufs = _layer_fwd(h, n, p3[i], bufs, small[i], conv_full[i], norms[i], i, lay, tables, i + 1 < depth,
                                   norms[i + 1]["ffn1_norm"] if i + 1 < depth else None)
        if i + 1 < depth:
            small[i + 1] = _small_weights(bufs, i + 1, lay)
        saved.append(s)
    loss_part, dh, _, g_final = _loss_head(h, final_norm, loss_target[0])
    loss = lax.psum(loss_part[0, 0], ("x", "y", "c"))

    big = _big_views(bufs)
    gstacks = [lax.empty(args[name].shape, F32) for name in BIG]
    norm_grads, conv_grads = [None] * depth, [None] * depth

    def finish(layer, lands, stacks):
        stacks = [_sum_chips(land, gs, layer) for land, gs in zip(lands, stacks)]
        return _pair_gather(stacks, layer)

    pending = None
    for i in reversed(range(depth)):
        dh, gw, conv_grads[i], norm_grads[i], landed = _layer_bwd(
            dh, saved[i], big, small[i], conv_full[i], norms[i], i, lay, tables,
            exchange=None if pending is None else pending[1:])
        if pending is not None:
            gstacks = finish(pending[0], landed, gstacks)
        grads_i = [gw[name] for name in BIG]
        pairs = [_pair_add(g, o) for g, o in zip(grads_i, _pair_swap(grads_i))]
        pending = (i, [pb for pb, _ in pairs], [land for _, land in pairs])
    gstacks = finish(pending[0], _chip_all_to_all(pending[1], pending[2]), gstacks)
    grad_x = dh[None]
    grads = dict(zip(BIG, gstacks))

    pieces = [norm_grads[i][name].reshape(-1) for i in range(depth) for name in REPLICATED]
    pieces += [g_final.reshape(-1)] + [conv_grads[i].reshape(-1) for i in range(depth)]
    vec = _all_gather_small(_to_rows128(jnp.concatenate(pieces)), "all_sum_small")[0].reshape(-1)
    off = 0
    per_name = {name: [] for name in REPLICATED}
    for i in range(depth):
        for name in REPLICATED:
            size = args[name].shape[1]
            per_name[name].append(vec[off:off + size])
            off += size
    for name in REPLICATED:
        grads[name] = jnp.stack(per_name[name])
    grads["final_norm"] = vec[off:off + d]
    off += d
    conv_g = vec[off:off + depth * 3 * conv].reshape(depth, 3, conv)
    grads["conv_w"] = lax.dynamic_slice_in_dim(conv_g, chip * conv_w.shape[-1], conv_w.shape[-1], axis=2)

    delta, new_m, new_v = {}, {}, {}
    for name in WEIGHTS:
        w_, g_, m_, v_ = args[name], grads[name], args["m_" + name], args["v_" + name]
        if w_.ndim == 1:
            outs = _adamw(w_[None], g_[None], m_[None], v_[None])
            delta[name], new_m[name], new_v[name] = (o[0] for o in outs)
        else:
            delta[name], new_m[name], new_v[name] = _adamw(w_, g_, m_, v_)
    return (loss, grad_x, *[grads[n] for n in WEIGHTS], *[delta[n] for n in WEIGHTS],
            *[new_m[n] for n in WEIGHTS], *[new_v[n] for n in WEIGHTS])
```

```python
import functools

import jax
import jax.numpy as jnp
from jax import lax
from jax.experimental import pallas as pl
from jax.experimental.pallas import tpu as pltpu

BF16 = jnp.bfloat16
F32 = jnp.float32
SDS = jax.ShapeDtypeStruct
MESH = pl.DeviceIdType.MESH

N_HEADS = 8
NOPE_DIM = 128
ROPE_DIM = 64
V_DIM = 128
QK_PAD = 256
CHUNK = 64
ROPE_THETA = 10000.0
EPS = 1e-6
ATTN_SCALE = (NOPE_DIM + ROPE_DIM) ** -0.5
NEG_BIG = -1e30

ADAM_LR = 0.001
ADAM_B1 = 0.9
ADAM_B2 = 0.999
ADAM_EPS = 1e-08
ADAM_WD = 0.01
ADAM_STEP = 10

LANES = 128
N_CHIPS = 4
VMEM_LIMIT_BYTES = 56 * 1024 * 1024
ACC_BYTES = 6 * 1024 * 1024
BLOCK_ELEMS = 1 << 19

SHARDED = (("ffn1_w_gu", 1), ("ffn1_w_down", 0), ("w_in", 1), ("w_conv_out", 1), ("w_uq", 1), ("w_ukv", 1),
           ("w_mla_out", 0), ("w_o", 0), ("ffn2_w_gu", 1), ("ffn2_w_down", 0), ("w_ple_gate", 0), ("w_ple_proj", 1))
BIG = tuple(name for name, _ in SHARDED)
REPLICATED = ("ffn1_norm", "mix_norm", "q_norm", "kv_norm", "ffn2_norm", "ple_norm")
WEIGHTS = ("ffn1_norm", "ffn1_w_gu", "ffn1_w_down", "mix_norm", "w_in", "conv_w", "w_conv_out", "q_norm",
           "kv_norm", "w_uq", "w_ukv", "w_mla_out", "w_o", "ffn2_norm", "ffn2_w_gu", "ffn2_w_down",
           "ple_norm", "w_ple_gate", "w_ple_proj", "final_norm")
ARG_NAMES = ("x", "p", "positions") + WEIGHTS + ("loss_target",) + tuple("m_" + n for n in WEIGHTS) + tuple(
    "v_" + n for n in WEIGHTS)


def _cparams(semantics=None):
    return pltpu.CompilerParams(dimension_semantics=semantics, vmem_limit_bytes=VMEM_LIMIT_BYTES)


def _tile(n, cap, mult=LANES):
    best = None
    for t in range(mult, min(n, cap) + 1, mult):
        if n % t == 0:
            best = t
    return n if best is None else best


def _sigmoid(x):
    return 1.0 / (1.0 + jnp.exp(-x))


def _rowspec(tm, width, col_block=0):
    return pl.BlockSpec((tm, width), lambda i: (i, col_block))


def _colspec(tm, width, offset):
    assert offset % width == 0, (width, offset)
    return _rowspec(tm, width, offset // width)


def _mm(a, b, mode, out_dtype, name, scale=None, res=None, layer=None, b_chip=False, out_chip=False, norm_gain=None,
        a_halves=False, b_halves=False, norm_bwd=None):
    bshape = b.shape if layer is None else b.shape[1:]
    if b_chip:
        bshape = (bshape[1], N_CHIPS * bshape[2])
    if b_halves:
        bshape = (bshape[1], 2 * bshape[2])
    ashape = (a.shape[1], 2 * a.shape[2]) if a_halves else a.shape
    if mode == "nn":
        (m, k), (k2, n) = ashape, bshape
    elif mode == "nt":
        (m, k), (n, k2) = ashape, bshape
    else:
        (k, m), (k2, n) = ashape, bshape
    assert k == k2, (a.shape, b.shape, mode)
    n_unit = n // N_CHIPS if (out_chip or (b_chip and mode == "nn")) else n
    k_unit = k // N_CHIPS if (b_chip and mode == "nt") else k
    tn = _tile(n_unit, 1536)
    tm = _tile(m, min(512 if norm_bwd is not None else 1408, ACC_BYTES // (4 * tn)))
    tk = _tile(k_unit, 1536)
    nk = k // tk
    n_per, k_per = n_unit // tn, k_unit // tk
    n_half, k_half = n // 2 // tn, k // 2 // tk
    dims = {"nn": (((1,), (0,)), ((), ())), "nt": (((1,), (1,)), ((), ())), "tn": (((0,), (0,)), ((), ()))}[mode]

    fuse_norm = norm_gain is not None
    fuse_bwd = norm_bwd is not None
    assert not (fuse_norm or fuse_bwd) or (tn == n and not out_chip), (name, tn, n)
    assert not a_halves or mode == "nt"
    assert not b_halves or mode == "tn"
    n_extra_in = (1 if res is not None else 0) + (1 if fuse_norm else 0) + (3 if fuse_bwd else 0)

    def body(*refs):
        a_ref, b_ref = refs[0], refs[1]
        extra = list(refs[2:2 + n_extra_in])
        outs = list(refs[2 + n_extra_in:])
        acc_ref = outs.pop() if nk > 1 else None
        res_ref = extra.pop(0) if res is not None else None
        gain_ref = extra.pop(0) if fuse_norm else None

        def finish(acc):
            if scale is not None:
                acc = acc * scale
            if res_ref is not None:
                acc = res_ref[...] + acc
            if fuse_bwd:
                h_ref, g_ref, dhin_ref = extra
                dh_ref, dhb_ref, dg_ref = outs
                dx, dg = _rn_bwd_math(h_ref[...], g_ref[...], acc)
                dh = dhin_ref[...] + dx
                dh_ref[...] = dh
                dhb_ref[...] = dh.astype(BF16)

                @pl.when(pl.program_id(0) == 0)
                def _():
                    dg_ref[...] = dg

                @pl.when(pl.program_id(0) > 0)
                def _():
                    dg_ref[...] += dg
                return
            outs[0][...] = acc.astype(out_dtype)
            if fuse_norm:
                outs[1][...] = _rn_fwd_math(acc, gain_ref[...]).astype(BF16)

        part = lax.dot_general(a_ref[...].astype(BF16), b_ref[...].astype(BF16), dims,
                               preferred_element_type=F32)
        if nk == 1:
            finish(part)
        else:
            kk = pl.program_id(2)

            @pl.when(kk == 0)
            def _():
                acc_ref[...] = part

            @pl.when(kk > 0)
            def _():
                acc_ref[...] += part

            @pl.when(kk == nk - 1)
            def _():
                finish(acc_ref[...])

    lead = () if layer is None else (layer,)
    lead_block = () if layer is None else (None,)
    if mode == "nn":
        a_spec = pl.BlockSpec((tm, tk), lambda i, j, kk: (i, kk))
        if b_chip:
            b_spec = pl.BlockSpec(lead_block + (None, tk, tn), lambda i, j, kk: lead + (j // n_per, kk, j % n_per))
        else:
            b_spec = pl.BlockSpec(lead_block + (tk, tn), lambda i, j, kk: lead + (kk, j))
    elif mode == "nt":
        if a_halves:
            a_spec = pl.BlockSpec((None, tm, tk), lambda i, j, kk: (kk // k_half, i, kk % k_half))
        else:
            a_spec = pl.BlockSpec((tm, tk), lambda i, j, kk: (i, kk))
        if b_chip:
            b_spec = pl.BlockSpec(lead_block + (None, tn, tk), lambda i, j, kk: lead + (kk // k_per, j, kk % k_per))
        else:
            b_spec = pl.BlockSpec(lead_block + (tn, tk), lambda i, j, kk: lead + (j, kk))
    else:
        assert layer is None and not b_chip
        a_spec = pl.BlockSpec((tk, tm), lambda i, j, kk: (kk, i))
        if b_halves:
            b_spec = pl.BlockSpec((None, tk, tn), lambda i, j, kk: (j // n_half, kk, j % n_half))
        else:
            b_spec = pl.BlockSpec((tk, tn), lambda i, j, kk: (kk, j))
    if out_chip:
        o_spec = pl.BlockSpec((None, tm, tn), lambda i, j, kk: (j // n_per, i, j % n_per))
        out_shape = SDS((N_CHIPS, m, n_unit), out_dtype)
    else:
        o_spec = pl.BlockSpec((tm, tn), lambda i, j, kk: (i, j))
        out_shape = SDS((m, n), out_dtype)
    in_specs = [a_spec, b_spec] + ([o_spec] if res is not None else [])
    operands = (a, b) + ((res,) if res is not None else ())
    out_specs = o_spec
    vec = pl.BlockSpec((1, tn), lambda i, j, kk: (0, j))
    if fuse_norm:
        in_specs.append(vec)
        operands += (norm_gain.reshape(1, n),)
        out_shape, out_specs = (out_shape, SDS((m, n), BF16)), (o_spec, o_spec)
    if fuse_bwd:
        h, gain, dh_in = norm_bwd
        in_specs += [o_spec, vec, o_spec]
        operands += (h, gain.reshape(1, n), dh_in)
        out_shape = (SDS((m, n), F32), SDS((m, n), BF16), SDS((1, n), F32))
        out_specs = (o_spec, o_spec, vec)
    return pl.pallas_call(
        body, out_shape=out_shape, grid=(m // tm, n // tn, nk), in_specs=in_specs, out_specs=out_specs,
        scratch_shapes=[pltpu.VMEM((tm, tn), F32)] if nk > 1 else [],
        compiler_params=_cparams(("arbitrary",) * 3 if fuse_bwd else ("parallel", "parallel", "arbitrary")), name=name)(*operands)


def _rn_fwd_math(x, g):
    r = lax.rsqrt(jnp.mean(x * x, axis=-1, keepdims=True) + EPS)
    return (x * r) * g


def _rn_bwd_math(x, g, dn):
    r = lax.rsqrt(jnp.mean(x * x, axis=-1, keepdims=True) + EPS)
    xh = x * r
    gy = dn * g
    dx = r * (gy - xh * jnp.mean(gy * xh, axis=-1, keepdims=True))
    dg = jnp.sum(dn * xh, axis=0, keepdims=True)
    return dx, dg


def _rmsnorm_fwd(h, gain, name):
    t, d = h.shape
    tm = _tile(t, 512, 8)

    def body(h_ref, g_ref, o_ref):
        o_ref[...] = _rn_fwd_math(h_ref[...], g_ref[...]).astype(BF16)

    return pl.pallas_call(
        body, out_shape=SDS((t, d), BF16), grid=(t // tm,),
        in_specs=[_rowspec(tm, d), pl.BlockSpec((1, d), lambda i: (0, 0))], out_specs=_rowspec(tm, d),
        compiler_params=_cparams(("parallel",)), name=name)(h, gain.reshape(1, d))


def _loss_head(h, gain, target):
    t, d = h.shape
    tm = _tile(t, 512, 8)

    def body(h_ref, g_ref, t_ref, loss_ref, dh_ref, dhb_ref, dg_ref):
        x, g = h_ref[...], g_ref[...]
        err = _rn_fwd_math(x, g) - t_ref[...]
        part = 0.5 * jnp.sum(jnp.sum(err * err, axis=1, keepdims=True), axis=0, keepdims=True) * (1.0 / d)
        dx, dg = _rn_bwd_math(x, g, err * (1.0 / d))
        dh_ref[...] = dx
        dhb_ref[...] = dx.astype(BF16)

        @pl.when(pl.program_id(0) == 0)
        def _():
            dg_ref[...] = dg
            loss_ref[...] = jnp.broadcast_to(part, (1, LANES))

        @pl.when(pl.program_id(0) > 0)
        def _():
            dg_ref[...] += dg
            loss_ref[...] += jnp.broadcast_to(part, (1, LANES))

    vec = pl.BlockSpec((1, d), lambda i: (0, 0))
    return pl.pallas_call(
        body, out_shape=(SDS((1, LANES), F32), SDS((t, d), F32), SDS((t, d), BF16), SDS((1, d), F32)),
        grid=(t // tm,), in_specs=[_rowspec(tm, d), vec, _rowspec(tm, d)],
        out_specs=(pl.BlockSpec((1, LANES), lambda i: (0, 0)), _rowspec(tm, d), _rowspec(tm, d), vec),
        compiler_params=_cparams(("arbitrary",)), name="loss_head")(h, gain.reshape(1, d), target)


def _gu_swiglu_fwd(n, w_gu, layer, name):
    t, d = n.shape
    cols = w_gu.shape[3]
    f = 2 * cols
    tn = _tile(cols, 1536)
    tm = _tile(t, 512)
    per = cols // tn

    def body(n_ref, wg_ref, wu_ref, gu_ref, a_ref):
        x = n_ref[...]
        g = jnp.dot(x, wg_ref[...], preferred_element_type=F32)
        u = jnp.dot(x, wu_ref[...], preferred_element_type=F32)
        gu_ref[0] = g.astype(BF16)
        gu_ref[1] = u.astype(BF16)
        a_ref[...] = (g * _sigmoid(g) * u).astype(BF16)

    return pl.pallas_call(
        body, out_shape=(SDS((2, t, f), BF16), SDS((t, f), BF16)), grid=(f // tn, t // tm),
        in_specs=[pl.BlockSpec((tm, d), lambda j, i: (i, 0)),
                  pl.BlockSpec((None, None, d, tn), lambda j, i: (layer, j // per, 0, j % per)),
                  pl.BlockSpec((None, None, d, tn), lambda j, i: (layer, 2 + j // per, 0, j % per))],
        out_specs=(pl.BlockSpec((2, tm, tn), lambda j, i: (0, i, j)), pl.BlockSpec((tm, tn), lambda j, i: (i, j))),
        compiler_params=_cparams(("parallel", "parallel")), name=name)(n, w_gu, w_gu)


def _down_dx_swiglu_bwd(dhb, w_down, gu, layer, name):
    t, d = dhb.shape
    f = gu.shape[2]
    tn = _tile(f, 1536)
    tm = _tile(t, 512)

    def body(dh_ref, w_ref, gu_ref, dgu_ref):
        da = 0.5 * lax.dot_general(dh_ref[...], w_ref[...], _NT, preferred_element_type=F32)
        g = gu_ref[0].astype(F32)
        u = gu_ref[1].astype(F32)
        sg = _sigmoid(g)
        dgu_ref[0] = (da * u * (sg * (1.0 + g * (1.0 - sg)))).astype(BF16)
        dgu_ref[1] = (da * (g * sg)).astype(BF16)

    blk = pl.BlockSpec((2, tm, tn), lambda j, i: (0, i, j))
    return pl.pallas_call(
        body, out_shape=SDS((2, t, f), BF16), grid=(f // tn, t // tm),
        in_specs=[pl.BlockSpec((tm, d), lambda j, i: (i, 0)), pl.BlockSpec((None, tn, d), lambda j, i: (layer, j, 0)), blk],
        out_specs=blk, compiler_params=_cparams(("parallel", "parallel")), name=name)(dhb, w_down, gu)


def _shift_down(z, k, row):
    return jnp.where(row >= k, pltpu.roll(z, k, 0), 0.0)


def _shift_up(z, k, row, t):
    return jnp.where(row < t - k, pltpu.roll(z, t - k, 0), 0.0)


def _conv_specs(t, conv):
    nb = conv // LANES
    return [pl.BlockSpec((t, LANES), lambda j: (0, j)), pl.BlockSpec((t, LANES), lambda j: (0, nb + j)),
            pl.BlockSpec((t, LANES), lambda j: (0, 2 * nb + j))]


def _conv_fwd(proj, conv_w):
    t = proj.shape[0]
    conv = conv_w.shape[1]

    def body(b_ref, c_ref, v_ref, w_ref, o_ref):
        z = c_ref[...].astype(F32) * v_ref[...].astype(F32)
        row = lax.broadcasted_iota(jnp.int32, z.shape, 0)
        y = w_ref[0:1, :] * _shift_down(z, 2, row) + w_ref[1:2, :] * _shift_down(z, 1, row) + w_ref[2:3, :] * z
        o_ref[...] = (b_ref[...].astype(F32) * y).astype(BF16)

    cspec = pl.BlockSpec((t, LANES), lambda j: (0, j))
    return pl.pallas_call(
        body, out_shape=SDS((t, conv), BF16), grid=(conv // LANES,),
        in_specs=_conv_specs(t, conv) + [pl.BlockSpec((3, LANES), lambda j: (0, j))], out_specs=cspec,
        compiler_params=_cparams(("parallel",)), name="conv_fwd")(proj, proj, proj, conv_w)


def _conv_bwd(proj, conv_w, dcb):
    t = proj.shape[0]
    conv = conv_w.shape[1]

    def body(b_ref, c_ref, v_ref, w_ref, d_ref, db_ref, dc_ref, dv_ref, dw_ref):
        b, c, v = b_ref[...].astype(F32), c_ref[...].astype(F32), v_ref[...].astype(F32)
        d = d_ref[...].astype(F32)
        z = c * v
        row = lax.broadcasted_iota(jnp.int32, z.shape, 0)
        z1, z2 = _shift_down(z, 1, row), _shift_down(z, 2, row)
        w0, w1, w2 = w_ref[0:1, :], w_ref[1:2, :], w_ref[2:3, :]
        y = w0 * z2 + w1 * z1 + w2 * z
        dy = d * b
        db_ref[...] = (d * y).astype(BF16)
        dz = w2 * dy + w1 * _shift_up(dy, 1, row, t) + w0 * _shift_up(dy, 2, row, t)
        dc_ref[...] = (dz * v).astype(BF16)
        dv_ref[...] = (dz * c).astype(BF16)
        dw_ref[0:1, :] = jnp.sum(dy * z2, axis=0, keepdims=True)
        dw_ref[1:2, :] = jnp.sum(dy * z1, axis=0, keepdims=True)
        dw_ref[2:3, :] = jnp.sum(dy * z, axis=0, keepdims=True)

    cspec = pl.BlockSpec((t, LANES), lambda j: (0, j))
    wspec = pl.BlockSpec((3, LANES), lambda j: (0, j))
    return pl.pallas_call(
        body, out_shape=(SDS((t, conv), BF16),) * 3 + (SDS((3, conv), F32),), grid=(conv // LANES,),
        in_specs=_conv_specs(t, conv) + [wspec, cspec], out_specs=(cspec, cspec, cspec, wspec),
        compiler_params=_cparams(("parallel",)), name="conv_bwd")(proj, proj, proj, conv_w, dcb)


def _qkvnorm_fwd(proj, lay, q_gain, kv_gain):
    t = proj.shape[0]
    ql, kvl = lay["ql"], lay["kvl"]
    tm = _tile(t, 512, 8)

    def body(q_ref, kv_ref, gq_ref, gkv_ref, qn_ref, kvn_ref):
        qn_ref[...] = _rn_fwd_math(q_ref[...].astype(F32), gq_ref[...]).astype(BF16)
        kvn_ref[...] = _rn_fwd_math(kv_ref[...].astype(F32), gkv_ref[...]).astype(BF16)

    return pl.pallas_call(
        body, out_shape=(SDS((t, ql), BF16), SDS((t, kvl), BF16)), grid=(t // tm,),
        in_specs=[_colspec(tm, ql, lay["q"]), _colspec(tm, kvl, lay["kv"]),
                  pl.BlockSpec((1, ql), lambda i: (0, 0)), pl.BlockSpec((1, kvl), lambda i: (0, 0))],
        out_specs=(_rowspec(tm, ql), _rowspec(tm, kvl)), compiler_params=_cparams(("parallel",)),
        name="qkvnorm_fwd")(proj, proj, q_gain.reshape(1, ql), kv_gain.reshape(1, kvl))


def _qkvnorm_bwd(proj, lay, q_gain, kv_gain, dqn, dkvn):
    t = proj.shape[0]
    ql, kvl = lay["ql"], lay["kvl"]
    tm = _tile(t, 512, 8)

    def body(q_ref, kv_ref, gq_ref, gkv_ref, dqn_ref, dkvn_ref, dq_ref, dkv_ref, dgq_ref, dgkv_ref):
        dq, dgq = _rn_bwd_math(q_ref[...].astype(F32), gq_ref[...], dqn_ref[...].astype(F32))
        dkv, dgkv = _rn_bwd_math(kv_ref[...].astype(F32), gkv_ref[...], dkvn_ref[...].astype(F32))
        dq_ref[...] = dq.astype(BF16)
        dkv_ref[...] = dkv.astype(BF16)

        @pl.when(pl.program_id(0) == 0)
        def _():
            dgq_ref[...] = dgq
            dgkv_ref[...] = dgkv

        @pl.when(pl.program_id(0) > 0)
        def _():
            dgq_ref[...] += dgq
            dgkv_ref[...] += dgkv

    vq = pl.BlockSpec((1, ql), lambda i: (0, 0))
    vkv = pl.BlockSpec((1, kvl), lambda i: (0, 0))
    return pl.pallas_call(
        body, out_shape=(SDS((t, ql), BF16), SDS((t, kvl), BF16), SDS((1, ql), F32), SDS((1, kvl), F32)),
        grid=(t // tm,),
        in_specs=[_colspec(tm, ql, lay["q"]), _colspec(tm, kvl, lay["kv"]), vq, vkv, _rowspec(tm, ql),
                  _rowspec(tm, kvl)],
        out_specs=(_rowspec(tm, ql), _rowspec(tm, kvl), vq, vkv), compiler_params=_cparams(("arbitrary",)),
        name="qkvnorm_bwd")(proj, proj, q_gain.reshape(1, ql), kv_gain.reshape(1, kvl), dqn, dkvn)


def _rope(x, cos_t, sin_a, sin_b):
    return x * cos_t + pltpu.roll(x, LANES - ROPE_DIM // 2, 1) * sin_a + pltpu.roll(x, ROPE_DIM // 2, 1) * sin_b


def _rope_fwd(qf, kv, proj, lay, tables):
    t = qf.shape[0]
    tm = _tile(t, 256, 8)
    hq = N_HEADS * QK_PAD

    def body(q_ref, kn_ref, kr_ref, cos_ref, sa_ref, sb_ref, qr_ref, kf_ref):
        cos_t, sin_a, sin_b = cos_ref[...], sa_ref[...], sb_ref[...]
        kr = _rope(kr_ref[...].astype(F32), cos_t, sin_a, sin_b).astype(BF16)
        for h in range(N_HEADS):
            lo = h * QK_PAD
            qr_ref[:, lo:lo + NOPE_DIM] = q_ref[:, lo:lo + NOPE_DIM]
            qr_ref[:, lo + NOPE_DIM:lo + QK_PAD] = _rope(
                q_ref[:, lo + NOPE_DIM:lo + QK_PAD].astype(F32), cos_t, sin_a, sin_b).astype(BF16)
            kf_ref[:, lo:lo + NOPE_DIM] = kn_ref[:, h * NOPE_DIM:(h + 1) * NOPE_DIM]
            kf_ref[:, lo + NOPE_DIM:lo + QK_PAD] = kr

    tab = _rowspec(tm, LANES)
    return pl.pallas_call(
        body, out_shape=(SDS((t, hq), BF16), SDS((t, hq), BF16)), grid=(t // tm,),
        in_specs=[_rowspec(tm, hq), _rowspec(tm, N_HEADS * NOPE_DIM), _colspec(tm, LANES, lay["kr"]), tab, tab, tab],
        out_specs=(_rowspec(tm, hq), _rowspec(tm, hq)), compiler_params=_cparams(("parallel",)),
        name="rope_fwd")(qf, kv, proj, *tables)


def _rope_bwd(dqr, dkf, dv, tables):
    t = dqr.shape[0]
    tm = _tile(t, 256, 8)
    hq = N_HEADS * QK_PAD
    hn = N_HEADS * NOPE_DIM

    def body(dq_ref, dk_ref, dv_ref, cos_ref, sa_ref, sb_ref, dqf_ref, dkv_ref, dkr_ref):
        cos_t, sin_a, sin_b = cos_ref[...], -sa_ref[...], -sb_ref[...]
        dkr = jnp.zeros((tm, LANES), F32)
        for h in range(N_HEADS):
            lo = h * QK_PAD
            dqf_ref[:, lo:lo + NOPE_DIM] = dq_ref[:, lo:lo + NOPE_DIM].astype(BF16)
            dqf_ref[:, lo + NOPE_DIM:lo + QK_PAD] = _rope(
                dq_ref[:, lo + NOPE_DIM:lo + QK_PAD].astype(F32), cos_t, sin_a, sin_b).astype(BF16)
            dkv_ref[:, h * NOPE_DIM:(h + 1) * NOPE_DIM] = dk_ref[:, lo:lo + NOPE_DIM]
            dkr = dkr + dk_ref[:, lo + NOPE_DIM:lo + QK_PAD].astype(F32)
        dkv_ref[:, hn:] = dv_ref[...]
        dkr_ref[...] = _rope(dkr, cos_t, sin_a, sin_b).astype(BF16)

    tab = _rowspec(tm, LANES)
    return pl.pallas_call(
        body, out_shape=(SDS((t, hq), BF16), SDS((t, 2 * hn), BF16), SDS((t, LANES), BF16)), grid=(t // tm,),
        in_specs=[_rowspec(tm, hq), _rowspec(tm, hq), _rowspec(tm, hn), tab, tab, tab],
        out_specs=(_rowspec(tm, hq), _rowspec(tm, 2 * hn), tab), compiler_params=_cparams(("parallel",)),
        name="rope_bwd")(dqr, dkf, dv, *tables)


def _chunk_mask(bq):
    qc = lax.broadcasted_iota(jnp.int32, (bq, bq), 0) // CHUNK
    kc = lax.broadcasted_iota(jnp.int32, (bq, bq), 1) // CHUNK
    return kc <= qc


_NT = (((1,), (1,)), ((), ()))
_TN = (((0,), (0,)), ((), ()))
LOG2E = 1.4426950408889634
EXP2_SCALE = ATTN_SCALE * LOG2E


def _attn_block(t):
    return 512 if t >= 2048 else 128


def _two_slot_pipeline(unmasked, issue, consume, carry):
    issue(0, 0)

    def pair(n, c):
        issue(2 * n + 1, 1)
        c = consume(2 * n, 0, c, False)
        issue(2 * n + 2, 0)
        return consume(2 * n + 1, 1, c, False)

    carry = lax.fori_loop(0, unmasked // 2, pair, carry)

    def even(c):
        return consume(unmasked, 0, c, True)

    def odd(c):
        issue(unmasked, 1)
        c = consume(unmasked - 1, 0, c, False)
        return consume(unmasked, 1, c, True)

    return lax.cond(unmasked % 2 == 0, even, odd, carry)


def _attn_fwd(qr, kf, kv, gather=None):
    t = qr.shape[0]
    bq = _attn_block(t)
    nq = t // bq
    bufs, layer = gather if gather is not None else ((), None)
    nw = len(bufs)

    def body(*refs):
        q_ref, k_ref, v_ref = refs[:3]
        o_ref, lse_ref = refs[3 + nw:5 + nw]
        buf_refs = refs[5 + nw:5 + 2 * nw]
        vaug_ref, s_ref = refs[5 + 2 * nw], refs[6 + 2 * nw]
        sems = refs[7 + 2 * nw:]
        h, i = pl.program_id(0), pl.program_id(1)

        if nw:
            @pl.when((h == 0) & (i == 0))
            def _():
                _gather_start(buf_refs, layer, *sems)

        @pl.when(i == 0)
        def _():
            vaug_ref[:, :V_DIM] = v_ref[...]
            vaug_ref[:, V_DIM:] = jnp.ones((t, V_DIM), BF16)

        def issue(j, slot):
            off = pl.multiple_of(j * bq, bq)
            s_ref[slot] = lax.dot_general(q_ref[...], k_ref[pl.ds(off, bq), :], _NT, preferred_element_type=F32)

        def consume(j, slot, carry, masked):
            m, acc = carry
            off = pl.multiple_of(j * bq, bq)
            s = s_ref[slot]
            if masked:
                s = jnp.where(_chunk_mask(bq), s, NEG_BIG)
            m_new = jnp.maximum(m, jnp.max(s, axis=1, keepdims=True))
            alpha = jnp.exp2((m - m_new) * EXP2_SCALE)
            pr = jnp.exp2((s - m_new) * EXP2_SCALE)
            acc = alpha * acc + jnp.dot(pr.astype(BF16), vaug_ref[pl.ds(off, bq), :], preferred_element_type=F32)
            return m_new, acc

        init = (jnp.full((bq, 1), NEG_BIG, F32), jnp.zeros((bq, 2 * V_DIM), F32))
        m, acc = _two_slot_pipeline(i, issue, consume, init)
        l = acc[:, V_DIM:V_DIM + 1]
        o_ref[...] = (acc[:, :V_DIM] / l).astype(BF16)
        lse_ref[0] = jnp.broadcast_to(m * ATTN_SCALE + jnp.log(l), (bq, LANES))

        if nw:
            @pl.when((h == N_HEADS - 1) & (i == nq - 1))
            def _():
                _gather_finish(buf_refs, layer, *sems)

    out_shape = (SDS((t, N_HEADS * V_DIM), BF16), SDS((N_HEADS, t, LANES), F32)) + tuple(SDS(b.shape, b.dtype) for b in bufs)
    sem_shapes = [pltpu.SemaphoreType.DMA((6 * nw,)), pltpu.SemaphoreType.DMA((6 * nw,))] if nw else []
    outs = pl.pallas_call(
        body, out_shape=out_shape, grid=(N_HEADS, nq),
        in_specs=[pl.BlockSpec((bq, QK_PAD), lambda h, i: (i, h)), pl.BlockSpec((t, QK_PAD), lambda h, i: (0, h)),
                  pl.BlockSpec((t, V_DIM), lambda h, i: (0, N_HEADS + h))] + [ANY] * nw,
        out_specs=(pl.BlockSpec((bq, V_DIM), lambda h, i: (i, h)),
                   pl.BlockSpec((1, bq, LANES), lambda h, i: (h, i, 0))) + (ANY,) * nw,
        input_output_aliases={3 + n: 2 + n for n in range(nw)},
        scratch_shapes=[pltpu.VMEM((t, 2 * V_DIM), BF16), pltpu.VMEM((2, bq, bq), F32)] + sem_shapes,
        compiler_params=_cparams(("arbitrary", "arbitrary")), name="attn_fwd_gather" if nw else "attn_fwd")(qr, kf, kv, *bufs)
    return outs[0], outs[1], list(outs[2:])


def _attn_delta(do, o):
    t = do.shape[0]
    tm = _tile(t, 512, 8)

    def body(do_ref, o_ref, dl_ref):
        prod = do_ref[...].astype(F32) * o_ref[...].astype(F32)
        for h in range(N_HEADS):
            s = jnp.sum(prod[:, h * V_DIM:(h + 1) * V_DIM], axis=1, keepdims=True)
            dl_ref[h] = jnp.broadcast_to(s, (tm, LANES))

    return pl.pallas_call(
        body, out_shape=SDS((N_HEADS, t, LANES), F32), grid=(t // tm,),
        in_specs=[_rowspec(tm, N_HEADS * V_DIM), _rowspec(tm, N_HEADS * V_DIM)],
        out_specs=pl.BlockSpec((N_HEADS, tm, LANES), lambda i: (0, i, 0)), compiler_params=_cparams(("parallel",)),
        name="attn_delta")(do, o)


def _attn_bwd(qr, kf, kv, do, lse, delta, exchange=None):
    t = qr.shape[0]
    bq = _attn_block(t)
    nq = t // bq
    pbs, lands = exchange if exchange is not None else ((), ())
    nw = len(pbs)

    def body(*refs):
        k_ref, v_ref, q_ref, do_ref, lse_ref, dl_ref = refs[:6]
        pb_refs = refs[6:6 + nw]
        dk_ref, dv_ref, dq_ref = refs[6 + 2 * nw:9 + 2 * nw]
        land_refs = refs[9 + 2 * nw:9 + 3 * nw]
        s_ref, dp_ref = refs[9 + 3 * nw], refs[10 + 3 * nw]
        sems = refs[11 + 3 * nw:]
        h, j = pl.program_id(0), pl.program_id(1)

        if nw:
            @pl.when((h == 0) & (j == 0))
            def _():
                _exchange_start(pb_refs, land_refs, *sems)

        @pl.when(j == 0)
        def _():
            dq_ref[...] = jnp.zeros_like(dq_ref)

        k = k_ref[...]
        v = v_ref[...]

        def issue(b, slot):
            off = pl.multiple_of((nq - 1 - b) * bq, bq)
            s_ref[slot] = lax.dot_general(q_ref[pl.ds(off, bq), :], k, _NT, preferred_element_type=F32)
            dp_ref[slot] = lax.dot_general(do_ref[pl.ds(off, bq), :], v, _NT, preferred_element_type=F32)

        def consume(b, slot, carry, masked):
            dk, dv = carry
            off = pl.multiple_of((nq - 1 - b) * bq, bq)
            q = q_ref[pl.ds(off, bq), :]
            do_ = do_ref[pl.ds(off, bq), :]
            lse2 = lse_ref[0, pl.ds(off, bq), :][:, :1] * LOG2E
            dl_i = dl_ref[0, pl.ds(off, bq), :][:, :1]
            s = s_ref[slot]
            if masked:
                s = jnp.where(_chunk_mask(bq), s, NEG_BIG)
            pr = jnp.exp2(s * EXP2_SCALE - lse2)
            dv = dv + lax.dot_general(pr.astype(BF16), do_, _TN, preferred_element_type=F32)
            ds = (pr * (dp_ref[slot] - dl_i)).astype(BF16)
            dk = dk + lax.dot_general(ds, q, _TN, preferred_element_type=F32)
            dq_ref[pl.ds(off, bq), :] += jnp.dot(ds, k, preferred_element_type=F32) * ATTN_SCALE
            return dk, dv

        init = (jnp.zeros((bq, QK_PAD), F32), jnp.zeros((bq, V_DIM), F32))
        dk, dv = _two_slot_pipeline(nq - 1 - j, issue, consume, init)
        dk_ref[...] = (dk * ATTN_SCALE).astype(BF16)
        dv_ref[...] = dv.astype(BF16)

        if nw:
            @pl.when((h == N_HEADS - 1) & (j == nq - 1))
            def _():
                _exchange_finish(pb_refs, land_refs, *sems)

    stat = pl.BlockSpec((1, t, LANES), lambda h, j: (h, 0, 0))
    out_shape = (SDS((t, N_HEADS * QK_PAD), BF16), SDS((t, N_HEADS * V_DIM), BF16), SDS((t, N_HEADS * QK_PAD), F32))
    sem_shapes = [pltpu.SemaphoreType.DMA((3 * nw,)), pltpu.SemaphoreType.DMA((3 * nw,))] if nw else []
    outs = pl.pallas_call(
        body, out_shape=out_shape + tuple(SDS(l.shape, l.dtype) for l in lands), grid=(N_HEADS, nq),
        in_specs=[pl.BlockSpec((bq, QK_PAD), lambda h, j: (j, h)), pl.BlockSpec((bq, V_DIM), lambda h, j: (j, N_HEADS + h)),
                  pl.BlockSpec((t, QK_PAD), lambda h, j: (0, h)), pl.BlockSpec((t, V_DIM), lambda h, j: (0, h)), stat, stat]
        + [ANY] * (2 * nw),
        out_specs=(pl.BlockSpec((bq, QK_PAD), lambda h, j: (j, h)), pl.BlockSpec((bq, V_DIM), lambda h, j: (j, h)),
                   pl.BlockSpec((t, QK_PAD), lambda h, j: (0, h))) + (ANY,) * nw,
        input_output_aliases={6 + nw + n: 3 + n for n in range(nw)},
        scratch_shapes=[pltpu.VMEM((2, bq, bq), F32), pltpu.VMEM((2, bq, bq), F32)] + sem_shapes,
        compiler_params=_cparams(("arbitrary", "arbitrary")),
        name="attn_bwd_exchange" if nw else "attn_bwd")(kf, kv, qr, do, lse, delta, *pbs, *lands)
    return outs[0], outs[1], outs[2], list(outs[3:])


def _merge_fwd(proj, lay, ya, yb):
    t, d = ya.shape
    tm = _tile(t, 512, 8)

    def body(gc_ref, gm_ref, ya_ref, yb_ref, o_ref):
        o_ref[...] = (_sigmoid(gc_ref[...].astype(F32)) * ya_ref[...].astype(F32)
                      + _sigmoid(gm_ref[...].astype(F32)) * yb_ref[...].astype(F32)).astype(BF16)

    return pl.pallas_call(
        body, out_shape=SDS((t, d), BF16), grid=(t // tm,),
        in_specs=[_colspec(tm, d, lay["gc"]), _colspec(tm, d, lay["gm"]), _rowspec(tm, d), _rowspec(tm, d)],
        out_specs=_rowspec(tm, d), compiler_params=_cparams(("parallel",)), name="merge_fwd")(proj, proj, ya, yb)


def _merge_bwd(proj, lay, ya, yb, dmg):
    t, d = ya.shape
    tm = _tile(t, 512, 8)

    def body(gc_ref, gm_ref, ya_ref, yb_ref, d_ref, dya_ref, dyb_ref, dgc_ref, dgm_ref):
        dm = d_ref[...].astype(F32)
        sc = _sigmoid(gc_ref[...].astype(F32))
        sm = _sigmoid(gm_ref[...].astype(F32))
        dya_ref[...] = (dm * sc).astype(BF16)
        dyb_ref[...] = (dm * sm).astype(BF16)
        dgc_ref[...] = (dm * ya_ref[...].astype(F32) * (sc * (1.0 - sc))).astype(BF16)
        dgm_ref[...] = (dm * yb_ref[...].astype(F32) * (sm * (1.0 - sm))).astype(BF16)

    r = _rowspec(tm, d)
    return pl.pallas_call(
        body, out_shape=(SDS((t, d), BF16),) * 4, grid=(t // tm,),
        in_specs=[_colspec(tm, d, lay["gc"]), _colspec(tm, d, lay["gm"]), r, r, r], out_specs=(r, r, r, r),
        compiler_params=_cparams(("parallel",)), name="merge_bwd")(proj, proj, ya, yb, dmg)


def _ple_fwd(h, gp, pp, norm_gain=None):
    t, d = h.shape
    tm = _tile(t, 512, 8)
    fuse_norm = norm_gain is not None

    def body(h_ref, gp_ref, pp_ref, *rest):
        out = h_ref[...] + _sigmoid(gp_ref[...].astype(F32)) * pp_ref[...].astype(F32)
        rest[-2 if fuse_norm else -1][...] = out
        if fuse_norm:
            rest[-1][...] = _rn_fwd_math(out, rest[0][...]).astype(BF16)

    r = _rowspec(tm, d)
    if not fuse_norm:
        return pl.pallas_call(body, out_shape=SDS((t, d), F32), grid=(t // tm,), in_specs=[r, r, r], out_specs=r,
                              compiler_params=_cparams(("parallel",)), name="ple_fwd")(h, gp, pp), None
    return pl.pallas_call(
        body, out_shape=(SDS((t, d), F32), SDS((t, d), BF16)), grid=(t // tm,),
        in_specs=[r, r, r, pl.BlockSpec((1, d), lambda i: (0, 0))], out_specs=(r, r),
        compiler_params=_cparams(("parallel",)), name="ple_fwd_norm")(h, gp, pp, norm_gain.reshape(1, d))


def _ple_bwd(dh, gp, pp):
    t, d = dh.shape
    tm = _tile(t, 512, 8)

    def body(dh_ref, gp_ref, pp_ref, dpp_ref, dgp_ref):
        g = dh_ref[...]
        s = _sigmoid(gp_ref[...].astype(F32))
        dpp_ref[...] = (g * s).astype(BF16)
        dgp_ref[...] = (g * pp_ref[...].astype(F32) * (s * (1.0 - s))).astype(BF16)

    r = _rowspec(tm, d)
    return pl.pallas_call(body, out_shape=(SDS((t, d), BF16),) * 2, grid=(t // tm,), in_specs=[r, r, r],
                          out_specs=(r, r), compiler_params=_cparams(("parallel",)), name="ple_bwd")(dh, gp, pp)


def _adamw(w, g, m, v):
    shape = w.shape
    cols = shape[-1]
    rows = w.size // cols
    tr = _tile(rows, max(8, BLOCK_ELEMS // cols // 8 * 8), 8)

    def body(w_ref, g_ref, m_ref, v_ref, d_ref, nm_ref, nv_ref):
        g_ = g_ref[...]
        nm = ADAM_B1 * m_ref[...] + (1.0 - ADAM_B1) * g_
        nv = ADAM_B2 * v_ref[...] + (1.0 - ADAM_B2) * (g_ * g_)
        m_hat = nm / (1.0 - ADAM_B1 ** ADAM_STEP)
        v_hat = nv / (1.0 - ADAM_B2 ** ADAM_STEP)
        d_ref[...] = -ADAM_LR * (m_hat / (jnp.sqrt(v_hat) + ADAM_EPS) + ADAM_WD * w_ref[...])
        nm_ref[...] = nm
        nv_ref[...] = nv

    r = _rowspec(tr, cols)
    outs = pl.pallas_call(
        body, out_shape=(SDS((rows, cols), F32),) * 3, grid=(rows // tr,), in_specs=[r, r, r, r], out_specs=(r, r, r),
        compiler_params=_cparams(("parallel",)), name="adamw")(*(a.reshape(rows, cols) for a in (w, g, m, v)))
    return tuple(o.reshape(shape) for o in outs)


ANY = pl.BlockSpec(memory_space=pl.ANY)


def _place():
    x, y, c = lax.axis_index("x"), lax.axis_index("y"), lax.axis_index("c")
    return x, y, c, [(1 - x, y), (x, 1 - y), (1 - x, 1 - y)]


def _half_rows(rows, cols):
    half = rows // 2
    return half, _tile(half, max(16, BLOCK_ELEMS // cols // 16 * 16), 16)


def _my_chip():
    return 2 * lax.axis_index("x") + lax.axis_index("y")


def _cast_into_slot(w):
    nl, r, c = w.shape
    tr = _tile(r, max(16, BLOCK_ELEMS // c // 16 * 16), 16)

    def body(w_ref, o_ref):
        o_ref[...] = w_ref[...].astype(BF16)

    return pl.pallas_call(
        body, out_shape=SDS((nl, N_CHIPS, r, c), BF16), grid=(nl, r // tr),
        in_specs=[pl.BlockSpec((None, tr, c), lambda l, i: (l, i, 0))],
        out_specs=pl.BlockSpec((None, None, tr, c), lambda l, i: (l, _my_chip(), i, 0)),
        compiler_params=_cparams(("parallel", "parallel")), name="cast_into_slot")(w)


def _gather_copy(ref, layer, send_sems, recv_sems, sem, chip, half, to):
    r2 = ref.shape[2] // 2
    rows = ref.at[layer, chip, pl.ds(half * r2, r2)]
    return pltpu.make_async_remote_copy(src_ref=rows, dst_ref=rows, send_sem=send_sems.at[sem], recv_sem=recv_sems.at[sem],
                                        device_id=to, device_id_type=MESH)


def _gather_start(refs, layer, send_sems, recv_sems):
    x, y, c, chips = _place()
    for wi, ref in enumerate(refs):
        for n, chip in enumerate(chips):
            _gather_copy(ref, layer, send_sems, recv_sems, 6 * wi + n, 2 * x + y, c, (*chip, c)).start()


def _gather_finish(refs, layer, send_sems, recv_sems):
    x, y, c, chips = _place()
    me, sibling = (x, y, c), (x, y, 1 - c)
    for wi, ref in enumerate(refs):
        for n, chip in enumerate(chips):
            k = 2 * chip[0] + chip[1]
            _gather_copy(ref, layer, send_sems, recv_sems, 6 * wi + n, k, c, me).wait_recv()
            _gather_copy(ref, layer, send_sems, recv_sems, 6 * wi + 3 + n, k, c, sibling).start()
    for wi, ref in enumerate(refs):
        for n, chip in enumerate(chips):
            _gather_copy(ref, layer, send_sems, recv_sems, 6 * wi + 3 + n, 2 * chip[0] + chip[1], 1 - c, me).wait_recv()
    for wi, ref in enumerate(refs):
        for n, chip in enumerate(chips):
            _gather_copy(ref, layer, send_sems, recv_sems, 6 * wi + n, 2 * x + y, c, (*chip, c)).wait_send()
            _gather_copy(ref, layer, send_sems, recv_sems, 6 * wi + 3 + n, 2 * chip[0] + chip[1], c, sibling).wait_send()


def _all_gather_weights(bufs, layer):
    nw = len(bufs)

    def body(*refs):
        _gather_start(refs[nw:2 * nw], layer, refs[2 * nw], refs[2 * nw + 1])
        _gather_finish(refs[nw:2 * nw], layer, refs[2 * nw], refs[2 * nw + 1])

    return list(pl.pallas_call(
        body, out_shape=tuple(SDS(b.shape, b.dtype) for b in bufs), in_specs=[ANY] * nw, out_specs=(ANY,) * nw,
        input_output_aliases={i: i for i in range(nw)},
        scratch_shapes=[pltpu.SemaphoreType.DMA((6 * nw,)), pltpu.SemaphoreType.DMA((6 * nw,))],
        name="all_gather_weights")(*bufs))


def _pair_swap(grads):
    nw = len(grads)

    def body(*refs):
        ins, outs = refs[:nw], refs[nw:2 * nw]
        send_sems, recv_sems = refs[2 * nw], refs[2 * nw + 1]
        x, y, c, _ = _place()
        copies = []
        for wi, (g_ref, o_ref) in enumerate(zip(ins, outs)):
            r2 = g_ref.shape[1] // 2
            copies.append(pltpu.make_async_remote_copy(
                src_ref=g_ref.at[:, pl.ds((1 - c) * r2, r2)], dst_ref=o_ref, send_sem=send_sems.at[wi],
                recv_sem=recv_sems.at[wi], device_id=(x, y, 1 - c), device_id_type=MESH))
            copies[-1].start()
        for cp in copies:
            cp.wait()

    return pl.pallas_call(
        body, out_shape=tuple(SDS((N_CHIPS, g.shape[1] // 2, g.shape[2]), g.dtype) for g in grads),
        in_specs=[ANY] * nw, out_specs=(ANY,) * nw,
        scratch_shapes=[pltpu.SemaphoreType.DMA((nw,)), pltpu.SemaphoreType.DMA((nw,))], name="pair_swap")(*grads)


def _pair_add(g, other):
    _, r, c = g.shape
    r2, tr = _half_rows(r, c)
    nb = r2 // tr

    def body(g_ref, o_ref, pb_ref, land_ref):
        total = (g_ref[...].astype(F32) + o_ref[...].astype(F32)).astype(BF16)
        pb_ref[...] = total

        @pl.when(pl.program_id(1) == _my_chip())
        def _():
            land_ref[...] = total

    blk = pl.BlockSpec((None, tr, c), lambda j, k: (k, j, 0))
    return pl.pallas_call(
        body, out_shape=(SDS((N_CHIPS, r2, c), BF16),) * 2, grid=(nb, N_CHIPS),
        in_specs=[pl.BlockSpec((None, tr, c), lambda j, k: (k, lax.axis_index("c") * nb + j, 0)), blk],
        out_specs=(blk, pl.BlockSpec((None, tr, c), lambda j, k: (_my_chip(), j, 0))),
        compiler_params=_cparams(("parallel", "arbitrary")), name="pair_add")(g, other)


def _exchange_copy(p_ref, l_ref, send_sems, recv_sems, sem, src_slot, dst_slot, to):
    return pltpu.make_async_remote_copy(src_ref=p_ref.at[src_slot], dst_ref=l_ref.at[dst_slot], send_sem=send_sems.at[sem],
                                        recv_sem=recv_sems.at[sem], device_id=to, device_id_type=MESH)


def _exchange_start(p_refs, l_refs, send_sems, recv_sems):
    x, y, c, chips = _place()
    for wi, (p_ref, l_ref) in enumerate(zip(p_refs, l_refs)):
        for n, chip in enumerate(chips):
            _exchange_copy(p_ref, l_ref, send_sems, recv_sems, 3 * wi + n, 2 * chip[0] + chip[1], 2 * x + y, (*chip, c)).start()


def _exchange_finish(p_refs, l_refs, send_sems, recv_sems):
    x, y, c, chips = _place()
    for wi, (p_ref, l_ref) in enumerate(zip(p_refs, l_refs)):
        for n, chip in enumerate(chips):
            _exchange_copy(p_ref, l_ref, send_sems, recv_sems, 3 * wi + n, 2 * x + y, 2 * chip[0] + chip[1], (x, y, c)).wait_recv()
    for wi, (p_ref, l_ref) in enumerate(zip(p_refs, l_refs)):
        for n, chip in enumerate(chips):
            _exchange_copy(p_ref, l_ref, send_sems, recv_sems, 3 * wi + n, 2 * chip[0] + chip[1], 2 * x + y, (*chip, c)).wait_send()


def _chip_all_to_all(pbs, lands):
    nw = len(pbs)

    def body(*refs):
        _exchange_start(refs[:nw], refs[2 * nw:3 * nw], refs[3 * nw], refs[3 * nw + 1])
        _exchange_finish(refs[:nw], refs[2 * nw:3 * nw], refs[3 * nw], refs[3 * nw + 1])

    return list(pl.pallas_call(
        body, out_shape=tuple(SDS(l.shape, l.dtype) for l in lands), in_specs=[ANY] * (2 * nw), out_specs=(ANY,) * nw,
        input_output_aliases={nw + i: i for i in range(nw)},
        scratch_shapes=[pltpu.SemaphoreType.DMA((3 * nw,)), pltpu.SemaphoreType.DMA((3 * nw,))],
        name="chip_all_to_all")(*pbs, *lands))


def _sum_chips(land, gstack, layer):
    _, r, c = gstack.shape
    r2, tr = _half_rows(r, c)
    nb = r2 // tr

    def body(l_ref, g_ref, out_ref):
        out_ref[...] = ((l_ref[0].astype(F32) + l_ref[1].astype(F32)) + l_ref[2].astype(F32)) + l_ref[3].astype(F32)

    return pl.pallas_call(
        body, out_shape=SDS(gstack.shape, F32), grid=(nb,),
        in_specs=[pl.BlockSpec((N_CHIPS, tr, c), lambda j: (0, j, 0)), ANY],
        out_specs=pl.BlockSpec((None, tr, c), lambda j: (layer, lax.axis_index("c") * nb + j, 0)),
        input_output_aliases={1: 0}, compiler_params=_cparams(("parallel",)), name="sum_chips")(land, gstack)


def _pair_gather(gstacks, layer):
    nw = len(gstacks)

    def body(*refs):
        outs = refs[nw:2 * nw]
        send_sems, recv_sems = refs[2 * nw], refs[2 * nw + 1]
        x, y, c, _ = _place()

        def copy(ref, wi, half):
            r2 = ref.shape[1] // 2
            blk = ref.at[layer, pl.ds(half * r2, r2)]
            return pltpu.make_async_remote_copy(src_ref=blk, dst_ref=blk, send_sem=send_sems.at[wi],
                                                recv_sem=recv_sems.at[wi], device_id=(x, y, 1 - c), device_id_type=MESH)

        sent = [copy(ref, wi, c) for wi, ref in enumerate(outs)]
        for cp in sent:
            cp.start()
        for wi, ref in enumerate(outs):
            copy(ref, wi, 1 - c).wait_recv()
        for cp in sent:
            cp.wait_send()

    return pl.pallas_call(
        body, out_shape=tuple(SDS(g.shape, g.dtype) for g in gstacks), in_specs=[ANY] * nw, out_specs=(ANY,) * nw,
        input_output_aliases={i: i for i in range(nw)},
        scratch_shapes=[pltpu.SemaphoreType.DMA((nw,)), pltpu.SemaphoreType.DMA((nw,))], name="pair_gather")(*gstacks)


def _all_gather_small(vec, name):
    rows, w = vec.shape

    def body(v_ref, sum_ref, all_ref, send_sems, recv_sems):
        x, y, c, chips = _place()
        me, sibling = (x, y, c), (x, y, 1 - c)

        def slot(px, py, pc):
            return all_ref.at[4 * px + 2 * py + pc]

        def copy(k, block, to, src=None):
            return pltpu.make_async_remote_copy(
                src_ref=slot(*block) if src is None else src, dst_ref=slot(*block), send_sem=send_sems.at[k],
                recv_sem=recv_sems.at[k], device_id=to, device_id_type=MESH)

        first = [copy(0, me, sibling, src=v_ref)]
        first += [copy(1 + n, me, (*chip, c), src=v_ref) for n, chip in enumerate(chips)]
        for cp in first:
            cp.start()
        slot(*me)[...] = v_ref[...]
        passed = [copy(4 + n, (*chip, c), sibling) for n, chip in enumerate(chips)]
        for n, chip in enumerate(chips):
            copy(1 + n, (*chip, c), me).wait_recv()
            passed[n].start()
        copy(0, sibling, me).wait_recv()
        for n, chip in enumerate(chips):
            copy(4 + n, (*chip, 1 - c), me).wait_recv()
        for cp in first + passed:
            cp.wait_send()
        total = all_ref[0]
        for dev in range(1, 8):
            total = total + all_ref[dev]
        sum_ref[...] = total

    vm = pl.BlockSpec(memory_space=pltpu.VMEM)
    return pl.pallas_call(
        body, out_shape=(SDS((rows, w), F32), SDS((8, rows, w), F32)), in_specs=[vm], out_specs=(vm, vm),
        scratch_shapes=[pltpu.SemaphoreType.DMA((7,)), pltpu.SemaphoreType.DMA((7,))], name=name)(vec)


def _to_rows128(flat):
    n = flat.shape[0]
    rows = -(-n // (8 * LANES)) * 8
    return jnp.pad(flat, (0, rows * LANES - n)).reshape(rows, LANES)


def _in_layout(conv, ql, kvl, d):
    lay = {"conv": conv, "ql": ql, "kvl": kvl, "d": d}
    lay["q"] = 3 * conv
    lay["kr"] = lay["q"] + ql
    lay["gc"] = lay["kr"] + LANES
    lay["gm"] = lay["gc"] + d
    lay["kv"] = lay["gm"] + d
    used = lay["kv"] + kvl
    lay["width"] = -(-used // 512) * 512
    return lay


def _w_in_to_layout(w, lay):
    conv, ql, kvl, d = lay["conv"], lay["ql"], lay["kvl"], lay["d"]
    o_kv = 3 * conv + ql
    o_kr = o_kv + kvl
    o_g = o_kr + ROPE_DIM
    lead = w.shape[:-1]
    parts = [w[..., :o_kv], w[..., o_kr:o_g], jnp.zeros(lead + (LANES - ROPE_DIM,), w.dtype), w[..., o_g:o_g + 2 * d],
             w[..., o_kv:o_kr], jnp.zeros(lead + (lay["width"] - lay["kv"] - kvl,), w.dtype)]
    return jnp.concatenate(parts, axis=-1)


def _w_in_from_layout(g, lay):
    ql, kvl, d = lay["ql"], lay["kvl"], lay["d"]
    return jnp.concatenate([g[:, :lay["q"] + ql], g[:, lay["kv"]:lay["kv"] + kvl], g[:, lay["kr"]:lay["kr"] + ROPE_DIM],
                            g[:, lay["gc"]:lay["gc"] + 2 * d]], axis=1)


def _w_uq_to_layout(w):
    r = w.shape[0]
    w3 = w.reshape(r, N_HEADS, NOPE_DIM + ROPE_DIM)
    return jnp.pad(w3, ((0, 0), (0, 0), (0, QK_PAD - NOPE_DIM - ROPE_DIM))).reshape(r, N_HEADS * QK_PAD)


def _w_uq_from_layout(g):
    r = g.shape[0]
    return g.reshape(r, N_HEADS, QK_PAD)[:, :, :NOPE_DIM + ROPE_DIM].reshape(r, N_HEADS * (NOPE_DIM + ROPE_DIM))


def _w_ukv_to_layout(w):
    r = w.shape[0]
    return w.reshape(r, N_HEADS, 2, NOPE_DIM).transpose(0, 2, 1, 3).reshape(r, 2 * N_HEADS * NOPE_DIM)


def _w_ukv_from_layout(g):
    r = g.shape[0]
    return g.reshape(r, 2, N_HEADS, NOPE_DIM).transpose(0, 2, 1, 3).reshape(r, 2 * N_HEADS * NOPE_DIM)


def _chips_to_cols(buf):
    _, r, c = buf.shape
    return buf.transpose(1, 0, 2).reshape(r, N_CHIPS * c)


def _cols_to_chips(g):
    r, c4 = g.shape
    return g.reshape(r, N_CHIPS, c4 // N_CHIPS).transpose(1, 0, 2)


GATE_UP = ("ffn1_w_gu", "ffn2_w_gu")
SMALL = ("w_in", "w_conv_out", "w_uq", "w_ukv", "w_ple_proj")


def _big_views(bufs):
    views = {}
    for (name, axis), buf in zip(SHARDED, bufs):
        if axis == 0:
            views[name] = buf.reshape(buf.shape[0], N_CHIPS * buf.shape[2], buf.shape[3])
        elif name in GATE_UP:
            views[name] = buf
    return views


def _small_weights(bufs, layer, lay):
    by_name = dict(zip(BIG, bufs))
    cols = {name: _chips_to_cols(by_name[name][layer]) for name in SMALL}
    cols["w_in"] = _w_in_to_layout(cols["w_in"], lay)
    cols["w_uq"] = _w_uq_to_layout(cols["w_uq"])
    cols["w_ukv"] = _w_ukv_to_layout(cols["w_ukv"])
    return cols


def _ffn_fwd(h, n, w_gu, w_down, layer, tag, next_gain):
    gu, a = _gu_swiglu_fwd(n, w_gu, layer, tag + "_gu_fwd")
    out, n_out = _mm(a, w_down, "nn", F32, tag + "_down_fwd", scale=0.5, res=h, layer=layer, norm_gain=next_gain)
    return out, n_out, (h, n, gu, a)


def _ffn_bwd(dh, dhb, saved, gain, w_gu, w_down, layer, tag):
    h, n, gu, a = saved
    d_wdown = _mm(a, dhb, "tn", BF16, tag + "_down_dw", scale=0.5)
    dgu = _down_dx_swiglu_bwd(dhb, w_down, gu, layer, tag + "_down_dx")
    d_wgu = _mm(n, dgu, "tn", BF16, tag + "_gu_dw", out_chip=True, b_halves=True)
    dh, dhb, dgain = _mm(dgu, w_gu, "nt", BF16, tag + "_gu_dx", layer=layer, b_chip=True, a_halves=True,
                         norm_bwd=(h, gain, dh))
    return dh, dhb, d_wgu, d_wdown, dgain


def _layer_fwd(h0, n0, p_i, bufs, small, conv_w, norms, layer, lay, tables, gather_next, next_gain):
    big = _big_views(bufs)
    h1, n2, s_ffn1 = _ffn_fwd(h0, n0, big["ffn1_w_gu"], big["ffn1_w_down"], layer, "ffn1", norms["mix_norm"])
    proj = _mm(n2, small["w_in"], "nn", BF16, "in_fwd")
    cb = _conv_fwd(proj, conv_w)
    ya = _mm(cb, small["w_conv_out"], "nn", BF16, "conv_out_fwd")
    qn, kvn = _qkvnorm_fwd(proj, lay, norms["q_norm"], norms["kv_norm"])
    qf = _mm(qn, small["w_uq"], "nn", BF16, "uq_fwd")
    kv = _mm(kvn, small["w_ukv"], "nn", BF16, "ukv_fwd")
    qr, kf = _rope_fwd(qf, kv, proj, lay, tables)
    o, lse, new_bufs = _attn_fwd(qr, kf, kv, gather=(bufs, layer + 1) if gather_next else None)
    if gather_next:
        bufs = new_bufs
        big = _big_views(bufs)
    yb = _mm(o, big["w_mla_out"], "nn", BF16, "mla_out_fwd", layer=layer)
    mg = _merge_fwd(proj, lay, ya, yb)
    h2, n3 = _mm(mg, big["w_o"], "nn", F32, "o_fwd", res=h1, layer=layer, norm_gain=norms["ffn2_norm"])
    h3, n4, s_ffn2 = _ffn_fwd(h2, n3, big["ffn2_w_gu"], big["ffn2_w_down"], layer, "ffn2", norms["ple_norm"])
    gp = _mm(n4, big["w_ple_gate"], "nn", BF16, "ple_gate_fwd", layer=layer)
    pp = _mm(p_i, small["w_ple_proj"], "nn", BF16, "ple_proj_fwd")
    h4, n_out = _ple_fwd(h3, gp, pp, next_gain)
    saved = dict(s_ffn1=s_ffn1, h1=h1, n2=n2, proj=proj, cb=cb, ya=ya, qn=qn, kvn=kvn, qr=qr, kf=kf, kv=kv, o=o,
                 lse=lse, yb=yb, mg=mg, h2=h2, s_ffn2=s_ffn2, h3=h3, n4=n4, gp=gp, pp=pp, p=p_i)
    return h4, n_out, saved, bufs


def _layer_bwd(dh, s, big, small, conv_w, norms, layer, lay, tables, exchange):
    gw, gn = {}, {}

    def by_rows(g):
        return g.reshape(N_CHIPS, g.shape[0] // N_CHIPS, g.shape[1])

    dpp, dgp = _ple_bwd(dh, s["gp"], s["pp"])
    gw["w_ple_proj"] = _cols_to_chips(_mm(s["p"], dpp, "tn", BF16, "ple_proj_dw"))
    gw["w_ple_gate"] = by_rows(_mm(s["n4"], dgp, "tn", BF16, "ple_gate_dw"))
    dh, dhb, gn["ple_norm"] = _mm(dgp, big["w_ple_gate"], "nt", BF16, "ple_gate_dx", layer=layer,
                                  norm_bwd=(s["h3"], norms["ple_norm"], dh))
    dh, dhb, gw["ffn2_w_gu"], g_down, gn["ffn2_norm"] = _ffn_bwd(
        dh, dhb, s["s_ffn2"], norms["ffn2_norm"], big["ffn2_w_gu"], big["ffn2_w_down"], layer, "ffn2")
    gw["ffn2_w_down"] = by_rows(g_down)
    gw["w_o"] = by_rows(_mm(s["mg"], dhb, "tn", BF16, "o_dw"))
    dmg = _mm(dhb, big["w_o"], "nt", BF16, "o_dx", layer=layer)
    dya, dyb, dgc, dgm = _merge_bwd(s["proj"], lay, s["ya"], s["yb"], dmg)
    gw["w_conv_out"] = _cols_to_chips(_mm(s["cb"], dya, "tn", BF16, "conv_out_dw"))
    dcb = _mm(dya, small["w_conv_out"], "nt", BF16, "conv_out_dx")
    db, dc, dv_conv, g_conv = _conv_bwd(s["proj"], conv_w, dcb)
    gw["w_mla_out"] = by_rows(_mm(s["o"], dyb, "tn", BF16, "mla_out_dw"))
    do = _mm(dyb, big["w_mla_out"], "nt", BF16, "mla_out_dx", layer=layer)
    delta = _attn_delta(do, s["o"])
    dkf, dv, dqr, landed = _attn_bwd(s["qr"], s["kf"], s["kv"], do, s["lse"], delta, exchange=exchange)
    dqf, dkv, dkr = _rope_bwd(dqr, dkf, dv, tables)
    gw["w_uq"] = _cols_to_chips(_w_uq_from_layout(_mm(s["qn"], dqf, "tn", BF16, "uq_dw")))
    dqn = _mm(dqf, small["w_uq"], "nt", BF16, "uq_dx")
    gw["w_ukv"] = _cols_to_chips(_w_ukv_from_layout(_mm(s["kvn"], dkv, "tn", BF16, "ukv_dw")))
    dkvn = _mm(dkv, small["w_ukv"], "nt", BF16, "ukv_dx")
    dqc, dkvc, gn["q_norm"], gn["kv_norm"] = _qkvnorm_bwd(s["proj"], lay, norms["q_norm"], norms["kv_norm"], dqn, dkvn)
    t = dh.shape[0]
    dproj = jnp.concatenate([db, dc, dv_conv, dqc, dkr, dgc, dgm, dkvc,
                             jnp.zeros((t, lay["width"] - lay["kv"] - lay["kvl"]), BF16)], axis=1)
    gw["w_in"] = _cols_to_chips(_w_in_from_layout(_mm(s["n2"], dproj, "tn", BF16, "in_dw"), lay))
    dh, dhb, gn["mix_norm"] = _mm(dproj, small["w_in"], "nt", BF16, "in_dx", norm_bwd=(s["h1"], norms["mix_norm"], dh))
    dh, dhb, gw["ffn1_w_gu"], g_down, gn["ffn1_norm"] = _ffn_bwd(
        dh, dhb, s["s_ffn1"], norms["ffn1_norm"], big["ffn1_w_gu"], big["ffn1_w_down"], layer, "ffn1")
    gw["ffn1_w_down"] = by_rows(g_down)
    return dh, gw, g_conv, gn, landed


def _rope_tables(positions):
    half = ROPE_DIM // 2
    inv_freq = ROPE_THETA ** (-jnp.arange(0, ROPE_DIM, 2, dtype=F32) / ROPE_DIM)
    ang = positions.astype(F32)[:, None] * inv_freq
    cos, sin = jnp.cos(ang), jnp.sin(ang)
    zeros = jnp.zeros_like(cos)
    cos_t = jnp.concatenate([cos, cos, zeros, zeros], axis=1)
    sin_a = jnp.concatenate([-sin, zeros, zeros, zeros], axis=1)
    sin_b = jnp.concatenate([zeros, sin, zeros, zeros], axis=1)
    assert cos_t.shape[1] == LANES and half * 4 == LANES
    return cos_t, sin_a, sin_b


def kernel(x, p, positions, ffn1_norm, ffn1_w_gu, ffn1_w_down, mix_norm, w_in, conv_w, w_conv_out, q_norm, kv_norm, w_uq, w_ukv, w_mla_out, w_o, ffn2_norm, ffn2_w_gu, ffn2_w_down, ple_norm, w_ple_gate, w_ple_proj, final_norm, loss_target, m_ffn1_norm, m_ffn1_w_gu, m_ffn1_w_down, m_mix_norm, m_w_in, m_conv_w, m_w_conv_out, m_q_norm, m_kv_norm, m_w_uq, m_w_ukv, m_w_mla_out, m_w_o, m_ffn2_norm, m_ffn2_w_gu, m_ffn2_w_down, m_ple_norm, m_w_ple_gate, m_w_ple_proj, m_final_norm, v_ffn1_norm, v_ffn1_w_gu, v_ffn1_w_down, v_mix_norm, v_w_in, v_conv_w, v_w_conv_out, v_q_norm, v_kv_norm, v_w_uq, v_w_ukv, v_w_mla_out, v_w_o, v_ffn2_norm, v_ffn2_w_gu, v_ffn2_w_down, v_ple_norm, v_w_ple_gate, v_w_ple_proj, v_final_norm):
    args = dict(zip(ARG_NAMES, (x, p, positions, ffn1_norm, ffn1_w_gu, ffn1_w_down, mix_norm, w_in, conv_w, w_conv_out, q_norm, kv_norm, w_uq, w_ukv, w_mla_out, w_o, ffn2_norm, ffn2_w_gu, ffn2_w_down, ple_norm, w_ple_gate, w_ple_proj, final_norm, loss_target, m_ffn1_norm, m_ffn1_w_gu, m_ffn1_w_down, m_mix_norm, m_w_in, m_conv_w, m_w_conv_out, m_q_norm, m_kv_norm, m_w_uq, m_w_ukv, m_w_mla_out, m_w_o, m_ffn2_norm, m_ffn2_w_gu, m_ffn2_w_down, m_ple_norm, m_w_ple_gate, m_w_ple_proj, m_final_norm, v_ffn1_norm, v_ffn1_w_gu, v_ffn1_w_down, v_mix_norm, v_w_in, v_conv_w, v_w_conv_out, v_q_norm, v_kv_norm, v_w_uq, v_w_ukv, v_w_mla_out, v_w_o, v_ffn2_norm, v_ffn2_w_gu, v_ffn2_w_down, v_ple_norm, v_w_ple_gate, v_w_ple_proj, v_final_norm)))
    depth = ffn1_norm.shape[0]
    t, d = x.shape[1], x.shape[2]
    conv = conv_w.shape[-1] * N_CHIPS
    lay = _in_layout(conv, q_norm.shape[-1], kv_norm.shape[-1], d)
    chip = 2 * lax.axis_index("x") + lax.axis_index("y")
    tables = _rope_tables(positions[0])

    bufs = _all_gather_weights([_cast_into_slot(args[name]) for name in BIG], 0)
    small = [None] * depth
    small[0] = _small_weights(bufs, 0, lay)
    conv_rows = depth * conv_w.shape[1]
    conv_all = _all_gather_small(_to_rows128(conv_w.reshape(-1)), "all_gather_conv_w")[1]
    conv_full = conv_all[0::2, :conv_rows].reshape(N_CHIPS, depth, conv_w.shape[1], LANES)
    conv_full = conv_full.transpose(1, 2, 0, 3).reshape(depth, conv_w.shape[1], conv)
    norms = [{name: args[name][i] for name in REPLICATED} for i in range(depth)]
    p3 = p.reshape(depth, t, p.shape[-1])

    h = x[0]
    n = _rmsnorm_fwd(h, norms[0]["ffn1_norm"], "first_norm_fwd")
    saved = []
    for i in range(depth):
        h, n, s, bufs = _layer_fwd(h, n, p3[i], bufs, small[i], conv_full[i], norms[i], i, lay, tables, i + 1 < depth,
                                   norms[i + 1]["ffn1_norm"] if i + 1 < depth else None)
        if i + 1 < depth:
            small[i + 1] = _small_weights(bufs, i + 1, lay)
        saved.append(s)
    loss_part, dh, _, g_final = _loss_head(h, final_norm, loss_target[0])
    loss = lax.psum(loss_part[0, 0], ("x", "y", "c"))

    big = _big_views(bufs)
    gstacks = [lax.empty(args[name].shape, F32) for name in BIG]
    norm_grads, conv_grads = [None] * depth, [None] * depth

    def finish(layer, lands, stacks):
        stacks = [_sum_chips(land, gs, layer) for land, gs in zip(lands, stacks)]
        return _pair_gather(stacks, layer)

    pending = None
    for i in reversed(range(depth)):
        dh, gw, conv_grads[i], norm_grads[i], landed = _layer_bwd(
            dh, saved[i], big, small[i], conv_full[i], norms[i], i, lay, tables,
            exchange=None if pending is None else pending[1:])
        if pending is not None:
            gstacks = finish(pending[0], landed, gstacks)
        grads_i = [gw[name] for name in BIG]
        pairs = [_pair_add(g, o) for g, o in zip(grads_i, _pair_swap(grads_i))]
        pending = (i, [pb for pb, _ in pairs], [land for _, land in pairs])
    gstacks = finish(pending[0], _chip_all_to_all(pending[1], pending[2]), gstacks)
    grad_x = dh[None]
    grads = dict(zip(BIG, gstacks))

    pieces = [norm_grads[i][name].reshape(-1) for i in range(depth) for name in REPLICATED]
    pieces += [g_final.reshape(-1)] + [conv_grads[i].reshape(-1) for i in range(depth)]
    vec = _all_gather_small(_to_rows128(jnp.concatenate(pieces)), "all_sum_small")[0].reshape(-1)
    off = 0
    per_name = {name: [] for name in REPLICATED}
    for i in range(depth):
        for name in REPLICATED:
            size = args[name].shape[1]
            per_name[name].append(vec[off:off + size])
            off += size
    for name in REPLICATED:
        grads[name] = jnp.stack(per_name[name])
    grads["final_norm"] = vec[off:off + d]
    off += d
    conv_g = vec[off:off + depth * 3 * conv].reshape(depth, 3, conv)
    grads["conv_w"] = lax.dynamic_slice_in_dim(conv_g, chip * conv_w.shape[-1], conv_w.shape[-1], axis=2)

    delta, new_m, new_v = {}, {}, {}
    for name in WEIGHTS:
        w_, g_, m_, v_ = args[name], grads[name], args["m_" + name], args["v_" + name]
        if w_.ndim == 1:
            outs = _adamw(w_[None], g_[None], m_[None], v_[None])
            delta[name], new_m[name], new_v[name] = (o[0] for o in outs)
        else:
            delta[name], new_m[name], new_v[name] = _adamw(w_, g_, m_, v_)
    return (loss, grad_x, *[grads[n] for n in WEIGHTS], *[delta[n] for n in WEIGHTS],
            *[new_m[n] for n in WEIGHTS], *[new_v[n] for n in WEIGHTS])
```

```python
import functools

import jax
import jax.numpy as jnp
from jax import lax
from jax.experimental import pallas as pl
from jax.experimental.pallas import tpu as pltpu

BF16 = jnp.bfloat16
F32 = jnp.float32
SDS = jax.ShapeDtypeStruct
MESH = pl.DeviceIdType.MESH

N_HEADS = 8
NOPE_DIM = 128
ROPE_DIM = 64
V_DIM = 128
QK_PAD = 256
CHUNK = 64
ROPE_THETA = 10000.0
EPS = 1e-6
ATTN_SCALE = (NOPE_DIM + ROPE_DIM) ** -0.5
NEG_BIG = -1e30

ADAM_LR = 0.001
ADAM_B1 = 0.9
ADAM_B2 = 0.999
ADAM_EPS = 1e-08
ADAM_WD = 0.01
ADAM_STEP = 10

LANES = 128
N_CHIPS = 4
VMEM_LIMIT_BYTES = 56 * 1024 * 1024
ACC_BYTES = 6 * 1024 * 1024
BLOCK_ELEMS = 1 << 19

SHARDED = (("ffn1_w_gu", 1), ("ffn1_w_down", 0), ("w_in", 1), ("w_conv_out", 1), ("w_uq", 1), ("w_ukv", 1),
           ("w_mla_out", 0), ("w_o", 0), ("ffn2_w_gu", 1), ("ffn2_w_down", 0), ("w_ple_gate", 0), ("w_ple_proj", 1))
BIG = tuple(name for name, _ in SHARDED)
REPLICATED = ("ffn1_norm", "mix_norm", "q_norm", "kv_norm", "ffn2_norm", "ple_norm")
WEIGHTS = ("ffn1_norm", "ffn1_w_gu", "ffn1_w_down", "mix_norm", "w_in", "conv_w", "w_conv_out", "q_norm",
           "kv_norm", "w_uq", "w_ukv", "w_mla_out", "w_o", "ffn2_norm", "ffn2_w_gu", "ffn2_w_down",
           "ple_norm", "w_ple_gate", "w_ple_proj", "final_norm")
ARG_NAMES = ("x", "p", "positions") + WEIGHTS + ("loss_target",) + tuple("m_" + n for n in WEIGHTS) + tuple(
    "v_" + n for n in WEIGHTS)


def _cparams(semantics=None):
    return pltpu.CompilerParams(dimension_semantics=semantics, vmem_limit_bytes=VMEM_LIMIT_BYTES)


def _tile(n, cap, mult=LANES):
    best = None
    for t in range(mult, min(n, cap) + 1, mult):
        if n % t == 0:
            best = t
    return n if best is None else best


def _sigmoid(x):
    return 1.0 / (1.0 + jnp.exp(-x))


def _rowspec(tm, width, col_block=0):
    return pl.BlockSpec((tm, width), lambda i: (i, col_block))


def _colspec(tm, width, offset):
    assert offset % width == 0, (width, offset)
    return _rowspec(tm, width, offset // width)


def _mm(a, b, mode, out_dtype, name, scale=None, res=None, layer=None, b_chip=False, out_chip=False, norm_gain=None,
        a_halves=False, b_halves=False, norm_bwd=None, rider=None):
    bshape = b.shape if layer is None else b.shape[1:]
    if b_chip:
        bshape = (bshape[1], N_CHIPS * bshape[2])
    if b_halves:
        bshape = (bshape[1], 2 * bshape[2])
    ashape = (a.shape[1], 2 * a.shape[2]) if a_halves else a.shape
    if mode == "nn":
        (m, k), (k2, n) = ashape, bshape
    elif mode == "nt":
        (m, k), (n, k2) = ashape, bshape
    else:
        (k, m), (k2, n) = ashape, bshape
    assert k == k2, (a.shape, b.shape, mode)
    n_unit = n // N_CHIPS if (out_chip or (b_chip and mode == "nn")) else n
    k_unit = k // N_CHIPS if (b_chip and mode == "nt") else k
    tn = _tile(n_unit, 1536)
    tm = _tile(m, min(512 if norm_bwd is not None else 1408, ACC_BYTES // (4 * tn)))
    tk = _tile(k_unit, 1536)
    nk = k // tk
    n_per, k_per = n_unit // tn, k_unit // tk
    n_half, k_half = n // 2 // tn, k // 2 // tk
    dims = {"nn": (((1,), (0,)), ((), ())), "nt": (((1,), (1,)), ((), ())), "tn": (((0,), (0,)), ((), ()))}[mode]

    fuse_norm = norm_gain is not None
    fuse_bwd = norm_bwd is not None
    assert not (fuse_norm or fuse_bwd) or (tn == n and not out_chip), (name, tn, n)
    assert not a_halves or mode == "nt"
    assert not b_halves or mode == "tn"
    n_extra_in = (1 if res is not None else 0) + (1 if fuse_norm else 0) + (3 if fuse_bwd else 0)
    n_outs = 3 if fuse_bwd else (2 if fuse_norm else 1)
    grid = (m // tm, n // tn, nk)

    def body(*refs):
        a_ref, b_ref = refs[0], refs[1]
        extra = list(refs[2:2 + n_extra_in])
        rest = refs[2 + n_extra_in:]
        if rider is not None:
            rest = rider.split(rest, n_outs, 1 if nk > 1 else 0, grid)
        outs = list(rest)
        acc_ref = outs.pop() if nk > 1 else None
        res_ref = extra.pop(0) if res is not None else None
        gain_ref = extra.pop(0) if fuse_norm else None

        def finish(acc):
            if scale is not None:
                acc = acc * scale
            if res_ref is not None:
                acc = res_ref[...] + acc
            if fuse_bwd:
                h_ref, g_ref, dhin_ref = extra
                dh_ref, dhb_ref, dg_ref = outs
                dx, dg = _rn_bwd_math(h_ref[...], g_ref[...], acc)
                dh = dhin_ref[...] + dx
                dh_ref[...] = dh
                dhb_ref[...] = dh.astype(BF16)

                @pl.when(pl.program_id(0) == 0)
                def _():
                    dg_ref[...] = dg

                @pl.when(pl.program_id(0) > 0)
                def _():
                    dg_ref[...] += dg
                return
            outs[0][...] = acc.astype(out_dtype)
            if fuse_norm:
                outs[1][...] = _rn_fwd_math(acc, gain_ref[...]).astype(BF16)

        part = lax.dot_general(a_ref[...].astype(BF16), b_ref[...].astype(BF16), dims,
                               preferred_element_type=F32)
        if nk == 1:
            finish(part)
        else:
            kk = pl.program_id(2)

            @pl.when(kk == 0)
            def _():
                acc_ref[...] = part

            @pl.when(kk > 0)
            def _():
                acc_ref[...] += part

            @pl.when(kk == nk - 1)
            def _():
                finish(acc_ref[...])
        if rider is not None:
            rider.finish_at_last_step(grid)

    lead = () if layer is None else (layer,)
    lead_block = () if layer is None else (None,)
    if mode == "nn":
        a_spec = pl.BlockSpec((tm, tk), lambda i, j, kk: (i, kk))
        if b_chip:
            b_spec = pl.BlockSpec(lead_block + (None, tk, tn), lambda i, j, kk: lead + (j // n_per, kk, j % n_per))
        else:
            b_spec = pl.BlockSpec(lead_block + (tk, tn), lambda i, j, kk: lead + (kk, j))
    elif mode == "nt":
        if a_halves:
            a_spec = pl.BlockSpec((None, tm, tk), lambda i, j, kk: (kk // k_half, i, kk % k_half))
        else:
            a_spec = pl.BlockSpec((tm, tk), lambda i, j, kk: (i, kk))
        if b_chip:
            b_spec = pl.BlockSpec(lead_block + (None, tn, tk), lambda i, j, kk: lead + (kk // k_per, j, kk % k_per))
        else:
            b_spec = pl.BlockSpec(lead_block + (tn, tk), lambda i, j, kk: lead + (j, kk))
    else:
        assert layer is None and not b_chip
        a_spec = pl.BlockSpec((tk, tm), lambda i, j, kk: (kk, i))
        if b_halves:
            b_spec = pl.BlockSpec((None, tk, tn), lambda i, j, kk: (j // n_half, kk, j % n_half))
        else:
            b_spec = pl.BlockSpec((tk, tn), lambda i, j, kk: (kk, j))
    if out_chip:
        o_spec = pl.BlockSpec((None, tm, tn), lambda i, j, kk: (j // n_per, i, j % n_per))
        out_shape = SDS((N_CHIPS, m, n_unit), out_dtype)
    else:
        o_spec = pl.BlockSpec((tm, tn), lambda i, j, kk: (i, j))
        out_shape = SDS((m, n), out_dtype)
    in_specs = [a_spec, b_spec] + ([o_spec] if res is not None else [])
    operands = (a, b) + ((res,) if res is not None else ())
    out_specs = o_spec
    vec = pl.BlockSpec((1, tn), lambda i, j, kk: (0, j))
    if fuse_norm:
        in_specs.append(vec)
        operands += (norm_gain.reshape(1, n),)
        out_shape, out_specs = (out_shape, SDS((m, n), BF16)), (o_spec, o_spec)
    if fuse_bwd:
        h, gain, dh_in = norm_bwd
        in_specs += [o_spec, vec, o_spec]
        operands += (h, gain.reshape(1, n), dh_in)
        out_shape = (SDS((m, n), F32), SDS((m, n), BF16), SDS((1, n), F32))
        out_specs = (o_spec, o_spec, vec)
    scratch = [pltpu.VMEM((tm, tn), F32)] if nk > 1 else []
    semantics = ("arbitrary",) * 3 if fuse_bwd else ("parallel", "parallel", "arbitrary")
    if rider is None:
        return pl.pallas_call(
            body, out_shape=out_shape, grid=grid, in_specs=in_specs, out_specs=out_specs, scratch_shapes=scratch,
            compiler_params=_cparams(semantics), name=name)(*operands)
    out_shape = out_shape if isinstance(out_shape, tuple) else (out_shape,)
    out_specs = out_specs if isinstance(out_specs, tuple) else (out_specs,)
    outs = pl.pallas_call(
        body, out_shape=out_shape + tuple(rider.out_shapes), grid=grid, in_specs=in_specs + [ANY] * len(rider.operands),
        out_specs=out_specs + (ANY,) * len(rider.out_shapes), scratch_shapes=scratch + rider.sem_shapes,
        input_output_aliases={len(operands) + i: len(out_shape) + o for i, o in rider.aliases.items()},
        compiler_params=_cparams(("arbitrary",) * 3), name=name + rider.tag)(*operands, *rider.operands)
    return tuple(outs[:n_outs]) + (list(outs[n_outs:]),)


def _rn_fwd_math(x, g):
    r = lax.rsqrt(jnp.mean(x * x, axis=-1, keepdims=True) + EPS)
    return (x * r) * g


def _rn_bwd_math(x, g, dn):
    r = lax.rsqrt(jnp.mean(x * x, axis=-1, keepdims=True) + EPS)
    xh = x * r
    gy = dn * g
    dx = r * (gy - xh * jnp.mean(gy * xh, axis=-1, keepdims=True))
    dg = jnp.sum(dn * xh, axis=0, keepdims=True)
    return dx, dg


def _rmsnorm_fwd(h, gain, name):
    t, d = h.shape
    tm = _tile(t, 512, 8)

    def body(h_ref, g_ref, o_ref):
        o_ref[...] = _rn_fwd_math(h_ref[...], g_ref[...]).astype(BF16)

    return pl.pallas_call(
        body, out_shape=SDS((t, d), BF16), grid=(t // tm,),
        in_specs=[_rowspec(tm, d), pl.BlockSpec((1, d), lambda i: (0, 0))], out_specs=_rowspec(tm, d),
        compiler_params=_cparams(("parallel",)), name=name)(h, gain.reshape(1, d))


def _loss_head(h, gain, target):
    t, d = h.shape
    tm = _tile(t, 512, 8)

    def body(h_ref, g_ref, t_ref, loss_ref, dh_ref, dhb_ref, dg_ref):
        x, g = h_ref[...], g_ref[...]
        err = _rn_fwd_math(x, g) - t_ref[...]
        part = 0.5 * jnp.sum(jnp.sum(err * err, axis=1, keepdims=True), axis=0, keepdims=True) * (1.0 / d)
        dx, dg = _rn_bwd_math(x, g, err * (1.0 / d))
        dh_ref[...] = dx
        dhb_ref[...] = dx.astype(BF16)

        @pl.when(pl.program_id(0) == 0)
        def _():
            dg_ref[...] = dg
            loss_ref[...] = jnp.broadcast_to(part, (1, LANES))

        @pl.when(pl.program_id(0) > 0)
        def _():
            dg_ref[...] += dg
            loss_ref[...] += jnp.broadcast_to(part, (1, LANES))

    vec = pl.BlockSpec((1, d), lambda i: (0, 0))
    return pl.pallas_call(
        body, out_shape=(SDS((1, LANES), F32), SDS((t, d), F32), SDS((t, d), BF16), SDS((1, d), F32)),
        grid=(t // tm,), in_specs=[_rowspec(tm, d), vec, _rowspec(tm, d)],
        out_specs=(pl.BlockSpec((1, LANES), lambda i: (0, 0)), _rowspec(tm, d), _rowspec(tm, d), vec),
        compiler_params=_cparams(("arbitrary",)), name="loss_head")(h, gain.reshape(1, d), target)


def _gu_swiglu_fwd(n, w_gu, layer, name, rider=None):
    t, d = n.shape
    cols = w_gu.shape[3]
    f = 2 * cols
    tn = _tile(cols, 1536)
    tm = _tile(t, 512)
    per = cols // tn

    grid = (f // tn, t // tm)

    def body(n_ref, wg_ref, wu_ref, *rest):
        gu_ref, a_ref = rest[:2] if rider is None else rider.split(rest, 2, 0, grid)
        x = n_ref[...]
        g = jnp.dot(x, wg_ref[...], preferred_element_type=F32)
        u = jnp.dot(x, wu_ref[...], preferred_element_type=F32)
        gu_ref[0] = g.astype(BF16)
        gu_ref[1] = u.astype(BF16)
        a_ref[...] = (g * _sigmoid(g) * u).astype(BF16)
        if rider is not None:
            rider.finish_at_last_step(grid)

    in_specs = [pl.BlockSpec((tm, d), lambda j, i: (i, 0)),
                pl.BlockSpec((None, None, d, tn), lambda j, i: (layer, j // per, 0, j % per)),
                pl.BlockSpec((None, None, d, tn), lambda j, i: (layer, 2 + j // per, 0, j % per))]
    out_shape = (SDS((2, t, f), BF16), SDS((t, f), BF16))
    out_specs = (pl.BlockSpec((2, tm, tn), lambda j, i: (0, i, j)), pl.BlockSpec((tm, tn), lambda j, i: (i, j)))
    if rider is None:
        return pl.pallas_call(body, out_shape=out_shape, grid=grid, in_specs=in_specs, out_specs=out_specs,
                              compiler_params=_cparams(("parallel", "parallel")), name=name)(n, w_gu, w_gu)
    outs = pl.pallas_call(
        body, out_shape=out_shape + tuple(rider.out_shapes), grid=grid, in_specs=in_specs + [ANY] * len(rider.operands),
        out_specs=out_specs + (ANY,) * len(rider.out_shapes), scratch_shapes=rider.sem_shapes,
        input_output_aliases={3 + i: 2 + o for i, o in rider.aliases.items()},
        compiler_params=_cparams(("arbitrary", "arbitrary")), name=name + rider.tag)(n, w_gu, w_gu, *rider.operands)
    return outs[0], outs[1], list(outs[2:])


def _down_dx_swiglu_bwd(dhb, w_down, gu, layer, name):
    t, d = dhb.shape
    f = gu.shape[2]
    tn = _tile(f, 1536)
    tm = _tile(t, 512)

    def body(dh_ref, w_ref, gu_ref, dgu_ref):
        da = 0.5 * lax.dot_general(dh_ref[...], w_ref[...], _NT, preferred_element_type=F32)
        g = gu_ref[0].astype(F32)
        u = gu_ref[1].astype(F32)
        sg = _sigmoid(g)
        dgu_ref[0] = (da * u * (sg * (1.0 + g * (1.0 - sg)))).astype(BF16)
        dgu_ref[1] = (da * (g * sg)).astype(BF16)

    blk = pl.BlockSpec((2, tm, tn), lambda j, i: (0, i, j))
    return pl.pallas_call(
        body, out_shape=SDS((2, t, f), BF16), grid=(f // tn, t // tm),
        in_specs=[pl.BlockSpec((tm, d), lambda j, i: (i, 0)), pl.BlockSpec((None, tn, d), lambda j, i: (layer, j, 0)), blk],
        out_specs=blk, compiler_params=_cparams(("parallel", "parallel")), name=name)(dhb, w_down, gu)


def _shift_down(z, k, row):
    return jnp.where(row >= k, pltpu.roll(z, k, 0), 0.0)


def _shift_up(z, k, row, t):
    return jnp.where(row < t - k, pltpu.roll(z, t - k, 0), 0.0)


def _conv_specs(t, conv):
    nb = conv // LANES
    return [pl.BlockSpec((t, LANES), lambda j: (0, j)), pl.BlockSpec((t, LANES), lambda j: (0, nb + j)),
            pl.BlockSpec((t, LANES), lambda j: (0, 2 * nb + j))]


def _conv_fwd(proj, conv_w):
    t = proj.shape[0]
    conv = conv_w.shape[1]

    def body(b_ref, c_ref, v_ref, w_ref, o_ref):
        z = c_ref[...].astype(F32) * v_ref[...].astype(F32)
        row = lax.broadcasted_iota(jnp.int32, z.shape, 0)
        y = w_ref[0:1, :] * _shift_down(z, 2, row) + w_ref[1:2, :] * _shift_down(z, 1, row) + w_ref[2:3, :] * z
        o_ref[...] = (b_ref[...].astype(F32) * y).astype(BF16)

    cspec = pl.BlockSpec((t, LANES), lambda j: (0, j))
    return pl.pallas_call(
        body, out_shape=SDS((t, conv), BF16), grid=(conv // LANES,),
        in_specs=_conv_specs(t, conv) + [pl.BlockSpec((3, LANES), lambda j: (0, j))], out_specs=cspec,
        compiler_params=_cparams(("parallel",)), name="conv_fwd")(proj, proj, proj, conv_w)


def _conv_bwd(proj, conv_w, dcb):
    t = proj.shape[0]
    conv = conv_w.shape[1]

    def body(b_ref, c_ref, v_ref, w_ref, d_ref, db_ref, dc_ref, dv_ref, dw_ref):
        b, c, v = b_ref[...].astype(F32), c_ref[...].astype(F32), v_ref[...].astype(F32)
        d = d_ref[...].astype(F32)
        z = c * v
        row = lax.broadcasted_iota(jnp.int32, z.shape, 0)
        z1, z2 = _shift_down(z, 1, row), _shift_down(z, 2, row)
        w0, w1, w2 = w_ref[0:1, :], w_ref[1:2, :], w_ref[2:3, :]
        y = w0 * z2 + w1 * z1 + w2 * z
        dy = d * b
        db_ref[...] = (d * y).astype(BF16)
        dz = w2 * dy + w1 * _shift_up(dy, 1, row, t) + w0 * _shift_up(dy, 2, row, t)
        dc_ref[...] = (dz * v).astype(BF16)
        dv_ref[...] = (dz * c).astype(BF16)
        dw_ref[0:1, :] = jnp.sum(dy * z2, axis=0, keepdims=True)
        dw_ref[1:2, :] = jnp.sum(dy * z1, axis=0, keepdims=True)
        dw_ref[2:3, :] = jnp.sum(dy * z, axis=0, keepdims=True)

    cspec = pl.BlockSpec((t, LANES), lambda j: (0, j))
    wspec = pl.BlockSpec((3, LANES), lambda j: (0, j))
    return pl.pallas_call(
        body, out_shape=(SDS((t, conv), BF16),) * 3 + (SDS((3, conv), F32),), grid=(conv // LANES,),
        in_specs=_conv_specs(t, conv) + [wspec, cspec], out_specs=(cspec, cspec, cspec, wspec),
        compiler_params=_cparams(("parallel",)), name="conv_bwd")(proj, proj, proj, conv_w, dcb)


def _qkvnorm_fwd(proj, lay, q_gain, kv_gain):
    t = proj.shape[0]
    ql, kvl = lay["ql"], lay["kvl"]
    tm = _tile(t, 512, 8)

    def body(q_ref, kv_ref, gq_ref, gkv_ref, qn_ref, kvn_ref):
        qn_ref[...] = _rn_fwd_math(q_ref[...].astype(F32), gq_ref[...]).astype(BF16)
        kvn_ref[...] = _rn_fwd_math(kv_ref[...].astype(F32), gkv_ref[...]).astype(BF16)

    return pl.pallas_call(
        body, out_shape=(SDS((t, ql), BF16), SDS((t, kvl), BF16)), grid=(t // tm,),
        in_specs=[_colspec(tm, ql, lay["q"]), _colspec(tm, kvl, lay["kv"]),
                  pl.BlockSpec((1, ql), lambda i: (0, 0)), pl.BlockSpec((1, kvl), lambda i: (0, 0))],
        out_specs=(_rowspec(tm, ql), _rowspec(tm, kvl)), compiler_params=_cparams(("parallel",)),
        name="qkvnorm_fwd")(proj, proj, q_gain.reshape(1, ql), kv_gain.reshape(1, kvl))


def _qkvnorm_bwd(proj, lay, q_gain, kv_gain, dqn, dkvn):
    t = proj.shape[0]
    ql, kvl = lay["ql"], lay["kvl"]
    tm = _tile(t, 512, 8)

    def body(q_ref, kv_ref, gq_ref, gkv_ref, dqn_ref, dkvn_ref, dq_ref, dkv_ref, dgq_ref, dgkv_ref):
        dq, dgq = _rn_bwd_math(q_ref[...].astype(F32), gq_ref[...], dqn_ref[...].astype(F32))
        dkv, dgkv = _rn_bwd_math(kv_ref[...].astype(F32), gkv_ref[...], dkvn_ref[...].astype(F32))
        dq_ref[...] = dq.astype(BF16)
        dkv_ref[...] = dkv.astype(BF16)

        @pl.when(pl.program_id(0) == 0)
        def _():
            dgq_ref[...] = dgq
            dgkv_ref[...] = dgkv

        @pl.when(pl.program_id(0) > 0)
        def _():
            dgq_ref[...] += dgq
            dgkv_ref[...] += dgkv

    vq = pl.BlockSpec((1, ql), lambda i: (0, 0))
    vkv = pl.BlockSpec((1, kvl), lambda i: (0, 0))
    return pl.pallas_call(
        body, out_shape=(SDS((t, ql), BF16), SDS((t, kvl), BF16), SDS((1, ql), F32), SDS((1, kvl), F32)),
        grid=(t // tm,),
        in_specs=[_colspec(tm, ql, lay["q"]), _colspec(tm, kvl, lay["kv"]), vq, vkv, _rowspec(tm, ql),
                  _rowspec(tm, kvl)],
        out_specs=(_rowspec(tm, ql), _rowspec(tm, kvl), vq, vkv), compiler_params=_cparams(("arbitrary",)),
        name="qkvnorm_bwd")(proj, proj, q_gain.reshape(1, ql), kv_gain.reshape(1, kvl), dqn, dkvn)


def _rope(x, cos_t, sin_a, sin_b):
    return x * cos_t + pltpu.roll(x, LANES - ROPE_DIM // 2, 1) * sin_a + pltpu.roll(x, ROPE_DIM // 2, 1) * sin_b


def _rope_fwd(qf, kv, proj, lay, tables):
    t = qf.shape[0]
    tm = _tile(t, 256, 8)
    hq = N_HEADS * QK_PAD

    def body(q_ref, kn_ref, kr_ref, cos_ref, sa_ref, sb_ref, qr_ref, kf_ref):
        cos_t, sin_a, sin_b = cos_ref[...], sa_ref[...], sb_ref[...]
        kr = _rope(kr_ref[...].astype(F32), cos_t, sin_a, sin_b).astype(BF16)
        for h in range(N_HEADS):
            lo = h * QK_PAD
            qr_ref[:, lo:lo + NOPE_DIM] = q_ref[:, lo:lo + NOPE_DIM]
            qr_ref[:, lo + NOPE_DIM:lo + QK_PAD] = _rope(
                q_ref[:, lo + NOPE_DIM:lo + QK_PAD].astype(F32), cos_t, sin_a, sin_b).astype(BF16)
            kf_ref[:, lo:lo + NOPE_DIM] = kn_ref[:, h * NOPE_DIM:(h + 1) * NOPE_DIM]
            kf_ref[:, lo + NOPE_DIM:lo + QK_PAD] = kr

    tab = _rowspec(tm, LANES)
    return pl.pallas_call(
        body, out_shape=(SDS((t, hq), BF16), SDS((t, hq), BF16)), grid=(t // tm,),
        in_specs=[_rowspec(tm, hq), _rowspec(tm, N_HEADS * NOPE_DIM), _colspec(tm, LANES, lay["kr"]), tab, tab, tab],
        out_specs=(_rowspec(tm, hq), _rowspec(tm, hq)), compiler_params=_cparams(("parallel",)),
        name="rope_fwd")(qf, kv, proj, *tables)


def _rope_bwd(dqr, dkf, dv, tables):
    t = dqr.shape[0]
    tm = _tile(t, 256, 8)
    hq = N_HEADS * QK_PAD
    hn = N_HEADS * NOPE_DIM

    def body(dq_ref, dk_ref, dv_ref, cos_ref, sa_ref, sb_ref, dqf_ref, dkv_ref, dkr_ref):
        cos_t, sin_a, sin_b = cos_ref[...], -sa_ref[...], -sb_ref[...]
        dkr = jnp.zeros((tm, LANES), F32)
        for h in range(N_HEADS):
            lo = h * QK_PAD
            dqf_ref[:, lo:lo + NOPE_DIM] = dq_ref[:, lo:lo + NOPE_DIM].astype(BF16)
            dqf_ref[:, lo + NOPE_DIM:lo + QK_PAD] = _rope(
                dq_ref[:, lo + NOPE_DIM:lo + QK_PAD].astype(F32), cos_t, sin_a, sin_b).astype(BF16)
            dkv_ref[:, h * NOPE_DIM:(h + 1) * NOPE_DIM] = dk_ref[:, lo:lo + NOPE_DIM]
            dkr = dkr + dk_ref[:, lo + NOPE_DIM:lo + QK_PAD].astype(F32)
        dkv_ref[:, hn:] = dv_ref[...]
        dkr_ref[...] = _rope(dkr, cos_t, sin_a, sin_b).astype(BF16)

    tab = _rowspec(tm, LANES)
    return pl.pallas_call(
        body, out_shape=(SDS((t, hq), BF16), SDS((t, 2 * hn), BF16), SDS((t, LANES), BF16)), grid=(t // tm,),
        in_specs=[_rowspec(tm, hq), _rowspec(tm, hq), _rowspec(tm, hn), tab, tab, tab],
        out_specs=(_rowspec(tm, hq), _rowspec(tm, 2 * hn), tab), compiler_params=_cparams(("parallel",)),
        name="rope_bwd")(dqr, dkf, dv, *tables)


def _chunk_mask(bq):
    qc = lax.broadcasted_iota(jnp.int32, (bq, bq), 0) // CHUNK
    kc = lax.broadcasted_iota(jnp.int32, (bq, bq), 1) // CHUNK
    return kc <= qc


_NT = (((1,), (1,)), ((), ()))
_TN = (((0,), (0,)), ((), ()))
LOG2E = 1.4426950408889634
EXP2_SCALE = ATTN_SCALE * LOG2E


def _attn_block(t):
    return 512 if t >= 2048 else 128


def _two_slot_pipeline(unmasked, issue, consume, carry):
    issue(0, 0)

    def pair(n, c):
        issue(2 * n + 1, 1)
        c = consume(2 * n, 0, c, False)
        issue(2 * n + 2, 0)
        return consume(2 * n + 1, 1, c, False)

    carry = lax.fori_loop(0, unmasked // 2, pair, carry)

    def even(c):
        return consume(unmasked, 0, c, True)

    def odd(c):
        issue(unmasked, 1)
        c = consume(unmasked - 1, 0, c, False)
        return consume(unmasked, 1, c, True)

    return lax.cond(unmasked % 2 == 0, even, odd, carry)


def _attn_fwd(qr, kf, kv, gather=None):
    t = qr.shape[0]
    bq = _attn_block(t)
    nq = t // bq
    bufs, layer = gather if gather is not None else ((), None)
    nw = len(bufs)

    def body(*refs):
        q_ref, k_ref, v_ref = refs[:3]
        o_ref, lse_ref = refs[3 + nw:5 + nw]
        buf_refs = refs[5 + nw:5 + 2 * nw]
        vaug_ref, s_ref = refs[5 + 2 * nw], refs[6 + 2 * nw]
        sems = refs[7 + 2 * nw:]
        h, i = pl.program_id(0), pl.program_id(1)

        if nw:
            @pl.when((h == 0) & (i == 0))
            def _():
                _gather_start(buf_refs, layer, *sems)

        @pl.when(i == 0)
        def _():
            vaug_ref[:, :V_DIM] = v_ref[...]
            vaug_ref[:, V_DIM:] = jnp.ones((t, V_DIM), BF16)

        def issue(j, slot):
            off = pl.multiple_of(j * bq, bq)
            s_ref[slot] = lax.dot_general(q_ref[...], k_ref[pl.ds(off, bq), :], _NT, preferred_element_type=F32)

        def consume(j, slot, carry, masked):
            m, acc = carry
            off = pl.multiple_of(j * bq, bq)
            s = s_ref[slot]
            if masked:
                s = jnp.where(_chunk_mask(bq), s, NEG_BIG)
            m_new = jnp.maximum(m, jnp.max(s, axis=1, keepdims=True))
            alpha = jnp.exp2((m - m_new) * EXP2_SCALE)
            pr = jnp.exp2((s - m_new) * EXP2_SCALE)
            acc = alpha * acc + jnp.dot(pr.astype(BF16), vaug_ref[pl.ds(off, bq), :], preferred_element_type=F32)
            return m_new, acc

        init = (jnp.full((bq, 1), NEG_BIG, F32), jnp.zeros((bq, 2 * V_DIM), F32))
        m, acc = _two_slot_pipeline(i, issue, consume, init)
        l = acc[:, V_DIM:V_DIM + 1]
        o_ref[...] = (acc[:, :V_DIM] / l).astype(BF16)
        lse_ref[0] = jnp.broadcast_to(m * ATTN_SCALE + jnp.log(l), (bq, LANES))

        if nw:
            @pl.when((h == N_HEADS - 1) & (i == nq - 1))
            def _():
                _gather_finish(buf_refs, layer, *sems)

    out_shape = (SDS((t, N_HEADS * V_DIM), BF16), SDS((N_HEADS, t, LANES), F32)) + tuple(SDS(b.shape, b.dtype) for b in bufs)
    sem_shapes = [pltpu.SemaphoreType.DMA((6 * nw,)), pltpu.SemaphoreType.DMA((6 * nw,))] if nw else []
    outs = pl.pallas_call(
        body, out_shape=out_shape, grid=(N_HEADS, nq),
        in_specs=[pl.BlockSpec((bq, QK_PAD), lambda h, i: (i, h)), pl.BlockSpec((t, QK_PAD), lambda h, i: (0, h)),
                  pl.BlockSpec((t, V_DIM), lambda h, i: (0, N_HEADS + h))] + [ANY] * nw,
        out_specs=(pl.BlockSpec((bq, V_DIM), lambda h, i: (i, h)),
                   pl.BlockSpec((1, bq, LANES), lambda h, i: (h, i, 0))) + (ANY,) * nw,
        input_output_aliases={3 + n: 2 + n for n in range(nw)},
        scratch_shapes=[pltpu.VMEM((t, 2 * V_DIM), BF16), pltpu.VMEM((2, bq, bq), F32)] + sem_shapes,
        compiler_params=_cparams(("arbitrary", "arbitrary")), name="attn_fwd_gather" if nw else "attn_fwd")(qr, kf, kv, *bufs)
    return outs[0], outs[1], list(outs[2:])


def _attn_delta(do, o):
    t = do.shape[0]
    tm = _tile(t, 512, 8)

    def body(do_ref, o_ref, dl_ref):
        prod = do_ref[...].astype(F32) * o_ref[...].astype(F32)
        for h in range(N_HEADS):
            s = jnp.sum(prod[:, h * V_DIM:(h + 1) * V_DIM], axis=1, keepdims=True)
            dl_ref[h] = jnp.broadcast_to(s, (tm, LANES))

    return pl.pallas_call(
        body, out_shape=SDS((N_HEADS, t, LANES), F32), grid=(t // tm,),
        in_specs=[_rowspec(tm, N_HEADS * V_DIM), _rowspec(tm, N_HEADS * V_DIM)],
        out_specs=pl.BlockSpec((N_HEADS, tm, LANES), lambda i: (0, i, 0)), compiler_params=_cparams(("parallel",)),
        name="attn_delta")(do, o)


def _attn_bwd(qr, kf, kv, do, lse, delta, exchange=None):
    t = qr.shape[0]
    bq = _attn_block(t)
    nq = t // bq
    pbs, lands = exchange if exchange is not None else ((), ())
    nw = len(pbs)

    def body(*refs):
        k_ref, v_ref, q_ref, do_ref, lse_ref, dl_ref = refs[:6]
        pb_refs = refs[6:6 + nw]
        dk_ref, dv_ref, dq_ref = refs[6 + 2 * nw:9 + 2 * nw]
        land_refs = refs[9 + 2 * nw:9 + 3 * nw]
        s_ref, dp_ref = refs[9 + 3 * nw], refs[10 + 3 * nw]
        sems = refs[11 + 3 * nw:]
        h, j = pl.program_id(0), pl.program_id(1)

        if nw:
            @pl.when((h == 0) & (j == 0))
            def _():
                _exchange_start(pb_refs, land_refs, *sems)

        @pl.when(j == 0)
        def _():
            dq_ref[...] = jnp.zeros_like(dq_ref)

        k = k_ref[...]
        v = v_ref[...]

        def issue(b, slot):
            off = pl.multiple_of((nq - 1 - b) * bq, bq)
            s_ref[slot] = lax.dot_general(q_ref[pl.ds(off, bq), :], k, _NT, preferred_element_type=F32)
            dp_ref[slot] = lax.dot_general(do_ref[pl.ds(off, bq), :], v, _NT, preferred_element_type=F32)

        def consume(b, slot, carry, masked):
            dk, dv = carry
            off = pl.multiple_of((nq - 1 - b) * bq, bq)
            q = q_ref[pl.ds(off, bq), :]
            do_ = do_ref[pl.ds(off, bq), :]
            lse2 = lse_ref[0, pl.ds(off, bq), :][:, :1] * LOG2E
            dl_i = dl_ref[0, pl.ds(off, bq), :][:, :1]
            s = s_ref[slot]
            if masked:
                s = jnp.where(_chunk_mask(bq), s, NEG_BIG)
            pr = jnp.exp2(s * EXP2_SCALE - lse2)
            dv = dv + lax.dot_general(pr.astype(BF16), do_, _TN, preferred_element_type=F32)
            ds = (pr * (dp_ref[slot] - dl_i)).astype(BF16)
            dk = dk + lax.dot_general(ds, q, _TN, preferred_element_type=F32)
            dq_ref[pl.ds(off, bq), :] += jnp.dot(ds, k, preferred_element_type=F32) * ATTN_SCALE
            return dk, dv

        init = (jnp.zeros((bq, QK_PAD), F32), jnp.zeros((bq, V_DIM), F32))
        dk, dv = _two_slot_pipeline(nq - 1 - j, issue, consume, init)
        dk_ref[...] = (dk * ATTN_SCALE).astype(BF16)
        dv_ref[...] = dv.astype(BF16)

        if nw:
            @pl.when((h == N_HEADS - 1) & (j == nq - 1))
            def _():
                _exchange_finish(pb_refs, land_refs, *sems)

    stat = pl.BlockSpec((1, t, LANES), lambda h, j: (h, 0, 0))
    out_shape = (SDS((t, N_HEADS * QK_PAD), BF16), SDS((t, N_HEADS * V_DIM), BF16), SDS((t, N_HEADS * QK_PAD), F32))
    sem_shapes = [pltpu.SemaphoreType.DMA((3 * nw,)), pltpu.SemaphoreType.DMA((3 * nw,))] if nw else []
    outs = pl.pallas_call(
        body, out_shape=out_shape + tuple(SDS(l.shape, l.dtype) for l in lands), grid=(N_HEADS, nq),
        in_specs=[pl.BlockSpec((bq, QK_PAD), lambda h, j: (j, h)), pl.BlockSpec((bq, V_DIM), lambda h, j: (j, N_HEADS + h)),
                  pl.BlockSpec((t, QK_PAD), lambda h, j: (0, h)), pl.BlockSpec((t, V_DIM), lambda h, j: (0, h)), stat, stat]
        + [ANY] * (2 * nw),
        out_specs=(pl.BlockSpec((bq, QK_PAD), lambda h, j: (j, h)), pl.BlockSpec((bq, V_DIM), lambda h, j: (j, h)),
                   pl.BlockSpec((t, QK_PAD), lambda h, j: (0, h))) + (ANY,) * nw,
        input_output_aliases={6 + nw + n: 3 + n for n in range(nw)},
        scratch_shapes=[pltpu.VMEM((2, bq, bq), F32), pltpu.VMEM((2, bq, bq), F32)] + sem_shapes,
        compiler_params=_cparams(("arbitrary", "arbitrary")),
        name="attn_bwd_exchange" if nw else "attn_bwd")(kf, kv, qr, do, lse, delta, *pbs, *lands)
    return outs[0], outs[1], outs[2], list(outs[3:])


def _merge_fwd(proj, lay, ya, yb):
    t, d = ya.shape
    tm = _tile(t, 512, 8)

    def body(gc_ref, gm_ref, ya_ref, yb_ref, o_ref):
        o_ref[...] = (_sigmoid(gc_ref[...].astype(F32)) * ya_ref[...].astype(F32)
                      + _sigmoid(gm_ref[...].astype(F32)) * yb_ref[...].astype(F32)).astype(BF16)

    return pl.pallas_call(
        body, out_shape=SDS((t, d), BF16), grid=(t // tm,),
        in_specs=[_colspec(tm, d, lay["gc"]), _colspec(tm, d, lay["gm"]), _rowspec(tm, d), _rowspec(tm, d)],
        out_specs=_rowspec(tm, d), compiler_params=_cparams(("parallel",)), name="merge_fwd")(proj, proj, ya, yb)


def _merge_bwd(proj, lay, ya, yb, dmg):
    t, d = ya.shape
    tm = _tile(t, 512, 8)

    def body(gc_ref, gm_ref, ya_ref, yb_ref, d_ref, dya_ref, dyb_ref, dgc_ref, dgm_ref):
        dm = d_ref[...].astype(F32)
        sc = _sigmoid(gc_ref[...].astype(F32))
        sm = _sigmoid(gm_ref[...].astype(F32))
        dya_ref[...] = (dm * sc).astype(BF16)
        dyb_ref[...] = (dm * sm).astype(BF16)
        dgc_ref[...] = (dm * ya_ref[...].astype(F32) * (sc * (1.0 - sc))).astype(BF16)
        dgm_ref[...] = (dm * yb_ref[...].astype(F32) * (sm * (1.0 - sm))).astype(BF16)

    r = _rowspec(tm, d)
    return pl.pallas_call(
        body, out_shape=(SDS((t, d), BF16),) * 4, grid=(t // tm,),
        in_specs=[_colspec(tm, d, lay["gc"]), _colspec(tm, d, lay["gm"]), r, r, r], out_specs=(r, r, r, r),
        compiler_params=_cparams(("parallel",)), name="merge_bwd")(proj, proj, ya, yb, dmg)


def _ple_fwd(h, gp, pp, norm_gain=None):
    t, d = h.shape
    tm = _tile(t, 512, 8)
    fuse_norm = norm_gain is not None

    def body(h_ref, gp_ref, pp_ref, *rest):
        out = h_ref[...] + _sigmoid(gp_ref[...].astype(F32)) * pp_ref[...].astype(F32)
        rest[-2 if fuse_norm else -1][...] = out
        if fuse_norm:
            rest[-1][...] = _rn_fwd_math(out, rest[0][...]).astype(BF16)

    r = _rowspec(tm, d)
    if not fuse_norm:
        return pl.pallas_call(body, out_shape=SDS((t, d), F32), grid=(t // tm,), in_specs=[r, r, r], out_specs=r,
                              compiler_params=_cparams(("parallel",)), name="ple_fwd")(h, gp, pp), None
    return pl.pallas_call(
        body, out_shape=(SDS((t, d), F32), SDS((t, d), BF16)), grid=(t // tm,),
        in_specs=[r, r, r, pl.BlockSpec((1, d), lambda i: (0, 0))], out_specs=(r, r),
        compiler_params=_cparams(("parallel",)), name="ple_fwd_norm")(h, gp, pp, norm_gain.reshape(1, d))


def _ple_bwd(dh, gp, pp):
    t, d = dh.shape
    tm = _tile(t, 512, 8)

    def body(dh_ref, gp_ref, pp_ref, dpp_ref, dgp_ref):
        g = dh_ref[...]
        s = _sigmoid(gp_ref[...].astype(F32))
        dpp_ref[...] = (g * s).astype(BF16)
        dgp_ref[...] = (g * pp_ref[...].astype(F32) * (s * (1.0 - s))).astype(BF16)

    r = _rowspec(tm, d)
    return pl.pallas_call(body, out_shape=(SDS((t, d), BF16),) * 2, grid=(t // tm,), in_specs=[r, r, r],
                          out_specs=(r, r), compiler_params=_cparams(("parallel",)), name="ple_bwd")(dh, gp, pp)


def _adamw(w, g, m, v):
    shape = w.shape
    cols = shape[-1]
    rows = w.size // cols
    tr = _tile(rows, max(8, BLOCK_ELEMS // cols // 8 * 8), 8)

    def body(w_ref, g_ref, m_ref, v_ref, d_ref, nm_ref, nv_ref):
        g_ = g_ref[...]
        nm = ADAM_B1 * m_ref[...] + (1.0 - ADAM_B1) * g_
        nv = ADAM_B2 * v_ref[...] + (1.0 - ADAM_B2) * (g_ * g_)
        m_hat = nm / (1.0 - ADAM_B1 ** ADAM_STEP)
        v_hat = nv / (1.0 - ADAM_B2 ** ADAM_STEP)
        d_ref[...] = -ADAM_LR * (m_hat / (jnp.sqrt(v_hat) + ADAM_EPS) + ADAM_WD * w_ref[...])
        nm_ref[...] = nm
        nv_ref[...] = nv

    r = _rowspec(tr, cols)
    outs = pl.pallas_call(
        body, out_shape=(SDS((rows, cols), F32),) * 3, grid=(rows // tr,), in_specs=[r, r, r, r], out_specs=(r, r, r),
        compiler_params=_cparams(("parallel",)), name="adamw")(*(a.reshape(rows, cols) for a in (w, g, m, v)))
    return tuple(o.reshape(shape) for o in outs)


ANY = pl.BlockSpec(memory_space=pl.ANY)


def _place():
    x, y, c = lax.axis_index("x"), lax.axis_index("y"), lax.axis_index("c")
    return x, y, c, [(1 - x, y), (x, 1 - y), (1 - x, 1 - y)]


def _half_rows(rows, cols):
    half = rows // 2
    return half, _tile(half, max(16, BLOCK_ELEMS // cols // 16 * 16), 16)


def _my_chip():
    return 2 * lax.axis_index("x") + lax.axis_index("y")


def _cast_into_slot(w):
    nl, r, c = w.shape
    tr = _tile(r, max(16, BLOCK_ELEMS // c // 16 * 16), 16)

    def body(w_ref, o_ref):
        o_ref[...] = w_ref[...].astype(BF16)

    return pl.pallas_call(
        body, out_shape=SDS((nl, N_CHIPS, r, c), BF16), grid=(nl, r // tr),
        in_specs=[pl.BlockSpec((None, tr, c), lambda l, i: (l, i, 0))],
        out_specs=pl.BlockSpec((None, None, tr, c), lambda l, i: (l, _my_chip(), i, 0)),
        compiler_params=_cparams(("parallel", "parallel")), name="cast_into_slot")(w)


def _gather_copy(ref, layer, send_sems, recv_sems, sem, chip, half, to):
    r2 = ref.shape[2] // 2
    rows = ref.at[layer, chip, pl.ds(half * r2, r2)]
    return pltpu.make_async_remote_copy(src_ref=rows, dst_ref=rows, send_sem=send_sems.at[sem], recv_sem=recv_sems.at[sem],
                                        device_id=to, device_id_type=MESH)


def _gather_start(refs, layer, send_sems, recv_sems):
    x, y, c, chips = _place()
    for wi, ref in enumerate(refs):
        for n, chip in enumerate(chips):
            _gather_copy(ref, layer, send_sems, recv_sems, 6 * wi + n, 2 * x + y, c, (*chip, c)).start()


def _gather_finish(refs, layer, send_sems, recv_sems):
    x, y, c, chips = _place()
    me, sibling = (x, y, c), (x, y, 1 - c)
    for wi, ref in enumerate(refs):
        for n, chip in enumerate(chips):
            k = 2 * chip[0] + chip[1]
            _gather_copy(ref, layer, send_sems, recv_sems, 6 * wi + n, k, c, me).wait_recv()
            _gather_copy(ref, layer, send_sems, recv_sems, 6 * wi + 3 + n, k, c, sibling).start()
    for wi, ref in enumerate(refs):
        for n, chip in enumerate(chips):
            _gather_copy(ref, layer, send_sems, recv_sems, 6 * wi + 3 + n, 2 * chip[0] + chip[1], 1 - c, me).wait_recv()
    for wi, ref in enumerate(refs):
        for n, chip in enumerate(chips):
            _gather_copy(ref, layer, send_sems, recv_sems, 6 * wi + n, 2 * x + y, c, (*chip, c)).wait_send()
            _gather_copy(ref, layer, send_sems, recv_sems, 6 * wi + 3 + n, 2 * chip[0] + chip[1], c, sibling).wait_send()


def _all_gather_weights(bufs, layer):
    nw = len(bufs)

    def body(*refs):
        _gather_start(refs[nw:2 * nw], layer, refs[2 * nw], refs[2 * nw + 1])
        _gather_finish(refs[nw:2 * nw], layer, refs[2 * nw], refs[2 * nw + 1])

    return list(pl.pallas_call(
        body, out_shape=tuple(SDS(b.shape, b.dtype) for b in bufs), in_specs=[ANY] * nw, out_specs=(ANY,) * nw,
        input_output_aliases={i: i for i in range(nw)},
        scratch_shapes=[pltpu.SemaphoreType.DMA((6 * nw,)), pltpu.SemaphoreType.DMA((6 * nw,))],
        name="all_gather_weights")(*bufs))


def _pair_swap(grads):
    nw = len(grads)

    def body(*refs):
        ins, outs = refs[:nw], refs[nw:2 * nw]
        send_sems, recv_sems = refs[2 * nw], refs[2 * nw + 1]
        x, y, c, _ = _place()
        copies = []
        for wi, (g_ref, o_ref) in enumerate(zip(ins, outs)):
            r2 = g_ref.shape[1] // 2
            copies.append(pltpu.make_async_remote_copy(
                src_ref=g_ref.at[:, pl.ds((1 - c) * r2, r2)], dst_ref=o_ref, send_sem=send_sems.at[wi],
                recv_sem=recv_sems.at[wi], device_id=(x, y, 1 - c), device_id_type=MESH))
            copies[-1].start()
        for cp in copies:
            cp.wait()

    return pl.pallas_call(
        body, out_shape=tuple(SDS((N_CHIPS, g.shape[1] // 2, g.shape[2]), g.dtype) for g in grads),
        in_specs=[ANY] * nw, out_specs=(ANY,) * nw,
        scratch_shapes=[pltpu.SemaphoreType.DMA((nw,)), pltpu.SemaphoreType.DMA((nw,))], name="pair_swap")(*grads)


def _pair_add(g, other):
    _, r, c = g.shape
    r2, tr = _half_rows(r, c)
    nb = r2 // tr

    def body(g_ref, o_ref, pb_ref, land_ref):
        total = (g_ref[...].astype(F32) + o_ref[...].astype(F32)).astype(BF16)
        pb_ref[...] = total

        @pl.when(pl.program_id(1) == _my_chip())
        def _():
            land_ref[...] = total

    blk = pl.BlockSpec((None, tr, c), lambda j, k: (k, j, 0))
    return pl.pallas_call(
        body, out_shape=(SDS((N_CHIPS, r2, c), BF16),) * 2, grid=(nb, N_CHIPS),
        in_specs=[pl.BlockSpec((None, tr, c), lambda j, k: (k, lax.axis_index("c") * nb + j, 0)), blk],
        out_specs=(blk, pl.BlockSpec((None, tr, c), lambda j, k: (_my_chip(), j, 0))),
        compiler_params=_cparams(("parallel", "arbitrary")), name="pair_add")(g, other)


def _exchange_copy(p_ref, l_ref, send_sems, recv_sems, sem, src_slot, dst_slot, to):
    return pltpu.make_async_remote_copy(src_ref=p_ref.at[src_slot], dst_ref=l_ref.at[dst_slot], send_sem=send_sems.at[sem],
                                        recv_sem=recv_sems.at[sem], device_id=to, device_id_type=MESH)


def _exchange_start(p_refs, l_refs, send_sems, recv_sems):
    x, y, c, chips = _place()
    for wi, (p_ref, l_ref) in enumerate(zip(p_refs, l_refs)):
        for n, chip in enumerate(chips):
            _exchange_copy(p_ref, l_ref, send_sems, recv_sems, 3 * wi + n, 2 * chip[0] + chip[1], 2 * x + y, (*chip, c)).start()


def _exchange_finish(p_refs, l_refs, send_sems, recv_sems):
    x, y, c, chips = _place()
    for wi, (p_ref, l_ref) in enumerate(zip(p_refs, l_refs)):
        for n, chip in enumerate(chips):
            _exchange_copy(p_ref, l_ref, send_sems, recv_sems, 3 * wi + n, 2 * x + y, 2 * chip[0] + chip[1], (x, y, c)).wait_recv()
    for wi, (p_ref, l_ref) in enumerate(zip(p_refs, l_refs)):
        for n, chip in enumerate(chips):
            _exchange_copy(p_ref, l_ref, send_sems, recv_sems, 3 * wi + n, 2 * chip[0] + chip[1], 2 * x + y, (*chip, c)).wait_send()


def _chip_all_to_all(pbs, lands):
    nw = len(pbs)

    def body(*refs):
        _exchange_start(refs[:nw], refs[2 * nw:3 * nw], refs[3 * nw], refs[3 * nw + 1])
        _exchange_finish(refs[:nw], refs[2 * nw:3 * nw], refs[3 * nw], refs[3 * nw + 1])

    return list(pl.pallas_call(
        body, out_shape=tuple(SDS(l.shape, l.dtype) for l in lands), in_specs=[ANY] * (2 * nw), out_specs=(ANY,) * nw,
        input_output_aliases={nw + i: i for i in range(nw)},
        scratch_shapes=[pltpu.SemaphoreType.DMA((3 * nw,)), pltpu.SemaphoreType.DMA((3 * nw,))],
        name="chip_all_to_all")(*pbs, *lands))


def _sum_chips(land, gstack, layer):
    _, r, c = gstack.shape
    r2, tr = _half_rows(r, c)
    nb = r2 // tr

    def body(l_ref, g_ref, out_ref):
        out_ref[...] = ((l_ref[0].astype(F32) + l_ref[1].astype(F32)) + l_ref[2].astype(F32)) + l_ref[3].astype(F32)

    return pl.pallas_call(
        body, out_shape=SDS(gstack.shape, F32), grid=(nb,),
        in_specs=[pl.BlockSpec((N_CHIPS, tr, c), lambda j: (0, j, 0)), ANY],
        out_specs=pl.BlockSpec((None, tr, c), lambda j: (layer, lax.axis_index("c") * nb + j, 0)),
        input_output_aliases={1: 0}, compiler_params=_cparams(("parallel",)), name="sum_chips")(land, gstack)


def _pair_gather(gstacks, layer):
    nw = len(gstacks)

    def body(*refs):
        outs = refs[nw:2 * nw]
        send_sems, recv_sems = refs[2 * nw], refs[2 * nw + 1]
        x, y, c, _ = _place()

        def copy(ref, wi, half):
            r2 = ref.shape[1] // 2
            blk = ref.at[layer, pl.ds(half * r2, r2)]
            return pltpu.make_async_remote_copy(src_ref=blk, dst_ref=blk, send_sem=send_sems.at[wi],
                                                recv_sem=recv_sems.at[wi], device_id=(x, y, 1 - c), device_id_type=MESH)

        sent = [copy(ref, wi, c) for wi, ref in enumerate(outs)]
        for cp in sent:
            cp.start()
        for wi, ref in enumerate(outs):
            copy(ref, wi, 1 - c).wait_recv()
        for cp in sent:
            cp.wait_send()

    return pl.pallas_call(
        body, out_shape=tuple(SDS(g.shape, g.dtype) for g in gstacks), in_specs=[ANY] * nw, out_specs=(ANY,) * nw,
        input_output_aliases={i: i for i in range(nw)},
        scratch_shapes=[pltpu.SemaphoreType.DMA((nw,)), pltpu.SemaphoreType.DMA((nw,))], name="pair_gather")(*gstacks)


class _Rider:
    def __init__(self, tag, operands, out_shapes, aliases, sem_shapes, start, finish):
        self.tag, self.operands, self.out_shapes, self.aliases = tag, list(operands), list(out_shapes), dict(aliases)
        self.sem_shapes, self._start, self._finish = list(sem_shapes), start, finish
        self._refs = None

    def split(self, refs, n_outs, n_scratch, grid):
        n_in, n_out = len(self.operands), len(self.out_shapes)
        ins = refs[:n_in]
        own_outs = refs[n_in:n_in + n_outs]
        outs = refs[n_in + n_outs:n_in + n_outs + n_out]
        own_scratch = refs[n_in + n_outs + n_out:n_in + n_outs + n_out + n_scratch]
        sems = refs[n_in + n_outs + n_out + n_scratch:]
        self._refs = (ins, outs, sems)
        first = functools.reduce(lambda p, q: p & q, [pl.program_id(ax) == 0 for ax in range(len(grid))])

        @pl.when(first)
        def _():
            self._start(ins, outs, sems)

        return tuple(own_outs) + tuple(own_scratch)

    def finish_at_last_step(self, grid):
        ins, outs, sems = self._refs
        last = functools.reduce(lambda p, q: p & q, [pl.program_id(ax) == g - 1 for ax, g in enumerate(grid)])

        @pl.when(last)
        def _():
            self._finish(ins, outs, sems)


def _gather_rider(bufs, layer):
    nw = len(bufs)
    return _Rider("_gather", bufs, [SDS(b.shape, b.dtype) for b in bufs], {i: i for i in range(nw)},
                  [pltpu.SemaphoreType.DMA((6 * nw,)), pltpu.SemaphoreType.DMA((6 * nw,))],
                  lambda ins, outs, sems: _gather_start(outs, layer, *sems),
                  lambda ins, outs, sems: _gather_finish(outs, layer, *sems))


def _pair_swap_copies(ins, outs, sems):
    x, y, c, _ = _place()
    copies = []
    for wi, (g_ref, o_ref) in enumerate(zip(ins, outs)):
        r2 = g_ref.shape[1] // 2
        copies.append(pltpu.make_async_remote_copy(
            src_ref=g_ref.at[:, pl.ds((1 - c) * r2, r2)], dst_ref=o_ref, send_sem=sems[0].at[wi],
            recv_sem=sems[1].at[wi], device_id=(x, y, 1 - c), device_id_type=MESH))
    return copies


def _pair_swap_rider(grads):
    nw = len(grads)

    def start(ins, outs, sems):
        for cp in _pair_swap_copies(ins, outs, sems):
            cp.start()

    def finish(ins, outs, sems):
        for cp in _pair_swap_copies(ins, outs, sems):
            cp.wait()

    return _Rider("_swap", grads, [SDS((N_CHIPS, g.shape[1] // 2, g.shape[2]), g.dtype) for g in grads], {},
                  [pltpu.SemaphoreType.DMA((nw,)), pltpu.SemaphoreType.DMA((nw,))], start, finish)


def _pair_gather_copy(ref, layer, sems, wi, half, peer):
    r2 = ref.shape[1] // 2
    blk = ref.at[layer, pl.ds(half * r2, r2)]
    return pltpu.make_async_remote_copy(src_ref=blk, dst_ref=blk, send_sem=sems[0].at[wi], recv_sem=sems[1].at[wi],
                                        device_id=peer, device_id_type=MESH)


def _pair_gather_rider(gstacks, layer):
    nw = len(gstacks)

    def start(ins, outs, sems):
        x, y, c, _ = _place()
        for wi, ref in enumerate(outs):
            _pair_gather_copy(ref, layer, sems, wi, c, (x, y, 1 - c)).start()

    def finish(ins, outs, sems):
        x, y, c, _ = _place()
        for wi, ref in enumerate(outs):
            _pair_gather_copy(ref, layer, sems, wi, 1 - c, (x, y, 1 - c)).wait_recv()
        for wi, ref in enumerate(outs):
            _pair_gather_copy(ref, layer, sems, wi, c, (x, y, 1 - c)).wait_send()

    return _Rider("_pair_gather", gstacks, [SDS(g.shape, g.dtype) for g in gstacks], {i: i for i in range(nw)},
                  [pltpu.SemaphoreType.DMA((nw,)), pltpu.SemaphoreType.DMA((nw,))], start, finish)


def _all_gather_small(vec, name):
    rows, w = vec.shape

    def body(v_ref, sum_ref, all_ref, send_sems, recv_sems):
        x, y, c, chips = _place()
        me, sibling = (x, y, c), (x, y, 1 - c)

        def slot(px, py, pc):
            return all_ref.at[4 * px + 2 * py + pc]

        def copy(k, block, to, src=None):
            return pltpu.make_async_remote_copy(
                src_ref=slot(*block) if src is None else src, dst_ref=slot(*block), send_sem=send_sems.at[k],
                recv_sem=recv_sems.at[k], device_id=to, device_id_type=MESH)

        first = [copy(0, me, sibling, src=v_ref)]
        first += [copy(1 + n, me, (*chip, c), src=v_ref) for n, chip in enumerate(chips)]
        for cp in first:
            cp.start()
        slot(*me)[...] = v_ref[...]
        passed = [copy(4 + n, (*chip, c), sibling) for n, chip in enumerate(chips)]
        for n, chip in enumerate(chips):
            copy(1 + n, (*chip, c), me).wait_recv()
            passed[n].start()
        copy(0, sibling, me).wait_recv()
        for n, chip in enumerate(chips):
            copy(4 + n, (*chip, 1 - c), me).wait_recv()
        for cp in first + passed:
            cp.wait_send()
        total = all_ref[0]
        for dev in range(1, 8):
            total = total + all_ref[dev]
        sum_ref[...] = total

    vm = pl.BlockSpec(memory_space=pltpu.VMEM)
    return pl.pallas_call(
        body, out_shape=(SDS((rows, w), F32), SDS((8, rows, w), F32)), in_specs=[vm], out_specs=(vm, vm),
        scratch_shapes=[pltpu.SemaphoreType.DMA((7,)), pltpu.SemaphoreType.DMA((7,))], name=name)(vec)


def _to_rows128(flat):
    n = flat.shape[0]
    rows = -(-n // (8 * LANES)) * 8
    return jnp.pad(flat, (0, rows * LANES - n)).reshape(rows, LANES)


def _in_layout(conv, ql, kvl, d):
    lay = {"conv": conv, "ql": ql, "kvl": kvl, "d": d}
    lay["q"] = 3 * conv
    lay["kr"] = lay["q"] + ql
    lay["gc"] = lay["kr"] + LANES
    lay["gm"] = lay["gc"] + d
    lay["kv"] = lay["gm"] + d
    used = lay["kv"] + kvl
    lay["width"] = -(-used // 512) * 512
    return lay


def _w_in_to_layout(w, lay):
    conv, ql, kvl, d = lay["conv"], lay["ql"], lay["kvl"], lay["d"]
    o_kv = 3 * conv + ql
    o_kr = o_kv + kvl
    o_g = o_kr + ROPE_DIM
    lead = w.shape[:-1]
    parts = [w[..., :o_kv], w[..., o_kr:o_g], jnp.zeros(lead + (LANES - ROPE_DIM,), w.dtype), w[..., o_g:o_g + 2 * d],
             w[..., o_kv:o_kr], jnp.zeros(lead + (lay["width"] - lay["kv"] - kvl,), w.dtype)]
    return jnp.concatenate(parts, axis=-1)


def _w_in_from_layout(g, lay):
    ql, kvl, d = lay["ql"], lay["kvl"], lay["d"]
    return jnp.concatenate([g[:, :lay["q"] + ql], g[:, lay["kv"]:lay["kv"] + kvl], g[:, lay["kr"]:lay["kr"] + ROPE_DIM],
                            g[:, lay["gc"]:lay["gc"] + 2 * d]], axis=1)


def _w_uq_to_layout(w):
    r = w.shape[0]
    w3 = w.reshape(r, N_HEADS, NOPE_DIM + ROPE_DIM)
    return jnp.pad(w3, ((0, 0), (0, 0), (0, QK_PAD - NOPE_DIM - ROPE_DIM))).reshape(r, N_HEADS * QK_PAD)


def _w_uq_from_layout(g):
    r = g.shape[0]
    return g.reshape(r, N_HEADS, QK_PAD)[:, :, :NOPE_DIM + ROPE_DIM].reshape(r, N_HEADS * (NOPE_DIM + ROPE_DIM))


def _w_ukv_to_layout(w):
    r = w.shape[0]
    return w.reshape(r, N_HEADS, 2, NOPE_DIM).transpose(0, 2, 1, 3).reshape(r, 2 * N_HEADS * NOPE_DIM)


def _w_ukv_from_layout(g):
    r = g.shape[0]
    return g.reshape(r, 2, N_HEADS, NOPE_DIM).transpose(0, 2, 1, 3).reshape(r, 2 * N_HEADS * NOPE_DIM)


def _chips_to_cols(buf):
    _, r, c = buf.shape
    return buf.transpose(1, 0, 2).reshape(r, N_CHIPS * c)


def _cols_to_chips(g):
    r, c4 = g.shape
    return g.reshape(r, N_CHIPS, c4 // N_CHIPS).transpose(1, 0, 2)


GATE_UP = ("ffn1_w_gu", "ffn2_w_gu")
EARLY = ("ffn1_w_gu", "ffn1_w_down", "w_in", "w_conv_out", "w_uq", "w_ukv")
LATE_WITH_GU = ("ffn2_w_down", "w_mla_out", "w_o", "w_ple_gate", "w_ple_proj")
LATE_WITH_DOWN = ("ffn2_w_gu",)


def _rows_view(buf):
    return buf.reshape(buf.shape[0], N_CHIPS * buf.shape[2], buf.shape[3])


def _cols_view(buf, layer):
    return _chips_to_cols(buf[layer])


def _ffn_fwd(h, n, bufs, which, layer, next_gain, gather_layer):
    tag = which
    rider = _gather_rider([bufs[k] for k in LATE_WITH_GU], gather_layer) if gather_layer is not None else None
    outs = _gu_swiglu_fwd(n, bufs[which + "_w_gu"], layer, tag + "_gu_fwd", rider=rider)
    gu, a = outs[0], outs[1]
    if rider is not None:
        bufs = {**bufs, **dict(zip(LATE_WITH_GU, outs[2]))}
    rider = _gather_rider([bufs[k] for k in LATE_WITH_DOWN], gather_layer) if gather_layer is not None else None
    outs = _mm(a, _rows_view(bufs[which + "_w_down"]), "nn", F32, tag + "_down_fwd", scale=0.5, res=h, layer=layer,
               norm_gain=next_gain, rider=rider)
    if rider is not None:
        bufs = {**bufs, **dict(zip(LATE_WITH_DOWN, outs[2]))}
    return outs[0], outs[1], (h, n, gu, a), bufs


def _ffn_bwd(dh, dhb, saved, gain, bufs, which, layer, swap=None):
    tag = which
    h, n, gu, a = saved
    d_wdown = _mm(a, dhb, "tn", BF16, tag + "_down_dw", scale=0.5)
    dgu = _down_dx_swiglu_bwd(dhb, _rows_view(bufs[which + "_w_down"]), gu, layer, tag + "_down_dx")
    rider = _pair_swap_rider(swap) if swap is not None else None
    outs = _mm(n, dgu, "tn", BF16, tag + "_gu_dw", out_chip=True, b_halves=True, rider=rider)
    d_wgu, swapped = (outs[0], outs[1]) if rider is not None else (outs, None)
    dh, dhb, dgain = _mm(dgu, bufs[which + "_w_gu"], "nt", BF16, tag + "_gu_dx", layer=layer, b_chip=True, a_halves=True,
                         norm_bwd=(h, gain, dh))
    return dh, dhb, d_wgu, d_wdown, dgain, swapped


def _layer_fwd(h0, n0, p_i, bufs, conv_w, norms, layer, lay, tables, depth, next_gain):
    nxt = layer + 1 if layer + 1 < depth else None
    h1, n2, s_ffn1, bufs = _ffn_fwd(h0, n0, bufs, "ffn1", layer, norms["mix_norm"], 0 if layer == 0 else None)
    w_in = _w_in_to_layout(_cols_view(bufs["w_in"], layer), lay)
    w_conv_out = _cols_view(bufs["w_conv_out"], layer)
    w_uq = _w_uq_to_layout(_cols_view(bufs["w_uq"], layer))
    w_ukv = _w_ukv_to_layout(_cols_view(bufs["w_ukv"], layer))
    proj = _mm(n2, w_in, "nn", BF16, "in_fwd")
    cb = _conv_fwd(proj, conv_w)
    ya = _mm(cb, w_conv_out, "nn", BF16, "conv_out_fwd")
    qn, kvn = _qkvnorm_fwd(proj, lay, norms["q_norm"], norms["kv_norm"])
    qf = _mm(qn, w_uq, "nn", BF16, "uq_fwd")
    kv = _mm(kvn, w_ukv, "nn", BF16, "ukv_fwd")
    qr, kf = _rope_fwd(qf, kv, proj, lay, tables)
    o, lse, gathered = _attn_fwd(qr, kf, kv, gather=([bufs[k] for k in EARLY], nxt) if nxt is not None else None)
    if nxt is not None:
        bufs = {**bufs, **dict(zip(EARLY, gathered))}
    yb = _mm(o, _rows_view(bufs["w_mla_out"]), "nn", BF16, "mla_out_fwd", layer=layer)
    mg = _merge_fwd(proj, lay, ya, yb)
    h2, n3 = _mm(mg, _rows_view(bufs["w_o"]), "nn", F32, "o_fwd", res=h1, layer=layer, norm_gain=norms["ffn2_norm"])
    h3, n4, s_ffn2, bufs = _ffn_fwd(h2, n3, bufs, "ffn2", layer, norms["ple_norm"], nxt)
    w_ple_proj = _cols_view(bufs["w_ple_proj"], layer)
    gp = _mm(n4, _rows_view(bufs["w_ple_gate"]), "nn", BF16, "ple_gate_fwd", layer=layer)
    pp = _mm(p_i, w_ple_proj, "nn", BF16, "ple_proj_fwd")
    h4, n_out = _ple_fwd(h3, gp, pp, next_gain)
    saved = dict(s_ffn1=s_ffn1, h1=h1, n2=n2, proj=proj, cb=cb, ya=ya, qn=qn, kvn=kvn, qr=qr, kf=kf, kv=kv, o=o,
                 lse=lse, yb=yb, mg=mg, h2=h2, s_ffn2=s_ffn2, h3=h3, n4=n4, gp=gp, pp=pp, p=p_i,
                 w_in=w_in, w_conv_out=w_conv_out, w_uq=w_uq, w_ukv=w_ukv)
    return h4, n_out, saved, bufs


def _layer_bwd(dh, s, bufs, conv_w, norms, layer, lay, tables, above, gstacks):
    gw, gn = {}, {}

    def by_rows(g):
        return g.reshape(N_CHIPS, g.shape[0] // N_CHIPS, g.shape[1])

    dpp, dgp = _ple_bwd(dh, s["gp"], s["pp"])
    gw["w_ple_proj"] = _cols_to_chips(_mm(s["p"], dpp, "tn", BF16, "ple_proj_dw"))
    gw["w_ple_gate"] = by_rows(_mm(s["n4"], dgp, "tn", BF16, "ple_gate_dw"))
    dh, dhb, gn["ple_norm"] = _mm(dgp, _rows_view(bufs["w_ple_gate"]), "nt", BF16, "ple_gate_dx", layer=layer,
                                  norm_bwd=(s["h3"], norms["ple_norm"], dh))
    dh, dhb, gw["ffn2_w_gu"], g_down, gn["ffn2_norm"], swapped = _ffn_bwd(
        dh, dhb, s["s_ffn2"], norms["ffn2_norm"], bufs, "ffn2", layer, swap=None if above is None else above[1])
    gw["ffn2_w_down"] = by_rows(g_down)
    exchange = None
    if above is not None:
        pairs = [_pair_add(g, o) for g, o in zip(above[1], swapped)]
        exchange = ([pb for pb, _ in pairs], [land for _, land in pairs])
    gw["w_o"] = by_rows(_mm(s["mg"], dhb, "tn", BF16, "o_dw"))
    dmg = _mm(dhb, _rows_view(bufs["w_o"]), "nt", BF16, "o_dx", layer=layer)
    dya, dyb, dgc, dgm = _merge_bwd(s["proj"], lay, s["ya"], s["yb"], dmg)
    gw["w_conv_out"] = _cols_to_chips(_mm(s["cb"], dya, "tn", BF16, "conv_out_dw"))
    dcb = _mm(dya, s["w_conv_out"], "nt", BF16, "conv_out_dx")
    db, dc, dv_conv, g_conv = _conv_bwd(s["proj"], conv_w, dcb)
    gw["w_mla_out"] = by_rows(_mm(s["o"], dyb, "tn", BF16, "mla_out_dw"))
    do = _mm(dyb, _rows_view(bufs["w_mla_out"]), "nt", BF16, "mla_out_dx", layer=layer)
    delta = _attn_delta(do, s["o"])
    dkf, dv, dqr, landed = _attn_bwd(s["qr"], s["kf"], s["kv"], do, s["lse"], delta, exchange=exchange)
    dqf, dkv, dkr = _rope_bwd(dqr, dkf, dv, tables)
    gw["w_uq"] = _cols_to_chips(_w_uq_from_layout(_mm(s["qn"], dqf, "tn", BF16, "uq_dw")))
    dqn = _mm(dqf, s["w_uq"], "nt", BF16, "uq_dx")
    gw["w_ukv"] = _cols_to_chips(_w_ukv_from_layout(_mm(s["kvn"], dkv, "tn", BF16, "ukv_dw")))
    dkvn = _mm(dkv, s["w_ukv"], "nt", BF16, "ukv_dx")
    dqc, dkvc, gn["q_norm"], gn["kv_norm"] = _qkvnorm_bwd(s["proj"], lay, norms["q_norm"], norms["kv_norm"], dqn, dkvn)
    t = dh.shape[0]
    dproj = jnp.concatenate([db, dc, dv_conv, dqc, dkr, dgc, dgm, dkvc,
                             jnp.zeros((t, lay["width"] - lay["kv"] - lay["kvl"]), BF16)], axis=1)
    rider = None
    if above is not None:
        gstacks = [_sum_chips(land, gs, above[0]) for land, gs in zip(landed, gstacks)]
        rider = _pair_gather_rider(gstacks, above[0])
    outs = _mm(s["n2"], dproj, "tn", BF16, "in_dw", rider=rider)
    g_in, gstacks = (outs[0], outs[1]) if rider is not None else (outs, gstacks)
    gw["w_in"] = _cols_to_chips(_w_in_from_layout(g_in, lay))
    dh, dhb, gn["mix_norm"] = _mm(dproj, s["w_in"], "nt", BF16, "in_dx", norm_bwd=(s["h1"], norms["mix_norm"], dh))
    dh, dhb, gw["ffn1_w_gu"], g_down, gn["ffn1_norm"], _ = _ffn_bwd(
        dh, dhb, s["s_ffn1"], norms["ffn1_norm"], bufs, "ffn1", layer)
    gw["ffn1_w_down"] = by_rows(g_down)
    return dh, gw, g_conv, gn, gstacks


def _rope_tables(positions):
    half = ROPE_DIM // 2
    inv_freq = ROPE_THETA ** (-jnp.arange(0, ROPE_DIM, 2, dtype=F32) / ROPE_DIM)
    ang = positions.astype(F32)[:, None] * inv_freq
    cos, sin = jnp.cos(ang), jnp.sin(ang)
    zeros = jnp.zeros_like(cos)
    cos_t = jnp.concatenate([cos, cos, zeros, zeros], axis=1)
    sin_a = jnp.concatenate([-sin, zeros, zeros, zeros], axis=1)
    sin_b = jnp.concatenate([zeros, sin, zeros, zeros], axis=1)
    assert cos_t.shape[1] == LANES and half * 4 == LANES
    return cos_t, sin_a, sin_b


def kernel(x, p, positions, ffn1_norm, ffn1_w_gu, ffn1_w_down, mix_norm, w_in, conv_w, w_conv_out, q_norm, kv_norm, w_uq, w_ukv, w_mla_out, w_o, ffn2_norm, ffn2_w_gu, ffn2_w_down, ple_norm, w_ple_gate, w_ple_proj, final_norm, loss_target, m_ffn1_norm, m_ffn1_w_gu, m_ffn1_w_down, m_mix_norm, m_w_in, m_conv_w, m_w_conv_out, m_q_norm, m_kv_norm, m_w_uq, m_w_ukv, m_w_mla_out, m_w_o, m_ffn2_norm, m_ffn2_w_gu, m_ffn2_w_down, m_ple_norm, m_w_ple_gate, m_w_ple_proj, m_final_norm, v_ffn1_norm, v_ffn1_w_gu, v_ffn1_w_down, v_mix_norm, v_w_in, v_conv_w, v_w_conv_out, v_q_norm, v_kv_norm, v_w_uq, v_w_ukv, v_w_mla_out, v_w_o, v_ffn2_norm, v_ffn2_w_gu, v_ffn2_w_down, v_ple_norm, v_w_ple_gate, v_w_ple_proj, v_final_norm):
    args = dict(zip(ARG_NAMES, (x, p, positions, ffn1_norm, ffn1_w_gu, ffn1_w_down, mix_norm, w_in, conv_w, w_conv_out, q_norm, kv_norm, w_uq, w_ukv, w_mla_out, w_o, ffn2_norm, ffn2_w_gu, ffn2_w_down, ple_norm, w_ple_gate, w_ple_proj, final_norm, loss_target, m_ffn1_norm, m_ffn1_w_gu, m_ffn1_w_down, m_mix_norm, m_w_in, m_conv_w, m_w_conv_out, m_q_norm, m_kv_norm, m_w_uq, m_w_ukv, m_w_mla_out, m_w_o, m_ffn2_norm, m_ffn2_w_gu, m_ffn2_w_down, m_ple_norm, m_w_ple_gate, m_w_ple_proj, m_final_norm, v_ffn1_norm, v_ffn1_w_gu, v_ffn1_w_down, v_mix_norm, v_w_in, v_conv_w, v_w_conv_out, v_q_norm, v_kv_norm, v_w_uq, v_w_ukv, v_w_mla_out, v_w_o, v_ffn2_norm, v_ffn2_w_gu, v_ffn2_w_down, v_ple_norm, v_w_ple_gate, v_w_ple_proj, v_final_norm)))
    depth = ffn1_norm.shape[0]
    t, d = x.shape[1], x.shape[2]
    conv = conv_w.shape[-1] * N_CHIPS
    lay = _in_layout(conv, q_norm.shape[-1], kv_norm.shape[-1], d)
    chip = 2 * lax.axis_index("x") + lax.axis_index("y")
    tables = _rope_tables(positions[0])

    bufs = {name: _cast_into_slot(args[name]) for name in BIG}
    bufs.update(zip(EARLY, _all_gather_weights([bufs[k] for k in EARLY], 0)))
    conv_rows = depth * conv_w.shape[1]
    conv_all = _all_gather_small(_to_rows128(conv_w.reshape(-1)), "all_gather_conv_w")[1]
    conv_full = conv_all[0::2, :conv_rows].reshape(N_CHIPS, depth, conv_w.shape[1], LANES)
    conv_full = conv_full.transpose(1, 2, 0, 3).reshape(depth, conv_w.shape[1], conv)
    norms = [{name: args[name][i] for name in REPLICATED} for i in range(depth)]
    p3 = p.reshape(depth, t, p.shape[-1])

    h = x[0]
    n = _rmsnorm_fwd(h, norms[0]["ffn1_norm"], "first_norm_fwd")
    saved = []
    for i in range(depth):
        h, n, s, bufs = _layer_fwd(h, n, p3[i], bufs, conv_full[i], norms[i], i, lay, tables, depth,
                                   norms[i + 1]["ffn1_norm"] if i + 1 < depth else None)
        saved.append(s)
    loss_part, dh, _, g_final = _loss_head(h, final_norm, loss_target[0])
    loss = lax.psum(loss_part[0, 0], ("x", "y", "c"))

    gstacks = [lax.empty(args[name].shape, F32) for name in BIG]
    norm_grads, conv_grads = [None] * depth, [None] * depth
    above = None
    for i in reversed(range(depth)):
        dh, gw, conv_grads[i], norm_grads[i], gstacks = _layer_bwd(
            dh, saved[i], bufs, conv_full[i], norms[i], i, lay, tables, above, gstacks)
        above = (i, [gw[name] for name in BIG])
    pairs = [_pair_add(g, o) for g, o in zip(above[1], _pair_swap(above[1]))]
    landed = _chip_all_to_all([pb for pb, _ in pairs], [land for _, land in pairs])
    gstacks = _pair_gather([_sum_chips(land, gs, above[0]) for land, gs in zip(landed, gstacks)], above[0])
    grad_x = dh[None]
    grads = dict(zip(BIG, gstacks))

    pieces = [norm_grads[i][name].reshape(-1) for i in range(depth) for name in REPLICATED]
    pieces += [g_final.reshape(-1)] + [conv_grads[i].reshape(-1) for i in range(depth)]
    vec = _all_gather_small(_to_rows128(jnp.concatenate(pieces)), "all_sum_small")[0].reshape(-1)
    off = 0
    per_name = {name: [] for name in REPLICATED}
    for i in range(depth):
        for name in REPLICATED:
            size = args[name].shape[1]
            per_name[name].append(vec[off:off + size])
            off += size
    for name in REPLICATED:
        grads[name] = jnp.stack(per_name[name])
    grads["final_norm"] = vec[off:off + d]
    off += d
    conv_g = vec[off:off + depth * 3 * conv].reshape(depth, 3, conv)
    grads["conv_w"] = lax.dynamic_slice_in_dim(conv_g, chip * conv_w.shape[-1], conv_w.shape[-1], axis=2)

    delta, new_m, new_v = {}, {}, {}
    for name in WEIGHTS:
        w_, g_, m_, v_ = args[name], grads[name], args["m_" + name], args["v_" + name]
        if w_.ndim == 1:
            outs = _adamw(w_[None], g_[None], m_[None], v_[None])
            delta[name], new_m[name], new_v[name] = (o[0] for o in outs)
        else:
            delta[name], new_m[name], new_v[name] = _adamw(w_, g_, m_, v_)
    return (loss, grad_x, *[grads[n] for n in WEIGHTS], *[delta[n] for n in WEIGHTS],
            *[new_m[n] for n in WEIGHTS], *[new_v[n] for n in WEIGHTS])
```

```python
import functools

import jax
import jax.numpy as jnp
from jax import lax
from jax.experimental import pallas as pl
from jax.experimental.pallas import tpu as pltpu

BF16 = jnp.bfloat16
F32 = jnp.float32
SDS = jax.ShapeDtypeStruct
MESH = pl.DeviceIdType.MESH

N_HEADS = 8
NOPE_DIM = 128
ROPE_DIM = 64
V_DIM = 128
QK_PAD = 256
CHUNK = 64
ROPE_THETA = 10000.0
EPS = 1e-6
ATTN_SCALE = (NOPE_DIM + ROPE_DIM) ** -0.5
NEG_BIG = -1e30

ADAM_LR = 0.001
ADAM_B1 = 0.9
ADAM_B2 = 0.999
ADAM_EPS = 1e-08
ADAM_WD = 0.01
ADAM_STEP = 10

LANES = 128
N_CHIPS = 4
VMEM_LIMIT_BYTES = 56 * 1024 * 1024
ACC_BYTES = 6 * 1024 * 1024
BLOCK_ELEMS = 1 << 19

SHARDED = (("ffn1_w_gu", 1), ("ffn1_w_down", 0), ("w_in", 1), ("w_conv_out", 1), ("w_uq", 1), ("w_ukv", 1),
           ("w_mla_out", 0), ("w_o", 0), ("ffn2_w_gu", 1), ("ffn2_w_down", 0), ("w_ple_gate", 0), ("w_ple_proj", 1))
BIG = tuple(name for name, _ in SHARDED)
REPLICATED = ("ffn1_norm", "mix_norm", "q_norm", "kv_norm", "ffn2_norm", "ple_norm")
WEIGHTS = ("ffn1_norm", "ffn1_w_gu", "ffn1_w_down", "mix_norm", "w_in", "conv_w", "w_conv_out", "q_norm",
           "kv_norm", "w_uq", "w_ukv", "w_mla_out", "w_o", "ffn2_norm", "ffn2_w_gu", "ffn2_w_down",
           "ple_norm", "w_ple_gate", "w_ple_proj", "final_norm")
ARG_NAMES = ("x", "p", "positions") + WEIGHTS + ("loss_target",) + tuple("m_" + n for n in WEIGHTS) + tuple(
    "v_" + n for n in WEIGHTS)


def _cparams(semantics=None):
    return pltpu.CompilerParams(dimension_semantics=semantics, vmem_limit_bytes=VMEM_LIMIT_BYTES)


def _tile(n, cap, mult=LANES):
    best = None
    for t in range(mult, min(n, cap) + 1, mult):
        if n % t == 0:
            best = t
    return n if best is None else best


def _sigmoid(x):
    return 1.0 / (1.0 + jnp.exp(-x))


def _rowspec(tm, width, col_block=0):
    return pl.BlockSpec((tm, width), lambda i: (i, col_block))


def _colspec(tm, width, offset):
    assert offset % width == 0, (width, offset)
    return _rowspec(tm, width, offset // width)


def _mm(a, b, mode, out_dtype, name, scale=None, res=None, layer=None, b_chip=False, out_chip=False, norm_gain=None,
        a_halves=False, b_halves=False, norm_bwd=None, rider=None):
    bshape = b.shape if layer is None else b.shape[1:]
    if b_chip:
        bshape = (bshape[1], N_CHIPS * bshape[2])
    if b_halves:
        bshape = (bshape[1], 2 * bshape[2])
    ashape = (a.shape[1], 2 * a.shape[2]) if a_halves else a.shape
    if mode == "nn":
        (m, k), (k2, n) = ashape, bshape
    elif mode == "nt":
        (m, k), (n, k2) = ashape, bshape
    else:
        (k, m), (k2, n) = ashape, bshape
    assert k == k2, (a.shape, b.shape, mode)
    n_unit = n // N_CHIPS if (out_chip or (b_chip and mode == "nn")) else n
    k_unit = k // N_CHIPS if (b_chip and mode == "nt") else k
    tn = _tile(n_unit, 1536)
    tm = _tile(m, min(512 if norm_bwd is not None else 1408, ACC_BYTES // (4 * tn)))
    tk = _tile(k_unit, 1536)
    nk = k // tk
    n_per, k_per = n_unit // tn, k_unit // tk
    n_half, k_half = n // 2 // tn, k // 2 // tk
    dims = {"nn": (((1,), (0,)), ((), ())), "nt": (((1,), (1,)), ((), ())), "tn": (((0,), (0,)), ((), ()))}[mode]

    fuse_norm = norm_gain is not None
    fuse_bwd = norm_bwd is not None
    assert not (fuse_norm or fuse_bwd) or (tn == n and not out_chip), (name, tn, n)
    assert not a_halves or mode == "nt"
    assert not b_halves or mode == "tn"
    n_extra_in = (1 if res is not None else 0) + (1 if fuse_norm else 0) + (3 if fuse_bwd else 0)
    n_outs = 3 if fuse_bwd else (2 if fuse_norm else 1)
    grid = (m // tm, n // tn, nk)
    if fuse_bwd and nk > 1:
        assert rider is None and res is None and scale is None
        return _mm_norm_bwd_k_outer(a, b, mode, name, layer, b_chip, a_halves, norm_bwd, (m, n, k), (tm, tk), k_per, k_half)

    def body(*refs):
        a_ref, b_ref = refs[0], refs[1]
        extra = list(refs[2:2 + n_extra_in])
        rest = refs[2 + n_extra_in:]
        if rider is not None:
            rest = rider.split(rest, n_outs, 1 if nk > 1 else 0, grid)
        outs = list(rest)
        acc_ref = outs.pop() if nk > 1 else None
        res_ref = extra.pop(0) if res is not None else None
        gain_ref = extra.pop(0) if fuse_norm else None

        def finish(acc):
            if scale is not None:
                acc = acc * scale
            if res_ref is not None:
                acc = res_ref[...] + acc
            if fuse_bwd:
                h_ref, g_ref, dhin_ref = extra
                dh_ref, dhb_ref, dg_ref = outs
                dx, dg = _rn_bwd_math(h_ref[...], g_ref[...], acc)
                dh = dhin_ref[...] + dx
                dh_ref[...] = dh
                dhb_ref[...] = dh.astype(BF16)

                @pl.when(pl.program_id(0) == 0)
                def _():
                    dg_ref[...] = dg

                @pl.when(pl.program_id(0) > 0)
                def _():
                    dg_ref[...] += dg
                return
            outs[0][...] = acc.astype(out_dtype)
            if fuse_norm:
                outs[1][...] = _rn_fwd_math(acc, gain_ref[...]).astype(BF16)

        part = lax.dot_general(a_ref[...].astype(BF16), b_ref[...].astype(BF16), dims,
                               preferred_element_type=F32)
        if nk == 1:
            finish(part)
        else:
            kk = pl.program_id(2)

            @pl.when(kk == 0)
            def _():
                acc_ref[...] = part

            @pl.when(kk > 0)
            def _():
                acc_ref[...] += part

            @pl.when(kk == nk - 1)
            def _():
                finish(acc_ref[...])
        if rider is not None:
            rider.finish_at_last_step(grid)

    lead = () if layer is None else (layer,)
    lead_block = () if layer is None else (None,)
    if mode == "nn":
        a_spec = pl.BlockSpec((tm, tk), lambda i, j, kk: (i, kk))
        if b_chip:
            b_spec = pl.BlockSpec(lead_block + (None, tk, tn), lambda i, j, kk: lead + (j // n_per, kk, j % n_per))
        else:
            b_spec = pl.BlockSpec(lead_block + (tk, tn), lambda i, j, kk: lead + (kk, j))
    elif mode == "nt":
        if a_halves:
            a_spec = pl.BlockSpec((None, tm, tk), lambda i, j, kk: (kk // k_half, i, kk % k_half))
        else:
            a_spec = pl.BlockSpec((tm, tk), lambda i, j, kk: (i, kk))
        if b_chip:
            b_spec = pl.BlockSpec(lead_block + (None, tn, tk), lambda i, j, kk: lead + (kk // k_per, j, kk % k_per))
        else:
            b_spec = pl.BlockSpec(lead_block + (tn, tk), lambda i, j, kk: lead + (j, kk))
    else:
        assert layer is None and not b_chip
        a_spec = pl.BlockSpec((tk, tm), lambda i, j, kk: (kk, i))
        if b_halves:
            b_spec = pl.BlockSpec((None, tk, tn), lambda i, j, kk: (j // n_half, kk, j % n_half))
        else:
            b_spec = pl.BlockSpec((tk, tn), lambda i, j, kk: (kk, j))
    if out_chip:
        o_spec = pl.BlockSpec((None, tm, tn), lambda i, j, kk: (j // n_per, i, j % n_per))
        out_shape = SDS((N_CHIPS, m, n_unit), out_dtype)
    else:
        o_spec = pl.BlockSpec((tm, tn), lambda i, j, kk: (i, j))
        out_shape = SDS((m, n), out_dtype)
    in_specs = [a_spec, b_spec] + ([o_spec] if res is not None else [])
    operands = (a, b) + ((res,) if res is not None else ())
    out_specs = o_spec
    vec = pl.BlockSpec((1, tn), lambda i, j, kk: (0, j))
    if fuse_norm:
        in_specs.append(vec)
        operands += (norm_gain.reshape(1, n),)
        out_shape, out_specs = (out_shape, SDS((m, n), BF16)), (o_spec, o_spec)
    if fuse_bwd:
        h, gain, dh_in = norm_bwd
        in_specs += [o_spec, vec, o_spec]
        operands += (h, gain.reshape(1, n), dh_in)
        out_shape = (SDS((m, n), F32), SDS((m, n), BF16), SDS((1, n), F32))
        out_specs = (o_spec, o_spec, vec)
    scratch = [pltpu.VMEM((tm, tn), F32)] if nk > 1 else []
    semantics = ("arbitrary",) * 3 if fuse_bwd else ("parallel", "parallel", "arbitrary")
    if rider is None:
        return pl.pallas_call(
            body, out_shape=out_shape, grid=grid, in_specs=in_specs, out_specs=out_specs, scratch_shapes=scratch,
            compiler_params=_cparams(semantics), name=name)(*operands)
    out_shape = out_shape if isinstance(out_shape, tuple) else (out_shape,)
    out_specs = out_specs if isinstance(out_specs, tuple) else (out_specs,)
    outs = pl.pallas_call(
        body, out_shape=out_shape + tuple(rider.out_shapes), grid=grid, in_specs=in_specs + [ANY] * len(rider.operands),
        out_specs=out_specs + (ANY,) * len(rider.out_shapes), scratch_shapes=scratch + rider.sem_shapes,
        input_output_aliases={len(operands) + i: len(out_shape) + o for i, o in rider.aliases.items()},
        compiler_params=_cparams(("arbitrary",) * 3), name=name + rider.tag)(*operands, *rider.operands)
    return tuple(outs[:n_outs]) + (list(outs[n_outs:]),)


def _mm_norm_bwd_k_outer(a, b, mode, name, layer, b_chip, a_halves, norm_bwd, sizes, tiles, k_per, k_half):
    (m, n, k), (tm, tk) = sizes, tiles
    nk, ni = k // tk, m // tm
    h, gain, dh_in = norm_bwd
    dims = {"nn": (((1,), (0,)), ((), ())), "nt": (((1,), (1,)), ((), ()))}[mode]

    def body(a_ref, b_ref, h_ref, g_ref, dhin_ref, dh_ref, dhb_ref, dg_ref, acc_ref):
        kk, i = pl.program_id(0), pl.program_id(1)
        rows = pl.ds(pl.multiple_of(i * tm, tm), tm)
        part = lax.dot_general(a_ref[...].astype(BF16), b_ref[...].astype(BF16), dims, preferred_element_type=F32)

        @pl.when(kk == 0)
        def _():
            acc_ref[rows, :] = part

        @pl.when((kk > 0) & (kk < nk - 1))
        def _():
            acc_ref[rows, :] += part

        @pl.when(kk == nk - 1)
        def _():
            dx, dg = _rn_bwd_math(h_ref[...], g_ref[...], acc_ref[rows, :] + part)
            dh = dhin_ref[...] + dx
            dh_ref[...] = dh
            dhb_ref[...] = dh.astype(BF16)

            @pl.when(i == 0)
            def _():
                dg_ref[...] = dg

            @pl.when(i > 0)
            def _():
                dg_ref[...] += dg

    lead = () if layer is None else (layer,)
    lead_block = () if layer is None else (None,)
    if a_halves:
        a_spec = pl.BlockSpec((None, tm, tk), lambda kk, i: (kk // k_half, i, kk % k_half))
    else:
        a_spec = pl.BlockSpec((tm, tk), lambda kk, i: (i, kk))
    if mode == "nn":
        assert not b_chip
        b_spec = pl.BlockSpec(lead_block + (tk, n), lambda kk, i: lead + (kk, 0))
    elif b_chip:
        b_spec = pl.BlockSpec(lead_block + (None, n, tk), lambda kk, i: lead + (kk // k_per, 0, kk % k_per))
    else:
        b_spec = pl.BlockSpec(lead_block + (n, tk), lambda kk, i: lead + (0, kk))
    late = pl.BlockSpec((tm, n), lambda kk, i: (jnp.where(kk == nk - 1, i, 0), 0))
    vec = pl.BlockSpec((1, n), lambda kk, i: (0, 0))
    return pl.pallas_call(
        body, out_shape=(SDS((m, n), F32), SDS((m, n), BF16), SDS((1, n), F32)), grid=(nk, ni),
        in_specs=[a_spec, b_spec, late, vec, late], out_specs=(late, late, vec),
        scratch_shapes=[pltpu.VMEM((m, n), F32)], compiler_params=_cparams(("arbitrary", "arbitrary")),
        name=name)(a, b, h, gain.reshape(1, n), dh_in)


def _rn_fwd_math(x, g):
    r = lax.rsqrt(jnp.mean(x * x, axis=-1, keepdims=True) + EPS)
    return (x * r) * g


def _rn_bwd_math(x, g, dn):
    r = lax.rsqrt(jnp.mean(x * x, axis=-1, keepdims=True) + EPS)
    xh = x * r
    gy = dn * g
    dx = r * (gy - xh * jnp.mean(gy * xh, axis=-1, keepdims=True))
    dg = jnp.sum(dn * xh, axis=0, keepdims=True)
    return dx, dg


def _rmsnorm_fwd(h, gain, name):
    t, d = h.shape
    tm = _tile(t, 512, 8)

    def body(h_ref, g_ref, o_ref):
        o_ref[...] = _rn_fwd_math(h_ref[...], g_ref[...]).astype(BF16)

    return pl.pallas_call(
        body, out_shape=SDS((t, d), BF16), grid=(t // tm,),
        in_specs=[_rowspec(tm, d), pl.BlockSpec((1, d), lambda i: (0, 0))], out_specs=_rowspec(tm, d),
        compiler_params=_cparams(("parallel",)), name=name)(h, gain.reshape(1, d))


def _loss_head(h, gain, target):
    t, d = h.shape
    tm = _tile(t, 512, 8)

    def body(h_ref, g_ref, t_ref, loss_ref, dh_ref, dhb_ref, dg_ref):
        x, g = h_ref[...], g_ref[...]
        err = _rn_fwd_math(x, g) - t_ref[...]
        part = 0.5 * jnp.sum(jnp.sum(err * err, axis=1, keepdims=True), axis=0, keepdims=True) * (1.0 / d)
        dx, dg = _rn_bwd_math(x, g, err * (1.0 / d))
        dh_ref[...] = dx
        dhb_ref[...] = dx.astype(BF16)

        @pl.when(pl.program_id(0) == 0)
        def _():
            dg_ref[...] = dg
            loss_ref[...] = jnp.broadcast_to(part, (1, LANES))

        @pl.when(pl.program_id(0) > 0)
        def _():
            dg_ref[...] += dg
            loss_ref[...] += jnp.broadcast_to(part, (1, LANES))

    vec = pl.BlockSpec((1, d), lambda i: (0, 0))
    return pl.pallas_call(
        body, out_shape=(SDS((1, LANES), F32), SDS((t, d), F32), SDS((t, d), BF16), SDS((1, d), F32)),
        grid=(t // tm,), in_specs=[_rowspec(tm, d), vec, _rowspec(tm, d)],
        out_specs=(pl.BlockSpec((1, LANES), lambda i: (0, 0)), _rowspec(tm, d), _rowspec(tm, d), vec),
        compiler_params=_cparams(("arbitrary",)), name="loss_head")(h, gain.reshape(1, d), target)


def _gu_swiglu_fwd(n, w_gu, layer, name, rider=None):
    t, d = n.shape
    cols = w_gu.shape[3]
    f = 2 * cols
    tn = _tile(cols, 1536)
    tm = _tile(t, 512)
    per = cols // tn

    grid = (f // tn, t // tm)

    def body(n_ref, wg_ref, wu_ref, *rest):
        gu_ref, a_ref = rest[:2] if rider is None else rider.split(rest, 2, 0, grid)
        x = n_ref[...]
        g = jnp.dot(x, wg_ref[...], preferred_element_type=F32)
        u = jnp.dot(x, wu_ref[...], preferred_element_type=F32)
        gu_ref[0] = g.astype(BF16)
        gu_ref[1] = u.astype(BF16)
        a_ref[...] = (g * _sigmoid(g) * u).astype(BF16)
        if rider is not None:
            rider.finish_at_last_step(grid)

    in_specs = [pl.BlockSpec((tm, d), lambda j, i: (i, 0)),
                pl.BlockSpec((None, None, d, tn), lambda j, i: (layer, j // per, 0, j % per)),
                pl.BlockSpec((None, None, d, tn), lambda j, i: (layer, 2 + j // per, 0, j % per))]
    out_shape = (SDS((2, t, f), BF16), SDS((t, f), BF16))
    out_specs = (pl.BlockSpec((2, tm, tn), lambda j, i: (0, i, j)), pl.BlockSpec((tm, tn), lambda j, i: (i, j)))
    if rider is None:
        return pl.pallas_call(body, out_shape=out_shape, grid=grid, in_specs=in_specs, out_specs=out_specs,
                              compiler_params=_cparams(("parallel", "parallel")), name=name)(n, w_gu, w_gu)
    outs = pl.pallas_call(
        body, out_shape=out_shape + tuple(rider.out_shapes), grid=grid, in_specs=in_specs + [ANY] * len(rider.operands),
        out_specs=out_specs + (ANY,) * len(rider.out_shapes), scratch_shapes=rider.sem_shapes,
        input_output_aliases={3 + i: 2 + o for i, o in rider.aliases.items()},
        compiler_params=_cparams(("arbitrary", "arbitrary")), name=name + rider.tag)(n, w_gu, w_gu, *rider.operands)
    return outs[0], outs[1], list(outs[2:])


def _down_dx_swiglu_bwd(dhb, w_down, gu, layer, name):
    t, d = dhb.shape
    f = gu.shape[2]
    tn = _tile(f, 1536)
    tm = _tile(t, 512)

    def body(dh_ref, w_ref, gu_ref, dgu_ref):
        da = 0.5 * lax.dot_general(dh_ref[...], w_ref[...], _NT, preferred_element_type=F32)
        g = gu_ref[0].astype(F32)
        u = gu_ref[1].astype(F32)
        sg = _sigmoid(g)
        dgu_ref[0] = (da * u * (sg * (1.0 + g * (1.0 - sg)))).astype(BF16)
        dgu_ref[1] = (da * (g * sg)).astype(BF16)

    blk = pl.BlockSpec((2, tm, tn), lambda j, i: (0, i, j))
    return pl.pallas_call(
        body, out_shape=SDS((2, t, f), BF16), grid=(f // tn, t // tm),
        in_specs=[pl.BlockSpec((tm, d), lambda j, i: (i, 0)), pl.BlockSpec((None, tn, d), lambda j, i: (layer, j, 0)), blk],
        out_specs=blk, compiler_params=_cparams(("parallel", "parallel")), name=name)(dhb, w_down, gu)


def _shift_down(z, k, row):
    return jnp.where(row >= k, pltpu.roll(z, k, 0), 0.0)


def _shift_up(z, k, row, t):
    return jnp.where(row < t - k, pltpu.roll(z, t - k, 0), 0.0)


def _conv_specs(t, conv):
    nb = conv // LANES
    return [pl.BlockSpec((t, LANES), lambda j: (0, j)), pl.BlockSpec((t, LANES), lambda j: (0, nb + j)),
            pl.BlockSpec((t, LANES), lambda j: (0, 2 * nb + j))]


def _conv_fwd(proj, conv_w):
    t = proj.shape[0]
    conv = conv_w.shape[1]

    def body(b_ref, c_ref, v_ref, w_ref, o_ref):
        z = c_ref[...].astype(F32) * v_ref[...].astype(F32)
        row = lax.broadcasted_iota(jnp.int32, z.shape, 0)
        y = w_ref[0:1, :] * _shift_down(z, 2, row) + w_ref[1:2, :] * _shift_down(z, 1, row) + w_ref[2:3, :] * z
        o_ref[...] = (b_ref[...].astype(F32) * y).astype(BF16)

    cspec = pl.BlockSpec((t, LANES), lambda j: (0, j))
    return pl.pallas_call(
        body, out_shape=SDS((t, conv), BF16), grid=(conv // LANES,),
        in_specs=_conv_specs(t, conv) + [pl.BlockSpec((3, LANES), lambda j: (0, j))], out_specs=cspec,
        compiler_params=_cparams(("parallel",)), name="conv_fwd")(proj, proj, proj, conv_w)


def _conv_bwd(proj, conv_w, dcb):
    t = proj.shape[0]
    conv = conv_w.shape[1]

    def body(b_ref, c_ref, v_ref, w_ref, d_ref, db_ref, dc_ref, dv_ref, dw_ref):
        b, c, v = b_ref[...].astype(F32), c_ref[...].astype(F32), v_ref[...].astype(F32)
        d = d_ref[...].astype(F32)
        z = c * v
        row = lax.broadcasted_iota(jnp.int32, z.shape, 0)
        z1, z2 = _shift_down(z, 1, row), _shift_down(z, 2, row)
        w0, w1, w2 = w_ref[0:1, :], w_ref[1:2, :], w_ref[2:3, :]
        y = w0 * z2 + w1 * z1 + w2 * z
        dy = d * b
        db_ref[...] = (d * y).astype(BF16)
        dz = w2 * dy + w1 * _shift_up(dy, 1, row, t) + w0 * _shift_up(dy, 2, row, t)
        dc_ref[...] = (dz * v).astype(BF16)
        dv_ref[...] = (dz * c).astype(BF16)
        dw_ref[0:1, :] = jnp.sum(dy * z2, axis=0, keepdims=True)
        dw_ref[1:2, :] = jnp.sum(dy * z1, axis=0, keepdims=True)
        dw_ref[2:3, :] = jnp.sum(dy * z, axis=0, keepdims=True)

    cspec = pl.BlockSpec((t, LANES), lambda j: (0, j))
    wspec = pl.BlockSpec((3, LANES), lambda j: (0, j))
    return pl.pallas_call(
        body, out_shape=(SDS((t, conv), BF16),) * 3 + (SDS((3, conv), F32),), grid=(conv // LANES,),
        in_specs=_conv_specs(t, conv) + [wspec, cspec], out_specs=(cspec, cspec, cspec, wspec),
        compiler_params=_cparams(("parallel",)), name="conv_bwd")(proj, proj, proj, conv_w, dcb)


def _qkvnorm_fwd(proj, lay, q_gain, kv_gain):
    t = proj.shape[0]
    ql, kvl = lay["ql"], lay["kvl"]
    tm = _tile(t, 512, 8)

    def body(q_ref, kv_ref, gq_ref, gkv_ref, qn_ref, kvn_ref):
        qn_ref[...] = _rn_fwd_math(q_ref[...].astype(F32), gq_ref[...]).astype(BF16)
        kvn_ref[...] = _rn_fwd_math(kv_ref[...].astype(F32), gkv_ref[...]).astype(BF16)

    return pl.pallas_call(
        body, out_shape=(SDS((t, ql), BF16), SDS((t, kvl), BF16)), grid=(t // tm,),
        in_specs=[_colspec(tm, ql, lay["q"]), _colspec(tm, kvl, lay["kv"]),
                  pl.BlockSpec((1, ql), lambda i: (0, 0)), pl.BlockSpec((1, kvl), lambda i: (0, 0))],
        out_specs=(_rowspec(tm, ql), _rowspec(tm, kvl)), compiler_params=_cparams(("parallel",)),
        name="qkvnorm_fwd")(proj, proj, q_gain.reshape(1, ql), kv_gain.reshape(1, kvl))


def _qkvnorm_bwd(proj, lay, q_gain, kv_gain, dqn, dkvn):
    t = proj.shape[0]
    ql, kvl = lay["ql"], lay["kvl"]
    tm = _tile(t, 512, 8)

    def body(q_ref, kv_ref, gq_ref, gkv_ref, dqn_ref, dkvn_ref, dq_ref, dkv_ref, dgq_ref, dgkv_ref):
        dq, dgq = _rn_bwd_math(q_ref[...].astype(F32), gq_ref[...], dqn_ref[...].astype(F32))
        dkv, dgkv = _rn_bwd_math(kv_ref[...].astype(F32), gkv_ref[...], dkvn_ref[...].astype(F32))
        dq_ref[...] = dq.astype(BF16)
        dkv_ref[...] = dkv.astype(BF16)

        @pl.when(pl.program_id(0) == 0)
        def _():
            dgq_ref[...] = dgq
            dgkv_ref[...] = dgkv

        @pl.when(pl.program_id(0) > 0)
        def _():
            dgq_ref[...] += dgq
            dgkv_ref[...] += dgkv

    vq = pl.BlockSpec((1, ql), lambda i: (0, 0))
    vkv = pl.BlockSpec((1, kvl), lambda i: (0, 0))
    return pl.pallas_call(
        body, out_shape=(SDS((t, ql), BF16), SDS((t, kvl), BF16), SDS((1, ql), F32), SDS((1, kvl), F32)),
        grid=(t // tm,),
        in_specs=[_colspec(tm, ql, lay["q"]), _colspec(tm, kvl, lay["kv"]), vq, vkv, _rowspec(tm, ql),
                  _rowspec(tm, kvl)],
        out_specs=(_rowspec(tm, ql), _rowspec(tm, kvl), vq, vkv), compiler_params=_cparams(("arbitrary",)),
        name="qkvnorm_bwd")(proj, proj, q_gain.reshape(1, ql), kv_gain.reshape(1, kvl), dqn, dkvn)


def _rope(x, cos_t, sin_a, sin_b):
    return x * cos_t + pltpu.roll(x, LANES - ROPE_DIM // 2, 1) * sin_a + pltpu.roll(x, ROPE_DIM // 2, 1) * sin_b


def _rope_fwd(qf, kv, proj, lay, tables):
    t = qf.shape[0]
    tm = _tile(t, 256, 8)
    hq = N_HEADS * QK_PAD

    def body(q_ref, kn_ref, kr_ref, cos_ref, sa_ref, sb_ref, qr_ref, kf_ref):
        cos_t, sin_a, sin_b = cos_ref[...], sa_ref[...], sb_ref[...]
        kr = _rope(kr_ref[...].astype(F32), cos_t, sin_a, sin_b).astype(BF16)
        for h in range(N_HEADS):
            lo = h * QK_PAD
            qr_ref[:, lo:lo + NOPE_DIM] = q_ref[:, lo:lo + NOPE_DIM]
            qr_ref[:, lo + NOPE_DIM:lo + QK_PAD] = _rope(
                q_ref[:, lo + NOPE_DIM:lo + QK_PAD].astype(F32), cos_t, sin_a, sin_b).astype(BF16)
            kf_ref[:, lo:lo + NOPE_DIM] = kn_ref[:, h * NOPE_DIM:(h + 1) * NOPE_DIM]
            kf_ref[:, lo + NOPE_DIM:lo + QK_PAD] = kr

    tab = _rowspec(tm, LANES)
    return pl.pallas_call(
        body, out_shape=(SDS((t, hq), BF16), SDS((t, hq), BF16)), grid=(t // tm,),
        in_specs=[_rowspec(tm, hq), _rowspec(tm, N_HEADS * NOPE_DIM), _colspec(tm, LANES, lay["kr"]), tab, tab, tab],
        out_specs=(_rowspec(tm, hq), _rowspec(tm, hq)), compiler_params=_cparams(("parallel",)),
        name="rope_fwd")(qf, kv, proj, *tables)


def _rope_bwd(dqr, dkf, dv, tables):
    t = dqr.shape[0]
    tm = _tile(t, 256, 8)
    hq = N_HEADS * QK_PAD
    hn = N_HEADS * NOPE_DIM

    def body(dq_ref, dk_ref, dv_ref, cos_ref, sa_ref, sb_ref, dqf_ref, dkv_ref, dkr_ref):
        cos_t, sin_a, sin_b = cos_ref[...], -sa_ref[...], -sb_ref[...]
        dkr = jnp.zeros((tm, LANES), F32)
        for h in range(N_HEADS):
            lo = h * QK_PAD
            dqf_ref[:, lo:lo + NOPE_DIM] = dq_ref[:, lo:lo + NOPE_DIM].astype(BF16)
            dqf_ref[:, lo + NOPE_DIM:lo + QK_PAD] = _rope(
                dq_ref[:, lo + NOPE_DIM:lo + QK_PAD].astype(F32), cos_t, sin_a, sin_b).astype(BF16)
            dkv_ref[:, h * NOPE_DIM:(h + 1) * NOPE_DIM] = dk_ref[:, lo:lo + NOPE_DIM]
            dkr = dkr + dk_ref[:, lo + NOPE_DIM:lo + QK_PAD].astype(F32)
        dkv_ref[:, hn:] = dv_ref[...]
        dkr_ref[...] = _rope(dkr, cos_t, sin_a, sin_b).astype(BF16)

    tab = _rowspec(tm, LANES)
    return pl.pallas_call(
        body, out_shape=(SDS((t, hq), BF16), SDS((t, 2 * hn), BF16), SDS((t, LANES), BF16)), grid=(t // tm,),
        in_specs=[_rowspec(tm, hq), _rowspec(tm, hq), _rowspec(tm, hn), tab, tab, tab],
        out_specs=(_rowspec(tm, hq), _rowspec(tm, 2 * hn), tab), compiler_params=_cparams(("parallel",)),
        name="rope_bwd")(dqr, dkf, dv, *tables)


def _chunk_mask(bq):
    qc = lax.broadcasted_iota(jnp.int32, (bq, bq), 0) // CHUNK
    kc = lax.broadcasted_iota(jnp.int32, (bq, bq), 1) // CHUNK
    return kc <= qc


_NT = (((1,), (1,)), ((), ()))
_TN = (((0,), (0,)), ((), ()))
LOG2E = 1.4426950408889634
EXP2_SCALE = ATTN_SCALE * LOG2E


def _attn_block(t):
    return 512 if t >= 2048 else 128


def _two_slot_pipeline(unmasked, issue, consume, carry):
    issue(0, 0)

    def pair(n, c):
        issue(2 * n + 1, 1)
        c = consume(2 * n, 0, c, False)
        issue(2 * n + 2, 0)
        return consume(2 * n + 1, 1, c, False)

    carry = lax.fori_loop(0, unmasked // 2, pair, carry)

    def even(c):
        return consume(unmasked, 0, c, True)

    def odd(c):
        issue(unmasked, 1)
        c = consume(unmasked - 1, 0, c, False)
        return consume(unmasked, 1, c, True)

    return lax.cond(unmasked % 2 == 0, even, odd, carry)


def _attn_fwd(qr, kf, kv, gather=None):
    t = qr.shape[0]
    bq = _attn_block(t)
    nq = t // bq
    bufs, pieces = gather if gather is not None else ((), ())
    nw = len(bufs)

    def body(*refs):
        q_ref, k_ref, v_ref = refs[:3]
        o_ref, lse_ref = refs[3 + nw:5 + nw]
        buf_refs = refs[5 + nw:5 + 2 * nw]
        vaug_ref, s_ref = refs[5 + 2 * nw], refs[6 + 2 * nw]
        sems = refs[7 + 2 * nw:]
        h, i = pl.program_id(0), pl.program_id(1)

        if nw:
            @pl.when((h == 0) & (i == 0))
            def _():
                _gather_start(buf_refs, pieces, *sems)

        @pl.when(i == 0)
        def _():
            vaug_ref[:, :V_DIM] = v_ref[...]
            vaug_ref[:, V_DIM:] = jnp.ones((t, V_DIM), BF16)

        def issue(j, slot):
            off = pl.multiple_of(j * bq, bq)
            s_ref[slot] = lax.dot_general(q_ref[...], k_ref[pl.ds(off, bq), :], _NT, preferred_element_type=F32)

        def consume(j, slot, carry, masked):
            m, acc = carry
            off = pl.multiple_of(j * bq, bq)
            s = s_ref[slot]
            if masked:
                s = jnp.where(_chunk_mask(bq), s, NEG_BIG)
            m_new = jnp.maximum(m, jnp.max(s, axis=1, keepdims=True))
            alpha = jnp.exp2((m - m_new) * EXP2_SCALE)
            pr = jnp.exp2((s - m_new) * EXP2_SCALE)
            acc = alpha * acc + jnp.dot(pr.astype(BF16), vaug_ref[pl.ds(off, bq), :], preferred_element_type=F32)
            return m_new, acc

        init = (jnp.full((bq, 1), NEG_BIG, F32), jnp.zeros((bq, 2 * V_DIM), F32))
        m, acc = _two_slot_pipeline(i, issue, consume, init)
        l = acc[:, V_DIM:V_DIM + 1]
        o_ref[...] = (acc[:, :V_DIM] / l).astype(BF16)
        lse_ref[0] = jnp.broadcast_to(m * ATTN_SCALE + jnp.log(l), (bq, LANES))

        if nw:
            @pl.when((h == N_HEADS - 1) & (i == nq - 1))
            def _():
                _gather_finish(buf_refs, pieces, *sems)

    out_shape = (SDS((t, N_HEADS * V_DIM), BF16), SDS((N_HEADS, t, LANES), F32)) + tuple(SDS(b.shape, b.dtype) for b in bufs)
    sem_shapes = _gather_sems(pieces) if nw else []
    outs = pl.pallas_call(
        body, out_shape=out_shape, grid=(N_HEADS, nq),
        in_specs=[pl.BlockSpec((bq, QK_PAD), lambda h, i: (i, h)), pl.BlockSpec((t, QK_PAD), lambda h, i: (0, h)),
                  pl.BlockSpec((t, V_DIM), lambda h, i: (0, N_HEADS + h))] + [ANY] * nw,
        out_specs=(pl.BlockSpec((bq, V_DIM), lambda h, i: (i, h)),
                   pl.BlockSpec((1, bq, LANES), lambda h, i: (h, i, 0))) + (ANY,) * nw,
        input_output_aliases={3 + n: 2 + n for n in range(nw)},
        scratch_shapes=[pltpu.VMEM((t, 2 * V_DIM), BF16), pltpu.VMEM((2, bq, bq), F32)] + sem_shapes,
        compiler_params=_cparams(("arbitrary", "arbitrary")), name="attn_fwd_gather" if nw else "attn_fwd")(qr, kf, kv, *bufs)
    return outs[0], outs[1], list(outs[2:])


def _attn_delta(do, o):
    t = do.shape[0]
    tm = _tile(t, 512, 8)

    def body(do_ref, o_ref, dl_ref):
        prod = do_ref[...].astype(F32) * o_ref[...].astype(F32)
        for h in range(N_HEADS):
            s = jnp.sum(prod[:, h * V_DIM:(h + 1) * V_DIM], axis=1, keepdims=True)
            dl_ref[h] = jnp.broadcast_to(s, (tm, LANES))

    return pl.pallas_call(
        body, out_shape=SDS((N_HEADS, t, LANES), F32), grid=(t // tm,),
        in_specs=[_rowspec(tm, N_HEADS * V_DIM), _rowspec(tm, N_HEADS * V_DIM)],
        out_specs=pl.BlockSpec((N_HEADS, tm, LANES), lambda i: (0, i, 0)), compiler_params=_cparams(("parallel",)),
        name="attn_delta")(do, o)


def _attn_bwd(qr, kf, kv, do, lse, delta, exchange=None):
    t = qr.shape[0]
    bq = _attn_block(t)
    nq = t // bq
    pbs, lands = exchange if exchange is not None else ((), ())
    nw = len(pbs)

    def body(*refs):
        k_ref, v_ref, q_ref, do_ref, lse_ref, dl_ref = refs[:6]
        pb_refs = refs[6:6 + nw]
        dk_ref, dv_ref, dq_ref = refs[6 + 2 * nw:9 + 2 * nw]
        land_refs = refs[9 + 2 * nw:9 + 3 * nw]
        s_ref, dp_ref = refs[9 + 3 * nw], refs[10 + 3 * nw]
        sems = refs[11 + 3 * nw:]
        h, j = pl.program_id(0), pl.program_id(1)

        if nw:
            @pl.when((h == 0) & (j == 0))
            def _():
                _exchange_start(pb_refs, land_refs, *sems)

        @pl.when(j == 0)
        def _():
            dq_ref[...] = jnp.zeros_like(dq_ref)

        k = k_ref[...]
        v = v_ref[...]

        def issue(b, slot):
            off = pl.multiple_of((nq - 1 - b) * bq, bq)
            s_ref[slot] = lax.dot_general(q_ref[pl.ds(off, bq), :], k, _NT, preferred_element_type=F32)
            dp_ref[slot] = lax.dot_general(do_ref[pl.ds(off, bq), :], v, _NT, preferred_element_type=F32)

        def consume(b, slot, carry, masked):
            dk, dv = carry
            off = pl.multiple_of((nq - 1 - b) * bq, bq)
            q = q_ref[pl.ds(off, bq), :]
            do_ = do_ref[pl.ds(off, bq), :]
            lse2 = lse_ref[0, pl.ds(off, bq), :][:, :1] * LOG2E
            dl_i = dl_ref[0, pl.ds(off, bq), :][:, :1]
            s = s_ref[slot]
            if masked:
                s = jnp.where(_chunk_mask(bq), s, NEG_BIG)
            pr = jnp.exp2(s * EXP2_SCALE - lse2)
            dv = dv + lax.dot_general(pr.astype(BF16), do_, _TN, preferred_element_type=F32)
            ds = (pr * (dp_ref[slot] - dl_i)).astype(BF16)
            dk = dk + lax.dot_general(ds, q, _TN, preferred_element_type=F32)
            dq_ref[pl.ds(off, bq), :] += jnp.dot(ds, k, preferred_element_type=F32) * ATTN_SCALE
            return dk, dv

        init = (jnp.zeros((bq, QK_PAD), F32), jnp.zeros((bq, V_DIM), F32))
        dk, dv = _two_slot_pipeline(nq - 1 - j, issue, consume, init)
        dk_ref[...] = (dk * ATTN_SCALE).astype(BF16)
        dv_ref[...] = dv.astype(BF16)

        if nw:
            @pl.when((h == N_HEADS - 1) & (j == nq - 1))
            def _():
                _exchange_finish(pb_refs, land_refs, *sems)

    stat = pl.BlockSpec((1, t, LANES), lambda h, j: (h, 0, 0))
    out_shape = (SDS((t, N_HEADS * QK_PAD), BF16), SDS((t, N_HEADS * V_DIM), BF16), SDS((t, N_HEADS * QK_PAD), F32))
    sem_shapes = [pltpu.SemaphoreType.DMA((3 * nw,)), pltpu.SemaphoreType.DMA((3 * nw,))] if nw else []
    outs = pl.pallas_call(
        body, out_shape=out_shape + tuple(SDS(l.shape, l.dtype) for l in lands), grid=(N_HEADS, nq),
        in_specs=[pl.BlockSpec((bq, QK_PAD), lambda h, j: (j, h)), pl.BlockSpec((bq, V_DIM), lambda h, j: (j, N_HEADS + h)),
                  pl.BlockSpec((t, QK_PAD), lambda h, j: (0, h)), pl.BlockSpec((t, V_DIM), lambda h, j: (0, h)), stat, stat]
        + [ANY] * (2 * nw),
        out_specs=(pl.BlockSpec((bq, QK_PAD), lambda h, j: (j, h)), pl.BlockSpec((bq, V_DIM), lambda h, j: (j, h)),
                   pl.BlockSpec((t, QK_PAD), lambda h, j: (0, h))) + (ANY,) * nw,
        input_output_aliases={6 + nw + n: 3 + n for n in range(nw)},
        scratch_shapes=[pltpu.VMEM((2, bq, bq), F32), pltpu.VMEM((2, bq, bq), F32)] + sem_shapes,
        compiler_params=_cparams(("arbitrary", "arbitrary")),
        name="attn_bwd_exchange" if nw else "attn_bwd")(kf, kv, qr, do, lse, delta, *pbs, *lands)
    return outs[0], outs[1], outs[2], list(outs[3:])


def _merge_fwd(proj, lay, ya, yb):
    t, d = ya.shape
    tm = _tile(t, 512, 8)

    def body(gc_ref, gm_ref, ya_ref, yb_ref, o_ref):
        o_ref[...] = (_sigmoid(gc_ref[...].astype(F32)) * ya_ref[...].astype(F32)
                      + _sigmoid(gm_ref[...].astype(F32)) * yb_ref[...].astype(F32)).astype(BF16)

    return pl.pallas_call(
        body, out_shape=SDS((t, d), BF16), grid=(t // tm,),
        in_specs=[_colspec(tm, d, lay["gc"]), _colspec(tm, d, lay["gm"]), _rowspec(tm, d), _rowspec(tm, d)],
        out_specs=_rowspec(tm, d), compiler_params=_cparams(("parallel",)), name="merge_fwd")(proj, proj, ya, yb)


def _merge_bwd(proj, lay, ya, yb, dmg):
    t, d = ya.shape
    tm = _tile(t, 512, 8)

    def body(gc_ref, gm_ref, ya_ref, yb_ref, d_ref, dya_ref, dyb_ref, dgc_ref, dgm_ref):
        dm = d_ref[...].astype(F32)
        sc = _sigmoid(gc_ref[...].astype(F32))
        sm = _sigmoid(gm_ref[...].astype(F32))
        dya_ref[...] = (dm * sc).astype(BF16)
        dyb_ref[...] = (dm * sm).astype(BF16)
        dgc_ref[...] = (dm * ya_ref[...].astype(F32) * (sc * (1.0 - sc))).astype(BF16)
        dgm_ref[...] = (dm * yb_ref[...].astype(F32) * (sm * (1.0 - sm))).astype(BF16)

    r = _rowspec(tm, d)
    return pl.pallas_call(
        body, out_shape=(SDS((t, d), BF16),) * 4, grid=(t // tm,),
        in_specs=[_colspec(tm, d, lay["gc"]), _colspec(tm, d, lay["gm"]), r, r, r], out_specs=(r, r, r, r),
        compiler_params=_cparams(("parallel",)), name="merge_bwd")(proj, proj, ya, yb, dmg)


def _ple_fwd(h, gp, pp, norm_gain=None):
    t, d = h.shape
    tm = _tile(t, 512, 8)
    fuse_norm = norm_gain is not None

    def body(h_ref, gp_ref, pp_ref, *rest):
        out = h_ref[...] + _sigmoid(gp_ref[...].astype(F32)) * pp_ref[...].astype(F32)
        rest[-2 if fuse_norm else -1][...] = out
        if fuse_norm:
            rest[-1][...] = _rn_fwd_math(out, rest[0][...]).astype(BF16)

    r = _rowspec(tm, d)
    if not fuse_norm:
        return pl.pallas_call(body, out_shape=SDS((t, d), F32), grid=(t // tm,), in_specs=[r, r, r], out_specs=r,
                              compiler_params=_cparams(("parallel",)), name="ple_fwd")(h, gp, pp), None
    return pl.pallas_call(
        body, out_shape=(SDS((t, d), F32), SDS((t, d), BF16)), grid=(t // tm,),
        in_specs=[r, r, r, pl.BlockSpec((1, d), lambda i: (0, 0))], out_specs=(r, r),
        compiler_params=_cparams(("parallel",)), name="ple_fwd_norm")(h, gp, pp, norm_gain.reshape(1, d))


def _ple_bwd(dh, gp, pp):
    t, d = dh.shape
    tm = _tile(t, 512, 8)

    def body(dh_ref, gp_ref, pp_ref, dpp_ref, dgp_ref):
        g = dh_ref[...]
        s = _sigmoid(gp_ref[...].astype(F32))
        dpp_ref[...] = (g * s).astype(BF16)
        dgp_ref[...] = (g * pp_ref[...].astype(F32) * (s * (1.0 - s))).astype(BF16)

    r = _rowspec(tm, d)
    return pl.pallas_call(body, out_shape=(SDS((t, d), BF16),) * 2, grid=(t // tm,), in_specs=[r, r, r],
                          out_specs=(r, r), compiler_params=_cparams(("parallel",)), name="ple_bwd")(dh, gp, pp)


def _adamw(w, g, m, v):
    shape = w.shape
    cols = shape[-1]
    rows = w.size // cols
    tr = _tile(rows, max(8, BLOCK_ELEMS // cols // 8 * 8), 8)

    def body(w_ref, g_ref, m_ref, v_ref, d_ref, nm_ref, nv_ref):
        g_ = g_ref[...]
        nm = ADAM_B1 * m_ref[...] + (1.0 - ADAM_B1) * g_
        nv = ADAM_B2 * v_ref[...] + (1.0 - ADAM_B2) * (g_ * g_)
        m_hat = nm / (1.0 - ADAM_B1 ** ADAM_STEP)
        v_hat = nv / (1.0 - ADAM_B2 ** ADAM_STEP)
        d_ref[...] = -ADAM_LR * (m_hat / (jnp.sqrt(v_hat) + ADAM_EPS) + ADAM_WD * w_ref[...])
        nm_ref[...] = nm
        nv_ref[...] = nv

    r = _rowspec(tr, cols)
    outs = pl.pallas_call(
        body, out_shape=(SDS((rows, cols), F32),) * 3, grid=(rows // tr,), in_specs=[r, r, r, r], out_specs=(r, r, r),
        compiler_params=_cparams(("parallel",)), name="adamw")(*(a.reshape(rows, cols) for a in (w, g, m, v)))
    return tuple(o.reshape(shape) for o in outs)


ANY = pl.BlockSpec(memory_space=pl.ANY)


def _place():
    x, y, c = lax.axis_index("x"), lax.axis_index("y"), lax.axis_index("c")
    return x, y, c, [(1 - x, y), (x, 1 - y), (1 - x, 1 - y)]


def _half_rows(rows, cols):
    half = rows // 2
    return half, _tile(half, max(16, BLOCK_ELEMS // cols // 16 * 16), 16)


def _my_chip():
    return 2 * lax.axis_index("x") + lax.axis_index("y")


def _cast_into_slot(w):
    nl, r, c = w.shape
    tr = _tile(r, max(16, BLOCK_ELEMS // c // 16 * 16), 16)

    def body(w_ref, o_ref):
        o_ref[...] = w_ref[...].astype(BF16)

    return pl.pallas_call(
        body, out_shape=SDS((nl, N_CHIPS, r, c), BF16), grid=(nl, r // tr),
        in_specs=[pl.BlockSpec((None, tr, c), lambda l, i: (l, i, 0))],
        out_specs=pl.BlockSpec((None, None, tr, c), lambda l, i: (l, _my_chip(), i, 0)),
        compiler_params=_cparams(("parallel", "parallel")), name="cast_into_slot")(w)


def _gather_copy(ref, layer, send_sems, recv_sems, sem, chip, half, to):
    r2 = ref.shape[2] // 2
    rows = ref.at[layer, chip, pl.ds(half * r2, r2)]
    return pltpu.make_async_remote_copy(src_ref=rows, dst_ref=rows, send_sem=send_sems.at[sem], recv_sem=recv_sems.at[sem],
                                        device_id=to, device_id_type=MESH)


def _gather_start(refs, pieces, send_sems, recv_sems):
    x, y, c, chips = _place()
    for pi, (ri, layer) in enumerate(pieces):
        for n, chip in enumerate(chips):
            _gather_copy(refs[ri], layer, send_sems, recv_sems, 6 * pi + n, 2 * x + y, c, (*chip, c)).start()


def _gather_finish(refs, pieces, send_sems, recv_sems):
    x, y, c, chips = _place()
    me, sibling = (x, y, c), (x, y, 1 - c)
    for pi, (ri, layer) in enumerate(pieces):
        for n, chip in enumerate(chips):
            k = 2 * chip[0] + chip[1]
            _gather_copy(refs[ri], layer, send_sems, recv_sems, 6 * pi + n, k, c, me).wait_recv()
            _gather_copy(refs[ri], layer, send_sems, recv_sems, 6 * pi + 3 + n, k, c, sibling).start()
    for pi, (ri, layer) in enumerate(pieces):
        for n, chip in enumerate(chips):
            _gather_copy(refs[ri], layer, send_sems, recv_sems, 6 * pi + 3 + n, 2 * chip[0] + chip[1], 1 - c, me).wait_recv()
    for pi, (ri, layer) in enumerate(pieces):
        for n, chip in enumerate(chips):
            _gather_copy(refs[ri], layer, send_sems, recv_sems, 6 * pi + n, 2 * x + y, c, (*chip, c)).wait_send()
            _gather_copy(refs[ri], layer, send_sems, recv_sems, 6 * pi + 3 + n, 2 * chip[0] + chip[1], c, sibling).wait_send()


def _gather_sems(pieces):
    return [pltpu.SemaphoreType.DMA((6 * len(pieces),)), pltpu.SemaphoreType.DMA((6 * len(pieces),))]


def _all_gather_weights(bufs, pieces):
    nw = len(bufs)

    def body(*refs):
        _gather_start(refs[nw:2 * nw], pieces, refs[2 * nw], refs[2 * nw + 1])
        _gather_finish(refs[nw:2 * nw], pieces, refs[2 * nw], refs[2 * nw + 1])

    return list(pl.pallas_call(
        body, out_shape=tuple(SDS(b.shape, b.dtype) for b in bufs), in_specs=[ANY] * nw, out_specs=(ANY,) * nw,
        input_output_aliases={i: i for i in range(nw)}, scratch_shapes=_gather_sems(pieces),
        name="all_gather_weights")(*bufs))


def _pair_swap(grads):
    nw = len(grads)

    def body(*refs):
        ins, outs = refs[:nw], refs[nw:2 * nw]
        send_sems, recv_sems = refs[2 * nw], refs[2 * nw + 1]
        x, y, c, _ = _place()
        copies = []
        for wi, (g_ref, o_ref) in enumerate(zip(ins, outs)):
            r2 = g_ref.shape[1] // 2
            copies.append(pltpu.make_async_remote_copy(
                src_ref=g_ref.at[:, pl.ds((1 - c) * r2, r2)], dst_ref=o_ref, send_sem=send_sems.at[wi],
                recv_sem=recv_sems.at[wi], device_id=(x, y, 1 - c), device_id_type=MESH))
            copies[-1].start()
        for cp in copies:
            cp.wait()

    return pl.pallas_call(
        body, out_shape=tuple(SDS((N_CHIPS, g.shape[1] // 2, g.shape[2]), g.dtype) for g in grads),
        in_specs=[ANY] * nw, out_specs=(ANY,) * nw,
        scratch_shapes=[pltpu.SemaphoreType.DMA((nw,)), pltpu.SemaphoreType.DMA((nw,))], name="pair_swap")(*grads)


def _pair_add(g, other):
    _, r, c = g.shape
    r2, tr = _half_rows(r, c)
    nb = r2 // tr

    def body(g_ref, o_ref, pb_ref, land_ref):
        total = (g_ref[...].astype(F32) + o_ref[...].astype(F32)).astype(BF16)
        pb_ref[...] = total

        @pl.when(pl.program_id(1) == _my_chip())
        def _():
            land_ref[...] = total

    blk = pl.BlockSpec((None, tr, c), lambda j, k: (k, j, 0))
    return pl.pallas_call(
        body, out_shape=(SDS((N_CHIPS, r2, c), BF16),) * 2, grid=(nb, N_CHIPS),
        in_specs=[pl.BlockSpec((None, tr, c), lambda j, k: (k, lax.axis_index("c") * nb + j, 0)), blk],
        out_specs=(blk, pl.BlockSpec((None, tr, c), lambda j, k: (_my_chip(), j, 0))),
        compiler_params=_cparams(("parallel", "arbitrary")), name="pair_add")(g, other)


def _exchange_copy(p_ref, l_ref, send_sems, recv_sems, sem, src_slot, dst_slot, to):
    return pltpu.make_async_remote_copy(src_ref=p_ref.at[src_slot], dst_ref=l_ref.at[dst_slot], send_sem=send_sems.at[sem],
                                        recv_sem=recv_sems.at[sem], device_id=to, device_id_type=MESH)


def _exchange_start(p_refs, l_refs, send_sems, recv_sems):
    x, y, c, chips = _place()
    for wi, (p_ref, l_ref) in enumerate(zip(p_refs, l_refs)):
        for n, chip in enumerate(chips):
            _exchange_copy(p_ref, l_ref, send_sems, recv_sems, 3 * wi + n, 2 * chip[0] + chip[1], 2 * x + y, (*chip, c)).start()


def _exchange_finish(p_refs, l_refs, send_sems, recv_sems):
    x, y, c, chips = _place()
    for wi, (p_ref, l_ref) in enumerate(zip(p_refs, l_refs)):
        for n, chip in enumerate(chips):
            _exchange_copy(p_ref, l_ref, send_sems, recv_sems, 3 * wi + n, 2 * x + y, 2 * chip[0] + chip[1], (x, y, c)).wait_recv()
    for wi, (p_ref, l_ref) in enumerate(zip(p_refs, l_refs)):
        for n, chip in enumerate(chips):
            _exchange_copy(p_ref, l_ref, send_sems, recv_sems, 3 * wi + n, 2 * chip[0] + chip[1], 2 * x + y, (*chip, c)).wait_send()


def _chip_all_to_all(pbs, lands):
    nw = len(pbs)

    def body(*refs):
        _exchange_start(refs[:nw], refs[2 * nw:3 * nw], refs[3 * nw], refs[3 * nw + 1])
        _exchange_finish(refs[:nw], refs[2 * nw:3 * nw], refs[3 * nw], refs[3 * nw + 1])

    return list(pl.pallas_call(
        body, out_shape=tuple(SDS(l.shape, l.dtype) for l in lands), in_specs=[ANY] * (2 * nw), out_specs=(ANY,) * nw,
        input_output_aliases={nw + i: i for i in range(nw)},
        scratch_shapes=[pltpu.SemaphoreType.DMA((3 * nw,)), pltpu.SemaphoreType.DMA((3 * nw,))],
        name="chip_all_to_all")(*pbs, *lands))


def _sum_chips(land, gstack, layer):
    _, r, c = gstack.shape
    r2, tr = _half_rows(r, c)
    nb = r2 // tr

    def body(l_ref, g_ref, out_ref):
        out_ref[...] = ((l_ref[0].astype(F32) + l_ref[1].astype(F32)) + l_ref[2].astype(F32)) + l_ref[3].astype(F32)

    return pl.pallas_call(
        body, out_shape=SDS(gstack.shape, F32), grid=(nb,),
        in_specs=[pl.BlockSpec((N_CHIPS, tr, c), lambda j: (0, j, 0)), ANY],
        out_specs=pl.BlockSpec((None, tr, c), lambda j: (layer, lax.axis_index("c") * nb + j, 0)),
        input_output_aliases={1: 0}, compiler_params=_cparams(("parallel",)), name="sum_chips")(land, gstack)


def _pair_gather(gstacks, layer):
    nw = len(gstacks)

    def body(*refs):
        outs = refs[nw:2 * nw]
        send_sems, recv_sems = refs[2 * nw], refs[2 * nw + 1]
        x, y, c, _ = _place()

        def copy(ref, wi, half):
            r2 = ref.shape[1] // 2
            blk = ref.at[layer, pl.ds(half * r2, r2)]
            return pltpu.make_async_remote_copy(src_ref=blk, dst_ref=blk, send_sem=send_sems.at[wi],
                                                recv_sem=recv_sems.at[wi], device_id=(x, y, 1 - c), device_id_type=MESH)

        sent = [copy(ref, wi, c) for wi, ref in enumerate(outs)]
        for cp in sent:
            cp.start()
        for wi, ref in enumerate(outs):
            copy(ref, wi, 1 - c).wait_recv()
        for cp in sent:
            cp.wait_send()

    return pl.pallas_call(
        body, out_shape=tuple(SDS(g.shape, g.dtype) for g in gstacks), in_specs=[ANY] * nw, out_specs=(ANY,) * nw,
        input_output_aliases={i: i for i in range(nw)},
        scratch_shapes=[pltpu.SemaphoreType.DMA((nw,)), pltpu.SemaphoreType.DMA((nw,))], name="pair_gather")(*gstacks)


class _Rider:
    def __init__(self, tag, operands, out_shapes, aliases, sem_shapes, start, finish):
        self.tag, self.operands, self.out_shapes, self.aliases = tag, list(operands), list(out_shapes), dict(aliases)
        self.sem_shapes, self._start, self._finish = list(sem_shapes), start, finish
        self._refs = None

    def split(self, refs, n_outs, n_scratch, grid):
        n_in, n_out = len(self.operands), len(self.out_shapes)
        ins = refs[:n_in]
        own_outs = refs[n_in:n_in + n_outs]
        outs = refs[n_in + n_outs:n_in + n_outs + n_out]
        own_scratch = refs[n_in + n_outs + n_out:n_in + n_outs + n_out + n_scratch]
        sems = refs[n_in + n_outs + n_out + n_scratch:]
        self._refs = (ins, outs, sems)
        first = functools.reduce(lambda p, q: p & q, [pl.program_id(ax) == 0 for ax in range(len(grid))])

        @pl.when(first)
        def _():
            self._start(ins, outs, sems)

        return tuple(own_outs) + tuple(own_scratch)

    def finish_at_last_step(self, grid):
        ins, outs, sems = self._refs
        last = functools.reduce(lambda p, q: p & q, [pl.program_id(ax) == g - 1 for ax, g in enumerate(grid)])

        @pl.when(last)
        def _():
            self._finish(ins, outs, sems)


def _gather_rider(bufs, pieces):
    nw = len(bufs)
    return _Rider("_gather", bufs, [SDS(b.shape, b.dtype) for b in bufs], {i: i for i in range(nw)}, _gather_sems(pieces),
                  lambda ins, outs, sems: _gather_start(outs, pieces, *sems),
                  lambda ins, outs, sems: _gather_finish(outs, pieces, *sems))


def _pair_swap_copies(ins, outs, sems):
    x, y, c, _ = _place()
    copies = []
    for wi, (g_ref, o_ref) in enumerate(zip(ins, outs)):
        r2 = g_ref.shape[1] // 2
        copies.append(pltpu.make_async_remote_copy(
            src_ref=g_ref.at[:, pl.ds((1 - c) * r2, r2)], dst_ref=o_ref, send_sem=sems[0].at[wi],
            recv_sem=sems[1].at[wi], device_id=(x, y, 1 - c), device_id_type=MESH))
    return copies


def _pair_swap_rider(grads):
    nw = len(grads)

    def start(ins, outs, sems):
        for cp in _pair_swap_copies(ins, outs, sems):
            cp.start()

    def finish(ins, outs, sems):
        for cp in _pair_swap_copies(ins, outs, sems):
            cp.wait()

    return _Rider("_swap", grads, [SDS((N_CHIPS, g.shape[1] // 2, g.shape[2]), g.dtype) for g in grads], {},
                  [pltpu.SemaphoreType.DMA((nw,)), pltpu.SemaphoreType.DMA((nw,))], start, finish)


def _pair_gather_copy(ref, layer, sems, wi, half, peer):
    r2 = ref.shape[1] // 2
    blk = ref.at[layer, pl.ds(half * r2, r2)]
    return pltpu.make_async_remote_copy(src_ref=blk, dst_ref=blk, send_sem=sems[0].at[wi], recv_sem=sems[1].at[wi],
                                        device_id=peer, device_id_type=MESH)


def _pair_gather_rider(gstacks, layer):
    nw = len(gstacks)

    def start(ins, outs, sems):
        x, y, c, _ = _place()
        for wi, ref in enumerate(outs):
            _pair_gather_copy(ref, layer, sems, wi, c, (x, y, 1 - c)).start()

    def finish(ins, outs, sems):
        x, y, c, _ = _place()
        for wi, ref in enumerate(outs):
            _pair_gather_copy(ref, layer, sems, wi, 1 - c, (x, y, 1 - c)).wait_recv()
        for wi, ref in enumerate(outs):
            _pair_gather_copy(ref, layer, sems, wi, c, (x, y, 1 - c)).wait_send()

    return _Rider("_pair_gather", gstacks, [SDS(g.shape, g.dtype) for g in gstacks], {i: i for i in range(nw)},
                  [pltpu.SemaphoreType.DMA((nw,)), pltpu.SemaphoreType.DMA((nw,))], start, finish)


def _all_gather_small(vec, name):
    rows, w = vec.shape

    def body(v_ref, sum_ref, all_ref, send_sems, recv_sems):
        x, y, c, chips = _place()
        me, sibling = (x, y, c), (x, y, 1 - c)

        def slot(px, py, pc):
            return all_ref.at[4 * px + 2 * py + pc]

        def copy(k, block, to, src=None):
            return pltpu.make_async_remote_copy(
                src_ref=slot(*block) if src is None else src, dst_ref=slot(*block), send_sem=send_sems.at[k],
                recv_sem=recv_sems.at[k], device_id=to, device_id_type=MESH)

        first = [copy(0, me, sibling, src=v_ref)]
        first += [copy(1 + n, me, (*chip, c), src=v_ref) for n, chip in enumerate(chips)]
        for cp in first:
            cp.start()
        slot(*me)[...] = v_ref[...]
        passed = [copy(4 + n, (*chip, c), sibling) for n, chip in enumerate(chips)]
        for n, chip in enumerate(chips):
            copy(1 + n, (*chip, c), me).wait_recv()
            passed[n].start()
        copy(0, sibling, me).wait_recv()
        for n, chip in enumerate(chips):
            copy(4 + n, (*chip, 1 - c), me).wait_recv()
        for cp in first + passed:
            cp.wait_send()
        total = all_ref[0]
        for dev in range(1, 8):
            total = total + all_ref[dev]
        sum_ref[...] = total

    vm = pl.BlockSpec(memory_space=pltpu.VMEM)
    return pl.pallas_call(
        body, out_shape=(SDS((rows, w), F32), SDS((8, rows, w), F32)), in_specs=[vm], out_specs=(vm, vm),
        scratch_shapes=[pltpu.SemaphoreType.DMA((7,)), pltpu.SemaphoreType.DMA((7,))], name=name)(vec)


def _to_rows128(flat):
    n = flat.shape[0]
    rows = -(-n // (8 * LANES)) * 8
    return jnp.pad(flat, (0, rows * LANES - n)).reshape(rows, LANES)


def _in_layout(conv, ql, kvl, d):
    lay = {"conv": conv, "ql": ql, "kvl": kvl, "d": d}
    lay["q"] = 3 * conv
    lay["kr"] = lay["q"] + ql
    lay["gc"] = lay["kr"] + LANES
    lay["gm"] = lay["gc"] + d
    lay["kv"] = lay["gm"] + d
    used = lay["kv"] + kvl
    lay["width"] = -(-used // 512) * 512
    return lay


def _w_in_to_layout(w, lay):
    conv, ql, kvl, d = lay["conv"], lay["ql"], lay["kvl"], lay["d"]
    o_kv = 3 * conv + ql
    o_kr = o_kv + kvl
    o_g = o_kr + ROPE_DIM
    lead = w.shape[:-1]
    parts = [w[..., :o_kv], w[..., o_kr:o_g], jnp.zeros(lead + (LANES - ROPE_DIM,), w.dtype), w[..., o_g:o_g + 2 * d],
             w[..., o_kv:o_kr], jnp.zeros(lead + (lay["width"] - lay["kv"] - kvl,), w.dtype)]
    return jnp.concatenate(parts, axis=-1)


def _w_in_from_layout(g, lay):
    ql, kvl, d = lay["ql"], lay["kvl"], lay["d"]
    return jnp.concatenate([g[:, :lay["q"] + ql], g[:, lay["kv"]:lay["kv"] + kvl], g[:, lay["kr"]:lay["kr"] + ROPE_DIM],
                            g[:, lay["gc"]:lay["gc"] + 2 * d]], axis=1)


def _w_uq_to_layout(w):
    r = w.shape[0]
    w3 = w.reshape(r, N_HEADS, NOPE_DIM + ROPE_DIM)
    return jnp.pad(w3, ((0, 0), (0, 0), (0, QK_PAD - NOPE_DIM - ROPE_DIM))).reshape(r, N_HEADS * QK_PAD)


def _w_uq_from_layout(g):
    r = g.shape[0]
    return g.reshape(r, N_HEADS, QK_PAD)[:, :, :NOPE_DIM + ROPE_DIM].reshape(r, N_HEADS * (NOPE_DIM + ROPE_DIM))


def _w_ukv_to_layout(w):
    r = w.shape[0]
    return w.reshape(r, N_HEADS, 2, NOPE_DIM).transpose(0, 2, 1, 3).reshape(r, 2 * N_HEADS * NOPE_DIM)


def _w_ukv_from_layout(g):
    r = g.shape[0]
    return g.reshape(r, 2, N_HEADS, NOPE_DIM).transpose(0, 2, 1, 3).reshape(r, 2 * N_HEADS * NOPE_DIM)


def _chips_to_cols(buf):
    _, r, c = buf.shape
    return buf.transpose(1, 0, 2).reshape(r, N_CHIPS * c)


def _cols_to_chips(g):
    r, c4 = g.shape
    return g.reshape(r, N_CHIPS, c4 // N_CHIPS).transpose(1, 0, 2)


BEFORE_FFN1_GU = ("ffn1_w_gu",)
BEFORE_FFN1_DOWN = ("ffn1_w_down", "w_uq", "w_ukv")
IN_ATTENTION = ("w_in", "w_conv_out", "ffn2_w_gu", "ffn2_w_down", "w_mla_out", "w_o", "w_ple_gate", "w_ple_proj")
LAYER0_FIRST = ("ffn1_w_gu", "ffn1_w_down")
LAYER0_IN_FFN1_GU = ("w_in", "w_conv_out", "w_uq", "w_ukv")
LAYER0_IN_FFN1_DOWN = ("ffn2_w_down", "w_mla_out")
LAYER0_IN_PROJ = ("ffn2_w_gu",)
LAYER0_IN_ATTENTION = ("w_o", "w_ple_gate", "w_ple_proj")


def _gather_plan(layer, depth):
    plan = {}
    if layer == 0:
        plan["ffn1_gu"] = [(k, 0) for k in LAYER0_IN_FFN1_GU]
        plan["ffn1_down"] = [(k, 0) for k in LAYER0_IN_FFN1_DOWN]
        plan["in"] = [(k, 0) for k in LAYER0_IN_PROJ]
        plan["attn"] = [(k, 0) for k in LAYER0_IN_ATTENTION]
    if layer + 1 < depth:
        plan["attn"] = plan.get("attn", []) + [(k, layer + 1) for k in IN_ATTENTION]
        plan["ffn2_gu"] = [(k, layer + 1) for k in BEFORE_FFN1_GU]
        plan["ffn2_down"] = [(k, layer + 1) for k in BEFORE_FFN1_DOWN]
    return plan


def _plan_operands(bufs, todo):
    names = list(dict.fromkeys(name for name, _ in todo))
    return [bufs[k] for k in names], names, [(names.index(name), layer) for name, layer in todo]


def _rows_view(buf):
    return buf.reshape(buf.shape[0], N_CHIPS * buf.shape[2], buf.shape[3])


def _cols_view(buf, layer):
    return _chips_to_cols(buf[layer])


def _ffn_fwd(h, n, bufs, which, layer, next_gain, plan):
    tag = which
    rider = None
    if plan.get(which + "_gu"):
        ops, names, pieces = _plan_operands(bufs, plan[which + "_gu"])
        rider = _gather_rider(ops, pieces)
    outs = _gu_swiglu_fwd(n, bufs[which + "_w_gu"], layer, tag + "_gu_fwd", rider=rider)
    gu, a = outs[0], outs[1]
    if rider is not None:
        bufs = {**bufs, **dict(zip(names, outs[2]))}
    rider = None
    if plan.get(which + "_down"):
        ops, names, pieces = _plan_operands(bufs, plan[which + "_down"])
        rider = _gather_rider(ops, pieces)
    outs = _mm(a, _rows_view(bufs[which + "_w_down"]), "nn", F32, tag + "_down_fwd", scale=0.5, res=h, layer=layer,
               norm_gain=next_gain, rider=rider)
    if rider is not None:
        bufs = {**bufs, **dict(zip(names, outs[2]))}
    return outs[0], outs[1], (h, n, gu, a), bufs


def _ffn_bwd(dh, dhb, saved, gain, bufs, which, layer, swap=None):
    tag = which
    h, n, gu, a = saved
    d_wdown = _mm(a, dhb, "tn", BF16, tag + "_down_dw", scale=0.5)
    dgu = _down_dx_swiglu_bwd(dhb, _rows_view(bufs[which + "_w_down"]), gu, layer, tag + "_down_dx")
    rider = _pair_swap_rider(swap) if swap is not None else None
    outs = _mm(n, dgu, "tn", BF16, tag + "_gu_dw", out_chip=True, b_halves=True, rider=rider)
    d_wgu, swapped = (outs[0], outs[1]) if rider is not None else (outs, None)
    dh, dhb, dgain = _mm(dgu, bufs[which + "_w_gu"], "nt", BF16, tag + "_gu_dx", layer=layer, b_chip=True, a_halves=True,
                         norm_bwd=(h, gain, dh))
    return dh, dhb, d_wgu, d_wdown, dgain, swapped


def _layer_fwd(h0, n0, p_i, bufs, conv_w, norms, layer, lay, tables, depth, next_gain):
    plan = _gather_plan(layer, depth)
    h1, n2, s_ffn1, bufs = _ffn_fwd(h0, n0, bufs, "ffn1", layer, norms["mix_norm"], plan)
    w_in = _w_in_to_layout(_cols_view(bufs["w_in"], layer), lay)
    w_conv_out = _cols_view(bufs["w_conv_out"], layer)
    w_uq = _w_uq_to_layout(_cols_view(bufs["w_uq"], layer))
    w_ukv = _w_ukv_to_layout(_cols_view(bufs["w_ukv"], layer))
    if plan.get("in"):
        ops, names, pieces = _plan_operands(bufs, plan["in"])
        proj, gathered = _mm(n2, w_in, "nn", BF16, "in_fwd", rider=_gather_rider(ops, pieces))
        bufs = {**bufs, **dict(zip(names, gathered))}
    else:
        proj = _mm(n2, w_in, "nn", BF16, "in_fwd")
    cb = _conv_fwd(proj, conv_w)
    ya = _mm(cb, w_conv_out, "nn", BF16, "conv_out_fwd")
    qn, kvn = _qkvnorm_fwd(proj, lay, norms["q_norm"], norms["kv_norm"])
    qf = _mm(qn, w_uq, "nn", BF16, "uq_fwd")
    kv = _mm(kvn, w_ukv, "nn", BF16, "ukv_fwd")
    qr, kf = _rope_fwd(qf, kv, proj, lay, tables)
    if plan.get("attn"):
        ops, names, pieces = _plan_operands(bufs, plan["attn"])
        o, lse, gathered = _attn_fwd(qr, kf, kv, gather=(ops, pieces))
        bufs = {**bufs, **dict(zip(names, gathered))}
    else:
        o, lse, _ = _attn_fwd(qr, kf, kv)
    yb = _mm(o, _rows_view(bufs["w_mla_out"]), "nn", BF16, "mla_out_fwd", layer=layer)
    mg = _merge_fwd(proj, lay, ya, yb)
    h2, n3 = _mm(mg, _rows_view(bufs["w_o"]), "nn", F32, "o_fwd", res=h1, layer=layer, norm_gain=norms["ffn2_norm"])
    h3, n4, s_ffn2, bufs = _ffn_fwd(h2, n3, bufs, "ffn2", layer, norms["ple_norm"], plan)
    w_ple_proj = _cols_view(bufs["w_ple_proj"], layer)
    gp = _mm(n4, _rows_view(bufs["w_ple_gate"]), "nn", BF16, "ple_gate_fwd", layer=layer)
    pp = _mm(p_i, w_ple_proj, "nn", BF16, "ple_proj_fwd")
    h4, n_out = _ple_fwd(h3, gp, pp, next_gain)
    saved = dict(s_ffn1=s_ffn1, h1=h1, n2=n2, proj=proj, cb=cb, ya=ya, qn=qn, kvn=kvn, qr=qr, kf=kf, kv=kv, o=o,
                 lse=lse, yb=yb, mg=mg, h2=h2, s_ffn2=s_ffn2, h3=h3, n4=n4, gp=gp, pp=pp, p=p_i,
                 w_in=w_in, w_conv_out=w_conv_out, w_uq=w_uq, w_ukv=w_ukv)
    return h4, n_out, saved, bufs


def _layer_bwd(dh, s, bufs, conv_w, norms, layer, lay, tables, above, gstacks):
    gw, gn = {}, {}

    def by_rows(g):
        return g.reshape(N_CHIPS, g.shape[0] // N_CHIPS, g.shape[1])

    dpp, dgp = _ple_bwd(dh, s["gp"], s["pp"])
    gw["w_ple_proj"] = _cols_to_chips(_mm(s["p"], dpp, "tn", BF16, "ple_proj_dw"))
    gw["w_ple_gate"] = by_rows(_mm(s["n4"], dgp, "tn", BF16, "ple_gate_dw"))
    dh, dhb, gn["ple_norm"] = _mm(dgp, _rows_view(bufs["w_ple_gate"]), "nt", BF16, "ple_gate_dx", layer=layer,
                                  norm_bwd=(s["h3"], norms["ple_norm"], dh))
    dh, dhb, gw["ffn2_w_gu"], g_down, gn["ffn2_norm"], swapped = _ffn_bwd(
        dh, dhb, s["s_ffn2"], norms["ffn2_norm"], bufs, "ffn2", layer, swap=None if above is None else above[1])
    gw["ffn2_w_down"] = by_rows(g_down)
    exchange = None
    if above is not None:
        pairs = [_pair_add(g, o) for g, o in zip(above[1], swapped)]
        exchange = ([pb for pb, _ in pairs], [land for _, land in pairs])
    gw["w_o"] = by_rows(_mm(s["mg"], dhb, "tn", BF16, "o_dw"))
    dmg = _mm(dhb, _rows_view(bufs["w_o"]), "nt", BF16, "o_dx", layer=layer)
    dya, dyb, dgc, dgm = _merge_bwd(s["proj"], lay, s["ya"], s["yb"], dmg)
    gw["w_conv_out"] = _cols_to_chips(_mm(s["cb"], dya, "tn", BF16, "conv_out_dw"))
    dcb = _mm(dya, s["w_conv_out"], "nt", BF16, "conv_out_dx")
    db, dc, dv_conv, g_conv = _conv_bwd(s["proj"], conv_w, dcb)
    gw["w_mla_out"] = by_rows(_mm(s["o"], dyb, "tn", BF16, "mla_out_dw"))
    do = _mm(dyb, _rows_view(bufs["w_mla_out"]), "nt", BF16, "mla_out_dx", layer=layer)
    delta = _attn_delta(do, s["o"])
    dkf, dv, dqr, landed = _attn_bwd(s["qr"], s["kf"], s["kv"], do, s["lse"], delta, exchange=exchange)
    dqf, dkv, dkr = _rope_bwd(dqr, dkf, dv, tables)
    gw["w_uq"] = _cols_to_chips(_w_uq_from_layout(_mm(s["qn"], dqf, "tn", BF16, "uq_dw")))
    dqn = _mm(dqf, s["w_uq"], "nt", BF16, "uq_dx")
    gw["w_ukv"] = _cols_to_chips(_w_ukv_from_layout(_mm(s["kvn"], dkv, "tn", BF16, "ukv_dw")))
    dkvn = _mm(dkv, s["w_ukv"], "nt", BF16, "ukv_dx")
    dqc, dkvc, gn["q_norm"], gn["kv_norm"] = _qkvnorm_bwd(s["proj"], lay, norms["q_norm"], norms["kv_norm"], dqn, dkvn)
    t = dh.shape[0]
    dproj = jnp.concatenate([db, dc, dv_conv, dqc, dkr, dgc, dgm, dkvc,
                             jnp.zeros((t, lay["width"] - lay["kv"] - lay["kvl"]), BF16)], axis=1)
    rider = None
    if above is not None:
        gstacks = [_sum_chips(land, gs, above[0]) for land, gs in zip(landed, gstacks)]
        rider = _pair_gather_rider(gstacks, above[0])
    outs = _mm(s["n2"], dproj, "tn", BF16, "in_dw", rider=rider)
    g_in, gstacks = (outs[0], outs[1]) if rider is not None else (outs, gstacks)
    gw["w_in"] = _cols_to_chips(_w_in_from_layout(g_in, lay))
    dh, dhb, gn["mix_norm"] = _mm(dproj, s["w_in"], "nt", BF16, "in_dx", norm_bwd=(s["h1"], norms["mix_norm"], dh))
    dh, dhb, gw["ffn1_w_gu"], g_down, gn["ffn1_norm"], _ = _ffn_bwd(
        dh, dhb, s["s_ffn1"], norms["ffn1_norm"], bufs, "ffn1", layer)
    gw["ffn1_w_down"] = by_rows(g_down)
    return dh, gw, g_conv, gn, gstacks


def _rope_tables(positions):
    half = ROPE_DIM // 2
    inv_freq = ROPE_THETA ** (-jnp.arange(0, ROPE_DIM, 2, dtype=F32) / ROPE_DIM)
    ang = positions.astype(F32)[:, None] * inv_freq
    cos, sin = jnp.cos(ang), jnp.sin(ang)
    zeros = jnp.zeros_like(cos)
    cos_t = jnp.concatenate([cos, cos, zeros, zeros], axis=1)
    sin_a = jnp.concatenate([-sin, zeros, zeros, zeros], axis=1)
    sin_b = jnp.concatenate([zeros, sin, zeros, zeros], axis=1)
    assert cos_t.shape[1] == LANES and half * 4 == LANES
    return cos_t, sin_a, sin_b


def kernel(x, p, positions, ffn1_norm, ffn1_w_gu, ffn1_w_down, mix_norm, w_in, conv_w, w_conv_out, q_norm, kv_norm, w_uq, w_ukv, w_mla_out, w_o, ffn2_norm, ffn2_w_gu, ffn2_w_down, ple_norm, w_ple_gate, w_ple_proj, final_norm, loss_target, m_ffn1_norm, m_ffn1_w_gu, m_ffn1_w_down, m_mix_norm, m_w_in, m_conv_w, m_w_conv_out, m_q_norm, m_kv_norm, m_w_uq, m_w_ukv, m_w_mla_out, m_w_o, m_ffn2_norm, m_ffn2_w_gu, m_ffn2_w_down, m_ple_norm, m_w_ple_gate, m_w_ple_proj, m_final_norm, v_ffn1_norm, v_ffn1_w_gu, v_ffn1_w_down, v_mix_norm, v_w_in, v_conv_w, v_w_conv_out, v_q_norm, v_kv_norm, v_w_uq, v_w_ukv, v_w_mla_out, v_w_o, v_ffn2_norm, v_ffn2_w_gu, v_ffn2_w_down, v_ple_norm, v_w_ple_gate, v_w_ple_proj, v_final_norm):
    args = dict(zip(ARG_NAMES, (x, p, positions, ffn1_norm, ffn1_w_gu, ffn1_w_down, mix_norm, w_in, conv_w, w_conv_out, q_norm, kv_norm, w_uq, w_ukv, w_mla_out, w_o, ffn2_norm, ffn2_w_gu, ffn2_w_down, ple_norm, w_ple_gate, w_ple_proj, final_norm, loss_target, m_ffn1_norm, m_ffn1_w_gu, m_ffn1_w_down, m_mix_norm, m_w_in, m_conv_w, m_w_conv_out, m_q_norm, m_kv_norm, m_w_uq, m_w_ukv, m_w_mla_out, m_w_o, m_ffn2_norm, m_ffn2_w_gu, m_ffn2_w_down, m_ple_norm, m_w_ple_gate, m_w_ple_proj, m_final_norm, v_ffn1_norm, v_ffn1_w_gu, v_ffn1_w_down, v_mix_norm, v_w_in, v_conv_w, v_w_conv_out, v_q_norm, v_kv_norm, v_w_uq, v_w_ukv, v_w_mla_out, v_w_o, v_ffn2_norm, v_ffn2_w_gu, v_ffn2_w_down, v_ple_norm, v_w_ple_gate, v_w_ple_proj, v_final_norm)))
    depth = ffn1_norm.shape[0]
    t, d = x.shape[1], x.shape[2]
    conv = conv_w.shape[-1] * N_CHIPS
    lay = _in_layout(conv, q_norm.shape[-1], kv_norm.shape[-1], d)
    chip = 2 * lax.axis_index("x") + lax.axis_index("y")
    tables = _rope_tables(positions[0])

    bufs = {name: _cast_into_slot(args[name]) for name in BIG}
    bufs.update(zip(LAYER0_FIRST, _all_gather_weights([bufs[k] for k in LAYER0_FIRST], [(i, 0) for i in range(len(LAYER0_FIRST))])))
    conv_rows = depth * conv_w.shape[1]
    conv_all = _all_gather_small(_to_rows128(conv_w.reshape(-1)), "all_gather_conv_w")[1]
    conv_full = conv_all[0::2, :conv_rows].reshape(N_CHIPS, depth, conv_w.shape[1], LANES)
    conv_full = conv_full.transpose(1, 2, 0, 3).reshape(depth, conv_w.shape[1], conv)
    norms = [{name: args[name][i] for name in REPLICATED} for i in range(depth)]
    p3 = p.reshape(depth, t, p.shape[-1])

    h = x[0]
    n = _rmsnorm_fwd(h, norms[0]["ffn1_norm"], "first_norm_fwd")
    saved = []
    for i in range(depth):
        h, n, s, bufs = _layer_fwd(h, n, p3[i], bufs, conv_full[i], norms[i], i, lay, tables, depth,
                                   norms[i + 1]["ffn1_norm"] if i + 1 < depth else None)
        saved.append(s)
    loss_part, dh, _, g_final = _loss_head(h, final_norm, loss_target[0])
    loss = lax.psum(loss_part[0, 0], ("x", "y", "c"))

    gstacks = [lax.empty(args[name].shape, F32) for name in BIG]
    norm_grads, conv_grads = [None] * depth, [None] * depth
    above = None
    for i in reversed(range(depth)):
        dh, gw, conv_grads[i], norm_grads[i], gstacks = _layer_bwd(
            dh, saved[i], bufs, conv_full[i], norms[i], i, lay, tables, above, gstacks)
        above = (i, [gw[name] for name in BIG])
    pairs = [_pair_add(g, o) for g, o in zip(above[1], _pair_swap(above[1]))]
    landed = _chip_all_to_all([pb for pb, _ in pairs], [land for _, land in pairs])
    gstacks = _pair_gather([_sum_chips(land, gs, above[0]) for land, gs in zip(landed, gstacks)], above[0])
    grad_x = dh[None]
    grads = dict(zip(BIG, gstacks))

    pieces = [norm_grads[i][name].reshape(-1) for i in range(depth) for name in REPLICATED]
    pieces += [g_final.reshape(-1)] + [conv_grads[i].reshape(-1) for i in range(depth)]
    vec = _all_gather_small(_to_rows128(jnp.concatenate(pieces)), "all_sum_small")[0].reshape(-1)
    off = 0
    per_name = {name: [] for name in REPLICATED}
    for i in range(depth):
        for name in REPLICATED:
            size = args[name].shape[1]
            per_name[name].append(vec[off:off + size])
            off += size
    for name in REPLICATED:
        grads[name] = jnp.stack(per_name[name])
    grads["final_norm"] = vec[off:off + d]
    off += d
    conv_g = vec[off:off + depth * 3 * conv].reshape(depth, 3, conv)
    grads["conv_w"] = lax.dynamic_slice_in_dim(conv_g, chip * conv_w.shape[-1], conv_w.shape[-1], axis=2)

    delta, new_m, new_v = {}, {}, {}
    for name in WEIGHTS:
        w_, g_, m_, v_ = args[name], grads[name], args["m_" + name], args["v_" + name]
        if w_.ndim == 1:
            outs = _adamw(w_[None], g_[None], m_[None], v_[None])
            delta[name], new_m[name], new_v[name] = (o[0] for o in outs)
        else:
            delta[name], new_m[name], new_v[name] = _adamw(w_, g_, m_, v_)
    return (loss, grad_x, *[grads[n] for n in WEIGHTS], *[delta[n] for n in WEIGHTS],
            *[new_m[n] for n in WEIGHTS], *[new_v[n] for n in WEIGHTS])
```

```python
import functools

import jax
import jax.numpy as jnp
from jax import lax
from jax.experimental import pallas as pl
from jax.experimental.pallas import tpu as pltpu

BF16 = jnp.bfloat16
F32 = jnp.float32
SDS = jax.ShapeDtypeStruct
MESH = pl.DeviceIdType.MESH

N_HEADS = 8
NOPE_DIM = 128
ROPE_DIM = 64
V_DIM = 128
QK_PAD = 256
CHUNK = 64
ROPE_THETA = 10000.0
EPS = 1e-6
ATTN_SCALE = (NOPE_DIM + ROPE_DIM) ** -0.5
NEG_BIG = -1e30

ADAM_LR = 0.001
ADAM_B1 = 0.9
ADAM_B2 = 0.999
ADAM_EPS = 1e-08
ADAM_WD = 0.01
ADAM_STEP = 10

LANES = 128
N_CHIPS = 4
VMEM_LIMIT_BYTES = 56 * 1024 * 1024
ACC_BYTES = 6 * 1024 * 1024
BLOCK_ELEMS = 1 << 19

SHARDED = (("ffn1_w_gu", 1), ("ffn1_w_down", 0), ("w_in", 1), ("w_conv_out", 1), ("w_uq", 1), ("w_ukv", 1),
           ("w_mla_out", 0), ("w_o", 0), ("ffn2_w_gu", 1), ("ffn2_w_down", 0), ("w_ple_gate", 0), ("w_ple_proj", 1))
BIG = tuple(name for name, _ in SHARDED)
REPLICATED = ("ffn1_norm", "mix_norm", "q_norm", "kv_norm", "ffn2_norm", "ple_norm")
WEIGHTS = ("ffn1_norm", "ffn1_w_gu", "ffn1_w_down", "mix_norm", "w_in", "conv_w", "w_conv_out", "q_norm",
           "kv_norm", "w_uq", "w_ukv", "w_mla_out", "w_o", "ffn2_norm", "ffn2_w_gu", "ffn2_w_down",
           "ple_norm", "w_ple_gate", "w_ple_proj", "final_norm")
ARG_NAMES = ("x", "p", "positions") + WEIGHTS + ("loss_target",) + tuple("m_" + n for n in WEIGHTS) + tuple(
    "v_" + n for n in WEIGHTS)


def _cparams(semantics=None):
    return pltpu.CompilerParams(dimension_semantics=semantics, vmem_limit_bytes=VMEM_LIMIT_BYTES)


def _tile(n, cap, mult=LANES):
    best = None
    for t in range(mult, min(n, cap) + 1, mult):
        if n % t == 0:
            best = t
    return n if best is None else best


def _sigmoid(x):
    return 1.0 / (1.0 + jnp.exp(-x))


def _rowspec(tm, width, col_block=0):
    return pl.BlockSpec((tm, width), lambda i: (i, col_block))


def _colspec(tm, width, offset):
    assert offset % width == 0, (width, offset)
    return _rowspec(tm, width, offset // width)


def _mm(a, b, mode, out_dtype, name, scale=None, res=None, layer=None, b_chip=False, out_chip=False, norm_gain=None,
        a_halves=False, b_halves=False, norm_bwd=None, rider=None, epilogue=None):
    bshape = b.shape if layer is None else b.shape[1:]
    if b_chip:
        bshape = (bshape[1], N_CHIPS * bshape[2])
    if b_halves:
        bshape = (bshape[1], 2 * bshape[2])
    ashape = (a.shape[1], 2 * a.shape[2]) if a_halves else a.shape
    if mode == "nn":
        (m, k), (k2, n) = ashape, bshape
    elif mode == "nt":
        (m, k), (n, k2) = ashape, bshape
    else:
        (k, m), (k2, n) = ashape, bshape
    assert k == k2, (a.shape, b.shape, mode)
    n_unit = n // N_CHIPS if (out_chip or (b_chip and mode == "nn")) else n
    k_unit = k // N_CHIPS if (b_chip and mode == "nt") else k
    tn = _tile(n_unit, 1536)
    tm = _tile(m, min(512 if (norm_bwd is not None or epilogue is not None) else 1408, ACC_BYTES // (4 * tn)))
    tk = _tile(k_unit, 1536)
    nk = k // tk
    n_per, k_per = n_unit // tn, k_unit // tk
    n_half, k_half = n // 2 // tn, k // 2 // tk
    dims = {"nn": (((1,), (0,)), ((), ())), "nt": (((1,), (1,)), ((), ())), "tn": (((0,), (0,)), ((), ()))}[mode]

    fuse_norm = norm_gain is not None
    fuse_bwd = norm_bwd is not None
    assert not (fuse_norm or fuse_bwd or epilogue is not None) or (tn == n and not out_chip), (name, tn, n)
    assert epilogue is None or not (fuse_norm or fuse_bwd)
    assert not a_halves or mode == "nt"
    assert not b_halves or mode == "tn"
    n_extra_in = (1 if res is not None else 0) + (1 if fuse_norm else 0) + (3 if fuse_bwd else 0)
    n_outs = 3 if fuse_bwd else (2 if fuse_norm else 1)
    if epilogue is not None:
        n_extra_in += len(epilogue.inputs)
        n_outs = (1 if epilogue.keep_product else 0) + len(epilogue.out_shapes)
    grid = (m // tm, n // tn, nk)
    if fuse_bwd and nk > 1:
        assert rider is None and res is None and scale is None
        return _mm_norm_bwd_k_outer(a, b, mode, name, layer, b_chip, a_halves, norm_bwd, (m, n, k), (tm, tk), k_per, k_half)

    def body(*refs):
        a_ref, b_ref = refs[0], refs[1]
        extra = list(refs[2:2 + n_extra_in])
        rest = refs[2 + n_extra_in:]
        if rider is not None:
            rest = rider.split(rest, n_outs, 1 if nk > 1 else 0, grid)
        outs = list(rest)
        acc_ref = outs.pop() if nk > 1 else None
        res_ref = extra.pop(0) if res is not None else None
        gain_ref = extra.pop(0) if fuse_norm else None

        def finish(acc):
            if scale is not None:
                acc = acc * scale
            if res_ref is not None:
                acc = res_ref[...] + acc
            if fuse_bwd:
                h_ref, g_ref, dhin_ref = extra
                dh_ref, dhb_ref, dg_ref = outs
                dx, dg = _rn_bwd_math(h_ref[...], g_ref[...], acc)
                dh = dhin_ref[...] + dx
                dh_ref[...] = dh
                dhb_ref[...] = dh.astype(BF16)

                @pl.when(pl.program_id(0) == 0)
                def _():
                    dg_ref[...] = dg

                @pl.when(pl.program_id(0) > 0)
                def _():
                    dg_ref[...] += dg
                return
            if epilogue is not None:
                if epilogue.keep_product:
                    outs[0][...] = acc.astype(out_dtype)
                epilogue.fn(acc, extra, outs[1:] if epilogue.keep_product else outs)
                return
            outs[0][...] = acc.astype(out_dtype)
            if fuse_norm:
                outs[1][...] = _rn_fwd_math(acc, gain_ref[...]).astype(BF16)

        part = lax.dot_general(a_ref[...].astype(BF16), b_ref[...].astype(BF16), dims,
                               preferred_element_type=F32)
        if nk == 1:
            finish(part)
        else:
            kk = pl.program_id(2)

            @pl.when(kk == 0)
            def _():
                acc_ref[...] = part

            @pl.when(kk > 0)
            def _():
                acc_ref[...] += part

            @pl.when(kk == nk - 1)
            def _():
                finish(acc_ref[...])
        if rider is not None:
            rider.finish_at_last_step(grid)

    lead = () if layer is None else (layer,)
    lead_block = () if layer is None else (None,)
    if mode == "nn":
        a_spec = pl.BlockSpec((tm, tk), lambda i, j, kk: (i, kk))
        if b_chip:
            b_spec = pl.BlockSpec(lead_block + (None, tk, tn), lambda i, j, kk: lead + (j // n_per, kk, j % n_per))
        else:
            b_spec = pl.BlockSpec(lead_block + (tk, tn), lambda i, j, kk: lead + (kk, j))
    elif mode == "nt":
        if a_halves:
            a_spec = pl.BlockSpec((None, tm, tk), lambda i, j, kk: (kk // k_half, i, kk % k_half))
        else:
            a_spec = pl.BlockSpec((tm, tk), lambda i, j, kk: (i, kk))
        if b_chip:
            b_spec = pl.BlockSpec(lead_block + (None, tn, tk), lambda i, j, kk: lead + (kk // k_per, j, kk % k_per))
        else:
            b_spec = pl.BlockSpec(lead_block + (tn, tk), lambda i, j, kk: lead + (j, kk))
    else:
        assert layer is None and not b_chip
        a_spec = pl.BlockSpec((tk, tm), lambda i, j, kk: (kk, i))
        if b_halves:
            b_spec = pl.BlockSpec((None, tk, tn), lambda i, j, kk: (j // n_half, kk, j % n_half))
        else:
            b_spec = pl.BlockSpec((tk, tn), lambda i, j, kk: (kk, j))
    if out_chip:
        o_spec = pl.BlockSpec((None, tm, tn), lambda i, j, kk: (j // n_per, i, j % n_per))
        out_shape = SDS((N_CHIPS, m, n_unit), out_dtype)
    else:
        o_spec = pl.BlockSpec((tm, tn), lambda i, j, kk: (i, j))
        out_shape = SDS((m, n), out_dtype)
    in_specs = [a_spec, b_spec] + ([o_spec] if res is not None else [])
    operands = (a, b) + ((res,) if res is not None else ())
    out_specs = o_spec
    vec = pl.BlockSpec((1, tn), lambda i, j, kk: (0, j))
    if fuse_norm:
        in_specs.append(vec)
        operands += (norm_gain.reshape(1, n),)
        out_shape, out_specs = (out_shape, SDS((m, n), BF16)), (o_spec, o_spec)
    if epilogue is not None:
        in_specs += [pl.BlockSpec(blk, (lambda i, j, kk, f=f: f(i))) for _, blk, f in epilogue.inputs]
        operands += tuple(arr for arr, _, _ in epilogue.inputs)
        ep_specs = tuple(pl.BlockSpec(blk, (lambda i, j, kk, f=f: f(i))) for _, blk, f in epilogue.out_shapes)
        ep_shapes = tuple(sds for sds, _, _ in epilogue.out_shapes)
        out_shape = ((out_shape,) if epilogue.keep_product else ()) + ep_shapes
        out_specs = ((o_spec,) if epilogue.keep_product else ()) + ep_specs
    if fuse_bwd:
        h, gain, dh_in = norm_bwd
        in_specs += [o_spec, vec, o_spec]
        operands += (h, gain.reshape(1, n), dh_in)
        out_shape = (SDS((m, n), F32), SDS((m, n), BF16), SDS((1, n), F32))
        out_specs = (o_spec, o_spec, vec)
    scratch = [pltpu.VMEM((tm, tn), F32)] if nk > 1 else []
    semantics = ("arbitrary",) * 3 if fuse_bwd else ("parallel", "parallel", "arbitrary")
    if rider is None:
        return pl.pallas_call(
            body, out_shape=out_shape, grid=grid, in_specs=in_specs, out_specs=out_specs, scratch_shapes=scratch,
            compiler_params=_cparams(semantics), name=name)(*operands)
    out_shape = out_shape if isinstance(out_shape, tuple) else (out_shape,)
    out_specs = out_specs if isinstance(out_specs, tuple) else (out_specs,)
    outs = pl.pallas_call(
        body, out_shape=out_shape + tuple(rider.out_shapes), grid=grid, in_specs=in_specs + [ANY] * len(rider.operands),
        out_specs=out_specs + (ANY,) * len(rider.out_shapes), scratch_shapes=scratch + rider.sem_shapes,
        input_output_aliases={len(operands) + i: len(out_shape) + o for i, o in rider.aliases.items()},
        compiler_params=_cparams(("arbitrary",) * 3), name=name + rider.tag)(*operands, *rider.operands)
    return tuple(outs[:n_outs]) + (list(outs[n_outs:]),)


class _RowEpilogue:
    def __init__(self, inputs, out_shapes, fn, keep_product):
        self.inputs, self.out_shapes, self.fn, self.keep_product = list(inputs), list(out_shapes), fn, keep_product


def _rows(arr, tm_of):
    return (arr, (tm_of, arr.shape[1]), lambda i: (i, 0))


def _epilogue_rows(t):
    return _tile(t, 512)


def _delta_epilogue(o):
    t = o.shape[0]
    tm = _epilogue_rows(t)

    def fn(acc, ins, outs):
        prod = acc * ins[0][...].astype(F32)
        for h in range(N_HEADS):
            part = jnp.sum(prod[:, h * V_DIM:(h + 1) * V_DIM], axis=1, keepdims=True)
            outs[0][h] = jnp.broadcast_to(part, (tm, LANES))

    return _RowEpilogue([_rows(o, tm)], [(SDS((N_HEADS, t, LANES), F32), (N_HEADS, tm, LANES), lambda i: (0, i, 0))], fn, True)


def _merge_bwd_epilogue(proj, lay, ya, yb):
    t, d = ya.shape
    tm = _epilogue_rows(t)
    assert lay["gc"] % d == 0 and lay["gm"] % d == 0

    def fn(acc, ins, outs):
        gc, gm, ya_, yb_ = (r[...].astype(F32) for r in ins)
        sc, sm = _sigmoid(gc), _sigmoid(gm)
        outs[0][...] = (acc * sc).astype(BF16)
        outs[1][...] = (acc * sm).astype(BF16)
        outs[2][...] = (acc * ya_ * (sc * (1.0 - sc))).astype(BF16)
        outs[3][...] = (acc * yb_ * (sm * (1.0 - sm))).astype(BF16)

    gate = [(proj, (tm, d), lambda i, c=lay[k] // d: (i, c)) for k in ("gc", "gm")]
    out = (SDS((t, d), BF16), (tm, d), lambda i: (i, 0))
    return _RowEpilogue(gate + [_rows(ya, tm), _rows(yb, tm)], [out] * 4, fn, False)


def _ple_fwd_epilogue(h, pp, norm_gain):
    t, d = h.shape
    tm = _epilogue_rows(t)
    fuse_norm = norm_gain is not None

    def fn(acc, ins, outs):
        gp = acc.astype(BF16).astype(F32)
        out = ins[0][...] + _sigmoid(gp) * ins[1][...].astype(F32)
        outs[0][...] = out
        if fuse_norm:
            outs[1][...] = _rn_fwd_math(out, ins[2][...]).astype(BF16)

    inputs = [_rows(h, tm), _rows(pp, tm)]
    outs = [(SDS((t, d), F32), (tm, d), lambda i: (i, 0))]
    if fuse_norm:
        inputs.append((norm_gain.reshape(1, d), (1, d), lambda i: (0, 0)))
        outs.append((SDS((t, d), BF16), (tm, d), lambda i: (i, 0)))
    return _RowEpilogue(inputs, outs, fn, True)


def _mm_norm_bwd_k_outer(a, b, mode, name, layer, b_chip, a_halves, norm_bwd, sizes, tiles, k_per, k_half):
    (m, n, k), (tm, tk) = sizes, tiles
    nk, ni = k // tk, m // tm
    h, gain, dh_in = norm_bwd
    dims = {"nn": (((1,), (0,)), ((), ())), "nt": (((1,), (1,)), ((), ()))}[mode]

    def body(a_ref, b_ref, h_ref, g_ref, dhin_ref, dh_ref, dhb_ref, dg_ref, acc_ref):
        kk, i = pl.program_id(0), pl.program_id(1)
        rows = pl.ds(pl.multiple_of(i * tm, tm), tm)
        part = lax.dot_general(a_ref[...].astype(BF16), b_ref[...].astype(BF16), dims, preferred_element_type=F32)

        @pl.when(kk == 0)
        def _():
            acc_ref[rows, :] = part

        @pl.when((kk > 0) & (kk < nk - 1))
        def _():
            acc_ref[rows, :] += part

        @pl.when(kk == nk - 1)
        def _():
            dx, dg = _rn_bwd_math(h_ref[...], g_ref[...], acc_ref[rows, :] + part)
            dh = dhin_ref[...] + dx
            dh_ref[...] = dh
            dhb_ref[...] = dh.astype(BF16)

            @pl.when(i == 0)
            def _():
                dg_ref[...] = dg

            @pl.when(i > 0)
            def _():
                dg_ref[...] += dg

    lead = () if layer is None else (layer,)
    lead_block = () if layer is None else (None,)
    if a_halves:
        a_spec = pl.BlockSpec((None, tm, tk), lambda kk, i: (kk // k_half, i, kk % k_half))
    else:
        a_spec = pl.BlockSpec((tm, tk), lambda kk, i: (i, kk))
    if mode == "nn":
        assert not b_chip
        b_spec = pl.BlockSpec(lead_block + (tk, n), lambda kk, i: lead + (kk, 0))
    elif b_chip:
        b_spec = pl.BlockSpec(lead_block + (None, n, tk), lambda kk, i: lead + (kk // k_per, 0, kk % k_per))
    else:
        b_spec = pl.BlockSpec(lead_block + (n, tk), lambda kk, i: lead + (0, kk))
    late = pl.BlockSpec((tm, n), lambda kk, i: (jnp.where(kk == nk - 1, i, 0), 0))
    vec = pl.BlockSpec((1, n), lambda kk, i: (0, 0))
    return pl.pallas_call(
        body, out_shape=(SDS((m, n), F32), SDS((m, n), BF16), SDS((1, n), F32)), grid=(nk, ni),
        in_specs=[a_spec, b_spec, late, vec, late], out_specs=(late, late, vec),
        scratch_shapes=[pltpu.VMEM((m, n), F32)], compiler_params=_cparams(("arbitrary", "arbitrary")),
        name=name)(a, b, h, gain.reshape(1, n), dh_in)


def _rn_fwd_math(x, g):
    r = lax.rsqrt(jnp.mean(x * x, axis=-1, keepdims=True) + EPS)
    return (x * r) * g


def _rn_bwd_math(x, g, dn):
    r = lax.rsqrt(jnp.mean(x * x, axis=-1, keepdims=True) + EPS)
    xh = x * r
    gy = dn * g
    dx = r * (gy - xh * jnp.mean(gy * xh, axis=-1, keepdims=True))
    dg = jnp.sum(dn * xh, axis=0, keepdims=True)
    return dx, dg


def _rmsnorm_fwd(h, gain, name):
    t, d = h.shape
    tm = _tile(t, 512, 8)

    def body(h_ref, g_ref, o_ref):
        o_ref[...] = _rn_fwd_math(h_ref[...], g_ref[...]).astype(BF16)

    return pl.pallas_call(
        body, out_shape=SDS((t, d), BF16), grid=(t // tm,),
        in_specs=[_rowspec(tm, d), pl.BlockSpec((1, d), lambda i: (0, 0))], out_specs=_rowspec(tm, d),
        compiler_params=_cparams(("parallel",)), name=name)(h, gain.reshape(1, d))


def _loss_head(h, gain, target):
    t, d = h.shape
    tm = _tile(t, 512, 8)

    def body(h_ref, g_ref, t_ref, loss_ref, dh_ref, dhb_ref, dg_ref):
        x, g = h_ref[...], g_ref[...]
        err = _rn_fwd_math(x, g) - t_ref[...]
        part = 0.5 * jnp.sum(jnp.sum(err * err, axis=1, keepdims=True), axis=0, keepdims=True) * (1.0 / d)
        dx, dg = _rn_bwd_math(x, g, err * (1.0 / d))
        dh_ref[...] = dx
        dhb_ref[...] = dx.astype(BF16)

        @pl.when(pl.program_id(0) == 0)
        def _():
            dg_ref[...] = dg
            loss_ref[...] = jnp.broadcast_to(part, (1, LANES))

        @pl.when(pl.program_id(0) > 0)
        def _():
            dg_ref[...] += dg
            loss_ref[...] += jnp.broadcast_to(part, (1, LANES))

    vec = pl.BlockSpec((1, d), lambda i: (0, 0))
    return pl.pallas_call(
        body, out_shape=(SDS((1, LANES), F32), SDS((t, d), F32), SDS((t, d), BF16), SDS((1, d), F32)),
        grid=(t // tm,), in_specs=[_rowspec(tm, d), vec, _rowspec(tm, d)],
        out_specs=(pl.BlockSpec((1, LANES), lambda i: (0, 0)), _rowspec(tm, d), _rowspec(tm, d), vec),
        compiler_params=_cparams(("arbitrary",)), name="loss_head")(h, gain.reshape(1, d), target)


def _gu_swiglu_fwd(n, w_gu, layer, name, rider=None):
    t, d = n.shape
    cols = w_gu.shape[3]
    f = 2 * cols
    tn = _tile(cols, 1536)
    tm = _tile(t, 512)
    per = cols // tn

    grid = (f // tn, t // tm)

    def body(n_ref, wg_ref, wu_ref, *rest):
        gu_ref, a_ref = rest[:2] if rider is None else rider.split(rest, 2, 0, grid)
        x = n_ref[...]
        g = jnp.dot(x, wg_ref[...], preferred_element_type=F32)
        u = jnp.dot(x, wu_ref[...], preferred_element_type=F32)
        gu_ref[0] = g.astype(BF16)
        gu_ref[1] = u.astype(BF16)
        a_ref[...] = (g * _sigmoid(g) * u).astype(BF16)
        if rider is not None:
            rider.finish_at_last_step(grid)

    in_specs = [pl.BlockSpec((tm, d), lambda j, i: (i, 0)),
                pl.BlockSpec((None, None, d, tn), lambda j, i: (layer, j // per, 0, j % per)),
                pl.BlockSpec((None, None, d, tn), lambda j, i: (layer, 2 + j // per, 0, j % per))]
    out_shape = (SDS((2, t, f), BF16), SDS((t, f), BF16))
    out_specs = (pl.BlockSpec((2, tm, tn), lambda j, i: (0, i, j)), pl.BlockSpec((tm, tn), lambda j, i: (i, j)))
    if rider is None:
        return pl.pallas_call(body, out_shape=out_shape, grid=grid, in_specs=in_specs, out_specs=out_specs,
                              compiler_params=_cparams(("parallel", "parallel")), name=name)(n, w_gu, w_gu)
    outs = pl.pallas_call(
        body, out_shape=out_shape + tuple(rider.out_shapes), grid=grid, in_specs=in_specs + [ANY] * len(rider.operands),
        out_specs=out_specs + (ANY,) * len(rider.out_shapes), scratch_shapes=rider.sem_shapes,
        input_output_aliases={3 + i: 2 + o for i, o in rider.aliases.items()},
        compiler_params=_cparams(("arbitrary", "arbitrary")), name=name + rider.tag)(n, w_gu, w_gu, *rider.operands)
    return outs[0], outs[1], list(outs[2:])


def _down_dx_swiglu_bwd(dhb, w_down, gu, layer, name):
    t, d = dhb.shape
    f = gu.shape[2]
    tn = _tile(f, 1536)
    tm = _tile(t, 512)

    def body(dh_ref, w_ref, gu_ref, dgu_ref):
        da = 0.5 * lax.dot_general(dh_ref[...], w_ref[...], _NT, preferred_element_type=F32)
        g = gu_ref[0].astype(F32)
        u = gu_ref[1].astype(F32)
        sg = _sigmoid(g)
        dgu_ref[0] = (da * u * (sg * (1.0 + g * (1.0 - sg)))).astype(BF16)
        dgu_ref[1] = (da * (g * sg)).astype(BF16)

    blk = pl.BlockSpec((2, tm, tn), lambda j, i: (0, i, j))
    return pl.pallas_call(
        body, out_shape=SDS((2, t, f), BF16), grid=(f // tn, t // tm),
        in_specs=[pl.BlockSpec((tm, d), lambda j, i: (i, 0)), pl.BlockSpec((None, tn, d), lambda j, i: (layer, j, 0)), blk],
        out_specs=blk, compiler_params=_cparams(("parallel", "parallel")), name=name)(dhb, w_down, gu)


def _shift_down(z, k, row):
    return jnp.where(row >= k, pltpu.roll(z, k, 0), 0.0)


def _shift_up(z, k, row, t):
    return jnp.where(row < t - k, pltpu.roll(z, t - k, 0), 0.0)


def _conv_specs(t, conv):
    nb = conv // LANES
    return [pl.BlockSpec((t, LANES), lambda j: (0, j)), pl.BlockSpec((t, LANES), lambda j: (0, nb + j)),
            pl.BlockSpec((t, LANES), lambda j: (0, 2 * nb + j))]


def _conv_fwd(proj, conv_w):
    t = proj.shape[0]
    conv = conv_w.shape[1]

    def body(b_ref, c_ref, v_ref, w_ref, o_ref):
        z = c_ref[...].astype(F32) * v_ref[...].astype(F32)
        row = lax.broadcasted_iota(jnp.int32, z.shape, 0)
        y = w_ref[0:1, :] * _shift_down(z, 2, row) + w_ref[1:2, :] * _shift_down(z, 1, row) + w_ref[2:3, :] * z
        o_ref[...] = (b_ref[...].astype(F32) * y).astype(BF16)

    cspec = pl.BlockSpec((t, LANES), lambda j: (0, j))
    return pl.pallas_call(
        body, out_shape=SDS((t, conv), BF16), grid=(conv // LANES,),
        in_specs=_conv_specs(t, conv) + [pl.BlockSpec((3, LANES), lambda j: (0, j))], out_specs=cspec,
        compiler_params=_cparams(("parallel",)), name="conv_fwd")(proj, proj, proj, conv_w)


def _conv_bwd(proj, conv_w, dcb):
    t = proj.shape[0]
    conv = conv_w.shape[1]

    def body(b_ref, c_ref, v_ref, w_ref, d_ref, db_ref, dc_ref, dv_ref, dw_ref):
        b, c, v = b_ref[...].astype(F32), c_ref[...].astype(F32), v_ref[...].astype(F32)
        d = d_ref[...].astype(F32)
        z = c * v
        row = lax.broadcasted_iota(jnp.int32, z.shape, 0)
        z1, z2 = _shift_down(z, 1, row), _shift_down(z, 2, row)
        w0, w1, w2 = w_ref[0:1, :], w_ref[1:2, :], w_ref[2:3, :]
        y = w0 * z2 + w1 * z1 + w2 * z
        dy = d * b
        db_ref[...] = (d * y).astype(BF16)
        dz = w2 * dy + w1 * _shift_up(dy, 1, row, t) + w0 * _shift_up(dy, 2, row, t)
        dc_ref[...] = (dz * v).astype(BF16)
        dv_ref[...] = (dz * c).astype(BF16)
        dw_ref[0:1, :] = jnp.sum(dy * z2, axis=0, keepdims=True)
        dw_ref[1:2, :] = jnp.sum(dy * z1, axis=0, keepdims=True)
        dw_ref[2:3, :] = jnp.sum(dy * z, axis=0, keepdims=True)

    cspec = pl.BlockSpec((t, LANES), lambda j: (0, j))
    wspec = pl.BlockSpec((3, LANES), lambda j: (0, j))
    return pl.pallas_call(
        body, out_shape=(SDS((t, conv), BF16),) * 3 + (SDS((3, conv), F32),), grid=(conv // LANES,),
        in_specs=_conv_specs(t, conv) + [wspec, cspec], out_specs=(cspec, cspec, cspec, wspec),
        compiler_params=_cparams(("parallel",)), name="conv_bwd")(proj, proj, proj, conv_w, dcb)


def _qkvnorm_fwd(proj, lay, q_gain, kv_gain):
    t = proj.shape[0]
    ql, kvl = lay["ql"], lay["kvl"]
    tm = _tile(t, 512, 8)

    def body(q_ref, kv_ref, gq_ref, gkv_ref, qn_ref, kvn_ref):
        qn_ref[...] = _rn_fwd_math(q_ref[...].astype(F32), gq_ref[...]).astype(BF16)
        kvn_ref[...] = _rn_fwd_math(kv_ref[...].astype(F32), gkv_ref[...]).astype(BF16)

    return pl.pallas_call(
        body, out_shape=(SDS((t, ql), BF16), SDS((t, kvl), BF16)), grid=(t // tm,),
        in_specs=[_colspec(tm, ql, lay["q"]), _colspec(tm, kvl, lay["kv"]),
                  pl.BlockSpec((1, ql), lambda i: (0, 0)), pl.BlockSpec((1, kvl), lambda i: (0, 0))],
        out_specs=(_rowspec(tm, ql), _rowspec(tm, kvl)), compiler_params=_cparams(("parallel",)),
        name="qkvnorm_fwd")(proj, proj, q_gain.reshape(1, ql), kv_gain.reshape(1, kvl))


def _qkvnorm_bwd(proj, lay, q_gain, kv_gain, dqn, dkvn):
    t = proj.shape[0]
    ql, kvl = lay["ql"], lay["kvl"]
    tm = _tile(t, 512, 8)

    def body(q_ref, kv_ref, gq_ref, gkv_ref, dqn_ref, dkvn_ref, dq_ref, dkv_ref, dgq_ref, dgkv_ref):
        dq, dgq = _rn_bwd_math(q_ref[...].astype(F32), gq_ref[...], dqn_ref[...].astype(F32))
        dkv, dgkv = _rn_bwd_math(kv_ref[...].astype(F32), gkv_ref[...], dkvn_ref[...].astype(F32))
        dq_ref[...] = dq.astype(BF16)
        dkv_ref[...] = dkv.astype(BF16)

        @pl.when(pl.program_id(0) == 0)
        def _():
            dgq_ref[...] = dgq
            dgkv_ref[...] = dgkv

        @pl.when(pl.program_id(0) > 0)
        def _():
            dgq_ref[...] += dgq
            dgkv_ref[...] += dgkv

    vq = pl.BlockSpec((1, ql), lambda i: (0, 0))
    vkv = pl.BlockSpec((1, kvl), lambda i: (0, 0))
    return pl.pallas_call(
        body, out_shape=(SDS((t, ql), BF16), SDS((t, kvl), BF16), SDS((1, ql), F32), SDS((1, kvl), F32)),
        grid=(t // tm,),
        in_specs=[_colspec(tm, ql, lay["q"]), _colspec(tm, kvl, lay["kv"]), vq, vkv, _rowspec(tm, ql),
                  _rowspec(tm, kvl)],
        out_specs=(_rowspec(tm, ql), _rowspec(tm, kvl), vq, vkv), compiler_params=_cparams(("arbitrary",)),
        name="qkvnorm_bwd")(proj, proj, q_gain.reshape(1, ql), kv_gain.reshape(1, kvl), dqn, dkvn)


def _rope(x, cos_t, sin_a, sin_b):
    return x * cos_t + pltpu.roll(x, LANES - ROPE_DIM // 2, 1) * sin_a + pltpu.roll(x, ROPE_DIM // 2, 1) * sin_b


def _rope_fwd(qf, kv, proj, lay, tables):
    t = qf.shape[0]
    tm = _tile(t, 256, 8)
    hq = N_HEADS * QK_PAD

    def body(q_ref, kn_ref, kr_ref, cos_ref, sa_ref, sb_ref, qr_ref, kf_ref):
        cos_t, sin_a, sin_b = cos_ref[...], sa_ref[...], sb_ref[...]
        kr = _rope(kr_ref[...].astype(F32), cos_t, sin_a, sin_b).astype(BF16)
        for h in range(N_HEADS):
            lo = h * QK_PAD
            qr_ref[:, lo:lo + NOPE_DIM] = q_ref[:, lo:lo + NOPE_DIM]
            qr_ref[:, lo + NOPE_DIM:lo + QK_PAD] = _rope(
                q_ref[:, lo + NOPE_DIM:lo + QK_PAD].astype(F32), cos_t, sin_a, sin_b).astype(BF16)
            kf_ref[:, lo:lo + NOPE_DIM] = kn_ref[:, h * NOPE_DIM:(h + 1) * NOPE_DIM]
            kf_ref[:, lo + NOPE_DIM:lo + QK_PAD] = kr

    tab = _rowspec(tm, LANES)
    return pl.pallas_call(
        body, out_shape=(SDS((t, hq), BF16), SDS((t, hq), BF16)), grid=(t // tm,),
        in_specs=[_rowspec(tm, hq), _rowspec(tm, N_HEADS * NOPE_DIM), _colspec(tm, LANES, lay["kr"]), tab, tab, tab],
        out_specs=(_rowspec(tm, hq), _rowspec(tm, hq)), compiler_params=_cparams(("parallel",)),
        name="rope_fwd")(qf, kv, proj, *tables)


def _rope_bwd(dqr, dkf, dv, tables):
    t = dqr.shape[0]
    tm = _tile(t, 256, 8)
    hq = N_HEADS * QK_PAD
    hn = N_HEADS * NOPE_DIM

    def body(dq_ref, dk_ref, dv_ref, cos_ref, sa_ref, sb_ref, dqf_ref, dkv_ref, dkr_ref):
        cos_t, sin_a, sin_b = cos_ref[...], -sa_ref[...], -sb_ref[...]
        dkr = jnp.zeros((tm, LANES), F32)
        for h in range(N_HEADS):
            lo = h * QK_PAD
            dqf_ref[:, lo:lo + NOPE_DIM] = dq_ref[:, lo:lo + NOPE_DIM].astype(BF16)
            dqf_ref[:, lo + NOPE_DIM:lo + QK_PAD] = _rope(
                dq_ref[:, lo + NOPE_DIM:lo + QK_PAD].astype(F32), cos_t, sin_a, sin_b).astype(BF16)
            dkv_ref[:, h * NOPE_DIM:(h + 1) * NOPE_DIM] = dk_ref[:, lo:lo + NOPE_DIM]
            dkr = dkr + dk_ref[:, lo + NOPE_DIM:lo + QK_PAD].astype(F32)
        dkv_ref[:, hn:] = dv_ref[...]
        dkr_ref[...] = _rope(dkr, cos_t, sin_a, sin_b).astype(BF16)

    tab = _rowspec(tm, LANES)
    return pl.pallas_call(
        body, out_shape=(SDS((t, hq), BF16), SDS((t, 2 * hn), BF16), SDS((t, LANES), BF16)), grid=(t // tm,),
        in_specs=[_rowspec(tm, hq), _rowspec(tm, hq), _rowspec(tm, hn), tab, tab, tab],
        out_specs=(_rowspec(tm, hq), _rowspec(tm, 2 * hn), tab), compiler_params=_cparams(("parallel",)),
        name="rope_bwd")(dqr, dkf, dv, *tables)


def _chunk_mask(bq):
    qc = lax.broadcasted_iota(jnp.int32, (bq, bq), 0) // CHUNK
    kc = lax.broadcasted_iota(jnp.int32, (bq, bq), 1) // CHUNK
    return kc <= qc


_NT = (((1,), (1,)), ((), ()))
_TN = (((0,), (0,)), ((), ()))
LOG2E = 1.4426950408889634
EXP2_SCALE = ATTN_SCALE * LOG2E


def _attn_block(t):
    return 512 if t >= 2048 else 128


def _two_slot_pipeline(unmasked, issue, consume, carry):
    issue(0, 0)

    def pair(n, c):
        issue(2 * n + 1, 1)
        c = consume(2 * n, 0, c, False)
        issue(2 * n + 2, 0)
        return consume(2 * n + 1, 1, c, False)

    carry = lax.fori_loop(0, unmasked // 2, pair, carry)

    def even(c):
        return consume(unmasked, 0, c, True)

    def odd(c):
        issue(unmasked, 1)
        c = consume(unmasked - 1, 0, c, False)
        return consume(unmasked, 1, c, True)

    return lax.cond(unmasked % 2 == 0, even, odd, carry)


def _attn_fwd(qr, kf, kv, gather=None):
    t = qr.shape[0]
    bq = _attn_block(t)
    nq = t // bq
    bufs, pieces = gather if gather is not None else ((), ())
    nw = len(bufs)

    def body(*refs):
        q_ref, k_ref, v_ref = refs[:3]
        o_ref, lse_ref = refs[3 + nw:5 + nw]
        buf_refs = refs[5 + nw:5 + 2 * nw]
        vaug_ref, s_ref = refs[5 + 2 * nw], refs[6 + 2 * nw]
        sems = refs[7 + 2 * nw:]
        h, i = pl.program_id(0), pl.program_id(1)

        if nw:
            @pl.when((h == 0) & (i == 0))
            def _():
                _gather_start(buf_refs, pieces, *sems)

        @pl.when(i == 0)
        def _():
            vaug_ref[:, :V_DIM] = v_ref[...]
            vaug_ref[:, V_DIM:] = jnp.ones((t, V_DIM), BF16)

        def issue(j, slot):
            off = pl.multiple_of(j * bq, bq)
            s_ref[slot] = lax.dot_general(q_ref[...], k_ref[pl.ds(off, bq), :], _NT, preferred_element_type=F32)

        def consume(j, slot, carry, masked):
            m, acc = carry
            off = pl.multiple_of(j * bq, bq)
            s = s_ref[slot]
            if masked:
                s = jnp.where(_chunk_mask(bq), s, NEG_BIG)
            m_new = jnp.maximum(m, jnp.max(s, axis=1, keepdims=True))
            alpha = jnp.exp2((m - m_new) * EXP2_SCALE)
            pr = jnp.exp2((s - m_new) * EXP2_SCALE)
            acc = alpha * acc + jnp.dot(pr.astype(BF16), vaug_ref[pl.ds(off, bq), :], preferred_element_type=F32)
            return m_new, acc

        init = (jnp.full((bq, 1), NEG_BIG, F32), jnp.zeros((bq, 2 * V_DIM), F32))
        m, acc = _two_slot_pipeline(i, issue, consume, init)
        l = acc[:, V_DIM:V_DIM + 1]
        o_ref[...] = (acc[:, :V_DIM] / l).astype(BF16)
        lse_ref[0] = jnp.broadcast_to(m * ATTN_SCALE + jnp.log(l), (bq, LANES))

        if nw:
            @pl.when((h == N_HEADS - 1) & (i == nq - 1))
            def _():
                _gather_finish(buf_refs, pieces, *sems)

    out_shape = (SDS((t, N_HEADS * V_DIM), BF16), SDS((N_HEADS, t, LANES), F32)) + tuple(SDS(b.shape, b.dtype) for b in bufs)
    sem_shapes = _gather_sems(pieces) if nw else []
    outs = pl.pallas_call(
        body, out_shape=out_shape, grid=(N_HEADS, nq),
        in_specs=[pl.BlockSpec((bq, QK_PAD), lambda h, i: (i, h)), pl.BlockSpec((t, QK_PAD), lambda h, i: (0, h)),
                  pl.BlockSpec((t, V_DIM), lambda h, i: (0, N_HEADS + h))] + [ANY] * nw,
        out_specs=(pl.BlockSpec((bq, V_DIM), lambda h, i: (i, h)),
                   pl.BlockSpec((1, bq, LANES), lambda h, i: (h, i, 0))) + (ANY,) * nw,
        input_output_aliases={3 + n: 2 + n for n in range(nw)},
        scratch_shapes=[pltpu.VMEM((t, 2 * V_DIM), BF16), pltpu.VMEM((2, bq, bq), F32)] + sem_shapes,
        compiler_params=_cparams(("arbitrary", "arbitrary")), name="attn_fwd_gather" if nw else "attn_fwd")(qr, kf, kv, *bufs)
    return outs[0], outs[1], list(outs[2:])


def _attn_bwd(qr, kf, kv, do, lse, delta, exchange=None):
    t = qr.shape[0]
    bq = _attn_block(t)
    nq = t // bq
    pbs, lands = exchange if exchange is not None else ((), ())
    nw = len(pbs)

    def body(*refs):
        k_ref, v_ref, q_ref, do_ref, lse_ref, dl_ref = refs[:6]
        pb_refs = refs[6:6 + nw]
        dk_ref, dv_ref, dq_ref = refs[6 + 2 * nw:9 + 2 * nw]
        land_refs = refs[9 + 2 * nw:9 + 3 * nw]
        s_ref, dp_ref = refs[9 + 3 * nw], refs[10 + 3 * nw]
        sems = refs[11 + 3 * nw:]
        h, j = pl.program_id(0), pl.program_id(1)

        if nw:
            @pl.when((h == 0) & (j == 0))
            def _():
                _exchange_start(pb_refs, land_refs, *sems)

        @pl.when(j == 0)
        def _():
            dq_ref[...] = jnp.zeros_like(dq_ref)

        k = k_ref[...]
        v = v_ref[...]

        def issue(b, slot):
            off = pl.multiple_of((nq - 1 - b) * bq, bq)
            s_ref[slot] = lax.dot_general(q_ref[pl.ds(off, bq), :], k, _NT, preferred_element_type=F32)
            dp_ref[slot] = lax.dot_general(do_ref[pl.ds(off, bq), :], v, _NT, preferred_element_type=F32)

        def consume(b, slot, carry, masked):
            dk, dv = carry
            off = pl.multiple_of((nq - 1 - b) * bq, bq)
            q = q_ref[pl.ds(off, bq), :]
            do_ = do_ref[pl.ds(off, bq), :]
            lse2 = lse_ref[0, pl.ds(off, bq), :][:, :1] * LOG2E
            dl_i = dl_ref[0, pl.ds(off, bq), :][:, :1]
            s = s_ref[slot]
            if masked:
                s = jnp.where(_chunk_mask(bq), s, NEG_BIG)
            pr = jnp.exp2(s * EXP2_SCALE - lse2)
            dv = dv + lax.dot_general(pr.astype(BF16), do_, _TN, preferred_element_type=F32)
            ds = (pr * (dp_ref[slot] - dl_i)).astype(BF16)
            dk = dk + lax.dot_general(ds, q, _TN, preferred_element_type=F32)
            dq_ref[pl.ds(off, bq), :] += jnp.dot(ds, k, preferred_element_type=F32) * ATTN_SCALE
            return dk, dv

        init = (jnp.zeros((bq, QK_PAD), F32), jnp.zeros((bq, V_DIM), F32))
        dk, dv = _two_slot_pipeline(nq - 1 - j, issue, consume, init)
        dk_ref[...] = (dk * ATTN_SCALE).astype(BF16)
        dv_ref[...] = dv.astype(BF16)

        if nw:
            @pl.when((h == N_HEADS - 1) & (j == nq - 1))
            def _():
                _exchange_finish(pb_refs, land_refs, *sems)

    stat = pl.BlockSpec((1, t, LANES), lambda h, j: (h, 0, 0))
    out_shape = (SDS((t, N_HEADS * QK_PAD), BF16), SDS((t, N_HEADS * V_DIM), BF16), SDS((t, N_HEADS * QK_PAD), F32))
    sem_shapes = [pltpu.SemaphoreType.DMA((3 * nw,)), pltpu.SemaphoreType.DMA((3 * nw,))] if nw else []
    outs = pl.pallas_call(
        body, out_shape=out_shape + tuple(SDS(l.shape, l.dtype) for l in lands), grid=(N_HEADS, nq),
        in_specs=[pl.BlockSpec((bq, QK_PAD), lambda h, j: (j, h)), pl.BlockSpec((bq, V_DIM), lambda h, j: (j, N_HEADS + h)),
                  pl.BlockSpec((t, QK_PAD), lambda h, j: (0, h)), pl.BlockSpec((t, V_DIM), lambda h, j: (0, h)), stat, stat]
        + [ANY] * (2 * nw),
        out_specs=(pl.BlockSpec((bq, QK_PAD), lambda h, j: (j, h)), pl.BlockSpec((bq, V_DIM), lambda h, j: (j, h)),
                   pl.BlockSpec((t, QK_PAD), lambda h, j: (0, h))) + (ANY,) * nw,
        input_output_aliases={6 + nw + n: 3 + n for n in range(nw)},
        scratch_shapes=[pltpu.VMEM((2, bq, bq), F32), pltpu.VMEM((2, bq, bq), F32)] + sem_shapes,
        compiler_params=_cparams(("arbitrary", "arbitrary")),
        name="attn_bwd_exchange" if nw else "attn_bwd")(kf, kv, qr, do, lse, delta, *pbs, *lands)
    return outs[0], outs[1], outs[2], list(outs[3:])


def _merge_fwd(proj, lay, ya, yb):
    t, d = ya.shape
    tm = _tile(t, 512, 8)

    def body(gc_ref, gm_ref, ya_ref, yb_ref, o_ref):
        o_ref[...] = (_sigmoid(gc_ref[...].astype(F32)) * ya_ref[...].astype(F32)
                      + _sigmoid(gm_ref[...].astype(F32)) * yb_ref[...].astype(F32)).astype(BF16)

    return pl.pallas_call(
        body, out_shape=SDS((t, d), BF16), grid=(t // tm,),
        in_specs=[_colspec(tm, d, lay["gc"]), _colspec(tm, d, lay["gm"]), _rowspec(tm, d), _rowspec(tm, d)],
        out_specs=_rowspec(tm, d), compiler_params=_cparams(("parallel",)), name="merge_fwd")(proj, proj, ya, yb)


def _ple_bwd(dh, gp, pp):
    t, d = dh.shape
    tm = _tile(t, 512, 8)

    def body(dh_ref, gp_ref, pp_ref, dpp_ref, dgp_ref):
        g = dh_ref[...]
        s = _sigmoid(gp_ref[...].astype(F32))
        dpp_ref[...] = (g * s).astype(BF16)
        dgp_ref[...] = (g * pp_ref[...].astype(F32) * (s * (1.0 - s))).astype(BF16)

    r = _rowspec(tm, d)
    return pl.pallas_call(body, out_shape=(SDS((t, d), BF16),) * 2, grid=(t // tm,), in_specs=[r, r, r],
                          out_specs=(r, r), compiler_params=_cparams(("parallel",)), name="ple_bwd")(dh, gp, pp)


def _adamw(w, g, m, v):
    shape = w.shape
    cols = shape[-1]
    rows = w.size // cols
    tr = _tile(rows, max(8, BLOCK_ELEMS // cols // 8 * 8), 8)

    def body(w_ref, g_ref, m_ref, v_ref, d_ref, nm_ref, nv_ref):
        g_ = g_ref[...]
        nm = ADAM_B1 * m_ref[...] + (1.0 - ADAM_B1) * g_
        nv = ADAM_B2 * v_ref[...] + (1.0 - ADAM_B2) * (g_ * g_)
        m_hat = nm / (1.0 - ADAM_B1 ** ADAM_STEP)
        v_hat = nv / (1.0 - ADAM_B2 ** ADAM_STEP)
        d_ref[...] = -ADAM_LR * (m_hat / (jnp.sqrt(v_hat) + ADAM_EPS) + ADAM_WD * w_ref[...])
        nm_ref[...] = nm
        nv_ref[...] = nv

    r = _rowspec(tr, cols)
    outs = pl.pallas_call(
        body, out_shape=(SDS((rows, cols), F32),) * 3, grid=(rows // tr,), in_specs=[r, r, r, r], out_specs=(r, r, r),
        compiler_params=_cparams(("parallel",)), name="adamw")(*(a.reshape(rows, cols) for a in (w, g, m, v)))
    return tuple(o.reshape(shape) for o in outs)


ANY = pl.BlockSpec(memory_space=pl.ANY)


def _place():
    x, y, c = lax.axis_index("x"), lax.axis_index("y"), lax.axis_index("c")
    return x, y, c, [(1 - x, y), (x, 1 - y), (1 - x, 1 - y)]


def _half_rows(rows, cols):
    half = rows // 2
    return half, _tile(half, max(16, BLOCK_ELEMS // cols // 16 * 16), 16)


def _my_chip():
    return 2 * lax.axis_index("x") + lax.axis_index("y")


def _cast_into_slot(w):
    nl, r, c = w.shape
    tr = _tile(r, max(16, BLOCK_ELEMS // c // 16 * 16), 16)

    def body(w_ref, o_ref):
        o_ref[...] = w_ref[...].astype(BF16)

    return pl.pallas_call(
        body, out_shape=SDS((nl, N_CHIPS, r, c), BF16), grid=(nl, r // tr),
        in_specs=[pl.BlockSpec((None, tr, c), lambda l, i: (l, i, 0))],
        out_specs=pl.BlockSpec((None, None, tr, c), lambda l, i: (l, _my_chip(), i, 0)),
        compiler_params=_cparams(("parallel", "parallel")), name="cast_into_slot")(w)


def _gather_copy(ref, layer, send_sems, recv_sems, sem, chip, half, to):
    r2 = ref.shape[2] // 2
    rows = ref.at[layer, chip, pl.ds(half * r2, r2)]
    return pltpu.make_async_remote_copy(src_ref=rows, dst_ref=rows, send_sem=send_sems.at[sem], recv_sem=recv_sems.at[sem],
                                        device_id=to, device_id_type=MESH)


def _gather_start(refs, pieces, send_sems, recv_sems):
    x, y, c, chips = _place()
    for pi, (ri, layer) in enumerate(pieces):
        for n, chip in enumerate(chips):
            _gather_copy(refs[ri], layer, send_sems, recv_sems, 6 * pi + n, 2 * x + y, c, (*chip, c)).start()


def _gather_finish(refs, pieces, send_sems, recv_sems):
    x, y, c, chips = _place()
    me, sibling = (x, y, c), (x, y, 1 - c)
    for pi, (ri, layer) in enumerate(pieces):
        for n, chip in enumerate(chips):
            k = 2 * chip[0] + chip[1]
            _gather_copy(refs[ri], layer, send_sems, recv_sems, 6 * pi + n, k, c, me).wait_recv()
            _gather_copy(refs[ri], layer, send_sems, recv_sems, 6 * pi + 3 + n, k, c, sibling).start()
    for pi, (ri, layer) in enumerate(pieces):
        for n, chip in enumerate(chips):
            _gather_copy(refs[ri], layer, send_sems, recv_sems, 6 * pi + 3 + n, 2 * chip[0] + chip[1], 1 - c, me).wait_recv()
    for pi, (ri, layer) in enumerate(pieces):
        for n, chip in enumerate(chips):
            _gather_copy(refs[ri], layer, send_sems, recv_sems, 6 * pi + n, 2 * x + y, c, (*chip, c)).wait_send()
            _gather_copy(refs[ri], layer, send_sems, recv_sems, 6 * pi + 3 + n, 2 * chip[0] + chip[1], c, sibling).wait_send()


def _gather_sems(pieces):
    return [pltpu.SemaphoreType.DMA((6 * len(pieces),)), pltpu.SemaphoreType.DMA((6 * len(pieces),))]


def _all_gather_weights(bufs, pieces):
    nw = len(bufs)

    def body(*refs):
        _gather_start(refs[nw:2 * nw], pieces, refs[2 * nw], refs[2 * nw + 1])
        _gather_finish(refs[nw:2 * nw], pieces, refs[2 * nw], refs[2 * nw + 1])

    return list(pl.pallas_call(
        body, out_shape=tuple(SDS(b.shape, b.dtype) for b in bufs), in_specs=[ANY] * nw, out_specs=(ANY,) * nw,
        input_output_aliases={i: i for i in range(nw)}, scratch_shapes=_gather_sems(pieces),
        name="all_gather_weights")(*bufs))


def _pair_swap(grads):
    nw = len(grads)

    def body(*refs):
        ins, outs = refs[:nw], refs[nw:2 * nw]
        send_sems, recv_sems = refs[2 * nw], refs[2 * nw + 1]
        x, y, c, _ = _place()
        copies = []
        for wi, (g_ref, o_ref) in enumerate(zip(ins, outs)):
            r2 = g_ref.shape[1] // 2
            copies.append(pltpu.make_async_remote_copy(
                src_ref=g_ref.at[:, pl.ds((1 - c) * r2, r2)], dst_ref=o_ref, send_sem=send_sems.at[wi],
                recv_sem=recv_sems.at[wi], device_id=(x, y, 1 - c), device_id_type=MESH))
            copies[-1].start()
        for cp in copies:
            cp.wait()

    return pl.pallas_call(
        body, out_shape=tuple(SDS((N_CHIPS, g.shape[1] // 2, g.shape[2]), g.dtype) for g in grads),
        in_specs=[ANY] * nw, out_specs=(ANY,) * nw,
        scratch_shapes=[pltpu.SemaphoreType.DMA((nw,)), pltpu.SemaphoreType.DMA((nw,))], name="pair_swap")(*grads)


def _pair_add(g, other):
    _, r, c = g.shape
    r2, tr = _half_rows(r, c)
    nb = r2 // tr

    def body(g_ref, o_ref, pb_ref, land_ref):
        total = (g_ref[...].astype(F32) + o_ref[...].astype(F32)).astype(BF16)
        pb_ref[...] = total

        @pl.when(pl.program_id(1) == _my_chip())
        def _():
            land_ref[...] = total

    blk = pl.BlockSpec((None, tr, c), lambda j, k: (k, j, 0))
    return pl.pallas_call(
        body, out_shape=(SDS((N_CHIPS, r2, c), BF16),) * 2, grid=(nb, N_CHIPS),
        in_specs=[pl.BlockSpec((None, tr, c), lambda j, k: (k, lax.axis_index("c") * nb + j, 0)), blk],
        out_specs=(blk, pl.BlockSpec((None, tr, c), lambda j, k: (_my_chip(), j, 0))),
        compiler_params=_cparams(("parallel", "arbitrary")), name="pair_add")(g, other)


def _exchange_copy(p_ref, l_ref, send_sems, recv_sems, sem, src_slot, dst_slot, to):
    return pltpu.make_async_remote_copy(src_ref=p_ref.at[src_slot], dst_ref=l_ref.at[dst_slot], send_sem=send_sems.at[sem],
                                        recv_sem=recv_sems.at[sem], device_id=to, device_id_type=MESH)


def _exchange_start(p_refs, l_refs, send_sems, recv_sems):
    x, y, c, chips = _place()
    for wi, (p_ref, l_ref) in enumerate(zip(p_refs, l_refs)):
        for n, chip in enumerate(chips):
            _exchange_copy(p_ref, l_ref, send_sems, recv_sems, 3 * wi + n, 2 * chip[0] + chip[1], 2 * x + y, (*chip, c)).start()


def _exchange_finish(p_refs, l_refs, send_sems, recv_sems):
    x, y, c, chips = _place()
    for wi, (p_ref, l_ref) in enumerate(zip(p_refs, l_refs)):
        for n, chip in enumerate(chips):
            _exchange_copy(p_ref, l_ref, send_sems, recv_sems, 3 * wi + n, 2 * x + y, 2 * chip[0] + chip[1], (x, y, c)).wait_recv()
    for wi, (p_ref, l_ref) in enumerate(zip(p_refs, l_refs)):
        for n, chip in enumerate(chips):
            _exchange_copy(p_ref, l_ref, send_sems, recv_sems, 3 * wi + n, 2 * chip[0] + chip[1], 2 * x + y, (*chip, c)).wait_send()


def _chip_all_to_all(pbs, lands):
    nw = len(pbs)

    def body(*refs):
        _exchange_start(refs[:nw], refs[2 * nw:3 * nw], refs[3 * nw], refs[3 * nw + 1])
        _exchange_finish(refs[:nw], refs[2 * nw:3 * nw], refs[3 * nw], refs[3 * nw + 1])

    return list(pl.pallas_call(
        body, out_shape=tuple(SDS(l.shape, l.dtype) for l in lands), in_specs=[ANY] * (2 * nw), out_specs=(ANY,) * nw,
        input_output_aliases={nw + i: i for i in range(nw)},
        scratch_shapes=[pltpu.SemaphoreType.DMA((3 * nw,)), pltpu.SemaphoreType.DMA((3 * nw,))],
        name="chip_all_to_all")(*pbs, *lands))


def _sum_chips(land, gstack, layer):
    _, r, c = gstack.shape
    r2, tr = _half_rows(r, c)
    nb = r2 // tr

    def body(l_ref, g_ref, out_ref):
        out_ref[...] = ((l_ref[0].astype(F32) + l_ref[1].astype(F32)) + l_ref[2].astype(F32)) + l_ref[3].astype(F32)

    return pl.pallas_call(
        body, out_shape=SDS(gstack.shape, F32), grid=(nb,),
        in_specs=[pl.BlockSpec((N_CHIPS, tr, c), lambda j: (0, j, 0)), ANY],
        out_specs=pl.BlockSpec((None, tr, c), lambda j: (layer, lax.axis_index("c") * nb + j, 0)),
        input_output_aliases={1: 0}, compiler_params=_cparams(("parallel",)), name="sum_chips")(land, gstack)


def _pair_gather(gstacks, layer):
    nw = len(gstacks)

    def body(*refs):
        outs = refs[nw:2 * nw]
        send_sems, recv_sems = refs[2 * nw], refs[2 * nw + 1]
        x, y, c, _ = _place()

        def copy(ref, wi, half):
            r2 = ref.shape[1] // 2
            blk = ref.at[layer, pl.ds(half * r2, r2)]
            return pltpu.make_async_remote_copy(src_ref=blk, dst_ref=blk, send_sem=send_sems.at[wi],
                                                recv_sem=recv_sems.at[wi], device_id=(x, y, 1 - c), device_id_type=MESH)

        sent = [copy(ref, wi, c) for wi, ref in enumerate(outs)]
        for cp in sent:
            cp.start()
        for wi, ref in enumerate(outs):
            copy(ref, wi, 1 - c).wait_recv()
        for cp in sent:
            cp.wait_send()

    return pl.pallas_call(
        body, out_shape=tuple(SDS(g.shape, g.dtype) for g in gstacks), in_specs=[ANY] * nw, out_specs=(ANY,) * nw,
        input_output_aliases={i: i for i in range(nw)},
        scratch_shapes=[pltpu.SemaphoreType.DMA((nw,)), pltpu.SemaphoreType.DMA((nw,))], name="pair_gather")(*gstacks)


class _Rider:
    def __init__(self, tag, operands, out_shapes, aliases, sem_shapes, start, finish):
        self.tag, self.operands, self.out_shapes, self.aliases = tag, list(operands), list(out_shapes), dict(aliases)
        self.sem_shapes, self._start, self._finish = list(sem_shapes), start, finish
        self._refs = None

    def split(self, refs, n_outs, n_scratch, grid):
        n_in, n_out = len(self.operands), len(self.out_shapes)
        ins = refs[:n_in]
        own_outs = refs[n_in:n_in + n_outs]
        outs = refs[n_in + n_outs:n_in + n_outs + n_out]
        own_scratch = refs[n_in + n_outs + n_out:n_in + n_outs + n_out + n_scratch]
        sems = refs[n_in + n_outs + n_out + n_scratch:]
        self._refs = (ins, outs, sems)
        first = functools.reduce(lambda p, q: p & q, [pl.program_id(ax) == 0 for ax in range(len(grid))])

        @pl.when(first)
        def _():
            self._start(ins, outs, sems)

        return tuple(own_outs) + tuple(own_scratch)

    def finish_at_last_step(self, grid):
        ins, outs, sems = self._refs
        last = functools.reduce(lambda p, q: p & q, [pl.program_id(ax) == g - 1 for ax, g in enumerate(grid)])

        @pl.when(last)
        def _():
            self._finish(ins, outs, sems)


def _gather_rider(bufs, pieces):
    nw = len(bufs)
    return _Rider("_gather", bufs, [SDS(b.shape, b.dtype) for b in bufs], {i: i for i in range(nw)}, _gather_sems(pieces),
                  lambda ins, outs, sems: _gather_start(outs, pieces, *sems),
                  lambda ins, outs, sems: _gather_finish(outs, pieces, *sems))


def _pair_swap_copies(ins, outs, sems):
    x, y, c, _ = _place()
    copies = []
    for wi, (g_ref, o_ref) in enumerate(zip(ins, outs)):
        r2 = g_ref.shape[1] // 2
        copies.append(pltpu.make_async_remote_copy(
            src_ref=g_ref.at[:, pl.ds((1 - c) * r2, r2)], dst_ref=o_ref, send_sem=sems[0].at[wi],
            recv_sem=sems[1].at[wi], device_id=(x, y, 1 - c), device_id_type=MESH))
    return copies


def _pair_swap_rider(grads):
    nw = len(grads)

    def start(ins, outs, sems):
        for cp in _pair_swap_copies(ins, outs, sems):
            cp.start()

    def finish(ins, outs, sems):
        for cp in _pair_swap_copies(ins, outs, sems):
            cp.wait()

    return _Rider("_swap", grads, [SDS((N_CHIPS, g.shape[1] // 2, g.shape[2]), g.dtype) for g in grads], {},
                  [pltpu.SemaphoreType.DMA((nw,)), pltpu.SemaphoreType.DMA((nw,))], start, finish)


def _pair_gather_copy(ref, layer, sems, wi, half, peer):
    r2 = ref.shape[1] // 2
    blk = ref.at[layer, pl.ds(half * r2, r2)]
    return pltpu.make_async_remote_copy(src_ref=blk, dst_ref=blk, send_sem=sems[0].at[wi], recv_sem=sems[1].at[wi],
                                        device_id=peer, device_id_type=MESH)


def _pair_gather_rider(gstacks, layer):
    nw = len(gstacks)

    def start(ins, outs, sems):
        x, y, c, _ = _place()
        for wi, ref in enumerate(outs):
            _pair_gather_copy(ref, layer, sems, wi, c, (x, y, 1 - c)).start()

    def finish(ins, outs, sems):
        x, y, c, _ = _place()
        for wi, ref in enumerate(outs):
            _pair_gather_copy(ref, layer, sems, wi, 1 - c, (x, y, 1 - c)).wait_recv()
        for wi, ref in enumerate(outs):
            _pair_gather_copy(ref, layer, sems, wi, c, (x, y, 1 - c)).wait_send()

    return _Rider("_pair_gather", gstacks, [SDS(g.shape, g.dtype) for g in gstacks], {i: i for i in range(nw)},
                  [pltpu.SemaphoreType.DMA((nw,)), pltpu.SemaphoreType.DMA((nw,))], start, finish)


def _all_gather_small(vec, name):
    rows, w = vec.shape

    def body(v_ref, sum_ref, all_ref, send_sems, recv_sems):
        x, y, c, chips = _place()
        me, sibling = (x, y, c), (x, y, 1 - c)

        def slot(px, py, pc):
            return all_ref.at[4 * px + 2 * py + pc]

        def copy(k, block, to, src=None):
            return pltpu.make_async_remote_copy(
                src_ref=slot(*block) if src is None else src, dst_ref=slot(*block), send_sem=send_sems.at[k],
                recv_sem=recv_sems.at[k], device_id=to, device_id_type=MESH)

        first = [copy(0, me, sibling, src=v_ref)]
        first += [copy(1 + n, me, (*chip, c), src=v_ref) for n, chip in enumerate(chips)]
        for cp in first:
            cp.start()
        slot(*me)[...] = v_ref[...]
        passed = [copy(4 + n, (*chip, c), sibling) for n, chip in enumerate(chips)]
        for n, chip in enumerate(chips):
            copy(1 + n, (*chip, c), me).wait_recv()
            passed[n].start()
        copy(0, sibling, me).wait_recv()
        for n, chip in enumerate(chips):
            copy(4 + n, (*chip, 1 - c), me).wait_recv()
        for cp in first + passed:
            cp.wait_send()
        total = all_ref[0]
        for dev in range(1, 8):
            total = total + all_ref[dev]
        sum_ref[...] = total

    vm = pl.BlockSpec(memory_space=pltpu.VMEM)
    return pl.pallas_call(
        body, out_shape=(SDS((rows, w), F32), SDS((8, rows, w), F32)), in_specs=[vm], out_specs=(vm, vm),
        scratch_shapes=[pltpu.SemaphoreType.DMA((7,)), pltpu.SemaphoreType.DMA((7,))], name=name)(vec)


def _to_rows128(flat):
    n = flat.shape[0]
    rows = -(-n // (8 * LANES)) * 8
    return jnp.pad(flat, (0, rows * LANES - n)).reshape(rows, LANES)


def _in_layout(conv, ql, kvl, d):
    lay = {"conv": conv, "ql": ql, "kvl": kvl, "d": d}
    lay["q"] = 3 * conv
    lay["kr"] = lay["q"] + ql
    lay["gc"] = lay["kr"] + LANES
    lay["gm"] = lay["gc"] + d
    lay["kv"] = lay["gm"] + d
    used = lay["kv"] + kvl
    lay["width"] = -(-used // 512) * 512
    return lay


def _w_in_to_layout(w, lay):
    conv, ql, kvl, d = lay["conv"], lay["ql"], lay["kvl"], lay["d"]
    o_kv = 3 * conv + ql
    o_kr = o_kv + kvl
    o_g = o_kr + ROPE_DIM
    lead = w.shape[:-1]
    parts = [w[..., :o_kv], w[..., o_kr:o_g], jnp.zeros(lead + (LANES - ROPE_DIM,), w.dtype), w[..., o_g:o_g + 2 * d],
             w[..., o_kv:o_kr], jnp.zeros(lead + (lay["width"] - lay["kv"] - kvl,), w.dtype)]
    return jnp.concatenate(parts, axis=-1)


def _w_in_from_layout(g, lay):
    ql, kvl, d = lay["ql"], lay["kvl"], lay["d"]
    return jnp.concatenate([g[:, :lay["q"] + ql], g[:, lay["kv"]:lay["kv"] + kvl], g[:, lay["kr"]:lay["kr"] + ROPE_DIM],
                            g[:, lay["gc"]:lay["gc"] + 2 * d]], axis=1)


def _w_uq_to_layout(w):
    r = w.shape[0]
    w3 = w.reshape(r, N_HEADS, NOPE_DIM + ROPE_DIM)
    return jnp.pad(w3, ((0, 0), (0, 0), (0, QK_PAD - NOPE_DIM - ROPE_DIM))).reshape(r, N_HEADS * QK_PAD)


def _w_uq_from_layout(g):
    r = g.shape[0]
    return g.reshape(r, N_HEADS, QK_PAD)[:, :, :NOPE_DIM + ROPE_DIM].reshape(r, N_HEADS * (NOPE_DIM + ROPE_DIM))


def _w_ukv_to_layout(w):
    r = w.shape[0]
    return w.reshape(r, N_HEADS, 2, NOPE_DIM).transpose(0, 2, 1, 3).reshape(r, 2 * N_HEADS * NOPE_DIM)


def _w_ukv_from_layout(g):
    r = g.shape[0]
    return g.reshape(r, 2, N_HEADS, NOPE_DIM).transpose(0, 2, 1, 3).reshape(r, 2 * N_HEADS * NOPE_DIM)


def _chips_to_cols(buf):
    _, r, c = buf.shape
    return buf.transpose(1, 0, 2).reshape(r, N_CHIPS * c)


def _cols_to_chips(g):
    r, c4 = g.shape
    return g.reshape(r, N_CHIPS, c4 // N_CHIPS).transpose(1, 0, 2)


BEFORE_FFN1_GU = ("ffn1_w_gu",)
BEFORE_FFN1_DOWN = ("ffn1_w_down", "w_uq", "w_ukv")
IN_ATTENTION = ("w_in", "w_conv_out", "ffn2_w_gu", "ffn2_w_down", "w_mla_out", "w_o", "w_ple_gate", "w_ple_proj")
LAYER0_FIRST = ("ffn1_w_gu", "ffn1_w_down")
LAYER0_IN_FFN1_GU = ("w_in", "w_conv_out", "w_uq", "w_ukv")
LAYER0_IN_FFN1_DOWN = ("ffn2_w_down", "w_mla_out")
LAYER0_IN_PROJ = ("ffn2_w_gu",)
LAYER0_IN_ATTENTION = ("w_o", "w_ple_gate", "w_ple_proj")


def _gather_plan(layer, depth):
    plan = {}
    if layer == 0:
        plan["ffn1_gu"] = [(k, 0) for k in LAYER0_IN_FFN1_GU]
        plan["ffn1_down"] = [(k, 0) for k in LAYER0_IN_FFN1_DOWN]
        plan["in"] = [(k, 0) for k in LAYER0_IN_PROJ]
        plan["attn"] = [(k, 0) for k in LAYER0_IN_ATTENTION]
    if layer + 1 < depth:
        plan["attn"] = plan.get("attn", []) + [(k, layer + 1) for k in IN_ATTENTION]
        plan["ffn2_gu"] = [(k, layer + 1) for k in BEFORE_FFN1_GU]
        plan["ffn2_down"] = [(k, layer + 1) for k in BEFORE_FFN1_DOWN]
    return plan


def _plan_operands(bufs, todo):
    names = list(dict.fromkeys(name for name, _ in todo))
    return [bufs[k] for k in names], names, [(names.index(name), layer) for name, layer in todo]


def _rows_view(buf):
    return buf.reshape(buf.shape[0], N_CHIPS * buf.shape[2], buf.shape[3])


def _cols_view(buf, layer):
    return _chips_to_cols(buf[layer])


def _ffn_fwd(h, n, bufs, which, layer, next_gain, plan):
    tag = which
    rider = None
    if plan.get(which + "_gu"):
        ops, names, pieces = _plan_operands(bufs, plan[which + "_gu"])
        rider = _gather_rider(ops, pieces)
    outs = _gu_swiglu_fwd(n, bufs[which + "_w_gu"], layer, tag + "_gu_fwd", rider=rider)
    gu, a = outs[0], outs[1]
    if rider is not None:
        bufs = {**bufs, **dict(zip(names, outs[2]))}
    rider = None
    if plan.get(which + "_down"):
        ops, names, pieces = _plan_operands(bufs, plan[which + "_down"])
        rider = _gather_rider(ops, pieces)
    outs = _mm(a, _rows_view(bufs[which + "_w_down"]), "nn", F32, tag + "_down_fwd", scale=0.5, res=h, layer=layer,
               norm_gain=next_gain, rider=rider)
    if rider is not None:
        bufs = {**bufs, **dict(zip(names, outs[2]))}
    return outs[0], outs[1], (h, n, gu, a), bufs


def _ffn_bwd(dh, dhb, saved, gain, bufs, which, layer, swap=None):
    tag = which
    h, n, gu, a = saved
    d_wdown = _mm(a, dhb, "tn", BF16, tag + "_down_dw", scale=0.5)
    dgu = _down_dx_swiglu_bwd(dhb, _rows_view(bufs[which + "_w_down"]), gu, layer, tag + "_down_dx")
    rider = _pair_swap_rider(swap) if swap is not None else None
    outs = _mm(n, dgu, "tn", BF16, tag + "_gu_dw", out_chip=True, b_halves=True, rider=rider)
    d_wgu, swapped = (outs[0], outs[1]) if rider is not None else (outs, None)
    dh, dhb, dgain = _mm(dgu, bufs[which + "_w_gu"], "nt", BF16, tag + "_gu_dx", layer=layer, b_chip=True, a_halves=True,
                         norm_bwd=(h, gain, dh))
    return dh, dhb, d_wgu, d_wdown, dgain, swapped


def _layer_fwd(h0, n0, p_i, bufs, conv_w, norms, layer, lay, tables, depth, next_gain):
    plan = _gather_plan(layer, depth)
    h1, n2, s_ffn1, bufs = _ffn_fwd(h0, n0, bufs, "ffn1", layer, norms["mix_norm"], plan)
    w_in = _w_in_to_layout(_cols_view(bufs["w_in"], layer), lay)
    w_conv_out = _cols_view(bufs["w_conv_out"], layer)
    w_uq = _w_uq_to_layout(_cols_view(bufs["w_uq"], layer))
    w_ukv = _w_ukv_to_layout(_cols_view(bufs["w_ukv"], layer))
    if plan.get("in"):
        ops, names, pieces = _plan_operands(bufs, plan["in"])
        proj, gathered = _mm(n2, w_in, "nn", BF16, "in_fwd", rider=_gather_rider(ops, pieces))
        bufs = {**bufs, **dict(zip(names, gathered))}
    else:
        proj = _mm(n2, w_in, "nn", BF16, "in_fwd")
    cb = _conv_fwd(proj, conv_w)
    ya = _mm(cb, w_conv_out, "nn", BF16, "conv_out_fwd")
    qn, kvn = _qkvnorm_fwd(proj, lay, norms["q_norm"], norms["kv_norm"])
    qf = _mm(qn, w_uq, "nn", BF16, "uq_fwd")
    kv = _mm(kvn, w_ukv, "nn", BF16, "ukv_fwd")
    qr, kf = _rope_fwd(qf, kv, proj, lay, tables)
    if plan.get("attn"):
        ops, names, pieces = _plan_operands(bufs, plan["attn"])
        o, lse, gathered = _attn_fwd(qr, kf, kv, gather=(ops, pieces))
        bufs = {**bufs, **dict(zip(names, gathered))}
    else:
        o, lse, _ = _attn_fwd(qr, kf, kv)
    yb = _mm(o, _rows_view(bufs["w_mla_out"]), "nn", BF16, "mla_out_fwd", layer=layer)
    mg = _merge_fwd(proj, lay, ya, yb)
    h2, n3 = _mm(mg, _rows_view(bufs["w_o"]), "nn", F32, "o_fwd", res=h1, layer=layer, norm_gain=norms["ffn2_norm"])
    h3, n4, s_ffn2, bufs = _ffn_fwd(h2, n3, bufs, "ffn2", layer, norms["ple_norm"], plan)
    w_ple_proj = _cols_view(bufs["w_ple_proj"], layer)
    pp = _mm(p_i, w_ple_proj, "nn", BF16, "ple_proj_fwd")
    outs = _mm(n4, _rows_view(bufs["w_ple_gate"]), "nn", BF16, "ple_gate_fwd", layer=layer,
               epilogue=_ple_fwd_epilogue(h3, pp, next_gain))
    gp, h4, n_out = outs[0], outs[1], (outs[2] if next_gain is not None else None)
    saved = dict(s_ffn1=s_ffn1, h1=h1, n2=n2, proj=proj, cb=cb, ya=ya, qn=qn, kvn=kvn, qr=qr, kf=kf, kv=kv, o=o,
                 lse=lse, yb=yb, mg=mg, h2=h2, s_ffn2=s_ffn2, h3=h3, n4=n4, gp=gp, pp=pp, p=p_i,
                 w_in=w_in, w_conv_out=w_conv_out, w_uq=w_uq, w_ukv=w_ukv)
    return h4, n_out, saved, bufs


def _layer_bwd(dh, s, bufs, conv_w, norms, layer, lay, tables, above, gstacks):
    gw, gn = {}, {}

    def by_rows(g):
        return g.reshape(N_CHIPS, g.shape[0] // N_CHIPS, g.shape[1])

    dpp, dgp = _ple_bwd(dh, s["gp"], s["pp"])
    gw["w_ple_proj"] = _cols_to_chips(_mm(s["p"], dpp, "tn", BF16, "ple_proj_dw"))
    gw["w_ple_gate"] = by_rows(_mm(s["n4"], dgp, "tn", BF16, "ple_gate_dw"))
    dh, dhb, gn["ple_norm"] = _mm(dgp, _rows_view(bufs["w_ple_gate"]), "nt", BF16, "ple_gate_dx", layer=layer,
                                  norm_bwd=(s["h3"], norms["ple_norm"], dh))
    dh, dhb, gw["ffn2_w_gu"], g_down, gn["ffn2_norm"], swapped = _ffn_bwd(
        dh, dhb, s["s_ffn2"], norms["ffn2_norm"], bufs, "ffn2", layer, swap=None if above is None else above[1])
    gw["ffn2_w_down"] = by_rows(g_down)
    exchange = None
    if above is not None:
        pairs = [_pair_add(g, o) for g, o in zip(above[1], swapped)]
        exchange = ([pb for pb, _ in pairs], [land for _, land in pairs])
    gw["w_o"] = by_rows(_mm(s["mg"], dhb, "tn", BF16, "o_dw"))
    dya, dyb, dgc, dgm = _mm(dhb, _rows_view(bufs["w_o"]), "nt", BF16, "o_dx", layer=layer,
                             epilogue=_merge_bwd_epilogue(s["proj"], lay, s["ya"], s["yb"]))
    gw["w_conv_out"] = _cols_to_chips(_mm(s["cb"], dya, "tn", BF16, "conv_out_dw"))
    dcb = _mm(dya, s["w_conv_out"], "nt", BF16, "conv_out_dx")
    db, dc, dv_conv, g_conv = _conv_bwd(s["proj"], conv_w, dcb)
    gw["w_mla_out"] = by_rows(_mm(s["o"], dyb, "tn", BF16, "mla_out_dw"))
    do, delta = _mm(dyb, _rows_view(bufs["w_mla_out"]), "nt", BF16, "mla_out_dx", layer=layer,
                    epilogue=_delta_epilogue(s["o"]))
    dkf, dv, dqr, landed = _attn_bwd(s["qr"], s["kf"], s["kv"], do, s["lse"], delta, exchange=exchange)
    dqf, dkv, dkr = _rope_bwd(dqr, dkf, dv, tables)
    gw["w_uq"] = _cols_to_chips(_w_uq_from_layout(_mm(s["qn"], dqf, "tn", BF16, "uq_dw")))
    dqn = _mm(dqf, s["w_uq"], "nt", BF16, "uq_dx")
    gw["w_ukv"] = _cols_to_chips(_w_ukv_from_layout(_mm(s["kvn"], dkv, "tn", BF16, "ukv_dw")))
    dkvn = _mm(dkv, s["w_ukv"], "nt", BF16, "ukv_dx")
    dqc, dkvc, gn["q_norm"], gn["kv_norm"] = _qkvnorm_bwd(s["proj"], lay, norms["q_norm"], norms["kv_norm"], dqn, dkvn)
    t = dh.shape[0]
    dproj = jnp.concatenate([db, dc, dv_conv, dqc, dkr, dgc, dgm, dkvc,
                             jnp.zeros((t, lay["width"] - lay["kv"] - lay["kvl"]), BF16)], axis=1)
    rider = None
    if above is not None:
        gstacks = [_sum_chips(land, gs, above[0]) for land, gs in zip(landed, gstacks)]
        rider = _pair_gather_rider(gstacks, above[0])
    outs = _mm(s["n2"], dproj, "tn", BF16, "in_dw", rider=rider)
    g_in, gstacks = (outs[0], outs[1]) if rider is not None else (outs, gstacks)
    gw["w_in"] = _cols_to_chips(_w_in_from_layout(g_in, lay))
    dh, dhb, gn["mix_norm"] = _mm(dproj, s["w_in"], "nt", BF16, "in_dx", norm_bwd=(s["h1"], norms["mix_norm"], dh))
    dh, dhb, gw["ffn1_w_gu"], g_down, gn["ffn1_norm"], _ = _ffn_bwd(
        dh, dhb, s["s_ffn1"], norms["ffn1_norm"], bufs, "ffn1", layer)
    gw["ffn1_w_down"] = by_rows(g_down)
    return dh, gw, g_conv, gn, gstacks


def _rope_tables(positions):
    half = ROPE_DIM // 2
    inv_freq = ROPE_THETA ** (-jnp.arange(0, ROPE_DIM, 2, dtype=F32) / ROPE_DIM)
    ang = positions.astype(F32)[:, None] * inv_freq
    cos, sin = jnp.cos(ang), jnp.sin(ang)
    zeros = jnp.zeros_like(cos)
    cos_t = jnp.concatenate([cos, cos, zeros, zeros], axis=1)
    sin_a = jnp.concatenate([-sin, zeros, zeros, zeros], axis=1)
    sin_b = jnp.concatenate([zeros, sin, zeros, zeros], axis=1)
    assert cos_t.shape[1] == LANES and half * 4 == LANES
    return cos_t, sin_a, sin_b


def kernel(x, p, positions, ffn1_norm, ffn1_w_gu, ffn1_w_down, mix_norm, w_in, conv_w, w_conv_out, q_norm, kv_norm, w_uq, w_ukv, w_mla_out, w_o, ffn2_norm, ffn2_w_gu, ffn2_w_down, ple_norm, w_ple_gate, w_ple_proj, final_norm, loss_target, m_ffn1_norm, m_ffn1_w_gu, m_ffn1_w_down, m_mix_norm, m_w_in, m_conv_w, m_w_conv_out, m_q_norm, m_kv_norm, m_w_uq, m_w_ukv, m_w_mla_out, m_w_o, m_ffn2_norm, m_ffn2_w_gu, m_ffn2_w_down, m_ple_norm, m_w_ple_gate, m_w_ple_proj, m_final_norm, v_ffn1_norm, v_ffn1_w_gu, v_ffn1_w_down, v_mix_norm, v_w_in, v_conv_w, v_w_conv_out, v_q_norm, v_kv_norm, v_w_uq, v_w_ukv, v_w_mla_out, v_w_o, v_ffn2_norm, v_ffn2_w_gu, v_ffn2_w_down, v_ple_norm, v_w_ple_gate, v_w_ple_proj, v_final_norm):
    args = dict(zip(ARG_NAMES, (x, p, positions, ffn1_norm, ffn1_w_gu, ffn1_w_down, mix_norm, w_in, conv_w, w_conv_out, q_norm, kv_norm, w_uq, w_ukv, w_mla_out, w_o, ffn2_norm, ffn2_w_gu, ffn2_w_down, ple_norm, w_ple_gate, w_ple_proj, final_norm, loss_target, m_ffn1_norm, m_ffn1_w_gu, m_ffn1_w_down, m_mix_norm, m_w_in, m_conv_w, m_w_conv_out, m_q_norm, m_kv_norm, m_w_uq, m_w_ukv, m_w_mla_out, m_w_o, m_ffn2_norm, m_ffn2_w_gu, m_ffn2_w_down, m_ple_norm, m_w_ple_gate, m_w_ple_proj, m_final_norm, v_ffn1_norm, v_ffn1_w_gu, v_ffn1_w_down, v_mix_norm, v_w_in, v_conv_w, v_w_conv_out, v_q_norm, v_kv_norm, v_w_uq, v_w_ukv, v_w_mla_out, v_w_o, v_ffn2_norm, v_ffn2_w_gu, v_ffn2_w_down, v_ple_norm, v_w_ple_gate, v_w_ple_proj, v_final_norm)))
    depth = ffn1_norm.shape[0]
    t, d = x.shape[1], x.shape[2]
    conv = conv_w.shape[-1] * N_CHIPS
    lay = _in_layout(conv, q_norm.shape[-1], kv_norm.shape[-1], d)
    chip = 2 * lax.axis_index("x") + lax.axis_index("y")
    tables = _rope_tables(positions[0])

    bufs = {name: _cast_into_slot(args[name]) for name in BIG}
    bufs.update(zip(LAYER0_FIRST, _all_gather_weights([bufs[k] for k in LAYER0_FIRST], [(i, 0) for i in range(len(LAYER0_FIRST))])))
    conv_rows = depth * conv_w.shape[1]
    conv_all = _all_gather_small(_to_rows128(conv_w.reshape(-1)), "all_gather_conv_w")[1]
    conv_full = conv_all[0::2, :conv_rows].reshape(N_CHIPS, depth, conv_w.shape[1], LANES)
    conv_full = conv_full.transpose(1, 2, 0, 3).reshape(depth, conv_w.shape[1], conv)
    norms = [{name: args[name][i] for name in REPLICATED} for i in range(depth)]
    p3 = p.reshape(depth, t, p.shape[-1])

    h = x[0]
    n = _rmsnorm_fwd(h, norms[0]["ffn1_norm"], "first_norm_fwd")
    saved = []
    for i in range(depth):
        h, n, s, bufs = _layer_fwd(h, n, p3[i], bufs, conv_full[i], norms[i], i, lay, tables, depth,
                                   norms[i + 1]["ffn1_norm"] if i + 1 < depth else None)
        saved.append(s)
    loss_part, dh, _, g_final = _loss_head(h, final_norm, loss_target[0])
    loss = lax.psum(loss_part[0, 0], ("x", "y", "c"))

    gstacks = [lax.empty(args[name].shape, F32) for name in BIG]
    norm_grads, conv_grads = [None] * depth, [None] * depth
    above = None
    for i in reversed(range(depth)):
        dh, gw, conv_grads[i], norm_grads[i], gstacks = _layer_bwd(
            dh, saved[i], bufs, conv_full[i], norms[i], i, lay, tables, above, gstacks)
        above = (i, [gw[name] for name in BIG])
    pairs = [_pair_add(g, o) for g, o in zip(above[1], _pair_swap(above[1]))]
    landed = _chip_all_to_all([pb for pb, _ in pairs], [land for _, land in pairs])
    gstacks = _pair_gather([_sum_chips(land, gs, above[0]) for land, gs in zip(landed, gstacks)], above[0])
    grad_x = dh[None]
    grads = dict(zip(BIG, gstacks))

    pieces = [norm_grads[i][name].reshape(-1) for i in range(depth) for name in REPLICATED]
    pieces += [g_final.reshape(-1)] + [conv_grads[i].reshape(-1) for i in range(depth)]
    vec = _all_gather_small(_to_rows128(jnp.concatenate(pieces)), "all_sum_small")[0].reshape(-1)
    off = 0
    per_name = {name: [] for name in REPLICATED}
    for i in range(depth):
        for name in REPLICATED:
            size = args[name].shape[1]
            per_name[name].append(vec[off:off + size])
            off += size
    for name in REPLICATED:
        grads[name] = jnp.stack(per_name[name])
    grads["final_norm"] = vec[off:off + d]
    off += d
    conv_g = vec[off:off + depth * 3 * conv].reshape(depth, 3, conv)
    grads["conv_w"] = lax.dynamic_slice_in_dim(conv_g, chip * conv_w.shape[-1], conv_w.shape[-1], axis=2)

    delta, new_m, new_v = {}, {}, {}
    for name in WEIGHTS:
        w_, g_, m_, v_ = args[name], grads[name], args["m_" + name], args["v_" + name]
        if w_.ndim == 1:
            outs = _adamw(w_[None], g_[None], m_[None], v_[None])
            delta[name], new_m[name], new_v[name] = (o[0] for o in outs)
        else:
            delta[name], new_m[name], new_v[name] = _adamw(w_, g_, m_, v_)
    return (loss, grad_x, *[grads[n] for n in WEIGHTS], *[delta[n] for n in WEIGHTS],
            *[new_m[n] for n in WEIGHTS], *[new_v[n] for n in WEIGHTS])
```

```python
import functools

import jax
import jax.numpy as jnp
from jax import lax
from jax.experimental import pallas as pl
from jax.experimental.pallas import tpu as pltpu

BF16 = jnp.bfloat16
F32 = jnp.float32
SDS = jax.ShapeDtypeStruct
MESH = pl.DeviceIdType.MESH

N_HEADS = 8
NOPE_DIM = 128
ROPE_DIM = 64
V_DIM = 128
QK_PAD = 256
CHUNK = 64
ROPE_THETA = 10000.0
EPS = 1e-6
ATTN_SCALE = (NOPE_DIM + ROPE_DIM) ** -0.5
NEG_BIG = -1e30

ADAM_LR = 0.001
ADAM_B1 = 0.9
ADAM_B2 = 0.999
ADAM_EPS = 1e-08
ADAM_WD = 0.01
ADAM_STEP = 10

LANES = 128
N_CHIPS = 4
VMEM_LIMIT_BYTES = 56 * 1024 * 1024
ACC_BYTES = 6 * 1024 * 1024
BLOCK_ELEMS = 1 << 19

SHARDED = (("ffn1_w_gu", 1), ("ffn1_w_down", 0), ("w_in", 1), ("w_conv_out", 1), ("w_uq", 1), ("w_ukv", 1),
           ("w_mla_out", 0), ("w_o", 0), ("ffn2_w_gu", 1), ("ffn2_w_down", 0), ("w_ple_gate", 0), ("w_ple_proj", 1))
BIG = tuple(name for name, _ in SHARDED)
REPLICATED = ("ffn1_norm", "mix_norm", "q_norm", "kv_norm", "ffn2_norm", "ple_norm")
WEIGHTS = ("ffn1_norm", "ffn1_w_gu", "ffn1_w_down", "mix_norm", "w_in", "conv_w", "w_conv_out", "q_norm",
           "kv_norm", "w_uq", "w_ukv", "w_mla_out", "w_o", "ffn2_norm", "ffn2_w_gu", "ffn2_w_down",
           "ple_norm", "w_ple_gate", "w_ple_proj", "final_norm")
ARG_NAMES = ("x", "p", "positions") + WEIGHTS + ("loss_target",) + tuple("m_" + n for n in WEIGHTS) + tuple(
    "v_" + n for n in WEIGHTS)


def _cparams(semantics=None):
    return pltpu.CompilerParams(dimension_semantics=semantics, vmem_limit_bytes=VMEM_LIMIT_BYTES)


def _tile(n, cap, mult=LANES):
    best = None
    for t in range(mult, min(n, cap) + 1, mult):
        if n % t == 0:
            best = t
    return n if best is None else best


def _sigmoid(x):
    return 1.0 / (1.0 + jnp.exp(-x))


def _rowspec(tm, width, col_block=0):
    return pl.BlockSpec((tm, width), lambda i: (i, col_block))


def _colspec(tm, width, offset):
    assert offset % width == 0, (width, offset)
    return _rowspec(tm, width, offset // width)


def _mm(a, b, mode, out_dtype, name, scale=None, res=None, layer=None, b_chip=False, out_chip=False, norm_gain=None,
        a_halves=False, b_halves=False, norm_bwd=None, rider=None, epilogue=None):
    bshape = b.shape if layer is None else b.shape[1:]
    if b_chip:
        bshape = (bshape[1], N_CHIPS * bshape[2])
    if b_halves:
        bshape = (bshape[1], 2 * bshape[2])
    ashape = (a.shape[1], 2 * a.shape[2]) if a_halves else a.shape
    if mode == "nn":
        (m, k), (k2, n) = ashape, bshape
    elif mode == "nt":
        (m, k), (n, k2) = ashape, bshape
    else:
        (k, m), (k2, n) = ashape, bshape
    assert k == k2, (a.shape, b.shape, mode)
    n_unit = n // N_CHIPS if (out_chip or (b_chip and mode == "nn")) else n
    k_unit = k // N_CHIPS if (b_chip and mode == "nt") else k
    tn = _tile(n_unit, 1536)
    tm = _tile(m, min(512 if (norm_bwd is not None or epilogue is not None) else 1408, ACC_BYTES // (4 * tn)))
    tk = _tile(k_unit, 1536)
    nk = k // tk
    n_per, k_per = n_unit // tn, k_unit // tk
    n_half, k_half = n // 2 // tn, k // 2 // tk
    dims = {"nn": (((1,), (0,)), ((), ())), "nt": (((1,), (1,)), ((), ())), "tn": (((0,), (0,)), ((), ()))}[mode]

    fuse_norm = norm_gain is not None
    fuse_bwd = norm_bwd is not None
    assert not (fuse_norm or fuse_bwd or epilogue is not None) or (tn == n and not out_chip), (name, tn, n)
    assert epilogue is None or not (fuse_norm or fuse_bwd)
    assert not a_halves or mode == "nt"
    assert not b_halves or mode == "tn"
    n_extra_in = (1 if res is not None else 0) + (1 if fuse_norm else 0) + (3 if fuse_bwd else 0)
    n_outs = 3 if fuse_bwd else (2 if fuse_norm else 1)
    if epilogue is not None:
        n_extra_in += len(epilogue.inputs)
        n_outs = (1 if epilogue.keep_product else 0) + len(epilogue.out_shapes)
    grid = (m // tm, n // tn, nk)
    if fuse_bwd and nk > 1:
        assert rider is None and res is None and scale is None
        return _mm_norm_bwd_k_outer(a, b, mode, name, layer, b_chip, a_halves, norm_bwd, (m, n, k), (tm, tk), k_per, k_half)

    def body(*refs):
        a_ref, b_ref = refs[0], refs[1]
        extra = list(refs[2:2 + n_extra_in])
        rest = refs[2 + n_extra_in:]
        if rider is not None:
            rest = rider.split(rest, n_outs, 1 if nk > 1 else 0, grid)
        outs = list(rest)
        acc_ref = outs.pop() if nk > 1 else None
        res_ref = extra.pop(0) if res is not None else None
        gain_ref = extra.pop(0) if fuse_norm else None

        def finish(acc):
            if scale is not None:
                acc = acc * scale
            if res_ref is not None:
                acc = res_ref[...] + acc
            if fuse_bwd:
                h_ref, g_ref, dhin_ref = extra
                dh_ref, dhb_ref, dg_ref = outs
                dx, dg = _rn_bwd_math(h_ref[...], g_ref[...], acc)
                dh = dhin_ref[...] + dx
                dh_ref[...] = dh
                dhb_ref[...] = dh.astype(BF16)

                @pl.when(pl.program_id(0) == 0)
                def _():
                    dg_ref[...] = dg

                @pl.when(pl.program_id(0) > 0)
                def _():
                    dg_ref[...] += dg
                return
            if epilogue is not None:
                if epilogue.keep_product:
                    outs[0][...] = acc.astype(out_dtype)
                epilogue.fn(acc, extra, outs[1:] if epilogue.keep_product else outs)
                return
            outs[0][...] = acc.astype(out_dtype)
            if fuse_norm:
                outs[1][...] = _rn_fwd_math(acc, gain_ref[...]).astype(BF16)

        part = lax.dot_general(a_ref[...].astype(BF16), b_ref[...].astype(BF16), dims,
                               preferred_element_type=F32)
        if nk == 1:
            finish(part)
        else:
            kk = pl.program_id(2)

            @pl.when(kk == 0)
            def _():
                acc_ref[...] = part

            @pl.when(kk > 0)
            def _():
                acc_ref[...] += part

            @pl.when(kk == nk - 1)
            def _():
                finish(acc_ref[...])
        if rider is not None:
            rider.finish_at_last_step(grid)

    lead = () if layer is None else (layer,)
    lead_block = () if layer is None else (None,)
    if mode == "nn":
        a_spec = pl.BlockSpec((tm, tk), lambda i, j, kk: (i, kk))
        if b_chip:
            b_spec = pl.BlockSpec(lead_block + (None, tk, tn), lambda i, j, kk: lead + (j // n_per, kk, j % n_per))
        else:
            b_spec = pl.BlockSpec(lead_block + (tk, tn), lambda i, j, kk: lead + (kk, j))
    elif mode == "nt":
        if a_halves:
            a_spec = pl.BlockSpec((None, tm, tk), lambda i, j, kk: (kk // k_half, i, kk % k_half))
        else:
            a_spec = pl.BlockSpec((tm, tk), lambda i, j, kk: (i, kk))
        if b_chip:
            b_spec = pl.BlockSpec(lead_block + (None, tn, tk), lambda i, j, kk: lead + (kk // k_per, j, kk % k_per))
        else:
            b_spec = pl.BlockSpec(lead_block + (tn, tk), lambda i, j, kk: lead + (j, kk))
    else:
        assert layer is None and not b_chip
        a_spec = pl.BlockSpec((tk, tm), lambda i, j, kk: (kk, i))
        if b_halves:
            b_spec = pl.BlockSpec((None, tk, tn), lambda i, j, kk: (j // n_half, kk, j % n_half))
        else:
            b_spec = pl.BlockSpec((tk, tn), lambda i, j, kk: (kk, j))
    if out_chip:
        o_spec = pl.BlockSpec((None, tm, tn), lambda i, j, kk: (j // n_per, i, j % n_per))
        out_shape = SDS((N_CHIPS, m, n_unit), out_dtype)
    else:
        o_spec = pl.BlockSpec((tm, tn), lambda i, j, kk: (i, j))
        out_shape = SDS((m, n), out_dtype)
    in_specs = [a_spec, b_spec] + ([o_spec] if res is not None else [])
    operands = (a, b) + ((res,) if res is not None else ())
    out_specs = o_spec
    vec = pl.BlockSpec((1, tn), lambda i, j, kk: (0, j))
    if fuse_norm:
        in_specs.append(vec)
        operands += (norm_gain.reshape(1, n),)
        out_shape, out_specs = (out_shape, SDS((m, n), BF16)), (o_spec, o_spec)
    if epilogue is not None:
        in_specs += [pl.BlockSpec(blk, (lambda i, j, kk, f=f: f(i))) for _, blk, f in epilogue.inputs]
        operands += tuple(arr for arr, _, _ in epilogue.inputs)
        ep_specs = tuple(pl.BlockSpec(blk, (lambda i, j, kk, f=f: f(i))) for _, blk, f in epilogue.out_shapes)
        ep_shapes = tuple(sds for sds, _, _ in epilogue.out_shapes)
        out_shape = ((out_shape,) if epilogue.keep_product else ()) + ep_shapes
        out_specs = ((o_spec,) if epilogue.keep_product else ()) + ep_specs
    if fuse_bwd:
        h, gain, dh_in = norm_bwd
        in_specs += [o_spec, vec, o_spec]
        operands += (h, gain.reshape(1, n), dh_in)
        out_shape = (SDS((m, n), F32), SDS((m, n), BF16), SDS((1, n), F32))
        out_specs = (o_spec, o_spec, vec)
    scratch = [pltpu.VMEM((tm, tn), F32)] if nk > 1 else []
    semantics = ("arbitrary",) * 3 if fuse_bwd else ("parallel", "parallel", "arbitrary")
    if rider is None:
        return pl.pallas_call(
            body, out_shape=out_shape, grid=grid, in_specs=in_specs, out_specs=out_specs, scratch_shapes=scratch,
            compiler_params=_cparams(semantics), name=name)(*operands)
    out_shape = out_shape if isinstance(out_shape, tuple) else (out_shape,)
    out_specs = out_specs if isinstance(out_specs, tuple) else (out_specs,)
    outs = pl.pallas_call(
        body, out_shape=out_shape + tuple(rider.out_shapes), grid=grid, in_specs=in_specs + [ANY] * len(rider.operands),
        out_specs=out_specs + (ANY,) * len(rider.out_shapes), scratch_shapes=scratch + rider.sem_shapes,
        input_output_aliases={len(operands) + i: len(out_shape) + o for i, o in rider.aliases.items()},
        compiler_params=_cparams(("arbitrary",) * 3), name=name + rider.tag)(*operands, *rider.operands)
    return tuple(outs[:n_outs]) + (list(outs[n_outs:]),)


class _RowEpilogue:
    def __init__(self, inputs, out_shapes, fn, keep_product):
        self.inputs, self.out_shapes, self.fn, self.keep_product = list(inputs), list(out_shapes), fn, keep_product


def _rows(arr, tm_of):
    return (arr, (tm_of, arr.shape[1]), lambda i: (i, 0))


def _epilogue_rows(t):
    return _tile(t, 512)


def _delta_epilogue(o):
    t = o.shape[0]
    tm = _epilogue_rows(t)

    def fn(acc, ins, outs):
        prod = acc * ins[0][...].astype(F32)
        for h in range(N_HEADS):
            part = jnp.sum(prod[:, h * V_DIM:(h + 1) * V_DIM], axis=1, keepdims=True)
            outs[0][h] = jnp.broadcast_to(part, (tm, LANES))

    return _RowEpilogue([_rows(o, tm)], [(SDS((N_HEADS, t, LANES), F32), (N_HEADS, tm, LANES), lambda i: (0, i, 0))], fn, True)


def _merge_bwd_epilogue(proj, lay, ya, yb):
    t, d = ya.shape
    tm = _epilogue_rows(t)
    assert lay["gc"] % d == 0 and lay["gm"] % d == 0

    def fn(acc, ins, outs):
        gc, gm, ya_, yb_ = (r[...].astype(F32) for r in ins)
        sc, sm = _sigmoid(gc), _sigmoid(gm)
        outs[0][...] = (acc * sc).astype(BF16)
        outs[1][...] = (acc * sm).astype(BF16)
        outs[2][...] = (acc * ya_ * (sc * (1.0 - sc))).astype(BF16)
        outs[3][...] = (acc * yb_ * (sm * (1.0 - sm))).astype(BF16)

    gate = [(proj, (tm, d), lambda i, c=lay[k] // d: (i, c)) for k in ("gc", "gm")]
    out = (SDS((t, d), BF16), (tm, d), lambda i: (i, 0))
    return _RowEpilogue(gate + [_rows(ya, tm), _rows(yb, tm)], [out] * 4, fn, False)


def _ple_fwd_epilogue(h, pp, norm_gain):
    t, d = h.shape
    tm = _epilogue_rows(t)
    fuse_norm = norm_gain is not None

    def fn(acc, ins, outs):
        gp = acc.astype(BF16).astype(F32)
        out = ins[0][...] + _sigmoid(gp) * ins[1][...].astype(F32)
        outs[0][...] = out
        if fuse_norm:
            outs[1][...] = _rn_fwd_math(out, ins[2][...]).astype(BF16)

    inputs = [_rows(h, tm), _rows(pp, tm)]
    outs = [(SDS((t, d), F32), (tm, d), lambda i: (i, 0))]
    if fuse_norm:
        inputs.append((norm_gain.reshape(1, d), (1, d), lambda i: (0, 0)))
        outs.append((SDS((t, d), BF16), (tm, d), lambda i: (i, 0)))
    return _RowEpilogue(inputs, outs, fn, True)


def _mm_norm_bwd_k_outer(a, b, mode, name, layer, b_chip, a_halves, norm_bwd, sizes, tiles, k_per, k_half):
    (m, n, k), (tm, tk) = sizes, tiles
    nk, ni = k // tk, m // tm
    h, gain, dh_in = norm_bwd
    dims = {"nn": (((1,), (0,)), ((), ())), "nt": (((1,), (1,)), ((), ()))}[mode]

    def body(a_ref, b_ref, h_ref, g_ref, dhin_ref, dh_ref, dhb_ref, dg_ref, acc_ref):
        kk, i = pl.program_id(0), pl.program_id(1)
        rows = pl.ds(pl.multiple_of(i * tm, tm), tm)
        part = lax.dot_general(a_ref[...].astype(BF16), b_ref[...].astype(BF16), dims, preferred_element_type=F32)

        @pl.when(kk == 0)
        def _():
            acc_ref[rows, :] = part

        @pl.when((kk > 0) & (kk < nk - 1))
        def _():
            acc_ref[rows, :] += part

        @pl.when(kk == nk - 1)
        def _():
            dx, dg = _rn_bwd_math(h_ref[...], g_ref[...], acc_ref[rows, :] + part)
            dh = dhin_ref[...] + dx
            dh_ref[...] = dh
            dhb_ref[...] = dh.astype(BF16)

            @pl.when(i == 0)
            def _():
                dg_ref[...] = dg

            @pl.when(i > 0)
            def _():
                dg_ref[...] += dg

    lead = () if layer is None else (layer,)
    lead_block = () if layer is None else (None,)
    if a_halves:
        a_spec = pl.BlockSpec((None, tm, tk), lambda kk, i: (kk // k_half, i, kk % k_half))
    else:
        a_spec = pl.BlockSpec((tm, tk), lambda kk, i: (i, kk))
    if mode == "nn":
        assert not b_chip
        b_spec = pl.BlockSpec(lead_block + (tk, n), lambda kk, i: lead + (kk, 0))
    elif b_chip:
        b_spec = pl.BlockSpec(lead_block + (None, n, tk), lambda kk, i: lead + (kk // k_per, 0, kk % k_per))
    else:
        b_spec = pl.BlockSpec(lead_block + (n, tk), lambda kk, i: lead + (0, kk))
    late = pl.BlockSpec((tm, n), lambda kk, i: (jnp.where(kk == nk - 1, i, 0), 0))
    vec = pl.BlockSpec((1, n), lambda kk, i: (0, 0))
    return pl.pallas_call(
        body, out_shape=(SDS((m, n), F32), SDS((m, n), BF16), SDS((1, n), F32)), grid=(nk, ni),
        in_specs=[a_spec, b_spec, late, vec, late], out_specs=(late, late, vec),
        scratch_shapes=[pltpu.VMEM((m, n), F32)], compiler_params=_cparams(("arbitrary", "arbitrary")),
        name=name)(a, b, h, gain.reshape(1, n), dh_in)


def _rn_fwd_math(x, g):
    r = lax.rsqrt(jnp.mean(x * x, axis=-1, keepdims=True) + EPS)
    return (x * r) * g


def _rn_bwd_math(x, g, dn):
    r = lax.rsqrt(jnp.mean(x * x, axis=-1, keepdims=True) + EPS)
    xh = x * r
    gy = dn * g
    dx = r * (gy - xh * jnp.mean(gy * xh, axis=-1, keepdims=True))
    dg = jnp.sum(dn * xh, axis=0, keepdims=True)
    return dx, dg


def _rmsnorm_fwd(h, gain, name):
    t, d = h.shape
    tm = _tile(t, 512, 8)

    def body(h_ref, g_ref, o_ref):
        o_ref[...] = _rn_fwd_math(h_ref[...], g_ref[...]).astype(BF16)

    return pl.pallas_call(
        body, out_shape=SDS((t, d), BF16), grid=(t // tm,),
        in_specs=[_rowspec(tm, d), pl.BlockSpec((1, d), lambda i: (0, 0))], out_specs=_rowspec(tm, d),
        compiler_params=_cparams(("parallel",)), name=name)(h, gain.reshape(1, d))


def _loss_head(h, gain, target):
    t, d = h.shape
    tm = _tile(t, 512, 8)

    def body(h_ref, g_ref, t_ref, loss_ref, dh_ref, dhb_ref, dg_ref):
        x, g = h_ref[...], g_ref[...]
        err = _rn_fwd_math(x, g) - t_ref[...]
        part = 0.5 * jnp.sum(jnp.sum(err * err, axis=1, keepdims=True), axis=0, keepdims=True) * (1.0 / d)
        dx, dg = _rn_bwd_math(x, g, err * (1.0 / d))
        dh_ref[...] = dx
        dhb_ref[...] = dx.astype(BF16)

        @pl.when(pl.program_id(0) == 0)
        def _():
            dg_ref[...] = dg
            loss_ref[...] = jnp.broadcast_to(part, (1, LANES))

        @pl.when(pl.program_id(0) > 0)
        def _():
            dg_ref[...] += dg
            loss_ref[...] += jnp.broadcast_to(part, (1, LANES))

    vec = pl.BlockSpec((1, d), lambda i: (0, 0))
    return pl.pallas_call(
        body, out_shape=(SDS((1, LANES), F32), SDS((t, d), F32), SDS((t, d), BF16), SDS((1, d), F32)),
        grid=(t // tm,), in_specs=[_rowspec(tm, d), vec, _rowspec(tm, d)],
        out_specs=(pl.BlockSpec((1, LANES), lambda i: (0, 0)), _rowspec(tm, d), _rowspec(tm, d), vec),
        compiler_params=_cparams(("arbitrary",)), name="loss_head")(h, gain.reshape(1, d), target)


def _gu_swiglu_fwd(n, w_gu, layer, name, rider=None):
    t, d = n.shape
    cols = w_gu.shape[3]
    f = 2 * cols
    tn = _tile(cols, 1536)
    tm = _tile(t, 512)
    per = cols // tn

    grid = (f // tn, t // tm)

    def body(n_ref, wg_ref, wu_ref, *rest):
        gu_ref, a_ref = rest[:2] if rider is None else rider.split(rest, 2, 0, grid)
        x = n_ref[...]
        g = jnp.dot(x, wg_ref[...], preferred_element_type=F32)
        u = jnp.dot(x, wu_ref[...], preferred_element_type=F32)
        gu_ref[0] = g.astype(BF16)
        gu_ref[1] = u.astype(BF16)
        a_ref[...] = (g * _sigmoid(g) * u).astype(BF16)
        if rider is not None:
            rider.finish_at_last_step(grid)

    in_specs = [pl.BlockSpec((tm, d), lambda j, i: (i, 0)),
                pl.BlockSpec((None, None, d, tn), lambda j, i: (layer, j // per, 0, j % per)),
                pl.BlockSpec((None, None, d, tn), lambda j, i: (layer, 2 + j // per, 0, j % per))]
    out_shape = (SDS((2, t, f), BF16), SDS((t, f), BF16))
    out_specs = (pl.BlockSpec((2, tm, tn), lambda j, i: (0, i, j)), pl.BlockSpec((tm, tn), lambda j, i: (i, j)))
    if rider is None:
        return pl.pallas_call(body, out_shape=out_shape, grid=grid, in_specs=in_specs, out_specs=out_specs,
                              compiler_params=_cparams(("parallel", "parallel")), name=name)(n, w_gu, w_gu)
    outs = pl.pallas_call(
        body, out_shape=out_shape + tuple(rider.out_shapes), grid=grid, in_specs=in_specs + [ANY] * len(rider.operands),
        out_specs=out_specs + (ANY,) * len(rider.out_shapes), scratch_shapes=rider.sem_shapes,
        input_output_aliases={3 + i: 2 + o for i, o in rider.aliases.items()},
        compiler_params=_cparams(("arbitrary", "arbitrary")), name=name + rider.tag)(n, w_gu, w_gu, *rider.operands)
    return outs[0], outs[1], list(outs[2:])


def _down_dx_swiglu_bwd(dhb, w_down, gu, layer, name):
    t, d = dhb.shape
    f = gu.shape[2]
    tn = _tile(f, 1536)
    tm = _tile(t, 512)

    def body(dh_ref, w_ref, gu_ref, dgu_ref):
        da = 0.5 * lax.dot_general(dh_ref[...], w_ref[...], _NT, preferred_element_type=F32)
        g = gu_ref[0].astype(F32)
        u = gu_ref[1].astype(F32)
        sg = _sigmoid(g)
        dgu_ref[0] = (da * u * (sg * (1.0 + g * (1.0 - sg)))).astype(BF16)
        dgu_ref[1] = (da * (g * sg)).astype(BF16)

    blk = pl.BlockSpec((2, tm, tn), lambda j, i: (0, i, j))
    return pl.pallas_call(
        body, out_shape=SDS((2, t, f), BF16), grid=(f // tn, t // tm),
        in_specs=[pl.BlockSpec((tm, d), lambda j, i: (i, 0)), pl.BlockSpec((None, tn, d), lambda j, i: (layer, j, 0)), blk],
        out_specs=blk, compiler_params=_cparams(("parallel", "parallel")), name=name)(dhb, w_down, gu)


def _shift_down(z, k, row):
    return jnp.where(row >= k, pltpu.roll(z, k, 0), 0.0)


def _shift_up(z, k, row, t):
    return jnp.where(row < t - k, pltpu.roll(z, t - k, 0), 0.0)


def _conv_specs(t, conv):
    nb = conv // LANES
    return [pl.BlockSpec((t, LANES), lambda j: (0, j)), pl.BlockSpec((t, LANES), lambda j: (0, nb + j)),
            pl.BlockSpec((t, LANES), lambda j: (0, 2 * nb + j))]


def _conv_fwd(proj, conv_w):
    t = proj.shape[0]
    conv = conv_w.shape[1]

    def body(b_ref, c_ref, v_ref, w_ref, o_ref):
        z = c_ref[...].astype(F32) * v_ref[...].astype(F32)
        row = lax.broadcasted_iota(jnp.int32, z.shape, 0)
        y = w_ref[0:1, :] * _shift_down(z, 2, row) + w_ref[1:2, :] * _shift_down(z, 1, row) + w_ref[2:3, :] * z
        o_ref[...] = (b_ref[...].astype(F32) * y).astype(BF16)

    cspec = pl.BlockSpec((t, LANES), lambda j: (0, j))
    return pl.pallas_call(
        body, out_shape=SDS((t, conv), BF16), grid=(conv // LANES,),
        in_specs=_conv_specs(t, conv) + [pl.BlockSpec((3, LANES), lambda j: (0, j))], out_specs=cspec,
        compiler_params=_cparams(("parallel",)), name="conv_fwd")(proj, proj, proj, conv_w)


def _conv_bwd(proj, conv_w, dcb):
    t = proj.shape[0]
    conv = conv_w.shape[1]

    def body(b_ref, c_ref, v_ref, w_ref, d_ref, db_ref, dc_ref, dv_ref, dw_ref):
        b, c, v = b_ref[...].astype(F32), c_ref[...].astype(F32), v_ref[...].astype(F32)
        d = d_ref[...].astype(F32)
        z = c * v
        row = lax.broadcasted_iota(jnp.int32, z.shape, 0)
        z1, z2 = _shift_down(z, 1, row), _shift_down(z, 2, row)
        w0, w1, w2 = w_ref[0:1, :], w_ref[1:2, :], w_ref[2:3, :]
        y = w0 * z2 + w1 * z1 + w2 * z
        dy = d * b
        db_ref[...] = (d * y).astype(BF16)
        dz = w2 * dy + w1 * _shift_up(dy, 1, row, t) + w0 * _shift_up(dy, 2, row, t)
        dc_ref[...] = (dz * v).astype(BF16)
        dv_ref[...] = (dz * c).astype(BF16)
        dw_ref[0:1, :] = jnp.sum(dy * z2, axis=0, keepdims=True)
        dw_ref[1:2, :] = jnp.sum(dy * z1, axis=0, keepdims=True)
        dw_ref[2:3, :] = jnp.sum(dy * z, axis=0, keepdims=True)

    cspec = pl.BlockSpec((t, LANES), lambda j: (0, j))
    wspec = pl.BlockSpec((3, LANES), lambda j: (0, j))
    return pl.pallas_call(
        body, out_shape=(SDS((t, conv), BF16),) * 3 + (SDS((3, conv), F32),), grid=(conv // LANES,),
        in_specs=_conv_specs(t, conv) + [wspec, cspec], out_specs=(cspec, cspec, cspec, wspec),
        compiler_params=_cparams(("parallel",)), name="conv_bwd")(proj, proj, proj, conv_w, dcb)


def _qkvnorm_fwd(proj, lay, q_gain, kv_gain):
    t = proj.shape[0]
    ql, kvl = lay["ql"], lay["kvl"]
    tm = _tile(t, 512, 8)

    def body(q_ref, kv_ref, gq_ref, gkv_ref, qn_ref, kvn_ref):
        qn_ref[...] = _rn_fwd_math(q_ref[...].astype(F32), gq_ref[...]).astype(BF16)
        kvn_ref[...] = _rn_fwd_math(kv_ref[...].astype(F32), gkv_ref[...]).astype(BF16)

    return pl.pallas_call(
        body, out_shape=(SDS((t, ql), BF16), SDS((t, kvl), BF16)), grid=(t // tm,),
        in_specs=[_colspec(tm, ql, lay["q"]), _colspec(tm, kvl, lay["kv"]),
                  pl.BlockSpec((1, ql), lambda i: (0, 0)), pl.BlockSpec((1, kvl), lambda i: (0, 0))],
        out_specs=(_rowspec(tm, ql), _rowspec(tm, kvl)), compiler_params=_cparams(("parallel",)),
        name="qkvnorm_fwd")(proj, proj, q_gain.reshape(1, ql), kv_gain.reshape(1, kvl))


def _qkvnorm_bwd(proj, lay, q_gain, kv_gain, dqn, dkvn):
    t = proj.shape[0]
    ql, kvl = lay["ql"], lay["kvl"]
    tm = _tile(t, 512, 8)

    def body(q_ref, kv_ref, gq_ref, gkv_ref, dqn_ref, dkvn_ref, dq_ref, dkv_ref, dgq_ref, dgkv_ref):
        dq, dgq = _rn_bwd_math(q_ref[...].astype(F32), gq_ref[...], dqn_ref[...].astype(F32))
        dkv, dgkv = _rn_bwd_math(kv_ref[...].astype(F32), gkv_ref[...], dkvn_ref[...].astype(F32))
        dq_ref[...] = dq.astype(BF16)
        dkv_ref[...] = dkv.astype(BF16)

        @pl.when(pl.program_id(0) == 0)
        def _():
            dgq_ref[...] = dgq
            dgkv_ref[...] = dgkv

        @pl.when(pl.program_id(0) > 0)
        def _():
            dgq_ref[...] += dgq
            dgkv_ref[...] += dgkv

    vq = pl.BlockSpec((1, ql), lambda i: (0, 0))
    vkv = pl.BlockSpec((1, kvl), lambda i: (0, 0))
    return pl.pallas_call(
        body, out_shape=(SDS((t, ql), BF16), SDS((t, kvl), BF16), SDS((1, ql), F32), SDS((1, kvl), F32)),
        grid=(t // tm,),
        in_specs=[_colspec(tm, ql, lay["q"]), _colspec(tm, kvl, lay["kv"]), vq, vkv, _rowspec(tm, ql),
                  _rowspec(tm, kvl)],
        out_specs=(_rowspec(tm, ql), _rowspec(tm, kvl), vq, vkv), compiler_params=_cparams(("arbitrary",)),
        name="qkvnorm_bwd")(proj, proj, q_gain.reshape(1, ql), kv_gain.reshape(1, kvl), dqn, dkvn)


def _rope(x, cos_t, sin_a, sin_b):
    return x * cos_t + pltpu.roll(x, LANES - ROPE_DIM // 2, 1) * sin_a + pltpu.roll(x, ROPE_DIM // 2, 1) * sin_b


def _rope_fwd(qf, kv, proj, lay, tables):
    t = qf.shape[0]
    tm = _tile(t, 256, 8)
    hq = N_HEADS * QK_PAD

    def body(q_ref, kn_ref, kr_ref, cos_ref, sa_ref, sb_ref, qr_ref, kf_ref):
        cos_t, sin_a, sin_b = cos_ref[...], sa_ref[...], sb_ref[...]
        kr = _rope(kr_ref[...].astype(F32), cos_t, sin_a, sin_b).astype(BF16)
        for h in range(N_HEADS):
            lo = h * QK_PAD
            qr_ref[:, lo:lo + NOPE_DIM] = q_ref[:, lo:lo + NOPE_DIM]
            qr_ref[:, lo + NOPE_DIM:lo + QK_PAD] = _rope(
                q_ref[:, lo + NOPE_DIM:lo + QK_PAD].astype(F32), cos_t, sin_a, sin_b).astype(BF16)
            kf_ref[:, lo:lo + NOPE_DIM] = kn_ref[:, h * NOPE_DIM:(h + 1) * NOPE_DIM]
            kf_ref[:, lo + NOPE_DIM:lo + QK_PAD] = kr

    tab = _rowspec(tm, LANES)
    return pl.pallas_call(
        body, out_shape=(SDS((t, hq), BF16), SDS((t, hq), BF16)), grid=(t // tm,),
        in_specs=[_rowspec(tm, hq), _rowspec(tm, N_HEADS * NOPE_DIM), _colspec(tm, LANES, lay["kr"]), tab, tab, tab],
        out_specs=(_rowspec(tm, hq), _rowspec(tm, hq)), compiler_params=_cparams(("parallel",)),
        name="rope_fwd")(qf, kv, proj, *tables)


def _rope_bwd(dqr, dkf, dv, tables):
    t = dqr.shape[0]
    tm = _tile(t, 256, 8)
    hq = N_HEADS * QK_PAD
    hn = N_HEADS * NOPE_DIM

    def body(dq_ref, dk_ref, dv_ref, cos_ref, sa_ref, sb_ref, dqf_ref, dkv_ref, dkr_ref):
        cos_t, sin_a, sin_b = cos_ref[...], -sa_ref[...], -sb_ref[...]
        dkr = jnp.zeros((tm, LANES), F32)
        for h in range(N_HEADS):
            lo = h * QK_PAD
            dqf_ref[:, lo:lo + NOPE_DIM] = dq_ref[:, lo:lo + NOPE_DIM].astype(BF16)
            dqf_ref[:, lo + NOPE_DIM:lo + QK_PAD] = _rope(
                dq_ref[:, lo + NOPE_DIM:lo + QK_PAD].astype(F32), cos_t, sin_a, sin_b).astype(BF16)
            dkv_ref[:, h * NOPE_DIM:(h + 1) * NOPE_DIM] = dk_ref[:, lo:lo + NOPE_DIM]
            dkr = dkr + dk_ref[:, lo + NOPE_DIM:lo + QK_PAD].astype(F32)
        dkv_ref[:, hn:] = dv_ref[...]
        dkr_ref[...] = _rope(dkr, cos_t, sin_a, sin_b).astype(BF16)

    tab = _rowspec(tm, LANES)
    return pl.pallas_call(
        body, out_shape=(SDS((t, hq), BF16), SDS((t, 2 * hn), BF16), SDS((t, LANES), BF16)), grid=(t // tm,),
        in_specs=[_rowspec(tm, hq), _rowspec(tm, hq), _rowspec(tm, hn), tab, tab, tab],
        out_specs=(_rowspec(tm, hq), _rowspec(tm, 2 * hn), tab), compiler_params=_cparams(("parallel",)),
        name="rope_bwd")(dqr, dkf, dv, *tables)


def _chunk_mask(bq):
    qc = lax.broadcasted_iota(jnp.int32, (bq, bq), 0) // CHUNK
    kc = lax.broadcasted_iota(jnp.int32, (bq, bq), 1) // CHUNK
    return kc <= qc


_NT = (((1,), (1,)), ((), ()))
_TN = (((0,), (0,)), ((), ()))
LOG2E = 1.4426950408889634
EXP2_SCALE = ATTN_SCALE * LOG2E


def _attn_block(t):
    return 512 if t >= 2048 else 128


def _two_slot_pipeline(unmasked, issue, consume, carry):
    issue(0, 0)

    def pair(n, c):
        issue(2 * n + 1, 1)
        c = consume(2 * n, 0, c, False)
        issue(2 * n + 2, 0)
        return consume(2 * n + 1, 1, c, False)

    carry = lax.fori_loop(0, unmasked // 2, pair, carry)

    def even(c):
        return consume(unmasked, 0, c, True)

    def odd(c):
        issue(unmasked, 1)
        c = consume(unmasked - 1, 0, c, False)
        return consume(unmasked, 1, c, True)

    return lax.cond(unmasked % 2 == 0, even, odd, carry)


def _attn_fwd(qr, kf, kv, gather=None):
    t = qr.shape[0]
    bq = _attn_block(t)
    nq = t // bq
    bufs, pieces = gather if gather is not None else ((), ())
    nw = len(bufs)

    def body(*refs):
        q_ref, k_ref, v_ref = refs[:3]
        o_ref, lse_ref = refs[3 + nw:5 + nw]
        buf_refs = refs[5 + nw:5 + 2 * nw]
        vaug_ref, s_ref = refs[5 + 2 * nw], refs[6 + 2 * nw]
        sems = refs[7 + 2 * nw:]
        h, i = pl.program_id(0), pl.program_id(1)

        if nw:
            @pl.when((h == 0) & (i == 0))
            def _():
                _gather_start(buf_refs, pieces, *sems)

        @pl.when(i == 0)
        def _():
            vaug_ref[:, :V_DIM] = v_ref[...]
            vaug_ref[:, V_DIM:] = jnp.ones((t, V_DIM), BF16)

        def issue(j, slot):
            off = pl.multiple_of(j * bq, bq)
            s_ref[slot] = lax.dot_general(q_ref[...], k_ref[pl.ds(off, bq), :], _NT, preferred_element_type=F32)

        def consume(j, slot, carry, masked):
            m, acc = carry
            off = pl.multiple_of(j * bq, bq)
            s = s_ref[slot]
            if masked:
                s = jnp.where(_chunk_mask(bq), s, NEG_BIG)
            m_new = jnp.maximum(m, jnp.max(s, axis=1, keepdims=True))
            alpha = jnp.exp2((m - m_new) * EXP2_SCALE)
            pr = jnp.exp2((s - m_new) * EXP2_SCALE)
            acc = alpha * acc + jnp.dot(pr.astype(BF16), vaug_ref[pl.ds(off, bq), :], preferred_element_type=F32)
            return m_new, acc

        init = (jnp.full((bq, 1), NEG_BIG, F32), jnp.zeros((bq, 2 * V_DIM), F32))
        m, acc = _two_slot_pipeline(i, issue, consume, init)
        l = acc[:, V_DIM:V_DIM + 1]
        o_ref[...] = (acc[:, :V_DIM] / l).astype(BF16)
        lse_ref[0] = jnp.broadcast_to(m * ATTN_SCALE + jnp.log(l), (bq, LANES))

        if nw:
            @pl.when((h == N_HEADS - 1) & (i == nq - 1))
            def _():
                _gather_finish(buf_refs, pieces, *sems)

    out_shape = (SDS((t, N_HEADS * V_DIM), BF16), SDS((N_HEADS, t, LANES), F32)) + tuple(SDS(b.shape, b.dtype) for b in bufs)
    sem_shapes = _gather_sems(pieces) if nw else []
    outs = pl.pallas_call(
        body, out_shape=out_shape, grid=(N_HEADS, nq),
        in_specs=[pl.BlockSpec((bq, QK_PAD), lambda h, i: (i, h)), pl.BlockSpec((t, QK_PAD), lambda h, i: (0, h)),
                  pl.BlockSpec((t, V_DIM), lambda h, i: (0, N_HEADS + h))] + [ANY] * nw,
        out_specs=(pl.BlockSpec((bq, V_DIM), lambda h, i: (i, h)),
                   pl.BlockSpec((1, bq, LANES), lambda h, i: (h, i, 0))) + (ANY,) * nw,
        input_output_aliases={3 + n: 2 + n for n in range(nw)},
        scratch_shapes=[pltpu.VMEM((t, 2 * V_DIM), BF16), pltpu.VMEM((2, bq, bq), F32)] + sem_shapes,
        compiler_params=_cparams(("arbitrary", "arbitrary")), name="attn_fwd_gather" if nw else "attn_fwd")(qr, kf, kv, *bufs)
    return outs[0], outs[1], list(outs[2:])


def _attn_bwd(qr, kf, kv, do, lse, delta, exchange=None):
    t = qr.shape[0]
    bq = _attn_block(t)
    nq = t // bq
    pbs, lands = exchange if exchange is not None else ((), ())
    nw = len(pbs)

    def body(*refs):
        k_ref, v_ref, q_ref, do_ref, lse_ref, dl_ref = refs[:6]
        pb_refs = refs[6:6 + nw]
        dk_ref, dv_ref, dq_ref = refs[6 + 2 * nw:9 + 2 * nw]
        land_refs = refs[9 + 2 * nw:9 + 3 * nw]
        s_ref, dp_ref = refs[9 + 3 * nw], refs[10 + 3 * nw]
        sems = refs[11 + 3 * nw:]
        h, j = pl.program_id(0), pl.program_id(1)

        if nw:
            @pl.when((h == 0) & (j == 0))
            def _():
                _exchange_start(pb_refs, land_refs, *sems)

        @pl.when(j == 0)
        def _():
            dq_ref[...] = jnp.zeros_like(dq_ref)

        k = k_ref[...]
        v = v_ref[...]

        def issue(b, slot):
            off = pl.multiple_of((nq - 1 - b) * bq, bq)
            s_ref[slot] = lax.dot_general(q_ref[pl.ds(off, bq), :], k, _NT, preferred_element_type=F32)
            dp_ref[slot] = lax.dot_general(do_ref[pl.ds(off, bq), :], v, _NT, preferred_element_type=F32)

        def consume(b, slot, carry, masked):
            dk, dv = carry
            off = pl.multiple_of((nq - 1 - b) * bq, bq)
            q = q_ref[pl.ds(off, bq), :]
            do_ = do_ref[pl.ds(off, bq), :]
            lse2 = lse_ref[0, pl.ds(off, bq), :][:, :1] * LOG2E
            dl_i = dl_ref[0, pl.ds(off, bq), :][:, :1]
            s = s_ref[slot]
            if masked:
                s = jnp.where(_chunk_mask(bq), s, NEG_BIG)
            pr = jnp.exp2(s * EXP2_SCALE - lse2)
            dv = dv + lax.dot_general(pr.astype(BF16), do_, _TN, preferred_element_type=F32)
            ds = (pr * (dp_ref[slot] - dl_i)).astype(BF16)
            dk = dk + lax.dot_general(ds, q, _TN, preferred_element_type=F32)
            dq_ref[pl.ds(off, bq), :] += jnp.dot(ds, k, preferred_element_type=F32) * ATTN_SCALE
            return dk, dv

        init = (jnp.zeros((bq, QK_PAD), F32), jnp.zeros((bq, V_DIM), F32))
        dk, dv = _two_slot_pipeline(nq - 1 - j, issue, consume, init)
        dk_ref[...] = (dk * ATTN_SCALE).astype(BF16)
        dv_ref[...] = dv.astype(BF16)

        if nw:
            @pl.when((h == N_HEADS - 1) & (j == nq - 1))
            def _():
                _exchange_finish(pb_refs, land_refs, *sems)

    stat = pl.BlockSpec((1, t, LANES), lambda h, j: (h, 0, 0))
    out_shape = (SDS((t, N_HEADS * QK_PAD), BF16), SDS((t, N_HEADS * V_DIM), BF16), SDS((t, N_HEADS * QK_PAD), F32))
    sem_shapes = [pltpu.SemaphoreType.DMA((3 * nw,)), pltpu.SemaphoreType.DMA((3 * nw,))] if nw else []
    outs = pl.pallas_call(
        body, out_shape=out_shape + tuple(SDS(l.shape, l.dtype) for l in lands), grid=(N_HEADS, nq),
        in_specs=[pl.BlockSpec((bq, QK_PAD), lambda h, j: (j, h)), pl.BlockSpec((bq, V_DIM), lambda h, j: (j, N_HEADS + h)),
                  pl.BlockSpec((t, QK_PAD), lambda h, j: (0, h)), pl.BlockSpec((t, V_DIM), lambda h, j: (0, h)), stat, stat]
        + [ANY] * (2 * nw),
        out_specs=(pl.BlockSpec((bq, QK_PAD), lambda h, j: (j, h)), pl.BlockSpec((bq, V_DIM), lambda h, j: (j, h)),
                   pl.BlockSpec((t, QK_PAD), lambda h, j: (0, h))) + (ANY,) * nw,
        input_output_aliases={6 + nw + n: 3 + n for n in range(nw)},
        scratch_shapes=[pltpu.VMEM((2, bq, bq), F32), pltpu.VMEM((2, bq, bq), F32)] + sem_shapes,
        compiler_params=_cparams(("arbitrary", "arbitrary")),
        name="attn_bwd_exchange" if nw else "attn_bwd")(kf, kv, qr, do, lse, delta, *pbs, *lands)
    return outs[0], outs[1], outs[2], list(outs[3:])


def _merge_fwd(proj, lay, ya, yb):
    t, d = ya.shape
    tm = _tile(t, 512, 8)

    def body(gc_ref, gm_ref, ya_ref, yb_ref, o_ref):
        o_ref[...] = (_sigmoid(gc_ref[...].astype(F32)) * ya_ref[...].astype(F32)
                      + _sigmoid(gm_ref[...].astype(F32)) * yb_ref[...].astype(F32)).astype(BF16)

    return pl.pallas_call(
        body, out_shape=SDS((t, d), BF16), grid=(t // tm,),
        in_specs=[_colspec(tm, d, lay["gc"]), _colspec(tm, d, lay["gm"]), _rowspec(tm, d), _rowspec(tm, d)],
        out_specs=_rowspec(tm, d), compiler_params=_cparams(("parallel",)), name="merge_fwd")(proj, proj, ya, yb)


def _ple_bwd(dh, gp, pp):
    t, d = dh.shape
    tm = _tile(t, 512, 8)

    def body(dh_ref, gp_ref, pp_ref, dpp_ref, dgp_ref):
        g = dh_ref[...]
        s = _sigmoid(gp_ref[...].astype(F32))
        dpp_ref[...] = (g * s).astype(BF16)
        dgp_ref[...] = (g * pp_ref[...].astype(F32) * (s * (1.0 - s))).astype(BF16)

    r = _rowspec(tm, d)
    return pl.pallas_call(body, out_shape=(SDS((t, d), BF16),) * 2, grid=(t // tm,), in_specs=[r, r, r],
                          out_specs=(r, r), compiler_params=_cparams(("parallel",)), name="ple_bwd")(dh, gp, pp)


def _adamw(w, g, m, v):
    shape = w.shape
    cols = shape[-1]
    rows = w.size // cols
    tr = _tile(rows, max(8, BLOCK_ELEMS // cols // 8 * 8), 8)

    def body(w_ref, g_ref, m_ref, v_ref, d_ref, nm_ref, nv_ref):
        g_ = g_ref[...]
        nm = ADAM_B1 * m_ref[...] + (1.0 - ADAM_B1) * g_
        nv = ADAM_B2 * v_ref[...] + (1.0 - ADAM_B2) * (g_ * g_)
        m_hat = nm / (1.0 - ADAM_B1 ** ADAM_STEP)
        v_hat = nv / (1.0 - ADAM_B2 ** ADAM_STEP)
        d_ref[...] = -ADAM_LR * (m_hat / (jnp.sqrt(v_hat) + ADAM_EPS) + ADAM_WD * w_ref[...])
        nm_ref[...] = nm
        nv_ref[...] = nv

    r = _rowspec(tr, cols)
    outs = pl.pallas_call(
        body, out_shape=(SDS((rows, cols), F32),) * 3, grid=(rows // tr,), in_specs=[r, r, r, r], out_specs=(r, r, r),
        compiler_params=_cparams(("parallel",)), name="adamw")(*(a.reshape(rows, cols) for a in (w, g, m, v)))
    return tuple(o.reshape(shape) for o in outs)


ANY = pl.BlockSpec(memory_space=pl.ANY)


def _place():
    x, y, c = lax.axis_index("x"), lax.axis_index("y"), lax.axis_index("c")
    return x, y, c, [(1 - x, y), (x, 1 - y), (1 - x, 1 - y)]


def _half_rows(rows, cols):
    half = rows // 2
    return half, _tile(half, max(16, BLOCK_ELEMS // cols // 16 * 16), 16)


def _my_chip():
    return 2 * lax.axis_index("x") + lax.axis_index("y")


def _cast_into_slot(ws):
    k = len(ws)
    nl, r, c = ws[0].shape
    tr = _tile(r, max(16, BLOCK_ELEMS // c // 16 * 16), 16)

    def body(*refs):
        for n in range(k):
            refs[k + n][...] = refs[n][...].astype(BF16)

    return list(pl.pallas_call(
        body, out_shape=(SDS((nl, N_CHIPS, r, c), BF16),) * k, grid=(nl, r // tr),
        in_specs=[pl.BlockSpec((None, tr, c), lambda l, i: (l, i, 0))] * k,
        out_specs=(pl.BlockSpec((None, None, tr, c), lambda l, i: (l, _my_chip(), i, 0)),) * k,
        compiler_params=_cparams(("parallel", "parallel")), name="cast_into_slot")(*ws))


def _gather_copy(ref, layer, send_sems, recv_sems, sem, chip, half, to):
    r2 = ref.shape[2] // 2
    rows = ref.at[layer, chip, pl.ds(half * r2, r2)]
    return pltpu.make_async_remote_copy(src_ref=rows, dst_ref=rows, send_sem=send_sems.at[sem], recv_sem=recv_sems.at[sem],
                                        device_id=to, device_id_type=MESH)


def _gather_start(refs, pieces, send_sems, recv_sems):
    x, y, c, chips = _place()
    for pi, (ri, layer) in enumerate(pieces):
        for n, chip in enumerate(chips):
            _gather_copy(refs[ri], layer, send_sems, recv_sems, 6 * pi + n, 2 * x + y, c, (*chip, c)).start()


def _gather_finish(refs, pieces, send_sems, recv_sems):
    x, y, c, chips = _place()
    me, sibling = (x, y, c), (x, y, 1 - c)
    for pi, (ri, layer) in enumerate(pieces):
        for n, chip in enumerate(chips):
            k = 2 * chip[0] + chip[1]
            _gather_copy(refs[ri], layer, send_sems, recv_sems, 6 * pi + n, k, c, me).wait_recv()
            _gather_copy(refs[ri], layer, send_sems, recv_sems, 6 * pi + 3 + n, k, c, sibling).start()
    for pi, (ri, layer) in enumerate(pieces):
        for n, chip in enumerate(chips):
            _gather_copy(refs[ri], layer, send_sems, recv_sems, 6 * pi + 3 + n, 2 * chip[0] + chip[1], 1 - c, me).wait_recv()
    for pi, (ri, layer) in enumerate(pieces):
        for n, chip in enumerate(chips):
            _gather_copy(refs[ri], layer, send_sems, recv_sems, 6 * pi + n, 2 * x + y, c, (*chip, c)).wait_send()
            _gather_copy(refs[ri], layer, send_sems, recv_sems, 6 * pi + 3 + n, 2 * chip[0] + chip[1], c, sibling).wait_send()


def _gather_sems(pieces):
    return [pltpu.SemaphoreType.DMA((6 * len(pieces),)), pltpu.SemaphoreType.DMA((6 * len(pieces),))]


def _all_gather_weights(bufs, pieces):
    nw = len(bufs)

    def body(*refs):
        _gather_start(refs[nw:2 * nw], pieces, refs[2 * nw], refs[2 * nw + 1])
        _gather_finish(refs[nw:2 * nw], pieces, refs[2 * nw], refs[2 * nw + 1])

    return list(pl.pallas_call(
        body, out_shape=tuple(SDS(b.shape, b.dtype) for b in bufs), in_specs=[ANY] * nw, out_specs=(ANY,) * nw,
        input_output_aliases={i: i for i in range(nw)}, scratch_shapes=_gather_sems(pieces),
        name="all_gather_weights")(*bufs))


def _pair_swap(grads):
    nw = len(grads)

    def body(*refs):
        ins, outs = refs[:nw], refs[nw:2 * nw]
        send_sems, recv_sems = refs[2 * nw], refs[2 * nw + 1]
        x, y, c, _ = _place()
        copies = []
        for wi, (g_ref, o_ref) in enumerate(zip(ins, outs)):
            r2 = g_ref.shape[1] // 2
            copies.append(pltpu.make_async_remote_copy(
                src_ref=g_ref.at[:, pl.ds((1 - c) * r2, r2)], dst_ref=o_ref, send_sem=send_sems.at[wi],
                recv_sem=recv_sems.at[wi], device_id=(x, y, 1 - c), device_id_type=MESH))
            copies[-1].start()
        for cp in copies:
            cp.wait()

    return pl.pallas_call(
        body, out_shape=tuple(SDS((N_CHIPS, g.shape[1] // 2, g.shape[2]), g.dtype) for g in grads),
        in_specs=[ANY] * nw, out_specs=(ANY,) * nw,
        scratch_shapes=[pltpu.SemaphoreType.DMA((nw,)), pltpu.SemaphoreType.DMA((nw,))], name="pair_swap")(*grads)


def _same_shape_groups(arrays):
    groups = {}
    for i, arr in enumerate(arrays):
        groups.setdefault(tuple(arr.shape), []).append(i)
    return list(groups.values())


def _pair_add(gs, others):
    k = len(gs)
    _, r, c = gs[0].shape
    r2, tr = _half_rows(r, c)
    nb = r2 // tr

    def body(*refs):
        for n in range(k):
            g_ref, o_ref, pb_ref, land_ref = refs[n], refs[k + n], refs[2 * k + n], refs[3 * k + n]
            total = (g_ref[...].astype(F32) + o_ref[...].astype(F32)).astype(BF16)
            pb_ref[...] = total

            @pl.when(pl.program_id(1) == _my_chip())
            def _():
                land_ref[...] = total

    blk = pl.BlockSpec((None, tr, c), lambda j, n: (n, j, 0))
    mine = pl.BlockSpec((None, tr, c), lambda j, n: (n, lax.axis_index("c") * nb + j, 0))
    land = pl.BlockSpec((None, tr, c), lambda j, n: (_my_chip(), j, 0))
    outs = pl.pallas_call(
        body, out_shape=(SDS((N_CHIPS, r2, c), BF16),) * (2 * k), grid=(nb, N_CHIPS),
        in_specs=[mine] * k + [blk] * k, out_specs=(blk,) * k + (land,) * k,
        compiler_params=_cparams(("parallel", "arbitrary")), name="pair_add")(*gs, *others)
    return [(outs[n], outs[k + n]) for n in range(k)]


def _pair_add_all(gs, others):
    pairs = [None] * len(gs)
    for group in _same_shape_groups(gs):
        for i, pair in zip(group, _pair_add([gs[i] for i in group], [others[i] for i in group])):
            pairs[i] = pair
    return pairs


def _exchange_copy(p_ref, l_ref, send_sems, recv_sems, sem, src_slot, dst_slot, to):
    return pltpu.make_async_remote_copy(src_ref=p_ref.at[src_slot], dst_ref=l_ref.at[dst_slot], send_sem=send_sems.at[sem],
                                        recv_sem=recv_sems.at[sem], device_id=to, device_id_type=MESH)


def _exchange_start(p_refs, l_refs, send_sems, recv_sems):
    x, y, c, chips = _place()
    for wi, (p_ref, l_ref) in enumerate(zip(p_refs, l_refs)):
        for n, chip in enumerate(chips):
            _exchange_copy(p_ref, l_ref, send_sems, recv_sems, 3 * wi + n, 2 * chip[0] + chip[1], 2 * x + y, (*chip, c)).start()


def _exchange_finish(p_refs, l_refs, send_sems, recv_sems):
    x, y, c, chips = _place()
    for wi, (p_ref, l_ref) in enumerate(zip(p_refs, l_refs)):
        for n, chip in enumerate(chips):
            _exchange_copy(p_ref, l_ref, send_sems, recv_sems, 3 * wi + n, 2 * x + y, 2 * chip[0] + chip[1], (x, y, c)).wait_recv()
    for wi, (p_ref, l_ref) in enumerate(zip(p_refs, l_refs)):
        for n, chip in enumerate(chips):
            _exchange_copy(p_ref, l_ref, send_sems, recv_sems, 3 * wi + n, 2 * chip[0] + chip[1], 2 * x + y, (*chip, c)).wait_send()


def _chip_all_to_all(pbs, lands):
    nw = len(pbs)

    def body(*refs):
        _exchange_start(refs[:nw], refs[2 * nw:3 * nw], refs[3 * nw], refs[3 * nw + 1])
        _exchange_finish(refs[:nw], refs[2 * nw:3 * nw], refs[3 * nw], refs[3 * nw + 1])

    return list(pl.pallas_call(
        body, out_shape=tuple(SDS(l.shape, l.dtype) for l in lands), in_specs=[ANY] * (2 * nw), out_specs=(ANY,) * nw,
        input_output_aliases={nw + i: i for i in range(nw)},
        scratch_shapes=[pltpu.SemaphoreType.DMA((3 * nw,)), pltpu.SemaphoreType.DMA((3 * nw,))],
        name="chip_all_to_all")(*pbs, *lands))


def _sum_chips(lands, gstacks, layer):
    k = len(lands)
    _, r, c = gstacks[0].shape
    r2, tr = _half_rows(r, c)
    nb = r2 // tr

    def body(*refs):
        for n in range(k):
            l_ref, out_ref = refs[n], refs[2 * k + n]
            out_ref[...] = ((l_ref[0].astype(F32) + l_ref[1].astype(F32)) + l_ref[2].astype(F32)) + l_ref[3].astype(F32)

    return list(pl.pallas_call(
        body, out_shape=tuple(SDS(g.shape, F32) for g in gstacks), grid=(nb,),
        in_specs=[pl.BlockSpec((N_CHIPS, tr, c), lambda j: (0, j, 0))] * k + [ANY] * k,
        out_specs=(pl.BlockSpec((None, tr, c), lambda j: (layer, lax.axis_index("c") * nb + j, 0)),) * k,
        input_output_aliases={k + n: n for n in range(k)}, compiler_params=_cparams(("parallel",)),
        name="sum_chips")(*lands, *gstacks))


def _sum_chips_all(lands, gstacks, layer):
    out = [None] * len(lands)
    for group in _same_shape_groups(gstacks):
        for i, g in zip(group, _sum_chips([lands[i] for i in group], [gstacks[i] for i in group], layer)):
            out[i] = g
    return out


def _pair_gather(gstacks, layer):
    nw = len(gstacks)

    def body(*refs):
        outs = refs[nw:2 * nw]
        send_sems, recv_sems = refs[2 * nw], refs[2 * nw + 1]
        x, y, c, _ = _place()

        def copy(ref, wi, half):
            r2 = ref.shape[1] // 2
            blk = ref.at[layer, pl.ds(half * r2, r2)]
            return pltpu.make_async_remote_copy(src_ref=blk, dst_ref=blk, send_sem=send_sems.at[wi],
                                                recv_sem=recv_sems.at[wi], device_id=(x, y, 1 - c), device_id_type=MESH)

        sent = [copy(ref, wi, c) for wi, ref in enumerate(outs)]
        for cp in sent:
            cp.start()
        for wi, ref in enumerate(outs):
            copy(ref, wi, 1 - c).wait_recv()
        for cp in sent:
            cp.wait_send()

    return pl.pallas_call(
        body, out_shape=tuple(SDS(g.shape, g.dtype) for g in gstacks), in_specs=[ANY] * nw, out_specs=(ANY,) * nw,
        input_output_aliases={i: i for i in range(nw)},
        scratch_shapes=[pltpu.SemaphoreType.DMA((nw,)), pltpu.SemaphoreType.DMA((nw,))], name="pair_gather")(*gstacks)


class _Rider:
    def __init__(self, tag, operands, out_shapes, aliases, sem_shapes, start, finish):
        self.tag, self.operands, self.out_shapes, self.aliases = tag, list(operands), list(out_shapes), dict(aliases)
        self.sem_shapes, self._start, self._finish = list(sem_shapes), start, finish
        self._refs = None

    def split(self, refs, n_outs, n_scratch, grid):
        n_in, n_out = len(self.operands), len(self.out_shapes)
        ins = refs[:n_in]
        own_outs = refs[n_in:n_in + n_outs]
        outs = refs[n_in + n_outs:n_in + n_outs + n_out]
        own_scratch = refs[n_in + n_outs + n_out:n_in + n_outs + n_out + n_scratch]
        sems = refs[n_in + n_outs + n_out + n_scratch:]
        self._refs = (ins, outs, sems)
        first = functools.reduce(lambda p, q: p & q, [pl.program_id(ax) == 0 for ax in range(len(grid))])

        @pl.when(first)
        def _():
            self._start(ins, outs, sems)

        return tuple(own_outs) + tuple(own_scratch)

    def finish_at_last_step(self, grid):
        ins, outs, sems = self._refs
        last = functools.reduce(lambda p, q: p & q, [pl.program_id(ax) == g - 1 for ax, g in enumerate(grid)])

        @pl.when(last)
        def _():
            self._finish(ins, outs, sems)


def _gather_rider(bufs, pieces):
    nw = len(bufs)
    return _Rider("_gather", bufs, [SDS(b.shape, b.dtype) for b in bufs], {i: i for i in range(nw)}, _gather_sems(pieces),
                  lambda ins, outs, sems: _gather_start(outs, pieces, *sems),
                  lambda ins, outs, sems: _gather_finish(outs, pieces, *sems))


def _pair_swap_copies(ins, outs, sems):
    x, y, c, _ = _place()
    copies = []
    for wi, (g_ref, o_ref) in enumerate(zip(ins, outs)):
        r2 = g_ref.shape[1] // 2
        copies.append(pltpu.make_async_remote_copy(
            src_ref=g_ref.at[:, pl.ds((1 - c) * r2, r2)], dst_ref=o_ref, send_sem=sems[0].at[wi],
            recv_sem=sems[1].at[wi], device_id=(x, y, 1 - c), device_id_type=MESH))
    return copies


def _pair_swap_rider(grads):
    nw = len(grads)

    def start(ins, outs, sems):
        for cp in _pair_swap_copies(ins, outs, sems):
            cp.start()

    def finish(ins, outs, sems):
        for cp in _pair_swap_copies(ins, outs, sems):
            cp.wait()

    return _Rider("_swap", grads, [SDS((N_CHIPS, g.shape[1] // 2, g.shape[2]), g.dtype) for g in grads], {},
                  [pltpu.SemaphoreType.DMA((nw,)), pltpu.SemaphoreType.DMA((nw,))], start, finish)


def _pair_gather_copy(ref, layer, sems, wi, half, peer):
    r2 = ref.shape[1] // 2
    blk = ref.at[layer, pl.ds(half * r2, r2)]
    return pltpu.make_async_remote_copy(src_ref=blk, dst_ref=blk, send_sem=sems[0].at[wi], recv_sem=sems[1].at[wi],
                                        device_id=peer, device_id_type=MESH)


def _pair_gather_rider(gstacks, layer):
    nw = len(gstacks)

    def start(ins, outs, sems):
        x, y, c, _ = _place()
        for wi, ref in enumerate(outs):
            _pair_gather_copy(ref, layer, sems, wi, c, (x, y, 1 - c)).start()

    def finish(ins, outs, sems):
        x, y, c, _ = _place()
        for wi, ref in enumerate(outs):
            _pair_gather_copy(ref, layer, sems, wi, 1 - c, (x, y, 1 - c)).wait_recv()
        for wi, ref in enumerate(outs):
            _pair_gather_copy(ref, layer, sems, wi, c, (x, y, 1 - c)).wait_send()

    return _Rider("_pair_gather", gstacks, [SDS(g.shape, g.dtype) for g in gstacks], {i: i for i in range(nw)},
                  [pltpu.SemaphoreType.DMA((nw,)), pltpu.SemaphoreType.DMA((nw,))], start, finish)


def _all_gather_small(vec, name):
    rows, w = vec.shape

    def body(v_ref, sum_ref, all_ref, send_sems, recv_sems):
        x, y, c, chips = _place()
        me, sibling = (x, y, c), (x, y, 1 - c)

        def slot(px, py, pc):
            return all_ref.at[4 * px + 2 * py + pc]

        def copy(k, block, to, src=None):
            return pltpu.make_async_remote_copy(
                src_ref=slot(*block) if src is None else src, dst_ref=slot(*block), send_sem=send_sems.at[k],
                recv_sem=recv_sems.at[k], device_id=to, device_id_type=MESH)

        first = [copy(0, me, sibling, src=v_ref)]
        first += [copy(1 + n, me, (*chip, c), src=v_ref) for n, chip in enumerate(chips)]
        for cp in first:
            cp.start()
        slot(*me)[...] = v_ref[...]
        passed = [copy(4 + n, (*chip, c), sibling) for n, chip in enumerate(chips)]
        for n, chip in enumerate(chips):
            copy(1 + n, (*chip, c), me).wait_recv()
            passed[n].start()
        copy(0, sibling, me).wait_recv()
        for n, chip in enumerate(chips):
            copy(4 + n, (*chip, 1 - c), me).wait_recv()
        for cp in first + passed:
            cp.wait_send()
        total = all_ref[0]
        for dev in range(1, 8):
            total = total + all_ref[dev]
        sum_ref[...] = total

    vm = pl.BlockSpec(memory_space=pltpu.VMEM)
    return pl.pallas_call(
        body, out_shape=(SDS((rows, w), F32), SDS((8, rows, w), F32)), in_specs=[vm], out_specs=(vm, vm),
        scratch_shapes=[pltpu.SemaphoreType.DMA((7,)), pltpu.SemaphoreType.DMA((7,))], name=name)(vec)


def _to_rows128(flat):
    n = flat.shape[0]
    rows = -(-n // (8 * LANES)) * 8
    return jnp.pad(flat, (0, rows * LANES - n)).reshape(rows, LANES)


def _in_layout(conv, ql, kvl, d):
    lay = {"conv": conv, "ql": ql, "kvl": kvl, "d": d}
    lay["q"] = 3 * conv
    lay["kr"] = lay["q"] + ql
    lay["gc"] = lay["kr"] + LANES
    lay["gm"] = lay["gc"] + d
    lay["kv"] = lay["gm"] + d
    used = lay["kv"] + kvl
    lay["width"] = -(-used // 512) * 512
    return lay


def _w_in_to_layout(w, lay):
    conv, ql, kvl, d = lay["conv"], lay["ql"], lay["kvl"], lay["d"]
    o_kv = 3 * conv + ql
    o_kr = o_kv + kvl
    o_g = o_kr + ROPE_DIM
    lead = w.shape[:-1]
    parts = [w[..., :o_kv], w[..., o_kr:o_g], jnp.zeros(lead + (LANES - ROPE_DIM,), w.dtype), w[..., o_g:o_g + 2 * d],
             w[..., o_kv:o_kr], jnp.zeros(lead + (lay["width"] - lay["kv"] - kvl,), w.dtype)]
    return jnp.concatenate(parts, axis=-1)


def _w_in_from_layout(g, lay):
    ql, kvl, d = lay["ql"], lay["kvl"], lay["d"]
    return jnp.concatenate([g[:, :lay["q"] + ql], g[:, lay["kv"]:lay["kv"] + kvl], g[:, lay["kr"]:lay["kr"] + ROPE_DIM],
                            g[:, lay["gc"]:lay["gc"] + 2 * d]], axis=1)


def _w_uq_to_layout(w):
    r = w.shape[0]
    w3 = w.reshape(r, N_HEADS, NOPE_DIM + ROPE_DIM)
    return jnp.pad(w3, ((0, 0), (0, 0), (0, QK_PAD - NOPE_DIM - ROPE_DIM))).reshape(r, N_HEADS * QK_PAD)


def _w_uq_from_layout(g):
    r = g.shape[0]
    return g.reshape(r, N_HEADS, QK_PAD)[:, :, :NOPE_DIM + ROPE_DIM].reshape(r, N_HEADS * (NOPE_DIM + ROPE_DIM))


def _w_ukv_to_layout(w):
    r = w.shape[0]
    return w.reshape(r, N_HEADS, 2, NOPE_DIM).transpose(0, 2, 1, 3).reshape(r, 2 * N_HEADS * NOPE_DIM)


def _w_ukv_from_layout(g):
    r = g.shape[0]
    return g.reshape(r, 2, N_HEADS, NOPE_DIM).transpose(0, 2, 1, 3).reshape(r, 2 * N_HEADS * NOPE_DIM)


def _chips_to_cols(buf):
    _, r, c = buf.shape
    return buf.transpose(1, 0, 2).reshape(r, N_CHIPS * c)


def _cols_to_chips(g):
    r, c4 = g.shape
    return g.reshape(r, N_CHIPS, c4 // N_CHIPS).transpose(1, 0, 2)


BEFORE_FFN1_GU = ("ffn1_w_gu",)
BEFORE_FFN1_DOWN = ("ffn1_w_down", "w_uq", "w_ukv")
IN_ATTENTION = ("w_in", "w_conv_out", "ffn2_w_gu", "ffn2_w_down", "w_mla_out", "w_o", "w_ple_gate", "w_ple_proj")
LAYER0_FIRST = ("ffn1_w_gu", "ffn1_w_down")
LAYER0_IN_FFN1_GU = ("w_in", "w_conv_out", "w_uq", "w_ukv")
LAYER0_IN_FFN1_DOWN = ("ffn2_w_down", "w_mla_out")
LAYER0_IN_PROJ = ("ffn2_w_gu",)
LAYER0_IN_ATTENTION = ("w_o", "w_ple_gate", "w_ple_proj")


def _gather_plan(layer, depth):
    plan = {}
    if layer == 0:
        plan["ffn1_gu"] = [(k, 0) for k in LAYER0_IN_FFN1_GU]
        plan["ffn1_down"] = [(k, 0) for k in LAYER0_IN_FFN1_DOWN]
        plan["in"] = [(k, 0) for k in LAYER0_IN_PROJ]
        plan["attn"] = [(k, 0) for k in LAYER0_IN_ATTENTION]
    if layer + 1 < depth:
        plan["attn"] = plan.get("attn", []) + [(k, layer + 1) for k in IN_ATTENTION]
        plan["ffn2_gu"] = [(k, layer + 1) for k in BEFORE_FFN1_GU]
        plan["ffn2_down"] = [(k, layer + 1) for k in BEFORE_FFN1_DOWN]
    return plan


def _plan_operands(bufs, todo):
    names = list(dict.fromkeys(name for name, _ in todo))
    return [bufs[k] for k in names], names, [(names.index(name), layer) for name, layer in todo]


def _rows_view(buf):
    return buf.reshape(buf.shape[0], N_CHIPS * buf.shape[2], buf.shape[3])


def _cols_view(buf, layer):
    return _chips_to_cols(buf[layer])


def _ffn_fwd(h, n, bufs, which, layer, next_gain, plan):
    tag = which
    rider = None
    if plan.get(which + "_gu"):
        ops, names, pieces = _plan_operands(bufs, plan[which + "_gu"])
        rider = _gather_rider(ops, pieces)
    outs = _gu_swiglu_fwd(n, bufs[which + "_w_gu"], layer, tag + "_gu_fwd", rider=rider)
    gu, a = outs[0], outs[1]
    if rider is not None:
        bufs = {**bufs, **dict(zip(names, outs[2]))}
    rider = None
    if plan.get(which + "_down"):
        ops, names, pieces = _plan_operands(bufs, plan[which + "_down"])
        rider = _gather_rider(ops, pieces)
    outs = _mm(a, _rows_view(bufs[which + "_w_down"]), "nn", F32, tag + "_down_fwd", scale=0.5, res=h, layer=layer,
               norm_gain=next_gain, rider=rider)
    if rider is not None:
        bufs = {**bufs, **dict(zip(names, outs[2]))}
    return outs[0], outs[1], (h, n, gu, a), bufs


def _ffn_bwd(dh, dhb, saved, gain, bufs, which, layer, swap=None):
    tag = which
    h, n, gu, a = saved
    d_wdown = _mm(a, dhb, "tn", BF16, tag + "_down_dw", scale=0.5)
    dgu = _down_dx_swiglu_bwd(dhb, _rows_view(bufs[which + "_w_down"]), gu, layer, tag + "_down_dx")
    rider = _pair_swap_rider(swap) if swap is not None else None
    outs = _mm(n, dgu, "tn", BF16, tag + "_gu_dw", out_chip=True, b_halves=True, rider=rider)
    d_wgu, swapped = (outs[0], outs[1]) if rider is not None else (outs, None)
    dh, dhb, dgain = _mm(dgu, bufs[which + "_w_gu"], "nt", BF16, tag + "_gu_dx", layer=layer, b_chip=True, a_halves=True,
                         norm_bwd=(h, gain, dh))
    return dh, dhb, d_wgu, d_wdown, dgain, swapped


def _layer_fwd(h0, n0, p_i, bufs, conv_w, norms, layer, lay, tables, depth, next_gain):
    plan = _gather_plan(layer, depth)
    h1, n2, s_ffn1, bufs = _ffn_fwd(h0, n0, bufs, "ffn1", layer, norms["mix_norm"], plan)
    w_in = _w_in_to_layout(_cols_view(bufs["w_in"], layer), lay)
    w_conv_out = _cols_view(bufs["w_conv_out"], layer)
    w_uq = _w_uq_to_layout(_cols_view(bufs["w_uq"], layer))
    w_ukv = _w_ukv_to_layout(_cols_view(bufs["w_ukv"], layer))
    if plan.get("in"):
        ops, names, pieces = _plan_operands(bufs, plan["in"])
        proj, gathered = _mm(n2, w_in, "nn", BF16, "in_fwd", rider=_gather_rider(ops, pieces))
        bufs = {**bufs, **dict(zip(names, gathered))}
    else:
        proj = _mm(n2, w_in, "nn", BF16, "in_fwd")
    cb = _conv_fwd(proj, conv_w)
    ya = _mm(cb, w_conv_out, "nn", BF16, "conv_out_fwd")
    qn, kvn = _qkvnorm_fwd(proj, lay, norms["q_norm"], norms["kv_norm"])
    qf = _mm(qn, w_uq, "nn", BF16, "uq_fwd")
    kv = _mm(kvn, w_ukv, "nn", BF16, "ukv_fwd")
    qr, kf = _rope_fwd(qf, kv, proj, lay, tables)
    if plan.get("attn"):
        ops, names, pieces = _plan_operands(bufs, plan["attn"])
        o, lse, gathered = _attn_fwd(qr, kf, kv, gather=(ops, pieces))
        bufs = {**bufs, **dict(zip(names, gathered))}
    else:
        o, lse, _ = _attn_fwd(qr, kf, kv)
    yb = _mm(o, _rows_view(bufs["w_mla_out"]), "nn", BF16, "mla_out_fwd", layer=layer)
    mg = _merge_fwd(proj, lay, ya, yb)
    h2, n3 = _mm(mg, _rows_view(bufs["w_o"]), "nn", F32, "o_fwd", res=h1, layer=layer, norm_gain=norms["ffn2_norm"])
    h3, n4, s_ffn2, bufs = _ffn_fwd(h2, n3, bufs, "ffn2", layer, norms["ple_norm"], plan)
    w_ple_proj = _cols_view(bufs["w_ple_proj"], layer)
    pp = _mm(p_i, w_ple_proj, "nn", BF16, "ple_proj_fwd")
    outs = _mm(n4, _rows_view(bufs["w_ple_gate"]), "nn", BF16, "ple_gate_fwd", layer=layer,
               epilogue=_ple_fwd_epilogue(h3, pp, next_gain))
    gp, h4, n_out = outs[0], outs[1], (outs[2] if next_gain is not None else None)
    saved = dict(s_ffn1=s_ffn1, h1=h1, n2=n2, proj=proj, cb=cb, ya=ya, qn=qn, kvn=kvn, qr=qr, kf=kf, kv=kv, o=o,
                 lse=lse, yb=yb, mg=mg, h2=h2, s_ffn2=s_ffn2, h3=h3, n4=n4, gp=gp, pp=pp, p=p_i,
                 w_in=w_in, w_conv_out=w_conv_out, w_uq=w_uq, w_ukv=w_ukv)
    return h4, n_out, saved, bufs


def _layer_bwd(dh, s, bufs, conv_w, norms, layer, lay, tables, above, gstacks):
    gw, gn = {}, {}

    def by_rows(g):
        return g.reshape(N_CHIPS, g.shape[0] // N_CHIPS, g.shape[1])

    dpp, dgp = _ple_bwd(dh, s["gp"], s["pp"])
    gw["w_ple_proj"] = _cols_to_chips(_mm(s["p"], dpp, "tn", BF16, "ple_proj_dw"))
    gw["w_ple_gate"] = by_rows(_mm(s["n4"], dgp, "tn", BF16, "ple_gate_dw"))
    dh, dhb, gn["ple_norm"] = _mm(dgp, _rows_view(bufs["w_ple_gate"]), "nt", BF16, "ple_gate_dx", layer=layer,
                                  norm_bwd=(s["h3"], norms["ple_norm"], dh))
    dh, dhb, gw["ffn2_w_gu"], g_down, gn["ffn2_norm"], swapped = _ffn_bwd(
        dh, dhb, s["s_ffn2"], norms["ffn2_norm"], bufs, "ffn2", layer, swap=None if above is None else above[1])
    gw["ffn2_w_down"] = by_rows(g_down)
    exchange = None
    if above is not None:
        pairs = _pair_add_all(above[1], swapped)
        exchange = ([pb for pb, _ in pairs], [land for _, land in pairs])
    gw["w_o"] = by_rows(_mm(s["mg"], dhb, "tn", BF16, "o_dw"))
    dya, dyb, dgc, dgm = _mm(dhb, _rows_view(bufs["w_o"]), "nt", BF16, "o_dx", layer=layer,
                             epilogue=_merge_bwd_epilogue(s["proj"], lay, s["ya"], s["yb"]))
    gw["w_conv_out"] = _cols_to_chips(_mm(s["cb"], dya, "tn", BF16, "conv_out_dw"))
    dcb = _mm(dya, s["w_conv_out"], "nt", BF16, "conv_out_dx")
    db, dc, dv_conv, g_conv = _conv_bwd(s["proj"], conv_w, dcb)
    gw["w_mla_out"] = by_rows(_mm(s["o"], dyb, "tn", BF16, "mla_out_dw"))
    do, delta = _mm(dyb, _rows_view(bufs["w_mla_out"]), "nt", BF16, "mla_out_dx", layer=layer,
                    epilogue=_delta_epilogue(s["o"]))
    dkf, dv, dqr, landed = _attn_bwd(s["qr"], s["kf"], s["kv"], do, s["lse"], delta, exchange=exchange)
    dqf, dkv, dkr = _rope_bwd(dqr, dkf, dv, tables)
    gw["w_uq"] = _cols_to_chips(_w_uq_from_layout(_mm(s["qn"], dqf, "tn", BF16, "uq_dw")))
    dqn = _mm(dqf, s["w_uq"], "nt", BF16, "uq_dx")
    gw["w_ukv"] = _cols_to_chips(_w_ukv_from_layout(_mm(s["kvn"], dkv, "tn", BF16, "ukv_dw")))
    dkvn = _mm(dkv, s["w_ukv"], "nt", BF16, "ukv_dx")
    dqc, dkvc, gn["q_norm"], gn["kv_norm"] = _qkvnorm_bwd(s["proj"], lay, norms["q_norm"], norms["kv_norm"], dqn, dkvn)
    t = dh.shape[0]
    dproj = jnp.concatenate([db, dc, dv_conv, dqc, dkr, dgc, dgm, dkvc,
                             jnp.zeros((t, lay["width"] - lay["kv"] - lay["kvl"]), BF16)], axis=1)
    rider = None
    if above is not None:
        gstacks = _sum_chips_all(landed, gstacks, above[0])
        rider = _pair_gather_rider(gstacks, above[0])
    outs = _mm(s["n2"], dproj, "tn", BF16, "in_dw", rider=rider)
    g_in, gstacks = (outs[0], outs[1]) if rider is not None else (outs, gstacks)
    gw["w_in"] = _cols_to_chips(_w_in_from_layout(g_in, lay))
    dh, dhb, gn["mix_norm"] = _mm(dproj, s["w_in"], "nt", BF16, "in_dx", norm_bwd=(s["h1"], norms["mix_norm"], dh))
    dh, dhb, gw["ffn1_w_gu"], g_down, gn["ffn1_norm"], _ = _ffn_bwd(
        dh, dhb, s["s_ffn1"], norms["ffn1_norm"], bufs, "ffn1", layer)
    gw["ffn1_w_down"] = by_rows(g_down)
    return dh, gw, g_conv, gn, gstacks


def _rope_tables(positions):
    half = ROPE_DIM // 2
    inv_freq = ROPE_THETA ** (-jnp.arange(0, ROPE_DIM, 2, dtype=F32) / ROPE_DIM)
    ang = positions.astype(F32)[:, None] * inv_freq
    cos, sin = jnp.cos(ang), jnp.sin(ang)
    zeros = jnp.zeros_like(cos)
    cos_t = jnp.concatenate([cos, cos, zeros, zeros], axis=1)
    sin_a = jnp.concatenate([-sin, zeros, zeros, zeros], axis=1)
    sin_b = jnp.concatenate([zeros, sin, zeros, zeros], axis=1)
    assert cos_t.shape[1] == LANES and half * 4 == LANES
    return cos_t, sin_a, sin_b


def kernel(x, p, positions, ffn1_norm, ffn1_w_gu, ffn1_w_down, mix_norm, w_in, conv_w, w_conv_out, q_norm, kv_norm, w_uq, w_ukv, w_mla_out, w_o, ffn2_norm, ffn2_w_gu, ffn2_w_down, ple_norm, w_ple_gate, w_ple_proj, final_norm, loss_target, m_ffn1_norm, m_ffn1_w_gu, m_ffn1_w_down, m_mix_norm, m_w_in, m_conv_w, m_w_conv_out, m_q_norm, m_kv_norm, m_w_uq, m_w_ukv, m_w_mla_out, m_w_o, m_ffn2_norm, m_ffn2_w_gu, m_ffn2_w_down, m_ple_norm, m_w_ple_gate, m_w_ple_proj, m_final_norm, v_ffn1_norm, v_ffn1_w_gu, v_ffn1_w_down, v_mix_norm, v_w_in, v_conv_w, v_w_conv_out, v_q_norm, v_kv_norm, v_w_uq, v_w_ukv, v_w_mla_out, v_w_o, v_ffn2_norm, v_ffn2_w_gu, v_ffn2_w_down, v_ple_norm, v_w_ple_gate, v_w_ple_proj, v_final_norm):
    args = dict(zip(ARG_NAMES, (x, p, positions, ffn1_norm, ffn1_w_gu, ffn1_w_down, mix_norm, w_in, conv_w, w_conv_out, q_norm, kv_norm, w_uq, w_ukv, w_mla_out, w_o, ffn2_norm, ffn2_w_gu, ffn2_w_down, ple_norm, w_ple_gate, w_ple_proj, final_norm, loss_target, m_ffn1_norm, m_ffn1_w_gu, m_ffn1_w_down, m_mix_norm, m_w_in, m_conv_w, m_w_conv_out, m_q_norm, m_kv_norm, m_w_uq, m_w_ukv, m_w_mla_out, m_w_o, m_ffn2_norm, m_ffn2_w_gu, m_ffn2_w_down, m_ple_norm, m_w_ple_gate, m_w_ple_proj, m_final_norm, v_ffn1_norm, v_ffn1_w_gu, v_ffn1_w_down, v_mix_norm, v_w_in, v_conv_w, v_w_conv_out, v_q_norm, v_kv_norm, v_w_uq, v_w_ukv, v_w_mla_out, v_w_o, v_ffn2_norm, v_ffn2_w_gu, v_ffn2_w_down, v_ple_norm, v_w_ple_gate, v_w_ple_proj, v_final_norm)))
    depth = ffn1_norm.shape[0]
    t, d = x.shape[1], x.shape[2]
    conv = conv_w.shape[-1] * N_CHIPS
    lay = _in_layout(conv, q_norm.shape[-1], kv_norm.shape[-1], d)
    chip = 2 * lax.axis_index("x") + lax.axis_index("y")
    tables = _rope_tables(positions[0])

    bufs = {}
    for group in _same_shape_groups([args[name] for name in BIG]):
        bufs.update(zip([BIG[i] for i in group], _cast_into_slot([args[BIG[i]] for i in group])))
    bufs.update(zip(LAYER0_FIRST, _all_gather_weights([bufs[k] for k in LAYER0_FIRST], [(i, 0) for i in range(len(LAYER0_FIRST))])))
    conv_rows = depth * conv_w.shape[1]
    conv_all = _all_gather_small(_to_rows128(conv_w.reshape(-1)), "all_gather_conv_w")[1]
    conv_full = conv_all[0::2, :conv_rows].reshape(N_CHIPS, depth, conv_w.shape[1], LANES)
    conv_full = conv_full.transpose(1, 2, 0, 3).reshape(depth, conv_w.shape[1], conv)
    norms = [{name: args[name][i] for name in REPLICATED} for i in range(depth)]
    p3 = p.reshape(depth, t, p.shape[-1])

    h = x[0]
    n = _rmsnorm_fwd(h, norms[0]["ffn1_norm"], "first_norm_fwd")
    saved = []
    for i in range(depth):
        h, n, s, bufs = _layer_fwd(h, n, p3[i], bufs, conv_full[i], norms[i], i, lay, tables, depth,
                                   norms[i + 1]["ffn1_norm"] if i + 1 < depth else None)
        saved.append(s)
    loss_part, dh, _, g_final = _loss_head(h, final_norm, loss_target[0])
    loss = lax.psum(loss_part[0, 0], ("x", "y", "c"))

    gstacks = [lax.empty(args[name].shape, F32) for name in BIG]
    norm_grads, conv_grads = [None] * depth, [None] * depth
    above = None
    for i in reversed(range(depth)):
        dh, gw, conv_grads[i], norm_grads[i], gstacks = _layer_bwd(
            dh, saved[i], bufs, conv_full[i], norms[i], i, lay, tables, above, gstacks)
        above = (i, [gw[name] for name in BIG])
    pairs = _pair_add_all(above[1], _pair_swap(above[1]))
    landed = _chip_all_to_all([pb for pb, _ in pairs], [land for _, land in pairs])
    gstacks = _pair_gather(_sum_chips_all(landed, gstacks, above[0]), above[0])
    grad_x = dh[None]
    grads = dict(zip(BIG, gstacks))

    pieces = [norm_grads[i][name].reshape(-1) for i in range(depth) for name in REPLICATED]
    pieces += [g_final.reshape(-1)] + [conv_grads[i].reshape(-1) for i in range(depth)]
    vec = _all_gather_small(_to_rows128(jnp.concatenate(pieces)), "all_sum_small")[0].reshape(-1)
    off = 0
    per_name = {name: [] for name in REPLICATED}
    for i in range(depth):
        for name in REPLICATED:
            size = args[name].shape[1]
            per_name[name].append(vec[off:off + size])
            off += size
    for name in REPLICATED:
        grads[name] = jnp.stack(per_name[name])
    grads["final_norm"] = vec[off:off + d]
    off += d
    conv_g = vec[off:off + depth * 3 * conv].reshape(depth, 3, conv)
    grads["conv_w"] = lax.dynamic_slice_in_dim(conv_g, chip * conv_w.shape[-1], conv_w.shape[-1], axis=2)

    delta, new_m, new_v = {}, {}, {}
    for name in WEIGHTS:
        w_, g_, m_, v_ = args[name], grads[name], args["m_" + name], args["v_" + name]
        if w_.ndim == 1:
            outs = _adamw(w_[None], g_[None], m_[None], v_[None])
            delta[name], new_m[name], new_v[name] = (o[0] for o in outs)
        else:
            delta[name], new_m[name], new_v[name] = _adamw(w_, g_, m_, v_)
    return (loss, grad_x, *[grads[n] for n in WEIGHTS], *[delta[n] for n in WEIGHTS],
            *[new_m[n] for n in WEIGHTS], *[new_v[n] for n in WEIGHTS])
```

```python
import functools

import jax
import jax.numpy as jnp
from jax import lax
from jax.experimental import pallas as pl
from jax.experimental.pallas import tpu as pltpu

BF16 = jnp.bfloat16
F32 = jnp.float32
SDS = jax.ShapeDtypeStruct
MESH = pl.DeviceIdType.MESH

N_HEADS = 8
NOPE_DIM = 128
ROPE_DIM = 64
V_DIM = 128
QK_PAD = 256
CHUNK = 64
ROPE_THETA = 10000.0
EPS = 1e-6
ATTN_SCALE = (NOPE_DIM + ROPE_DIM) ** -0.5
NEG_BIG = -1e30

ADAM_LR = 0.001
ADAM_B1 = 0.9
ADAM_B2 = 0.999
ADAM_EPS = 1e-08
ADAM_WD = 0.01
ADAM_STEP = 10

LANES = 128
N_CHIPS = 4
VMEM_LIMIT_BYTES = 56 * 1024 * 1024
ACC_BYTES = 6 * 1024 * 1024
BLOCK_ELEMS = 1 << 19

SHARDED = (("ffn1_w_gu", 1), ("ffn1_w_down", 0), ("w_in", 1), ("w_conv_out", 1), ("w_uq", 1), ("w_ukv", 1),
           ("w_mla_out", 0), ("w_o", 0), ("ffn2_w_gu", 1), ("ffn2_w_down", 0), ("w_ple_gate", 0), ("w_ple_proj", 1))
BIG = tuple(name for name, _ in SHARDED)
REPLICATED = ("ffn1_norm", "mix_norm", "q_norm", "kv_norm", "ffn2_norm", "ple_norm")
WEIGHTS = ("ffn1_norm", "ffn1_w_gu", "ffn1_w_down", "mix_norm", "w_in", "conv_w", "w_conv_out", "q_norm",
           "kv_norm", "w_uq", "w_ukv", "w_mla_out", "w_o", "ffn2_norm", "ffn2_w_gu", "ffn2_w_down",
           "ple_norm", "w_ple_gate", "w_ple_proj", "final_norm")
ARG_NAMES = ("x", "p", "positions") + WEIGHTS + ("loss_target",) + tuple("m_" + n for n in WEIGHTS) + tuple(
    "v_" + n for n in WEIGHTS)


def _cparams(semantics=None):
    return pltpu.CompilerParams(dimension_semantics=semantics, vmem_limit_bytes=VMEM_LIMIT_BYTES)


def _tile(n, cap, mult=LANES):
    best = None
    for t in range(mult, min(n, cap) + 1, mult):
        if n % t == 0:
            best = t
    return n if best is None else best


def _sigmoid(x):
    return 1.0 / (1.0 + jnp.exp(-x))


def _rowspec(tm, width, col_block=0):
    return pl.BlockSpec((tm, width), lambda i: (i, col_block))


def _colspec(tm, width, offset):
    assert offset % width == 0, (width, offset)
    return _rowspec(tm, width, offset // width)


def _mm(a, b, mode, out_dtype, name, scale=None, res=None, layer=None, b_chip=False, out_chip=False, norm_gain=None,
        a_halves=False, b_halves=False, norm_bwd=None, rider=None, epilogue=None):
    bshape = b.shape if layer is None else b.shape[1:]
    if b_chip:
        bshape = (bshape[1], N_CHIPS * bshape[2])
    if b_halves:
        bshape = (bshape[1], 2 * bshape[2])
    ashape = (a.shape[1], 2 * a.shape[2]) if a_halves else a.shape
    if mode == "nn":
        (m, k), (k2, n) = ashape, bshape
    elif mode == "nt":
        (m, k), (n, k2) = ashape, bshape
    else:
        (k, m), (k2, n) = ashape, bshape
    assert k == k2, (a.shape, b.shape, mode)
    n_unit = n // N_CHIPS if (out_chip or (b_chip and mode == "nn")) else n
    k_unit = k // N_CHIPS if (b_chip and mode == "nt") else k
    tn = _tile(n_unit, 1536)
    tm = _tile(m, min(512 if (norm_bwd is not None or epilogue is not None) else 1408, ACC_BYTES // (4 * tn)))
    tk = _tile(k_unit, 1536)
    nk = k // tk
    n_per, k_per = n_unit // tn, k_unit // tk
    n_half, k_half = n // 2 // tn, k // 2 // tk
    dims = {"nn": (((1,), (0,)), ((), ())), "nt": (((1,), (1,)), ((), ())), "tn": (((0,), (0,)), ((), ()))}[mode]

    fuse_norm = norm_gain is not None
    fuse_bwd = norm_bwd is not None
    assert not (fuse_norm or fuse_bwd or epilogue is not None) or (tn == n and not out_chip), (name, tn, n)
    assert epilogue is None or not (fuse_norm or fuse_bwd)
    assert not a_halves or mode == "nt"
    assert not b_halves or mode == "tn"
    n_extra_in = (1 if res is not None else 0) + (1 if fuse_norm else 0) + (3 if fuse_bwd else 0)
    n_outs = 3 if fuse_bwd else (2 if fuse_norm else 1)
    if epilogue is not None:
        n_extra_in += len(epilogue.inputs)
        n_outs = (1 if epilogue.keep_product else 0) + len(epilogue.out_shapes)
    grid = (m // tm, n // tn, nk)
    if fuse_bwd and nk > 1:
        assert rider is None and res is None and scale is None
        return _mm_norm_bwd_k_outer(a, b, mode, name, layer, b_chip, a_halves, norm_bwd, (m, n, k), (tm, tk), k_per, k_half)

    def body(*refs):
        a_ref, b_ref = refs[0], refs[1]
        extra = list(refs[2:2 + n_extra_in])
        rest = refs[2 + n_extra_in:]
        if rider is not None:
            rest = rider.split(rest, n_outs, 1 if nk > 1 else 0, grid)
        outs = list(rest)
        acc_ref = outs.pop() if nk > 1 else None
        res_ref = extra.pop(0) if res is not None else None
        gain_ref = extra.pop(0) if fuse_norm else None

        def finish(acc):
            if scale is not None:
                acc = acc * scale
            if res_ref is not None:
                acc = res_ref[...] + acc
            if fuse_bwd:
                h_ref, g_ref, dhin_ref = extra
                dh_ref, dhb_ref, dg_ref = outs
                dx, dg = _rn_bwd_math(h_ref[...], g_ref[...], acc)
                dh = dhin_ref[...] + dx
                dh_ref[...] = dh
                dhb_ref[...] = dh.astype(BF16)

                @pl.when(pl.program_id(0) == 0)
                def _():
                    dg_ref[...] = dg

                @pl.when(pl.program_id(0) > 0)
                def _():
                    dg_ref[...] += dg
                return
            if epilogue is not None:
                if epilogue.keep_product:
                    outs[0][...] = acc.astype(out_dtype)
                epilogue.fn(acc, extra, outs[1:] if epilogue.keep_product else outs)
                return
            outs[0][...] = acc.astype(out_dtype)
            if fuse_norm:
                outs[1][...] = _rn_fwd_math(acc, gain_ref[...]).astype(BF16)

        part = lax.dot_general(a_ref[...].astype(BF16), b_ref[...].astype(BF16), dims,
                               preferred_element_type=F32)
        if nk == 1:
            finish(part)
        else:
            kk = pl.program_id(2)

            @pl.when(kk == 0)
            def _():
                acc_ref[...] = part

            @pl.when(kk > 0)
            def _():
                acc_ref[...] += part

            @pl.when(kk == nk - 1)
            def _():
                finish(acc_ref[...])
        if rider is not None:
            rider.finish_at_last_step(grid)

    lead = () if layer is None else (layer,)
    lead_block = () if layer is None else (None,)
    if mode == "nn":
        a_spec = pl.BlockSpec((tm, tk), lambda i, j, kk: (i, kk))
        if b_chip:
            b_spec = pl.BlockSpec(lead_block + (None, tk, tn), lambda i, j, kk: lead + (j // n_per, kk, j % n_per))
        else:
            b_spec = pl.BlockSpec(lead_block + (tk, tn), lambda i, j, kk: lead + (kk, j))
    elif mode == "nt":
        if a_halves:
            a_spec = pl.BlockSpec((None, tm, tk), lambda i, j, kk: (kk // k_half, i, kk % k_half))
        else:
            a_spec = pl.BlockSpec((tm, tk), lambda i, j, kk: (i, kk))
        if b_chip:
            b_spec = pl.BlockSpec(lead_block + (None, tn, tk), lambda i, j, kk: lead + (kk // k_per, j, kk % k_per))
        else:
            b_spec = pl.BlockSpec(lead_block + (tn, tk), lambda i, j, kk: lead + (j, kk))
    else:
        assert layer is None and not b_chip
        a_spec = pl.BlockSpec((tk, tm), lambda i, j, kk: (kk, i))
        if b_halves:
            b_spec = pl.BlockSpec((None, tk, tn), lambda i, j, kk: (j // n_half, kk, j % n_half))
        else:
            b_spec = pl.BlockSpec((tk, tn), lambda i, j, kk: (kk, j))
    if out_chip:
        o_spec = pl.BlockSpec((None, tm, tn), lambda i, j, kk: (j // n_per, i, j % n_per))
        out_shape = SDS((N_CHIPS, m, n_unit), out_dtype)
    else:
        o_spec = pl.BlockSpec((tm, tn), lambda i, j, kk: (i, j))
        out_shape = SDS((m, n), out_dtype)
    in_specs = [a_spec, b_spec] + ([o_spec] if res is not None else [])
    operands = (a, b) + ((res,) if res is not None else ())
    out_specs = o_spec
    vec = pl.BlockSpec((1, tn), lambda i, j, kk: (0, j))
    if fuse_norm:
        in_specs.append(vec)
        operands += (norm_gain.reshape(1, n),)
        out_shape, out_specs = (out_shape, SDS((m, n), BF16)), (o_spec, o_spec)
    if epilogue is not None:
        in_specs += [pl.BlockSpec(blk, (lambda i, j, kk, f=f: f(i))) for _, blk, f in epilogue.inputs]
        operands += tuple(arr for arr, _, _ in epilogue.inputs)
        ep_specs = tuple(pl.BlockSpec(blk, (lambda i, j, kk, f=f: f(i))) for _, blk, f in epilogue.out_shapes)
        ep_shapes = tuple(sds for sds, _, _ in epilogue.out_shapes)
        out_shape = ((out_shape,) if epilogue.keep_product else ()) + ep_shapes
        out_specs = ((o_spec,) if epilogue.keep_product else ()) + ep_specs
    if fuse_bwd:
        h, gain, dh_in = norm_bwd
        in_specs += [o_spec, vec, o_spec]
        operands += (h, gain.reshape(1, n), dh_in)
        out_shape = (SDS((m, n), F32), SDS((m, n), BF16), SDS((1, n), F32))
        out_specs = (o_spec, o_spec, vec)
    scratch = [pltpu.VMEM((tm, tn), F32)] if nk > 1 else []
    semantics = ("arbitrary",) * 3 if fuse_bwd else ("parallel", "parallel", "arbitrary")
    if rider is None:
        return pl.pallas_call(
            body, out_shape=out_shape, grid=grid, in_specs=in_specs, out_specs=out_specs, scratch_shapes=scratch,
            compiler_params=_cparams(semantics), name=name)(*operands)
    out_shape = out_shape if isinstance(out_shape, tuple) else (out_shape,)
    out_specs = out_specs if isinstance(out_specs, tuple) else (out_specs,)
    outs = pl.pallas_call(
        body, out_shape=out_shape + tuple(rider.out_shapes), grid=grid, in_specs=in_specs + [ANY] * len(rider.operands),
        out_specs=out_specs + (ANY,) * len(rider.out_shapes), scratch_shapes=scratch + rider.sem_shapes,
        input_output_aliases={len(operands) + i: len(out_shape) + o for i, o in rider.aliases.items()},
        compiler_params=_cparams(("arbitrary",) * 3), name=name + rider.tag)(*operands, *rider.operands)
    return tuple(outs[:n_outs]) + (list(outs[n_outs:]),)


class _RowEpilogue:
    def __init__(self, inputs, out_shapes, fn, keep_product):
        self.inputs, self.out_shapes, self.fn, self.keep_product = list(inputs), list(out_shapes), fn, keep_product


def _rows(arr, tm_of):
    return (arr, (tm_of, arr.shape[1]), lambda i: (i, 0))


def _epilogue_rows(t):
    return _tile(t, 512)


def _delta_epilogue(o):
    t = o.shape[0]
    tm = _epilogue_rows(t)

    def fn(acc, ins, outs):
        prod = acc * ins[0][...].astype(F32)
        for h in range(N_HEADS):
            part = jnp.sum(prod[:, h * V_DIM:(h + 1) * V_DIM], axis=1, keepdims=True)
            outs[0][h] = jnp.broadcast_to(part, (tm, LANES))

    return _RowEpilogue([_rows(o, tm)], [(SDS((N_HEADS, t, LANES), F32), (N_HEADS, tm, LANES), lambda i: (0, i, 0))], fn, True)


def _merge_bwd_epilogue(proj, lay, ya, yb):
    t, d = ya.shape
    tm = _epilogue_rows(t)
    assert lay["gc"] % d == 0 and lay["gm"] % d == 0

    def fn(acc, ins, outs):
        gc, gm, ya_, yb_ = (r[...].astype(F32) for r in ins)
        sc, sm = _sigmoid(gc), _sigmoid(gm)
        outs[0][...] = (acc * sc).astype(BF16)
        outs[1][...] = (acc * sm).astype(BF16)
        outs[2][...] = (acc * ya_ * (sc * (1.0 - sc))).astype(BF16)
        outs[3][...] = (acc * yb_ * (sm * (1.0 - sm))).astype(BF16)

    gate = [(proj, (tm, d), lambda i, c=lay[k] // d: (i, c)) for k in ("gc", "gm")]
    out = (SDS((t, d), BF16), (tm, d), lambda i: (i, 0))
    return _RowEpilogue(gate + [_rows(ya, tm), _rows(yb, tm)], [out] * 4, fn, False)


def _ple_fwd_epilogue(h, pp, norm_gain):
    t, d = h.shape
    tm = _epilogue_rows(t)
    fuse_norm = norm_gain is not None

    def fn(acc, ins, outs):
        gp = acc.astype(BF16).astype(F32)
        out = ins[0][...] + _sigmoid(gp) * ins[1][...].astype(F32)
        outs[0][...] = out
        if fuse_norm:
            outs[1][...] = _rn_fwd_math(out, ins[2][...]).astype(BF16)

    inputs = [_rows(h, tm), _rows(pp, tm)]
    outs = [(SDS((t, d), F32), (tm, d), lambda i: (i, 0))]
    if fuse_norm:
        inputs.append((norm_gain.reshape(1, d), (1, d), lambda i: (0, 0)))
        outs.append((SDS((t, d), BF16), (tm, d), lambda i: (i, 0)))
    return _RowEpilogue(inputs, outs, fn, True)


def _mm_norm_bwd_k_outer(a, b, mode, name, layer, b_chip, a_halves, norm_bwd, sizes, tiles, k_per, k_half):
    (m, n, k), (tm, tk) = sizes, tiles
    nk, ni = k // tk, m // tm
    h, gain, dh_in = norm_bwd
    dims = {"nn": (((1,), (0,)), ((), ())), "nt": (((1,), (1,)), ((), ()))}[mode]

    def body(a_ref, b_ref, h_ref, g_ref, dhin_ref, dh_ref, dhb_ref, dg_ref, acc_ref):
        kk, i = pl.program_id(0), pl.program_id(1)
        rows = pl.ds(pl.multiple_of(i * tm, tm), tm)
        part = lax.dot_general(a_ref[...].astype(BF16), b_ref[...].astype(BF16), dims, preferred_element_type=F32)

        @pl.when(kk == 0)
        def _():
            acc_ref[rows, :] = part

        @pl.when((kk > 0) & (kk < nk - 1))
        def _():
            acc_ref[rows, :] += part

        @pl.when(kk == nk - 1)
        def _():
            dx, dg = _rn_bwd_math(h_ref[...], g_ref[...], acc_ref[rows, :] + part)
            dh = dhin_ref[...] + dx
            dh_ref[...] = dh
            dhb_ref[...] = dh.astype(BF16)

            @pl.when(i == 0)
            def _():
                dg_ref[...] = dg

            @pl.when(i > 0)
            def _():
                dg_ref[...] += dg

    lead = () if layer is None else (layer,)
    lead_block = () if layer is None else (None,)
    if a_halves:
        a_spec = pl.BlockSpec((None, tm, tk), lambda kk, i: (kk // k_half, i, kk % k_half))
    else:
        a_spec = pl.BlockSpec((tm, tk), lambda kk, i: (i, kk))
    if mode == "nn":
        assert not b_chip
        b_spec = pl.BlockSpec(lead_block + (tk, n), lambda kk, i: lead + (kk, 0))
    elif b_chip:
        b_spec = pl.BlockSpec(lead_block + (None, n, tk), lambda kk, i: lead + (kk // k_per, 0, kk % k_per))
    else:
        b_spec = pl.BlockSpec(lead_block + (n, tk), lambda kk, i: lead + (0, kk))
    late = pl.BlockSpec((tm, n), lambda kk, i: (jnp.where(kk == nk - 1, i, 0), 0))
    vec = pl.BlockSpec((1, n), lambda kk, i: (0, 0))
    return pl.pallas_call(
        body, out_shape=(SDS((m, n), F32), SDS((m, n), BF16), SDS((1, n), F32)), grid=(nk, ni),
        in_specs=[a_spec, b_spec, late, vec, late], out_specs=(late, late, vec),
        scratch_shapes=[pltpu.VMEM((m, n), F32)], compiler_params=_cparams(("arbitrary", "arbitrary")),
        name=name)(a, b, h, gain.reshape(1, n), dh_in)


def _rn_fwd_math(x, g):
    r = lax.rsqrt(jnp.mean(x * x, axis=-1, keepdims=True) + EPS)
    return (x * r) * g


def _rn_bwd_math(x, g, dn):
    r = lax.rsqrt(jnp.mean(x * x, axis=-1, keepdims=True) + EPS)
    xh = x * r
    gy = dn * g
    dx = r * (gy - xh * jnp.mean(gy * xh, axis=-1, keepdims=True))
    dg = jnp.sum(dn * xh, axis=0, keepdims=True)
    return dx, dg


def _rmsnorm_fwd(h, gain, name):
    t, d = h.shape
    tm = _tile(t, 512, 8)

    def body(h_ref, g_ref, o_ref):
        o_ref[...] = _rn_fwd_math(h_ref[...], g_ref[...]).astype(BF16)

    return pl.pallas_call(
        body, out_shape=SDS((t, d), BF16), grid=(t // tm,),
        in_specs=[_rowspec(tm, d), pl.BlockSpec((1, d), lambda i: (0, 0))], out_specs=_rowspec(tm, d),
        compiler_params=_cparams(("parallel",)), name=name)(h, gain.reshape(1, d))


def _loss_head(h, gain, target):
    t, d = h.shape
    tm = _tile(t, 512, 8)

    def body(h_ref, g_ref, t_ref, loss_ref, dh_ref, dhb_ref, dg_ref):
        x, g = h_ref[...], g_ref[...]
        err = _rn_fwd_math(x, g) - t_ref[...]
        part = 0.5 * jnp.sum(jnp.sum(err * err, axis=1, keepdims=True), axis=0, keepdims=True) * (1.0 / d)
        dx, dg = _rn_bwd_math(x, g, err * (1.0 / d))
        dh_ref[...] = dx
        dhb_ref[...] = dx.astype(BF16)

        @pl.when(pl.program_id(0) == 0)
        def _():
            dg_ref[...] = dg
            loss_ref[...] = jnp.broadcast_to(part, (1, LANES))

        @pl.when(pl.program_id(0) > 0)
        def _():
            dg_ref[...] += dg
            loss_ref[...] += jnp.broadcast_to(part, (1, LANES))

    vec = pl.BlockSpec((1, d), lambda i: (0, 0))
    return pl.pallas_call(
        body, out_shape=(SDS((1, LANES), F32), SDS((t, d), F32), SDS((t, d), BF16), SDS((1, d), F32)),
        grid=(t // tm,), in_specs=[_rowspec(tm, d), vec, _rowspec(tm, d)],
        out_specs=(pl.BlockSpec((1, LANES), lambda i: (0, 0)), _rowspec(tm, d), _rowspec(tm, d), vec),
        compiler_params=_cparams(("arbitrary",)), name="loss_head")(h, gain.reshape(1, d), target)


def _gu_swiglu_fwd(n, w_gu, layer, name, rider=None):
    t, d = n.shape
    cols = w_gu.shape[3]
    f = 2 * cols
    tn = _tile(cols, 1536)
    tm = _tile(t, 512)
    per = cols // tn

    grid = (f // tn, t // tm)

    def body(n_ref, wg_ref, wu_ref, *rest):
        gu_ref, a_ref = rest[:2] if rider is None else rider.split(rest, 2, 0, grid)
        x = n_ref[...]
        g = jnp.dot(x, wg_ref[...], preferred_element_type=F32)
        u = jnp.dot(x, wu_ref[...], preferred_element_type=F32)
        gu_ref[0] = g.astype(BF16)
        gu_ref[1] = u.astype(BF16)
        a_ref[...] = (g * _sigmoid(g) * u).astype(BF16)
        if rider is not None:
            rider.finish_at_last_step(grid)

    in_specs = [pl.BlockSpec((tm, d), lambda j, i: (i, 0)),
                pl.BlockSpec((None, None, d, tn), lambda j, i: (layer, j // per, 0, j % per)),
                pl.BlockSpec((None, None, d, tn), lambda j, i: (layer, 2 + j // per, 0, j % per))]
    out_shape = (SDS((2, t, f), BF16), SDS((t, f), BF16))
    out_specs = (pl.BlockSpec((2, tm, tn), lambda j, i: (0, i, j)), pl.BlockSpec((tm, tn), lambda j, i: (i, j)))
    if rider is None:
        return pl.pallas_call(body, out_shape=out_shape, grid=grid, in_specs=in_specs, out_specs=out_specs,
                              compiler_params=_cparams(("parallel", "parallel")), name=name)(n, w_gu, w_gu)
    outs = pl.pallas_call(
        body, out_shape=out_shape + tuple(rider.out_shapes), grid=grid, in_specs=in_specs + [ANY] * len(rider.operands),
        out_specs=out_specs + (ANY,) * len(rider.out_shapes), scratch_shapes=rider.sem_shapes,
        input_output_aliases={3 + i: 2 + o for i, o in rider.aliases.items()},
        compiler_params=_cparams(("arbitrary", "arbitrary")), name=name + rider.tag)(n, w_gu, w_gu, *rider.operands)
    return outs[0], outs[1], list(outs[2:])


def _down_dx_swiglu_bwd(dhb, w_down, gu, layer, name):
    t, d = dhb.shape
    f = gu.shape[2]
    tn = _tile(f, 1536)
    tm = _tile(t, 512)

    def body(dh_ref, w_ref, gu_ref, dgu_ref):
        da = 0.5 * lax.dot_general(dh_ref[...], w_ref[...], _NT, preferred_element_type=F32)
        g = gu_ref[0].astype(F32)
        u = gu_ref[1].astype(F32)
        sg = _sigmoid(g)
        dgu_ref[0] = (da * u * (sg * (1.0 + g * (1.0 - sg)))).astype(BF16)
        dgu_ref[1] = (da * (g * sg)).astype(BF16)

    blk = pl.BlockSpec((2, tm, tn), lambda j, i: (0, i, j))
    return pl.pallas_call(
        body, out_shape=SDS((2, t, f), BF16), grid=(f // tn, t // tm),
        in_specs=[pl.BlockSpec((tm, d), lambda j, i: (i, 0)), pl.BlockSpec((None, tn, d), lambda j, i: (layer, j, 0)), blk],
        out_specs=blk, compiler_params=_cparams(("parallel", "parallel")), name=name)(dhb, w_down, gu)


def _shift_down(z, k, row):
    return jnp.where(row >= k, pltpu.roll(z, k, 0), 0.0)


def _shift_up(z, k, row, t):
    return jnp.where(row < t - k, pltpu.roll(z, t - k, 0), 0.0)


def _conv_specs(t, conv):
    nb = conv // LANES
    return [pl.BlockSpec((t, LANES), lambda j: (0, j)), pl.BlockSpec((t, LANES), lambda j: (0, nb + j)),
            pl.BlockSpec((t, LANES), lambda j: (0, 2 * nb + j))]


def _conv_fwd(proj, conv_w):
    t = proj.shape[0]
    conv = conv_w.shape[1]

    def body(b_ref, c_ref, v_ref, w_ref, o_ref):
        z = c_ref[...].astype(F32) * v_ref[...].astype(F32)
        row = lax.broadcasted_iota(jnp.int32, z.shape, 0)
        y = w_ref[0:1, :] * _shift_down(z, 2, row) + w_ref[1:2, :] * _shift_down(z, 1, row) + w_ref[2:3, :] * z
        o_ref[...] = (b_ref[...].astype(F32) * y).astype(BF16)

    cspec = pl.BlockSpec((t, LANES), lambda j: (0, j))
    return pl.pallas_call(
        body, out_shape=SDS((t, conv), BF16), grid=(conv // LANES,),
        in_specs=_conv_specs(t, conv) + [pl.BlockSpec((3, LANES), lambda j: (0, j))], out_specs=cspec,
        compiler_params=_cparams(("parallel",)), name="conv_fwd")(proj, proj, proj, conv_w)


def _conv_bwd(proj, conv_w, dcb):
    t = proj.shape[0]
    conv = conv_w.shape[1]

    def body(b_ref, c_ref, v_ref, w_ref, d_ref, db_ref, dc_ref, dv_ref, dw_ref):
        b, c, v = b_ref[...].astype(F32), c_ref[...].astype(F32), v_ref[...].astype(F32)
        d = d_ref[...].astype(F32)
        z = c * v
        row = lax.broadcasted_iota(jnp.int32, z.shape, 0)
        z1, z2 = _shift_down(z, 1, row), _shift_down(z, 2, row)
        w0, w1, w2 = w_ref[0:1, :], w_ref[1:2, :], w_ref[2:3, :]
        y = w0 * z2 + w1 * z1 + w2 * z
        dy = d * b
        db_ref[...] = (d * y).astype(BF16)
        dz = w2 * dy + w1 * _shift_up(dy, 1, row, t) + w0 * _shift_up(dy, 2, row, t)
        dc_ref[...] = (dz * v).astype(BF16)
        dv_ref[...] = (dz * c).astype(BF16)
        dw_ref[0:1, :] = jnp.sum(dy * z2, axis=0, keepdims=True)
        dw_ref[1:2, :] = jnp.sum(dy * z1, axis=0, keepdims=True)
        dw_ref[2:3, :] = jnp.sum(dy * z, axis=0, keepdims=True)

    cspec = pl.BlockSpec((t, LANES), lambda j: (0, j))
    wspec = pl.BlockSpec((3, LANES), lambda j: (0, j))
    return pl.pallas_call(
        body, out_shape=(SDS((t, conv), BF16),) * 3 + (SDS((3, conv), F32),), grid=(conv // LANES,),
        in_specs=_conv_specs(t, conv) + [wspec, cspec], out_specs=(cspec, cspec, cspec, wspec),
        compiler_params=_cparams(("parallel",)), name="conv_bwd")(proj, proj, proj, conv_w, dcb)


def _qkvnorm_fwd(proj, lay, q_gain, kv_gain):
    t = proj.shape[0]
    ql, kvl = lay["ql"], lay["kvl"]
    tm = _tile(t, 512, 8)

    def body(q_ref, kv_ref, gq_ref, gkv_ref, qn_ref, kvn_ref):
        qn_ref[...] = _rn_fwd_math(q_ref[...].astype(F32), gq_ref[...]).astype(BF16)
        kvn_ref[...] = _rn_fwd_math(kv_ref[...].astype(F32), gkv_ref[...]).astype(BF16)

    return pl.pallas_call(
        body, out_shape=(SDS((t, ql), BF16), SDS((t, kvl), BF16)), grid=(t // tm,),
        in_specs=[_colspec(tm, ql, lay["q"]), _colspec(tm, kvl, lay["kv"]),
                  pl.BlockSpec((1, ql), lambda i: (0, 0)), pl.BlockSpec((1, kvl), lambda i: (0, 0))],
        out_specs=(_rowspec(tm, ql), _rowspec(tm, kvl)), compiler_params=_cparams(("parallel",)),
        name="qkvnorm_fwd")(proj, proj, q_gain.reshape(1, ql), kv_gain.reshape(1, kvl))


def _qkvnorm_bwd(proj, lay, q_gain, kv_gain, dqn, dkvn):
    t = proj.shape[0]
    ql, kvl = lay["ql"], lay["kvl"]
    tm = _tile(t, 512, 8)

    def body(q_ref, kv_ref, gq_ref, gkv_ref, dqn_ref, dkvn_ref, dq_ref, dkv_ref, dgq_ref, dgkv_ref):
        dq, dgq = _rn_bwd_math(q_ref[...].astype(F32), gq_ref[...], dqn_ref[...].astype(F32))
        dkv, dgkv = _rn_bwd_math(kv_ref[...].astype(F32), gkv_ref[...], dkvn_ref[...].astype(F32))
        dq_ref[...] = dq.astype(BF16)
        dkv_ref[...] = dkv.astype(BF16)

        @pl.when(pl.program_id(0) == 0)
        def _():
            dgq_ref[...] = dgq
            dgkv_ref[...] = dgkv

        @pl.when(pl.program_id(0) > 0)
        def _():
            dgq_ref[...] += dgq
            dgkv_ref[...] += dgkv

    vq = pl.BlockSpec((1, ql), lambda i: (0, 0))
    vkv = pl.BlockSpec((1, kvl), lambda i: (0, 0))
    return pl.pallas_call(
        body, out_shape=(SDS((t, ql), BF16), SDS((t, kvl), BF16), SDS((1, ql), F32), SDS((1, kvl), F32)),
        grid=(t // tm,),
        in_specs=[_colspec(tm, ql, lay["q"]), _colspec(tm, kvl, lay["kv"]), vq, vkv, _rowspec(tm, ql),
                  _rowspec(tm, kvl)],
        out_specs=(_rowspec(tm, ql), _rowspec(tm, kvl), vq, vkv), compiler_params=_cparams(("arbitrary",)),
        name="qkvnorm_bwd")(proj, proj, q_gain.reshape(1, ql), kv_gain.reshape(1, kvl), dqn, dkvn)


def _rope(x, cos_t, sin_a, sin_b):
    return x * cos_t + pltpu.roll(x, LANES - ROPE_DIM // 2, 1) * sin_a + pltpu.roll(x, ROPE_DIM // 2, 1) * sin_b


def _rope_fwd(qf, kv, proj, lay, tables):
    t = qf.shape[0]
    tm = _tile(t, 256, 8)
    hq = N_HEADS * QK_PAD

    def body(q_ref, kn_ref, kr_ref, cos_ref, sa_ref, sb_ref, qr_ref, kf_ref):
        cos_t, sin_a, sin_b = cos_ref[...], sa_ref[...], sb_ref[...]
        kr = _rope(kr_ref[...].astype(F32), cos_t, sin_a, sin_b).astype(BF16)
        for h in range(N_HEADS):
            lo = h * QK_PAD
            qr_ref[:, lo:lo + NOPE_DIM] = q_ref[:, lo:lo + NOPE_DIM]
            qr_ref[:, lo + NOPE_DIM:lo + QK_PAD] = _rope(
                q_ref[:, lo + NOPE_DIM:lo + QK_PAD].astype(F32), cos_t, sin_a, sin_b).astype(BF16)
            kf_ref[:, lo:lo + NOPE_DIM] = kn_ref[:, h * NOPE_DIM:(h + 1) * NOPE_DIM]
            kf_ref[:, lo + NOPE_DIM:lo + QK_PAD] = kr

    tab = _rowspec(tm, LANES)
    return pl.pallas_call(
        body, out_shape=(SDS((t, hq), BF16), SDS((t, hq), BF16)), grid=(t // tm,),
        in_specs=[_rowspec(tm, hq), _rowspec(tm, N_HEADS * NOPE_DIM), _colspec(tm, LANES, lay["kr"]), tab, tab, tab],
        out_specs=(_rowspec(tm, hq), _rowspec(tm, hq)), compiler_params=_cparams(("parallel",)),
        name="rope_fwd")(qf, kv, proj, *tables)


def _rope_bwd(dqr, dkf, dv, tables):
    t = dqr.shape[0]
    tm = _tile(t, 256, 8)
    hq = N_HEADS * QK_PAD
    hn = N_HEADS * NOPE_DIM

    def body(dq_ref, dk_ref, dv_ref, cos_ref, sa_ref, sb_ref, dqf_ref, dkv_ref, dkr_ref):
        cos_t, sin_a, sin_b = cos_ref[...], -sa_ref[...], -sb_ref[...]
        dkr = jnp.zeros((tm, LANES), F32)
        for h in range(N_HEADS):
            lo = h * QK_PAD
            dqf_ref[:, lo:lo + NOPE_DIM] = dq_ref[:, lo:lo + NOPE_DIM].astype(BF16)
            dqf_ref[:, lo + NOPE_DIM:lo + QK_PAD] = _rope(
                dq_ref[:, lo + NOPE_DIM:lo + QK_PAD].astype(F32), cos_t, sin_a, sin_b).astype(BF16)
            dkv_ref[:, h * NOPE_DIM:(h + 1) * NOPE_DIM] = dk_ref[:, lo:lo + NOPE_DIM]
            dkr = dkr + dk_ref[:, lo + NOPE_DIM:lo + QK_PAD].astype(F32)
        dkv_ref[:, hn:] = dv_ref[...]
        dkr_ref[...] = _rope(dkr, cos_t, sin_a, sin_b).astype(BF16)

    tab = _rowspec(tm, LANES)
    return pl.pallas_call(
        body, out_shape=(SDS((t, hq), BF16), SDS((t, 2 * hn), BF16), SDS((t, LANES), BF16)), grid=(t // tm,),
        in_specs=[_rowspec(tm, hq), _rowspec(tm, hq), _rowspec(tm, hn), tab, tab, tab],
        out_specs=(_rowspec(tm, hq), _rowspec(tm, 2 * hn), tab), compiler_params=_cparams(("parallel",)),
        name="rope_bwd")(dqr, dkf, dv, *tables)


def _chunk_mask(bq):
    qc = lax.broadcasted_iota(jnp.int32, (bq, bq), 0) // CHUNK
    kc = lax.broadcasted_iota(jnp.int32, (bq, bq), 1) // CHUNK
    return kc <= qc


_NT = (((1,), (1,)), ((), ()))
_TN = (((0,), (0,)), ((), ()))
LOG2E = 1.4426950408889634
EXP2_SCALE = ATTN_SCALE * LOG2E


def _attn_block(t):
    return 512 if t >= 2048 else 128


def _two_slot_pipeline(unmasked, issue, consume, carry):
    issue(0, 0)

    def pair(n, c):
        issue(2 * n + 1, 1)
        c = consume(2 * n, 0, c, False)
        issue(2 * n + 2, 0)
        return consume(2 * n + 1, 1, c, False)

    carry = lax.fori_loop(0, unmasked // 2, pair, carry)

    def even(c):
        return consume(unmasked, 0, c, True)

    def odd(c):
        issue(unmasked, 1)
        c = consume(unmasked - 1, 0, c, False)
        return consume(unmasked, 1, c, True)

    return lax.cond(unmasked % 2 == 0, even, odd, carry)


def _attn_fwd(qr, kf, kv, gather=None):
    t = qr.shape[0]
    bq = _attn_block(t)
    nq = t // bq
    bufs, pieces = gather if gather is not None else ((), ())
    nw = len(bufs)

    def body(*refs):
        q_ref, k_ref, v_ref = refs[:3]
        o_ref, lse_ref = refs[3 + nw:5 + nw]
        buf_refs = refs[5 + nw:5 + 2 * nw]
        vaug_ref, s_ref = refs[5 + 2 * nw], refs[6 + 2 * nw]
        sems = refs[7 + 2 * nw:]
        h, i = pl.program_id(0), pl.program_id(1)

        if nw:
            @pl.when((h == 0) & (i == 0))
            def _():
                _gather_start(buf_refs, pieces, *sems)

        @pl.when(i == 0)
        def _():
            vaug_ref[:, :V_DIM] = v_ref[...]
            vaug_ref[:, V_DIM:] = jnp.ones((t, V_DIM), BF16)

        def issue(j, slot):
            off = pl.multiple_of(j * bq, bq)
            s_ref[slot] = lax.dot_general(q_ref[...], k_ref[pl.ds(off, bq), :], _NT, preferred_element_type=F32)

        def consume(j, slot, carry, masked):
            m, acc = carry
            off = pl.multiple_of(j * bq, bq)
            s = s_ref[slot]
            if masked:
                s = jnp.where(_chunk_mask(bq), s, NEG_BIG)
            m_new = jnp.maximum(m, jnp.max(s, axis=1, keepdims=True))
            alpha = jnp.exp2((m - m_new) * EXP2_SCALE)
            pr = jnp.exp2((s - m_new) * EXP2_SCALE)
            acc = alpha * acc + jnp.dot(pr.astype(BF16), vaug_ref[pl.ds(off, bq), :], preferred_element_type=F32)
            return m_new, acc

        init = (jnp.full((bq, 1), NEG_BIG, F32), jnp.zeros((bq, 2 * V_DIM), F32))
        m, acc = _two_slot_pipeline(i, issue, consume, init)
        l = acc[:, V_DIM:V_DIM + 1]
        o_ref[...] = (acc[:, :V_DIM] / l).astype(BF16)
        lse_ref[0] = jnp.broadcast_to(m * ATTN_SCALE + jnp.log(l), (bq, LANES))

        if nw:
            @pl.when((h == N_HEADS - 1) & (i == nq - 1))
            def _():
                _gather_finish(buf_refs, pieces, *sems)

    out_shape = (SDS((t, N_HEADS * V_DIM), BF16), SDS((N_HEADS, t, LANES), F32)) + tuple(SDS(b.shape, b.dtype) for b in bufs)
    sem_shapes = _gather_sems(pieces) if nw else []
    outs = pl.pallas_call(
        body, out_shape=out_shape, grid=(N_HEADS, nq),
        in_specs=[pl.BlockSpec((bq, QK_PAD), lambda h, i: (i, h)), pl.BlockSpec((t, QK_PAD), lambda h, i: (0, h)),
                  pl.BlockSpec((t, V_DIM), lambda h, i: (0, N_HEADS + h))] + [ANY] * nw,
        out_specs=(pl.BlockSpec((bq, V_DIM), lambda h, i: (i, h)),
                   pl.BlockSpec((1, bq, LANES), lambda h, i: (h, i, 0))) + (ANY,) * nw,
        input_output_aliases={3 + n: 2 + n for n in range(nw)},
        scratch_shapes=[pltpu.VMEM((t, 2 * V_DIM), BF16), pltpu.VMEM((2, bq, bq), F32)] + sem_shapes,
        compiler_params=_cparams(("arbitrary", "arbitrary")), name="attn_fwd_gather" if nw else "attn_fwd")(qr, kf, kv, *bufs)
    return outs[0], outs[1], list(outs[2:])


def _attn_bwd(qr, kf, kv, do, lse, delta, exchange=None):
    t = qr.shape[0]
    bq = _attn_block(t)
    nq = t // bq
    pbs, lands = exchange if exchange is not None else ((), ())
    nw = len(pbs)

    def body(*refs):
        k_ref, v_ref, q_ref, do_ref, lse_ref, dl_ref = refs[:6]
        pb_refs = refs[6:6 + nw]
        dk_ref, dv_ref, dq_ref = refs[6 + 2 * nw:9 + 2 * nw]
        land_refs = refs[9 + 2 * nw:9 + 3 * nw]
        s_ref, dp_ref = refs[9 + 3 * nw], refs[10 + 3 * nw]
        sems = refs[11 + 3 * nw:]
        h, j = pl.program_id(0), pl.program_id(1)

        if nw:
            @pl.when((h == 0) & (j == 0))
            def _():
                _exchange_start(pb_refs, land_refs, *sems)

        @pl.when(j == 0)
        def _():
            dq_ref[...] = jnp.zeros_like(dq_ref)

        k = k_ref[...]
        v = v_ref[...]

        def issue(b, slot):
            off = pl.multiple_of((nq - 1 - b) * bq, bq)
            s_ref[slot] = lax.dot_general(q_ref[pl.ds(off, bq), :], k, _NT, preferred_element_type=F32)
            dp_ref[slot] = lax.dot_general(do_ref[pl.ds(off, bq), :], v, _NT, preferred_element_type=F32)

        def consume(b, slot, carry, masked):
            dk, dv = carry
            off = pl.multiple_of((nq - 1 - b) * bq, bq)
            q = q_ref[pl.ds(off, bq), :]
            do_ = do_ref[pl.ds(off, bq), :]
            lse2 = lse_ref[0, pl.ds(off, bq), :][:, :1] * LOG2E
            dl_i = dl_ref[0, pl.ds(off, bq), :][:, :1]
            s = s_ref[slot]
            if masked:
                s = jnp.where(_chunk_mask(bq), s, NEG_BIG)
            pr = jnp.exp2(s * EXP2_SCALE - lse2)
            dv = dv + lax.dot_general(pr.astype(BF16), do_, _TN, preferred_element_type=F32)
            ds = (pr * (dp_ref[slot] - dl_i)).astype(BF16)
            dk = dk + lax.dot_general(ds, q, _TN, preferred_element_type=F32)
            dq_ref[pl.ds(off, bq), :] += jnp.dot(ds, k, preferred_element_type=F32) * ATTN_SCALE
            return dk, dv

        init = (jnp.zeros((bq, QK_PAD), F32), jnp.zeros((bq, V_DIM), F32))
        dk, dv = _two_slot_pipeline(nq - 1 - j, issue, consume, init)
        dk_ref[...] = (dk * ATTN_SCALE).astype(BF16)
        dv_ref[...] = dv.astype(BF16)

        if nw:
            @pl.when((h == N_HEADS - 1) & (j == nq - 1))
            def _():
                _exchange_finish(pb_refs, land_refs, *sems)

    stat = pl.BlockSpec((1, t, LANES), lambda h, j: (h, 0, 0))
    out_shape = (SDS((t, N_HEADS * QK_PAD), BF16), SDS((t, N_HEADS * V_DIM), BF16), SDS((t, N_HEADS * QK_PAD), F32))
    sem_shapes = [pltpu.SemaphoreType.DMA((3 * nw,)), pltpu.SemaphoreType.DMA((3 * nw,))] if nw else []
    outs = pl.pallas_call(
        body, out_shape=out_shape + tuple(SDS(l.shape, l.dtype) for l in lands), grid=(N_HEADS, nq),
        in_specs=[pl.BlockSpec((bq, QK_PAD), lambda h, j: (j, h)), pl.BlockSpec((bq, V_DIM), lambda h, j: (j, N_HEADS + h)),
                  pl.BlockSpec((t, QK_PAD), lambda h, j: (0, h)), pl.BlockSpec((t, V_DIM), lambda h, j: (0, h)), stat, stat]
        + [ANY] * (2 * nw),
        out_specs=(pl.BlockSpec((bq, QK_PAD), lambda h, j: (j, h)), pl.BlockSpec((bq, V_DIM), lambda h, j: (j, h)),
                   pl.BlockSpec((t, QK_PAD), lambda h, j: (0, h))) + (ANY,) * nw,
        input_output_aliases={6 + nw + n: 3 + n for n in range(nw)},
        scratch_shapes=[pltpu.VMEM((2, bq, bq), F32), pltpu.VMEM((2, bq, bq), F32)] + sem_shapes,
        compiler_params=_cparams(("arbitrary", "arbitrary")),
        name="attn_bwd_exchange" if nw else "attn_bwd")(kf, kv, qr, do, lse, delta, *pbs, *lands)
    return outs[0], outs[1], outs[2], list(outs[3:])


def _merge_fwd(proj, lay, ya, yb):
    t, d = ya.shape
    tm = _tile(t, 512, 8)

    def body(gc_ref, gm_ref, ya_ref, yb_ref, o_ref):
        o_ref[...] = (_sigmoid(gc_ref[...].astype(F32)) * ya_ref[...].astype(F32)
                      + _sigmoid(gm_ref[...].astype(F32)) * yb_ref[...].astype(F32)).astype(BF16)

    return pl.pallas_call(
        body, out_shape=SDS((t, d), BF16), grid=(t // tm,),
        in_specs=[_colspec(tm, d, lay["gc"]), _colspec(tm, d, lay["gm"]), _rowspec(tm, d), _rowspec(tm, d)],
        out_specs=_rowspec(tm, d), compiler_params=_cparams(("parallel",)), name="merge_fwd")(proj, proj, ya, yb)


def _ple_bwd(dh, gp, pp):
    t, d = dh.shape
    tm = _tile(t, 512, 8)

    def body(dh_ref, gp_ref, pp_ref, dpp_ref, dgp_ref):
        g = dh_ref[...]
        s = _sigmoid(gp_ref[...].astype(F32))
        dpp_ref[...] = (g * s).astype(BF16)
        dgp_ref[...] = (g * pp_ref[...].astype(F32) * (s * (1.0 - s))).astype(BF16)

    r = _rowspec(tm, d)
    return pl.pallas_call(body, out_shape=(SDS((t, d), BF16),) * 2, grid=(t // tm,), in_specs=[r, r, r],
                          out_specs=(r, r), compiler_params=_cparams(("parallel",)), name="ple_bwd")(dh, gp, pp)


def _adamw(w, g, m, v, emit_grad=False):
    shape = w.shape
    cols = shape[-1]
    rows = w.size // cols
    tr = _tile(rows, max(8, BLOCK_ELEMS // cols // 8 * 8), 8)
    n_out = 4 if emit_grad else 3

    def body(w_ref, g_ref, m_ref, v_ref, d_ref, nm_ref, nv_ref, *g_out):
        g_ = g_ref[...]
        nm = ADAM_B1 * m_ref[...] + (1.0 - ADAM_B1) * g_
        nv = ADAM_B2 * v_ref[...] + (1.0 - ADAM_B2) * (g_ * g_)
        m_hat = nm / (1.0 - ADAM_B1 ** ADAM_STEP)
        v_hat = nv / (1.0 - ADAM_B2 ** ADAM_STEP)
        d_ref[...] = -ADAM_LR * (m_hat / (jnp.sqrt(v_hat) + ADAM_EPS) + ADAM_WD * w_ref[...])
        nm_ref[...] = nm
        nv_ref[...] = nv
        if emit_grad:
            g_out[0][...] = g_

    r = _rowspec(tr, cols)
    outs = pl.pallas_call(
        body, out_shape=(SDS((rows, cols), F32),) * n_out, grid=(rows // tr,), in_specs=[r, r, r, r],
        out_specs=(r,) * n_out, compiler_params=_cparams(("parallel",)),
        name="adamw")(*(a.reshape(rows, cols) for a in (w, g, m, v)))
    return tuple(o.reshape(shape) for o in outs)


ANY = pl.BlockSpec(memory_space=pl.ANY)


def _place():
    x, y, c = lax.axis_index("x"), lax.axis_index("y"), lax.axis_index("c")
    return x, y, c, [(1 - x, y), (x, 1 - y), (1 - x, 1 - y)]


def _half_rows(rows, cols):
    half = rows // 2
    return half, _tile(half, max(16, BLOCK_ELEMS // cols // 16 * 16), 16)


def _my_chip():
    return 2 * lax.axis_index("x") + lax.axis_index("y")


def _cast_into_slot(ws):
    k = len(ws)
    nl, r, c = ws[0].shape
    tr = _tile(r, max(16, BLOCK_ELEMS // c // 16 * 16), 16)

    def body(*refs):
        for n in range(k):
            refs[k + n][...] = refs[n][...].astype(BF16)

    return list(pl.pallas_call(
        body, out_shape=(SDS((nl, N_CHIPS, r, c), BF16),) * k, grid=(nl, r // tr),
        in_specs=[pl.BlockSpec((None, tr, c), lambda l, i: (l, i, 0))] * k,
        out_specs=(pl.BlockSpec((None, None, tr, c), lambda l, i: (l, _my_chip(), i, 0)),) * k,
        compiler_params=_cparams(("parallel", "parallel")), name="cast_into_slot")(*ws))


def _gather_copy(ref, layer, send_sems, recv_sems, sem, chip, half, to):
    r2 = ref.shape[2] // 2
    rows = ref.at[layer, chip, pl.ds(half * r2, r2)]
    return pltpu.make_async_remote_copy(src_ref=rows, dst_ref=rows, send_sem=send_sems.at[sem], recv_sem=recv_sems.at[sem],
                                        device_id=to, device_id_type=MESH)


def _gather_start(refs, pieces, send_sems, recv_sems):
    x, y, c, chips = _place()
    for pi, (ri, layer) in enumerate(pieces):
        for n, chip in enumerate(chips):
            _gather_copy(refs[ri], layer, send_sems, recv_sems, 6 * pi + n, 2 * x + y, c, (*chip, c)).start()


def _gather_finish(refs, pieces, send_sems, recv_sems):
    x, y, c, chips = _place()
    me, sibling = (x, y, c), (x, y, 1 - c)
    for pi, (ri, layer) in enumerate(pieces):
        for n, chip in enumerate(chips):
            k = 2 * chip[0] + chip[1]
            _gather_copy(refs[ri], layer, send_sems, recv_sems, 6 * pi + n, k, c, me).wait_recv()
            _gather_copy(refs[ri], layer, send_sems, recv_sems, 6 * pi + 3 + n, k, c, sibling).start()
    for pi, (ri, layer) in enumerate(pieces):
        for n, chip in enumerate(chips):
            _gather_copy(refs[ri], layer, send_sems, recv_sems, 6 * pi + 3 + n, 2 * chip[0] + chip[1], 1 - c, me).wait_recv()
    for pi, (ri, layer) in enumerate(pieces):
        for n, chip in enumerate(chips):
            _gather_copy(refs[ri], layer, send_sems, recv_sems, 6 * pi + n, 2 * x + y, c, (*chip, c)).wait_send()
            _gather_copy(refs[ri], layer, send_sems, recv_sems, 6 * pi + 3 + n, 2 * chip[0] + chip[1], c, sibling).wait_send()


def _gather_sems(pieces):
    return [pltpu.SemaphoreType.DMA((6 * len(pieces),)), pltpu.SemaphoreType.DMA((6 * len(pieces),))]


def _all_gather_weights(bufs, pieces):
    nw = len(bufs)

    def body(*refs):
        _gather_start(refs[nw:2 * nw], pieces, refs[2 * nw], refs[2 * nw + 1])
        _gather_finish(refs[nw:2 * nw], pieces, refs[2 * nw], refs[2 * nw + 1])

    return list(pl.pallas_call(
        body, out_shape=tuple(SDS(b.shape, b.dtype) for b in bufs), in_specs=[ANY] * nw, out_specs=(ANY,) * nw,
        input_output_aliases={i: i for i in range(nw)}, scratch_shapes=_gather_sems(pieces),
        name="all_gather_weights")(*bufs))


def _pair_swap(grads):
    nw = len(grads)

    def body(*refs):
        ins, outs = refs[:nw], refs[nw:2 * nw]
        send_sems, recv_sems = refs[2 * nw], refs[2 * nw + 1]
        x, y, c, _ = _place()
        copies = []
        for wi, (g_ref, o_ref) in enumerate(zip(ins, outs)):
            r2 = g_ref.shape[1] // 2
            copies.append(pltpu.make_async_remote_copy(
                src_ref=g_ref.at[:, pl.ds((1 - c) * r2, r2)], dst_ref=o_ref, send_sem=send_sems.at[wi],
                recv_sem=recv_sems.at[wi], device_id=(x, y, 1 - c), device_id_type=MESH))
            copies[-1].start()
        for cp in copies:
            cp.wait()

    return pl.pallas_call(
        body, out_shape=tuple(SDS((N_CHIPS, g.shape[1] // 2, g.shape[2]), g.dtype) for g in grads),
        in_specs=[ANY] * nw, out_specs=(ANY,) * nw,
        scratch_shapes=[pltpu.SemaphoreType.DMA((nw,)), pltpu.SemaphoreType.DMA((nw,))], name="pair_swap")(*grads)


def _same_shape_groups(arrays):
    groups = {}
    for i, arr in enumerate(arrays):
        groups.setdefault(tuple(arr.shape), []).append(i)
    return list(groups.values())


def _pair_add(gs, others):
    k = len(gs)
    _, r, c = gs[0].shape
    r2, tr = _half_rows(r, c)
    nb = r2 // tr

    def body(*refs):
        for n in range(k):
            g_ref, o_ref, pb_ref, land_ref = refs[n], refs[k + n], refs[2 * k + n], refs[3 * k + n]
            total = (g_ref[...].astype(F32) + o_ref[...].astype(F32)).astype(BF16)
            pb_ref[...] = total

            @pl.when(pl.program_id(1) == _my_chip())
            def _():
                land_ref[...] = total

    blk = pl.BlockSpec((None, tr, c), lambda j, n: (n, j, 0))
    mine = pl.BlockSpec((None, tr, c), lambda j, n: (n, lax.axis_index("c") * nb + j, 0))
    land = pl.BlockSpec((None, tr, c), lambda j, n: (_my_chip(), j, 0))
    outs = pl.pallas_call(
        body, out_shape=(SDS((N_CHIPS, r2, c), BF16),) * (2 * k), grid=(nb, N_CHIPS),
        in_specs=[mine] * k + [blk] * k, out_specs=(blk,) * k + (land,) * k,
        compiler_params=_cparams(("parallel", "arbitrary")), name="pair_add")(*gs, *others)
    return [(outs[n], outs[k + n]) for n in range(k)]


def _pair_add_all(gs, others):
    pairs = [None] * len(gs)
    for group in _same_shape_groups(gs):
        for i, pair in zip(group, _pair_add([gs[i] for i in group], [others[i] for i in group])):
            pairs[i] = pair
    return pairs


def _exchange_copy(p_ref, l_ref, send_sems, recv_sems, sem, src_slot, dst_slot, to):
    return pltpu.make_async_remote_copy(src_ref=p_ref.at[src_slot], dst_ref=l_ref.at[dst_slot], send_sem=send_sems.at[sem],
                                        recv_sem=recv_sems.at[sem], device_id=to, device_id_type=MESH)


def _exchange_start(p_refs, l_refs, send_sems, recv_sems):
    x, y, c, chips = _place()
    for wi, (p_ref, l_ref) in enumerate(zip(p_refs, l_refs)):
        for n, chip in enumerate(chips):
            _exchange_copy(p_ref, l_ref, send_sems, recv_sems, 3 * wi + n, 2 * chip[0] + chip[1], 2 * x + y, (*chip, c)).start()


def _exchange_finish(p_refs, l_refs, send_sems, recv_sems):
    x, y, c, chips = _place()
    for wi, (p_ref, l_ref) in enumerate(zip(p_refs, l_refs)):
        for n, chip in enumerate(chips):
            _exchange_copy(p_ref, l_ref, send_sems, recv_sems, 3 * wi + n, 2 * x + y, 2 * chip[0] + chip[1], (x, y, c)).wait_recv()
    for wi, (p_ref, l_ref) in enumerate(zip(p_refs, l_refs)):
        for n, chip in enumerate(chips):
            _exchange_copy(p_ref, l_ref, send_sems, recv_sems, 3 * wi + n, 2 * chip[0] + chip[1], 2 * x + y, (*chip, c)).wait_send()


def _chip_all_to_all(pbs, lands):
    nw = len(pbs)

    def body(*refs):
        _exchange_start(refs[:nw], refs[2 * nw:3 * nw], refs[3 * nw], refs[3 * nw + 1])
        _exchange_finish(refs[:nw], refs[2 * nw:3 * nw], refs[3 * nw], refs[3 * nw + 1])

    return list(pl.pallas_call(
        body, out_shape=tuple(SDS(l.shape, l.dtype) for l in lands), in_specs=[ANY] * (2 * nw), out_specs=(ANY,) * nw,
        input_output_aliases={nw + i: i for i in range(nw)},
        scratch_shapes=[pltpu.SemaphoreType.DMA((3 * nw,)), pltpu.SemaphoreType.DMA((3 * nw,))],
        name="chip_all_to_all")(*pbs, *lands))


def _sum_chips(lands, gstacks, layer):
    k = len(lands)
    _, r, c = gstacks[0].shape
    r2, tr = _half_rows(r, c)
    nb = r2 // tr

    def body(*refs):
        for n in range(k):
            l_ref, out_ref = refs[n], refs[2 * k + n]
            out_ref[...] = ((l_ref[0].astype(F32) + l_ref[1].astype(F32)) + l_ref[2].astype(F32)) + l_ref[3].astype(F32)

    return list(pl.pallas_call(
        body, out_shape=tuple(SDS(g.shape, F32) for g in gstacks), grid=(nb,),
        in_specs=[pl.BlockSpec((N_CHIPS, tr, c), lambda j: (0, j, 0))] * k + [ANY] * k,
        out_specs=(pl.BlockSpec((None, tr, c), lambda j: (layer, lax.axis_index("c") * nb + j, 0)),) * k,
        input_output_aliases={k + n: n for n in range(k)}, compiler_params=_cparams(("parallel",)),
        name="sum_chips")(*lands, *gstacks))


def _sum_chips_all(lands, gstacks, layer):
    out = [None] * len(lands)
    for group in _same_shape_groups(gstacks):
        for i, g in zip(group, _sum_chips([lands[i] for i in group], [gstacks[i] for i in group], layer)):
            out[i] = g
    return out


def _pair_gather(gstacks, layer):
    nw = len(gstacks)

    def body(*refs):
        outs = refs[nw:2 * nw]
        send_sems, recv_sems = refs[2 * nw], refs[2 * nw + 1]
        x, y, c, _ = _place()

        def copy(ref, wi, half):
            r2 = ref.shape[1] // 2
            blk = ref.at[layer, pl.ds(half * r2, r2)]
            return pltpu.make_async_remote_copy(src_ref=blk, dst_ref=blk, send_sem=send_sems.at[wi],
                                                recv_sem=recv_sems.at[wi], device_id=(x, y, 1 - c), device_id_type=MESH)

        sent = [copy(ref, wi, c) for wi, ref in enumerate(outs)]
        for cp in sent:
            cp.start()
        for wi, ref in enumerate(outs):
            copy(ref, wi, 1 - c).wait_recv()
        for cp in sent:
            cp.wait_send()

    return pl.pallas_call(
        body, out_shape=tuple(SDS(g.shape, g.dtype) for g in gstacks), in_specs=[ANY] * nw, out_specs=(ANY,) * nw,
        input_output_aliases={i: i for i in range(nw)},
        scratch_shapes=[pltpu.SemaphoreType.DMA((nw,)), pltpu.SemaphoreType.DMA((nw,))], name="pair_gather")(*gstacks)


class _Rider:
    def __init__(self, tag, operands, out_shapes, aliases, sem_shapes, start, finish):
        self.tag, self.operands, self.out_shapes, self.aliases = tag, list(operands), list(out_shapes), dict(aliases)
        self.sem_shapes, self._start, self._finish = list(sem_shapes), start, finish
        self._refs = None

    def split(self, refs, n_outs, n_scratch, grid):
        n_in, n_out = len(self.operands), len(self.out_shapes)
        ins = refs[:n_in]
        own_outs = refs[n_in:n_in + n_outs]
        outs = refs[n_in + n_outs:n_in + n_outs + n_out]
        own_scratch = refs[n_in + n_outs + n_out:n_in + n_outs + n_out + n_scratch]
        sems = refs[n_in + n_outs + n_out + n_scratch:]
        self._refs = (ins, outs, sems)
        first = functools.reduce(lambda p, q: p & q, [pl.program_id(ax) == 0 for ax in range(len(grid))])

        @pl.when(first)
        def _():
            self._start(ins, outs, sems)

        return tuple(own_outs) + tuple(own_scratch)

    def finish_at_last_step(self, grid):
        ins, outs, sems = self._refs
        last = functools.reduce(lambda p, q: p & q, [pl.program_id(ax) == g - 1 for ax, g in enumerate(grid)])

        @pl.when(last)
        def _():
            self._finish(ins, outs, sems)


def _gather_rider(bufs, pieces):
    nw = len(bufs)
    return _Rider("_gather", bufs, [SDS(b.shape, b.dtype) for b in bufs], {i: i for i in range(nw)}, _gather_sems(pieces),
                  lambda ins, outs, sems: _gather_start(outs, pieces, *sems),
                  lambda ins, outs, sems: _gather_finish(outs, pieces, *sems))


def _pair_swap_copies(ins, outs, sems):
    x, y, c, _ = _place()
    copies = []
    for wi, (g_ref, o_ref) in enumerate(zip(ins, outs)):
        r2 = g_ref.shape[1] // 2
        copies.append(pltpu.make_async_remote_copy(
            src_ref=g_ref.at[:, pl.ds((1 - c) * r2, r2)], dst_ref=o_ref, send_sem=sems[0].at[wi],
            recv_sem=sems[1].at[wi], device_id=(x, y, 1 - c), device_id_type=MESH))
    return copies


def _pair_swap_rider(grads):
    nw = len(grads)

    def start(ins, outs, sems):
        for cp in _pair_swap_copies(ins, outs, sems):
            cp.start()

    def finish(ins, outs, sems):
        for cp in _pair_swap_copies(ins, outs, sems):
            cp.wait()

    return _Rider("_swap", grads, [SDS((N_CHIPS, g.shape[1] // 2, g.shape[2]), g.dtype) for g in grads], {},
                  [pltpu.SemaphoreType.DMA((nw,)), pltpu.SemaphoreType.DMA((nw,))], start, finish)


def _pair_gather_copy(ref, layer, sems, wi, half, peer):
    r2 = ref.shape[1] // 2
    blk = ref.at[layer, pl.ds(half * r2, r2)]
    return pltpu.make_async_remote_copy(src_ref=blk, dst_ref=blk, send_sem=sems[0].at[wi], recv_sem=sems[1].at[wi],
                                        device_id=peer, device_id_type=MESH)


def _pair_gather_rider(gstacks, layer):
    nw = len(gstacks)

    def start(ins, outs, sems):
        x, y, c, _ = _place()
        for wi, ref in enumerate(outs):
            _pair_gather_copy(ref, layer, sems, wi, c, (x, y, 1 - c)).start()

    def finish(ins, outs, sems):
        x, y, c, _ = _place()
        for wi, ref in enumerate(outs):
            _pair_gather_copy(ref, layer, sems, wi, 1 - c, (x, y, 1 - c)).wait_recv()
        for wi, ref in enumerate(outs):
            _pair_gather_copy(ref, layer, sems, wi, c, (x, y, 1 - c)).wait_send()

    return _Rider("_pair_gather", gstacks, [SDS(g.shape, g.dtype) for g in gstacks], {i: i for i in range(nw)},
                  [pltpu.SemaphoreType.DMA((nw,)), pltpu.SemaphoreType.DMA((nw,))], start, finish)


def _all_gather_small(vec, name):
    rows, w = vec.shape

    def body(v_ref, sum_ref, all_ref, send_sems, recv_sems):
        x, y, c, chips = _place()
        me, sibling = (x, y, c), (x, y, 1 - c)

        def slot(px, py, pc):
            return all_ref.at[4 * px + 2 * py + pc]

        def copy(k, block, to, src=None):
            return pltpu.make_async_remote_copy(
                src_ref=slot(*block) if src is None else src, dst_ref=slot(*block), send_sem=send_sems.at[k],
                recv_sem=recv_sems.at[k], device_id=to, device_id_type=MESH)

        first = [copy(0, me, sibling, src=v_ref)]
        first += [copy(1 + n, me, (*chip, c), src=v_ref) for n, chip in enumerate(chips)]
        for cp in first:
            cp.start()
        slot(*me)[...] = v_ref[...]
        passed = [copy(4 + n, (*chip, c), sibling) for n, chip in enumerate(chips)]
        for n, chip in enumerate(chips):
            copy(1 + n, (*chip, c), me).wait_recv()
            passed[n].start()
        copy(0, sibling, me).wait_recv()
        for n, chip in enumerate(chips):
            copy(4 + n, (*chip, 1 - c), me).wait_recv()
        for cp in first + passed:
            cp.wait_send()
        total = all_ref[0]
        for dev in range(1, 8):
            total = total + all_ref[dev]
        sum_ref[...] = total

    vm = pl.BlockSpec(memory_space=pltpu.VMEM)
    return pl.pallas_call(
        body, out_shape=(SDS((rows, w), F32), SDS((8, rows, w), F32)), in_specs=[vm], out_specs=(vm, vm),
        scratch_shapes=[pltpu.SemaphoreType.DMA((7,)), pltpu.SemaphoreType.DMA((7,))], name=name)(vec)


def _to_rows128(flat):
    n = flat.shape[0]
    rows = -(-n // (8 * LANES)) * 8
    return jnp.pad(flat, (0, rows * LANES - n)).reshape(rows, LANES)


def _in_layout(conv, ql, kvl, d):
    lay = {"conv": conv, "ql": ql, "kvl": kvl, "d": d}
    lay["q"] = 3 * conv
    lay["kr"] = lay["q"] + ql
    lay["gc"] = lay["kr"] + LANES
    lay["gm"] = lay["gc"] + d
    lay["kv"] = lay["gm"] + d
    used = lay["kv"] + kvl
    lay["width"] = -(-used // 512) * 512
    return lay


def _w_in_to_layout(w, lay):
    conv, ql, kvl, d = lay["conv"], lay["ql"], lay["kvl"], lay["d"]
    o_kv = 3 * conv + ql
    o_kr = o_kv + kvl
    o_g = o_kr + ROPE_DIM
    lead = w.shape[:-1]
    parts = [w[..., :o_kv], w[..., o_kr:o_g], jnp.zeros(lead + (LANES - ROPE_DIM,), w.dtype), w[..., o_g:o_g + 2 * d],
             w[..., o_kv:o_kr], jnp.zeros(lead + (lay["width"] - lay["kv"] - kvl,), w.dtype)]
    return jnp.concatenate(parts, axis=-1)


def _w_in_from_layout(g, lay):
    ql, kvl, d = lay["ql"], lay["kvl"], lay["d"]
    return jnp.concatenate([g[:, :lay["q"] + ql], g[:, lay["kv"]:lay["kv"] + kvl], g[:, lay["kr"]:lay["kr"] + ROPE_DIM],
                            g[:, lay["gc"]:lay["gc"] + 2 * d]], axis=1)


def _w_uq_to_layout(w):
    r = w.shape[0]
    w3 = w.reshape(r, N_HEADS, NOPE_DIM + ROPE_DIM)
    return jnp.pad(w3, ((0, 0), (0, 0), (0, QK_PAD - NOPE_DIM - ROPE_DIM))).reshape(r, N_HEADS * QK_PAD)


def _w_uq_from_layout(g):
    r = g.shape[0]
    return g.reshape(r, N_HEADS, QK_PAD)[:, :, :NOPE_DIM + ROPE_DIM].reshape(r, N_HEADS * (NOPE_DIM + ROPE_DIM))


def _w_ukv_to_layout(w):
    r = w.shape[0]
    return w.reshape(r, N_HEADS, 2, NOPE_DIM).transpose(0, 2, 1, 3).reshape(r, 2 * N_HEADS * NOPE_DIM)


def _w_ukv_from_layout(g):
    r = g.shape[0]
    return g.reshape(r, 2, N_HEADS, NOPE_DIM).transpose(0, 2, 1, 3).reshape(r, 2 * N_HEADS * NOPE_DIM)


def _chips_to_cols(buf):
    _, r, c = buf.shape
    return buf.transpose(1, 0, 2).reshape(r, N_CHIPS * c)


def _cols_to_chips(g):
    r, c4 = g.shape
    return g.reshape(r, N_CHIPS, c4 // N_CHIPS).transpose(1, 0, 2)


BEFORE_FFN1_GU = ("ffn1_w_gu",)
BEFORE_FFN1_DOWN = ("ffn1_w_down", "w_uq", "w_ukv")
IN_ATTENTION = ("w_in", "w_conv_out", "ffn2_w_gu", "ffn2_w_down", "w_mla_out", "w_o", "w_ple_gate", "w_ple_proj")
LAYER0_FIRST = ("ffn1_w_gu", "ffn1_w_down")
LAYER0_IN_FFN1_GU = ("w_in", "w_conv_out", "w_uq", "w_ukv")
LAYER0_IN_FFN1_DOWN = ("ffn2_w_down", "w_mla_out")
LAYER0_IN_PROJ = ("ffn2_w_gu",)
LAYER0_IN_ATTENTION = ("w_o", "w_ple_gate", "w_ple_proj")


def _gather_plan(layer, depth):
    plan = {}
    if layer == 0:
        plan["ffn1_gu"] = [(k, 0) for k in LAYER0_IN_FFN1_GU]
        plan["ffn1_down"] = [(k, 0) for k in LAYER0_IN_FFN1_DOWN]
        plan["in"] = [(k, 0) for k in LAYER0_IN_PROJ]
        plan["attn"] = [(k, 0) for k in LAYER0_IN_ATTENTION]
    if layer + 1 < depth:
        plan["attn"] = plan.get("attn", []) + [(k, layer + 1) for k in IN_ATTENTION]
        plan["ffn2_gu"] = [(k, layer + 1) for k in BEFORE_FFN1_GU]
        plan["ffn2_down"] = [(k, layer + 1) for k in BEFORE_FFN1_DOWN]
    return plan


def _plan_operands(bufs, todo):
    names = list(dict.fromkeys(name for name, _ in todo))
    return [bufs[k] for k in names], names, [(names.index(name), layer) for name, layer in todo]


def _rows_view(buf):
    return buf.reshape(buf.shape[0], N_CHIPS * buf.shape[2], buf.shape[3])


def _cols_view(buf, layer):
    return _chips_to_cols(buf[layer])


def _ffn_fwd(h, n, bufs, which, layer, next_gain, plan):
    tag = which
    rider = None
    if plan.get(which + "_gu"):
        ops, names, pieces = _plan_operands(bufs, plan[which + "_gu"])
        rider = _gather_rider(ops, pieces)
    outs = _gu_swiglu_fwd(n, bufs[which + "_w_gu"], layer, tag + "_gu_fwd", rider=rider)
    gu, a = outs[0], outs[1]
    if rider is not None:
        bufs = {**bufs, **dict(zip(names, outs[2]))}
    rider = None
    if plan.get(which + "_down"):
        ops, names, pieces = _plan_operands(bufs, plan[which + "_down"])
        rider = _gather_rider(ops, pieces)
    outs = _mm(a, _rows_view(bufs[which + "_w_down"]), "nn", F32, tag + "_down_fwd", scale=0.5, res=h, layer=layer,
               norm_gain=next_gain, rider=rider)
    if rider is not None:
        bufs = {**bufs, **dict(zip(names, outs[2]))}
    return outs[0], outs[1], (h, n, gu, a), bufs


def _ffn_bwd(dh, dhb, saved, gain, bufs, which, layer, swap=None):
    tag = which
    h, n, gu, a = saved
    d_wdown = _mm(a, dhb, "tn", BF16, tag + "_down_dw", scale=0.5)
    dgu = _down_dx_swiglu_bwd(dhb, _rows_view(bufs[which + "_w_down"]), gu, layer, tag + "_down_dx")
    rider = _pair_swap_rider(swap) if swap is not None else None
    outs = _mm(n, dgu, "tn", BF16, tag + "_gu_dw", out_chip=True, b_halves=True, rider=rider)
    d_wgu, swapped = (outs[0], outs[1]) if rider is not None else (outs, None)
    dh, dhb, dgain = _mm(dgu, bufs[which + "_w_gu"], "nt", BF16, tag + "_gu_dx", layer=layer, b_chip=True, a_halves=True,
                         norm_bwd=(h, gain, dh))
    return dh, dhb, d_wgu, d_wdown, dgain, swapped


def _layer_fwd(h0, n0, p_i, bufs, conv_w, norms, layer, lay, tables, depth, next_gain):
    plan = _gather_plan(layer, depth)
    h1, n2, s_ffn1, bufs = _ffn_fwd(h0, n0, bufs, "ffn1", layer, norms["mix_norm"], plan)
    w_in = _w_in_to_layout(_cols_view(bufs["w_in"], layer), lay)
    w_conv_out = _cols_view(bufs["w_conv_out"], layer)
    w_uq = _w_uq_to_layout(_cols_view(bufs["w_uq"], layer))
    w_ukv = _w_ukv_to_layout(_cols_view(bufs["w_ukv"], layer))
    if plan.get("in"):
        ops, names, pieces = _plan_operands(bufs, plan["in"])
        proj, gathered = _mm(n2, w_in, "nn", BF16, "in_fwd", rider=_gather_rider(ops, pieces))
        bufs = {**bufs, **dict(zip(names, gathered))}
    else:
        proj = _mm(n2, w_in, "nn", BF16, "in_fwd")
    cb = _conv_fwd(proj, conv_w)
    ya = _mm(cb, w_conv_out, "nn", BF16, "conv_out_fwd")
    qn, kvn = _qkvnorm_fwd(proj, lay, norms["q_norm"], norms["kv_norm"])
    qf = _mm(qn, w_uq, "nn", BF16, "uq_fwd")
    kv = _mm(kvn, w_ukv, "nn", BF16, "ukv_fwd")
    qr, kf = _rope_fwd(qf, kv, proj, lay, tables)
    if plan.get("attn"):
        ops, names, pieces = _plan_operands(bufs, plan["attn"])
        o, lse, gathered = _attn_fwd(qr, kf, kv, gather=(ops, pieces))
        bufs = {**bufs, **dict(zip(names, gathered))}
    else:
        o, lse, _ = _attn_fwd(qr, kf, kv)
    yb = _mm(o, _rows_view(bufs["w_mla_out"]), "nn", BF16, "mla_out_fwd", layer=layer)
    mg = _merge_fwd(proj, lay, ya, yb)
    h2, n3 = _mm(mg, _rows_view(bufs["w_o"]), "nn", F32, "o_fwd", res=h1, layer=layer, norm_gain=norms["ffn2_norm"])
    h3, n4, s_ffn2, bufs = _ffn_fwd(h2, n3, bufs, "ffn2", layer, norms["ple_norm"], plan)
    w_ple_proj = _cols_view(bufs["w_ple_proj"], layer)
    pp = _mm(p_i, w_ple_proj, "nn", BF16, "ple_proj_fwd")
    outs = _mm(n4, _rows_view(bufs["w_ple_gate"]), "nn", BF16, "ple_gate_fwd", layer=layer,
               epilogue=_ple_fwd_epilogue(h3, pp, next_gain))
    gp, h4, n_out = outs[0], outs[1], (outs[2] if next_gain is not None else None)
    saved = dict(s_ffn1=s_ffn1, h1=h1, n2=n2, proj=proj, cb=cb, ya=ya, qn=qn, kvn=kvn, qr=qr, kf=kf, kv=kv, o=o,
                 lse=lse, yb=yb, mg=mg, h2=h2, s_ffn2=s_ffn2, h3=h3, n4=n4, gp=gp, pp=pp, p=p_i,
                 w_in=w_in, w_conv_out=w_conv_out, w_uq=w_uq, w_ukv=w_ukv)
    return h4, n_out, saved, bufs


def _layer_bwd(dh, s, bufs, conv_w, norms, layer, lay, tables, above, gstacks):
    gw, gn = {}, {}

    def by_rows(g):
        return g.reshape(N_CHIPS, g.shape[0] // N_CHIPS, g.shape[1])

    dpp, dgp = _ple_bwd(dh, s["gp"], s["pp"])
    gw["w_ple_proj"] = _cols_to_chips(_mm(s["p"], dpp, "tn", BF16, "ple_proj_dw"))
    gw["w_ple_gate"] = by_rows(_mm(s["n4"], dgp, "tn", BF16, "ple_gate_dw"))
    dh, dhb, gn["ple_norm"] = _mm(dgp, _rows_view(bufs["w_ple_gate"]), "nt", BF16, "ple_gate_dx", layer=layer,
                                  norm_bwd=(s["h3"], norms["ple_norm"], dh))
    dh, dhb, gw["ffn2_w_gu"], g_down, gn["ffn2_norm"], swapped = _ffn_bwd(
        dh, dhb, s["s_ffn2"], norms["ffn2_norm"], bufs, "ffn2", layer, swap=None if above is None else above[1])
    gw["ffn2_w_down"] = by_rows(g_down)
    exchange = None
    if above is not None:
        pairs = _pair_add_all(above[1], swapped)
        exchange = ([pb for pb, _ in pairs], [land for _, land in pairs])
    gw["w_o"] = by_rows(_mm(s["mg"], dhb, "tn", BF16, "o_dw"))
    dya, dyb, dgc, dgm = _mm(dhb, _rows_view(bufs["w_o"]), "nt", BF16, "o_dx", layer=layer,
                             epilogue=_merge_bwd_epilogue(s["proj"], lay, s["ya"], s["yb"]))
    gw["w_conv_out"] = _cols_to_chips(_mm(s["cb"], dya, "tn", BF16, "conv_out_dw"))
    dcb = _mm(dya, s["w_conv_out"], "nt", BF16, "conv_out_dx")
    db, dc, dv_conv, g_conv = _conv_bwd(s["proj"], conv_w, dcb)
    gw["w_mla_out"] = by_rows(_mm(s["o"], dyb, "tn", BF16, "mla_out_dw"))
    do, delta = _mm(dyb, _rows_view(bufs["w_mla_out"]), "nt", BF16, "mla_out_dx", layer=layer,
                    epilogue=_delta_epilogue(s["o"]))
    dkf, dv, dqr, landed = _attn_bwd(s["qr"], s["kf"], s["kv"], do, s["lse"], delta, exchange=exchange)
    dqf, dkv, dkr = _rope_bwd(dqr, dkf, dv, tables)
    gw["w_uq"] = _cols_to_chips(_w_uq_from_layout(_mm(s["qn"], dqf, "tn", BF16, "uq_dw")))
    dqn = _mm(dqf, s["w_uq"], "nt", BF16, "uq_dx")
    gw["w_ukv"] = _cols_to_chips(_w_ukv_from_layout(_mm(s["kvn"], dkv, "tn", BF16, "ukv_dw")))
    dkvn = _mm(dkv, s["w_ukv"], "nt", BF16, "ukv_dx")
    dqc, dkvc, gn["q_norm"], gn["kv_norm"] = _qkvnorm_bwd(s["proj"], lay, norms["q_norm"], norms["kv_norm"], dqn, dkvn)
    t = dh.shape[0]
    dproj = jnp.concatenate([db, dc, dv_conv, dqc, dkr, dgc, dgm, dkvc,
                             jnp.zeros((t, lay["width"] - lay["kv"] - lay["kvl"]), BF16)], axis=1)
    rider = None
    if above is not None:
        gstacks = _sum_chips_all(landed, gstacks, above[0])
        rider = _pair_gather_rider(gstacks, above[0])
    outs = _mm(s["n2"], dproj, "tn", BF16, "in_dw", rider=rider)
    g_in, gstacks = (outs[0], outs[1]) if rider is not None else (outs, gstacks)
    gw["w_in"] = _cols_to_chips(_w_in_from_layout(g_in, lay))
    dh, dhb, gn["mix_norm"] = _mm(dproj, s["w_in"], "nt", BF16, "in_dx", norm_bwd=(s["h1"], norms["mix_norm"], dh))
    dh, dhb, gw["ffn1_w_gu"], g_down, gn["ffn1_norm"], _ = _ffn_bwd(
        dh, dhb, s["s_ffn1"], norms["ffn1_norm"], bufs, "ffn1", layer)
    gw["ffn1_w_down"] = by_rows(g_down)
    return dh, gw, g_conv, gn, gstacks


def _rope_tables(positions):
    half = ROPE_DIM // 2
    inv_freq = ROPE_THETA ** (-jnp.arange(0, ROPE_DIM, 2, dtype=F32) / ROPE_DIM)
    ang = positions.astype(F32)[:, None] * inv_freq
    cos, sin = jnp.cos(ang), jnp.sin(ang)
    zeros = jnp.zeros_like(cos)
    cos_t = jnp.concatenate([cos, cos, zeros, zeros], axis=1)
    sin_a = jnp.concatenate([-sin, zeros, zeros, zeros], axis=1)
    sin_b = jnp.concatenate([zeros, sin, zeros, zeros], axis=1)
    assert cos_t.shape[1] == LANES and half * 4 == LANES
    return cos_t, sin_a, sin_b


def kernel(x, p, positions, ffn1_norm, ffn1_w_gu, ffn1_w_down, mix_norm, w_in, conv_w, w_conv_out, q_norm, kv_norm, w_uq, w_ukv, w_mla_out, w_o, ffn2_norm, ffn2_w_gu, ffn2_w_down, ple_norm, w_ple_gate, w_ple_proj, final_norm, loss_target, m_ffn1_norm, m_ffn1_w_gu, m_ffn1_w_down, m_mix_norm, m_w_in, m_conv_w, m_w_conv_out, m_q_norm, m_kv_norm, m_w_uq, m_w_ukv, m_w_mla_out, m_w_o, m_ffn2_norm, m_ffn2_w_gu, m_ffn2_w_down, m_ple_norm, m_w_ple_gate, m_w_ple_proj, m_final_norm, v_ffn1_norm, v_ffn1_w_gu, v_ffn1_w_down, v_mix_norm, v_w_in, v_conv_w, v_w_conv_out, v_q_norm, v_kv_norm, v_w_uq, v_w_ukv, v_w_mla_out, v_w_o, v_ffn2_norm, v_ffn2_w_gu, v_ffn2_w_down, v_ple_norm, v_w_ple_gate, v_w_ple_proj, v_final_norm):
    args = dict(zip(ARG_NAMES, (x, p, positions, ffn1_norm, ffn1_w_gu, ffn1_w_down, mix_norm, w_in, conv_w, w_conv_out, q_norm, kv_norm, w_uq, w_ukv, w_mla_out, w_o, ffn2_norm, ffn2_w_gu, ffn2_w_down, ple_norm, w_ple_gate, w_ple_proj, final_norm, loss_target, m_ffn1_norm, m_ffn1_w_gu, m_ffn1_w_down, m_mix_norm, m_w_in, m_conv_w, m_w_conv_out, m_q_norm, m_kv_norm, m_w_uq, m_w_ukv, m_w_mla_out, m_w_o, m_ffn2_norm, m_ffn2_w_gu, m_ffn2_w_down, m_ple_norm, m_w_ple_gate, m_w_ple_proj, m_final_norm, v_ffn1_norm, v_ffn1_w_gu, v_ffn1_w_down, v_mix_norm, v_w_in, v_conv_w, v_w_conv_out, v_q_norm, v_kv_norm, v_w_uq, v_w_ukv, v_w_mla_out, v_w_o, v_ffn2_norm, v_ffn2_w_gu, v_ffn2_w_down, v_ple_norm, v_w_ple_gate, v_w_ple_proj, v_final_norm)))
    depth = ffn1_norm.shape[0]
    t, d = x.shape[1], x.shape[2]
    conv = conv_w.shape[-1] * N_CHIPS
    lay = _in_layout(conv, q_norm.shape[-1], kv_norm.shape[-1], d)
    chip = 2 * lax.axis_index("x") + lax.axis_index("y")
    tables = _rope_tables(positions[0])

    bufs = {}
    for group in _same_shape_groups([args[name] for name in BIG]):
        bufs.update(zip([BIG[i] for i in group], _cast_into_slot([args[BIG[i]] for i in group])))
    bufs.update(zip(LAYER0_FIRST, _all_gather_weights([bufs[k] for k in LAYER0_FIRST], [(i, 0) for i in range(len(LAYER0_FIRST))])))
    conv_rows = depth * conv_w.shape[1]
    conv_all = _all_gather_small(_to_rows128(conv_w.reshape(-1)), "all_gather_conv_w")[1]
    conv_full = conv_all[0::2, :conv_rows].reshape(N_CHIPS, depth, conv_w.shape[1], LANES)
    conv_full = conv_full.transpose(1, 2, 0, 3).reshape(depth, conv_w.shape[1], conv)
    norms = [{name: args[name][i] for name in REPLICATED} for i in range(depth)]
    p3 = p.reshape(depth, t, p.shape[-1])

    h = x[0]
    n = _rmsnorm_fwd(h, norms[0]["ffn1_norm"], "first_norm_fwd")
    saved = []
    for i in range(depth):
        h, n, s, bufs = _layer_fwd(h, n, p3[i], bufs, conv_full[i], norms[i], i, lay, tables, depth,
                                   norms[i + 1]["ffn1_norm"] if i + 1 < depth else None)
        saved.append(s)
    loss_part, dh, _, g_final = _loss_head(h, final_norm, loss_target[0])
    loss = lax.psum(loss_part[0, 0], ("x", "y", "c"))

    gstacks = [lax.empty(args[name].shape, F32) for name in BIG]
    norm_grads, conv_grads = [None] * depth, [None] * depth
    above = None
    for i in reversed(range(depth)):
        dh, gw, conv_grads[i], norm_grads[i], gstacks = _layer_bwd(
            dh, saved[i], bufs, conv_full[i], norms[i], i, lay, tables, above, gstacks)
        above = (i, [gw[name] for name in BIG])
    pairs = _pair_add_all(above[1], _pair_swap(above[1]))
    landed = _chip_all_to_all([pb for pb, _ in pairs], [land for _, land in pairs])
    gstacks = _pair_gather(_sum_chips_all(landed, gstacks, above[0]), above[0])
    grad_x = dh[None]
    grads = dict(zip(BIG, gstacks))

    pieces = [norm_grads[i][name].reshape(-1) for i in range(depth) for name in REPLICATED]
    pieces += [g_final.reshape(-1)] + [conv_grads[i].reshape(-1) for i in range(depth)]
    vec = _all_gather_small(_to_rows128(jnp.concatenate(pieces)), "all_sum_small")[0].reshape(-1)
    off = 0
    per_name = {name: [] for name in REPLICATED}
    for i in range(depth):
        for name in REPLICATED:
            size = args[name].shape[1]
            per_name[name].append(vec[off:off + size])
            off += size
    for name in REPLICATED:
        grads[name] = jnp.stack(per_name[name])
    grads["final_norm"] = vec[off:off + d]
    off += d
    conv_g = vec[off:off + depth * 3 * conv].reshape(depth, 3, conv)
    grads["conv_w"] = lax.dynamic_slice_in_dim(conv_g, chip * conv_w.shape[-1], conv_w.shape[-1], axis=2)

    delta, new_m, new_v = {}, {}, {}
    for name in WEIGHTS:
        w_, g_, m_, v_ = args[name], grads[name], args["m_" + name], args["v_" + name]
        if w_.ndim == 1:
            outs = _adamw(w_[None], g_[None], m_[None], v_[None])
            delta[name], new_m[name], new_v[name] = (o[0] for o in outs)
        elif name in BIG:
            delta[name], new_m[name], new_v[name], grads[name] = _adamw(w_, g_, m_, v_, emit_grad=True)
        else:
            delta[name], new_m[name], new_v[name] = _adamw(w_, g_, m_, v_)
    return (loss, grad_x, *[grads[n] for n in WEIGHTS], *[delta[n] for n in WEIGHTS],
            *[new_m[n] for n in WEIGHTS], *[new_v[n] for n in WEIGHTS])
```

```python
import functools

import jax
import jax.numpy as jnp
from jax import lax
from jax.experimental import pallas as pl
from jax.experimental.pallas import tpu as pltpu

BF16 = jnp.bfloat16
F32 = jnp.float32
SDS = jax.ShapeDtypeStruct
MESH = pl.DeviceIdType.MESH

N_HEADS = 8
NOPE_DIM = 128
ROPE_DIM = 64
V_DIM = 128
QK_PAD = 256
CHUNK = 64
ROPE_THETA = 10000.0
EPS = 1e-6
ATTN_SCALE = (NOPE_DIM + ROPE_DIM) ** -0.5
NEG_BIG = -1e30

ADAM_LR = 0.001
ADAM_B1 = 0.9
ADAM_B2 = 0.999
ADAM_EPS = 1e-08
ADAM_WD = 0.01
ADAM_STEP = 10

LANES = 128
N_CHIPS = 4
VMEM_LIMIT_BYTES = 56 * 1024 * 1024
ACC_BYTES = 6 * 1024 * 1024
BLOCK_ELEMS = 1 << 19

SHARDED = (("ffn1_w_gu", 1), ("ffn1_w_down", 0), ("w_in", 1), ("w_conv_out", 1), ("w_uq", 1), ("w_ukv", 1),
           ("w_mla_out", 0), ("w_o", 0), ("ffn2_w_gu", 1), ("ffn2_w_down", 0), ("w_ple_gate", 0), ("w_ple_proj", 1))
BIG = tuple(name for name, _ in SHARDED)
REPLICATED = ("ffn1_norm", "mix_norm", "q_norm", "kv_norm", "ffn2_norm", "ple_norm")
WEIGHTS = ("ffn1_norm", "ffn1_w_gu", "ffn1_w_down", "mix_norm", "w_in", "conv_w", "w_conv_out", "q_norm",
           "kv_norm", "w_uq", "w_ukv", "w_mla_out", "w_o", "ffn2_norm", "ffn2_w_gu", "ffn2_w_down",
           "ple_norm", "w_ple_gate", "w_ple_proj", "final_norm")
ARG_NAMES = ("x", "p", "positions") + WEIGHTS + ("loss_target",) + tuple("m_" + n for n in WEIGHTS) + tuple(
    "v_" + n for n in WEIGHTS)


def _cparams(semantics=None):
    return pltpu.CompilerParams(dimension_semantics=semantics, vmem_limit_bytes=VMEM_LIMIT_BYTES)


def _tile(n, cap, mult=LANES):
    best = None
    for t in range(mult, min(n, cap) + 1, mult):
        if n % t == 0:
            best = t
    return n if best is None else best


def _sigmoid(x):
    return 1.0 / (1.0 + jnp.exp(-x))


def _rowspec(tm, width, col_block=0):
    return pl.BlockSpec((tm, width), lambda i: (i, col_block))


def _colspec(tm, width, offset):
    assert offset % width == 0, (width, offset)
    return _rowspec(tm, width, offset // width)


def _mm(a, b, mode, out_dtype, name, scale=None, res=None, layer=None, b_chip=False, out_chip=False, norm_gain=None,
        a_halves=False, b_halves=False, norm_bwd=None, rider=None, epilogue=None):
    bshape = b.shape if layer is None else b.shape[1:]
    if b_chip:
        bshape = (bshape[1], N_CHIPS * bshape[2])
    if b_halves:
        bshape = (bshape[1], 2 * bshape[2])
    ashape = (a.shape[1], 2 * a.shape[2]) if a_halves else a.shape
    if mode == "nn":
        (m, k), (k2, n) = ashape, bshape
    elif mode == "nt":
        (m, k), (n, k2) = ashape, bshape
    else:
        (k, m), (k2, n) = ashape, bshape
    assert k == k2, (a.shape, b.shape, mode)
    n_unit = n // N_CHIPS if (out_chip or (b_chip and mode == "nn")) else n
    k_unit = k // N_CHIPS if (b_chip and mode == "nt") else k
    tn = _tile(n_unit, 1536)
    tm = _tile(m, min(512 if (norm_bwd is not None or epilogue is not None) else 1408, ACC_BYTES // (4 * tn)))
    tk = _tile(k_unit, 1536)
    nk = k // tk
    n_per, k_per = n_unit // tn, k_unit // tk
    n_half, k_half = n // 2 // tn, k // 2 // tk
    dims = {"nn": (((1,), (0,)), ((), ())), "nt": (((1,), (1,)), ((), ())), "tn": (((0,), (0,)), ((), ()))}[mode]

    fuse_norm = norm_gain is not None
    fuse_bwd = norm_bwd is not None
    assert not (fuse_norm or fuse_bwd or epilogue is not None) or (tn == n and not out_chip), (name, tn, n)
    assert epilogue is None or not (fuse_norm or fuse_bwd)
    assert not a_halves or mode == "nt"
    assert not b_halves or mode == "tn"
    n_extra_in = (1 if res is not None else 0) + (1 if fuse_norm else 0) + (3 if fuse_bwd else 0)
    n_outs = 3 if fuse_bwd else (2 if fuse_norm else 1)
    if epilogue is not None:
        n_extra_in += len(epilogue.inputs)
        n_outs = (1 if epilogue.keep_product else 0) + len(epilogue.out_shapes)
    grid = (m // tm, n // tn, nk)
    if fuse_bwd and nk > 1:
        assert rider is None and res is None and scale is None
        return _mm_norm_bwd_k_outer(a, b, mode, name, layer, b_chip, a_halves, norm_bwd, (m, n, k), (tm, tk), k_per, k_half)

    def body(*refs):
        a_ref, b_ref = refs[0], refs[1]
        extra = list(refs[2:2 + n_extra_in])
        rest = refs[2 + n_extra_in:]
        if rider is not None:
            rest = rider.split(rest, n_outs, 1 if nk > 1 else 0, grid)
        outs = list(rest)
        acc_ref = outs.pop() if nk > 1 else None
        res_ref = extra.pop(0) if res is not None else None
        gain_ref = extra.pop(0) if fuse_norm else None

        def finish(acc):
            if scale is not None:
                acc = acc * scale
            if res_ref is not None:
                acc = res_ref[...] + acc
            if fuse_bwd:
                h_ref, g_ref, dhin_ref = extra
                dh_ref, dhb_ref, dg_ref = outs
                dx, dg = _rn_bwd_math(h_ref[...], g_ref[...], acc)
                dh = dhin_ref[...] + dx
                dh_ref[...] = dh
                dhb_ref[...] = dh.astype(BF16)

                @pl.when(pl.program_id(0) == 0)
                def _():
                    dg_ref[...] = dg

                @pl.when(pl.program_id(0) > 0)
                def _():
                    dg_ref[...] += dg
                return
            if epilogue is not None:
                if epilogue.keep_product:
                    outs[0][...] = acc.astype(out_dtype)
                epilogue.fn(acc, extra, outs[1:] if epilogue.keep_product else outs)
                return
            outs[0][...] = acc.astype(out_dtype)
            if fuse_norm:
                outs[1][...] = _rn_fwd_math(acc, gain_ref[...]).astype(BF16)

        part = lax.dot_general(a_ref[...].astype(BF16), b_ref[...].astype(BF16), dims,
                               preferred_element_type=F32)
        if nk == 1:
            finish(part)
        else:
            kk = pl.program_id(2)

            @pl.when(kk == 0)
            def _():
                acc_ref[...] = part

            @pl.when(kk > 0)
            def _():
                acc_ref[...] += part

            @pl.when(kk == nk - 1)
            def _():
                finish(acc_ref[...])
        if rider is not None:
            rider.finish_at_last_step(grid)

    lead = () if layer is None else (layer,)
    lead_block = () if layer is None else (None,)
    if mode == "nn":
        a_spec = pl.BlockSpec((tm, tk), lambda i, j, kk: (i, kk))
        if b_chip:
            b_spec = pl.BlockSpec(lead_block + (None, tk, tn), lambda i, j, kk: lead + (j // n_per, kk, j % n_per))
        else:
            b_spec = pl.BlockSpec(lead_block + (tk, tn), lambda i, j, kk: lead + (kk, j))
    elif mode == "nt":
        if a_halves:
            a_spec = pl.BlockSpec((None, tm, tk), lambda i, j, kk: (kk // k_half, i, kk % k_half))
        else:
            a_spec = pl.BlockSpec((tm, tk), lambda i, j, kk: (i, kk))
        if b_chip:
            b_spec = pl.BlockSpec(lead_block + (None, tn, tk), lambda i, j, kk: lead + (kk // k_per, j, kk % k_per))
        else:
            b_spec = pl.BlockSpec(lead_block + (tn, tk), lambda i, j, kk: lead + (j, kk))
    else:
        assert layer is None and not b_chip
        a_spec = pl.BlockSpec((tk, tm), lambda i, j, kk: (kk, i))
        if b_halves:
            b_spec = pl.BlockSpec((None, tk, tn), lambda i, j, kk: (j // n_half, kk, j % n_half))
        else:
            b_spec = pl.BlockSpec((tk, tn), lambda i, j, kk: (kk, j))
    if out_chip:
        o_spec = pl.BlockSpec((None, tm, tn), lambda i, j, kk: (j // n_per, i, j % n_per))
        out_shape = SDS((N_CHIPS, m, n_unit), out_dtype)
    else:
        o_spec = pl.BlockSpec((tm, tn), lambda i, j, kk: (i, j))
        out_shape = SDS((m, n), out_dtype)
    in_specs = [a_spec, b_spec] + ([o_spec] if res is not None else [])
    operands = (a, b) + ((res,) if res is not None else ())
    out_specs = o_spec
    vec = pl.BlockSpec((1, tn), lambda i, j, kk: (0, j))
    if fuse_norm:
        in_specs.append(vec)
        operands += (norm_gain.reshape(1, n),)
        out_shape, out_specs = (out_shape, SDS((m, n), BF16)), (o_spec, o_spec)
    if epilogue is not None:
        in_specs += [pl.BlockSpec(blk, (lambda i, j, kk, f=f: f(i))) for _, blk, f in epilogue.inputs]
        operands += tuple(arr for arr, _, _ in epilogue.inputs)
        ep_specs = tuple(pl.BlockSpec(blk, (lambda i, j, kk, f=f: f(i))) for _, blk, f in epilogue.out_shapes)
        ep_shapes = tuple(sds for sds, _, _ in epilogue.out_shapes)
        out_shape = ((out_shape,) if epilogue.keep_product else ()) + ep_shapes
        out_specs = ((o_spec,) if epilogue.keep_product else ()) + ep_specs
    if fuse_bwd:
        h, gain, dh_in = norm_bwd
        in_specs += [o_spec, vec, o_spec]
        operands += (h, gain.reshape(1, n), dh_in)
        out_shape = (SDS((m, n), F32), SDS((m, n), BF16), SDS((1, n), F32))
        out_specs = (o_spec, o_spec, vec)
    scratch = [pltpu.VMEM((tm, tn), F32)] if nk > 1 else []
    semantics = ("arbitrary",) * 3 if fuse_bwd else ("parallel", "parallel", "arbitrary")
    if rider is None:
        return pl.pallas_call(
            body, out_shape=out_shape, grid=grid, in_specs=in_specs, out_specs=out_specs, scratch_shapes=scratch,
            compiler_params=_cparams(semantics), name=name)(*operands)
    out_shape = out_shape if isinstance(out_shape, tuple) else (out_shape,)
    out_specs = out_specs if isinstance(out_specs, tuple) else (out_specs,)
    outs = pl.pallas_call(
        body, out_shape=out_shape + tuple(rider.out_shapes), grid=grid, in_specs=in_specs + [ANY] * len(rider.operands),
        out_specs=out_specs + (ANY,) * len(rider.out_shapes), scratch_shapes=scratch + rider.sem_shapes,
        input_output_aliases={len(operands) + i: len(out_shape) + o for i, o in rider.aliases.items()},
        compiler_params=_cparams(("arbitrary",) * 3), name=name + rider.tag)(*operands, *rider.operands)
    return tuple(outs[:n_outs]) + (list(outs[n_outs:]),)


class _RowEpilogue:
    def __init__(self, inputs, out_shapes, fn, keep_product):
        self.inputs, self.out_shapes, self.fn, self.keep_product = list(inputs), list(out_shapes), fn, keep_product


def _rows(arr, tm_of):
    return (arr, (tm_of, arr.shape[1]), lambda i: (i, 0))


def _epilogue_rows(t):
    return _tile(t, 512)


def _delta_epilogue(o):
    t = o.shape[0]
    tm = _epilogue_rows(t)

    def fn(acc, ins, outs):
        prod = acc * ins[0][...].astype(F32)
        for h in range(N_HEADS):
            part = jnp.sum(prod[:, h * V_DIM:(h + 1) * V_DIM], axis=1, keepdims=True)
            outs[0][h] = jnp.broadcast_to(part, (tm, LANES))

    return _RowEpilogue([_rows(o, tm)], [(SDS((N_HEADS, t, LANES), F32), (N_HEADS, tm, LANES), lambda i: (0, i, 0))], fn, True)


def _merge_bwd_epilogue(proj, lay, ya, yb):
    t, d = ya.shape
    tm = _epilogue_rows(t)
    assert lay["gc"] % d == 0 and lay["gm"] % d == 0

    def fn(acc, ins, outs):
        gc, gm, ya_, yb_ = (r[...].astype(F32) for r in ins)
        sc, sm = _sigmoid(gc), _sigmoid(gm)
        outs[0][...] = (acc * sc).astype(BF16)
        outs[1][...] = (acc * sm).astype(BF16)
        outs[2][...] = (acc * ya_ * (sc * (1.0 - sc))).astype(BF16)
        outs[3][...] = (acc * yb_ * (sm * (1.0 - sm))).astype(BF16)

    gate = [(proj, (tm, d), lambda i, c=lay[k] // d: (i, c)) for k in ("gc", "gm")]
    out = (SDS((t, d), BF16), (tm, d), lambda i: (i, 0))
    return _RowEpilogue(gate + [_rows(ya, tm), _rows(yb, tm)], [out] * 4, fn, False)


def _ple_fwd_epilogue(h, pp, norm_gain):
    t, d = h.shape
    tm = _epilogue_rows(t)
    fuse_norm = norm_gain is not None

    def fn(acc, ins, outs):
        gp = acc.astype(BF16).astype(F32)
        out = ins[0][...] + _sigmoid(gp) * ins[1][...].astype(F32)
        outs[0][...] = out
        if fuse_norm:
            outs[1][...] = _rn_fwd_math(out, ins[2][...]).astype(BF16)

    inputs = [_rows(h, tm), _rows(pp, tm)]
    outs = [(SDS((t, d), F32), (tm, d), lambda i: (i, 0))]
    if fuse_norm:
        inputs.append((norm_gain.reshape(1, d), (1, d), lambda i: (0, 0)))
        outs.append((SDS((t, d), BF16), (tm, d), lambda i: (i, 0)))
    return _RowEpilogue(inputs, outs, fn, True)


def _mm_norm_bwd_k_outer(a, b, mode, name, layer, b_chip, a_halves, norm_bwd, sizes, tiles, k_per, k_half):
    (m, n, k), (tm, tk) = sizes, tiles
    nk, ni = k // tk, m // tm
    h, gain, dh_in = norm_bwd
    dims = {"nn": (((1,), (0,)), ((), ())), "nt": (((1,), (1,)), ((), ()))}[mode]

    def body(a_ref, b_ref, h_ref, g_ref, dhin_ref, dh_ref, dhb_ref, dg_ref, acc_ref):
        kk, i = pl.program_id(0), pl.program_id(1)
        rows = pl.ds(pl.multiple_of(i * tm, tm), tm)
        part = lax.dot_general(a_ref[...].astype(BF16), b_ref[...].astype(BF16), dims, preferred_element_type=F32)

        @pl.when(kk == 0)
        def _():
            acc_ref[rows, :] = part

        @pl.when((kk > 0) & (kk < nk - 1))
        def _():
            acc_ref[rows, :] += part

        @pl.when(kk == nk - 1)
        def _():
            dx, dg = _rn_bwd_math(h_ref[...], g_ref[...], acc_ref[rows, :] + part)
            dh = dhin_ref[...] + dx
            dh_ref[...] = dh
            dhb_ref[...] = dh.astype(BF16)

            @pl.when(i == 0)
            def _():
                dg_ref[...] = dg

            @pl.when(i > 0)
            def _():
                dg_ref[...] += dg

    lead = () if layer is None else (layer,)
    lead_block = () if layer is None else (None,)
    if a_halves:
        a_spec = pl.BlockSpec((None, tm, tk), lambda kk, i: (kk // k_half, i, kk % k_half))
    else:
        a_spec = pl.BlockSpec((tm, tk), lambda kk, i: (i, kk))
    if mode == "nn":
        assert not b_chip
        b_spec = pl.BlockSpec(lead_block + (tk, n), lambda kk, i: lead + (kk, 0))
    elif b_chip:
        b_spec = pl.BlockSpec(lead_block + (None, n, tk), lambda kk, i: lead + (kk // k_per, 0, kk % k_per))
    else:
        b_spec = pl.BlockSpec(lead_block + (n, tk), lambda kk, i: lead + (0, kk))
    late = pl.BlockSpec((tm, n), lambda kk, i: (jnp.where(kk == nk - 1, i, 0), 0))
    vec = pl.BlockSpec((1, n), lambda kk, i: (0, 0))
    return pl.pallas_call(
        body, out_shape=(SDS((m, n), F32), SDS((m, n), BF16), SDS((1, n), F32)), grid=(nk, ni),
        in_specs=[a_spec, b_spec, late, vec, late], out_specs=(late, late, vec),
        scratch_shapes=[pltpu.VMEM((m, n), F32)], compiler_params=_cparams(("arbitrary", "arbitrary")),
        name=name)(a, b, h, gain.reshape(1, n), dh_in)


def _rn_fwd_math(x, g):
    r = lax.rsqrt(jnp.mean(x * x, axis=-1, keepdims=True) + EPS)
    return (x * r) * g


def _rn_bwd_math(x, g, dn):
    r = lax.rsqrt(jnp.mean(x * x, axis=-1, keepdims=True) + EPS)
    xh = x * r
    gy = dn * g
    dx = r * (gy - xh * jnp.mean(gy * xh, axis=-1, keepdims=True))
    dg = jnp.sum(dn * xh, axis=0, keepdims=True)
    return dx, dg


def _rmsnorm_fwd(h, gain, name):
    t, d = h.shape
    tm = _tile(t, 512, 8)

    def body(h_ref, g_ref, o_ref):
        o_ref[...] = _rn_fwd_math(h_ref[...], g_ref[...]).astype(BF16)

    return pl.pallas_call(
        body, out_shape=SDS((t, d), BF16), grid=(t // tm,),
        in_specs=[_rowspec(tm, d), pl.BlockSpec((1, d), lambda i: (0, 0))], out_specs=_rowspec(tm, d),
        compiler_params=_cparams(("parallel",)), name=name)(h, gain.reshape(1, d))


def _loss_head(h, gain, target):
    t, d = h.shape
    tm = _tile(t, 512, 8)

    def body(h_ref, g_ref, t_ref, loss_ref, dh_ref, dhb_ref, dg_ref):
        x, g = h_ref[...], g_ref[...]
        err = _rn_fwd_math(x, g) - t_ref[...]
        part = 0.5 * jnp.sum(jnp.sum(err * err, axis=1, keepdims=True), axis=0, keepdims=True) * (1.0 / d)
        dx, dg = _rn_bwd_math(x, g, err * (1.0 / d))
        dh_ref[...] = dx
        dhb_ref[...] = dx.astype(BF16)

        @pl.when(pl.program_id(0) == 0)
        def _():
            dg_ref[...] = dg
            loss_ref[...] = jnp.broadcast_to(part, (1, LANES))

        @pl.when(pl.program_id(0) > 0)
        def _():
            dg_ref[...] += dg
            loss_ref[...] += jnp.broadcast_to(part, (1, LANES))

    vec = pl.BlockSpec((1, d), lambda i: (0, 0))
    return pl.pallas_call(
        body, out_shape=(SDS((1, LANES), F32), SDS((t, d), F32), SDS((t, d), BF16), SDS((1, d), F32)),
        grid=(t // tm,), in_specs=[_rowspec(tm, d), vec, _rowspec(tm, d)],
        out_specs=(pl.BlockSpec((1, LANES), lambda i: (0, 0)), _rowspec(tm, d), _rowspec(tm, d), vec),
        compiler_params=_cparams(("arbitrary",)), name="loss_head")(h, gain.reshape(1, d), target)


def _gu_swiglu_fwd(n, w_gu, layer, name, rider=None):
    t, d = n.shape
    cols = w_gu.shape[3]
    f = 2 * cols
    tn = _tile(cols, 1536)
    tm = _tile(t, 512)
    per = cols // tn

    grid = (f // tn, t // tm)

    def body(n_ref, wg_ref, wu_ref, *rest):
        gu_ref, a_ref = rest[:2] if rider is None else rider.split(rest, 2, 0, grid)
        x = n_ref[...]
        g = jnp.dot(x, wg_ref[...], preferred_element_type=F32)
        u = jnp.dot(x, wu_ref[...], preferred_element_type=F32)
        gu_ref[0] = g.astype(BF16)
        gu_ref[1] = u.astype(BF16)
        a_ref[...] = (g * _sigmoid(g) * u).astype(BF16)
        if rider is not None:
            rider.finish_at_last_step(grid)

    in_specs = [pl.BlockSpec((tm, d), lambda j, i: (i, 0)),
                pl.BlockSpec((None, None, d, tn), lambda j, i: (layer, j // per, 0, j % per)),
                pl.BlockSpec((None, None, d, tn), lambda j, i: (layer, 2 + j // per, 0, j % per))]
    out_shape = (SDS((2, t, f), BF16), SDS((t, f), BF16))
    out_specs = (pl.BlockSpec((2, tm, tn), lambda j, i: (0, i, j)), pl.BlockSpec((tm, tn), lambda j, i: (i, j)))
    if rider is None:
        return pl.pallas_call(body, out_shape=out_shape, grid=grid, in_specs=in_specs, out_specs=out_specs,
                              compiler_params=_cparams(("parallel", "parallel")), name=name)(n, w_gu, w_gu)
    outs = pl.pallas_call(
        body, out_shape=out_shape + tuple(rider.out_shapes), grid=grid, in_specs=in_specs + [ANY] * len(rider.operands),
        out_specs=out_specs + (ANY,) * len(rider.out_shapes), scratch_shapes=rider.sem_shapes,
        input_output_aliases={3 + i: 2 + o for i, o in rider.aliases.items()},
        compiler_params=_cparams(("arbitrary", "arbitrary")), name=name + rider.tag)(n, w_gu, w_gu, *rider.operands)
    return outs[0], outs[1], list(outs[2:])


def _down_dx_swiglu_bwd(dhb, w_down, gu, layer, name):
    t, d = dhb.shape
    f = gu.shape[2]
    tn = _tile(f, 1536)
    tm = _tile(t, 512)

    def body(dh_ref, w_ref, gu_ref, dgu_ref):
        da = 0.5 * lax.dot_general(dh_ref[...], w_ref[...], _NT, preferred_element_type=F32)
        g = gu_ref[0].astype(F32)
        u = gu_ref[1].astype(F32)
        sg = _sigmoid(g)
        dgu_ref[0] = (da * u * (sg * (1.0 + g * (1.0 - sg)))).astype(BF16)
        dgu_ref[1] = (da * (g * sg)).astype(BF16)

    blk = pl.BlockSpec((2, tm, tn), lambda j, i: (0, i, j))
    return pl.pallas_call(
        body, out_shape=SDS((2, t, f), BF16), grid=(f // tn, t // tm),
        in_specs=[pl.BlockSpec((tm, d), lambda j, i: (i, 0)), pl.BlockSpec((None, tn, d), lambda j, i: (layer, j, 0)), blk],
        out_specs=blk, compiler_params=_cparams(("parallel", "parallel")), name=name)(dhb, w_down, gu)


def _shift_down(z, k, row):
    return jnp.where(row >= k, pltpu.roll(z, k, 0), 0.0)


def _shift_up(z, k, row, t):
    return jnp.where(row < t - k, pltpu.roll(z, t - k, 0), 0.0)


def _conv_specs(t, conv):
    nb = conv // LANES
    return [pl.BlockSpec((t, LANES), lambda j: (0, j)), pl.BlockSpec((t, LANES), lambda j: (0, nb + j)),
            pl.BlockSpec((t, LANES), lambda j: (0, 2 * nb + j))]


def _conv_fwd(proj, conv_w):
    t = proj.shape[0]
    conv = conv_w.shape[1]

    def body(b_ref, c_ref, v_ref, w_ref, o_ref):
        z = c_ref[...].astype(F32) * v_ref[...].astype(F32)
        row = lax.broadcasted_iota(jnp.int32, z.shape, 0)
        y = w_ref[0:1, :] * _shift_down(z, 2, row) + w_ref[1:2, :] * _shift_down(z, 1, row) + w_ref[2:3, :] * z
        o_ref[...] = (b_ref[...].astype(F32) * y).astype(BF16)

    cspec = pl.BlockSpec((t, LANES), lambda j: (0, j))
    return pl.pallas_call(
        body, out_shape=SDS((t, conv), BF16), grid=(conv // LANES,),
        in_specs=_conv_specs(t, conv) + [pl.BlockSpec((3, LANES), lambda j: (0, j))], out_specs=cspec,
        compiler_params=_cparams(("parallel",)), name="conv_fwd")(proj, proj, proj, conv_w)


def _conv_bwd(proj, conv_w, dcb):
    t = proj.shape[0]
    conv = conv_w.shape[1]

    def body(b_ref, c_ref, v_ref, w_ref, d_ref, db_ref, dc_ref, dv_ref, dw_ref):
        b, c, v = b_ref[...].astype(F32), c_ref[...].astype(F32), v_ref[...].astype(F32)
        d = d_ref[...].astype(F32)
        z = c * v
        row = lax.broadcasted_iota(jnp.int32, z.shape, 0)
        z1, z2 = _shift_down(z, 1, row), _shift_down(z, 2, row)
        w0, w1, w2 = w_ref[0:1, :], w_ref[1:2, :], w_ref[2:3, :]
        y = w0 * z2 + w1 * z1 + w2 * z
        dy = d * b
        db_ref[...] = (d * y).astype(BF16)
        dz = w2 * dy + w1 * _shift_up(dy, 1, row, t) + w0 * _shift_up(dy, 2, row, t)
        dc_ref[...] = (dz * v).astype(BF16)
        dv_ref[...] = (dz * c).astype(BF16)
        dw_ref[0:1, :] = jnp.sum(dy * z2, axis=0, keepdims=True)
        dw_ref[1:2, :] = jnp.sum(dy * z1, axis=0, keepdims=True)
        dw_ref[2:3, :] = jnp.sum(dy * z, axis=0, keepdims=True)

    cspec = pl.BlockSpec((t, LANES), lambda j: (0, j))
    wspec = pl.BlockSpec((3, LANES), lambda j: (0, j))
    return pl.pallas_call(
        body, out_shape=(SDS((t, conv), BF16),) * 3 + (SDS((3, conv), F32),), grid=(conv // LANES,),
        in_specs=_conv_specs(t, conv) + [wspec, cspec], out_specs=(cspec, cspec, cspec, wspec),
        compiler_params=_cparams(("parallel",)), name="conv_bwd")(proj, proj, proj, conv_w, dcb)


def _qkvnorm_fwd(proj, lay, q_gain, kv_gain):
    t = proj.shape[0]
    ql, kvl = lay["ql"], lay["kvl"]
    tm = _tile(t, 512, 8)

    def body(q_ref, kv_ref, gq_ref, gkv_ref, qn_ref, kvn_ref):
        qn_ref[...] = _rn_fwd_math(q_ref[...].astype(F32), gq_ref[...]).astype(BF16)
        kvn_ref[...] = _rn_fwd_math(kv_ref[...].astype(F32), gkv_ref[...]).astype(BF16)

    return pl.pallas_call(
        body, out_shape=(SDS((t, ql), BF16), SDS((t, kvl), BF16)), grid=(t // tm,),
        in_specs=[_colspec(tm, ql, lay["q"]), _colspec(tm, kvl, lay["kv"]),
                  pl.BlockSpec((1, ql), lambda i: (0, 0)), pl.BlockSpec((1, kvl), lambda i: (0, 0))],
        out_specs=(_rowspec(tm, ql), _rowspec(tm, kvl)), compiler_params=_cparams(("parallel",)),
        name="qkvnorm_fwd")(proj, proj, q_gain.reshape(1, ql), kv_gain.reshape(1, kvl))


def _qkvnorm_bwd(proj, lay, q_gain, kv_gain, dqn, dkvn):
    t = proj.shape[0]
    ql, kvl = lay["ql"], lay["kvl"]
    tm = _tile(t, 512, 8)

    def body(q_ref, kv_ref, gq_ref, gkv_ref, dqn_ref, dkvn_ref, dq_ref, dkv_ref, dgq_ref, dgkv_ref):
        dq, dgq = _rn_bwd_math(q_ref[...].astype(F32), gq_ref[...], dqn_ref[...].astype(F32))
        dkv, dgkv = _rn_bwd_math(kv_ref[...].astype(F32), gkv_ref[...], dkvn_ref[...].astype(F32))
        dq_ref[...] = dq.astype(BF16)
        dkv_ref[...] = dkv.astype(BF16)

        @pl.when(pl.program_id(0) == 0)
        def _():
            dgq_ref[...] = dgq
            dgkv_ref[...] = dgkv

        @pl.when(pl.program_id(0) > 0)
        def _():
            dgq_ref[...] += dgq
            dgkv_ref[...] += dgkv

    vq = pl.BlockSpec((1, ql), lambda i: (0, 0))
    vkv = pl.BlockSpec((1, kvl), lambda i: (0, 0))
    return pl.pallas_call(
        body, out_shape=(SDS((t, ql), BF16), SDS((t, kvl), BF16), SDS((1, ql), F32), SDS((1, kvl), F32)),
        grid=(t // tm,),
        in_specs=[_colspec(tm, ql, lay["q"]), _colspec(tm, kvl, lay["kv"]), vq, vkv, _rowspec(tm, ql),
                  _rowspec(tm, kvl)],
        out_specs=(_rowspec(tm, ql), _rowspec(tm, kvl), vq, vkv), compiler_params=_cparams(("arbitrary",)),
        name="qkvnorm_bwd")(proj, proj, q_gain.reshape(1, ql), kv_gain.reshape(1, kvl), dqn, dkvn)


def _rope(x, cos_t, sin_a, sin_b):
    return x * cos_t + pltpu.roll(x, LANES - ROPE_DIM // 2, 1) * sin_a + pltpu.roll(x, ROPE_DIM // 2, 1) * sin_b


def _rope_fwd(qf, kv, proj, lay, tables):
    t = qf.shape[0]
    tm = _tile(t, 256, 8)
    hq = N_HEADS * QK_PAD

    def body(q_ref, kn_ref, kr_ref, cos_ref, sa_ref, sb_ref, qr_ref, kf_ref):
        cos_t, sin_a, sin_b = cos_ref[...], sa_ref[...], sb_ref[...]
        kr = _rope(kr_ref[...].astype(F32), cos_t, sin_a, sin_b).astype(BF16)
        for h in range(N_HEADS):
            lo = h * QK_PAD
            qr_ref[:, lo:lo + NOPE_DIM] = q_ref[:, lo:lo + NOPE_DIM]
            qr_ref[:, lo + NOPE_DIM:lo + QK_PAD] = _rope(
                q_ref[:, lo + NOPE_DIM:lo + QK_PAD].astype(F32), cos_t, sin_a, sin_b).astype(BF16)
            kf_ref[:, lo:lo + NOPE_DIM] = kn_ref[:, h * NOPE_DIM:(h + 1) * NOPE_DIM]
            kf_ref[:, lo + NOPE_DIM:lo + QK_PAD] = kr

    tab = _rowspec(tm, LANES)
    return pl.pallas_call(
        body, out_shape=(SDS((t, hq), BF16), SDS((t, hq), BF16)), grid=(t // tm,),
        in_specs=[_rowspec(tm, hq), _rowspec(tm, N_HEADS * NOPE_DIM), _colspec(tm, LANES, lay["kr"]), tab, tab, tab],
        out_specs=(_rowspec(tm, hq), _rowspec(tm, hq)), compiler_params=_cparams(("parallel",)),
        name="rope_fwd")(qf, kv, proj, *tables)


def _rope_bwd(dqr, dkf, dv, tables):
    t = dqr.shape[0]
    tm = _tile(t, 256, 8)
    hq = N_HEADS * QK_PAD
    hn = N_HEADS * NOPE_DIM

    def body(dq_ref, dk_ref, dv_ref, cos_ref, sa_ref, sb_ref, dqf_ref, dkv_ref, dkr_ref):
        cos_t, sin_a, sin_b = cos_ref[...], -sa_ref[...], -sb_ref[...]
        dkr = jnp.zeros((tm, LANES), F32)
        for h in range(N_HEADS):
            lo = h * QK_PAD
            dqf_ref[:, lo:lo + NOPE_DIM] = dq_ref[:, lo:lo + NOPE_DIM].astype(BF16)
            dqf_ref[:, lo + NOPE_DIM:lo + QK_PAD] = _rope(
                dq_ref[:, lo + NOPE_DIM:lo + QK_PAD].astype(F32), cos_t, sin_a, sin_b).astype(BF16)
            dkv_ref[:, h * NOPE_DIM:(h + 1) * NOPE_DIM] = dk_ref[:, lo:lo + NOPE_DIM]
            dkr = dkr + dk_ref[:, lo + NOPE_DIM:lo + QK_PAD].astype(F32)
        dkv_ref[:, hn:] = dv_ref[...]
        dkr_ref[...] = _rope(dkr, cos_t, sin_a, sin_b).astype(BF16)

    tab = _rowspec(tm, LANES)
    return pl.pallas_call(
        body, out_shape=(SDS((t, hq), BF16), SDS((t, 2 * hn), BF16), SDS((t, LANES), BF16)), grid=(t // tm,),
        in_specs=[_rowspec(tm, hq), _rowspec(tm, hq), _rowspec(tm, hn), tab, tab, tab],
        out_specs=(_rowspec(tm, hq), _rowspec(tm, 2 * hn), tab), compiler_params=_cparams(("parallel",)),
        name="rope_bwd")(dqr, dkf, dv, *tables)


def _chunk_mask(bq):
    qc = lax.broadcasted_iota(jnp.int32, (bq, bq), 0) // CHUNK
    kc = lax.broadcasted_iota(jnp.int32, (bq, bq), 1) // CHUNK
    return kc <= qc


_NT = (((1,), (1,)), ((), ()))
_TN = (((0,), (0,)), ((), ()))
LOG2E = 1.4426950408889634
EXP2_SCALE = ATTN_SCALE * LOG2E


def _attn_block(t):
    return 512 if t >= 2048 else 128


def _two_slot_pipeline(unmasked, issue, consume, carry):
    issue(0, 0)

    def pair(n, c):
        issue(2 * n + 1, 1)
        c = consume(2 * n, 0, c, False)
        issue(2 * n + 2, 0)
        return consume(2 * n + 1, 1, c, False)

    carry = lax.fori_loop(0, unmasked // 2, pair, carry)

    def even(c):
        return consume(unmasked, 0, c, True)

    def odd(c):
        issue(unmasked, 1)
        c = consume(unmasked - 1, 0, c, False)
        return consume(unmasked, 1, c, True)

    return lax.cond(unmasked % 2 == 0, even, odd, carry)


def _attn_fwd(qr, kf, kv, gather=None):
    t = qr.shape[0]
    bq = _attn_block(t)
    nq = t // bq
    bufs, pieces = gather if gather is not None else ((), ())
    nw = len(bufs)

    def body(*refs):
        q_ref, k_ref, v_ref = refs[:3]
        o_ref, lse_ref = refs[3 + nw:5 + nw]
        buf_refs = refs[5 + nw:5 + 2 * nw]
        vaug_ref, s_ref = refs[5 + 2 * nw], refs[6 + 2 * nw]
        sems = refs[7 + 2 * nw:]
        h, i = pl.program_id(0), pl.program_id(1)

        if nw:
            @pl.when((h == 0) & (i == 0))
            def _():
                _gather_start(buf_refs, pieces, *sems)

        @pl.when(i == 0)
        def _():
            vaug_ref[:, :V_DIM] = v_ref[...]
            vaug_ref[:, V_DIM:] = jnp.ones((t, V_DIM), BF16)

        def issue(j, slot):
            off = pl.multiple_of(j * bq, bq)
            s_ref[slot] = lax.dot_general(q_ref[...], k_ref[pl.ds(off, bq), :], _NT, preferred_element_type=F32)

        def consume(j, slot, carry, masked):
            m, acc = carry
            off = pl.multiple_of(j * bq, bq)
            s = s_ref[slot]
            if masked:
                s = jnp.where(_chunk_mask(bq), s, NEG_BIG)
            m_new = jnp.maximum(m, jnp.max(s, axis=1, keepdims=True))
            alpha = jnp.exp2((m - m_new) * EXP2_SCALE)
            pr = jnp.exp2(((s - m_new) * EXP2_SCALE).astype(BF16))
            acc = alpha * acc + jnp.dot(pr, vaug_ref[pl.ds(off, bq), :], preferred_element_type=F32)
            return m_new, acc

        init = (jnp.full((bq, 1), NEG_BIG, F32), jnp.zeros((bq, 2 * V_DIM), F32))
        m, acc = _two_slot_pipeline(i, issue, consume, init)
        l = acc[:, V_DIM:V_DIM + 1]
        o_ref[...] = (acc[:, :V_DIM] / l).astype(BF16)
        lse_ref[0] = jnp.broadcast_to(m * ATTN_SCALE + jnp.log(l), (bq, LANES))

        if nw:
            @pl.when((h == N_HEADS - 1) & (i == nq - 1))
            def _():
                _gather_finish(buf_refs, pieces, *sems)

    out_shape = (SDS((t, N_HEADS * V_DIM), BF16), SDS((N_HEADS, t, LANES), F32)) + tuple(SDS(b.shape, b.dtype) for b in bufs)
    sem_shapes = _gather_sems(pieces) if nw else []
    outs = pl.pallas_call(
        body, out_shape=out_shape, grid=(N_HEADS, nq),
        in_specs=[pl.BlockSpec((bq, QK_PAD), lambda h, i: (i, h)), pl.BlockSpec((t, QK_PAD), lambda h, i: (0, h)),
                  pl.BlockSpec((t, V_DIM), lambda h, i: (0, N_HEADS + h))] + [ANY] * nw,
        out_specs=(pl.BlockSpec((bq, V_DIM), lambda h, i: (i, h)),
                   pl.BlockSpec((1, bq, LANES), lambda h, i: (h, i, 0))) + (ANY,) * nw,
        input_output_aliases={3 + n: 2 + n for n in range(nw)},
        scratch_shapes=[pltpu.VMEM((t, 2 * V_DIM), BF16), pltpu.VMEM((2, bq, bq), F32)] + sem_shapes,
        compiler_params=_cparams(("arbitrary", "arbitrary")), name="attn_fwd_gather" if nw else "attn_fwd")(qr, kf, kv, *bufs)
    return outs[0], outs[1], list(outs[2:])


def _attn_bwd(qr, kf, kv, do, lse, delta, exchange=None):
    t = qr.shape[0]
    bq = _attn_block(t)
    nq = t // bq
    pbs, lands = exchange if exchange is not None else ((), ())
    nw = len(pbs)

    def body(*refs):
        k_ref, v_ref, q_ref, do_ref, lse_ref, dl_ref = refs[:6]
        pb_refs = refs[6:6 + nw]
        dk_ref, dv_ref, dq_ref = refs[6 + 2 * nw:9 + 2 * nw]
        land_refs = refs[9 + 2 * nw:9 + 3 * nw]
        s_ref, dp_ref = refs[9 + 3 * nw], refs[10 + 3 * nw]
        sems = refs[11 + 3 * nw:]
        h, j = pl.program_id(0), pl.program_id(1)

        if nw:
            @pl.when((h == 0) & (j == 0))
            def _():
                _exchange_start(pb_refs, land_refs, *sems)

        @pl.when(j == 0)
        def _():
            dq_ref[...] = jnp.zeros_like(dq_ref)

        k = k_ref[...]
        v = v_ref[...]

        def issue(b, slot):
            off = pl.multiple_of((nq - 1 - b) * bq, bq)
            s_ref[slot] = lax.dot_general(q_ref[pl.ds(off, bq), :], k, _NT, preferred_element_type=F32)
            dp_ref[slot] = lax.dot_general(do_ref[pl.ds(off, bq), :], v, _NT, preferred_element_type=F32)

        def consume(b, slot, carry, masked):
            dk, dv = carry
            off = pl.multiple_of((nq - 1 - b) * bq, bq)
            q = q_ref[pl.ds(off, bq), :]
            do_ = do_ref[pl.ds(off, bq), :]
            lse2 = lse_ref[0, pl.ds(off, bq), :][:, :1] * LOG2E
            dl_i = dl_ref[0, pl.ds(off, bq), :][:, :1]
            s = s_ref[slot]
            if masked:
                s = jnp.where(_chunk_mask(bq), s, NEG_BIG)
            pr = jnp.exp2((s * EXP2_SCALE - lse2).astype(BF16))
            dv = dv + lax.dot_general(pr, do_, _TN, preferred_element_type=F32)
            ds = pr * (dp_ref[slot] - dl_i).astype(BF16)
            dk = dk + lax.dot_general(ds, q, _TN, preferred_element_type=F32)
            dq_ref[pl.ds(off, bq), :] += jnp.dot(ds, k, preferred_element_type=F32) * ATTN_SCALE
            return dk, dv

        init = (jnp.zeros((bq, QK_PAD), F32), jnp.zeros((bq, V_DIM), F32))
        dk, dv = _two_slot_pipeline(nq - 1 - j, issue, consume, init)
        dk_ref[...] = (dk * ATTN_SCALE).astype(BF16)
        dv_ref[...] = dv.astype(BF16)

        if nw:
            @pl.when((h == N_HEADS - 1) & (j == nq - 1))
            def _():
                _exchange_finish(pb_refs, land_refs, *sems)

    stat = pl.BlockSpec((1, t, LANES), lambda h, j: (h, 0, 0))
    out_shape = (SDS((t, N_HEADS * QK_PAD), BF16), SDS((t, N_HEADS * V_DIM), BF16), SDS((t, N_HEADS * QK_PAD), F32))
    sem_shapes = [pltpu.SemaphoreType.DMA((3 * nw,)), pltpu.SemaphoreType.DMA((3 * nw,))] if nw else []
    outs = pl.pallas_call(
        body, out_shape=out_shape + tuple(SDS(l.shape, l.dtype) for l in lands), grid=(N_HEADS, nq),
        in_specs=[pl.BlockSpec((bq, QK_PAD), lambda h, j: (j, h)), pl.BlockSpec((bq, V_DIM), lambda h, j: (j, N_HEADS + h)),
                  pl.BlockSpec((t, QK_PAD), lambda h, j: (0, h)), pl.BlockSpec((t, V_DIM), lambda h, j: (0, h)), stat, stat]
        + [ANY] * (2 * nw),
        out_specs=(pl.BlockSpec((bq, QK_PAD), lambda h, j: (j, h)), pl.BlockSpec((bq, V_DIM), lambda h, j: (j, h)),
                   pl.BlockSpec((t, QK_PAD), lambda h, j: (0, h))) + (ANY,) * nw,
        input_output_aliases={6 + nw + n: 3 + n for n in range(nw)},
        scratch_shapes=[pltpu.VMEM((2, bq, bq), F32), pltpu.VMEM((2, bq, bq), F32)] + sem_shapes,
        compiler_params=_cparams(("arbitrary", "arbitrary")),
        name="attn_bwd_exchange" if nw else "attn_bwd")(kf, kv, qr, do, lse, delta, *pbs, *lands)
    return outs[0], outs[1], outs[2], list(outs[3:])


def _merge_fwd(proj, lay, ya, yb):
    t, d = ya.shape
    tm = _tile(t, 512, 8)

    def body(gc_ref, gm_ref, ya_ref, yb_ref, o_ref):
        o_ref[...] = (_sigmoid(gc_ref[...].astype(F32)) * ya_ref[...].astype(F32)
                      + _sigmoid(gm_ref[...].astype(F32)) * yb_ref[...].astype(F32)).astype(BF16)

    return pl.pallas_call(
        body, out_shape=SDS((t, d), BF16), grid=(t // tm,),
        in_specs=[_colspec(tm, d, lay["gc"]), _colspec(tm, d, lay["gm"]), _rowspec(tm, d), _rowspec(tm, d)],
        out_specs=_rowspec(tm, d), compiler_params=_cparams(("parallel",)), name="merge_fwd")(proj, proj, ya, yb)


def _ple_bwd(dh, gp, pp):
    t, d = dh.shape
    tm = _tile(t, 512, 8)

    def body(dh_ref, gp_ref, pp_ref, dpp_ref, dgp_ref):
        g = dh_ref[...]
        s = _sigmoid(gp_ref[...].astype(F32))
        dpp_ref[...] = (g * s).astype(BF16)
        dgp_ref[...] = (g * pp_ref[...].astype(F32) * (s * (1.0 - s))).astype(BF16)

    r = _rowspec(tm, d)
    return pl.pallas_call(body, out_shape=(SDS((t, d), BF16),) * 2, grid=(t // tm,), in_specs=[r, r, r],
                          out_specs=(r, r), compiler_params=_cparams(("parallel",)), name="ple_bwd")(dh, gp, pp)


def _adamw(w, g, m, v, emit_grad=False):
    shape = w.shape
    cols = shape[-1]
    rows = w.size // cols
    tr = _tile(rows, max(8, BLOCK_ELEMS // cols // 8 * 8), 8)
    n_out = 4 if emit_grad else 3

    def body(w_ref, g_ref, m_ref, v_ref, d_ref, nm_ref, nv_ref, *g_out):
        g_ = g_ref[...]
        nm = ADAM_B1 * m_ref[...] + (1.0 - ADAM_B1) * g_
        nv = ADAM_B2 * v_ref[...] + (1.0 - ADAM_B2) * (g_ * g_)
        m_hat = nm / (1.0 - ADAM_B1 ** ADAM_STEP)
        v_hat = nv / (1.0 - ADAM_B2 ** ADAM_STEP)
        d_ref[...] = -ADAM_LR * (m_hat / (jnp.sqrt(v_hat) + ADAM_EPS) + ADAM_WD * w_ref[...])
        nm_ref[...] = nm
        nv_ref[...] = nv
        if emit_grad:
            g_out[0][...] = g_

    r = _rowspec(tr, cols)
    outs = pl.pallas_call(
        body, out_shape=(SDS((rows, cols), F32),) * n_out, grid=(rows // tr,), in_specs=[r, r, r, r],
        out_specs=(r,) * n_out, compiler_params=_cparams(("parallel",)),
        name="adamw")(*(a.reshape(rows, cols) for a in (w, g, m, v)))
    return tuple(o.reshape(shape) for o in outs)


ANY = pl.BlockSpec(memory_space=pl.ANY)


def _place():
    x, y, c = lax.axis_index("x"), lax.axis_index("y"), lax.axis_index("c")
    return x, y, c, [(1 - x, y), (x, 1 - y), (1 - x, 1 - y)]


def _half_rows(rows, cols):
    half = rows // 2
    return half, _tile(half, max(16, BLOCK_ELEMS // cols // 16 * 16), 16)


def _my_chip():
    return 2 * lax.axis_index("x") + lax.axis_index("y")


def _cast_into_slot(ws):
    k = len(ws)
    nl, r, c = ws[0].shape
    tr = _tile(r, max(16, BLOCK_ELEMS // c // 16 * 16), 16)

    def body(*refs):
        for n in range(k):
            refs[k + n][...] = refs[n][...].astype(BF16)

    return list(pl.pallas_call(
        body, out_shape=(SDS((nl, N_CHIPS, r, c), BF16),) * k, grid=(nl, r // tr),
        in_specs=[pl.BlockSpec((None, tr, c), lambda l, i: (l, i, 0))] * k,
        out_specs=(pl.BlockSpec((None, None, tr, c), lambda l, i: (l, _my_chip(), i, 0)),) * k,
        compiler_params=_cparams(("parallel", "parallel")), name="cast_into_slot")(*ws))


def _gather_copy(ref, layer, send_sems, recv_sems, sem, chip, half, to):
    r2 = ref.shape[2] // 2
    rows = ref.at[layer, chip, pl.ds(half * r2, r2)]
    return pltpu.make_async_remote_copy(src_ref=rows, dst_ref=rows, send_sem=send_sems.at[sem], recv_sem=recv_sems.at[sem],
                                        device_id=to, device_id_type=MESH)


def _gather_start(refs, pieces, send_sems, recv_sems):
    x, y, c, chips = _place()
    for pi, (ri, layer) in enumerate(pieces):
        for n, chip in enumerate(chips):
            _gather_copy(refs[ri], layer, send_sems, recv_sems, 6 * pi + n, 2 * x + y, c, (*chip, c)).start()


def _gather_finish(refs, pieces, send_sems, recv_sems):
    x, y, c, chips = _place()
    me, sibling = (x, y, c), (x, y, 1 - c)
    for pi, (ri, layer) in enumerate(pieces):
        for n, chip in enumerate(chips):
            k = 2 * chip[0] + chip[1]
            _gather_copy(refs[ri], layer, send_sems, recv_sems, 6 * pi + n, k, c, me).wait_recv()
            _gather_copy(refs[ri], layer, send_sems, recv_sems, 6 * pi + 3 + n, k, c, sibling).start()
    for pi, (ri, layer) in enumerate(pieces):
        for n, chip in enumerate(chips):
            _gather_copy(refs[ri], layer, send_sems, recv_sems, 6 * pi + 3 + n, 2 * chip[0] + chip[1], 1 - c, me).wait_recv()
    for pi, (ri, layer) in enumerate(pieces):
        for n, chip in enumerate(chips):
            _gather_copy(refs[ri], layer, send_sems, recv_sems, 6 * pi + n, 2 * x + y, c, (*chip, c)).wait_send()
            _gather_copy(refs[ri], layer, send_sems, recv_sems, 6 * pi + 3 + n, 2 * chip[0] + chip[1], c, sibling).wait_send()


def _gather_sems(pieces):
    return [pltpu.SemaphoreType.DMA((6 * len(pieces),)), pltpu.SemaphoreType.DMA((6 * len(pieces),))]


def _all_gather_weights(bufs, pieces):
    nw = len(bufs)

    def body(*refs):
        _gather_start(refs[nw:2 * nw], pieces, refs[2 * nw], refs[2 * nw + 1])
        _gather_finish(refs[nw:2 * nw], pieces, refs[2 * nw], refs[2 * nw + 1])

    return list(pl.pallas_call(
        body, out_shape=tuple(SDS(b.shape, b.dtype) for b in bufs), in_specs=[ANY] * nw, out_specs=(ANY,) * nw,
        input_output_aliases={i: i for i in range(nw)}, scratch_shapes=_gather_sems(pieces),
        name="all_gather_weights")(*bufs))


def _pair_swap(grads):
    nw = len(grads)

    def body(*refs):
        ins, outs = refs[:nw], refs[nw:2 * nw]
        send_sems, recv_sems = refs[2 * nw], refs[2 * nw + 1]
        x, y, c, _ = _place()
        copies = []
        for wi, (g_ref, o_ref) in enumerate(zip(ins, outs)):
            r2 = g_ref.shape[1] // 2
            copies.append(pltpu.make_async_remote_copy(
                src_ref=g_ref.at[:, pl.ds((1 - c) * r2, r2)], dst_ref=o_ref, send_sem=send_sems.at[wi],
                recv_sem=recv_sems.at[wi], device_id=(x, y, 1 - c), device_id_type=MESH))
            copies[-1].start()
        for cp in copies:
            cp.wait()

    return pl.pallas_call(
        body, out_shape=tuple(SDS((N_CHIPS, g.shape[1] // 2, g.shape[2]), g.dtype) for g in grads),
        in_specs=[ANY] * nw, out_specs=(ANY,) * nw,
        scratch_shapes=[pltpu.SemaphoreType.DMA((nw,)), pltpu.SemaphoreType.DMA((nw,))], name="pair_swap")(*grads)


def _same_shape_groups(arrays):
    groups = {}
    for i, arr in enumerate(arrays):
        groups.setdefault(tuple(arr.shape), []).append(i)
    return list(groups.values())


def _pair_add(gs, others):
    k = len(gs)
    _, r, c = gs[0].shape
    r2, tr = _half_rows(r, c)
    nb = r2 // tr

    def body(*refs):
        for n in range(k):
            g_ref, o_ref, pb_ref, land_ref = refs[n], refs[k + n], refs[2 * k + n], refs[3 * k + n]
            total = (g_ref[...].astype(F32) + o_ref[...].astype(F32)).astype(BF16)
            pb_ref[...] = total

            @pl.when(pl.program_id(1) == _my_chip())
            def _():
                land_ref[...] = total

    blk = pl.BlockSpec((None, tr, c), lambda j, n: (n, j, 0))
    mine = pl.BlockSpec((None, tr, c), lambda j, n: (n, lax.axis_index("c") * nb + j, 0))
    land = pl.BlockSpec((None, tr, c), lambda j, n: (_my_chip(), j, 0))
    outs = pl.pallas_call(
        body, out_shape=(SDS((N_CHIPS, r2, c), BF16),) * (2 * k), grid=(nb, N_CHIPS),
        in_specs=[mine] * k + [blk] * k, out_specs=(blk,) * k + (land,) * k,
        compiler_params=_cparams(("parallel", "arbitrary")), name="pair_add")(*gs, *others)
    return [(outs[n], outs[k + n]) for n in range(k)]


def _pair_add_all(gs, others):
    pairs = [None] * len(gs)
    for group in _same_shape_groups(gs):
        for i, pair in zip(group, _pair_add([gs[i] for i in group], [others[i] for i in group])):
            pairs[i] = pair
    return pairs


def _exchange_copy(p_ref, l_ref, send_sems, recv_sems, sem, src_slot, dst_slot, to):
    return pltpu.make_async_remote_copy(src_ref=p_ref.at[src_slot], dst_ref=l_ref.at[dst_slot], send_sem=send_sems.at[sem],
                                        recv_sem=recv_sems.at[sem], device_id=to, device_id_type=MESH)


def _exchange_start(p_refs, l_refs, send_sems, recv_sems):
    x, y, c, chips = _place()
    for wi, (p_ref, l_ref) in enumerate(zip(p_refs, l_refs)):
        for n, chip in enumerate(chips):
            _exchange_copy(p_ref, l_ref, send_sems, recv_sems, 3 * wi + n, 2 * chip[0] + chip[1], 2 * x + y, (*chip, c)).start()


def _exchange_finish(p_refs, l_refs, send_sems, recv_sems):
    x, y, c, chips = _place()
    for wi, (p_ref, l_ref) in enumerate(zip(p_refs, l_refs)):
        for n, chip in enumerate(chips):
            _exchange_copy(p_ref, l_ref, send_sems, recv_sems, 3 * wi + n, 2 * x + y, 2 * chip[0] + chip[1], (x, y, c)).wait_recv()
    for wi, (p_ref, l_ref) in enumerate(zip(p_refs, l_refs)):
        for n, chip in enumerate(chips):
            _exchange_copy(p_ref, l_ref, send_sems, recv_sems, 3 * wi + n, 2 * chip[0] + chip[1], 2 * x + y, (*chip, c)).wait_send()


def _chip_all_to_all(pbs, lands):
    nw = len(pbs)

    def body(*refs):
        _exchange_start(refs[:nw], refs[2 * nw:3 * nw], refs[3 * nw], refs[3 * nw + 1])
        _exchange_finish(refs[:nw], refs[2 * nw:3 * nw], refs[3 * nw], refs[3 * nw + 1])

    return list(pl.pallas_call(
        body, out_shape=tuple(SDS(l.shape, l.dtype) for l in lands), in_specs=[ANY] * (2 * nw), out_specs=(ANY,) * nw,
        input_output_aliases={nw + i: i for i in range(nw)},
        scratch_shapes=[pltpu.SemaphoreType.DMA((3 * nw,)), pltpu.SemaphoreType.DMA((3 * nw,))],
        name="chip_all_to_all")(*pbs, *lands))


def _sum_chips(lands, gstacks, layer):
    k = len(lands)
    _, r, c = gstacks[0].shape
    r2, tr = _half_rows(r, c)
    nb = r2 // tr

    def body(*refs):
        for n in range(k):
            l_ref, out_ref = refs[n], refs[2 * k + n]
            out_ref[...] = ((l_ref[0].astype(F32) + l_ref[1].astype(F32)) + l_ref[2].astype(F32)) + l_ref[3].astype(F32)

    return list(pl.pallas_call(
        body, out_shape=tuple(SDS(g.shape, F32) for g in gstacks), grid=(nb,),
        in_specs=[pl.BlockSpec((N_CHIPS, tr, c), lambda j: (0, j, 0))] * k + [ANY] * k,
        out_specs=(pl.BlockSpec((None, tr, c), lambda j: (layer, lax.axis_index("c") * nb + j, 0)),) * k,
        input_output_aliases={k + n: n for n in range(k)}, compiler_params=_cparams(("parallel",)),
        name="sum_chips")(*lands, *gstacks))


def _sum_chips_all(lands, gstacks, layer):
    out = [None] * len(lands)
    for group in _same_shape_groups(gstacks):
        for i, g in zip(group, _sum_chips([lands[i] for i in group], [gstacks[i] for i in group], layer)):
            out[i] = g
    return out


def _pair_gather(gstacks, layer):
    nw = len(gstacks)

    def body(*refs):
        outs = refs[nw:2 * nw]
        send_sems, recv_sems = refs[2 * nw], refs[2 * nw + 1]
        x, y, c, _ = _place()

        def copy(ref, wi, half):
            r2 = ref.shape[1] // 2
            blk = ref.at[layer, pl.ds(half * r2, r2)]
            return pltpu.make_async_remote_copy(src_ref=blk, dst_ref=blk, send_sem=send_sems.at[wi],
                                                recv_sem=recv_sems.at[wi], device_id=(x, y, 1 - c), device_id_type=MESH)

        sent = [copy(ref, wi, c) for wi, ref in enumerate(outs)]
        for cp in sent:
            cp.start()
        for wi, ref in enumerate(outs):
            copy(ref, wi, 1 - c).wait_recv()
        for cp in sent:
            cp.wait_send()

    return pl.pallas_call(
        body, out_shape=tuple(SDS(g.shape, g.dtype) for g in gstacks), in_specs=[ANY] * nw, out_specs=(ANY,) * nw,
        input_output_aliases={i: i for i in range(nw)},
        scratch_shapes=[pltpu.SemaphoreType.DMA((nw,)), pltpu.SemaphoreType.DMA((nw,))], name="pair_gather")(*gstacks)


class _Rider:
    def __init__(self, tag, operands, out_shapes, aliases, sem_shapes, start, finish):
        self.tag, self.operands, self.out_shapes, self.aliases = tag, list(operands), list(out_shapes), dict(aliases)
        self.sem_shapes, self._start, self._finish = list(sem_shapes), start, finish
        self._refs = None

    def split(self, refs, n_outs, n_scratch, grid):
        n_in, n_out = len(self.operands), len(self.out_shapes)
        ins = refs[:n_in]
        own_outs = refs[n_in:n_in + n_outs]
        outs = refs[n_in + n_outs:n_in + n_outs + n_out]
        own_scratch = refs[n_in + n_outs + n_out:n_in + n_outs + n_out + n_scratch]
        sems = refs[n_in + n_outs + n_out + n_scratch:]
        self._refs = (ins, outs, sems)
        first = functools.reduce(lambda p, q: p & q, [pl.program_id(ax) == 0 for ax in range(len(grid))])

        @pl.when(first)
        def _():
            self._start(ins, outs, sems)

        return tuple(own_outs) + tuple(own_scratch)

    def finish_at_last_step(self, grid):
        ins, outs, sems = self._refs
        last = functools.reduce(lambda p, q: p & q, [pl.program_id(ax) == g - 1 for ax, g in enumerate(grid)])

        @pl.when(last)
        def _():
            self._finish(ins, outs, sems)


def _gather_rider(bufs, pieces):
    nw = len(bufs)
    return _Rider("_gather", bufs, [SDS(b.shape, b.dtype) for b in bufs], {i: i for i in range(nw)}, _gather_sems(pieces),
                  lambda ins, outs, sems: _gather_start(outs, pieces, *sems),
                  lambda ins, outs, sems: _gather_finish(outs, pieces, *sems))


def _pair_swap_copies(ins, outs, sems):
    x, y, c, _ = _place()
    copies = []
    for wi, (g_ref, o_ref) in enumerate(zip(ins, outs)):
        r2 = g_ref.shape[1] // 2
        copies.append(pltpu.make_async_remote_copy(
            src_ref=g_ref.at[:, pl.ds((1 - c) * r2, r2)], dst_ref=o_ref, send_sem=sems[0].at[wi],
            recv_sem=sems[1].at[wi], device_id=(x, y, 1 - c), device_id_type=MESH))
    return copies


def _pair_swap_rider(grads):
    nw = len(grads)

    def start(ins, outs, sems):
        for cp in _pair_swap_copies(ins, outs, sems):
            cp.start()

    def finish(ins, outs, sems):
        for cp in _pair_swap_copies(ins, outs, sems):
            cp.wait()

    return _Rider("_swap", grads, [SDS((N_CHIPS, g.shape[1] // 2, g.shape[2]), g.dtype) for g in grads], {},
                  [pltpu.SemaphoreType.DMA((nw,)), pltpu.SemaphoreType.DMA((nw,))], start, finish)


def _pair_gather_copy(ref, layer, sems, wi, half, peer):
    r2 = ref.shape[1] // 2
    blk = ref.at[layer, pl.ds(half * r2, r2)]
    return pltpu.make_async_remote_copy(src_ref=blk, dst_ref=blk, send_sem=sems[0].at[wi], recv_sem=sems[1].at[wi],
                                        device_id=peer, device_id_type=MESH)


def _pair_gather_rider(gstacks, layer):
    nw = len(gstacks)

    def start(ins, outs, sems):
        x, y, c, _ = _place()
        for wi, ref in enumerate(outs):
            _pair_gather_copy(ref, layer, sems, wi, c, (x, y, 1 - c)).start()

    def finish(ins, outs, sems):
        x, y, c, _ = _place()
        for wi, ref in enumerate(outs):
            _pair_gather_copy(ref, layer, sems, wi, 1 - c, (x, y, 1 - c)).wait_recv()
        for wi, ref in enumerate(outs):
            _pair_gather_copy(ref, layer, sems, wi, c, (x, y, 1 - c)).wait_send()

    return _Rider("_pair_gather", gstacks, [SDS(g.shape, g.dtype) for g in gstacks], {i: i for i in range(nw)},
                  [pltpu.SemaphoreType.DMA((nw,)), pltpu.SemaphoreType.DMA((nw,))], start, finish)


def _all_gather_small(vec, name):
    rows, w = vec.shape

    def body(v_ref, sum_ref, all_ref, send_sems, recv_sems):
        x, y, c, chips = _place()
        me, sibling = (x, y, c), (x, y, 1 - c)

        def slot(px, py, pc):
            return all_ref.at[4 * px + 2 * py + pc]

        def copy(k, block, to, src=None):
            return pltpu.make_async_remote_copy(
                src_ref=slot(*block) if src is None else src, dst_ref=slot(*block), send_sem=send_sems.at[k],
                recv_sem=recv_sems.at[k], device_id=to, device_id_type=MESH)

        first = [copy(0, me, sibling, src=v_ref)]
        first += [copy(1 + n, me, (*chip, c), src=v_ref) for n, chip in enumerate(chips)]
        for cp in first:
            cp.start()
        slot(*me)[...] = v_ref[...]
        passed = [copy(4 + n, (*chip, c), sibling) for n, chip in enumerate(chips)]
        for n, chip in enumerate(chips):
            copy(1 + n, (*chip, c), me).wait_recv()
            passed[n].start()
        copy(0, sibling, me).wait_recv()
        for n, chip in enumerate(chips):
            copy(4 + n, (*chip, 1 - c), me).wait_recv()
        for cp in first + passed:
            cp.wait_send()
        total = all_ref[0]
        for dev in range(1, 8):
            total = total + all_ref[dev]
        sum_ref[...] = total

    vm = pl.BlockSpec(memory_space=pltpu.VMEM)
    return pl.pallas_call(
        body, out_shape=(SDS((rows, w), F32), SDS((8, rows, w), F32)), in_specs=[vm], out_specs=(vm, vm),
        scratch_shapes=[pltpu.SemaphoreType.DMA((7,)), pltpu.SemaphoreType.DMA((7,))], name=name)(vec)


def _to_rows128(flat):
    n = flat.shape[0]
    rows = -(-n // (8 * LANES)) * 8
    return jnp.pad(flat, (0, rows * LANES - n)).reshape(rows, LANES)


def _in_layout(conv, ql, kvl, d):
    lay = {"conv": conv, "ql": ql, "kvl": kvl, "d": d}
    lay["q"] = 3 * conv
    lay["kr"] = lay["q"] + ql
    lay["gc"] = lay["kr"] + LANES
    lay["gm"] = lay["gc"] + d
    lay["kv"] = lay["gm"] + d
    used = lay["kv"] + kvl
    lay["width"] = -(-used // 512) * 512
    return lay


def _w_in_to_layout(w, lay):
    conv, ql, kvl, d = lay["conv"], lay["ql"], lay["kvl"], lay["d"]
    o_kv = 3 * conv + ql
    o_kr = o_kv + kvl
    o_g = o_kr + ROPE_DIM
    lead = w.shape[:-1]
    parts = [w[..., :o_kv], w[..., o_kr:o_g], jnp.zeros(lead + (LANES - ROPE_DIM,), w.dtype), w[..., o_g:o_g + 2 * d],
             w[..., o_kv:o_kr], jnp.zeros(lead + (lay["width"] - lay["kv"] - kvl,), w.dtype)]
    return jnp.concatenate(parts, axis=-1)


def _w_in_from_layout(g, lay):
    ql, kvl, d = lay["ql"], lay["kvl"], lay["d"]
    return jnp.concatenate([g[:, :lay["q"] + ql], g[:, lay["kv"]:lay["kv"] + kvl], g[:, lay["kr"]:lay["kr"] + ROPE_DIM],
                            g[:, lay["gc"]:lay["gc"] + 2 * d]], axis=1)


def _w_uq_to_layout(w):
    r = w.shape[0]
    w3 = w.reshape(r, N_HEADS, NOPE_DIM + ROPE_DIM)
    return jnp.pad(w3, ((0, 0), (0, 0), (0, QK_PAD - NOPE_DIM - ROPE_DIM))).reshape(r, N_HEADS * QK_PAD)


def _w_uq_from_layout(g):
    r = g.shape[0]
    return g.reshape(r, N_HEADS, QK_PAD)[:, :, :NOPE_DIM + ROPE_DIM].reshape(r, N_HEADS * (NOPE_DIM + ROPE_DIM))


def _w_ukv_to_layout(w):
    r = w.shape[0]
    return w.reshape(r, N_HEADS, 2, NOPE_DIM).transpose(0, 2, 1, 3).reshape(r, 2 * N_HEADS * NOPE_DIM)


def _w_ukv_from_layout(g):
    r = g.shape[0]
    return g.reshape(r, 2, N_HEADS, NOPE_DIM).transpose(0, 2, 1, 3).reshape(r, 2 * N_HEADS * NOPE_DIM)


def _chips_to_cols(buf):
    _, r, c = buf.shape
    return buf.transpose(1, 0, 2).reshape(r, N_CHIPS * c)


def _cols_to_chips(g):
    r, c4 = g.shape
    return g.reshape(r, N_CHIPS, c4 // N_CHIPS).transpose(1, 0, 2)


BEFORE_FFN1_GU = ("ffn1_w_gu",)
BEFORE_FFN1_DOWN = ("ffn1_w_down", "w_uq", "w_ukv")
IN_ATTENTION = ("w_in", "w_conv_out", "ffn2_w_gu", "ffn2_w_down", "w_mla_out", "w_o", "w_ple_gate", "w_ple_proj")
LAYER0_FIRST = ("ffn1_w_gu", "ffn1_w_down")
LAYER0_IN_FFN1_GU = ("w_in", "w_conv_out", "w_uq", "w_ukv")
LAYER0_IN_FFN1_DOWN = ("ffn2_w_down", "w_mla_out")
LAYER0_IN_PROJ = ("ffn2_w_gu",)
LAYER0_IN_ATTENTION = ("w_o", "w_ple_gate", "w_ple_proj")


def _gather_plan(layer, depth):
    plan = {}
    if layer == 0:
        plan["ffn1_gu"] = [(k, 0) for k in LAYER0_IN_FFN1_GU]
        plan["ffn1_down"] = [(k, 0) for k in LAYER0_IN_FFN1_DOWN]
        plan["in"] = [(k, 0) for k in LAYER0_IN_PROJ]
        plan["attn"] = [(k, 0) for k in LAYER0_IN_ATTENTION]
    if layer + 1 < depth:
        plan["attn"] = plan.get("attn", []) + [(k, layer + 1) for k in IN_ATTENTION]
        plan["ffn2_gu"] = [(k, layer + 1) for k in BEFORE_FFN1_GU]
        plan["ffn2_down"] = [(k, layer + 1) for k in BEFORE_FFN1_DOWN]
    return plan


def _plan_operands(bufs, todo):
    names = list(dict.fromkeys(name for name, _ in todo))
    return [bufs[k] for k in names], names, [(names.index(name), layer) for name, layer in todo]


def _rows_view(buf):
    return buf.reshape(buf.shape[0], N_CHIPS * buf.shape[2], buf.shape[3])


def _cols_view(buf, layer):
    return _chips_to_cols(buf[layer])


def _ffn_fwd(h, n, bufs, which, layer, next_gain, plan):
    tag = which
    rider = None
    if plan.get(which + "_gu"):
        ops, names, pieces = _plan_operands(bufs, plan[which + "_gu"])
        rider = _gather_rider(ops, pieces)
    outs = _gu_swiglu_fwd(n, bufs[which + "_w_gu"], layer, tag + "_gu_fwd", rider=rider)
    gu, a = outs[0], outs[1]
    if rider is not None:
        bufs = {**bufs, **dict(zip(names, outs[2]))}
    rider = None
    if plan.get(which + "_down"):
        ops, names, pieces = _plan_operands(bufs, plan[which + "_down"])
        rider = _gather_rider(ops, pieces)
    outs = _mm(a, _rows_view(bufs[which + "_w_down"]), "nn", F32, tag + "_down_fwd", scale=0.5, res=h, layer=layer,
               norm_gain=next_gain, rider=rider)
    if rider is not None:
        bufs = {**bufs, **dict(zip(names, outs[2]))}
    return outs[0], outs[1], (h, n, gu, a), bufs


def _ffn_bwd(dh, dhb, saved, gain, bufs, which, layer, swap=None):
    tag = which
    h, n, gu, a = saved
    d_wdown = _mm(a, dhb, "tn", BF16, tag + "_down_dw", scale=0.5)
    dgu = _down_dx_swiglu_bwd(dhb, _rows_view(bufs[which + "_w_down"]), gu, layer, tag + "_down_dx")
    rider = _pair_swap_rider(swap) if swap is not None else None
    outs = _mm(n, dgu, "tn", BF16, tag + "_gu_dw", out_chip=True, b_halves=True, rider=rider)
    d_wgu, swapped = (outs[0], outs[1]) if rider is not None else (outs, None)
    dh, dhb, dgain = _mm(dgu, bufs[which + "_w_gu"], "nt", BF16, tag + "_gu_dx", layer=layer, b_chip=True, a_halves=True,
                         norm_bwd=(h, gain, dh))
    return dh, dhb, d_wgu, d_wdown, dgain, swapped


def _layer_fwd(h0, n0, p_i, bufs, conv_w, norms, layer, lay, tables, depth, next_gain):
    plan = _gather_plan(layer, depth)
    h1, n2, s_ffn1, bufs = _ffn_fwd(h0, n0, bufs, "ffn1", layer, norms["mix_norm"], plan)
    w_in = _w_in_to_layout(_cols_view(bufs["w_in"], layer), lay)
    w_conv_out = _cols_view(bufs["w_conv_out"], layer)
    w_uq = _w_uq_to_layout(_cols_view(bufs["w_uq"], layer))
    w_ukv = _w_ukv_to_layout(_cols_view(bufs["w_ukv"], layer))
    if plan.get("in"):
        ops, names, pieces = _plan_operands(bufs, plan["in"])
        proj, gathered = _mm(n2, w_in, "nn", BF16, "in_fwd", rider=_gather_rider(ops, pieces))
        bufs = {**bufs, **dict(zip(names, gathered))}
    else:
        proj = _mm(n2, w_in, "nn", BF16, "in_fwd")
    cb = _conv_fwd(proj, conv_w)
    ya = _mm(cb, w_conv_out, "nn", BF16, "conv_out_fwd")
    qn, kvn = _qkvnorm_fwd(proj, lay, norms["q_norm"], norms["kv_norm"])
    qf = _mm(qn, w_uq, "nn", BF16, "uq_fwd")
    kv = _mm(kvn, w_ukv, "nn", BF16, "ukv_fwd")
    qr, kf = _rope_fwd(qf, kv, proj, lay, tables)
    if plan.get("attn"):
        ops, names, pieces = _plan_operands(bufs, plan["attn"])
        o, lse, gathered = _attn_fwd(qr, kf, kv, gather=(ops, pieces))
        bufs = {**bufs, **dict(zip(names, gathered))}
    else:
        o, lse, _ = _attn_fwd(qr, kf, kv)
    yb = _mm(o, _rows_view(bufs["w_mla_out"]), "nn", BF16, "mla_out_fwd", layer=layer)
    mg = _merge_fwd(proj, lay, ya, yb)
    h2, n3 = _mm(mg, _rows_view(bufs["w_o"]), "nn", F32, "o_fwd", res=h1, layer=layer, norm_gain=norms["ffn2_norm"])
    h3, n4, s_ffn2, bufs = _ffn_fwd(h2, n3, bufs, "ffn2", layer, norms["ple_norm"], plan)
    w_ple_proj = _cols_view(bufs["w_ple_proj"], layer)
    pp = _mm(p_i, w_ple_proj, "nn", BF16, "ple_proj_fwd")
    outs = _mm(n4, _rows_view(bufs["w_ple_gate"]), "nn", BF16, "ple_gate_fwd", layer=layer,
               epilogue=_ple_fwd_epilogue(h3, pp, next_gain))
    gp, h4, n_out = outs[0], outs[1], (outs[2] if next_gain is not None else None)
    saved = dict(s_ffn1=s_ffn1, h1=h1, n2=n2, proj=proj, cb=cb, ya=ya, qn=qn, kvn=kvn, qr=qr, kf=kf, kv=kv, o=o,
                 lse=lse, yb=yb, mg=mg, h2=h2, s_ffn2=s_ffn2, h3=h3, n4=n4, gp=gp, pp=pp, p=p_i,
                 w_in=w_in, w_conv_out=w_conv_out, w_uq=w_uq, w_ukv=w_ukv)
    return h4, n_out, saved, bufs


def _layer_bwd(dh, s, bufs, conv_w, norms, layer, lay, tables, above, gstacks):
    gw, gn = {}, {}

    def by_rows(g):
        return g.reshape(N_CHIPS, g.shape[0] // N_CHIPS, g.shape[1])

    dpp, dgp = _ple_bwd(dh, s["gp"], s["pp"])
    gw["w_ple_proj"] = _cols_to_chips(_mm(s["p"], dpp, "tn", BF16, "ple_proj_dw"))
    gw["w_ple_gate"] = by_rows(_mm(s["n4"], dgp, "tn", BF16, "ple_gate_dw"))
    dh, dhb, gn["ple_norm"] = _mm(dgp, _rows_view(bufs["w_ple_gate"]), "nt", BF16, "ple_gate_dx", layer=layer,
                                  norm_bwd=(s["h3"], norms["ple_norm"], dh))
    dh, dhb, gw["ffn2_w_gu"], g_down, gn["ffn2_norm"], swapped = _ffn_bwd(
        dh, dhb, s["s_ffn2"], norms["ffn2_norm"], bufs, "ffn2", layer, swap=None if above is None else above[1])
    gw["ffn2_w_down"] = by_rows(g_down)
    exchange = None
    if above is not None:
        pairs = _pair_add_all(above[1], swapped)
        exchange = ([pb for pb, _ in pairs], [land for _, land in pairs])
    gw["w_o"] = by_rows(_mm(s["mg"], dhb, "tn", BF16, "o_dw"))
    dya, dyb, dgc, dgm = _mm(dhb, _rows_view(bufs["w_o"]), "nt", BF16, "o_dx", layer=layer,
                             epilogue=_merge_bwd_epilogue(s["proj"], lay, s["ya"], s["yb"]))
    gw["w_conv_out"] = _cols_to_chips(_mm(s["cb"], dya, "tn", BF16, "conv_out_dw"))
    dcb = _mm(dya, s["w_conv_out"], "nt", BF16, "conv_out_dx")
    db, dc, dv_conv, g_conv = _conv_bwd(s["proj"], conv_w, dcb)
    gw["w_mla_out"] = by_rows(_mm(s["o"], dyb, "tn", BF16, "mla_out_dw"))
    do, delta = _mm(dyb, _rows_view(bufs["w_mla_out"]), "nt", BF16, "mla_out_dx", layer=layer,
                    epilogue=_delta_epilogue(s["o"]))
    dkf, dv, dqr, landed = _attn_bwd(s["qr"], s["kf"], s["kv"], do, s["lse"], delta, exchange=exchange)
    dqf, dkv, dkr = _rope_bwd(dqr, dkf, dv, tables)
    gw["w_uq"] = _cols_to_chips(_w_uq_from_layout(_mm(s["qn"], dqf, "tn", BF16, "uq_dw")))
    dqn = _mm(dqf, s["w_uq"], "nt", BF16, "uq_dx")
    gw["w_ukv"] = _cols_to_chips(_w_ukv_from_layout(_mm(s["kvn"], dkv, "tn", BF16, "ukv_dw")))
    dkvn = _mm(dkv, s["w_ukv"], "nt", BF16, "ukv_dx")
    dqc, dkvc, gn["q_norm"], gn["kv_norm"] = _qkvnorm_bwd(s["proj"], lay, norms["q_norm"], norms["kv_norm"], dqn, dkvn)
    t = dh.shape[0]
    dproj = jnp.concatenate([db, dc, dv_conv, dqc, dkr, dgc, dgm, dkvc,
                             jnp.zeros((t, lay["width"] - lay["kv"] - lay["kvl"]), BF16)], axis=1)
    rider = None
    if above is not None:
        gstacks = _sum_chips_all(landed, gstacks, above[0])
        rider = _pair_gather_rider(gstacks, above[0])
    outs = _mm(s["n2"], dproj, "tn", BF16, "in_dw", rider=rider)
    g_in, gstacks = (outs[0], outs[1]) if rider is not None else (outs, gstacks)
    gw["w_in"] = _cols_to_chips(_w_in_from_layout(g_in, lay))
    dh, dhb, gn["mix_norm"] = _mm(dproj, s["w_in"], "nt", BF16, "in_dx", norm_bwd=(s["h1"], norms["mix_norm"], dh))
    dh, dhb, gw["ffn1_w_gu"], g_down, gn["ffn1_norm"], _ = _ffn_bwd(
        dh, dhb, s["s_ffn1"], norms["ffn1_norm"], bufs, "ffn1", layer)
    gw["ffn1_w_down"] = by_rows(g_down)
    return dh, gw, g_conv, gn, gstacks


def _rope_tables(positions):
    half = ROPE_DIM // 2
    inv_freq = ROPE_THETA ** (-jnp.arange(0, ROPE_DIM, 2, dtype=F32) / ROPE_DIM)
    ang = positions.astype(F32)[:, None] * inv_freq
    cos, sin = jnp.cos(ang), jnp.sin(ang)
    zeros = jnp.zeros_like(cos)
    cos_t = jnp.concatenate([cos, cos, zeros, zeros], axis=1)
    sin_a = jnp.concatenate([-sin, zeros, zeros, zeros], axis=1)
    sin_b = jnp.concatenate([zeros, sin, zeros, zeros], axis=1)
    assert cos_t.shape[1] == LANES and half * 4 == LANES
    return cos_t, sin_a, sin_b


def kernel(x, p, positions, ffn1_norm, ffn1_w_gu, ffn1_w_down, mix_norm, w_in, conv_w, w_conv_out, q_norm, kv_norm, w_uq, w_ukv, w_mla_out, w_o, ffn2_norm, ffn2_w_gu, ffn2_w_down, ple_norm, w_ple_gate, w_ple_proj, final_norm, loss_target, m_ffn1_norm, m_ffn1_w_gu, m_ffn1_w_down, m_mix_norm, m_w_in, m_conv_w, m_w_conv_out, m_q_norm, m_kv_norm, m_w_uq, m_w_ukv, m_w_mla_out, m_w_o, m_ffn2_norm, m_ffn2_w_gu, m_ffn2_w_down, m_ple_norm, m_w_ple_gate, m_w_ple_proj, m_final_norm, v_ffn1_norm, v_ffn1_w_gu, v_ffn1_w_down, v_mix_norm, v_w_in, v_conv_w, v_w_conv_out, v_q_norm, v_kv_norm, v_w_uq, v_w_ukv, v_w_mla_out, v_w_o, v_ffn2_norm, v_ffn2_w_gu, v_ffn2_w_down, v_ple_norm, v_w_ple_gate, v_w_ple_proj, v_final_norm):
    args = dict(zip(ARG_NAMES, (x, p, positions, ffn1_norm, ffn1_w_gu, ffn1_w_down, mix_norm, w_in, conv_w, w_conv_out, q_norm, kv_norm, w_uq, w_ukv, w_mla_out, w_o, ffn2_norm, ffn2_w_gu, ffn2_w_down, ple_norm, w_ple_gate, w_ple_proj, final_norm, loss_target, m_ffn1_norm, m_ffn1_w_gu, m_ffn1_w_down, m_mix_norm, m_w_in, m_conv_w, m_w_conv_out, m_q_norm, m_kv_norm, m_w_uq, m_w_ukv, m_w_mla_out, m_w_o, m_ffn2_norm, m_ffn2_w_gu, m_ffn2_w_down, m_ple_norm, m_w_ple_gate, m_w_ple_proj, m_final_norm, v_ffn1_norm, v_ffn1_w_gu, v_ffn1_w_down, v_mix_norm, v_w_in, v_conv_w, v_w_conv_out, v_q_norm, v_kv_norm, v_w_uq, v_w_ukv, v_w_mla_out, v_w_o, v_ffn2_norm, v_ffn2_w_gu, v_ffn2_w_down, v_ple_norm, v_w_ple_gate, v_w_ple_proj, v_final_norm)))
    depth = ffn1_norm.shape[0]
    t, d = x.shape[1], x.shape[2]
    conv = conv_w.shape[-1] * N_CHIPS
    lay = _in_layout(conv, q_norm.shape[-1], kv_norm.shape[-1], d)
    chip = 2 * lax.axis_index("x") + lax.axis_index("y")
    tables = _rope_tables(positions[0])

    bufs = {}
    for group in _same_shape_groups([args[name] for name in BIG]):
        bufs.update(zip([BIG[i] for i in group], _cast_into_slot([args[BIG[i]] for i in group])))
    bufs.update(zip(LAYER0_FIRST, _all_gather_weights([bufs[k] for k in LAYER0_FIRST], [(i, 0) for i in range(len(LAYER0_FIRST))])))
    conv_rows = depth * conv_w.shape[1]
    conv_all = _all_gather_small(_to_rows128(conv_w.reshape(-1)), "all_gather_conv_w")[1]
    conv_full = conv_all[0::2, :conv_rows].reshape(N_CHIPS, depth, conv_w.shape[1], LANES)
    conv_full = conv_full.transpose(1, 2, 0, 3).reshape(depth, conv_w.shape[1], conv)
    norms = [{name: args[name][i] for name in REPLICATED} for i in range(depth)]
    p3 = p.reshape(depth, t, p.shape[-1])

    h = x[0]
    n = _rmsnorm_fwd(h, norms[0]["ffn1_norm"], "first_norm_fwd")
    saved = []
    for i in range(depth):
        h, n, s, bufs = _layer_fwd(h, n, p3[i], bufs, conv_full[i], norms[i], i, lay, tables, depth,
                                   norms[i + 1]["ffn1_norm"] if i + 1 < depth else None)
        saved.append(s)
    loss_part, dh, _, g_final = _loss_head(h, final_norm, loss_target[0])
    loss = lax.psum(loss_part[0, 0], ("x", "y", "c"))

    gstacks = [lax.empty(args[name].shape, F32) for name in BIG]
    norm_grads, conv_grads = [None] * depth, [None] * depth
    above = None
    for i in reversed(range(depth)):
        dh, gw, conv_grads[i], norm_grads[i], gstacks = _layer_bwd(
            dh, saved[i], bufs, conv_full[i], norms[i], i, lay, tables, above, gstacks)
        above = (i, [gw[name] for name in BIG])
    pairs = _pair_add_all(above[1], _pair_swap(above[1]))
    landed = _chip_all_to_all([pb for pb, _ in pairs], [land for _, land in pairs])
    gstacks = _pair_gather(_sum_chips_all(landed, gstacks, above[0]), above[0])
    grad_x = dh[None]
    grads = dict(zip(BIG, gstacks))

    pieces = [norm_grads[i][name].reshape(-1) for i in range(depth) for name in REPLICATED]
    pieces += [g_final.reshape(-1)] + [conv_grads[i].reshape(-1) for i in range(depth)]
    vec = _all_gather_small(_to_rows128(jnp.concatenate(pieces)), "all_sum_small")[0].reshape(-1)
    off = 0
    per_name = {name: [] for name in REPLICATED}
    for i in range(depth):
        for name in REPLICATED:
            size = args[name].shape[1]
            per_name[name].append(vec[off:off + size])
            off += size
    for name in REPLICATED:
        grads[name] = jnp.stack(per_name[name])
    grads["final_norm"] = vec[off:off + d]
    off += d
    conv_g = vec[off:off + depth * 3 * conv].reshape(depth, 3, conv)
    grads["conv_w"] = lax.dynamic_slice_in_dim(conv_g, chip * conv_w.shape[-1], conv_w.shape[-1], axis=2)

    delta, new_m, new_v = {}, {}, {}
    for name in WEIGHTS:
        w_, g_, m_, v_ = args[name], grads[name], args["m_" + name], args["v_" + name]
        if w_.ndim == 1:
            outs = _adamw(w_[None], g_[None], m_[None], v_[None])
            delta[name], new_m[name], new_v[name] = (o[0] for o in outs)
        elif name in BIG:
            delta[name], new_m[name], new_v[name], grads[name] = _adamw(w_, g_, m_, v_, emit_grad=True)
        else:
            delta[name], new_m[name], new_v[name] = _adamw(w_, g_, m_, v_)
    return (loss, grad_x, *[grads[n] for n in WEIGHTS], *[delta[n] for n in WEIGHTS],
            *[new_m[n] for n in WEIGHTS], *[new_v[n] for n in WEIGHTS])
```
